```python
import jax, jax.numpy as jnp
from jax import lax
import numpy as np

D_MODEL = 1024
BATCH = 8
SEQ = 4096
DEPTH = 2

CHUNK = 64
N_MIXERS = 2
N_MLSTM = (DEPTH + 1) // 2
N_RGLRU = DEPTH // 2

ML_HEADS = 8
ML_QK_DIM = D_MODEL // 2 // ML_HEADS
ML_V_DIM = D_MODEL // ML_HEADS
ML_QK = ML_HEADS * ML_QK_DIM
ML_V = ML_HEADS * ML_V_DIM
ML_IN = 2 * ML_QK + 2 * ML_V + 2 * ML_HEADS
GATE_CAP = 15.0
ML_M_INIT = -1e30

LRU_WIDTH = D_MODEL
LRU_BLOCKS = 4
LRU_BLOCK = LRU_WIDTH // LRU_BLOCKS
CONV_WIDTH = 4
LRU_C = 8.0

_FF_RAW = -(-8 * D_MODEL // 3)
D_FF = -(-_FF_RAW // 256) * 256

EPS = 1e-6

kernel_name = "hybrid_mlstm_rglru_trunk"


def rms_norm(x, g):
    xf = x.astype(jnp.float32)
    y = xf * lax.rsqrt(jnp.mean(xf * xf, axis=-1, keepdims=True) + EPS)
    return (y * g.astype(jnp.float32)).astype(x.dtype)


def soft_cap(z):
    return GATE_CAP * jnp.tanh(z / GATE_CAP)


def mlstm_mixer(x, w_in, b_if, head_norm, w_out):
    B, S, _ = x.shape
    H, dk, dv, L = ML_HEADS, ML_QK_DIM, ML_V_DIM, CHUNK
    NC = S // L
    proj = x @ w_in
    q, k, v, o, g = jnp.split(proj, [ML_QK, 2 * ML_QK, 2 * ML_QK + ML_V, 2 * ML_QK + 2 * ML_V], axis=-1)
    g = soft_cap(g.astype(jnp.float32) + b_if.astype(jnp.float32))
    log_i = g[..., :H]
    log_f = jax.nn.log_sigmoid(g[..., H:])

    def to_chunks(t, d):
        return t.astype(jnp.float32).reshape(B, NC, L, H, d).transpose(1, 0, 3, 2, 4)

    qc = to_chunks(q, dk) * (dk ** -0.5)
    kc = to_chunks(k, dk)
    vc = to_chunks(v, dv)
    li = log_i.reshape(B, NC, L, H).transpose(1, 0, 3, 2)
    lf = log_f.reshape(B, NC, L, H).transpose(1, 0, 3, 2)
    causal = jnp.tril(jnp.ones((L, L), dtype=bool))

    def step(carry, inp):
        C, n, m = carry
        qb, kb, vb, lib, lfb = inp
        b = jnp.cumsum(lfb, axis=-1)
        D = b[..., :, None] - b[..., None, :] + lib[..., None, :]
        D = jnp.where(causal, D, -jnp.inf)
        inter = b + m[..., None]
        m_t = jnp.maximum(jnp.max(D, axis=-1), inter)
        P = jnp.exp(D - m_t[..., None]) * jnp.einsum('bhld,bhsd->bhls', qb, kb)
        w_inter = jnp.exp(inter - m_t)
        num = jnp.einsum('bhls,bhsv->bhlv', P, vb) + w_inter[..., None] * jnp.einsum('bhld,bhdv->bhlv', qb, C)
        den = jnp.sum(P, axis=-1) + w_inter * jnp.einsum('bhld,bhd->bhl', qb, n)
        den = jnp.maximum(jnp.abs(den), jnp.exp(-m_t))
        h = num / den[..., None]
        g_tot = b[..., -1]
        a = g_tot[..., None] - b + lib
        m_new = jnp.maximum(g_tot + m, jnp.max(a, axis=-1))
        decay = jnp.exp(g_tot + m - m_new)
        wk = jnp.exp(a - m_new[..., None])[..., None] * kb
        C_new = decay[..., None, None] * C + jnp.einsum('bhsd,bhsv->bhdv', wk, vb)
        n_new = decay[..., None] * n + jnp.sum(wk, axis=2)
        return (C_new, n_new, m_new), h

    carry0 = (jnp.zeros((B, H, dk, dv), jnp.float32),
              jnp.zeros((B, H, dk), jnp.float32),
              jnp.full((B, H), ML_M_INIT, jnp.float32))
    _, hs = lax.scan(step, carry0, (qc, kc, vc, li, lf))
    hs = hs.transpose(1, 0, 3, 2, 4).reshape(B, S, H, dv)
    hs = hs * lax.rsqrt(jnp.mean(hs * hs, axis=-1, keepdims=True) + EPS)
    hs = hs * head_norm.astype(jnp.float32).reshape(H, dv)
    y = hs.reshape(B, S, ML_V) * jax.nn.sigmoid(o.astype(jnp.float32))
    return y.astype(x.dtype) @ w_out


def causal_depthwise_conv(u, w, bias):
    C = u.shape[-1]
    out = lax.conv_general_dilated(
        u, w[:, None, :].astype(u.dtype), window_strides=(1,), padding=[(CONV_WIDTH - 1, 0)],
        dimension_numbers=('NWC', 'WIO', 'NWC'), feature_group_count=C)
    return out + bias.astype(u.dtype)


def rglru_mixer(x, w_in, conv_w, conv_b, w_gate_a, b_gate_a, w_gate_x, b_gate_x, lam, w_out):
    B, S, _ = x.shape
    proj = x @ w_in
    gate_branch, u = proj[..., :LRU_WIDTH], proj[..., LRU_WIDTH:]
    u = causal_depthwise_conv(u, conv_w, conv_b)
    ub = u.reshape(B, S, LRU_BLOCKS, LRU_BLOCK)
    r = jax.nn.sigmoid((jnp.einsum('bsnc,ncd->bsnd', ub, w_gate_a).reshape(B, S, LRU_WIDTH)
                        + b_gate_a).astype(jnp.float32))
    i = jax.nn.sigmoid((jnp.einsum('bsnc,ncd->bsnd', ub, w_gate_x).reshape(B, S, LRU_WIDTH)
                        + b_gate_x).astype(jnp.float32))
    log_a = LRU_C * r * jax.nn.log_sigmoid(lam.astype(jnp.float32))
    a = jnp.exp(log_a)
    mult = jnp.sqrt(-jnp.expm1(2.0 * log_a))
    b = mult * i * u.astype(jnp.float32)

    def combine(lhs, rhs):
        a1, b1 = lhs
        a2, b2 = rhs
        return a1 * a2, a2 * b1 + b2

    _, h = lax.associative_scan(combine, (a, b), axis=1)
    y = h * jax.nn.gelu(gate_branch.astype(jnp.float32))
    return y.astype(x.dtype) @ w_out


def swiglu(x, w_gate, w_up, w_down):
    return (jax.nn.silu(x @ w_gate) * (x @ w_up)) @ w_down


def _fwd_setup_inputs(seed: int = 0) -> dict:
    key = jax.random.key(seed)
    ks = jax.random.split(key, 24)
    f32 = jnp.float32

    def nrm(k, shape, fan_in):
        return jax.random.normal(k, shape, f32) * (fan_in ** -0.5)

    def gain(k, shape):
        return 1.0 + 0.05 * jax.random.normal(k, shape, f32)

    x = jax.random.normal(ks[0], (BATCH, SEQ, D_MODEL), f32)
    ml_w_in = nrm(ks[1], (N_MLSTM, D_MODEL, ML_IN), D_MODEL)
    b_i = 0.1 * jax.random.normal(ks[2], (N_MLSTM, ML_HEADS), f32)
    b_f = 3.0 + 3.0 * jax.random.uniform(ks[3], (N_MLSTM, ML_HEADS), f32)
    ml_b_if = jnp.concatenate([b_i, b_f], axis=-1)
    ml_head_norm = gain(ks[4], (N_MLSTM, ML_V))
    ml_w_out = nrm(ks[5], (N_MLSTM, ML_V, D_MODEL), ML_V)
    lru_w_in = nrm(ks[6], (N_RGLRU, D_MODEL, 2 * LRU_WIDTH), D_MODEL)
    lru_conv_w = nrm(ks[7], (N_RGLRU, CONV_WIDTH, LRU_WIDTH), CONV_WIDTH)
    lru_conv_b = 0.02 * jax.random.normal(ks[8], (N_RGLRU, LRU_WIDTH), f32)
    lru_w_gate_a = nrm(ks[9], (N_RGLRU, LRU_BLOCKS, LRU_BLOCK, LRU_BLOCK), LRU_BLOCK)
    lru_b_gate_a = 0.02 * jax.random.normal(ks[10], (N_RGLRU, LRU_WIDTH), f32)
    lru_w_gate_x = nrm(ks[11], (N_RGLRU, LRU_BLOCKS, LRU_BLOCK, LRU_BLOCK), LRU_BLOCK)
    lru_b_gate_x = 0.02 * jax.random.normal(ks[12], (N_RGLRU, LRU_WIDTH), f32)
    u = jax.random.uniform(ks[13], (N_RGLRU, LRU_WIDTH), f32, minval=0.9, maxval=0.999)
    lru_lambda = jnp.log(u) - jnp.log1p(-u)
    lru_w_out = nrm(ks[14], (N_RGLRU, LRU_WIDTH, D_MODEL), LRU_WIDTH)
    norm_pre_mix = gain(ks[15], (DEPTH, D_MODEL))
    norm_post_mix = gain(ks[16], (DEPTH, D_MODEL))
    norm_pre_ffn = gain(ks[17], (DEPTH, D_MODEL))
    norm_post_ffn = gain(ks[18], (DEPTH, D_MODEL))
    ffn_w_gate = nrm(ks[19], (DEPTH, D_MODEL, D_FF), D_MODEL)
    ffn_w_up = nrm(ks[20], (DEPTH, D_MODEL, D_FF), D_MODEL)
    ffn_w_down = nrm(ks[21], (DEPTH, D_FF, D_MODEL), D_FF)
    return {
        "x": x,
        "ml_w_in": ml_w_in, "ml_b_if": ml_b_if, "ml_head_norm": ml_head_norm, "ml_w_out": ml_w_out,
        "lru_w_in": lru_w_in, "lru_conv_w": lru_conv_w, "lru_conv_b": lru_conv_b,
        "lru_w_gate_a": lru_w_gate_a, "lru_b_gate_a": lru_b_gate_a,
        "lru_w_gate_x": lru_w_gate_x, "lru_b_gate_x": lru_b_gate_x,
        "lru_lambda": lru_lambda, "lru_w_out": lru_w_out,
        "norm_pre_mix": norm_pre_mix, "norm_post_mix": norm_post_mix,
        "norm_pre_ffn": norm_pre_ffn, "norm_post_ffn": norm_post_ffn,
        "ffn_w_gate": ffn_w_gate, "ffn_w_up": ffn_w_up, "ffn_w_down": ffn_w_down,
    }


def _fwd_reference(x, ml_w_in, ml_b_if, ml_head_norm, ml_w_out,
              lru_w_in, lru_conv_w, lru_conv_b, lru_w_gate_a, lru_b_gate_a,
              lru_w_gate_x, lru_b_gate_x, lru_lambda, lru_w_out,
              norm_pre_mix, norm_post_mix, norm_pre_ffn, norm_post_ffn,
              ffn_w_gate, ffn_w_up, ffn_w_down):
    h = x
    for layer in range(DEPTH):
        j = layer // N_MIXERS
        y = rms_norm(h, norm_pre_mix[layer])
        if layer % N_MIXERS == 0:
            y = mlstm_mixer(y, ml_w_in[j], ml_b_if[j], ml_head_norm[j], ml_w_out[j])
        else:
            y = rglru_mixer(y, lru_w_in[j], lru_conv_w[j], lru_conv_b[j],
                            lru_w_gate_a[j], lru_b_gate_a[j], lru_w_gate_x[j], lru_b_gate_x[j],
                            lru_lambda[j], lru_w_out[j])
        h = h + rms_norm(y, norm_post_mix[layer])
        y = swiglu(rms_norm(h, norm_pre_ffn[layer]), ffn_w_gate[layer], ffn_w_up[layer], ffn_w_down[layer])
        h = h + rms_norm(y, norm_post_ffn[layer])
    return h


import jax as _jax
import jax.numpy as _jnp

TWIN_FORMAT = 'train_step'
FWD_PARAMS = ['x', 'ml_w_in', 'ml_b_if', 'ml_head_norm', 'ml_w_out', 'lru_w_in', 'lru_conv_w', 'lru_conv_b', 'lru_w_gate_a', 'lru_b_gate_a', 'lru_w_gate_x', 'lru_b_gate_x', 'lru_lambda', 'lru_w_out', 'norm_pre_mix', 'norm_post_mix', 'norm_pre_ffn', 'norm_post_ffn', 'ffn_w_gate', 'ffn_w_up', 'ffn_w_down']
TWIN_WEIGHTS = ['ml_w_in', 'ml_b_if', 'ml_head_norm', 'ml_w_out', 'lru_w_in', 'lru_conv_w', 'lru_conv_b', 'lru_w_gate_a', 'lru_b_gate_a', 'lru_w_gate_x', 'lru_b_gate_x', 'lru_lambda', 'lru_w_out', 'norm_pre_mix', 'norm_post_mix', 'norm_pre_ffn', 'norm_post_ffn', 'ffn_w_gate', 'ffn_w_up', 'ffn_w_down']
TWIN_DIFF_INPUT = 'x'
TWIN_INPUTS = ['x', 'ml_w_in', 'ml_b_if', 'ml_head_norm', 'ml_w_out', 'lru_w_in', 'lru_conv_w', 'lru_conv_b', 'lru_w_gate_a', 'lru_b_gate_a', 'lru_w_gate_x', 'lru_b_gate_x', 'lru_lambda', 'lru_w_out', 'norm_pre_mix', 'norm_post_mix', 'norm_pre_ffn', 'norm_post_ffn', 'ffn_w_gate', 'ffn_w_up', 'ffn_w_down', 'loss_target', 'm_ml_w_in', 'm_ml_b_if', 'm_ml_head_norm', 'm_ml_w_out', 'm_lru_w_in', 'm_lru_conv_w', 'm_lru_conv_b', 'm_lru_w_gate_a', 'm_lru_b_gate_a', 'm_lru_w_gate_x', 'm_lru_b_gate_x', 'm_lru_lambda', 'm_lru_w_out', 'm_norm_pre_mix', 'm_norm_post_mix', 'm_norm_pre_ffn', 'm_norm_post_ffn', 'm_ffn_w_gate', 'm_ffn_w_up', 'm_ffn_w_down', 'v_ml_w_in', 'v_ml_b_if', 'v_ml_head_norm', 'v_ml_w_out', 'v_lru_w_in', 'v_lru_conv_w', 'v_lru_conv_b', 'v_lru_w_gate_a', 'v_lru_b_gate_a', 'v_lru_w_gate_x', 'v_lru_b_gate_x', 'v_lru_lambda', 'v_lru_w_out', 'v_norm_pre_mix', 'v_norm_post_mix', 'v_norm_pre_ffn', 'v_norm_post_ffn', 'v_ffn_w_gate', 'v_ffn_w_up', 'v_ffn_w_down']
TWIN_OUTPUTS = ['loss', 'grad_x', 'grad_ml_w_in', 'grad_ml_b_if', 'grad_ml_head_norm', 'grad_ml_w_out', 'grad_lru_w_in', 'grad_lru_conv_w', 'grad_lru_conv_b', 'grad_lru_w_gate_a', 'grad_lru_b_gate_a', 'grad_lru_w_gate_x', 'grad_lru_b_gate_x', 'grad_lru_lambda', 'grad_lru_w_out', 'grad_norm_pre_mix', 'grad_norm_post_mix', 'grad_norm_pre_ffn', 'grad_norm_post_ffn', 'grad_ffn_w_gate', 'grad_ffn_w_up', 'grad_ffn_w_down', 'delta_ml_w_in', 'delta_ml_b_if', 'delta_ml_head_norm', 'delta_ml_w_out', 'delta_lru_w_in', 'delta_lru_conv_w', 'delta_lru_conv_b', 'delta_lru_w_gate_a', 'delta_lru_b_gate_a', 'delta_lru_w_gate_x', 'delta_lru_b_gate_x', 'delta_lru_lambda', 'delta_lru_w_out', 'delta_norm_pre_mix', 'delta_norm_post_mix', 'delta_norm_pre_ffn', 'delta_norm_post_ffn', 'delta_ffn_w_gate', 'delta_ffn_w_up', 'delta_ffn_w_down', 'new_m_ml_w_in', 'new_m_ml_b_if', 'new_m_ml_head_norm', 'new_m_ml_w_out', 'new_m_lru_w_in', 'new_m_lru_conv_w', 'new_m_lru_conv_b', 'new_m_lru_w_gate_a', 'new_m_lru_b_gate_a', 'new_m_lru_w_gate_x', 'new_m_lru_b_gate_x', 'new_m_lru_lambda', 'new_m_lru_w_out', 'new_m_norm_pre_mix', 'new_m_norm_post_mix', 'new_m_norm_pre_ffn', 'new_m_norm_post_ffn', 'new_m_ffn_w_gate', 'new_m_ffn_w_up', 'new_m_ffn_w_down', 'new_v_ml_w_in', 'new_v_ml_b_if', 'new_v_ml_head_norm', 'new_v_ml_w_out', 'new_v_lru_w_in', 'new_v_lru_conv_w', 'new_v_lru_conv_b', 'new_v_lru_w_gate_a', 'new_v_lru_b_gate_a', 'new_v_lru_w_gate_x', 'new_v_lru_b_gate_x', 'new_v_lru_lambda', 'new_v_lru_w_out', 'new_v_norm_pre_mix', 'new_v_norm_post_mix', 'new_v_norm_pre_ffn', 'new_v_norm_post_ffn', 'new_v_ffn_w_gate', 'new_v_ffn_w_up', 'new_v_ffn_w_down']
TWIN_LEAF_KINDS = {'loss': 'loss', 'grad_x': 'grad_x', 'grad_ml_w_in': 'grad_w', 'grad_ml_b_if': 'grad_w', 'grad_ml_head_norm': 'grad_w', 'grad_ml_w_out': 'grad_w', 'grad_lru_w_in': 'grad_w', 'grad_lru_conv_w': 'grad_w', 'grad_lru_conv_b': 'grad_w', 'grad_lru_w_gate_a': 'grad_w', 'grad_lru_b_gate_a': 'grad_w', 'grad_lru_w_gate_x': 'grad_w', 'grad_lru_b_gate_x': 'grad_w', 'grad_lru_lambda': 'grad_w', 'grad_lru_w_out': 'grad_w', 'grad_norm_pre_mix': 'grad_w', 'grad_norm_post_mix': 'grad_w', 'grad_norm_pre_ffn': 'grad_w', 'grad_norm_post_ffn': 'grad_w', 'grad_ffn_w_gate': 'grad_w', 'grad_ffn_w_up': 'grad_w', 'grad_ffn_w_down': 'grad_w', 'delta_ml_w_in': 'delta_w', 'delta_ml_b_if': 'delta_w', 'delta_ml_head_norm': 'delta_w', 'delta_ml_w_out': 'delta_w', 'delta_lru_w_in': 'delta_w', 'delta_lru_conv_w': 'delta_w', 'delta_lru_conv_b': 'delta_w', 'delta_lru_w_gate_a': 'delta_w', 'delta_lru_b_gate_a': 'delta_w', 'delta_lru_w_gate_x': 'delta_w', 'delta_lru_b_gate_x': 'delta_w', 'delta_lru_lambda': 'delta_w', 'delta_lru_w_out': 'delta_w', 'delta_norm_pre_mix': 'delta_w', 'delta_norm_post_mix': 'delta_w', 'delta_norm_pre_ffn': 'delta_w', 'delta_norm_post_ffn': 'delta_w', 'delta_ffn_w_gate': 'delta_w', 'delta_ffn_w_up': 'delta_w', 'delta_ffn_w_down': 'delta_w', 'new_m_ml_w_in': 'new_m', 'new_m_ml_b_if': 'new_m', 'new_m_ml_head_norm': 'new_m', 'new_m_ml_w_out': 'new_m', 'new_m_lru_w_in': 'new_m', 'new_m_lru_conv_w': 'new_m', 'new_m_lru_conv_b': 'new_m', 'new_m_lru_w_gate_a': 'new_m', 'new_m_lru_b_gate_a': 'new_m', 'new_m_lru_w_gate_x': 'new_m', 'new_m_lru_b_gate_x': 'new_m', 'new_m_lru_lambda': 'new_m', 'new_m_lru_w_out': 'new_m', 'new_m_norm_pre_mix': 'new_m', 'new_m_norm_post_mix': 'new_m', 'new_m_norm_pre_ffn': 'new_m', 'new_m_norm_post_ffn': 'new_m', 'new_m_ffn_w_gate': 'new_m', 'new_m_ffn_w_up': 'new_m', 'new_m_ffn_w_down': 'new_m', 'new_v_ml_w_in': 'new_v', 'new_v_ml_b_if': 'new_v', 'new_v_ml_head_norm': 'new_v', 'new_v_ml_w_out': 'new_v', 'new_v_lru_w_in': 'new_v', 'new_v_lru_conv_w': 'new_v', 'new_v_lru_conv_b': 'new_v', 'new_v_lru_w_gate_a': 'new_v', 'new_v_lru_b_gate_a': 'new_v', 'new_v_lru_w_gate_x': 'new_v', 'new_v_lru_b_gate_x': 'new_v', 'new_v_lru_lambda': 'new_v', 'new_v_lru_w_out': 'new_v', 'new_v_norm_pre_mix': 'new_v', 'new_v_norm_post_mix': 'new_v', 'new_v_norm_pre_ffn': 'new_v', 'new_v_norm_post_ffn': 'new_v', 'new_v_ffn_w_gate': 'new_v', 'new_v_ffn_w_up': 'new_v', 'new_v_ffn_w_down': 'new_v'}


def _forward(args):
    return _fwd_reference(*[args[k] for k in FWD_PARAMS])


def _output_shape():
    out = _jax.eval_shape(lambda: _forward(_fwd_setup_inputs(0)))
    return out.shape, out.dtype

N_MICROBATCH = 1
ADAM_LR = 0.001
ADAM_B1 = 0.9
ADAM_B2 = 0.999
ADAM_EPS = 1e-08
ADAM_WD = 0.01
ADAM_STEP = 10
PER_EXAMPLE_BATCH_AXIS = {'x': 0, 'loss_target': 0}
SHARED_INPUTS = []
_WEIGHT_DTYPES = {'ml_w_in': _jnp.float32, 'ml_b_if': _jnp.float32, 'ml_head_norm': _jnp.float32, 'ml_w_out': _jnp.float32, 'lru_w_in': _jnp.float32, 'lru_conv_w': _jnp.float32, 'lru_conv_b': _jnp.float32, 'lru_w_gate_a': _jnp.float32, 'lru_b_gate_a': _jnp.float32, 'lru_w_gate_x': _jnp.float32, 'lru_b_gate_x': _jnp.float32, 'lru_lambda': _jnp.float32, 'lru_w_out': _jnp.float32, 'norm_pre_mix': _jnp.float32, 'norm_post_mix': _jnp.float32, 'norm_pre_ffn': _jnp.float32, 'norm_post_ffn': _jnp.float32, 'ffn_w_gate': _jnp.float32, 'ffn_w_up': _jnp.float32, 'ffn_w_down': _jnp.float32}
MOMENT_SCALE = {'ml_w_in': 1.118168e+00, 'ml_b_if': 6.699659e+00, 'ml_head_norm': 9.844981e-01, 'ml_w_out': 9.805506e-01, 'lru_w_in': 5.419918e-01, 'lru_conv_w': 5.756526e-01, 'lru_conv_b': 7.672277e+00, 'lru_w_gate_a': 1.384275e-01, 'lru_b_gate_a': 1.313196e-01, 'lru_w_gate_x': 2.413592e-01, 'lru_b_gate_x': 2.080793e-01, 'lru_lambda': 2.619227e-01, 'lru_w_out': 5.784851e-01, 'norm_pre_mix': 1.519796e+00, 'norm_post_mix': 3.187639e+01, 'norm_pre_ffn': 8.546911e-01, 'norm_post_ffn': 3.180426e+01, 'ffn_w_gate': 3.412544e-01, 'ffn_w_up': 4.026745e-01, 'ffn_w_down': 6.728282e-01}


def _to_microbatches(a, axis):
    t = _jnp.moveaxis(a, axis, 0)
    t = t.reshape((N_MICROBATCH, t.shape[0] // N_MICROBATCH) + t.shape[1:])
    return _jnp.moveaxis(t, 1, axis + 1)


def setup_inputs(seed: int = 0) -> dict:
    inp = _fwd_setup_inputs(seed)
    key = _jax.random.fold_in(_jax.random.key(seed), 7919)
    shape, _ = _output_shape()
    out = dict(inp)
    out["loss_target"] = _jax.random.normal(_jax.random.fold_in(key, 0), shape, _jnp.float32)
    for i, name in enumerate(TWIN_WEIGHTS):
        w = inp[name].astype(_jnp.float32)
        if MOMENT_SCALE is None:
            s = _jnp.sqrt(_jnp.mean(_jnp.square(w)) + 1e-30)
        else:
            s = MOMENT_SCALE[name]
        km, kv = _jax.random.split(_jax.random.fold_in(key, i + 1))
        out[name] = w
        out["m_" + name] = s * _jax.random.normal(km, w.shape, _jnp.float32)
        out["v_" + name] = (s * s) * _jax.random.uniform(kv, w.shape, _jnp.float32, 0.5, 1.5)
    if N_MICROBATCH > 1:
        for name, axis in PER_EXAMPLE_BATCH_AXIS.items():
            out[name] = _to_microbatches(out[name], axis)
    return {'x': out['x'], 'ml_w_in': out['ml_w_in'], 'ml_b_if': out['ml_b_if'], 'ml_head_norm': out['ml_head_norm'], 'ml_w_out': out['ml_w_out'], 'lru_w_in': out['lru_w_in'], 'lru_conv_w': out['lru_conv_w'], 'lru_conv_b': out['lru_conv_b'], 'lru_w_gate_a': out['lru_w_gate_a'], 'lru_b_gate_a': out['lru_b_gate_a'], 'lru_w_gate_x': out['lru_w_gate_x'], 'lru_b_gate_x': out['lru_b_gate_x'], 'lru_lambda': out['lru_lambda'], 'lru_w_out': out['lru_w_out'], 'norm_pre_mix': out['norm_pre_mix'], 'norm_post_mix': out['norm_post_mix'], 'norm_pre_ffn': out['norm_pre_ffn'], 'norm_post_ffn': out['norm_post_ffn'], 'ffn_w_gate': out['ffn_w_gate'], 'ffn_w_up': out['ffn_w_up'], 'ffn_w_down': out['ffn_w_down'], 'loss_target': out['loss_target'], 'm_ml_w_in': out['m_ml_w_in'], 'm_ml_b_if': out['m_ml_b_if'], 'm_ml_head_norm': out['m_ml_head_norm'], 'm_ml_w_out': out['m_ml_w_out'], 'm_lru_w_in': out['m_lru_w_in'], 'm_lru_conv_w': out['m_lru_conv_w'], 'm_lru_conv_b': out['m_lru_conv_b'], 'm_lru_w_gate_a': out['m_lru_w_gate_a'], 'm_lru_b_gate_a': out['m_lru_b_gate_a'], 'm_lru_w_gate_x': out['m_lru_w_gate_x'], 'm_lru_b_gate_x': out['m_lru_b_gate_x'], 'm_lru_lambda': out['m_lru_lambda'], 'm_lru_w_out': out['m_lru_w_out'], 'm_norm_pre_mix': out['m_norm_pre_mix'], 'm_norm_post_mix': out['m_norm_post_mix'], 'm_norm_pre_ffn': out['m_norm_pre_ffn'], 'm_norm_post_ffn': out['m_norm_post_ffn'], 'm_ffn_w_gate': out['m_ffn_w_gate'], 'm_ffn_w_up': out['m_ffn_w_up'], 'm_ffn_w_down': out['m_ffn_w_down'], 'v_ml_w_in': out['v_ml_w_in'], 'v_ml_b_if': out['v_ml_b_if'], 'v_ml_head_norm': out['v_ml_head_norm'], 'v_ml_w_out': out['v_ml_w_out'], 'v_lru_w_in': out['v_lru_w_in'], 'v_lru_conv_w': out['v_lru_conv_w'], 'v_lru_conv_b': out['v_lru_conv_b'], 'v_lru_w_gate_a': out['v_lru_w_gate_a'], 'v_lru_b_gate_a': out['v_lru_b_gate_a'], 'v_lru_w_gate_x': out['v_lru_w_gate_x'], 'v_lru_b_gate_x': out['v_lru_b_gate_x'], 'v_lru_lambda': out['v_lru_lambda'], 'v_lru_w_out': out['v_lru_w_out'], 'v_norm_pre_mix': out['v_norm_pre_mix'], 'v_norm_post_mix': out['v_norm_post_mix'], 'v_norm_pre_ffn': out['v_norm_pre_ffn'], 'v_norm_post_ffn': out['v_norm_post_ffn'], 'v_ffn_w_gate': out['v_ffn_w_gate'], 'v_ffn_w_up': out['v_ffn_w_up'], 'v_ffn_w_down': out['v_ffn_w_down']}


def _loss(weights, diff, rest, loss_target):
    with _jax.named_scope("forward"):
        args = {**rest, TWIN_DIFF_INPUT: diff, **{k: w.astype(_WEIGHT_DTYPES[k]) for k, w in weights.items()}}
        y = _forward(args)
    with _jax.named_scope("loss_head"):
        err = _jnp.square(y.astype(_jnp.float32) - loss_target)
        return 0.5 * _jnp.sum(_jnp.mean(err, axis=-1)) if err.ndim else 0.5 * err


def _adamw(w, g, m, v):
    m = ADAM_B1 * m + (1.0 - ADAM_B1) * g
    v = ADAM_B2 * v + (1.0 - ADAM_B2) * _jnp.square(g)
    m_hat = m / (1.0 - ADAM_B1 ** ADAM_STEP)
    v_hat = v / (1.0 - ADAM_B2 ** ADAM_STEP)
    delta = -ADAM_LR * (m_hat / (_jnp.sqrt(v_hat) + ADAM_EPS) + ADAM_WD * w)
    return delta, m, v


def reference(x, ml_w_in, ml_b_if, ml_head_norm, ml_w_out, lru_w_in, lru_conv_w, lru_conv_b, lru_w_gate_a, lru_b_gate_a, lru_w_gate_x, lru_b_gate_x, lru_lambda, lru_w_out, norm_pre_mix, norm_post_mix, norm_pre_ffn, norm_post_ffn, ffn_w_gate, ffn_w_up, ffn_w_down, loss_target, m_ml_w_in, m_ml_b_if, m_ml_head_norm, m_ml_w_out, m_lru_w_in, m_lru_conv_w, m_lru_conv_b, m_lru_w_gate_a, m_lru_b_gate_a, m_lru_w_gate_x, m_lru_b_gate_x, m_lru_lambda, m_lru_w_out, m_norm_pre_mix, m_norm_post_mix, m_norm_pre_ffn, m_norm_post_ffn, m_ffn_w_gate, m_ffn_w_up, m_ffn_w_down, v_ml_w_in, v_ml_b_if, v_ml_head_norm, v_ml_w_out, v_lru_w_in, v_lru_conv_w, v_lru_conv_b, v_lru_w_gate_a, v_lru_b_gate_a, v_lru_w_gate_x, v_lru_b_gate_x, v_lru_lambda, v_lru_w_out, v_norm_pre_mix, v_norm_post_mix, v_norm_pre_ffn, v_norm_post_ffn, v_ffn_w_gate, v_ffn_w_up, v_ffn_w_down):
    given = dict(x=x, ml_w_in=ml_w_in, ml_b_if=ml_b_if, ml_head_norm=ml_head_norm, ml_w_out=ml_w_out, lru_w_in=lru_w_in, lru_conv_w=lru_conv_w, lru_conv_b=lru_conv_b, lru_w_gate_a=lru_w_gate_a, lru_b_gate_a=lru_b_gate_a, lru_w_gate_x=lru_w_gate_x, lru_b_gate_x=lru_b_gate_x, lru_lambda=lru_lambda, lru_w_out=lru_w_out, norm_pre_mix=norm_pre_mix, norm_post_mix=norm_post_mix, norm_pre_ffn=norm_pre_ffn, norm_post_ffn=norm_post_ffn, ffn_w_gate=ffn_w_gate, ffn_w_up=ffn_w_up, ffn_w_down=ffn_w_down, loss_target=loss_target, m_ml_w_in=m_ml_w_in, m_ml_b_if=m_ml_b_if, m_ml_head_norm=m_ml_head_norm, m_ml_w_out=m_ml_w_out, m_lru_w_in=m_lru_w_in, m_lru_conv_w=m_lru_conv_w, m_lru_conv_b=m_lru_conv_b, m_lru_w_gate_a=m_lru_w_gate_a, m_lru_b_gate_a=m_lru_b_gate_a, m_lru_w_gate_x=m_lru_w_gate_x, m_lru_b_gate_x=m_lru_b_gate_x, m_lru_lambda=m_lru_lambda, m_lru_w_out=m_lru_w_out, m_norm_pre_mix=m_norm_pre_mix, m_norm_post_mix=m_norm_post_mix, m_norm_pre_ffn=m_norm_pre_ffn, m_norm_post_ffn=m_norm_post_ffn, m_ffn_w_gate=m_ffn_w_gate, m_ffn_w_up=m_ffn_w_up, m_ffn_w_down=m_ffn_w_down, v_ml_w_in=v_ml_w_in, v_ml_b_if=v_ml_b_if, v_ml_head_norm=v_ml_head_norm, v_ml_w_out=v_ml_w_out, v_lru_w_in=v_lru_w_in, v_lru_conv_w=v_lru_conv_w, v_lru_conv_b=v_lru_conv_b, v_lru_w_gate_a=v_lru_w_gate_a, v_lru_b_gate_a=v_lru_b_gate_a, v_lru_w_gate_x=v_lru_w_gate_x, v_lru_b_gate_x=v_lru_b_gate_x, v_lru_lambda=v_lru_lambda, v_lru_w_out=v_lru_w_out, v_norm_pre_mix=v_norm_pre_mix, v_norm_post_mix=v_norm_post_mix, v_norm_pre_ffn=v_norm_pre_ffn, v_norm_post_ffn=v_norm_post_ffn, v_ffn_w_gate=v_ffn_w_gate, v_ffn_w_up=v_ffn_w_up, v_ffn_w_down=v_ffn_w_down)
    weights = {n: given[n] for n in TWIN_WEIGHTS}
    shared = {n: given[n] for n in SHARED_INPUTS}
    per_example = {n: given[n] for n in ['x']}
    grad_fn = _jax.value_and_grad(_loss, argnums=(0, 1))

    def one_microbatch(ex, loss_target):
        ex = dict(ex)
        diff = ex.pop(TWIN_DIFF_INPUT)
        return grad_fn(weights, diff, {**shared, **ex}, loss_target)

    if N_MICROBATCH == 1:
        loss, (grad_w, grad_x) = one_microbatch(per_example, given["loss_target"])
    else:
        def body(carry, xs):
            loss_sum, grad_sum = carry
            l_k, (gw_k, gx_k) = one_microbatch(xs[0], xs[1])
            with _jax.named_scope("update"):
                return (loss_sum + l_k, _jax.tree.map(_jnp.add, grad_sum, gw_k)), gx_k

        init = (_jnp.zeros((), _jnp.float32), _jax.tree.map(_jnp.zeros_like, weights))
        (loss, grad_w), grad_x = _jax.lax.scan(body, init, (per_example, given["loss_target"]))
    with _jax.named_scope("update"):
        delta_w, new_m, new_v = {}, {}, {}
        for n in TWIN_WEIGHTS:
            delta_w[n], new_m[n], new_v[n] = _adamw(weights[n], grad_w[n], given["m_" + n], given["v_" + n])
    return (loss, grad_x, *[grad_w[n] for n in TWIN_WEIGHTS], *[delta_w[n] for n in TWIN_WEIGHTS],
            *[new_m[n] for n in TWIN_WEIGHTS], *[new_v[n] for n in TWIN_WEIGHTS])
```

```python
import functools
import math

import jax
import jax.numpy as jnp
from jax import lax
from jax.experimental import pallas as pl
from jax.experimental.pallas import tpu as pltpu

F32 = jnp.float32
BF16 = jnp.bfloat16
MESH = pl.DeviceIdType.MESH

D_MODEL = 1024
D_FF = 2816
ML_HEADS = 8
ML_DK = 64
ML_DV = 128
ML_QK = ML_HEADS * ML_DK
ML_V = ML_HEADS * ML_DV
ML_IN = 2 * ML_QK + 2 * ML_V + 2 * ML_HEADS
ML_IN_PAD = 3200
ML_GATE_BLOCK = (2 * ML_QK + 2 * ML_V) // 128
CHUNK = 64
GATE_CAP = 15.0
ML_M_INIT = -1e30
NEG_BIG = -1e30
LRU_BLOCKS = 4
LRU_BLOCK = 256
CONV_WIDTH = 4
LRU_C = 8.0
EPS = 1e-6
QK_SCALE = ML_DK ** -0.5

ADAM_LR = 0.001
ADAM_B1 = 0.9
ADAM_B2 = 0.999
ADAM_EPS = 1e-08
ADAM_WD = 0.01
ADAM_STEP = 10

N_CHIPS = 4
V7X_VMEM_LIMIT = 56 * 1024 * 1024

NT_DIMS = (((1,), (1,)), ((), ()))
TN_DIMS = (((0,), (0,)), ((), ()))


def _params(*semantics):
    return pltpu.CompilerParams(dimension_semantics=semantics, vmem_limit_bytes=V7X_VMEM_LIMIT)


def _sds(shape, dtype):
    return jax.ShapeDtypeStruct(shape, dtype)


def _full(shape):
    return pl.BlockSpec(shape, lambda *_: (0,) * len(shape))


def _rows(tm, n):
    return pl.BlockSpec((tm, n), lambda i: (i, 0))


def _sigmoid(x):
    return 1.0 / (1.0 + jnp.exp(-x))


def _log_sigmoid(x):
    return jnp.minimum(x, 0.0) - jnp.log1p(jnp.exp(-jnp.abs(x)))


def _rms(x):
    r = lax.rsqrt(jnp.mean(x * x, axis=-1, keepdims=True) + EPS)
    return x * r, r


def _rms_bwd(xhat, r, g, dy):
    dxh = dy * g
    return r * (dxh - xhat * jnp.mean(dxh * xhat, axis=-1, keepdims=True))


def _scan_rows(a, b, reverse=False):
    n = a.shape[0]
    row = lax.broadcasted_iota(jnp.int32, a.shape, 0)
    s = 1
    while s < n:
        if reverse:
            keep = row < n - s
            shift = n - s
        else:
            keep = row >= s
            shift = s
        b = b + a * jnp.where(keep, pltpu.roll(b, shift, 0), 0.0)
        a = a * jnp.where(keep, pltpu.roll(a, shift, 0), 1.0)
        s *= 2
    return a, b


def _cumsum_rows(x, reverse=False):
    n = x.shape[0]
    row = lax.broadcasted_iota(jnp.int32, x.shape, 0)
    s = 1
    while s < n:
        if reverse:
            x = x + jnp.where(row < n - s, pltpu.roll(x, n - s, 0), 0.0)
        else:
            x = x + jnp.where(row >= s, pltpu.roll(x, s, 0), 0.0)
        s *= 2
    return x


def _norm_matmul(h, g, w, tn, name):
    t, d = h.shape
    n = w.shape[1]
    tm = 512

    def body(h_ref, g_ref, w_ref, z_ref, o_ref):
        @pl.when(pl.program_id(1) == 0)
        def _():
            xhat, _ = _rms(h_ref[...])
            z_ref[...] = (xhat * g_ref[...]).astype(BF16)

        o_ref[...] = jnp.dot(z_ref[...], w_ref[...], preferred_element_type=F32)

    return pl.pallas_call(
        body, name=name, grid=(t // tm, n // tn),
        in_specs=[pl.BlockSpec((tm, d), lambda i, j: (i, 0)), pl.BlockSpec((1, d), lambda i, j: (0, 0)),
                  pl.BlockSpec((d, tn), lambda i, j: (0, j))],
        out_specs=[pl.BlockSpec((tm, d), lambda i, j: (i, 0)), pl.BlockSpec((tm, tn), lambda i, j: (i, j))],
        out_shape=[_sds((t, d), BF16), _sds((t, n), F32)],
        compiler_params=_params("parallel", "arbitrary"),
    )(h, g, w)


def _matmul_postnorm(a, w, g, res, swiglu, name):
    t = res.shape[0]
    k, d = w.shape
    tm = 256

    def body(a_ref, w_ref, g_ref, res_ref, y_ref, o_ref):
        if swiglu:
            gate = a_ref[:, :k]
            act = (gate * _sigmoid(gate) * a_ref[:, k:]).astype(BF16)
        else:
            act = a_ref[...]
        y = jnp.dot(act, w_ref[...], preferred_element_type=F32)
        y_ref[...] = y
        yhat, _ = _rms(y)
        o_ref[...] = res_ref[...] + yhat * g_ref[...]

    return pl.pallas_call(
        body, name=name, grid=(t // tm,),
        in_specs=[_rows(tm, a.shape[1]), _full((k, d)), _full((1, d)), _rows(tm, d)],
        out_specs=[_rows(tm, d), _rows(tm, d)],
        out_shape=[_sds((t, d), F32), _sds((t, d), F32)],
        compiler_params=_params("parallel"),
    )(a, w, g, res)


def _loss_head(h, target, name):
    t, d = h.shape
    tm = 512

    def body(h_ref, t_ref, dh_ref, l_ref):
        @pl.when(pl.program_id(0) == 0)
        def _():
            l_ref[...] = jnp.zeros_like(l_ref)

        err = h_ref[...] - t_ref[...]
        dh_ref[...] = err * (1.0 / d)
        part = jnp.sum(jnp.sum(err * err, axis=1, keepdims=True), axis=0, keepdims=True) * (0.5 / d)
        l_ref[...] += jnp.broadcast_to(part, l_ref.shape)

    return pl.pallas_call(
        body, name=name, grid=(t // tm,),
        in_specs=[_rows(tm, d), _rows(tm, d)],
        out_specs=[_rows(tm, d), _full((1, 128))],
        out_shape=[_sds((t, d), F32), _sds((1, 128), F32)],
        compiler_params=_params("arbitrary"),
    )(h, target)


def _bwd_out(dout, y, g, w, gu, name):
    t, d = dout.shape
    k = w.shape[0]
    ffn = gu is not None
    tm = 128 if ffn else 256

    def body(*refs):
        if ffn:
            dout_ref, y_ref, g_ref, w_ref, gu_ref, dy_ref, dact_ref, act_ref, dg_ref = refs
        else:
            dout_ref, y_ref, g_ref, w_ref, dy_ref, dact_ref, dg_ref = refs

        @pl.when(pl.program_id(0) == 0)
        def _():
            dg_ref[...] = jnp.zeros_like(dg_ref)

        do = dout_ref[...]
        yhat, r = _rms(y_ref[...])
        dg_ref[...] += jnp.sum(do * yhat, axis=0, keepdims=True)
        dy = _rms_bwd(yhat, r, g_ref[...], do).astype(BF16)
        dy_ref[...] = dy
        da = lax.dot_general(dy, w_ref[...], NT_DIMS, preferred_element_type=F32)
        if ffn:
            gate = gu_ref[:, :k]
            up = gu_ref[:, k:]
            sg = _sigmoid(gate)
            silu = gate * sg
            act_ref[...] = (silu * up).astype(BF16)
            dact_ref[:, :k] = (da * up * (sg * (1.0 + gate * (1.0 - sg)))).astype(BF16)
            dact_ref[:, k:] = (da * silu).astype(BF16)
        else:
            dact_ref[...] = da

    in_specs = [_rows(tm, d), _rows(tm, d), _full((1, d)), _full((k, d))]
    args = [dout, y, g, w]
    if ffn:
        in_specs.append(_rows(tm, 2 * k))
        args.append(gu)
        out_specs = [_rows(tm, d), _rows(tm, 2 * k), _rows(tm, k), _full((1, d))]
        out_shape = [_sds((t, d), BF16), _sds((t, 2 * k), BF16), _sds((t, k), BF16), _sds((1, d), F32)]
    else:
        out_specs = [_rows(tm, d), _rows(tm, k), _full((1, d))]
        out_shape = [_sds((t, d), BF16), _sds((t, k), F32), _sds((1, d), F32)]
    return pl.pallas_call(
        body, name=name, grid=(t // tm,), in_specs=in_specs, out_specs=out_specs, out_shape=out_shape,
        compiler_params=_params("arbitrary"),
    )(*args)


def _bwd_in(dp, w, h, g, dout, name):
    t, d = h.shape
    n = w.shape[1]
    tm = 256

    def body(dp_ref, w_ref, h_ref, g_ref, dout_ref, dh_ref, dg_ref):
        @pl.when(pl.program_id(0) == 0)
        def _():
            dg_ref[...] = jnp.zeros_like(dg_ref)

        dz = lax.dot_general(dp_ref[...], w_ref[...], NT_DIMS, preferred_element_type=F32)
        hhat, r = _rms(h_ref[...])
        dg_ref[...] += jnp.sum(dz * hhat, axis=0, keepdims=True)
        dh_ref[...] = dout_ref[...] + _rms_bwd(hhat, r, g_ref[...], dz)

    return pl.pallas_call(
        body, name=name, grid=(t // tm,),
        in_specs=[_rows(tm, n), _full((d, n)), _rows(tm, d), _full((1, d)), _rows(tm, d)],
        out_specs=[_rows(tm, d), _full((1, d))],
        out_shape=[_sds((t, d), F32), _sds((1, d), F32)],
        compiler_params=_params("arbitrary"),
    )(dp, w, h, g, dout)


def _matmul_tn(a, b, tile_a, tile, name):
    t, ka = a.shape
    nb = b.shape[1]

    def body(a_ref, b_ref, o_ref):
        o_ref[...] = lax.dot_general(a_ref[...], b_ref[...], TN_DIMS, preferred_element_type=F32).astype(BF16)

    if tile_a:
        grid = (ka // tile,)
        in_specs = [pl.BlockSpec((t, tile), lambda i: (0, i)), _full((t, nb))]
        out_specs = pl.BlockSpec((tile, nb), lambda i: (i, 0))
    else:
        grid = (nb // tile,)
        in_specs = [_full((t, ka)), pl.BlockSpec((t, tile), lambda i: (0, i))]
        out_specs = pl.BlockSpec((ka, tile), lambda i: (0, i))
    return pl.pallas_call(
        body, name=name, grid=grid, in_specs=in_specs, out_specs=out_specs,
        out_shape=_sds((ka, nb), BF16), compiler_params=_params("parallel"),
    )(a, b)


def _ml_gate_prep(gt, bif):
    th = jnp.tanh((gt + bif) / GATE_CAP)
    act = GATE_CAP * th
    cum = _cumsum_rows(_log_sigmoid(act))
    lane = lax.broadcasted_iota(jnp.int32, gt.shape, 1)
    x = jnp.where(lane < ML_HEADS, act, cum)
    return x, x.T, th, act


def _ml_head_fwd(h, q, k, v, x, xt, c_in, n_in, m_in, causal):
    qh = q[:, ML_DK * h:ML_DK * (h + 1)]
    kh = k[:, ML_DK * h:ML_DK * (h + 1)]
    vh = v[:, ML_DV * h:ML_DV * (h + 1)]
    s = lax.dot_general(qh, kh, NT_DIMS, preferred_element_type=F32) * QK_SCALE
    bcol = x[:, ML_HEADS + h:ML_HEADS + h + 1]
    licol = x[:, h:h + 1]
    brow = xt[ML_HEADS + h:ML_HEADS + h + 1, :]
    lirow = xt[h:h + 1, :]
    dmat = jnp.where(causal, bcol - brow + lirow, NEG_BIG)
    inter = bcol + m_in
    mt = jnp.maximum(jnp.max(dmat, axis=1, keepdims=True), inter)
    wt = jnp.exp(dmat - mt)
    p = wt * s
    winter = jnp.exp(inter - mt)
    cb = c_in.astype(BF16)
    qc = jnp.dot(qh, cb, preferred_element_type=F32) * QK_SCALE
    qf = qh.astype(F32)
    qn = jnp.sum(qf * n_in, axis=1, keepdims=True) * QK_SCALE
    den = jnp.sum(p, axis=1, keepdims=True) + winter * qn
    emt = jnp.exp(-mt)
    nrm = jnp.maximum(jnp.abs(den), emt)
    gtot = bcol[CHUNK - 1:CHUNK, :]
    a_col = gtot - bcol + licol
    m_new = jnp.maximum(gtot + m_in, jnp.max(a_col, axis=0, keepdims=True))
    decay = jnp.exp(gtot + m_in - m_new)
    wkf = jnp.exp(a_col - m_new)
    kw = kh.astype(F32) * wkf
    return dict(qh=qh, kh=kh, vh=vh, wt=wt, p=p, winter=winter, cb=cb, qc=qc, qf=qf, qn=qn, den=den,
                emt=emt, nrm=nrm, m_new=m_new, decay=decay, wkf=wkf, kw=kw)


def _ml_specs(nc, rev):
    def at(col):
        if rev:
            return lambda i: (nc - 1 - i, col)
        return lambda i: (i, col)

    return [pl.BlockSpec((CHUNK, ML_QK), at(0)), pl.BlockSpec((CHUNK, ML_QK), at(1)),
            pl.BlockSpec((CHUNK, ML_V), at(1)), pl.BlockSpec((CHUNK, ML_V), at(2)),
            pl.BlockSpec((CHUNK, 128), at(ML_GATE_BLOCK))]


def _mlstm_fwd(proj, bif, hn):
    t = proj.shape[0]
    nc = t // CHUNK

    def body(q_ref, k_ref, v_ref, o_ref, gt_ref, bif_ref, hn_ref, y_ref, hs_ref, cst_ref, nst_ref, mst_ref,
             c_sc, n_sc, m_sc):
        @pl.when(pl.program_id(0) == 0)
        def _():
            c_sc[...] = jnp.zeros_like(c_sc)
            n_sc[...] = jnp.zeros_like(n_sc)
            m_sc[...] = jnp.full_like(m_sc, ML_M_INIT)

        cst_ref[0] = c_sc[...]
        nst_ref[0] = n_sc[...]
        mst_ref[0] = m_sc[...]
        x, xt, _, _ = _ml_gate_prep(gt_ref[...], bif_ref[...])
        q = q_ref[...].astype(BF16)
        k = k_ref[...].astype(BF16)
        v = v_ref[...].astype(BF16)
        causal = (lax.broadcasted_iota(jnp.int32, (CHUNK, CHUNK), 0)
                  >= lax.broadcasted_iota(jnp.int32, (CHUNK, CHUNK), 1))
        for h in range(ML_HEADS):
            vs = slice(ML_DV * h, ML_DV * (h + 1))
            c_in = c_sc[ML_DK * h:ML_DK * (h + 1), :]
            n_in = n_sc[h:h + 1, :]
            m_in = m_sc[h:h + 1, 0:1]
            f = _ml_head_fwd(h, q, k, v, x, xt, c_in, n_in, m_in, causal)
            num = jnp.dot(f["p"].astype(BF16), f["vh"], preferred_element_type=F32) + f["winter"] * f["qc"]
            hh = num / f["nrm"]
            hs_ref[:, vs] = hh
            hhat, _ = _rms(hh)
            y_ref[:, vs] = (hhat * hn_ref[:, vs] * _sigmoid(o_ref[:, vs])).astype(BF16)
            c_sc[ML_DK * h:ML_DK * (h + 1), :] = f["decay"] * c_in + lax.dot_general(
                f["kw"].astype(BF16), f["vh"], TN_DIMS, preferred_element_type=F32)
            n_sc[h:h + 1, :] = f["decay"] * n_in + jnp.sum(f["kw"], axis=0, keepdims=True)
            m_sc[h:h + 1, :] = jnp.broadcast_to(f["m_new"], (1, 128))

    return pl.pallas_call(
        body, name="mlstm_fwd", grid=(nc,),
        in_specs=_ml_specs(nc, False) + [_full((1, 128)), _full((1, ML_V))],
        out_specs=[_rows(CHUNK, ML_V), _rows(CHUNK, ML_V),
                   pl.BlockSpec((1, ML_HEADS * ML_DK, ML_DV), lambda i: (i, 0, 0)),
                   pl.BlockSpec((1, ML_HEADS, ML_DK), lambda i: (i, 0, 0)),
                   pl.BlockSpec((1, ML_HEADS, 128), lambda i: (i, 0, 0))],
        out_shape=[_sds((t, ML_V), BF16), _sds((t, ML_V), F32),
                   _sds((nc, ML_HEADS * ML_DK, ML_DV), F32), _sds((nc, ML_HEADS, ML_DK), F32),
                   _sds((nc, ML_HEADS, 128), F32)],
        scratch_shapes=[pltpu.VMEM((ML_HEADS * ML_DK, ML_DV), F32), pltpu.VMEM((ML_HEADS, ML_DK), F32),
                        pltpu.VMEM((ML_HEADS, 128), F32)],
        compiler_params=_params("arbitrary"),
    )(proj, proj, proj, proj, proj, bif, hn)


def _mlstm_bwd(proj, bif, hn, hs, cst, nst, mst, dy):
    t = proj.shape[0]
    nc = t // CHUNK

    def body(q_ref, k_ref, v_ref, o_ref, gt_ref, bif_ref, hn_ref, hs_ref, cst_ref, nst_ref, mst_ref, dy_ref,
             dp_ref, dhn_ref, dbif_ref, dc_sc, dn_sc):
        @pl.when(pl.program_id(0) == 0)
        def _():
            dc_sc[...] = jnp.zeros_like(dc_sc)
            dn_sc[...] = jnp.zeros_like(dn_sc)
            dhn_ref[...] = jnp.zeros_like(dhn_ref)
            dbif_ref[...] = jnp.zeros_like(dbif_ref)

        x, xt, th, act = _ml_gate_prep(gt_ref[...], bif_ref[...])
        q = q_ref[...].astype(BF16)
        k = k_ref[...].astype(BF16)
        v = v_ref[...].astype(BF16)
        causal = (lax.broadcasted_iota(jnp.int32, (CHUNK, CHUNK), 0)
                  >= lax.broadcasted_iota(jnp.int32, (CHUNK, CHUNK), 1))
        lane = lax.broadcasted_iota(jnp.int32, (CHUNK, 128), 1)
        row = lax.broadcasted_iota(jnp.int32, (CHUNK, 1), 0)
        dx = jnp.zeros((CHUNK, 128), F32)
        for h in range(ML_HEADS):
            ks = slice(ML_DK * h, ML_DK * (h + 1))
            vs = slice(ML_DV * h, ML_DV * (h + 1))
            c_in = cst_ref[0, ks, :]
            n_in = nst_ref[0, h:h + 1, :]
            m_in = mst_ref[0, h:h + 1, 0:1]
            f = _ml_head_fwd(h, q, k, v, x, xt, c_in, n_in, m_in, causal)
            qh, kh, vh, p, winter = f["qh"], f["kh"], f["vh"], f["p"], f["winter"]
            hh = hs_ref[:, vs]
            hhat, r = _rms(hh)
            hn_h = hn_ref[:, vs]
            sg = _sigmoid(o_ref[:, vs])
            dyh = dy_ref[:, vs]
            dyn = dyh * sg
            dp_ref[:, 2 * ML_QK + ML_V + ML_DV * h:2 * ML_QK + ML_V + ML_DV * (h + 1)] = (
                dyh * hhat * hn_h * sg * (1.0 - sg)).astype(BF16)
            dhn_ref[:, vs] += jnp.sum(dyn * hhat, axis=0, keepdims=True)
            dh = _rms_bwd(hhat, r, hn_h, dyn)
            inv = 1.0 / f["nrm"]
            dnum = dh * inv
            dnrm = -jnp.sum(dh * hh, axis=1, keepdims=True) * inv
            dden = jnp.where(jnp.abs(f["den"]) > f["emt"], dnrm * jnp.sign(f["den"]), 0.0)
            dnb = dnum.astype(BF16)
            rmat = lax.dot_general(dnb, vh, NT_DIMS, preferred_element_type=F32) + dden
            ds = (f["wt"] * rmat).astype(BF16)
            qmat = p * rmat
            dv = lax.dot_general(p.astype(BF16), dnb, TN_DIMS, preferred_element_type=F32)
            dqs = jnp.dot(ds, kh, preferred_element_type=F32) + winter * (
                lax.dot_general(dnb, f["cb"], NT_DIMS, preferred_element_type=F32) + dden * n_in)
            dk = lax.dot_general(ds, qh, TN_DIMS, preferred_element_type=F32) * QK_SCALE
            dinter = (jnp.sum(f["qc"] * dnum, axis=1, keepdims=True) + f["qn"] * dden) * winter
            wq = winter * f["qf"] * QK_SCALE
            dc_loc = lax.dot_general(wq.astype(BF16), dnb, TN_DIMS, preferred_element_type=F32)
            dn_loc = jnp.sum(wq * dden, axis=0, keepdims=True)
            cs_q = jnp.sum(qmat.T, axis=1, keepdims=True)
            db = jnp.sum(qmat, axis=1, keepdims=True) + dinter - cs_q
            dli = cs_q
            dcn = dc_sc[ks, :]
            dnn = dn_sc[h:h + 1, :]
            dcnb = dcn.astype(BF16)
            ddecay = (jnp.sum(jnp.sum(dcn * c_in, axis=1, keepdims=True), axis=0, keepdims=True)
                      + jnp.sum(dnn * n_in, axis=1, keepdims=True))
            dkw = lax.dot_general(vh, dcnb, NT_DIMS, preferred_element_type=F32) + dnn
            dk = dk + f["wkf"] * dkw
            da = jnp.sum(dkw * kh.astype(F32), axis=1, keepdims=True) * f["wkf"]
            dv = dv + jnp.dot(f["kw"].astype(BF16), dcnb, preferred_element_type=F32)
            dgtot = jnp.sum(da, axis=0, keepdims=True) + ddecay * f["decay"]
            db = db - da + jnp.where(row == CHUNK - 1, dgtot, 0.0)
            dli = dli + da
            dc_sc[ks, :] = f["decay"] * dcn + dc_loc
            dn_sc[h:h + 1, :] = f["decay"] * dnn + dn_loc
            dp_ref[:, ks] = (dqs * QK_SCALE).astype(BF16)
            dp_ref[:, ML_QK + ML_DK * h:ML_QK + ML_DK * (h + 1)] = dk.astype(BF16)
            dp_ref[:, 2 * ML_QK + ML_DV * h:2 * ML_QK + ML_DV * (h + 1)] = dv.astype(BF16)
            dx = jnp.where(lane == h, dli, dx)
            dx = jnp.where(lane == ML_HEADS + h, db, dx)
        dlf = _cumsum_rows(dx, reverse=True)
        dact = jnp.where(lane < ML_HEADS, dx, dlf * _sigmoid(-act))
        dz = dact * (1.0 - th * th)
        dp_ref[:, 2 * ML_QK + 2 * ML_V:] = dz.astype(BF16)
        dbif_ref[...] += jnp.sum(dz, axis=0, keepdims=True)

    rev = lambda i: (nc - 1 - i, 0)
    rev3 = lambda i: (nc - 1 - i, 0, 0)
    return pl.pallas_call(
        body, name="mlstm_bwd", grid=(nc,),
        in_specs=_ml_specs(nc, True) + [
            _full((1, 128)), _full((1, ML_V)), pl.BlockSpec((CHUNK, ML_V), rev),
            pl.BlockSpec((1, ML_HEADS * ML_DK, ML_DV), rev3), pl.BlockSpec((1, ML_HEADS, ML_DK), rev3),
            pl.BlockSpec((1, ML_HEADS, 128), rev3), pl.BlockSpec((CHUNK, ML_V), rev)],
        out_specs=[pl.BlockSpec((CHUNK, ML_IN_PAD), rev), _full((1, ML_V)), _full((1, 128))],
        out_shape=[_sds((t, ML_IN_PAD), BF16), _sds((1, ML_V), F32), _sds((1, 128), F32)],
        scratch_shapes=[pltpu.VMEM((ML_HEADS * ML_DK, ML_DV), F32), pltpu.VMEM((ML_HEADS, ML_DK), F32)],
        compiler_params=_params("arbitrary"),
    )(proj, proj, proj, proj, proj, bif, hn, hs, cst, nst, mst, dy)


LRU_TM = 256
GELU_K = math.sqrt(2.0 / math.pi)
GELU_C = 0.044715


def _gelu(x):
    th = jnp.tanh(GELU_K * (x + GELU_C * x * x * x))
    return 0.5 * x * (1.0 + th), th


def _neg_expm1(x):
    series = -x * (1.0 + x * (0.5 + x * (1.0 / 6.0 + x * (1.0 / 24.0))))
    return jnp.where(x > -0.05, series, 1.0 - jnp.exp(x))


def _block_diag_dot(a, w_ref, dims):
    parts = [lax.dot_general(a[:, LRU_BLOCK * n:LRU_BLOCK * (n + 1)], w_ref[n], dims, preferred_element_type=F32)
             for n in range(LRU_BLOCKS)]
    return jnp.concatenate(parts, axis=1)


def _lru_gates(u, r, ig, lam):
    ls = _log_sigmoid(lam)
    la = LRU_C * r * ls
    a = jnp.exp(la)
    em = _neg_expm1(2.0 * la)
    mult = jnp.sqrt(em)
    return ls, a, em, mult


def _lru_fwd(proj, cw, cb, wa, ba, wx, bx, lam):
    t = proj.shape[0]
    w = D_MODEL
    tm = LRU_TM

    def body(gb_ref, up_ref, cw_ref, cb_ref, wa_ref, ba_ref, wx_ref, bx_ref, lam_ref,
             y_ref, u_ref, r_ref, i_ref, h_ref, tail_sc, hprev_sc):
        @pl.when(pl.program_id(0) == 0)
        def _():
            tail_sc[...] = jnp.zeros_like(tail_sc)
            hprev_sc[...] = jnp.zeros_like(hprev_sc)

        up = up_ref[...]
        ext = jnp.concatenate([tail_sc[...], up], axis=0)
        u = cb_ref[...] + cw_ref[CONV_WIDTH - 1:CONV_WIDTH, :] * up
        for s in range(1, CONV_WIDTH):
            u = u + cw_ref[CONV_WIDTH - 1 - s:CONV_WIDTH - s, :] * pltpu.roll(ext, s, 0)[8:8 + tm]
        tail_sc[...] = up[tm - 8:tm]
        ub = u.astype(BF16)
        r = _sigmoid(_block_diag_dot(ub, wa_ref, (((1,), (0,)), ((), ()))) + ba_ref[...])
        ig = _sigmoid(_block_diag_dot(ub, wx_ref, (((1,), (0,)), ((), ()))) + bx_ref[...])
        _, a, _, mult = _lru_gates(u, r, ig, lam_ref[...])
        acum, hloc = _scan_rows(a, mult * ig * u)
        h = hloc + acum * hprev_sc[0:1, :]
        hprev_sc[0:1, :] = h[tm - 1:tm]
        u_ref[...] = u
        r_ref[...] = r
        i_ref[...] = ig
        h_ref[...] = h
        gel, _ = _gelu(gb_ref[...])
        y_ref[...] = (h * gel).astype(BF16)

    vec = _full((1, w))
    wspec = _full((LRU_BLOCKS, LRU_BLOCK, LRU_BLOCK))
    return pl.pallas_call(
        body, name="lru_fwd", grid=(t // tm,),
        in_specs=[pl.BlockSpec((tm, w), lambda i: (i, 0)), pl.BlockSpec((tm, w), lambda i: (i, 1)),
                  _full((CONV_WIDTH, w)), vec, wspec, vec, wspec, vec, vec],
        out_specs=[_rows(tm, w)] * 5,
        out_shape=[_sds((t, w), BF16)] + [_sds((t, w), F32)] * 4,
        scratch_shapes=[pltpu.VMEM((8, w), F32), pltpu.VMEM((8, w), F32)],
        compiler_params=_params("arbitrary"),
    )(proj, proj, cw, cb, wa, ba, wx, bx, lam)


def _lru_bwd(proj, cw, wa, wx, lam, u, r, ig, h, dy):
    t = proj.shape[0]
    w = D_MODEL
    tm = LRU_TM
    nt = t // tm

    def body(gb_ref, up_ref, cw_ref, wa_ref, wx_ref, lam_ref, u_ref, r_ref, i_ref, h_ref, hp_ref, dy_ref,
             dp_ref, dcw_ref, dcb_ref, dwa_ref, dba_ref, dwx_ref, dbx_ref, dlam_ref, carry_sc, dutail_sc, dls_sc):
        step = pl.program_id(0)

        @pl.when(step == 0)
        def _():
            carry_sc[...] = jnp.zeros_like(carry_sc)
            dutail_sc[...] = jnp.zeros_like(dutail_sc)
            dls_sc[...] = jnp.zeros_like(dls_sc)
            for ref in (dcw_ref, dcb_ref, dwa_ref, dba_ref, dwx_ref, dbx_ref):
                ref[...] = jnp.zeros_like(ref)

        row = lax.broadcasted_iota(jnp.int32, (tm, w), 0)
        u_t, r_t, i_t, h_t = u_ref[...], r_ref[...], i_ref[...], h_ref[...]
        ls, a, em, mult = _lru_gates(u_t, r_t, i_t, lam_ref[...])
        gb = gb_ref[...]
        gel, th = _gelu(gb)
        dyv = dy_ref[...]
        dgb = dyv * h_t * (0.5 * (1.0 + th) + 0.5 * gb * (1.0 - th * th) * GELU_K * (1.0 + 3.0 * GELU_C * gb * gb))
        dh = jnp.where(row == tm - 1, dyv * gel + carry_sc[0:1, :], dyv * gel)
        a_next = jnp.where(row < tm - 1, pltpu.roll(a, tm - 1, 0), 0.0)
        _, g = _scan_rows(a_next, dh, reverse=True)
        carry_sc[0:1, :] = a[0:1] * g[0:1]
        has_prev = jnp.where(step == nt - 1, 0.0, 1.0)
        h_prev = jnp.where(row >= 1, pltpu.roll(h_t, 1, 0), hp_ref[7:8, :] * has_prev)
        dmult = g * i_t * u_t
        dig = g * mult * u_t
        du = g * mult * i_t
        dla = g * h_prev * a - dmult * (1.0 - em) / mult
        dls_sc[0:1, :] += jnp.sum(dla * r_t, axis=0, keepdims=True) * LRU_C
        dpa = dla * (LRU_C * ls) * r_t * (1.0 - r_t)
        dpx = dig * i_t * (1.0 - i_t)
        dba_ref[...] += jnp.sum(dpa, axis=0, keepdims=True)
        dbx_ref[...] += jnp.sum(dpx, axis=0, keepdims=True)
        ub = u_t.astype(BF16)
        dpab = dpa.astype(BF16)
        dpxb = dpx.astype(BF16)
        for n in range(LRU_BLOCKS):
            cs = slice(LRU_BLOCK * n, LRU_BLOCK * (n + 1))
            dwa_ref[n] += lax.dot_general(ub[:, cs], dpab[:, cs], TN_DIMS, preferred_element_type=F32)
            dwx_ref[n] += lax.dot_general(ub[:, cs], dpxb[:, cs], TN_DIMS, preferred_element_type=F32)
        du = du + _block_diag_dot(dpab, wa_ref, NT_DIMS) + _block_diag_dot(dpxb, wx_ref, NT_DIMS)
        dcb_ref[...] += jnp.sum(du, axis=0, keepdims=True)
        ext = jnp.concatenate([du, dutail_sc[...]], axis=0)
        up = up_ref[...]
        dup = cw_ref[CONV_WIDTH - 1:CONV_WIDTH, :] * du
        dcw_ref[CONV_WIDTH - 1:CONV_WIDTH, :] += jnp.sum(up * du, axis=0, keepdims=True)
        for s in range(1, CONV_WIDTH):
            du_s = pltpu.roll(ext, tm + 8 - s, 0)[0:tm]
            dup = dup + cw_ref[CONV_WIDTH - 1 - s:CONV_WIDTH - s, :] * du_s
            dcw_ref[CONV_WIDTH - 1 - s:CONV_WIDTH - s, :] += jnp.sum(up * du_s, axis=0, keepdims=True)
        dutail_sc[...] = du[0:8]
        dp_ref[:, :w] = dgb.astype(BF16)
        dp_ref[:, w:] = dup.astype(BF16)

        @pl.when(step == nt - 1)
        def _():
            dlam_ref[...] = dls_sc[0:1, :] * _sigmoid(-lam_ref[...])

    rev = lambda col: (lambda i: (nt - 1 - i, col))
    vec = _full((1, w))
    wspec = _full((LRU_BLOCKS, LRU_BLOCK, LRU_BLOCK))
    tile = pl.BlockSpec((tm, w), rev(0))
    prev8 = pl.BlockSpec((8, w), lambda i: (jnp.maximum((nt - 1 - i) * (tm // 8) - 1, 0), 0))
    return pl.pallas_call(
        body, name="lru_bwd", grid=(nt,),
        in_specs=[tile, pl.BlockSpec((tm, w), rev(1)), _full((CONV_WIDTH, w)), wspec, wspec, vec,
                  tile, tile, tile, tile, prev8, tile],
        out_specs=[pl.BlockSpec((tm, 2 * w), rev(0)), _full((CONV_WIDTH, w)), vec, wspec, vec, wspec, vec, vec],
        out_shape=[_sds((t, 2 * w), BF16), _sds((CONV_WIDTH, w), F32), _sds((1, w), F32),
                   _sds((LRU_BLOCKS, LRU_BLOCK, LRU_BLOCK), F32), _sds((1, w), F32),
                   _sds((LRU_BLOCKS, LRU_BLOCK, LRU_BLOCK), F32), _sds((1, w), F32), _sds((1, w), F32)],
        scratch_shapes=[pltpu.VMEM((8, w), F32), pltpu.VMEM((8, w), F32), pltpu.VMEM((8, w), F32)],
        compiler_params=_params("arbitrary"),
    )(proj, proj, cw, wa, wx, lam, u, r, ig, h, h, dy)


def _ffn_fwd(h, g_pre, w_gu, w_down, g_post, layer):
    z, gu = _norm_matmul(h, g_pre, w_gu, 512, f"ffn{layer}_up")
    y, out = _matmul_postnorm(gu, w_down, g_post, h, True, f"ffn{layer}_down")
    return out, (h, z, gu, y)


def _ffn_bwd(dout, saved, g_pre, w_gu, w_down, g_post, layer):
    h, z, gu, y = saved
    dy, dgu, act, dg_post = _bwd_out(dout, y, g_post, w_down, gu, f"ffn{layer}_bwd_down")
    dw_down = _matmul_tn(act, dy, True, 256, f"ffn{layer}_dw_down")
    dw_gu = _matmul_tn(z, dgu, False, 512, f"ffn{layer}_dw_up")
    dh, dg_pre = _bwd_in(dgu, w_gu, h, g_pre, dout, f"ffn{layer}_bwd_up")
    return dh, dw_gu, dw_down, dg_pre, dg_post


def _local_step(x, target, wts):
    npm, nqm, npf, nqf = wts["norm_pre_mix"], wts["norm_post_mix"], wts["norm_pre_ffn"], wts["norm_post_ffn"]
    row = lambda a, i: a[i:i + 1]
    z0, proj0 = _norm_matmul(x, row(npm, 0), wts["ml_w_in"], 640, "ml_in")
    y0, hs, cst, nst, mst = _mlstm_fwd(proj0, wts["ml_b_if"], wts["ml_head_norm"])
    ymix0, h1 = _matmul_postnorm(y0, wts["ml_w_out"], row(nqm, 0), x, False, "ml_out")
    h2, ffn0 = _ffn_fwd(h1, row(npf, 0), wts["ffn_w_gu"][0], wts["ffn_w_down"][0], row(nqf, 0), 0)
    z1, proj1 = _norm_matmul(h2, row(npm, 1), wts["lru_w_in"], 512, "lru_in")
    y1, u, r, ig, hl = _lru_fwd(proj1, wts["lru_conv_w"], wts["lru_conv_b"], wts["lru_w_gate_a"], wts["lru_b_gate_a"],
                                wts["lru_w_gate_x"], wts["lru_b_gate_x"], wts["lru_lambda"])
    ymix1, h3 = _matmul_postnorm(y1, wts["lru_w_out"], row(nqm, 1), h2, False, "lru_out")
    h4, ffn1 = _ffn_fwd(h3, row(npf, 1), wts["ffn_w_gu"][1], wts["ffn_w_down"][1], row(nqf, 1), 1)
    dh4, loss = _loss_head(h4, target, "loss_head")
    g = {}
    dh3, dgu1, dwd1, dpf1, dqf1 = _ffn_bwd(dh4, ffn1, row(npf, 1), wts["ffn_w_gu"][1], wts["ffn_w_down"][1], row(nqf, 1), 1)
    dymix1, dy1, dqm1 = _bwd_out(dh3, ymix1, row(nqm, 1), wts["lru_w_out"], None, "lru_bwd_out")
    g["lru_w_out"] = _matmul_tn(y1, dymix1, False, 256, "lru_dw_out")
    (dproj1, g["lru_conv_w"], g["lru_conv_b"], g["lru_w_gate_a"], g["lru_b_gate_a"], g["lru_w_gate_x"],
     g["lru_b_gate_x"], g["lru_lambda"]) = _lru_bwd(proj1, wts["lru_conv_w"], wts["lru_w_gate_a"], wts["lru_w_gate_x"],
                                                     wts["lru_lambda"], u, r, ig, hl, dy1)
    g["lru_w_in"] = _matmul_tn(z1, dproj1, False, 512, "lru_dw_in")
    dh2, dpm1 = _bwd_in(dproj1, wts["lru_w_in"], h2, row(npm, 1), dh3, "lru_bwd_in")
    dh1, dgu0, dwd0, dpf0, dqf0 = _ffn_bwd(dh2, ffn0, row(npf, 0), wts["ffn_w_gu"][0], wts["ffn_w_down"][0], row(nqf, 0), 0)
    dymix0, dy0, dqm0 = _bwd_out(dh1, ymix0, row(nqm, 0), wts["ml_w_out"], None, "ml_bwd_out")
    g["ml_w_out"] = _matmul_tn(y0, dymix0, False, 256, "ml_dw_out")
    dproj0, g["ml_head_norm"], g["ml_b_if"] = _mlstm_bwd(proj0, wts["ml_b_if"], wts["ml_head_norm"], hs, cst, nst, mst, dy0)
    g["ml_w_in"] = _matmul_tn(z0, dproj0, False, 640, "ml_dw_in")
    dx, dpm0 = _bwd_in(dproj0, wts["ml_w_in"], x, row(npm, 0), dh1, "ml_bwd_in")
    g["ffn_w_gu"] = jnp.stack([dgu0, dgu1])
    g["ffn_w_down"] = jnp.stack([dwd0, dwd1])
    g["norm_pre_mix"] = jnp.concatenate([dpm0, dpm1], axis=0)
    g["norm_post_mix"] = jnp.concatenate([dqm0, dqm1], axis=0)
    g["norm_pre_ffn"] = jnp.concatenate([dpf0, dpf1], axis=0)
    g["norm_post_ffn"] = jnp.concatenate([dqf0, dqf1], axis=0)
    return loss, dx, g


LANES = 1024
SLAB_PARTS = (
    ("ml_w_in", (1024, 772), 1),
    ("ml_w_out", (256, 1024), 0),
    ("lru_w_in", (1024, 512), 1),
    ("lru_w_gate_a", (4, 64, 256), 1),
    ("lru_w_gate_x", (4, 64, 256), 1),
    ("lru_w_out", (256, 1024), 0),
    ("ffn_w_gate", (2, 1024, 704), 2),
    ("ffn_w_up", (2, 1024, 704), 2),
    ("ffn_w_down", (2, 704, 1024), 1),
)
ROW_ALIGN = 16


def _part_rows(shape):
    rows = math.prod(shape) // LANES
    return -(-rows // ROW_ALIGN) * ROW_ALIGN


SLAB_ROWS = 6272
SLAB_HALF = SLAB_ROWS // 2
SUM_TILE = SLAB_HALF // 4
assert sum(_part_rows(s) for _, s, _ in SLAB_PARTS) <= SLAB_ROWS and SUM_TILE % ROW_ALIGN == 0

SMALL_ROWS = 24
ROW_LOSS, ROW_HEAD_NORM, ROW_B_IF, ROW_LRU = 8, 9, 10, 11


def _pack_slab(parts, dtype):
    rows = []
    used = 0
    for name, shape, _ in SLAB_PARTS:
        flat = parts[name].astype(dtype).reshape(-1, LANES)
        pad = _part_rows(shape) - flat.shape[0]
        rows.append(jnp.pad(flat, ((0, pad), (0, 0))) if pad else flat)
        used += _part_rows(shape)
    rows.append(jnp.zeros((SLAB_ROWS - used, LANES), dtype))
    return jnp.concatenate(rows, axis=0)


def _unpack_slab(slab):
    out = {}
    at = 0
    for name, shape, _ in SLAB_PARTS:
        rows = math.prod(shape) // LANES
        out[name] = slab[..., at:at + rows, :].reshape(slab.shape[:-2] + shape)
        at += _part_rows(shape)
    return out


def _chip_peers():
    x, y, c = lax.axis_index("x"), lax.axis_index("y"), lax.axis_index("c")
    return x, y, c, [(1 - x, y), (x, 1 - y), (1 - x, 1 - y)]


HBM_SPEC = pl.BlockSpec(memory_space=pltpu.HBM)


def _remote(src, dst, send_sems, recv_sems, k, to):
    return pltpu.make_async_remote_copy(src_ref=src, dst_ref=dst, send_sem=send_sems.at[k], recv_sem=recv_sems.at[k],
                                        device_id=to, device_id_type=MESH)


def _gather_weights(slab, small):
    half = SLAB_HALF

    def body(slab_ref, small_ref, out_ref, outs_ref, send_sems, recv_sems, loc_sems):
        x, y, c, chips = _chip_peers()
        me = 2 * x + y
        sib = (x, y, 1 - c)
        mine = pl.ds(c * half, half)
        theirs = pl.ds((1 - c) * half, half)
        copy = functools.partial(_remote, send_sems=send_sems, recv_sems=recv_sems)
        local = [pltpu.make_async_copy(slab_ref, out_ref.at[me], loc_sems.at[0]),
                 pltpu.make_async_copy(small_ref, outs_ref.at[me], loc_sems.at[1])]
        for cp in local:
            cp.start()
        first = []
        for j, (cx, cy) in enumerate(chips):
            first.append(copy(slab_ref.at[mine], out_ref.at[me, mine], k=j, to=(cx, cy, c)))
            first.append(copy(small_ref, outs_ref.at[me], k=3 + j, to=(cx, cy, c)))
        for cp in first:
            cp.start()
        passed = []
        for j, (cx, cy) in enumerate(chips):
            land = out_ref.at[2 * cx + cy, mine]
            copy(land, land, k=j, to=sib).wait_recv()
            passed.append(copy(land, land, k=6 + j, to=sib))
            passed[-1].start()
        for j, (cx, cy) in enumerate(chips):
            copy(small_ref, outs_ref.at[2 * cx + cy], k=3 + j, to=sib).wait_recv()
            land = out_ref.at[2 * cx + cy, theirs]
            copy(land, land, k=6 + j, to=sib).wait_recv()
        for cp in first + passed:
            cp.wait_send()
        for cp in local:
            cp.wait()

    return pl.pallas_call(
        body, name="gather_weights",
        in_specs=[HBM_SPEC, HBM_SPEC], out_specs=[HBM_SPEC, HBM_SPEC],
        out_shape=[_sds((N_CHIPS,) + slab.shape, slab.dtype), _sds((N_CHIPS,) + small.shape, small.dtype)],
        scratch_shapes=[pltpu.SemaphoreType.DMA((9,)), pltpu.SemaphoreType.DMA((9,)), pltpu.SemaphoreType.DMA((2,))],
    )(slab, small)


def _exchange_halves(gpack):
    half = SLAB_HALF

    def body(g_ref, out_ref, send_sems, recv_sems):
        x, y, c, _ = _chip_peers()
        cp = _remote(g_ref.at[:, pl.ds((1 - c) * half, half)], out_ref, send_sems, recv_sems, 0, (x, y, 1 - c))
        cp.start()
        cp.wait()

    return pl.pallas_call(
        body, name="exchange_halves", in_specs=[HBM_SPEC], out_specs=HBM_SPEC,
        out_shape=_sds((N_CHIPS, half, LANES), gpack.dtype),
        scratch_shapes=[pltpu.SemaphoreType.DMA((1,)), pltpu.SemaphoreType.DMA((1,))],
    )(gpack)


def _add_halves(gpack, recv, c):
    tiles = SLAB_HALF // SUM_TILE

    def body(c_ref, a_ref, b_ref, o_ref):
        o_ref[...] = (a_ref[...].astype(F32) + b_ref[...].astype(F32)).astype(BF16)

    spec = pl.BlockSpec((1, SUM_TILE, LANES), lambda k, i, c_ref: (k, i, 0))
    return pl.pallas_call(
        body, name="add_halves",
        grid_spec=pltpu.PrefetchScalarGridSpec(
            num_scalar_prefetch=1, grid=(N_CHIPS, tiles),
            in_specs=[pl.BlockSpec((1, SUM_TILE, LANES), lambda k, i, c_ref: (k, c_ref[0] * tiles + i, 0)), spec],
            out_specs=spec),
        out_shape=_sds((N_CHIPS, SLAB_HALF, LANES), BF16),
        compiler_params=_params("parallel", "parallel"),
    )(c.reshape(1), gpack, recv)


def _scatter_to_chips(s1, small):
    flips = [(fx, fy, fc) for fx in (0, 1) for fy in (0, 1) for fc in (0, 1)][1:]

    def body(s1_ref, small_ref, recv_ref, all_ref, send_sems, recv_sems, loc_sem):
        x, y, c, chips = _chip_peers()
        copy = functools.partial(_remote, send_sems=send_sems, recv_sems=recv_sems)
        local = pltpu.make_async_copy(small_ref, all_ref.at[4 * x + 2 * y + c], loc_sem.at[0])
        local.start()
        sent = [copy(s1_ref.at[2 * cx + cy], recv_ref.at[j], k=j, to=(cx, cy, c)) for j, (cx, cy) in enumerate(chips)]
        peers = [(1 - x if fx else x, 1 - y if fy else y, 1 - c if fc else c) for fx, fy, fc in flips]
        sent += [copy(small_ref, all_ref.at[4 * x + 2 * y + c], k=3 + n, to=p) for n, p in enumerate(peers)]
        for cp in sent:
            cp.start()
        for j in range(3):
            copy(s1_ref.at[0], recv_ref.at[j], k=j, to=peers[0]).wait_recv()
        for n, (px, py, pc) in enumerate(peers):
            copy(small_ref, all_ref.at[4 * px + 2 * py + pc], k=3 + n, to=peers[0]).wait_recv()
        for cp in sent:
            cp.wait_send()
        local.wait()

    return pl.pallas_call(
        body, name="scatter_to_chips", in_specs=[HBM_SPEC, HBM_SPEC], out_specs=[HBM_SPEC, HBM_SPEC],
        out_shape=[_sds((3, SLAB_HALF, LANES), s1.dtype), _sds((8,) + small.shape, small.dtype)],
        scratch_shapes=[pltpu.SemaphoreType.DMA((10,)), pltpu.SemaphoreType.DMA((10,)), pltpu.SemaphoreType.DMA((1,))],
    )(s1, small)


def _sum_chips(s1, recv, me):
    tiles = SLAB_HALF // SUM_TILE

    def body(me_ref, a_ref, b_ref, o_ref):
        acc = a_ref[0].astype(F32)
        for j in range(3):
            acc = acc + b_ref[j].astype(F32)
        o_ref[...] = acc

    return pl.pallas_call(
        body, name="sum_chips",
        grid_spec=pltpu.PrefetchScalarGridSpec(
            num_scalar_prefetch=1, grid=(tiles,),
            in_specs=[pl.BlockSpec((1, SUM_TILE, LANES), lambda i, me_ref: (me_ref[0], i, 0)),
                      pl.BlockSpec((3, SUM_TILE, LANES), lambda i, me_ref: (0, i, 0))],
            out_specs=pl.BlockSpec((SUM_TILE, LANES), lambda i, me_ref: (i, 0))),
        out_shape=_sds((SLAB_HALF, LANES), F32),
        compiler_params=_params("parallel"),
    )(me.reshape(1), s1, recv)


def _sum_small(small_all):
    def body(a_ref, o_ref):
        acc = a_ref[0]
        for d in range(1, 8):
            acc = acc + a_ref[d]
        o_ref[...] = acc

    return pl.pallas_call(body, name="sum_small", out_shape=_sds(small_all.shape[1:], F32))(small_all)


def _join_halves(s2):
    half = SLAB_HALF

    def body(s2_ref, out_ref, send_sems, recv_sems, loc_sem):
        x, y, c, _ = _chip_peers()
        mine = out_ref.at[pl.ds(c * half, half)]
        theirs = out_ref.at[pl.ds((1 - c) * half, half)]
        local = pltpu.make_async_copy(s2_ref, mine, loc_sem.at[0])
        local.start()
        cp = _remote(s2_ref, mine, send_sems, recv_sems, 0, (x, y, 1 - c))
        cp.start()
        _remote(s2_ref, theirs, send_sems, recv_sems, 0, (x, y, 1 - c)).wait_recv()
        cp.wait_send()
        local.wait()

    return pl.pallas_call(
        body, name="join_halves", in_specs=[HBM_SPEC], out_specs=HBM_SPEC,
        out_shape=_sds((SLAB_ROWS, LANES), s2.dtype),
        scratch_shapes=[pltpu.SemaphoreType.DMA((1,)), pltpu.SemaphoreType.DMA((1,)), pltpu.SemaphoreType.DMA((1,))],
    )(s2)


def _adamw(w, g, m, v, name):
    rows, cols = w.shape
    tm = rows
    for cand in (512, 256, 128, 64, 32, 16, 8):
        if rows % cand == 0 and rows > cand:
            tm = cand
            break

    def body(w_ref, g_ref, m_ref, v_ref, d_ref, nm_ref, nv_ref):
        gv = g_ref[...]
        nm = ADAM_B1 * m_ref[...] + (1.0 - ADAM_B1) * gv
        nv = ADAM_B2 * v_ref[...] + (1.0 - ADAM_B2) * (gv * gv)
        m_hat = nm / (1.0 - ADAM_B1 ** ADAM_STEP)
        v_hat = nv / (1.0 - ADAM_B2 ** ADAM_STEP)
        d_ref[...] = -ADAM_LR * (m_hat / (jnp.sqrt(v_hat) + ADAM_EPS) + ADAM_WD * w_ref[...])
        nm_ref[...] = nm
        nv_ref[...] = nv

    spec = _rows(tm, cols)
    return pl.pallas_call(
        body, name=name, grid=(rows // tm,), in_specs=[spec] * 4, out_specs=[spec] * 3,
        out_shape=[_sds((rows, cols), F32)] * 3, compiler_params=_params("parallel"),
    )(w, g, m, v)


WEIGHTS = ("ml_w_in", "ml_b_if", "ml_head_norm", "ml_w_out", "lru_w_in", "lru_conv_w", "lru_conv_b", "lru_w_gate_a",
           "lru_b_gate_a", "lru_w_gate_x", "lru_b_gate_x", "lru_lambda", "lru_w_out", "norm_pre_mix", "norm_post_mix",
           "norm_pre_ffn", "norm_post_ffn", "ffn_w_gate", "ffn_w_up", "ffn_w_down")
LRU_VECTORS = ("lru_conv_b", "lru_b_gate_a", "lru_b_gate_x", "lru_lambda")
NORMS = ("norm_pre_mix", "norm_post_mix", "norm_pre_ffn", "norm_post_ffn")


def _full_weights(w):
    small = jnp.concatenate([w["lru_conv_w"][0]] + [w[n] for n in LRU_VECTORS], axis=0)
    slab = _pack_slab({n: w[n][0] if w[n].shape[0] == 1 else w[n] for n, _, _ in SLAB_PARTS}, BF16)
    slabs, smalls = _gather_weights(slab, small)
    parts = _unpack_slab(slabs)
    full = {n: jnp.concatenate([parts[n][k] for k in range(N_CHIPS)], axis=ax) for n, _, ax in SLAB_PARTS}
    vec = jnp.concatenate([smalls[k] for k in range(N_CHIPS)], axis=1)
    return dict(
        ml_w_in=jnp.pad(full["ml_w_in"], ((0, 0), (0, ML_IN_PAD - ML_IN))),
        ml_b_if=jnp.pad(w["ml_b_if"], ((0, 0), (0, 128 - 2 * ML_HEADS))),
        ml_head_norm=w["ml_head_norm"], ml_w_out=full["ml_w_out"], lru_w_in=full["lru_w_in"],
        lru_conv_w=vec[0:4], lru_conv_b=vec[4:5], lru_b_gate_a=vec[5:6], lru_b_gate_x=vec[6:7], lru_lambda=vec[7:8],
        lru_w_gate_a=full["lru_w_gate_a"], lru_w_gate_x=full["lru_w_gate_x"], lru_w_out=full["lru_w_out"],
        ffn_w_gu=jnp.concatenate([full["ffn_w_gate"], full["ffn_w_up"]], axis=2), ffn_w_down=full["ffn_w_down"],
        **{n: w[n] for n in NORMS})


def _reduce_gradients(loss, g):
    x, y, c, _ = _chip_peers()
    full = dict(g)
    full["ml_w_in"] = g["ml_w_in"][:, :ML_IN]
    full["ffn_w_gate"] = g["ffn_w_gu"][:, :, :D_FF]
    full["ffn_w_up"] = g["ffn_w_gu"][:, :, D_FF:]

    def shard(name, shape, axis, k):
        return lax.slice_in_dim(full[name], k * shape[axis], (k + 1) * shape[axis], axis=axis)

    gpack = jnp.stack([_pack_slab({n: shard(n, s, ax, k) for n, s, ax in SLAB_PARTS}, BF16) for k in range(N_CHIPS)])
    pad_lanes = lambda a: jnp.pad(a, ((0, 0), (0, LANES - a.shape[1])))
    small = jnp.concatenate(
        [g[n] for n in NORMS] + [pad_lanes(loss), g["ml_head_norm"], pad_lanes(g["ml_b_if"]), g["lru_conv_w"]]
        + [g[n] for n in LRU_VECTORS] + [jnp.zeros((SMALL_ROWS - 19, LANES), F32)], axis=0)
    recv1 = _exchange_halves(gpack)
    s1 = _add_halves(gpack, recv1, c)
    recv2, small_all = _scatter_to_chips(s1, small)
    me = 2 * x + y
    s2 = _sum_chips(s1, recv2, me)
    reduced = _unpack_slab(_join_halves(s2))
    vec = _sum_small(small_all)
    cols = lambda a: lax.dynamic_slice_in_dim(a, me * 256, 256, axis=1)
    for i, n in enumerate(NORMS):
        reduced[n] = vec[2 * i:2 * i + 2]
    reduced["ml_head_norm"] = vec[ROW_HEAD_NORM:ROW_HEAD_NORM + 1]
    reduced["ml_b_if"] = vec[ROW_B_IF:ROW_B_IF + 1, :2 * ML_HEADS]
    reduced["lru_conv_w"] = cols(vec[ROW_LRU:ROW_LRU + 4])
    for i, n in enumerate(LRU_VECTORS):
        reduced[n] = cols(vec[ROW_LRU + 4 + i:ROW_LRU + 5 + i])
    return vec[ROW_LOSS, 0], reduced


def _update(w, grads, m, v):
    delta, new_m, new_v = {}, {}, {}
    for n in WEIGHTS:
        shape = w[n].shape
        as2d = lambda a: a.reshape(-1, shape[-1])
        d, nm, nv = _adamw(as2d(w[n]), as2d(grads[n]), as2d(m[n]), as2d(v[n]), "adamw_" + n)
        delta[n], new_m[n], new_v[n] = d.reshape(shape), nm.reshape(shape), nv.reshape(shape)
    return delta, new_m, new_v


def kernel(x, ml_w_in, ml_b_if, ml_head_norm, ml_w_out, lru_w_in, lru_conv_w, lru_conv_b, lru_w_gate_a, lru_b_gate_a, lru_w_gate_x, lru_b_gate_x, lru_lambda, lru_w_out, norm_pre_mix, norm_post_mix, norm_pre_ffn, norm_post_ffn, ffn_w_gate, ffn_w_up, ffn_w_down, loss_target, m_ml_w_in, m_ml_b_if, m_ml_head_norm, m_ml_w_out, m_lru_w_in, m_lru_conv_w, m_lru_conv_b, m_lru_w_gate_a, m_lru_b_gate_a, m_lru_w_gate_x, m_lru_b_gate_x, m_lru_lambda, m_lru_w_out, m_norm_pre_mix, m_norm_post_mix, m_norm_pre_ffn, m_norm_post_ffn, m_ffn_w_gate, m_ffn_w_up, m_ffn_w_down, v_ml_w_in, v_ml_b_if, v_ml_head_norm, v_ml_w_out, v_lru_w_in, v_lru_conv_w, v_lru_conv_b, v_lru_w_gate_a, v_lru_b_gate_a, v_lru_w_gate_x, v_lru_b_gate_x, v_lru_lambda, v_lru_w_out, v_norm_pre_mix, v_norm_post_mix, v_norm_pre_ffn, v_norm_post_ffn, v_ffn_w_gate, v_ffn_w_up, v_ffn_w_down):
    args = locals()
    w = {n: args[n] for n in WEIGHTS}
    m = {n: args["m_" + n] for n in WEIGHTS}
    v = {n: args["v_" + n] for n in WEIGHTS}
    loss_part, dx, g = _local_step(x[0], loss_target[0], _full_weights(w))
    loss, grads = _reduce_gradients(loss_part, g)
    grads = {n: grads[n].reshape(w[n].shape) for n in WEIGHTS}
    delta, new_m, new_v = _update(w, grads, m, v)
    return (loss, dx[None], *[grads[n] for n in WEIGHTS], *[delta[n] for n in WEIGHTS],
            *[new_m[n] for n in WEIGHTS], *[new_v[n] for n in WEIGHTS])
```

```python
import functools
import math

import jax
import jax.numpy as jnp
from jax import lax
from jax.experimental import pallas as pl
from jax.experimental.pallas import tpu as pltpu

F32 = jnp.float32
BF16 = jnp.bfloat16
MESH = pl.DeviceIdType.MESH

D_MODEL = 1024
D_FF = 2816
ML_HEADS = 8
ML_DK = 64
ML_DV = 128
ML_QK = ML_HEADS * ML_DK
ML_V = ML_HEADS * ML_DV
ML_IN = 2 * ML_QK + 2 * ML_V + 2 * ML_HEADS
ML_IN_PAD = 3200
ML_GATE_BLOCK = (2 * ML_QK + 2 * ML_V) // 128
CHUNK = 64
GATE_CAP = 15.0
ML_M_INIT = -1e30
NEG_BIG = -1e30
LRU_BLOCKS = 4
LRU_BLOCK = 256
CONV_WIDTH = 4
LRU_C = 8.0
EPS = 1e-6
QK_SCALE = ML_DK ** -0.5

ADAM_LR = 0.001
ADAM_B1 = 0.9
ADAM_B2 = 0.999
ADAM_EPS = 1e-08
ADAM_WD = 0.01
ADAM_STEP = 10

N_CHIPS = 4
V7X_VMEM_LIMIT = 56 * 1024 * 1024

NT_DIMS = (((1,), (1,)), ((), ()))
TN_DIMS = (((0,), (0,)), ((), ()))


def _params(*semantics):
    return pltpu.CompilerParams(dimension_semantics=semantics, vmem_limit_bytes=V7X_VMEM_LIMIT)


def _sds(shape, dtype):
    return jax.ShapeDtypeStruct(shape, dtype)


def _full(shape):
    return pl.BlockSpec(shape, lambda *_: (0,) * len(shape))


def _rows(tm, n):
    return pl.BlockSpec((tm, n), lambda i: (i, 0))


def _sigmoid(x):
    return 1.0 / (1.0 + jnp.exp(-x))


def _log_sigmoid(x):
    return jnp.minimum(x, 0.0) - jnp.log1p(jnp.exp(-jnp.abs(x)))


def _rms(x):
    r = lax.rsqrt(jnp.mean(x * x, axis=-1, keepdims=True) + EPS)
    return x * r, r


def _rms_bwd(xhat, r, g, dy):
    dxh = dy * g
    return r * (dxh - xhat * jnp.mean(dxh * xhat, axis=-1, keepdims=True))


def _scan_rows(a, b, reverse=False):
    n = a.shape[0]
    row = lax.broadcasted_iota(jnp.int32, a.shape, 0)
    s = 1
    while s < n:
        if reverse:
            keep = row < n - s
            shift = n - s
        else:
            keep = row >= s
            shift = s
        b = b + a * jnp.where(keep, pltpu.roll(b, shift, 0), 0.0)
        a = a * jnp.where(keep, pltpu.roll(a, shift, 0), 1.0)
        s *= 2
    return a, b


def _cumsum_rows(x, reverse=False):
    n = x.shape[0]
    row = lax.broadcasted_iota(jnp.int32, x.shape, 0)
    s = 1
    while s < n:
        if reverse:
            x = x + jnp.where(row < n - s, pltpu.roll(x, n - s, 0), 0.0)
        else:
            x = x + jnp.where(row >= s, pltpu.roll(x, s, 0), 0.0)
        s *= 2
    return x


def _norm_matmul(h, g, w, tn, name):
    t, d = h.shape
    n = w.shape[1]
    tm = 512

    def body(h_ref, g_ref, w_ref, z_ref, o_ref):
        @pl.when(pl.program_id(1) == 0)
        def _():
            xhat, _ = _rms(h_ref[...])
            z_ref[...] = (xhat * g_ref[...]).astype(BF16)

        o_ref[...] = jnp.dot(z_ref[...], w_ref[...], preferred_element_type=F32)

    return pl.pallas_call(
        body, name=name, grid=(t // tm, n // tn),
        in_specs=[pl.BlockSpec((tm, d), lambda i, j: (i, 0)), pl.BlockSpec((1, d), lambda i, j: (0, 0)),
                  pl.BlockSpec((d, tn), lambda i, j: (0, j))],
        out_specs=[pl.BlockSpec((tm, d), lambda i, j: (i, 0)), pl.BlockSpec((tm, tn), lambda i, j: (i, j))],
        out_shape=[_sds((t, d), BF16), _sds((t, n), F32)],
        compiler_params=_params("parallel", "arbitrary"),
    )(h, g, w)


def _matmul_postnorm(a, w, g, res, swiglu, name):
    t = res.shape[0]
    k, d = w.shape
    tm = 256

    def body(a_ref, w_ref, g_ref, res_ref, y_ref, o_ref):
        if swiglu:
            gate = a_ref[:, :k]
            act = (gate * _sigmoid(gate) * a_ref[:, k:]).astype(BF16)
        else:
            act = a_ref[...]
        y = jnp.dot(act, w_ref[...], preferred_element_type=F32)
        y_ref[...] = y
        yhat, _ = _rms(y)
        o_ref[...] = res_ref[...] + yhat * g_ref[...]

    return pl.pallas_call(
        body, name=name, grid=(t // tm,),
        in_specs=[_rows(tm, a.shape[1]), _full((k, d)), _full((1, d)), _rows(tm, d)],
        out_specs=[_rows(tm, d), _rows(tm, d)],
        out_shape=[_sds((t, d), F32), _sds((t, d), F32)],
        compiler_params=_params("parallel"),
    )(a, w, g, res)


def _loss_head(h, target, name):
    t, d = h.shape
    tm = 512

    def body(h_ref, t_ref, dh_ref, l_ref):
        @pl.when(pl.program_id(0) == 0)
        def _():
            l_ref[...] = jnp.zeros_like(l_ref)

        err = h_ref[...] - t_ref[...]
        dh_ref[...] = err * (1.0 / d)
        part = jnp.sum(jnp.sum(err * err, axis=1, keepdims=True), axis=0, keepdims=True) * (0.5 / d)
        l_ref[...] += jnp.broadcast_to(part, l_ref.shape)

    return pl.pallas_call(
        body, name=name, grid=(t // tm,),
        in_specs=[_rows(tm, d), _rows(tm, d)],
        out_specs=[_rows(tm, d), _full((1, 128))],
        out_shape=[_sds((t, d), F32), _sds((1, 128), F32)],
        compiler_params=_params("arbitrary"),
    )(h, target)


def _bwd_out(dout, y, g, w, gu, name):
    t, d = dout.shape
    k = w.shape[0]
    ffn = gu is not None
    tm = 128 if ffn else 256

    def body(*refs):
        if ffn:
            dout_ref, y_ref, g_ref, w_ref, gu_ref, dy_ref, dact_ref, act_ref, dg_ref = refs
        else:
            dout_ref, y_ref, g_ref, w_ref, dy_ref, dact_ref, dg_ref = refs

        @pl.when(pl.program_id(0) == 0)
        def _():
            dg_ref[...] = jnp.zeros_like(dg_ref)

        do = dout_ref[...]
        yhat, r = _rms(y_ref[...])
        dg_ref[...] += jnp.sum(do * yhat, axis=0, keepdims=True)
        dy = _rms_bwd(yhat, r, g_ref[...], do).astype(BF16)
        dy_ref[...] = dy
        da = lax.dot_general(dy, w_ref[...], NT_DIMS, preferred_element_type=F32)
        if ffn:
            gate = gu_ref[:, :k]
            up = gu_ref[:, k:]
            sg = _sigmoid(gate)
            silu = gate * sg
            act_ref[...] = (silu * up).astype(BF16)
            dact_ref[:, :k] = (da * up * (sg * (1.0 + gate * (1.0 - sg)))).astype(BF16)
            dact_ref[:, k:] = (da * silu).astype(BF16)
        else:
            dact_ref[...] = da

    in_specs = [_rows(tm, d), _rows(tm, d), _full((1, d)), _full((k, d))]
    args = [dout, y, g, w]
    if ffn:
        in_specs.append(_rows(tm, 2 * k))
        args.append(gu)
        out_specs = [_rows(tm, d), _rows(tm, 2 * k), _rows(tm, k), _full((1, d))]
        out_shape = [_sds((t, d), BF16), _sds((t, 2 * k), BF16), _sds((t, k), BF16), _sds((1, d), F32)]
    else:
        out_specs = [_rows(tm, d), _rows(tm, k), _full((1, d))]
        out_shape = [_sds((t, d), BF16), _sds((t, k), F32), _sds((1, d), F32)]
    return pl.pallas_call(
        body, name=name, grid=(t // tm,), in_specs=in_specs, out_specs=out_specs, out_shape=out_shape,
        compiler_params=_params("arbitrary"),
    )(*args)


def _bwd_in(dp, w, h, g, dout, name):
    t, d = h.shape
    n = w.shape[1]
    tm = 256

    def body(dp_ref, w_ref, h_ref, g_ref, dout_ref, dh_ref, dg_ref):
        @pl.when(pl.program_id(0) == 0)
        def _():
            dg_ref[...] = jnp.zeros_like(dg_ref)

        dz = lax.dot_general(dp_ref[...], w_ref[...], NT_DIMS, preferred_element_type=F32)
        hhat, r = _rms(h_ref[...])
        dg_ref[...] += jnp.sum(dz * hhat, axis=0, keepdims=True)
        dh_ref[...] = dout_ref[...] + _rms_bwd(hhat, r, g_ref[...], dz)

    return pl.pallas_call(
        body, name=name, grid=(t // tm,),
        in_specs=[_rows(tm, n), _full((d, n)), _rows(tm, d), _full((1, d)), _rows(tm, d)],
        out_specs=[_rows(tm, d), _full((1, d))],
        out_shape=[_sds((t, d), F32), _sds((1, d), F32)],
        compiler_params=_params("arbitrary"),
    )(dp, w, h, g, dout)


def _matmul_tn(a, b, tile_a, tile, name):
    t, ka = a.shape
    nb = b.shape[1]

    def body(a_ref, b_ref, o_ref):
        o_ref[...] = lax.dot_general(a_ref[...], b_ref[...], TN_DIMS, preferred_element_type=F32).astype(BF16)

    if tile_a:
        grid = (ka // tile,)
        in_specs = [pl.BlockSpec((t, tile), lambda i: (0, i)), _full((t, nb))]
        out_specs = pl.BlockSpec((tile, nb), lambda i: (i, 0))
    else:
        grid = (nb // tile,)
        in_specs = [_full((t, ka)), pl.BlockSpec((t, tile), lambda i: (0, i))]
        out_specs = pl.BlockSpec((ka, tile), lambda i: (0, i))
    return pl.pallas_call(
        body, name=name, grid=grid, in_specs=in_specs, out_specs=out_specs,
        out_shape=_sds((ka, nb), BF16), compiler_params=_params("parallel"),
    )(a, b)


def _ml_gate_prep(gt, bif):
    th = jnp.tanh((gt + bif) / GATE_CAP)
    act = GATE_CAP * th
    cum = _cumsum_rows(_log_sigmoid(act))
    lane = lax.broadcasted_iota(jnp.int32, gt.shape, 1)
    x = jnp.where(lane < ML_HEADS, act, cum)
    return x, x.T, th, act


def _ml_head_fwd(h, q, k, v, x, xt, c_in, n_in, m_in, causal):
    qh = q[:, ML_DK * h:ML_DK * (h + 1)]
    kh = k[:, ML_DK * h:ML_DK * (h + 1)]
    vh = v[:, ML_DV * h:ML_DV * (h + 1)]
    s = lax.dot_general(qh, kh, NT_DIMS, preferred_element_type=F32) * QK_SCALE
    bcol = x[:, ML_HEADS + h:ML_HEADS + h + 1]
    licol = x[:, h:h + 1]
    brow = xt[ML_HEADS + h:ML_HEADS + h + 1, :]
    lirow = xt[h:h + 1, :]
    dmat = jnp.where(causal, bcol - brow + lirow, NEG_BIG)
    inter = bcol + m_in
    mt = jnp.maximum(jnp.max(dmat, axis=1, keepdims=True), inter)
    wt = jnp.exp(dmat - mt)
    p = wt * s
    winter = jnp.exp(inter - mt)
    cb = c_in.astype(BF16)
    qc = jnp.dot(qh, cb, preferred_element_type=F32) * QK_SCALE
    qf = qh.astype(F32)
    qn = jnp.sum(qf * n_in, axis=1, keepdims=True) * QK_SCALE
    den = jnp.sum(p, axis=1, keepdims=True) + winter * qn
    emt = jnp.exp(-mt)
    nrm = jnp.maximum(jnp.abs(den), emt)
    gtot = bcol[CHUNK - 1:CHUNK, :]
    a_col = gtot - bcol + licol
    m_new = jnp.maximum(gtot + m_in, jnp.max(a_col, axis=0, keepdims=True))
    decay = jnp.exp(gtot + m_in - m_new)
    wkf = jnp.exp(a_col - m_new)
    kw = kh.astype(F32) * wkf
    return dict(qh=qh, kh=kh, vh=vh, wt=wt, p=p, winter=winter, cb=cb, qc=qc, qf=qf, qn=qn, den=den,
                emt=emt, nrm=nrm, m_new=m_new, decay=decay, wkf=wkf, kw=kw)


def _ml_specs(nc, rev):
    def at(col):
        if rev:
            return lambda i: (nc - 1 - i, col)
        return lambda i: (i, col)

    return [pl.BlockSpec((CHUNK, ML_QK), at(0)), pl.BlockSpec((CHUNK, ML_QK), at(1)),
            pl.BlockSpec((CHUNK, ML_V), at(1)), pl.BlockSpec((CHUNK, ML_V), at(2)),
            pl.BlockSpec((CHUNK, 128), at(ML_GATE_BLOCK))]


def _mlstm_fwd(proj, bif, hn):
    t = proj.shape[0]
    nc = t // CHUNK

    def body(q_ref, k_ref, v_ref, o_ref, gt_ref, bif_ref, hn_ref, y_ref, hs_ref, cst_ref, nst_ref, mst_ref,
             c_sc, n_sc, m_sc):
        @pl.when(pl.program_id(0) == 0)
        def _():
            c_sc[...] = jnp.zeros_like(c_sc)
            n_sc[...] = jnp.zeros_like(n_sc)
            m_sc[...] = jnp.full_like(m_sc, ML_M_INIT)

        cst_ref[0] = c_sc[...]
        nst_ref[0] = n_sc[...]
        mst_ref[0] = m_sc[...]
        x, xt, _, _ = _ml_gate_prep(gt_ref[...], bif_ref[...])
        q = q_ref[...].astype(BF16)
        k = k_ref[...].astype(BF16)
        v = v_ref[...].astype(BF16)
        causal = (lax.broadcasted_iota(jnp.int32, (CHUNK, CHUNK), 0)
                  >= lax.broadcasted_iota(jnp.int32, (CHUNK, CHUNK), 1))
        for h in range(ML_HEADS):
            vs = slice(ML_DV * h, ML_DV * (h + 1))
            c_in = c_sc[ML_DK * h:ML_DK * (h + 1), :]
            n_in = n_sc[h:h + 1, :]
            m_in = m_sc[h:h + 1, 0:1]
            f = _ml_head_fwd(h, q, k, v, x, xt, c_in, n_in, m_in, causal)
            num = jnp.dot(f["p"].astype(BF16), f["vh"], preferred_element_type=F32) + f["winter"] * f["qc"]
            hh = num / f["nrm"]
            hs_ref[:, vs] = hh
            hhat, _ = _rms(hh)
            y_ref[:, vs] = (hhat * hn_ref[:, vs] * _sigmoid(o_ref[:, vs])).astype(BF16)
            c_sc[ML_DK * h:ML_DK * (h + 1), :] = f["decay"] * c_in + lax.dot_general(
                f["kw"].astype(BF16), f["vh"], TN_DIMS, preferred_element_type=F32)
            n_sc[h:h + 1, :] = f["decay"] * n_in + jnp.sum(f["kw"], axis=0, keepdims=True)
            m_sc[h:h + 1, :] = jnp.broadcast_to(f["m_new"], (1, 128))

    return pl.pallas_call(
        body, name="mlstm_fwd", grid=(nc,),
        in_specs=_ml_specs(nc, False) + [_full((1, 128)), _full((1, ML_V))],
        out_specs=[_rows(CHUNK, ML_V), _rows(CHUNK, ML_V),
                   pl.BlockSpec((1, ML_HEADS * ML_DK, ML_DV), lambda i: (i, 0, 0)),
                   pl.BlockSpec((1, ML_HEADS, ML_DK), lambda i: (i, 0, 0)),
                   pl.BlockSpec((1, ML_HEADS, 128), lambda i: (i, 0, 0))],
        out_shape=[_sds((t, ML_V), BF16), _sds((t, ML_V), F32),
                   _sds((nc, ML_HEADS * ML_DK, ML_DV), F32), _sds((nc, ML_HEADS, ML_DK), F32),
                   _sds((nc, ML_HEADS, 128), F32)],
        scratch_shapes=[pltpu.VMEM((ML_HEADS * ML_DK, ML_DV), F32), pltpu.VMEM((ML_HEADS, ML_DK), F32),
                        pltpu.VMEM((ML_HEADS, 128), F32)],
        compiler_params=_params("arbitrary"),
    )(proj, proj, proj, proj, proj, bif, hn)


def _mlstm_bwd(proj, bif, hn, hs, cst, nst, mst, dy):
    t = proj.shape[0]
    nc = t // CHUNK

    def body(q_ref, k_ref, v_ref, o_ref, gt_ref, bif_ref, hn_ref, hs_ref, cst_ref, nst_ref, mst_ref, dy_ref,
             dp_ref, dhn_ref, dbif_ref, dc_sc, dn_sc):
        @pl.when(pl.program_id(0) == 0)
        def _():
            dc_sc[...] = jnp.zeros_like(dc_sc)
            dn_sc[...] = jnp.zeros_like(dn_sc)
            dhn_ref[...] = jnp.zeros_like(dhn_ref)
            dbif_ref[...] = jnp.zeros_like(dbif_ref)

        x, xt, th, act = _ml_gate_prep(gt_ref[...], bif_ref[...])
        q = q_ref[...].astype(BF16)
        k = k_ref[...].astype(BF16)
        v = v_ref[...].astype(BF16)
        causal = (lax.broadcasted_iota(jnp.int32, (CHUNK, CHUNK), 0)
                  >= lax.broadcasted_iota(jnp.int32, (CHUNK, CHUNK), 1))
        lane = lax.broadcasted_iota(jnp.int32, (CHUNK, 128), 1)
        row = lax.broadcasted_iota(jnp.int32, (CHUNK, 1), 0)
        dx = jnp.zeros((CHUNK, 128), F32)
        for h in range(ML_HEADS):
            ks = slice(ML_DK * h, ML_DK * (h + 1))
            vs = slice(ML_DV * h, ML_DV * (h + 1))
            c_in = cst_ref[0, ks, :]
            n_in = nst_ref[0, h:h + 1, :]
            m_in = mst_ref[0, h:h + 1, 0:1]
            f = _ml_head_fwd(h, q, k, v, x, xt, c_in, n_in, m_in, causal)
            qh, kh, vh, p, winter = f["qh"], f["kh"], f["vh"], f["p"], f["winter"]
            hh = hs_ref[:, vs]
            hhat, r = _rms(hh)
            hn_h = hn_ref[:, vs]
            sg = _sigmoid(o_ref[:, vs])
            dyh = dy_ref[:, vs]
            dyn = dyh * sg
            dp_ref[:, 2 * ML_QK + ML_V + ML_DV * h:2 * ML_QK + ML_V + ML_DV * (h + 1)] = (
                dyh * hhat * hn_h * sg * (1.0 - sg)).astype(BF16)
            dhn_ref[:, vs] += jnp.sum(dyn * hhat, axis=0, keepdims=True)
            dh = _rms_bwd(hhat, r, hn_h, dyn)
            inv = 1.0 / f["nrm"]
            dnum = dh * inv
            dnrm = -jnp.sum(dh * hh, axis=1, keepdims=True) * inv
            dden = jnp.where(jnp.abs(f["den"]) > f["emt"], dnrm * jnp.sign(f["den"]), 0.0)
            dnb = dnum.astype(BF16)
            rmat = lax.dot_general(dnb, vh, NT_DIMS, preferred_element_type=F32) + dden
            ds = (f["wt"] * rmat).astype(BF16)
            qmat = p * rmat
            dv = lax.dot_general(p.astype(BF16), dnb, TN_DIMS, preferred_element_type=F32)
            dqs = jnp.dot(ds, kh, preferred_element_type=F32) + winter * (
                lax.dot_general(dnb, f["cb"], NT_DIMS, preferred_element_type=F32) + dden * n_in)
            dk = lax.dot_general(ds, qh, TN_DIMS, preferred_element_type=F32) * QK_SCALE
            dinter = (jnp.sum(f["qc"] * dnum, axis=1, keepdims=True) + f["qn"] * dden) * winter
            wq = winter * f["qf"] * QK_SCALE
            dc_loc = lax.dot_general(wq.astype(BF16), dnb, TN_DIMS, preferred_element_type=F32)
            dn_loc = jnp.sum(wq * dden, axis=0, keepdims=True)
            cs_q = jnp.sum(qmat.T, axis=1, keepdims=True)
            db = jnp.sum(qmat, axis=1, keepdims=True) + dinter - cs_q
            dli = cs_q
            dcn = dc_sc[ks, :]
            dnn = dn_sc[h:h + 1, :]
            dcnb = dcn.astype(BF16)
            ddecay = (jnp.sum(jnp.sum(dcn * c_in, axis=1, keepdims=True), axis=0, keepdims=True)
                      + jnp.sum(dnn * n_in, axis=1, keepdims=True))
            dkw = lax.dot_general(vh, dcnb, NT_DIMS, preferred_element_type=F32) + dnn
            dk = dk + f["wkf"] * dkw
            da = jnp.sum(dkw * kh.astype(F32), axis=1, keepdims=True) * f["wkf"]
            dv = dv + jnp.dot(f["kw"].astype(BF16), dcnb, preferred_element_type=F32)
            dgtot = jnp.sum(da, axis=0, keepdims=True) + ddecay * f["decay"]
            db = db - da + jnp.where(row == CHUNK - 1, dgtot, 0.0)
            dli = dli + da
            dc_sc[ks, :] = f["decay"] * dcn + dc_loc
            dn_sc[h:h + 1, :] = f["decay"] * dnn + dn_loc
            dp_ref[:, ks] = (dqs * QK_SCALE).astype(BF16)
            dp_ref[:, ML_QK + ML_DK * h:ML_QK + ML_DK * (h + 1)] = dk.astype(BF16)
            dp_ref[:, 2 * ML_QK + ML_DV * h:2 * ML_QK + ML_DV * (h + 1)] = dv.astype(BF16)
            dx = jnp.where(lane == h, dli, dx)
            dx = jnp.where(lane == ML_HEADS + h, db, dx)
        dlf = _cumsum_rows(dx, reverse=True)
        dact = jnp.where(lane < ML_HEADS, dx, dlf * _sigmoid(-act))
        dz = dact * (1.0 - th * th)
        dp_ref[:, 2 * ML_QK + 2 * ML_V:] = dz.astype(BF16)
        dbif_ref[...] += jnp.sum(dz, axis=0, keepdims=True)

    rev = lambda i: (nc - 1 - i, 0)
    rev3 = lambda i: (nc - 1 - i, 0, 0)
    return pl.pallas_call(
        body, name="mlstm_bwd", grid=(nc,),
        in_specs=_ml_specs(nc, True) + [
            _full((1, 128)), _full((1, ML_V)), pl.BlockSpec((CHUNK, ML_V), rev),
            pl.BlockSpec((1, ML_HEADS * ML_DK, ML_DV), rev3), pl.BlockSpec((1, ML_HEADS, ML_DK), rev3),
            pl.BlockSpec((1, ML_HEADS, 128), rev3), pl.BlockSpec((CHUNK, ML_V), rev)],
        out_specs=[pl.BlockSpec((CHUNK, ML_IN_PAD), rev), _full((1, ML_V)), _full((1, 128))],
        out_shape=[_sds((t, ML_IN_PAD), BF16), _sds((1, ML_V), F32), _sds((1, 128), F32)],
        scratch_shapes=[pltpu.VMEM((ML_HEADS * ML_DK, ML_DV), F32), pltpu.VMEM((ML_HEADS, ML_DK), F32)],
        compiler_params=_params("arbitrary"),
    )(proj, proj, proj, proj, proj, bif, hn, hs, cst, nst, mst, dy)


LRU_TM = 256
GELU_K = math.sqrt(2.0 / math.pi)
GELU_C = 0.044715


def _gelu(x):
    th = jnp.tanh(GELU_K * (x + GELU_C * x * x * x))
    return 0.5 * x * (1.0 + th), th


def _neg_expm1(x):
    series = -x * (1.0 + x * (0.5 + x * (1.0 / 6.0 + x * (1.0 / 24.0))))
    return jnp.where(x > -0.05, series, 1.0 - jnp.exp(x))


def _block_diag_dot(a, w_ref, dims):
    parts = [lax.dot_general(a[:, LRU_BLOCK * n:LRU_BLOCK * (n + 1)], w_ref[n], dims, preferred_element_type=F32)
             for n in range(LRU_BLOCKS)]
    return jnp.concatenate(parts, axis=1)


def _lru_gates(u, r, ig, lam):
    ls = _log_sigmoid(lam)
    la = LRU_C * r * ls
    a = jnp.exp(la)
    em = _neg_expm1(2.0 * la)
    mult = jnp.sqrt(em)
    return ls, a, em, mult


def _lru_fwd(proj, cw, cb, wa, ba, wx, bx, lam):
    t = proj.shape[0]
    w = D_MODEL
    tm = LRU_TM

    def body(gb_ref, up_ref, cw_ref, cb_ref, wa_ref, ba_ref, wx_ref, bx_ref, lam_ref,
             y_ref, u_ref, r_ref, i_ref, h_ref, tail_sc, hprev_sc):
        @pl.when(pl.program_id(0) == 0)
        def _():
            tail_sc[...] = jnp.zeros_like(tail_sc)
            hprev_sc[...] = jnp.zeros_like(hprev_sc)

        up = up_ref[...]
        ext = jnp.concatenate([tail_sc[...], up], axis=0)
        u = cb_ref[...] + cw_ref[CONV_WIDTH - 1:CONV_WIDTH, :] * up
        for s in range(1, CONV_WIDTH):
            u = u + cw_ref[CONV_WIDTH - 1 - s:CONV_WIDTH - s, :] * pltpu.roll(ext, s, 0)[8:8 + tm]
        tail_sc[...] = up[tm - 8:tm]
        ub = u.astype(BF16)
        r = _sigmoid(_block_diag_dot(ub, wa_ref, (((1,), (0,)), ((), ()))) + ba_ref[...])
        ig = _sigmoid(_block_diag_dot(ub, wx_ref, (((1,), (0,)), ((), ()))) + bx_ref[...])
        _, a, _, mult = _lru_gates(u, r, ig, lam_ref[...])
        acum, hloc = _scan_rows(a, mult * ig * u)
        h = hloc + acum * hprev_sc[0:1, :]
        hprev_sc[0:1, :] = h[tm - 1:tm]
        u_ref[...] = u
        r_ref[...] = r
        i_ref[...] = ig
        h_ref[...] = h
        gel, _ = _gelu(gb_ref[...])
        y_ref[...] = (h * gel).astype(BF16)

    vec = _full((1, w))
    wspec = _full((LRU_BLOCKS, LRU_BLOCK, LRU_BLOCK))
    return pl.pallas_call(
        body, name="lru_fwd", grid=(t // tm,),
        in_specs=[pl.BlockSpec((tm, w), lambda i: (i, 0)), pl.BlockSpec((tm, w), lambda i: (i, 1)),
                  _full((CONV_WIDTH, w)), vec, wspec, vec, wspec, vec, vec],
        out_specs=[_rows(tm, w)] * 5,
        out_shape=[_sds((t, w), BF16)] + [_sds((t, w), F32)] * 4,
        scratch_shapes=[pltpu.VMEM((8, w), F32), pltpu.VMEM((8, w), F32)],
        compiler_params=_params("arbitrary"),
    )(proj, proj, cw, cb, wa, ba, wx, bx, lam)


def _lru_bwd(proj, cw, wa, wx, lam, u, r, ig, h, dy):
    t = proj.shape[0]
    w = D_MODEL
    tm = LRU_TM
    nt = t // tm

    def body(gb_ref, up_ref, cw_ref, wa_ref, wx_ref, lam_ref, u_ref, r_ref, i_ref, h_ref, hp_ref, dy_ref,
             dp_ref, dcw_ref, dcb_ref, dwa_ref, dba_ref, dwx_ref, dbx_ref, dlam_ref, carry_sc, dutail_sc, dls_sc):
        step = pl.program_id(0)

        @pl.when(step == 0)
        def _():
            carry_sc[...] = jnp.zeros_like(carry_sc)
            dutail_sc[...] = jnp.zeros_like(dutail_sc)
            dls_sc[...] = jnp.zeros_like(dls_sc)
            for ref in (dcw_ref, dcb_ref, dwa_ref, dba_ref, dwx_ref, dbx_ref):
                ref[...] = jnp.zeros_like(ref)

        row = lax.broadcasted_iota(jnp.int32, (tm, w), 0)
        u_t, r_t, i_t, h_t = u_ref[...], r_ref[...], i_ref[...], h_ref[...]
        ls, a, em, mult = _lru_gates(u_t, r_t, i_t, lam_ref[...])
        gb = gb_ref[...]
        gel, th = _gelu(gb)
        dyv = dy_ref[...]
        dgb = dyv * h_t * (0.5 * (1.0 + th) + 0.5 * gb * (1.0 - th * th) * GELU_K * (1.0 + 3.0 * GELU_C * gb * gb))
        dh = jnp.where(row == tm - 1, dyv * gel + carry_sc[0:1, :], dyv * gel)
        a_next = jnp.where(row < tm - 1, pltpu.roll(a, tm - 1, 0), 0.0)
        _, g = _scan_rows(a_next, dh, reverse=True)
        carry_sc[0:1, :] = a[0:1] * g[0:1]
        has_prev = jnp.where(step == nt - 1, 0.0, 1.0)
        h_prev = jnp.where(row >= 1, pltpu.roll(h_t, 1, 0), hp_ref[7:8, :] * has_prev)
        dmult = g * i_t * u_t
        dig = g * mult * u_t
        du = g * mult * i_t
        dla = g * h_prev * a - dmult * (1.0 - em) / mult
        dls_sc[0:1, :] += jnp.sum(dla * r_t, axis=0, keepdims=True) * LRU_C
        dpa = dla * (LRU_C * ls) * r_t * (1.0 - r_t)
        dpx = dig * i_t * (1.0 - i_t)
        dba_ref[...] += jnp.sum(dpa, axis=0, keepdims=True)
        dbx_ref[...] += jnp.sum(dpx, axis=0, keepdims=True)
        ub = u_t.astype(BF16)
        dpab = dpa.astype(BF16)
        dpxb = dpx.astype(BF16)
        for n in range(LRU_BLOCKS):
            cs = slice(LRU_BLOCK * n, LRU_BLOCK * (n + 1))
            dwa_ref[n] += lax.dot_general(ub[:, cs], dpab[:, cs], TN_DIMS, preferred_element_type=F32)
            dwx_ref[n] += lax.dot_general(ub[:, cs], dpxb[:, cs], TN_DIMS, preferred_element_type=F32)
        du = du + _block_diag_dot(dpab, wa_ref, NT_DIMS) + _block_diag_dot(dpxb, wx_ref, NT_DIMS)
        dcb_ref[...] += jnp.sum(du, axis=0, keepdims=True)
        ext = jnp.concatenate([du, dutail_sc[...]], axis=0)
        up = up_ref[...]
        dup = cw_ref[CONV_WIDTH - 1:CONV_WIDTH, :] * du
        dcw_ref[CONV_WIDTH - 1:CONV_WIDTH, :] += jnp.sum(up * du, axis=0, keepdims=True)
        for s in range(1, CONV_WIDTH):
            du_s = pltpu.roll(ext, tm + 8 - s, 0)[0:tm]
            dup = dup + cw_ref[CONV_WIDTH - 1 - s:CONV_WIDTH - s, :] * du_s
            dcw_ref[CONV_WIDTH - 1 - s:CONV_WIDTH - s, :] += jnp.sum(up * du_s, axis=0, keepdims=True)
        dutail_sc[...] = du[0:8]
        dp_ref[:, :w] = dgb.astype(BF16)
        dp_ref[:, w:] = dup.astype(BF16)

        @pl.when(step == nt - 1)
        def _():
            dlam_ref[...] = dls_sc[0:1, :] * _sigmoid(-lam_ref[...])

    rev = lambda col: (lambda i: (nt - 1 - i, col))
    vec = _full((1, w))
    wspec = _full((LRU_BLOCKS, LRU_BLOCK, LRU_BLOCK))
    tile = pl.BlockSpec((tm, w), rev(0))
    prev8 = pl.BlockSpec((8, w), lambda i: (jnp.maximum((nt - 1 - i) * (tm // 8) - 1, 0), 0))
    return pl.pallas_call(
        body, name="lru_bwd", grid=(nt,),
        in_specs=[tile, pl.BlockSpec((tm, w), rev(1)), _full((CONV_WIDTH, w)), wspec, wspec, vec,
                  tile, tile, tile, tile, prev8, tile],
        out_specs=[pl.BlockSpec((tm, 2 * w), rev(0)), _full((CONV_WIDTH, w)), vec, wspec, vec, wspec, vec, vec],
        out_shape=[_sds((t, 2 * w), BF16), _sds((CONV_WIDTH, w), F32), _sds((1, w), F32),
                   _sds((LRU_BLOCKS, LRU_BLOCK, LRU_BLOCK), F32), _sds((1, w), F32),
                   _sds((LRU_BLOCKS, LRU_BLOCK, LRU_BLOCK), F32), _sds((1, w), F32), _sds((1, w), F32)],
        scratch_shapes=[pltpu.VMEM((8, w), F32), pltpu.VMEM((8, w), F32), pltpu.VMEM((8, w), F32)],
        compiler_params=_params("arbitrary"),
    )(proj, proj, cw, wa, wx, lam, u, r, ig, h, h, dy)


def _ffn_fwd(h, g_pre, w_gu, w_down, g_post, layer):
    z, gu = _norm_matmul(h, g_pre, w_gu, 512, f"ffn{layer}_up")
    y, out = _matmul_postnorm(gu, w_down, g_post, h, True, f"ffn{layer}_down")
    return out, (h, z, gu, y)


def _ffn_bwd(dout, saved, g_pre, w_gu, w_down, g_post, layer):
    h, z, gu, y = saved
    dy, dgu, act, dg_post = _bwd_out(dout, y, g_post, w_down, gu, f"ffn{layer}_bwd_down")
    dw_down = _matmul_tn(act, dy, True, 256, f"ffn{layer}_dw_down")
    dw_gu = _matmul_tn(z, dgu, False, 512, f"ffn{layer}_dw_up")
    dh, dg_pre = _bwd_in(dgu, w_gu, h, g_pre, dout, f"ffn{layer}_bwd_up")
    return dh, dw_gu, dw_down, dg_pre, dg_post


def _local_step(x, target, wts):
    npm, nqm, npf, nqf = wts["norm_pre_mix"], wts["norm_post_mix"], wts["norm_pre_ffn"], wts["norm_post_ffn"]
    row = lambda a, i: a[i:i + 1]
    z0, proj0 = _norm_matmul(x, row(npm, 0), wts["ml_w_in"], 640, "ml_in")
    y0, hs, cst, nst, mst = _mlstm_fwd(proj0, wts["ml_b_if"], wts["ml_head_norm"])
    ymix0, h1 = _matmul_postnorm(y0, wts["ml_w_out"], row(nqm, 0), x, False, "ml_out")
    h2, ffn0 = _ffn_fwd(h1, row(npf, 0), wts["ffn_w_gu"][0], wts["ffn_w_down"][0], row(nqf, 0), 0)
    z1, proj1 = _norm_matmul(h2, row(npm, 1), wts["lru_w_in"], 512, "lru_in")
    y1, u, r, ig, hl = _lru_fwd(proj1, wts["lru_conv_w"], wts["lru_conv_b"], wts["lru_w_gate_a"], wts["lru_b_gate_a"],
                                wts["lru_w_gate_x"], wts["lru_b_gate_x"], wts["lru_lambda"])
    ymix1, h3 = _matmul_postnorm(y1, wts["lru_w_out"], row(nqm, 1), h2, False, "lru_out")
    h4, ffn1 = _ffn_fwd(h3, row(npf, 1), wts["ffn_w_gu"][1], wts["ffn_w_down"][1], row(nqf, 1), 1)
    dh4, loss = _loss_head(h4, target, "loss_head")
    g = {}
    dh3, dgu1, dwd1, dpf1, dqf1 = _ffn_bwd(dh4, ffn1, row(npf, 1), wts["ffn_w_gu"][1], wts["ffn_w_down"][1], row(nqf, 1), 1)
    dymix1, dy1, dqm1 = _bwd_out(dh3, ymix1, row(nqm, 1), wts["lru_w_out"], None, "lru_bwd_out")
    g["lru_w_out"] = _matmul_tn(y1, dymix1, False, 256, "lru_dw_out")
    (dproj1, g["lru_conv_w"], g["lru_conv_b"], g["lru_w_gate_a"], g["lru_b_gate_a"], g["lru_w_gate_x"],
     g["lru_b_gate_x"], g["lru_lambda"]) = _lru_bwd(proj1, wts["lru_conv_w"], wts["lru_w_gate_a"], wts["lru_w_gate_x"],
                                                     wts["lru_lambda"], u, r, ig, hl, dy1)
    g["lru_w_in"] = _matmul_tn(z1, dproj1, False, 512, "lru_dw_in")
    dh2, dpm1 = _bwd_in(dproj1, wts["lru_w_in"], h2, row(npm, 1), dh3, "lru_bwd_in")
    dh1, dgu0, dwd0, dpf0, dqf0 = _ffn_bwd(dh2, ffn0, row(npf, 0), wts["ffn_w_gu"][0], wts["ffn_w_down"][0], row(nqf, 0), 0)
    dymix0, dy0, dqm0 = _bwd_out(dh1, ymix0, row(nqm, 0), wts["ml_w_out"], None, "ml_bwd_out")
    g["ml_w_out"] = _matmul_tn(y0, dymix0, False, 256, "ml_dw_out")
    dproj0, g["ml_head_norm"], g["ml_b_if"] = _mlstm_bwd(proj0, wts["ml_b_if"], wts["ml_head_norm"], hs, cst, nst, mst, dy0)
    g["ml_w_in"] = _matmul_tn(z0, dproj0, False, 640, "ml_dw_in")
    dx, dpm0 = _bwd_in(dproj0, wts["ml_w_in"], x, row(npm, 0), dh1, "ml_bwd_in")
    g["ffn_w_gu"] = jnp.stack([dgu0, dgu1])
    g["ffn_w_down"] = jnp.stack([dwd0, dwd1])
    g["norm_pre_mix"] = jnp.concatenate([dpm0, dpm1], axis=0)
    g["norm_post_mix"] = jnp.concatenate([dqm0, dqm1], axis=0)
    g["norm_pre_ffn"] = jnp.concatenate([dpf0, dpf1], axis=0)
    g["norm_post_ffn"] = jnp.concatenate([dqf0, dqf1], axis=0)
    return loss, dx, g


LANES = 1024
HALF_FFN = D_FF // N_CHIPS
PARTS = (
    ("ml_w_in", 1024, ML_IN // N_CHIPS),
    ("ml_w_out", 256, 1024),
    ("lru_w_in", 1024, 512),
    ("lru_gates", 2 * LRU_BLOCKS * 64, LRU_BLOCK),
    ("lru_w_out", 256, 1024),
    ("ffn_gu", 2 * 1024, 2 * HALF_FFN),
    ("ffn_down", 2 * HALF_FFN, 1024),
)
N_PARTS = len(PARTS)

SMALL_ROWS = 24
ROW_LOSS, ROW_HEAD_NORM, ROW_B_IF, ROW_LRU = 8, 9, 10, 11


def _row_tile(rows, cols, itemsize, budget=3 << 19):
    best = 16
    for t in range(16, rows + 1, 16):
        if rows % t == 0 and t * cols * itemsize <= budget:
            best = t
    return best


def _chip_peers():
    x, y, c = lax.axis_index("x"), lax.axis_index("y"), lax.axis_index("c")
    return x, y, c, [(1 - x, y), (x, 1 - y), (1 - x, 1 - y)]


HBM_SPEC = pl.BlockSpec(memory_space=pltpu.HBM)


def _remote(src, dst, send_sems, recv_sems, k, to):
    return pltpu.make_async_remote_copy(src_ref=src, dst_ref=dst, send_sem=send_sems.at[k], recv_sem=recv_sems.at[k],
                                        device_id=to, device_id_type=MESH)


def _gather_weights(parts, small):
    per_part = 7

    def body(*refs):
        part_refs, small_ref = refs[:N_PARTS], refs[N_PARTS]
        out_refs, outs_ref = refs[N_PARTS + 1:2 * N_PARTS + 1], refs[2 * N_PARTS + 1]
        send_sems, recv_sems, loc_sem = refs[2 * N_PARTS + 2:]
        x, y, c, chips = _chip_peers()
        me = 2 * x + y
        sib = (x, y, 1 - c)
        copy = functools.partial(_remote, send_sems=send_sems, recv_sems=recv_sems)
        local = pltpu.make_async_copy(small_ref, outs_ref.at[me], loc_sem.at[0])
        local.start()
        sent = []
        for p, (_, rows, _) in enumerate(PARTS):
            mine = pl.ds(c * (rows // 2), rows // 2)
            for j, (cx, cy) in enumerate(chips):
                sent.append(copy(part_refs[p].at[mine], out_refs[p].at[me, mine], k=per_part * p + j, to=(cx, cy, c)))
        for j, (cx, cy) in enumerate(chips):
            sent.append(copy(small_ref, outs_ref.at[me], k=per_part * N_PARTS + j, to=(cx, cy, c)))
        for p in range(N_PARTS):
            sent.append(copy(part_refs[p], out_refs[p].at[me], k=per_part * p + 6, to=sib))
        for cp in sent:
            cp.start()
        for p, (_, rows, _) in enumerate(PARTS):
            mine = pl.ds(c * (rows // 2), rows // 2)
            for j, (cx, cy) in enumerate(chips):
                land = out_refs[p].at[2 * cx + cy, mine]
                copy(land, land, k=per_part * p + j, to=sib).wait_recv()
                sent.append(copy(land, land, k=per_part * p + 3 + j, to=sib))
                sent[-1].start()
        for p, (_, rows, _) in enumerate(PARTS):
            theirs = pl.ds((1 - c) * (rows // 2), rows // 2)
            copy(part_refs[p], out_refs[p].at[me], k=per_part * p + 6, to=sib).wait_recv()
            for j, (cx, cy) in enumerate(chips):
                land = out_refs[p].at[2 * cx + cy, theirs]
                copy(land, land, k=per_part * p + 3 + j, to=sib).wait_recv()
        for j, (cx, cy) in enumerate(chips):
            copy(small_ref, outs_ref.at[2 * cx + cy], k=per_part * N_PARTS + j, to=sib).wait_recv()
        for cp in sent:
            cp.wait_send()
        local.wait()

    n_sems = per_part * N_PARTS + 3
    return pl.pallas_call(
        body, name="gather_weights",
        in_specs=[HBM_SPEC] * (N_PARTS + 1), out_specs=[HBM_SPEC] * (N_PARTS + 1),
        out_shape=[_sds((N_CHIPS,) + a.shape, a.dtype) for a in parts] + [_sds((N_CHIPS,) + small.shape, small.dtype)],
        scratch_shapes=[pltpu.SemaphoreType.DMA((n_sems,)), pltpu.SemaphoreType.DMA((n_sems,)),
                        pltpu.SemaphoreType.DMA((1,))],
    )(*parts, small)


def _exchange_halves(gparts):
    def body(*refs):
        g_refs, out_refs = refs[:N_PARTS], refs[N_PARTS:2 * N_PARTS]
        send_sems, recv_sems = refs[2 * N_PARTS:]
        x, y, c, _ = _chip_peers()
        sent = []
        for p, (_, rows, _) in enumerate(PARTS):
            theirs = pl.ds((1 - c) * (rows // 2), rows // 2)
            sent.append(_remote(g_refs[p].at[:, theirs], out_refs[p], send_sems, recv_sems, p, (x, y, 1 - c)))
            sent[-1].start()
        for cp in sent:
            cp.wait()

    return pl.pallas_call(
        body, name="exchange_halves", in_specs=[HBM_SPEC] * N_PARTS, out_specs=[HBM_SPEC] * N_PARTS,
        out_shape=[_sds((N_CHIPS, rows // 2, cols), BF16) for _, rows, cols in PARTS],
        scratch_shapes=[pltpu.SemaphoreType.DMA((N_PARTS,)), pltpu.SemaphoreType.DMA((N_PARTS,))],
    )(*gparts)


def _add_halves(g, recv, pos, name):
    _, half, cols = recv.shape
    tr = _row_tile(half, cols, 2)
    tiles = half // tr

    def body(pos_ref, a_ref, b_ref, o_ref):
        o_ref[...] = (a_ref[...].astype(F32) + b_ref[...].astype(F32)).astype(BF16)

    spec = pl.BlockSpec((1, tr, cols), lambda k, i, pos_ref: (k, i, 0))
    return pl.pallas_call(
        body, name=name,
        grid_spec=pltpu.PrefetchScalarGridSpec(
            num_scalar_prefetch=1, grid=(N_CHIPS, tiles),
            in_specs=[pl.BlockSpec((1, tr, cols), lambda k, i, pos_ref: (k, pos_ref[0] * tiles + i, 0)), spec],
            out_specs=spec),
        out_shape=_sds((N_CHIPS, half, cols), BF16),
        compiler_params=_params("parallel", "parallel"),
    )(pos, g, recv)


def _scatter_to_chips(s1, small):
    flips = [(fx, fy, fc) for fx in (0, 1) for fy in (0, 1) for fc in (0, 1)][1:]

    def body(*refs):
        s1_refs, small_ref = refs[:N_PARTS], refs[N_PARTS]
        recv_refs, all_ref = refs[N_PARTS + 1:2 * N_PARTS + 1], refs[2 * N_PARTS + 1]
        send_sems, recv_sems, loc_sem = refs[2 * N_PARTS + 2:]
        x, y, c, chips = _chip_peers()
        copy = functools.partial(_remote, send_sems=send_sems, recv_sems=recv_sems)
        my_slot = all_ref.at[4 * x + 2 * y + c]
        local = pltpu.make_async_copy(small_ref, my_slot, loc_sem.at[0])
        local.start()
        sent = [copy(s1_refs[p].at[2 * cx + cy], recv_refs[p].at[j], k=3 * p + j, to=(cx, cy, c))
                for p in range(N_PARTS) for j, (cx, cy) in enumerate(chips)]
        peers = [(1 - x if fx else x, 1 - y if fy else y, 1 - c if fc else c) for fx, fy, fc in flips]
        sent += [copy(small_ref, my_slot, k=3 * N_PARTS + n, to=p) for n, p in enumerate(peers)]
        for cp in sent:
            cp.start()
        for p in range(N_PARTS):
            for j in range(3):
                copy(s1_refs[p].at[0], recv_refs[p].at[j], k=3 * p + j, to=peers[0]).wait_recv()
        for n, (px, py, pc) in enumerate(peers):
            copy(small_ref, all_ref.at[4 * px + 2 * py + pc], k=3 * N_PARTS + n, to=peers[0]).wait_recv()
        for cp in sent:
            cp.wait_send()
        local.wait()

    n_sems = 3 * N_PARTS + 7
    return pl.pallas_call(
        body, name="scatter_to_chips", in_specs=[HBM_SPEC] * (N_PARTS + 1), out_specs=[HBM_SPEC] * (N_PARTS + 1),
        out_shape=[_sds((3,) + a.shape[1:], a.dtype) for a in s1] + [_sds((8,) + small.shape, small.dtype)],
        scratch_shapes=[pltpu.SemaphoreType.DMA((n_sems,)), pltpu.SemaphoreType.DMA((n_sems,)),
                        pltpu.SemaphoreType.DMA((1,))],
    )(*s1, small)


def _sum_chips(s1, recv, pos, name):
    _, half, cols = recv.shape
    tr = _row_tile(half, cols, 4)
    tiles = half // tr

    def body(pos_ref, a_ref, b_ref, o_ref):
        acc = a_ref[0].astype(F32)
        for j in range(3):
            acc = acc + b_ref[j].astype(F32)
        o_ref[...] = acc

    return pl.pallas_call(
        body, name=name,
        grid_spec=pltpu.PrefetchScalarGridSpec(
            num_scalar_prefetch=1, grid=(tiles,),
            in_specs=[pl.BlockSpec((1, tr, cols), lambda i, pos_ref: (pos_ref[1], i, 0)),
                      pl.BlockSpec((3, tr, cols), lambda i, pos_ref: (0, i, 0))],
            out_specs=pl.BlockSpec((tr, cols), lambda i, pos_ref: (pos_ref[0] * tiles + i, 0))),
        out_shape=_sds((2 * half, cols), F32),
        compiler_params=_params("parallel"),
    )(pos, s1, recv)


def _sum_small(small_all):
    def body(a_ref, o_ref):
        acc = a_ref[0]
        for d in range(1, 8):
            acc = acc + a_ref[d]
        o_ref[...] = acc

    return pl.pallas_call(body, name="sum_small", out_shape=_sds(small_all.shape[1:], F32))(small_all)


def _join_halves(s2):
    def body(*refs):
        buf_refs = refs[N_PARTS:2 * N_PARTS]
        send_sems, recv_sems = refs[2 * N_PARTS:]
        x, y, c, _ = _chip_peers()
        sent = []
        for p, (_, rows, _) in enumerate(PARTS):
            mine = buf_refs[p].at[pl.ds(c * (rows // 2), rows // 2)]
            sent.append(_remote(mine, mine, send_sems, recv_sems, p, (x, y, 1 - c)))
            sent[-1].start()
        for p, (_, rows, _) in enumerate(PARTS):
            theirs = buf_refs[p].at[pl.ds((1 - c) * (rows // 2), rows // 2)]
            _remote(theirs, theirs, send_sems, recv_sems, p, (x, y, 1 - c)).wait_recv()
        for cp in sent:
            cp.wait_send()

    return pl.pallas_call(
        body, name="join_halves", in_specs=[HBM_SPEC] * N_PARTS, out_specs=[HBM_SPEC] * N_PARTS,
        out_shape=[_sds(a.shape, a.dtype) for a in s2],
        input_output_aliases={p: p for p in range(N_PARTS)},
        scratch_shapes=[pltpu.SemaphoreType.DMA((N_PARTS,)), pltpu.SemaphoreType.DMA((N_PARTS,))],
    )(*s2)


def _adamw(w, g, m, v, name):
    rows, cols = w.shape
    tm = rows
    for cand in (512, 256, 128, 64, 32, 16, 8):
        if rows % cand == 0 and rows > cand:
            tm = cand
            break

    def body(w_ref, g_ref, m_ref, v_ref, d_ref, nm_ref, nv_ref):
        gv = g_ref[...]
        nm = ADAM_B1 * m_ref[...] + (1.0 - ADAM_B1) * gv
        nv = ADAM_B2 * v_ref[...] + (1.0 - ADAM_B2) * (gv * gv)
        m_hat = nm / (1.0 - ADAM_B1 ** ADAM_STEP)
        v_hat = nv / (1.0 - ADAM_B2 ** ADAM_STEP)
        d_ref[...] = -ADAM_LR * (m_hat / (jnp.sqrt(v_hat) + ADAM_EPS) + ADAM_WD * w_ref[...])
        nm_ref[...] = nm
        nv_ref[...] = nv

    spec = _rows(tm, cols)
    return pl.pallas_call(
        body, name=name, grid=(rows // tm,), in_specs=[spec] * 4, out_specs=[spec] * 3,
        out_shape=[_sds((rows, cols), F32)] * 3, compiler_params=_params("parallel"),
    )(w, g, m, v)


WEIGHTS = ("ml_w_in", "ml_b_if", "ml_head_norm", "ml_w_out", "lru_w_in", "lru_conv_w", "lru_conv_b", "lru_w_gate_a",
           "lru_b_gate_a", "lru_w_gate_x", "lru_b_gate_x", "lru_lambda", "lru_w_out", "norm_pre_mix", "norm_post_mix",
           "norm_pre_ffn", "norm_post_ffn", "ffn_w_gate", "ffn_w_up", "ffn_w_down")
LRU_VECTORS = ("lru_conv_b", "lru_b_gate_a", "lru_b_gate_x", "lru_lambda")
NORMS = ("norm_pre_mix", "norm_post_mix", "norm_pre_ffn", "norm_post_ffn")


def _full_weights(w):
    small = jnp.concatenate([w["lru_conv_w"][0]] + [w[n] for n in LRU_VECTORS], axis=0)
    bf = lambda a: a.astype(BF16)
    shards = dict(
        ml_w_in=bf(w["ml_w_in"][0]), ml_w_out=bf(w["ml_w_out"][0]), lru_w_in=bf(w["lru_w_in"][0]),
        lru_gates=bf(jnp.concatenate([w["lru_w_gate_a"][0], w["lru_w_gate_x"][0]], axis=0)).reshape(-1, LRU_BLOCK),
        lru_w_out=bf(w["lru_w_out"][0]),
        ffn_gu=bf(jnp.concatenate([w["ffn_w_gate"], w["ffn_w_up"]], axis=2)).reshape(-1, 2 * HALF_FFN),
        ffn_down=bf(w["ffn_w_down"]).reshape(-1, 1024))
    *gathered, smalls = _gather_weights([shards[n] for n, _, _ in PARTS], small)
    got = dict(zip([n for n, _, _ in PARTS], gathered))
    by_cols = lambda a: jnp.concatenate([a[k] for k in range(N_CHIPS)], axis=-1)
    gates = got["lru_gates"].reshape(N_CHIPS, 2, LRU_BLOCKS, 64, LRU_BLOCK).transpose(1, 2, 0, 3, 4)
    gates = gates.reshape(2, LRU_BLOCKS, LRU_BLOCK, LRU_BLOCK)
    gu = got["ffn_gu"].reshape(N_CHIPS, 2, 1024, 2 * HALF_FFN)
    down = got["ffn_down"].reshape(N_CHIPS, 2, HALF_FFN, 1024).transpose(1, 0, 2, 3)
    vec = by_cols(smalls)
    return dict(
        ml_w_in=jnp.pad(by_cols(got["ml_w_in"]), ((0, 0), (0, ML_IN_PAD - ML_IN))),
        ml_b_if=jnp.pad(w["ml_b_if"], ((0, 0), (0, 128 - 2 * ML_HEADS))),
        ml_head_norm=w["ml_head_norm"], ml_w_out=got["ml_w_out"].reshape(1024, 1024),
        lru_w_in=by_cols(got["lru_w_in"]),
        lru_conv_w=vec[0:4], lru_conv_b=vec[4:5], lru_b_gate_a=vec[5:6], lru_b_gate_x=vec[6:7], lru_lambda=vec[7:8],
        lru_w_gate_a=gates[0], lru_w_gate_x=gates[1], lru_w_out=got["lru_w_out"].reshape(1024, 1024),
        ffn_w_gu=jnp.concatenate([by_cols(gu[..., :HALF_FFN]), by_cols(gu[..., HALF_FFN:])], axis=-1),
        ffn_w_down=down.reshape(2, D_FF, 1024),
        **{n: w[n] for n in NORMS})


def _reduce_gradients(loss, g):
    x, y, c, _ = _chip_peers()
    me = 2 * x + y
    pos = jnp.stack([c, me])
    by_chip = lambda a, width: jnp.stack([a[..., k * width:(k + 1) * width] for k in range(N_CHIPS)])
    gu = g["ffn_w_gu"]
    gates = jnp.stack([g["lru_w_gate_a"], g["lru_w_gate_x"]]).astype(BF16)
    gparts = dict(
        ml_w_in=by_chip(g["ml_w_in"], ML_IN // N_CHIPS),
        ml_w_out=g["ml_w_out"].reshape(N_CHIPS, 256, 1024),
        lru_w_in=by_chip(g["lru_w_in"], 512),
        lru_gates=gates.reshape(2, LRU_BLOCKS, N_CHIPS, 64, LRU_BLOCK).transpose(2, 0, 1, 3, 4).reshape(N_CHIPS, -1, LRU_BLOCK),
        lru_w_out=g["lru_w_out"].reshape(N_CHIPS, 256, 1024),
        ffn_gu=jnp.concatenate([by_chip(gu[..., :D_FF], HALF_FFN), by_chip(gu[..., D_FF:], HALF_FFN)],
                               axis=-1).reshape(N_CHIPS, -1, 2 * HALF_FFN),
        ffn_down=g["ffn_w_down"].reshape(2, N_CHIPS, HALF_FFN, 1024).transpose(1, 0, 2, 3).reshape(N_CHIPS, -1, 1024))
    names = [n for n, _, _ in PARTS]
    pad_lanes = lambda a: jnp.pad(a, ((0, 0), (0, LANES - a.shape[1])))
    small = jnp.concatenate(
        [g[n] for n in NORMS] + [pad_lanes(loss), g["ml_head_norm"], pad_lanes(g["ml_b_if"]), g["lru_conv_w"]]
        + [g[n] for n in LRU_VECTORS] + [jnp.zeros((SMALL_ROWS - 19, LANES), F32)], axis=0)
    recv1 = _exchange_halves([gparts[n] for n in names])
    s1 = [_add_halves(gparts[n], r, pos, "add_halves_" + n) for n, r in zip(names, recv1)]
    *recv2, small_all = _scatter_to_chips(s1, small)
    s2 = [_sum_chips(a, r, pos, "sum_chips_" + n) for n, a, r in zip(names, s1, recv2)]
    red = dict(zip(names, _join_halves(s2)))
    gates = red["lru_gates"].reshape(2, LRU_BLOCKS, 64, LRU_BLOCK)
    gu = red["ffn_gu"].reshape(2, 1024, 2 * HALF_FFN)
    reduced = dict(ml_w_in=red["ml_w_in"], ml_w_out=red["ml_w_out"], lru_w_in=red["lru_w_in"],
                   lru_w_gate_a=gates[0], lru_w_gate_x=gates[1], lru_w_out=red["lru_w_out"],
                   ffn_w_gate=gu[..., :HALF_FFN], ffn_w_up=gu[..., HALF_FFN:],
                   ffn_w_down=red["ffn_down"].reshape(2, HALF_FFN, 1024))
    vec = _sum_small(small_all)
    cols = lambda a: lax.dynamic_slice_in_dim(a, me * 256, 256, axis=1)
    for i, n in enumerate(NORMS):
        reduced[n] = vec[2 * i:2 * i + 2]
    reduced["ml_head_norm"] = vec[ROW_HEAD_NORM:ROW_HEAD_NORM + 1]
    reduced["ml_b_if"] = vec[ROW_B_IF:ROW_B_IF + 1, :2 * ML_HEADS]
    reduced["lru_conv_w"] = cols(vec[ROW_LRU:ROW_LRU + 4])
    for i, n in enumerate(LRU_VECTORS):
        reduced[n] = cols(vec[ROW_LRU + 4 + i:ROW_LRU + 5 + i])
    return vec[ROW_LOSS, 0], reduced


def _update(w, grads, m, v):
    delta, new_m, new_v = {}, {}, {}
    for n in WEIGHTS:
        shape = w[n].shape
        as2d = lambda a: a.reshape(-1, shape[-1])
        d, nm, nv = _adamw(as2d(w[n]), as2d(grads[n]), as2d(m[n]), as2d(v[n]), "adamw_" + n)
        delta[n], new_m[n], new_v[n] = d.reshape(shape), nm.reshape(shape), nv.reshape(shape)
    return delta, new_m, new_v


def kernel(x, ml_w_in, ml_b_if, ml_head_norm, ml_w_out, lru_w_in, lru_conv_w, lru_conv_b, lru_w_gate_a, lru_b_gate_a, lru_w_gate_x, lru_b_gate_x, lru_lambda, lru_w_out, norm_pre_mix, norm_post_mix, norm_pre_ffn, norm_post_ffn, ffn_w_gate, ffn_w_up, ffn_w_down, loss_target, m_ml_w_in, m_ml_b_if, m_ml_head_norm, m_ml_w_out, m_lru_w_in, m_lru_conv_w, m_lru_conv_b, m_lru_w_gate_a, m_lru_b_gate_a, m_lru_w_gate_x, m_lru_b_gate_x, m_lru_lambda, m_lru_w_out, m_norm_pre_mix, m_norm_post_mix, m_norm_pre_ffn, m_norm_post_ffn, m_ffn_w_gate, m_ffn_w_up, m_ffn_w_down, v_ml_w_in, v_ml_b_if, v_ml_head_norm, v_ml_w_out, v_lru_w_in, v_lru_conv_w, v_lru_conv_b, v_lru_w_gate_a, v_lru_b_gate_a, v_lru_w_gate_x, v_lru_b_gate_x, v_lru_lambda, v_lru_w_out, v_norm_pre_mix, v_norm_post_mix, v_norm_pre_ffn, v_norm_post_ffn, v_ffn_w_gate, v_ffn_w_up, v_ffn_w_down):
    args = locals()
    w = {n: args[n] for n in WEIGHTS}
    m = {n: args["m_" + n] for n in WEIGHTS}
    v = {n: args["v_" + n] for n in WEIGHTS}
    loss_part, dx, g = _local_step(x[0], loss_target[0], _full_weights(w))
    loss, grads = _reduce_gradients(loss_part, g)
    grads = {n: grads[n].reshape(w[n].shape) for n in WEIGHTS}
    delta, new_m, new_v = _update(w, grads, m, v)
    return (loss, dx[None], *[grads[n] for n in WEIGHTS], *[delta[n] for n in WEIGHTS],
            *[new_m[n] for n in WEIGHTS], *[new_v[n] for n in WEIGHTS])
```

```python
import functools
import math

import jax
import jax.numpy as jnp
from jax import lax
from jax.experimental import pallas as pl
from jax.experimental.pallas import tpu as pltpu

F32 = jnp.float32
BF16 = jnp.bfloat16
MESH = pl.DeviceIdType.MESH

D_MODEL = 1024
D_FF = 2816
ML_HEADS = 8
ML_DK = 64
ML_DV = 128
ML_QK = ML_HEADS * ML_DK
ML_V = ML_HEADS * ML_DV
ML_IN = 2 * ML_QK + 2 * ML_V + 2 * ML_HEADS
ML_IN_PAD = 3200
CHUNK = 64
GATE_CAP = 15.0
ML_M_INIT = -1e30
NEG_BIG = -1e30
LRU_BLOCKS = 4
LRU_BLOCK = 256
CONV_WIDTH = 4
LRU_C = 8.0
EPS = 1e-6
QK_SCALE = ML_DK ** -0.5

ADAM_LR = 0.001
ADAM_B1 = 0.9
ADAM_B2 = 0.999
ADAM_EPS = 1e-08
ADAM_WD = 0.01
ADAM_STEP = 10

N_CHIPS = 4
V7X_VMEM_LIMIT = 56 * 1024 * 1024

NT_DIMS = (((1,), (1,)), ((), ()))
TN_DIMS = (((0,), (0,)), ((), ()))


def _params(*semantics):
    return pltpu.CompilerParams(dimension_semantics=semantics, vmem_limit_bytes=V7X_VMEM_LIMIT)


def _sds(shape, dtype):
    return jax.ShapeDtypeStruct(shape, dtype)


def _full(shape):
    return pl.BlockSpec(shape, lambda *_: (0,) * len(shape))


def _rows(tm, n):
    return pl.BlockSpec((tm, n), lambda i: (i, 0))


def _sigmoid(x):
    return 1.0 / (1.0 + jnp.exp(-x))


def _log_sigmoid(x):
    return jnp.minimum(x, 0.0) - jnp.log1p(jnp.exp(-jnp.abs(x)))


def _rms(x):
    r = lax.rsqrt(jnp.mean(x * x, axis=-1, keepdims=True) + EPS)
    return x * r, r


def _rms_bwd(xhat, r, g, dy):
    dxh = dy * g
    return r * (dxh - xhat * jnp.mean(dxh * xhat, axis=-1, keepdims=True))


def _scan_rows(a, b, reverse=False):
    n = a.shape[0]
    row = lax.broadcasted_iota(jnp.int32, a.shape, 0)
    s = 1
    while s < n:
        if reverse:
            keep = row < n - s
            shift = n - s
        else:
            keep = row >= s
            shift = s
        b = b + a * jnp.where(keep, pltpu.roll(b, shift, 0), 0.0)
        a = a * jnp.where(keep, pltpu.roll(a, shift, 0), 1.0)
        s *= 2
    return a, b


def _cumsum_rows(x, reverse=False):
    n = x.shape[0]
    row = lax.broadcasted_iota(jnp.int32, x.shape, 0)
    s = 1
    while s < n:
        if reverse:
            x = x + jnp.where(row < n - s, pltpu.roll(x, n - s, 0), 0.0)
        else:
            x = x + jnp.where(row >= s, pltpu.roll(x, s, 0), 0.0)
        s *= 2
    return x


def _norm_matmul(h, g, w, n_bf16, name):
    t, d = h.shape
    n = w.shape[1]
    tm = 512

    def body(h_ref, g_ref, w_ref, z_ref, *o_refs):
        xhat, _ = _rms(h_ref[...])
        z = (xhat * g_ref[...]).astype(BF16)
        z_ref[...] = z
        out = jnp.dot(z, w_ref[...], preferred_element_type=F32)
        if n_bf16:
            o_refs[0][...] = out[:, :n_bf16].astype(BF16)
            o_refs[1][...] = out[:, n_bf16:]
        else:
            o_refs[0][...] = out

    if n_bf16:
        out_specs = [_rows(tm, d), _rows(tm, n_bf16), _rows(tm, n - n_bf16)]
        out_shape = [_sds((t, d), BF16), _sds((t, n_bf16), BF16), _sds((t, n - n_bf16), F32)]
    else:
        out_specs = [_rows(tm, d), _rows(tm, n)]
        out_shape = [_sds((t, d), BF16), _sds((t, n), F32)]
    return pl.pallas_call(
        body, name=name, grid=(t // tm,),
        in_specs=[_rows(tm, d), _full((1, d)), _full((d, n))],
        out_specs=out_specs, out_shape=out_shape, compiler_params=_params("parallel"),
    )(h, g, w)


def _matmul_postnorm(a, w, g, res, name):
    t = res.shape[0]
    k, d = w.shape
    tm = 256

    def body(a_ref, w_ref, g_ref, res_ref, y_ref, o_ref):
        y = jnp.dot(a_ref[...], w_ref[...], preferred_element_type=F32)
        y_ref[...] = y
        yhat, _ = _rms(y)
        o_ref[...] = res_ref[...] + yhat * g_ref[...]

    return pl.pallas_call(
        body, name=name, grid=(t // tm,),
        in_specs=[_rows(tm, k), _full((k, d)), _full((1, d)), _rows(tm, d)],
        out_specs=[_rows(tm, d), _rows(tm, d)],
        out_shape=[_sds((t, d), F32), _sds((t, d), F32)],
        compiler_params=_params("parallel"),
    )(a, w, g, res)


def _ffn_forward(h, g_pre, w_gu, w_down, g_post, name):
    t, d = h.shape
    k = w_down.shape[0]
    tm = 256

    def body(h_ref, gpre_ref, wgu_ref, wd_ref, gpost_ref, z_ref, gu_ref, y_ref, o_ref):
        hv = h_ref[...]
        xhat, _ = _rms(hv)
        z = (xhat * gpre_ref[...]).astype(BF16)
        z_ref[...] = z
        gu = jnp.dot(z, wgu_ref[...], preferred_element_type=F32).astype(BF16)
        gu_ref[...] = gu
        gate = gu[:, :k].astype(F32)
        act = (gate * _sigmoid(gate) * gu[:, k:].astype(F32)).astype(BF16)
        y = jnp.dot(act, wd_ref[...], preferred_element_type=F32)
        y_ref[...] = y
        yhat, _ = _rms(y)
        o_ref[...] = hv + yhat * gpost_ref[...]

    return pl.pallas_call(
        body, name=name, grid=(t // tm,),
        in_specs=[_rows(tm, d), _full((1, d)), _full((d, 2 * k)), _full((k, d)), _full((1, d))],
        out_specs=[_rows(tm, d), _rows(tm, 2 * k), _rows(tm, d), _rows(tm, d)],
        out_shape=[_sds((t, d), BF16), _sds((t, 2 * k), BF16), _sds((t, d), F32), _sds((t, d), F32)],
        compiler_params=_params("parallel"),
    )(h, g_pre, w_gu, w_down, g_post)


def _loss_head(h, target, name):
    t, d = h.shape
    tm = 512

    def body(h_ref, t_ref, dh_ref, l_ref):
        @pl.when(pl.program_id(0) == 0)
        def _():
            l_ref[...] = jnp.zeros_like(l_ref)

        err = h_ref[...] - t_ref[...]
        dh_ref[...] = err * (1.0 / d)
        part = jnp.sum(jnp.sum(err * err, axis=1, keepdims=True), axis=0, keepdims=True) * (0.5 / d)
        l_ref[...] += jnp.broadcast_to(part, l_ref.shape)

    return pl.pallas_call(
        body, name=name, grid=(t // tm,),
        in_specs=[_rows(tm, d), _rows(tm, d)],
        out_specs=[_rows(tm, d), _full((1, 128))],
        out_shape=[_sds((t, d), F32), _sds((1, 128), F32)],
        compiler_params=_params("arbitrary"),
    )(h, target)


def _bwd_out(dout, y, g, w, gu, name):
    t, d = dout.shape
    k = w.shape[0]
    ffn = gu is not None
    tm = 128 if ffn else 256

    def body(*refs):
        if ffn:
            dout_ref, y_ref, g_ref, w_ref, gu_ref, dy_ref, dact_ref, act_ref, dg_ref = refs
        else:
            dout_ref, y_ref, g_ref, w_ref, dy_ref, dact_ref, dg_ref = refs

        @pl.when(pl.program_id(0) == 0)
        def _():
            dg_ref[...] = jnp.zeros_like(dg_ref)

        do = dout_ref[...]
        yhat, r = _rms(y_ref[...])
        dg_ref[...] += jnp.sum(do * yhat, axis=0, keepdims=True)
        dy = _rms_bwd(yhat, r, g_ref[...], do).astype(BF16)
        dy_ref[...] = dy
        da = lax.dot_general(dy, w_ref[...], NT_DIMS, preferred_element_type=F32)
        if ffn:
            gate = gu_ref[:, :k].astype(F32)
            up = gu_ref[:, k:].astype(F32)
            sg = _sigmoid(gate)
            silu = gate * sg
            act_ref[...] = (silu * up).astype(BF16)
            dact_ref[:, :k] = (da * up * (sg * (1.0 + gate * (1.0 - sg)))).astype(BF16)
            dact_ref[:, k:] = (da * silu).astype(BF16)
        else:
            dact_ref[...] = da

    in_specs = [_rows(tm, d), _rows(tm, d), _full((1, d)), _full((k, d))]
    args = [dout, y, g, w]
    if ffn:
        in_specs.append(_rows(tm, 2 * k))
        args.append(gu)
        out_specs = [_rows(tm, d), _rows(tm, 2 * k), _rows(tm, k), _full((1, d))]
        out_shape = [_sds((t, d), BF16), _sds((t, 2 * k), BF16), _sds((t, k), BF16), _sds((1, d), F32)]
    else:
        out_specs = [_rows(tm, d), _rows(tm, k), _full((1, d))]
        out_shape = [_sds((t, d), BF16), _sds((t, k), F32), _sds((1, d), F32)]
    return pl.pallas_call(
        body, name=name, grid=(t // tm,), in_specs=in_specs, out_specs=out_specs, out_shape=out_shape,
        compiler_params=_params("arbitrary"),
    )(*args)


def _bwd_in(dp, w, h, g, dout, name):
    t, d = h.shape
    n = w.shape[1]
    tm = 256

    def body(dp_ref, w_ref, h_ref, g_ref, dout_ref, dh_ref, dg_ref):
        @pl.when(pl.program_id(0) == 0)
        def _():
            dg_ref[...] = jnp.zeros_like(dg_ref)

        dz = lax.dot_general(dp_ref[...], w_ref[...], NT_DIMS, preferred_element_type=F32)
        hhat, r = _rms(h_ref[...])
        dg_ref[...] += jnp.sum(dz * hhat, axis=0, keepdims=True)
        dh_ref[...] = dout_ref[...] + _rms_bwd(hhat, r, g_ref[...], dz)

    return pl.pallas_call(
        body, name=name, grid=(t // tm,),
        in_specs=[_rows(tm, n), _full((d, n)), _rows(tm, d), _full((1, d)), _rows(tm, d)],
        out_specs=[_rows(tm, d), _full((1, d))],
        out_shape=[_sds((t, d), F32), _sds((1, d), F32)],
        compiler_params=_params("arbitrary"),
    )(dp, w, h, g, dout)


def _matmul_tn(a, b, tile_a, tile, name):
    t, ka = a.shape
    nb = b.shape[1]

    def body(a_ref, b_ref, o_ref):
        o_ref[...] = lax.dot_general(a_ref[...], b_ref[...], TN_DIMS, preferred_element_type=F32).astype(BF16)

    if tile_a:
        grid = (ka // tile,)
        in_specs = [pl.BlockSpec((t, tile), lambda i: (0, i)), _full((t, nb))]
        out_specs = pl.BlockSpec((tile, nb), lambda i: (i, 0))
    else:
        grid = (nb // tile,)
        in_specs = [_full((t, ka)), pl.BlockSpec((t, tile), lambda i: (0, i))]
        out_specs = pl.BlockSpec((ka, tile), lambda i: (0, i))
    return pl.pallas_call(
        body, name=name, grid=grid, in_specs=in_specs, out_specs=out_specs,
        out_shape=_sds((ka, nb), BF16), compiler_params=_params("parallel"),
    )(a, b)


def _ml_gate_prep(gt, bif):
    th = jnp.tanh((gt + bif) / GATE_CAP)
    act = GATE_CAP * th
    cum = _cumsum_rows(_log_sigmoid(act))
    lane = lax.broadcasted_iota(jnp.int32, gt.shape, 1)
    x = jnp.where(lane < ML_HEADS, act, cum)
    return x, x.T, th, act


def _ml_head_fwd(h, q, k, v, x, xt, c_in, n_in, m_in, causal):
    qh = q[:, ML_DK * h:ML_DK * (h + 1)]
    kh = k[:, ML_DK * h:ML_DK * (h + 1)]
    vh = v[:, ML_DV * h:ML_DV * (h + 1)]
    s = lax.dot_general(qh, kh, NT_DIMS, preferred_element_type=F32) * QK_SCALE
    bcol = x[:, ML_HEADS + h:ML_HEADS + h + 1]
    licol = x[:, h:h + 1]
    brow = xt[ML_HEADS + h:ML_HEADS + h + 1, :]
    lirow = xt[h:h + 1, :]
    dmat = jnp.where(causal, bcol - brow + lirow, NEG_BIG)
    inter = bcol + m_in
    mt = jnp.maximum(jnp.max(dmat, axis=1, keepdims=True), inter)
    wt = jnp.exp(dmat - mt)
    p = wt * s
    winter = jnp.exp(inter - mt)
    cb = c_in.astype(BF16)
    qc = jnp.dot(qh, cb, preferred_element_type=F32) * QK_SCALE
    qf = qh.astype(F32)
    qn = jnp.sum(qf * n_in, axis=1, keepdims=True) * QK_SCALE
    den = jnp.sum(p, axis=1, keepdims=True) + winter * qn
    emt = jnp.exp(-mt)
    nrm = jnp.maximum(jnp.abs(den), emt)
    gtot = bcol[CHUNK - 1:CHUNK, :]
    a_col = gtot - bcol + licol
    m_new = jnp.maximum(gtot + m_in, jnp.max(a_col, axis=0, keepdims=True))
    decay = jnp.exp(gtot + m_in - m_new)
    wkf = jnp.exp(a_col - m_new)
    kw = kh.astype(F32) * wkf
    return dict(qh=qh, kh=kh, vh=vh, wt=wt, p=p, winter=winter, cb=cb, qc=qc, qf=qf, qn=qn, den=den,
                emt=emt, nrm=nrm, m_new=m_new, decay=decay, wkf=wkf, kw=kw)


def _ml_specs(nc, rev):
    def at(col):
        if rev:
            return lambda i: (nc - 1 - i, col)
        return lambda i: (i, col)

    return [pl.BlockSpec((CHUNK, ML_QK), at(0)), pl.BlockSpec((CHUNK, ML_QK), at(1)),
            pl.BlockSpec((CHUNK, ML_V), at(1)), pl.BlockSpec((CHUNK, ML_V), at(0)),
            pl.BlockSpec((CHUNK, 128), at(ML_V // 128))]


def _mlstm_fwd(qkv, og, bif, hn):
    t = qkv.shape[0]
    nc = t // CHUNK

    def body(q_ref, k_ref, v_ref, o_ref, gt_ref, bif_ref, hn_ref, y_ref, hs_ref, cst_ref, nst_ref, mst_ref,
             c_sc, n_sc, m_sc):
        @pl.when(pl.program_id(0) == 0)
        def _():
            c_sc[...] = jnp.zeros_like(c_sc)
            n_sc[...] = jnp.zeros_like(n_sc)
            m_sc[...] = jnp.full_like(m_sc, ML_M_INIT)

        cst_ref[0] = c_sc[...]
        nst_ref[0] = n_sc[...]
        mst_ref[0] = m_sc[...]
        x, xt, _, _ = _ml_gate_prep(gt_ref[...], bif_ref[...])
        q, k, v = q_ref[...], k_ref[...], v_ref[...]
        causal = (lax.broadcasted_iota(jnp.int32, (CHUNK, CHUNK), 0)
                  >= lax.broadcasted_iota(jnp.int32, (CHUNK, CHUNK), 1))
        for h in range(ML_HEADS):
            vs = slice(ML_DV * h, ML_DV * (h + 1))
            c_in = c_sc[ML_DK * h:ML_DK * (h + 1), :]
            n_in = n_sc[h:h + 1, :]
            m_in = m_sc[h:h + 1, 0:1]
            f = _ml_head_fwd(h, q, k, v, x, xt, c_in, n_in, m_in, causal)
            num = jnp.dot(f["p"].astype(BF16), f["vh"], preferred_element_type=F32) + f["winter"] * f["qc"]
            hh = num / f["nrm"]
            hs_ref[:, vs] = hh
            hhat, _ = _rms(hh)
            y_ref[:, vs] = (hhat * hn_ref[:, vs] * _sigmoid(o_ref[:, vs])).astype(BF16)
            c_sc[ML_DK * h:ML_DK * (h + 1), :] = f["decay"] * c_in + lax.dot_general(
                f["kw"].astype(BF16), f["vh"], TN_DIMS, preferred_element_type=F32)
            n_sc[h:h + 1, :] = f["decay"] * n_in + jnp.sum(f["kw"], axis=0, keepdims=True)
            m_sc[h:h + 1, :] = jnp.broadcast_to(f["m_new"], (1, 128))

    return pl.pallas_call(
        body, name="mlstm_fwd", grid=(nc,),
        in_specs=_ml_specs(nc, False) + [_full((1, 128)), _full((1, ML_V))],
        out_specs=[_rows(CHUNK, ML_V), _rows(CHUNK, ML_V),
                   pl.BlockSpec((1, ML_HEADS * ML_DK, ML_DV), lambda i: (i, 0, 0)),
                   pl.BlockSpec((1, ML_HEADS, ML_DK), lambda i: (i, 0, 0)),
                   pl.BlockSpec((1, ML_HEADS, 128), lambda i: (i, 0, 0))],
        out_shape=[_sds((t, ML_V), BF16), _sds((t, ML_V), F32),
                   _sds((nc, ML_HEADS * ML_DK, ML_DV), F32), _sds((nc, ML_HEADS, ML_DK), F32),
                   _sds((nc, ML_HEADS, 128), F32)],
        scratch_shapes=[pltpu.VMEM((ML_HEADS * ML_DK, ML_DV), F32), pltpu.VMEM((ML_HEADS, ML_DK), F32),
                        pltpu.VMEM((ML_HEADS, 128), F32)],
        compiler_params=_params("arbitrary"),
    )(qkv, qkv, qkv, og, og, bif, hn)


def _mlstm_bwd(qkv, og, bif, hn, hs, cst, nst, mst, dy):
    t = qkv.shape[0]
    nc = t // CHUNK

    def body(q_ref, k_ref, v_ref, o_ref, gt_ref, bif_ref, hn_ref, hs_ref, cst_ref, nst_ref, mst_ref, dy_ref,
             dp_ref, dhn_ref, dbif_ref, dc_sc, dn_sc):
        @pl.when(pl.program_id(0) == 0)
        def _():
            dc_sc[...] = jnp.zeros_like(dc_sc)
            dn_sc[...] = jnp.zeros_like(dn_sc)
            dhn_ref[...] = jnp.zeros_like(dhn_ref)
            dbif_ref[...] = jnp.zeros_like(dbif_ref)

        x, xt, th, act = _ml_gate_prep(gt_ref[...], bif_ref[...])
        q, k, v = q_ref[...], k_ref[...], v_ref[...]
        causal = (lax.broadcasted_iota(jnp.int32, (CHUNK, CHUNK), 0)
                  >= lax.broadcasted_iota(jnp.int32, (CHUNK, CHUNK), 1))
        lane = lax.broadcasted_iota(jnp.int32, (CHUNK, 128), 1)
        row = lax.broadcasted_iota(jnp.int32, (CHUNK, 1), 0)
        dx = jnp.zeros((CHUNK, 128), F32)
        for h in range(ML_HEADS):
            ks = slice(ML_DK * h, ML_DK * (h + 1))
            vs = slice(ML_DV * h, ML_DV * (h + 1))
            c_in = cst_ref[0, ks, :]
            n_in = nst_ref[0, h:h + 1, :]
            m_in = mst_ref[0, h:h + 1, 0:1]
            f = _ml_head_fwd(h, q, k, v, x, xt, c_in, n_in, m_in, causal)
            qh, kh, vh, p, winter = f["qh"], f["kh"], f["vh"], f["p"], f["winter"]
            hh = hs_ref[:, vs]
            hhat, r = _rms(hh)
            hn_h = hn_ref[:, vs]
            sg = _sigmoid(o_ref[:, vs])
            dyh = dy_ref[:, vs]
            dyn = dyh * sg
            dp_ref[:, 2 * ML_QK + ML_V + ML_DV * h:2 * ML_QK + ML_V + ML_DV * (h + 1)] = (
                dyh * hhat * hn_h * sg * (1.0 - sg)).astype(BF16)
            dhn_ref[:, vs] += jnp.sum(dyn * hhat, axis=0, keepdims=True)
            dh = _rms_bwd(hhat, r, hn_h, dyn)
            inv = 1.0 / f["nrm"]
            dnum = dh * inv
            dnrm = -jnp.sum(dh * hh, axis=1, keepdims=True) * inv
            dden = jnp.where(jnp.abs(f["den"]) > f["emt"], dnrm * jnp.sign(f["den"]), 0.0)
            dnb = dnum.astype(BF16)
            rmat = lax.dot_general(dnb, vh, NT_DIMS, preferred_element_type=F32) + dden
            ds = (f["wt"] * rmat).astype(BF16)
            qmat = p * rmat
            dv = lax.dot_general(p.astype(BF16), dnb, TN_DIMS, preferred_element_type=F32)
            dqs = jnp.dot(ds, kh, preferred_element_type=F32) + winter * (
                lax.dot_general(dnb, f["cb"], NT_DIMS, preferred_element_type=F32) + dden * n_in)
            dk = lax.dot_general(ds, qh, TN_DIMS, preferred_element_type=F32) * QK_SCALE
            dinter = (jnp.sum(f["qc"] * dnum, axis=1, keepdims=True) + f["qn"] * dden) * winter
            wq = winter * f["qf"] * QK_SCALE
            dc_loc = lax.dot_general(wq.astype(BF16), dnb, TN_DIMS, preferred_element_type=F32)
            dn_loc = jnp.sum(wq * dden, axis=0, keepdims=True)
            cs_q = jnp.sum(qmat.T, axis=1, keepdims=True)
            db = jnp.sum(qmat, axis=1, keepdims=True) + dinter - cs_q
            dli = cs_q
            dcn = dc_sc[ks, :]
            dnn = dn_sc[h:h + 1, :]
            dcnb = dcn.astype(BF16)
            ddecay = (jnp.sum(jnp.sum(dcn * c_in, axis=1, keepdims=True), axis=0, keepdims=True)
                      + jnp.sum(dnn * n_in, axis=1, keepdims=True))
            dkw = lax.dot_general(vh, dcnb, NT_DIMS, preferred_element_type=F32) + dnn
            dk = dk + f["wkf"] * dkw
            da = jnp.sum(dkw * kh.astype(F32), axis=1, keepdims=True) * f["wkf"]
            dv = dv + jnp.dot(f["kw"].astype(BF16), dcnb, preferred_element_type=F32)
            dgtot = jnp.sum(da, axis=0, keepdims=True) + ddecay * f["decay"]
            db = db - da + jnp.where(row == CHUNK - 1, dgtot, 0.0)
            dli = dli + da
            dc_sc[ks, :] = f["decay"] * dcn + dc_loc
            dn_sc[h:h + 1, :] = f["decay"] * dnn + dn_loc
            dp_ref[:, ks] = (dqs * QK_SCALE).astype(BF16)
            dp_ref[:, ML_QK + ML_DK * h:ML_QK + ML_DK * (h + 1)] = dk.astype(BF16)
            dp_ref[:, 2 * ML_QK + ML_DV * h:2 * ML_QK + ML_DV * (h + 1)] = dv.astype(BF16)
            dx = jnp.where(lane == h, dli, dx)
            dx = jnp.where(lane == ML_HEADS + h, db, dx)
        dlf = _cumsum_rows(dx, reverse=True)
        dact = jnp.where(lane < ML_HEADS, dx, dlf * _sigmoid(-act))
        dz = dact * (1.0 - th * th)
        dp_ref[:, 2 * ML_QK + 2 * ML_V:] = dz.astype(BF16)
        dbif_ref[...] += jnp.sum(dz, axis=0, keepdims=True)

    rev = lambda i: (nc - 1 - i, 0)
    rev3 = lambda i: (nc - 1 - i, 0, 0)
    return pl.pallas_call(
        body, name="mlstm_bwd", grid=(nc,),
        in_specs=_ml_specs(nc, True) + [
            _full((1, 128)), _full((1, ML_V)), pl.BlockSpec((CHUNK, ML_V), rev),
            pl.BlockSpec((1, ML_HEADS * ML_DK, ML_DV), rev3), pl.BlockSpec((1, ML_HEADS, ML_DK), rev3),
            pl.BlockSpec((1, ML_HEADS, 128), rev3), pl.BlockSpec((CHUNK, ML_V), rev)],
        out_specs=[pl.BlockSpec((CHUNK, ML_IN_PAD), rev), _full((1, ML_V)), _full((1, 128))],
        out_shape=[_sds((t, ML_IN_PAD), BF16), _sds((1, ML_V), F32), _sds((1, 128), F32)],
        scratch_shapes=[pltpu.VMEM((ML_HEADS * ML_DK, ML_DV), F32), pltpu.VMEM((ML_HEADS, ML_DK), F32)],
        compiler_params=_params("arbitrary"),
    )(qkv, qkv, qkv, og, og, bif, hn, hs, cst, nst, mst, dy)


LRU_TM = 256
GELU_K = math.sqrt(2.0 / math.pi)
GELU_C = 0.044715


def _gelu(x):
    th = jnp.tanh(GELU_K * (x + GELU_C * x * x * x))
    return 0.5 * x * (1.0 + th), th


def _neg_expm1(x):
    series = -x * (1.0 + x * (0.5 + x * (1.0 / 6.0 + x * (1.0 / 24.0))))
    return jnp.where(x > -0.05, series, 1.0 - jnp.exp(x))


def _block_diag_dot(a, w_ref, dims):
    parts = [lax.dot_general(a[:, LRU_BLOCK * n:LRU_BLOCK * (n + 1)], w_ref[n], dims, preferred_element_type=F32)
             for n in range(LRU_BLOCKS)]
    return jnp.concatenate(parts, axis=1)


def _lru_gates(u, r, ig, lam):
    ls = _log_sigmoid(lam)
    la = LRU_C * r * ls
    a = jnp.exp(la)
    em = _neg_expm1(2.0 * la)
    mult = jnp.sqrt(em)
    return ls, a, em, mult


def _lru_fwd(proj, cw, cb, wa, ba, wx, bx, lam):
    t = proj.shape[0]
    w = D_MODEL
    tm = LRU_TM

    def body(gb_ref, up_ref, cw_ref, cb_ref, wa_ref, ba_ref, wx_ref, bx_ref, lam_ref,
             y_ref, u_ref, r_ref, i_ref, h_ref, tail_sc, hprev_sc):
        @pl.when(pl.program_id(0) == 0)
        def _():
            tail_sc[...] = jnp.zeros_like(tail_sc)
            hprev_sc[...] = jnp.zeros_like(hprev_sc)

        up = up_ref[...]
        ext = jnp.concatenate([tail_sc[...], up], axis=0)
        u = cb_ref[...] + cw_ref[CONV_WIDTH - 1:CONV_WIDTH, :] * up
        for s in range(1, CONV_WIDTH):
            u = u + cw_ref[CONV_WIDTH - 1 - s:CONV_WIDTH - s, :] * pltpu.roll(ext, s, 0)[8:8 + tm]
        tail_sc[...] = up[tm - 8:tm]
        ub = u.astype(BF16)
        r = _sigmoid(_block_diag_dot(ub, wa_ref, (((1,), (0,)), ((), ()))) + ba_ref[...])
        ig = _sigmoid(_block_diag_dot(ub, wx_ref, (((1,), (0,)), ((), ()))) + bx_ref[...])
        _, a, _, mult = _lru_gates(u, r, ig, lam_ref[...])
        acum, hloc = _scan_rows(a, mult * ig * u)
        h = hloc + acum * hprev_sc[0:1, :]
        hprev_sc[0:1, :] = h[tm - 1:tm]
        u_ref[...] = u
        r_ref[...] = r
        i_ref[...] = ig
        h_ref[...] = h
        gel, _ = _gelu(gb_ref[...])
        y_ref[...] = (h * gel).astype(BF16)

    vec = _full((1, w))
    wspec = _full((LRU_BLOCKS, LRU_BLOCK, LRU_BLOCK))
    return pl.pallas_call(
        body, name="lru_fwd", grid=(t // tm,),
        in_specs=[pl.BlockSpec((tm, w), lambda i: (i, 0)), pl.BlockSpec((tm, w), lambda i: (i, 1)),
                  _full((CONV_WIDTH, w)), vec, wspec, vec, wspec, vec, vec],
        out_specs=[_rows(tm, w)] * 5,
        out_shape=[_sds((t, w), BF16)] + [_sds((t, w), F32)] * 4,
        scratch_shapes=[pltpu.VMEM((8, w), F32), pltpu.VMEM((8, w), F32)],
        compiler_params=_params("arbitrary"),
    )(proj, proj, cw, cb, wa, ba, wx, bx, lam)


def _lru_bwd(proj, cw, wa, wx, lam, u, r, ig, h, dy):
    t = proj.shape[0]
    w = D_MODEL
    tm = LRU_TM
    nt = t // tm

    def body(gb_ref, up_ref, cw_ref, wa_ref, wx_ref, lam_ref, u_ref, r_ref, i_ref, h_ref, hp_ref, dy_ref,
             dp_ref, dcw_ref, dcb_ref, dwa_ref, dba_ref, dwx_ref, dbx_ref, dlam_ref, carry_sc, dutail_sc, dls_sc):
        step = pl.program_id(0)

        @pl.when(step == 0)
        def _():
            carry_sc[...] = jnp.zeros_like(carry_sc)
            dutail_sc[...] = jnp.zeros_like(dutail_sc)
            dls_sc[...] = jnp.zeros_like(dls_sc)
            for ref in (dcw_ref, dcb_ref, dwa_ref, dba_ref, dwx_ref, dbx_ref):
                ref[...] = jnp.zeros_like(ref)

        row = lax.broadcasted_iota(jnp.int32, (tm, w), 0)
        u_t, r_t, i_t, h_t = u_ref[...], r_ref[...], i_ref[...], h_ref[...]
        ls, a, em, mult = _lru_gates(u_t, r_t, i_t, lam_ref[...])
        gb = gb_ref[...]
        gel, th = _gelu(gb)
        dyv = dy_ref[...]
        dgb = dyv * h_t * (0.5 * (1.0 + th) + 0.5 * gb * (1.0 - th * th) * GELU_K * (1.0 + 3.0 * GELU_C * gb * gb))
        dh = jnp.where(row == tm - 1, dyv * gel + carry_sc[0:1, :], dyv * gel)
        a_next = jnp.where(row < tm - 1, pltpu.roll(a, tm - 1, 0), 0.0)
        _, g = _scan_rows(a_next, dh, reverse=True)
        carry_sc[0:1, :] = a[0:1] * g[0:1]
        has_prev = jnp.where(step == nt - 1, 0.0, 1.0)
        h_prev = jnp.where(row >= 1, pltpu.roll(h_t, 1, 0), hp_ref[7:8, :] * has_prev)
        dmult = g * i_t * u_t
        dig = g * mult * u_t
        du = g * mult * i_t
        dla = g * h_prev * a - dmult * (1.0 - em) / mult
        dls_sc[0:1, :] += jnp.sum(dla * r_t, axis=0, keepdims=True) * LRU_C
        dpa = dla * (LRU_C * ls) * r_t * (1.0 - r_t)
        dpx = dig * i_t * (1.0 - i_t)
        dba_ref[...] += jnp.sum(dpa, axis=0, keepdims=True)
        dbx_ref[...] += jnp.sum(dpx, axis=0, keepdims=True)
        ub = u_t.astype(BF16)
        dpab = dpa.astype(BF16)
        dpxb = dpx.astype(BF16)
        for n in range(LRU_BLOCKS):
            cs = slice(LRU_BLOCK * n, LRU_BLOCK * (n + 1))
            dwa_ref[n] += lax.dot_general(ub[:, cs], dpab[:, cs], TN_DIMS, preferred_element_type=F32)
            dwx_ref[n] += lax.dot_general(ub[:, cs], dpxb[:, cs], TN_DIMS, preferred_element_type=F32)
        du = du + _block_diag_dot(dpab, wa_ref, NT_DIMS) + _block_diag_dot(dpxb, wx_ref, NT_DIMS)
        dcb_ref[...] += jnp.sum(du, axis=0, keepdims=True)
        ext = jnp.concatenate([du, dutail_sc[...]], axis=0)
        up = up_ref[...]
        dup = cw_ref[CONV_WIDTH - 1:CONV_WIDTH, :] * du
        dcw_ref[CONV_WIDTH - 1:CONV_WIDTH, :] += jnp.sum(up * du, axis=0, keepdims=True)
        for s in range(1, CONV_WIDTH):
            du_s = pltpu.roll(ext, tm + 8 - s, 0)[0:tm]
            dup = dup + cw_ref[CONV_WIDTH - 1 - s:CONV_WIDTH - s, :] * du_s
            dcw_ref[CONV_WIDTH - 1 - s:CONV_WIDTH - s, :] += jnp.sum(up * du_s, axis=0, keepdims=True)
        dutail_sc[...] = du[0:8]
        dp_ref[:, :w] = dgb.astype(BF16)
        dp_ref[:, w:] = dup.astype(BF16)

        @pl.when(step == nt - 1)
        def _():
            dlam_ref[...] = dls_sc[0:1, :] * _sigmoid(-lam_ref[...])

    rev = lambda col: (lambda i: (nt - 1 - i, col))
    vec = _full((1, w))
    wspec = _full((LRU_BLOCKS, LRU_BLOCK, LRU_BLOCK))
    tile = pl.BlockSpec((tm, w), rev(0))
    prev8 = pl.BlockSpec((8, w), lambda i: (jnp.maximum((nt - 1 - i) * (tm // 8) - 1, 0), 0))
    return pl.pallas_call(
        body, name="lru_bwd", grid=(nt,),
        in_specs=[tile, pl.BlockSpec((tm, w), rev(1)), _full((CONV_WIDTH, w)), wspec, wspec, vec,
                  tile, tile, tile, tile, prev8, tile],
        out_specs=[pl.BlockSpec((tm, 2 * w), rev(0)), _full((CONV_WIDTH, w)), vec, wspec, vec, wspec, vec, vec],
        out_shape=[_sds((t, 2 * w), BF16), _sds((CONV_WIDTH, w), F32), _sds((1, w), F32),
                   _sds((LRU_BLOCKS, LRU_BLOCK, LRU_BLOCK), F32), _sds((1, w), F32),
                   _sds((LRU_BLOCKS, LRU_BLOCK, LRU_BLOCK), F32), _sds((1, w), F32), _sds((1, w), F32)],
        scratch_shapes=[pltpu.VMEM((8, w), F32), pltpu.VMEM((8, w), F32), pltpu.VMEM((8, w), F32)],
        compiler_params=_params("arbitrary"),
    )(proj, proj, cw, wa, wx, lam, u, r, ig, h, h, dy)


def _ffn_fwd(h, g_pre, w_gu, w_down, g_post, layer):
    z, gu, y, out = _ffn_forward(h, g_pre, w_gu, w_down, g_post, f"ffn{layer}_fwd")
    return out, (h, z, gu, y)


def _ffn_bwd(dout, saved, g_pre, w_gu, w_down, g_post, layer):
    h, z, gu, y = saved
    dy, dgu, act, dg_post = _bwd_out(dout, y, g_post, w_down, gu, f"ffn{layer}_bwd_down")
    dw_down = _matmul_tn(act, dy, True, 256, f"ffn{layer}_dw_down")
    dw_gu = _matmul_tn(z, dgu, False, 512, f"ffn{layer}_dw_up")
    dh, dg_pre = _bwd_in(dgu, w_gu, h, g_pre, dout, f"ffn{layer}_bwd_up")
    return dh, dw_gu, dw_down, dg_pre, dg_post


def _local_step(x, target, wts):
    npm, nqm, npf, nqf = wts["norm_pre_mix"], wts["norm_post_mix"], wts["norm_pre_ffn"], wts["norm_post_ffn"]
    row = lambda a, i: a[i:i + 1]
    z0, qkv, og = _norm_matmul(x, row(npm, 0), wts["ml_w_in"], 2 * ML_QK + ML_V, "ml_in")
    y0, hs, cst, nst, mst = _mlstm_fwd(qkv, og, wts["ml_b_if"], wts["ml_head_norm"])
    ymix0, h1 = _matmul_postnorm(y0, wts["ml_w_out"], row(nqm, 0), x, "ml_out")
    h2, ffn0 = _ffn_fwd(h1, row(npf, 0), wts["ffn_w_gu"][0], wts["ffn_w_down"][0], row(nqf, 0), 0)
    z1, proj1 = _norm_matmul(h2, row(npm, 1), wts["lru_w_in"], 0, "lru_in")
    y1, u, r, ig, hl = _lru_fwd(proj1, wts["lru_conv_w"], wts["lru_conv_b"], wts["lru_w_gate_a"], wts["lru_b_gate_a"],
                                wts["lru_w_gate_x"], wts["lru_b_gate_x"], wts["lru_lambda"])
    ymix1, h3 = _matmul_postnorm(y1, wts["lru_w_out"], row(nqm, 1), h2, "lru_out")
    h4, ffn1 = _ffn_fwd(h3, row(npf, 1), wts["ffn_w_gu"][1], wts["ffn_w_down"][1], row(nqf, 1), 1)
    dh4, loss = _loss_head(h4, target, "loss_head")
    g = {}
    dh3, dgu1, dwd1, dpf1, dqf1 = _ffn_bwd(dh4, ffn1, row(npf, 1), wts["ffn_w_gu"][1], wts["ffn_w_down"][1], row(nqf, 1), 1)
    dymix1, dy1, dqm1 = _bwd_out(dh3, ymix1, row(nqm, 1), wts["lru_w_out"], None, "lru_bwd_out")
    g["lru_w_out"] = _matmul_tn(y1, dymix1, False, 256, "lru_dw_out")
    (dproj1, g["lru_conv_w"], g["lru_conv_b"], g["lru_w_gate_a"], g["lru_b_gate_a"], g["lru_w_gate_x"],
     g["lru_b_gate_x"], g["lru_lambda"]) = _lru_bwd(proj1, wts["lru_conv_w"], wts["lru_w_gate_a"], wts["lru_w_gate_x"],
                                                     wts["lru_lambda"], u, r, ig, hl, dy1)
    g["lru_w_in"] = _matmul_tn(z1, dproj1, False, 512, "lru_dw_in")
    dh2, dpm1 = _bwd_in(dproj1, wts["lru_w_in"], h2, row(npm, 1), dh3, "lru_bwd_in")
    dh1, dgu0, dwd0, dpf0, dqf0 = _ffn_bwd(dh2, ffn0, row(npf, 0), wts["ffn_w_gu"][0], wts["ffn_w_down"][0], row(nqf, 0), 0)
    dymix0, dy0, dqm0 = _bwd_out(dh1, ymix0, row(nqm, 0), wts["ml_w_out"], None, "ml_bwd_out")
    g["ml_w_out"] = _matmul_tn(y0, dymix0, False, 256, "ml_dw_out")
    dproj0, g["ml_head_norm"], g["ml_b_if"] = _mlstm_bwd(qkv, og, wts["ml_b_if"], wts["ml_head_norm"], hs, cst, nst, mst, dy0)
    g["ml_w_in"] = _matmul_tn(z0, dproj0, False, 640, "ml_dw_in")
    dx, dpm0 = _bwd_in(dproj0, wts["ml_w_in"], x, row(npm, 0), dh1, "ml_bwd_in")
    g["ffn_w_gu"] = jnp.stack([dgu0, dgu1])
    g["ffn_w_down"] = jnp.stack([dwd0, dwd1])
    g["norm_pre_mix"] = jnp.concatenate([dpm0, dpm1], axis=0)
    g["norm_post_mix"] = jnp.concatenate([dqm0, dqm1], axis=0)
    g["norm_pre_ffn"] = jnp.concatenate([dpf0, dpf1], axis=0)
    g["norm_post_ffn"] = jnp.concatenate([dqf0, dqf1], axis=0)
    return loss, dx, g


LANES = 1024
HALF_FFN = D_FF // N_CHIPS
PARTS = (
    ("ml_w_in", 1024, ML_IN // N_CHIPS),
    ("ml_w_out", 256, 1024),
    ("lru_w_in", 1024, 512),
    ("lru_gates", 2 * LRU_BLOCKS * 64, LRU_BLOCK),
    ("lru_w_out", 256, 1024),
    ("ffn_gu", 2 * 1024, 2 * HALF_FFN),
    ("ffn_down", 2 * HALF_FFN, 1024),
)
N_PARTS = len(PARTS)

SMALL_ROWS = 24
ROW_LOSS, ROW_HEAD_NORM, ROW_B_IF, ROW_LRU = 8, 9, 10, 11


def _row_tile(rows, cols, itemsize, budget=3 << 19):
    best = 16
    for t in range(16, rows + 1, 16):
        if rows % t == 0 and t * cols * itemsize <= budget:
            best = t
    return best


def _chip_peers():
    x, y, c = lax.axis_index("x"), lax.axis_index("y"), lax.axis_index("c")
    return x, y, c, [(1 - x, y), (x, 1 - y), (1 - x, 1 - y)]


HBM_SPEC = pl.BlockSpec(memory_space=pltpu.HBM)


def _remote(src, dst, send_sems, recv_sems, k, to):
    return pltpu.make_async_remote_copy(src_ref=src, dst_ref=dst, send_sem=send_sems.at[k], recv_sem=recv_sems.at[k],
                                        device_id=to, device_id_type=MESH)


def _gather_weights(parts, small):
    per_part = 7

    def body(*refs):
        part_refs, small_ref = refs[:N_PARTS], refs[N_PARTS]
        out_refs, outs_ref = refs[N_PARTS + 1:2 * N_PARTS + 1], refs[2 * N_PARTS + 1]
        send_sems, recv_sems, loc_sem = refs[2 * N_PARTS + 2:]
        x, y, c, chips = _chip_peers()
        me = 2 * x + y
        sib = (x, y, 1 - c)
        copy = functools.partial(_remote, send_sems=send_sems, recv_sems=recv_sems)
        local = pltpu.make_async_copy(small_ref, outs_ref.at[me], loc_sem.at[0])
        local.start()
        sent = []
        for p, (_, rows, _) in enumerate(PARTS):
            mine = pl.ds(c * (rows // 2), rows // 2)
            for j, (cx, cy) in enumerate(chips):
                sent.append(copy(part_refs[p].at[mine], out_refs[p].at[me, mine], k=per_part * p + j, to=(cx, cy, c)))
        for j, (cx, cy) in enumerate(chips):
            sent.append(copy(small_ref, outs_ref.at[me], k=per_part * N_PARTS + j, to=(cx, cy, c)))
        for p in range(N_PARTS):
            sent.append(copy(part_refs[p], out_refs[p].at[me], k=per_part * p + 6, to=sib))
        for cp in sent:
            cp.start()
        for p, (_, rows, _) in enumerate(PARTS):
            mine = pl.ds(c * (rows // 2), rows // 2)
            for j, (cx, cy) in enumerate(chips):
                land = out_refs[p].at[2 * cx + cy, mine]
                copy(land, land, k=per_part * p + j, to=sib).wait_recv()
                sent.append(copy(land, land, k=per_part * p + 3 + j, to=sib))
                sent[-1].start()
        for p, (_, rows, _) in enumerate(PARTS):
            theirs = pl.ds((1 - c) * (rows // 2), rows // 2)
            copy(part_refs[p], out_refs[p].at[me], k=per_part * p + 6, to=sib).wait_recv()
            for j, (cx, cy) in enumerate(chips):
                land = out_refs[p].at[2 * cx + cy, theirs]
                copy(land, land, k=per_part * p + 3 + j, to=sib).wait_recv()
        for j, (cx, cy) in enumerate(chips):
            copy(small_ref, outs_ref.at[2 * cx + cy], k=per_part * N_PARTS + j, to=sib).wait_recv()
        for cp in sent:
            cp.wait_send()
        local.wait()

    n_sems = per_part * N_PARTS + 3
    return pl.pallas_call(
        body, name="gather_weights",
        in_specs=[HBM_SPEC] * (N_PARTS + 1), out_specs=[HBM_SPEC] * (N_PARTS + 1),
        out_shape=[_sds((N_CHIPS,) + a.shape, a.dtype) for a in parts] + [_sds((N_CHIPS,) + small.shape, small.dtype)],
        scratch_shapes=[pltpu.SemaphoreType.DMA((n_sems,)), pltpu.SemaphoreType.DMA((n_sems,)),
                        pltpu.SemaphoreType.DMA((1,))],
    )(*parts, small)


def _exchange_halves(gparts):
    def body(*refs):
        g_refs, out_refs = refs[:N_PARTS], refs[N_PARTS:2 * N_PARTS]
        send_sems, recv_sems = refs[2 * N_PARTS:]
        x, y, c, _ = _chip_peers()
        sent = []
        for p, (_, rows, _) in enumerate(PARTS):
            theirs = pl.ds((1 - c) * (rows // 2), rows // 2)
            sent.append(_remote(g_refs[p].at[:, theirs], out_refs[p], send_sems, recv_sems, p, (x, y, 1 - c)))
            sent[-1].start()
        for cp in sent:
            cp.wait()

    return pl.pallas_call(
        body, name="exchange_halves", in_specs=[HBM_SPEC] * N_PARTS, out_specs=[HBM_SPEC] * N_PARTS,
        out_shape=[_sds((N_CHIPS, rows // 2, cols), BF16) for _, rows, cols in PARTS],
        scratch_shapes=[pltpu.SemaphoreType.DMA((N_PARTS,)), pltpu.SemaphoreType.DMA((N_PARTS,))],
    )(*gparts)


def _add_halves(g, recv, pos, name):
    _, half, cols = recv.shape
    tr = _row_tile(half, cols, 2)
    tiles = half // tr

    def body(pos_ref, a_ref, b_ref, o_ref):
        o_ref[...] = (a_ref[...].astype(F32) + b_ref[...].astype(F32)).astype(BF16)

    spec = pl.BlockSpec((1, tr, cols), lambda k, i, pos_ref: (k, i, 0))
    return pl.pallas_call(
        body, name=name,
        grid_spec=pltpu.PrefetchScalarGridSpec(
            num_scalar_prefetch=1, grid=(N_CHIPS, tiles),
            in_specs=[pl.BlockSpec((1, tr, cols), lambda k, i, pos_ref: (k, pos_ref[0] * tiles + i, 0)), spec],
            out_specs=spec),
        out_shape=_sds((N_CHIPS, half, cols), BF16),
        compiler_params=_params("parallel", "parallel"),
    )(pos, g, recv)


def _scatter_to_chips(s1, small):
    flips = [(fx, fy, fc) for fx in (0, 1) for fy in (0, 1) for fc in (0, 1)][1:]

    def body(*refs):
        s1_refs, small_ref = refs[:N_PARTS], refs[N_PARTS]
        recv_refs, all_ref = refs[N_PARTS + 1:2 * N_PARTS + 1], refs[2 * N_PARTS + 1]
        send_sems, recv_sems, loc_sem = refs[2 * N_PARTS + 2:]
        x, y, c, chips = _chip_peers()
        copy = functools.partial(_remote, send_sems=send_sems, recv_sems=recv_sems)
        my_slot = all_ref.at[4 * x + 2 * y + c]
        local = pltpu.make_async_copy(small_ref, my_slot, loc_sem.at[0])
        local.start()
        sent = [copy(s1_refs[p].at[2 * cx + cy], recv_refs[p].at[j], k=3 * p + j, to=(cx, cy, c))
                for p in range(N_PARTS) for j, (cx, cy) in enumerate(chips)]
        peers = [(1 - x if fx else x, 1 - y if fy else y, 1 - c if fc else c) for fx, fy, fc in flips]
        sent += [copy(small_ref, my_slot, k=3 * N_PARTS + n, to=p) for n, p in enumerate(peers)]
        for cp in sent:
            cp.start()
        for p in range(N_PARTS):
            for j in range(3):
                copy(s1_refs[p].at[0], recv_refs[p].at[j], k=3 * p + j, to=peers[0]).wait_recv()
        for n, (px, py, pc) in enumerate(peers):
            copy(small_ref, all_ref.at[4 * px + 2 * py + pc], k=3 * N_PARTS + n, to=peers[0]).wait_recv()
        for cp in sent:
            cp.wait_send()
        local.wait()

    n_sems = 3 * N_PARTS + 7
    return pl.pallas_call(
        body, name="scatter_to_chips", in_specs=[HBM_SPEC] * (N_PARTS + 1), out_specs=[HBM_SPEC] * (N_PARTS + 1),
        out_shape=[_sds((3,) + a.shape[1:], a.dtype) for a in s1] + [_sds((8,) + small.shape, small.dtype)],
        scratch_shapes=[pltpu.SemaphoreType.DMA((n_sems,)), pltpu.SemaphoreType.DMA((n_sems,)),
                        pltpu.SemaphoreType.DMA((1,))],
    )(*s1, small)


def _sum_chips(s1, recv, pos, name):
    _, half, cols = recv.shape
    tr = _row_tile(half, cols, 4)
    tiles = half // tr

    def body(pos_ref, a_ref, b_ref, o_ref):
        acc = a_ref[0].astype(F32)
        for j in range(3):
            acc = acc + b_ref[j].astype(F32)
        o_ref[...] = acc

    return pl.pallas_call(
        body, name=name,
        grid_spec=pltpu.PrefetchScalarGridSpec(
            num_scalar_prefetch=1, grid=(tiles,),
            in_specs=[pl.BlockSpec((1, tr, cols), lambda i, pos_ref: (pos_ref[1], i, 0)),
                      pl.BlockSpec((3, tr, cols), lambda i, pos_ref: (0, i, 0))],
            out_specs=pl.BlockSpec((tr, cols), lambda i, pos_ref: (pos_ref[0] * tiles + i, 0))),
        out_shape=_sds((2 * half, cols), F32),
        compiler_params=_params("parallel"),
    )(pos, s1, recv)


def _sum_small(small_all):
    def body(a_ref, o_ref):
        acc = a_ref[0]
        for d in range(1, 8):
            acc = acc + a_ref[d]
        o_ref[...] = acc

    return pl.pallas_call(body, name="sum_small", out_shape=_sds(small_all.shape[1:], F32))(small_all)


def _join_halves(s2):
    def body(*refs):
        buf_refs = refs[N_PARTS:2 * N_PARTS]
        send_sems, recv_sems = refs[2 * N_PARTS:]
        x, y, c, _ = _chip_peers()
        sent = []
        for p, (_, rows, _) in enumerate(PARTS):
            mine = buf_refs[p].at[pl.ds(c * (rows // 2), rows // 2)]
            sent.append(_remote(mine, mine, send_sems, recv_sems, p, (x, y, 1 - c)))
            sent[-1].start()
        for p, (_, rows, _) in enumerate(PARTS):
            theirs = buf_refs[p].at[pl.ds((1 - c) * (rows // 2), rows // 2)]
            _remote(theirs, theirs, send_sems, recv_sems, p, (x, y, 1 - c)).wait_recv()
        for cp in sent:
            cp.wait_send()

    return pl.pallas_call(
        body, name="join_halves", in_specs=[HBM_SPEC] * N_PARTS, out_specs=[HBM_SPEC] * N_PARTS,
        out_shape=[_sds(a.shape, a.dtype) for a in s2],
        input_output_aliases={p: p for p in range(N_PARTS)},
        scratch_shapes=[pltpu.SemaphoreType.DMA((N_PARTS,)), pltpu.SemaphoreType.DMA((N_PARTS,))],
    )(*s2)


def _adamw(w, g, m, v, name):
    rows, cols = w.shape
    tm = rows
    for cand in (512, 256, 128, 64, 32, 16, 8):
        if rows % cand == 0 and rows > cand:
            tm = cand
            break

    def body(w_ref, g_ref, m_ref, v_ref, d_ref, nm_ref, nv_ref):
        gv = g_ref[...]
        nm = ADAM_B1 * m_ref[...] + (1.0 - ADAM_B1) * gv
        nv = ADAM_B2 * v_ref[...] + (1.0 - ADAM_B2) * (gv * gv)
        m_hat = nm / (1.0 - ADAM_B1 ** ADAM_STEP)
        v_hat = nv / (1.0 - ADAM_B2 ** ADAM_STEP)
        d_ref[...] = -ADAM_LR * (m_hat / (jnp.sqrt(v_hat) + ADAM_EPS) + ADAM_WD * w_ref[...])
        nm_ref[...] = nm
        nv_ref[...] = nv

    spec = _rows(tm, cols)
    return pl.pallas_call(
        body, name=name, grid=(rows // tm,), in_specs=[spec] * 4, out_specs=[spec] * 3,
        out_shape=[_sds((rows, cols), F32)] * 3, compiler_params=_params("parallel"),
    )(w, g, m, v)


WEIGHTS = ("ml_w_in", "ml_b_if", "ml_head_norm", "ml_w_out", "lru_w_in", "lru_conv_w", "lru_conv_b", "lru_w_gate_a",
           "lru_b_gate_a", "lru_w_gate_x", "lru_b_gate_x", "lru_lambda", "lru_w_out", "norm_pre_mix", "norm_post_mix",
           "norm_pre_ffn", "norm_post_ffn", "ffn_w_gate", "ffn_w_up", "ffn_w_down")
LRU_VECTORS = ("lru_conv_b", "lru_b_gate_a", "lru_b_gate_x", "lru_lambda")
NORMS = ("norm_pre_mix", "norm_post_mix", "norm_pre_ffn", "norm_post_ffn")


def _full_weights(w):
    small = jnp.concatenate([w["lru_conv_w"][0]] + [w[n] for n in LRU_VECTORS], axis=0)
    bf = lambda a: a.astype(BF16)
    shards = dict(
        ml_w_in=bf(w["ml_w_in"][0]), ml_w_out=bf(w["ml_w_out"][0]), lru_w_in=bf(w["lru_w_in"][0]),
        lru_gates=bf(jnp.concatenate([w["lru_w_gate_a"][0], w["lru_w_gate_x"][0]], axis=0)).reshape(-1, LRU_BLOCK),
        lru_w_out=bf(w["lru_w_out"][0]),
        ffn_gu=bf(jnp.concatenate([w["ffn_w_gate"], w["ffn_w_up"]], axis=2)).reshape(-1, 2 * HALF_FFN),
        ffn_down=bf(w["ffn_w_down"]).reshape(-1, 1024))
    *gathered, smalls = _gather_weights([shards[n] for n, _, _ in PARTS], small)
    got = dict(zip([n for n, _, _ in PARTS], gathered))
    by_cols = lambda a: jnp.concatenate([a[k] for k in range(N_CHIPS)], axis=-1)
    gates = got["lru_gates"].reshape(N_CHIPS, 2, LRU_BLOCKS, 64, LRU_BLOCK).transpose(1, 2, 0, 3, 4)
    gates = gates.reshape(2, LRU_BLOCKS, LRU_BLOCK, LRU_BLOCK)
    gu = got["ffn_gu"].reshape(N_CHIPS, 2, 1024, 2 * HALF_FFN)
    down = got["ffn_down"].reshape(N_CHIPS, 2, HALF_FFN, 1024).transpose(1, 0, 2, 3)
    vec = by_cols(smalls)
    return dict(
        ml_w_in=jnp.pad(by_cols(got["ml_w_in"]), ((0, 0), (0, ML_IN_PAD - ML_IN))),
        ml_b_if=jnp.pad(w["ml_b_if"], ((0, 0), (0, 128 - 2 * ML_HEADS))),
        ml_head_norm=w["ml_head_norm"], ml_w_out=got["ml_w_out"].reshape(1024, 1024),
        lru_w_in=by_cols(got["lru_w_in"]),
        lru_conv_w=vec[0:4], lru_conv_b=vec[4:5], lru_b_gate_a=vec[5:6], lru_b_gate_x=vec[6:7], lru_lambda=vec[7:8],
        lru_w_gate_a=gates[0], lru_w_gate_x=gates[1], lru_w_out=got["lru_w_out"].reshape(1024, 1024),
        ffn_w_gu=jnp.concatenate([by_cols(gu[..., :HALF_FFN]), by_cols(gu[..., HALF_FFN:])], axis=-1),
        ffn_w_down=down.reshape(2, D_FF, 1024),
        **{n: w[n] for n in NORMS})


def _reduce_gradients(loss, g):
    x, y, c, _ = _chip_peers()
    me = 2 * x + y
    pos = jnp.stack([c, me])
    by_chip = lambda a, width: jnp.stack([a[..., k * width:(k + 1) * width] for k in range(N_CHIPS)])
    gu = g["ffn_w_gu"]
    gates = jnp.stack([g["lru_w_gate_a"], g["lru_w_gate_x"]]).astype(BF16)
    gparts = dict(
        ml_w_in=by_chip(g["ml_w_in"], ML_IN // N_CHIPS),
        ml_w_out=g["ml_w_out"].reshape(N_CHIPS, 256, 1024),
        lru_w_in=by_chip(g["lru_w_in"], 512),
        lru_gates=gates.reshape(2, LRU_BLOCKS, N_CHIPS, 64, LRU_BLOCK).transpose(2, 0, 1, 3, 4).reshape(N_CHIPS, -1, LRU_BLOCK),
        lru_w_out=g["lru_w_out"].reshape(N_CHIPS, 256, 1024),
        ffn_gu=jnp.concatenate([by_chip(gu[..., :D_FF], HALF_FFN), by_chip(gu[..., D_FF:], HALF_FFN)],
                               axis=-1).reshape(N_CHIPS, -1, 2 * HALF_FFN),
        ffn_down=g["ffn_w_down"].reshape(2, N_CHIPS, HALF_FFN, 1024).transpose(1, 0, 2, 3).reshape(N_CHIPS, -1, 1024))
    names = [n for n, _, _ in PARTS]
    pad_lanes = lambda a: jnp.pad(a, ((0, 0), (0, LANES - a.shape[1])))
    small = jnp.concatenate(
        [g[n] for n in NORMS] + [pad_lanes(loss), g["ml_head_norm"], pad_lanes(g["ml_b_if"]), g["lru_conv_w"]]
        + [g[n] for n in LRU_VECTORS] + [jnp.zeros((SMALL_ROWS - 19, LANES), F32)], axis=0)
    recv1 = _exchange_halves([gparts[n] for n in names])
    s1 = [_add_halves(gparts[n], r, pos, "add_halves_" + n) for n, r in zip(names, recv1)]
    *recv2, small_all = _scatter_to_chips(s1, small)
    s2 = [_sum_chips(a, r, pos, "sum_chips_" + n) for n, a, r in zip(names, s1, recv2)]
    red = dict(zip(names, _join_halves(s2)))
    gates = red["lru_gates"].reshape(2, LRU_BLOCKS, 64, LRU_BLOCK)
    gu = red["ffn_gu"].reshape(2, 1024, 2 * HALF_FFN)
    reduced = dict(ml_w_in=red["ml_w_in"], ml_w_out=red["ml_w_out"], lru_w_in=red["lru_w_in"],
                   lru_w_gate_a=gates[0], lru_w_gate_x=gates[1], lru_w_out=red["lru_w_out"],
                   ffn_w_gate=gu[..., :HALF_FFN], ffn_w_up=gu[..., HALF_FFN:],
                   ffn_w_down=red["ffn_down"].reshape(2, HALF_FFN, 1024))
    vec = _sum_small(small_all)
    cols = lambda a: lax.dynamic_slice_in_dim(a, me * 256, 256, axis=1)
    for i, n in enumerate(NORMS):
        reduced[n] = vec[2 * i:2 * i + 2]
    reduced["ml_head_norm"] = vec[ROW_HEAD_NORM:ROW_HEAD_NORM + 1]
    reduced["ml_b_if"] = vec[ROW_B_IF:ROW_B_IF + 1, :2 * ML_HEADS]
    reduced["lru_conv_w"] = cols(vec[ROW_LRU:ROW_LRU + 4])
    for i, n in enumerate(LRU_VECTORS):
        reduced[n] = cols(vec[ROW_LRU + 4 + i:ROW_LRU + 5 + i])
    return vec[ROW_LOSS, 0], reduced


def _update(w, grads, m, v):
    delta, new_m, new_v = {}, {}, {}
    for n in WEIGHTS:
        shape = w[n].shape
        as2d = lambda a: a.reshape(-1, shape[-1])
        d, nm, nv = _adamw(as2d(w[n]), as2d(grads[n]), as2d(m[n]), as2d(v[n]), "adamw_" + n)
        delta[n], new_m[n], new_v[n] = d.reshape(shape), nm.reshape(shape), nv.reshape(shape)
    return delta, new_m, new_v


def kernel(x, ml_w_in, ml_b_if, ml_head_norm, ml_w_out, lru_w_in, lru_conv_w, lru_conv_b, lru_w_gate_a, lru_b_gate_a, lru_w_gate_x, lru_b_gate_x, lru_lambda, lru_w_out, norm_pre_mix, norm_post_mix, norm_pre_ffn, norm_post_ffn, ffn_w_gate, ffn_w_up, ffn_w_down, loss_target, m_ml_w_in, m_ml_b_if, m_ml_head_norm, m_ml_w_out, m_lru_w_in, m_lru_conv_w, m_lru_conv_b, m_lru_w_gate_a, m_lru_b_gate_a, m_lru_w_gate_x, m_lru_b_gate_x, m_lru_lambda, m_lru_w_out, m_norm_pre_mix, m_norm_post_mix, m_norm_pre_ffn, m_norm_post_ffn, m_ffn_w_gate, m_ffn_w_up, m_ffn_w_down, v_ml_w_in, v_ml_b_if, v_ml_head_norm, v_ml_w_out, v_lru_w_in, v_lru_conv_w, v_lru_conv_b, v_lru_w_gate_a, v_lru_b_gate_a, v_lru_w_gate_x, v_lru_b_gate_x, v_lru_lambda, v_lru_w_out, v_norm_pre_mix, v_norm_post_mix, v_norm_pre_ffn, v_norm_post_ffn, v_ffn_w_gate, v_ffn_w_up, v_ffn_w_down):
    args = locals()
    w = {n: args[n] for n in WEIGHTS}
    m = {n: args["m_" + n] for n in WEIGHTS}
    v = {n: args["v_" + n] for n in WEIGHTS}
    loss_part, dx, g = _local_step(x[0], loss_target[0], _full_weights(w))
    loss, grads = _reduce_gradients(loss_part, g)
    grads = {n: grads[n].reshape(w[n].shape) for n in WEIGHTS}
    delta, new_m, new_v = _update(w, grads, m, v)
    return (loss, dx[None], *[grads[n] for n in WEIGHTS], *[delta[n] for n in WEIGHTS],
            *[new_m[n] for n in WEIGHTS], *[new_v[n] for n in WEIGHTS])
```

```python
import functools
import math
from typing import Callable, NamedTuple

import jax
import jax.numpy as jnp
from jax import lax
from jax.experimental import pallas as pl
from jax.experimental.pallas import tpu as pltpu

F32 = jnp.float32
BF16 = jnp.bfloat16
MESH = pl.DeviceIdType.MESH

D_MODEL = 1024
D_FF = 2816
ML_HEADS = 8
ML_DK = 64
ML_DV = 128
ML_QK = ML_HEADS * ML_DK
ML_V = ML_HEADS * ML_DV
ML_IN = 2 * ML_QK + 2 * ML_V + 2 * ML_HEADS
ML_IN_PAD = 3200
CHUNK = 64
GATE_CAP = 15.0
ML_M_INIT = -1e30
NEG_BIG = -1e30
LRU_BLOCKS = 4
LRU_BLOCK = 256
CONV_WIDTH = 4
LRU_C = 8.0
EPS = 1e-6
QK_SCALE = ML_DK ** -0.5

ADAM_LR = 0.001
ADAM_B1 = 0.9
ADAM_B2 = 0.999
ADAM_EPS = 1e-08
ADAM_WD = 0.01
ADAM_STEP = 10

N_CHIPS = 4
V7X_VMEM_LIMIT = 56 * 1024 * 1024

NT_DIMS = (((1,), (1,)), ((), ()))
TN_DIMS = (((0,), (0,)), ((), ()))


def _params(*semantics):
    return pltpu.CompilerParams(dimension_semantics=semantics, vmem_limit_bytes=V7X_VMEM_LIMIT)


def _sds(shape, dtype):
    return jax.ShapeDtypeStruct(shape, dtype)


def _full(shape):
    return pl.BlockSpec(shape, lambda *_: (0,) * len(shape))


def _rows(tm, n):
    return pl.BlockSpec((tm, n), lambda i: (i, 0))


def _sigmoid(x):
    return 1.0 / (1.0 + jnp.exp(-x))


def _log_sigmoid(x):
    return jnp.minimum(x, 0.0) - jnp.log1p(jnp.exp(-jnp.abs(x)))


def _rms(x):
    r = lax.rsqrt(jnp.mean(x * x, axis=-1, keepdims=True) + EPS)
    return x * r, r


def _rms_bwd(xhat, r, g, dy):
    dxh = dy * g
    return r * (dxh - xhat * jnp.mean(dxh * xhat, axis=-1, keepdims=True))


def _scan_rows(a, b, reverse=False):
    n = a.shape[0]
    row = lax.broadcasted_iota(jnp.int32, a.shape, 0)
    s = 1
    while s < n:
        if reverse:
            keep = row < n - s
            shift = n - s
        else:
            keep = row >= s
            shift = s
        b = b + a * jnp.where(keep, pltpu.roll(b, shift, 0), 0.0)
        a = a * jnp.where(keep, pltpu.roll(a, shift, 0), 1.0)
        s *= 2
    return a, b


def _cumsum_rows(x, reverse=False):
    n = x.shape[0]
    row = lax.broadcasted_iota(jnp.int32, x.shape, 0)
    s = 1
    while s < n:
        if reverse:
            x = x + jnp.where(row < n - s, pltpu.roll(x, n - s, 0), 0.0)
        else:
            x = x + jnp.where(row >= s, pltpu.roll(x, s, 0), 0.0)
        s *= 2
    return x


def _norm_matmul(h, g, w, n_bf16, name):
    t, d = h.shape
    n = w.shape[1]
    tm = min(512, t)

    def body(h_ref, g_ref, w_ref, z_ref, *o_refs):
        xhat, _ = _rms(h_ref[...])
        z = (xhat * g_ref[...]).astype(BF16)
        z_ref[...] = z
        out = jnp.dot(z, w_ref[...], preferred_element_type=F32)
        if n_bf16:
            o_refs[0][...] = out[:, :n_bf16].astype(BF16)
            o_refs[1][...] = out[:, n_bf16:]
        else:
            o_refs[0][...] = out

    if n_bf16:
        out_specs = [_rows(tm, d), _rows(tm, n_bf16), _rows(tm, n - n_bf16)]
        out_shape = [_sds((t, d), BF16), _sds((t, n_bf16), BF16), _sds((t, n - n_bf16), F32)]
    else:
        out_specs = [_rows(tm, d), _rows(tm, n)]
        out_shape = [_sds((t, d), BF16), _sds((t, n), F32)]
    return pl.pallas_call(
        body, name=name, grid=(t // tm,),
        in_specs=[_rows(tm, d), _full((1, d)), _full((d, n))],
        out_specs=out_specs, out_shape=out_shape, compiler_params=_params("parallel"),
    )(h, g, w)


def _matmul_postnorm(a, w, g, res, name):
    t = res.shape[0]
    k, d = w.shape
    tm = 256

    def body(a_ref, w_ref, g_ref, res_ref, y_ref, o_ref):
        y = jnp.dot(a_ref[...], w_ref[...], preferred_element_type=F32)
        y_ref[...] = y
        yhat, _ = _rms(y)
        o_ref[...] = res_ref[...] + yhat * g_ref[...]

    return pl.pallas_call(
        body, name=name, grid=(t // tm,),
        in_specs=[_rows(tm, k), _full((k, d)), _full((1, d)), _rows(tm, d)],
        out_specs=[_rows(tm, d), _rows(tm, d)],
        out_shape=[_sds((t, d), F32), _sds((t, d), F32)],
        compiler_params=_params("parallel"),
    )(a, w, g, res)


class _Side(NamedTuple):
    inputs: list
    out_shape: list
    n_sems: int
    start: Callable
    finish: Callable


def _side_call(body, side, *, name, steps, in_specs, out_specs, out_shape, scratch_shapes, args):
    n_in, n_out, n_scr = len(in_specs), len(out_specs), len(scratch_shapes)
    if side is None:
        outs = pl.pallas_call(
            body, name=name, grid=(steps,), in_specs=in_specs, out_specs=out_specs, out_shape=out_shape,
            scratch_shapes=scratch_shapes, compiler_params=_params("arbitrary"))(*args)
        return list(outs), []
    s_in, s_out = len(side.inputs), len(side.out_shape)

    def carrying(*refs):
        ins, side_ins = refs[:n_in], refs[n_in:n_in + s_in]
        outs = refs[n_in + s_in:n_in + s_in + n_out]
        side_outs = refs[n_in + s_in + n_out:n_in + s_in + n_out + s_out]
        scratch = refs[n_in + s_in + n_out + s_out:]
        own, sems = scratch[:n_scr], scratch[n_scr:]

        @pl.when(pl.program_id(0) == 0)
        def _():
            side.start(side_ins, side_outs, *sems)

        body(*ins, *outs, *own)

        @pl.when(pl.program_id(0) == steps - 1)
        def _():
            side.finish(side_ins, side_outs, *sems)

    outs = pl.pallas_call(
        carrying, name=name, grid=(steps,), in_specs=list(in_specs) + [HBM_SPEC] * s_in,
        out_specs=list(out_specs) + [HBM_SPEC] * s_out, out_shape=list(out_shape) + list(side.out_shape),
        scratch_shapes=list(scratch_shapes) + [pltpu.SemaphoreType.DMA((side.n_sems,))] * 2,
        compiler_params=_params("arbitrary"))(*args, *side.inputs)
    return list(outs[:n_out]), list(outs[n_out:])


def _ffn_forward(h, g_pre, w_gu, w_down, g_post, name, side):
    t, d = h.shape
    k = w_down.shape[0]
    tm = 256

    def body(h_ref, gpre_ref, wgu_ref, wd_ref, gpost_ref, z_ref, gu_ref, y_ref, o_ref):
        hv = h_ref[...]
        xhat, _ = _rms(hv)
        z = (xhat * gpre_ref[...]).astype(BF16)
        z_ref[...] = z
        gu = jnp.dot(z, wgu_ref[...], preferred_element_type=F32).astype(BF16)
        gu_ref[...] = gu
        gate = gu[:, :k].astype(F32)
        act = (gate * _sigmoid(gate) * gu[:, k:].astype(F32)).astype(BF16)
        y = jnp.dot(act, wd_ref[...], preferred_element_type=F32)
        y_ref[...] = y
        yhat, _ = _rms(y)
        o_ref[...] = hv + yhat * gpost_ref[...]

    return _side_call(
        body, side, name=name, steps=t // tm,
        in_specs=[_rows(tm, d), _full((1, d)), _full((d, 2 * k)), _full((k, d)), _full((1, d))],
        out_specs=[_rows(tm, d), _rows(tm, 2 * k), _rows(tm, d), _rows(tm, d)],
        out_shape=[_sds((t, d), BF16), _sds((t, 2 * k), BF16), _sds((t, d), F32), _sds((t, d), F32)],
        scratch_shapes=[], args=(h, g_pre, w_gu, w_down, g_post))


def _loss_head(h, target, name):
    t, d = h.shape
    tm = min(512, t)

    def body(h_ref, t_ref, dh_ref, l_ref):
        @pl.when(pl.program_id(0) == 0)
        def _():
            l_ref[...] = jnp.zeros_like(l_ref)

        err = h_ref[...] - t_ref[...]
        dh_ref[...] = err * (1.0 / d)
        part = jnp.sum(jnp.sum(err * err, axis=1, keepdims=True), axis=0, keepdims=True) * (0.5 / d)
        l_ref[...] += jnp.broadcast_to(part, l_ref.shape)

    return pl.pallas_call(
        body, name=name, grid=(t // tm,),
        in_specs=[_rows(tm, d), _rows(tm, d)],
        out_specs=[_rows(tm, d), _full((1, 128))],
        out_shape=[_sds((t, d), F32), _sds((1, 128), F32)],
        compiler_params=_params("arbitrary"),
    )(h, target)


def _bwd_out(dout, y, g, w, gu, name, side):
    t, d = dout.shape
    k = w.shape[0]
    ffn = gu is not None
    tm = 128 if ffn else 256

    def body(*refs):
        if ffn:
            dout_ref, y_ref, g_ref, w_ref, gu_ref, dy_ref, dact_ref, act_ref, dg_ref = refs
        else:
            dout_ref, y_ref, g_ref, w_ref, dy_ref, dact_ref, dg_ref = refs

        @pl.when(pl.program_id(0) == 0)
        def _():
            dg_ref[...] = jnp.zeros_like(dg_ref)

        do = dout_ref[...]
        yhat, r = _rms(y_ref[...])
        dg_ref[...] += jnp.sum(do * yhat, axis=0, keepdims=True)
        dy = _rms_bwd(yhat, r, g_ref[...], do).astype(BF16)
        dy_ref[...] = dy
        da = lax.dot_general(dy, w_ref[...], NT_DIMS, preferred_element_type=F32)
        if ffn:
            gate = gu_ref[:, :k].astype(F32)
            up = gu_ref[:, k:].astype(F32)
            sg = _sigmoid(gate)
            silu = gate * sg
            act_ref[...] = (silu * up).astype(BF16)
            dact_ref[:, :k] = (da * up * (sg * (1.0 + gate * (1.0 - sg)))).astype(BF16)
            dact_ref[:, k:] = (da * silu).astype(BF16)
        else:
            dact_ref[...] = da

    in_specs = [_rows(tm, d), _rows(tm, d), _full((1, d)), _full((k, d))]
    args = [dout, y, g, w]
    if ffn:
        in_specs.append(_rows(tm, 2 * k))
        args.append(gu)
        out_specs = [_rows(tm, d), _rows(tm, 2 * k), _rows(tm, k), _full((1, d))]
        out_shape = [_sds((t, d), BF16), _sds((t, 2 * k), BF16), _sds((t, k), BF16), _sds((1, d), F32)]
    else:
        out_specs = [_rows(tm, d), _rows(tm, k), _full((1, d))]
        out_shape = [_sds((t, d), BF16), _sds((t, k), F32), _sds((1, d), F32)]
    return _side_call(body, side, name=name, steps=t // tm, in_specs=in_specs, out_specs=out_specs,
                      out_shape=out_shape, scratch_shapes=[], args=args)


def _bwd_in(dp, w, h, g, dout, name):
    t, d = h.shape
    n = w.shape[1]
    tm = 256

    def body(dp_ref, w_ref, h_ref, g_ref, dout_ref, dh_ref, dg_ref):
        @pl.when(pl.program_id(0) == 0)
        def _():
            dg_ref[...] = jnp.zeros_like(dg_ref)

        dz = lax.dot_general(dp_ref[...], w_ref[...], NT_DIMS, preferred_element_type=F32)
        hhat, r = _rms(h_ref[...])
        dg_ref[...] += jnp.sum(dz * hhat, axis=0, keepdims=True)
        dh_ref[...] = dout_ref[...] + _rms_bwd(hhat, r, g_ref[...], dz)

    return pl.pallas_call(
        body, name=name, grid=(t // tm,),
        in_specs=[_rows(tm, n), _full((d, n)), _rows(tm, d), _full((1, d)), _rows(tm, d)],
        out_specs=[_rows(tm, d), _full((1, d))],
        out_shape=[_sds((t, d), F32), _sds((1, d), F32)],
        compiler_params=_params("arbitrary"),
    )(dp, w, h, g, dout)


def _matmul_tn(a, b, tile_a, tile, name):
    t, ka = a.shape
    nb = b.shape[1]

    def body(a_ref, b_ref, o_ref):
        o_ref[...] = lax.dot_general(a_ref[...], b_ref[...], TN_DIMS, preferred_element_type=F32).astype(BF16)

    if tile_a:
        grid = (ka // tile,)
        in_specs = [pl.BlockSpec((t, tile), lambda i: (0, i)), _full((t, nb))]
        out_specs = pl.BlockSpec((tile, nb), lambda i: (i, 0))
    else:
        grid = (nb // tile,)
        in_specs = [_full((t, ka)), pl.BlockSpec((t, tile), lambda i: (0, i))]
        out_specs = pl.BlockSpec((ka, tile), lambda i: (0, i))
    return pl.pallas_call(
        body, name=name, grid=grid, in_specs=in_specs, out_specs=out_specs,
        out_shape=_sds((ka, nb), BF16), compiler_params=_params("parallel"),
    )(a, b)


def _ml_gate_prep(gt, bif):
    th = jnp.tanh((gt + bif) / GATE_CAP)
    act = GATE_CAP * th
    cum = _cumsum_rows(_log_sigmoid(act))
    lane = lax.broadcasted_iota(jnp.int32, gt.shape, 1)
    x = jnp.where(lane < ML_HEADS, act, cum)
    return x, x.T, th, act


def _ml_head_fwd(h, q, k, v, kt, x, xt, c_in, n_in, m_in, causal):
    qh = q[:, ML_DK * h:ML_DK * (h + 1)]
    kh = k[:, ML_DK * h:ML_DK * (h + 1)]
    vh = v[:, ML_DV * h:ML_DV * (h + 1)]
    kth = kt[ML_DK * h:ML_DK * (h + 1), :]
    s = jnp.dot(qh, kth, preferred_element_type=F32) * QK_SCALE
    bcol = x[:, ML_HEADS + h:ML_HEADS + h + 1]
    licol = x[:, h:h + 1]
    brow = xt[ML_HEADS + h:ML_HEADS + h + 1, :]
    lirow = xt[h:h + 1, :]
    dmat = jnp.where(causal, bcol - brow + lirow, NEG_BIG)
    inter = bcol + m_in
    mt = jnp.maximum(jnp.max(dmat, axis=1, keepdims=True), inter)
    wt = jnp.exp(dmat - mt)
    p = wt * s
    winter = jnp.exp(inter - mt)
    cb = c_in.astype(BF16)
    qc = jnp.dot(qh, cb, preferred_element_type=F32) * QK_SCALE
    qf = qh.astype(F32)
    qn = jnp.sum(qf * n_in, axis=1, keepdims=True) * QK_SCALE
    den = jnp.sum(p, axis=1, keepdims=True) + winter * qn
    emt = jnp.exp(-mt)
    nrm = jnp.maximum(jnp.abs(den), emt)
    gtot = bcol[CHUNK - 1:CHUNK, :]
    a_col = gtot - bcol + licol
    a_row = gtot - brow + lirow
    m_new = jnp.maximum(gtot + m_in, jnp.max(a_row, axis=1, keepdims=True))
    decay = jnp.exp(gtot + m_in - m_new)
    wkf = jnp.exp(a_col - m_new)
    kw = kh.astype(F32) * wkf
    ktw = (kth.astype(F32) * jnp.exp(a_row - m_new)).astype(BF16)
    return dict(qh=qh, kh=kh, vh=vh, s=s, wt=wt, p=p, winter=winter, cb=cb, qc=qc, qf=qf, qn=qn, den=den,
                emt=emt, nrm=nrm, m_new=m_new, decay=decay, wkf=wkf, kw=kw, ktw=ktw, inter=inter,
                bcol=bcol, licol=licol, brow=brow, lirow=lirow)


def _ml_specs(nc, rev):
    def at(col):
        if rev:
            return lambda i: (nc - 1 - i, col)
        return lambda i: (i, col)

    return [pl.BlockSpec((CHUNK, ML_QK), at(0)), pl.BlockSpec((CHUNK, ML_QK), at(1)),
            pl.BlockSpec((CHUNK, ML_V), at(1)), pl.BlockSpec((CHUNK, ML_V), at(0)),
            pl.BlockSpec((CHUNK, 128), at(ML_V // 128))]


def _mlstm_fwd(qkv, og, bif, hn, side):
    t = qkv.shape[0]
    nc = t // CHUNK

    def body(q_ref, k_ref, v_ref, o_ref, gt_ref, bif_ref, hn_ref, y_ref, hs_ref, cst_ref, nst_ref, mst_ref,
             c_sc, n_sc, m_sc):
        @pl.when(pl.program_id(0) == 0)
        def _():
            c_sc[...] = jnp.zeros_like(c_sc)
            n_sc[...] = jnp.zeros_like(n_sc)
            m_sc[...] = jnp.full_like(m_sc, ML_M_INIT)

        cst_ref[0] = c_sc[...]
        nst_ref[0] = n_sc[...]
        mst_ref[0] = m_sc[...]
        x, xt, _, _ = _ml_gate_prep(gt_ref[...], bif_ref[...])
        q, k, v = q_ref[...], k_ref[...], v_ref[...]
        kt = k.T
        causal = (lax.broadcasted_iota(jnp.int32, (CHUNK, CHUNK), 0)
                  >= lax.broadcasted_iota(jnp.int32, (CHUNK, CHUNK), 1))
        c_all, n_all, m_all = c_sc[...], n_sc[...], m_sc[...]
        for h in range(ML_HEADS):
            vs = slice(ML_DV * h, ML_DV * (h + 1))
            c_in = c_all[ML_DK * h:ML_DK * (h + 1), :]
            n_in = n_all[h:h + 1, :]
            m_in = m_all[h:h + 1, 0:1]
            f = _ml_head_fwd(h, q, k, v, kt, x, xt, c_in, n_in, m_in, causal)
            num = jnp.dot(f["p"].astype(BF16), f["vh"], preferred_element_type=F32) + f["winter"] * f["qc"]
            hh = num / f["nrm"]
            hs_ref[:, vs] = hh
            hhat, _ = _rms(hh)
            y_ref[:, vs] = (hhat * hn_ref[:, vs] * _sigmoid(o_ref[:, vs])).astype(BF16)
            c_sc[ML_DK * h:ML_DK * (h + 1), :] = f["decay"] * c_in + jnp.dot(
                f["ktw"], f["vh"], preferred_element_type=F32)
            n_sc[h:h + 1, :] = f["decay"] * n_in + jnp.sum(f["kw"], axis=0, keepdims=True)
            m_sc[h:h + 1, :] = jnp.broadcast_to(f["m_new"], (1, 128))

    return _side_call(
        body, side, name="mlstm_fwd", steps=nc,
        in_specs=_ml_specs(nc, False) + [_full((1, 128)), _full((1, ML_V))],
        out_specs=[_rows(CHUNK, ML_V), _rows(CHUNK, ML_V),
                   pl.BlockSpec((1, ML_HEADS * ML_DK, ML_DV), lambda i: (i, 0, 0)),
                   pl.BlockSpec((1, ML_HEADS, ML_DK), lambda i: (i, 0, 0)),
                   pl.BlockSpec((1, ML_HEADS, 128), lambda i: (i, 0, 0))],
        out_shape=[_sds((t, ML_V), BF16), _sds((t, ML_V), F32),
                   _sds((nc, ML_HEADS * ML_DK, ML_DV), F32), _sds((nc, ML_HEADS, ML_DK), F32),
                   _sds((nc, ML_HEADS, 128), F32)],
        scratch_shapes=[pltpu.VMEM((ML_HEADS * ML_DK, ML_DV), F32), pltpu.VMEM((ML_HEADS, ML_DK), F32),
                        pltpu.VMEM((ML_HEADS, 128), F32)],
        args=(qkv, qkv, qkv, og, og, bif, hn))


def _mlstm_bwd(qkv, og, bif, hn, hs, cst, nst, mst, dy, side):
    t = qkv.shape[0]
    nc = t // CHUNK

    def body(q_ref, k_ref, v_ref, o_ref, gt_ref, bif_ref, hn_ref, hs_ref, cst_ref, nst_ref, mst_ref, dy_ref,
             dp_ref, dhn_ref, dbif_ref, dc_sc, dct_sc, dn_sc):
        @pl.when(pl.program_id(0) == 0)
        def _():
            dc_sc[...] = jnp.zeros_like(dc_sc)
            dct_sc[...] = jnp.zeros_like(dct_sc)
            dn_sc[...] = jnp.zeros_like(dn_sc)
            dhn_ref[...] = jnp.zeros_like(dhn_ref)
            dbif_ref[...] = jnp.zeros_like(dbif_ref)

        x, xt, th, act = _ml_gate_prep(gt_ref[...], bif_ref[...])
        q, k, v = q_ref[...], k_ref[...], v_ref[...]
        qt, kt, vt = q.T, k.T, v.T
        rows = lax.broadcasted_iota(jnp.int32, (CHUNK, CHUNK), 0)
        cols = lax.broadcasted_iota(jnp.int32, (CHUNK, CHUNK), 1)
        causal = rows >= cols
        lane = lax.broadcasted_iota(jnp.int32, (CHUNK, 128), 1)
        row = lax.broadcasted_iota(jnp.int32, (CHUNK, 1), 0)
        as_row = lambda col: jnp.sum(jnp.where(rows == cols, col, 0.0), axis=0, keepdims=True)
        dx = jnp.zeros((CHUNK, 128), F32)
        dc_all, dct_all, dn_all = dc_sc[...], dct_sc[...], dn_sc[...]
        for h in range(ML_HEADS):
            ks = slice(ML_DK * h, ML_DK * (h + 1))
            vs = slice(ML_DV * h, ML_DV * (h + 1))
            c_in = cst_ref[0, ks, :]
            n_in = nst_ref[0, h:h + 1, :]
            m_in = mst_ref[0, h:h + 1, 0:1]
            f = _ml_head_fwd(h, q, k, v, kt, x, xt, c_in, n_in, m_in, causal)
            qh, kh, vh, p, winter = f["qh"], f["kh"], f["vh"], f["p"], f["winter"]
            qth = qt[ks, :]
            vth = vt[vs, :]
            dmat_t = jnp.where(rows <= cols, f["brow"] - f["bcol"] + f["licol"], NEG_BIG)
            inter_row = f["brow"] + m_in
            mt_row = jnp.maximum(jnp.max(dmat_t, axis=0, keepdims=True), inter_row)
            wt_t = jnp.exp(dmat_t - mt_row)
            p_t = wt_t * (jnp.dot(kh, qth, preferred_element_type=F32) * QK_SCALE)
            winter_row = jnp.exp(inter_row - mt_row)
            hh = hs_ref[:, vs]
            hhat, r = _rms(hh)
            hn_h = hn_ref[:, vs]
            sg = _sigmoid(o_ref[:, vs])
            dyh = dy_ref[:, vs]
            dyn = dyh * sg
            dp_ref[:, 2 * ML_QK + ML_V + ML_DV * h:2 * ML_QK + ML_V + ML_DV * (h + 1)] = (
                dyh * hhat * hn_h * sg * (1.0 - sg)).astype(BF16)
            dhn_ref[:, vs] += jnp.sum(dyn * hhat, axis=0, keepdims=True)
            dh = _rms_bwd(hhat, r, hn_h, dyn)
            inv = 1.0 / f["nrm"]
            dnum = dh * inv
            dnrm = -jnp.sum(dh * hh, axis=1, keepdims=True) * inv
            dden = jnp.where(jnp.abs(f["den"]) > f["emt"], dnrm * jnp.sign(f["den"]), 0.0)
            dnb = dnum.astype(BF16)
            dnt = dnum.T.astype(BF16)
            rmat = jnp.dot(dnb, vth, preferred_element_type=F32) + dden
            rmat_t = jnp.dot(vh, dnt, preferred_element_type=F32) + as_row(dden)
            ds = (f["wt"] * rmat).astype(BF16)
            ds_t = (wt_t * rmat_t).astype(BF16)
            dv = jnp.dot(p_t.astype(BF16), dnb, preferred_element_type=F32)
            dqs = jnp.dot(ds, kh, preferred_element_type=F32) + winter * (
                jnp.dot(dnb, c_in.T.astype(BF16), preferred_element_type=F32) + dden * n_in)
            dk = jnp.dot(ds_t, qh, preferred_element_type=F32) * QK_SCALE
            dinter = (jnp.sum(f["qc"] * dnum, axis=1, keepdims=True) + f["qn"] * dden) * winter
            wq = winter * f["qf"] * QK_SCALE
            wq_t = (winter_row * qth.astype(F32) * QK_SCALE).astype(BF16)
            dc_loc = jnp.dot(wq_t, dnb, preferred_element_type=F32)
            dct_loc = jnp.dot(dnt, wq.astype(BF16), preferred_element_type=F32)
            dn_loc = jnp.sum(wq * dden, axis=0, keepdims=True)
            cs_q = jnp.sum(p_t * rmat_t, axis=1, keepdims=True)
            db = jnp.sum(p * rmat, axis=1, keepdims=True) + dinter - cs_q
            dli = cs_q
            dcn = dc_all[ks, :]
            dcn_t = dct_all[vs, :]
            dnn = dn_all[h:h + 1, :]
            ddecay = (jnp.sum(jnp.sum(dcn * c_in, axis=1, keepdims=True), axis=0, keepdims=True)
                      + jnp.sum(dnn * n_in, axis=1, keepdims=True))
            dkw = jnp.dot(vh, dcn_t.astype(BF16), preferred_element_type=F32) + dnn
            dk = dk + f["wkf"] * dkw
            da = jnp.sum(dkw * kh.astype(F32), axis=1, keepdims=True) * f["wkf"]
            dv = dv + jnp.dot(f["kw"].astype(BF16), dcn.astype(BF16), preferred_element_type=F32)
            dgtot = jnp.sum(da, axis=0, keepdims=True) + ddecay * f["decay"]
            db = db - da + jnp.where(row == CHUNK - 1, dgtot, 0.0)
            dli = dli + da
            dc_sc[ks, :] = f["decay"] * dcn + dc_loc
            dct_sc[vs, :] = f["decay"] * dcn_t + dct_loc
            dn_sc[h:h + 1, :] = f["decay"] * dnn + dn_loc
            dp_ref[:, ks] = (dqs * QK_SCALE).astype(BF16)
            dp_ref[:, ML_QK + ML_DK * h:ML_QK + ML_DK * (h + 1)] = dk.astype(BF16)
            dp_ref[:, 2 * ML_QK + ML_DV * h:2 * ML_QK + ML_DV * (h + 1)] = dv.astype(BF16)
            dx = jnp.where(lane == h, dli, dx)
            dx = jnp.where(lane == ML_HEADS + h, db, dx)
        dlf = _cumsum_rows(dx, reverse=True)
        dact = jnp.where(lane < ML_HEADS, dx, dlf * _sigmoid(-act))
        dz = dact * (1.0 - th * th)
        dp_ref[:, 2 * ML_QK + 2 * ML_V:] = dz.astype(BF16)
        dbif_ref[...] += jnp.sum(dz, axis=0, keepdims=True)

    rev = lambda i: (nc - 1 - i, 0)
    rev3 = lambda i: (nc - 1 - i, 0, 0)
    return _side_call(
        body, side, name="mlstm_bwd", steps=nc,
        in_specs=_ml_specs(nc, True) + [
            _full((1, 128)), _full((1, ML_V)), pl.BlockSpec((CHUNK, ML_V), rev),
            pl.BlockSpec((1, ML_HEADS * ML_DK, ML_DV), rev3), pl.BlockSpec((1, ML_HEADS, ML_DK), rev3),
            pl.BlockSpec((1, ML_HEADS, 128), rev3), pl.BlockSpec((CHUNK, ML_V), rev)],
        out_specs=[pl.BlockSpec((CHUNK, ML_IN_PAD), rev), _full((1, ML_V)), _full((1, 128))],
        out_shape=[_sds((t, ML_IN_PAD), BF16), _sds((1, ML_V), F32), _sds((1, 128), F32)],
        scratch_shapes=[pltpu.VMEM((ML_HEADS * ML_DK, ML_DV), F32), pltpu.VMEM((ML_HEADS * ML_DV, ML_DK), F32),
                        pltpu.VMEM((ML_HEADS, ML_DK), F32)],
        args=(qkv, qkv, qkv, og, og, bif, hn, hs, cst, nst, mst, dy))


LRU_TM = 256
GELU_K = math.sqrt(2.0 / math.pi)
GELU_C = 0.044715


def _gelu(x):
    th = jnp.tanh(GELU_K * (x + GELU_C * x * x * x))
    return 0.5 * x * (1.0 + th), th


def _neg_expm1(x):
    series = -x * (1.0 + x * (0.5 + x * (1.0 / 6.0 + x * (1.0 / 24.0))))
    return jnp.where(x > -0.05, series, 1.0 - jnp.exp(x))


def _block_diag_dot(a, w_ref, dims):
    parts = [lax.dot_general(a[:, LRU_BLOCK * n:LRU_BLOCK * (n + 1)], w_ref[n], dims, preferred_element_type=F32)
             for n in range(LRU_BLOCKS)]
    return jnp.concatenate(parts, axis=1)


def _lru_gates(u, r, ig, lam):
    ls = _log_sigmoid(lam)
    la = LRU_C * r * ls
    a = jnp.exp(la)
    em = _neg_expm1(2.0 * la)
    mult = jnp.sqrt(em)
    return ls, a, em, mult


def _lru_fwd(proj, cw, cb, wa, ba, wx, bx, lam):
    t = proj.shape[0]
    w = D_MODEL
    tm = min(LRU_TM, t)

    def body(gb_ref, up_ref, cw_ref, cb_ref, wa_ref, ba_ref, wx_ref, bx_ref, lam_ref,
             y_ref, u_ref, r_ref, i_ref, h_ref, tail_sc, hprev_sc):
        @pl.when(pl.program_id(0) == 0)
        def _():
            tail_sc[...] = jnp.zeros_like(tail_sc)
            hprev_sc[...] = jnp.zeros_like(hprev_sc)

        up = up_ref[...]
        ext = jnp.concatenate([tail_sc[...], up], axis=0)
        u = cb_ref[...] + cw_ref[CONV_WIDTH - 1:CONV_WIDTH, :] * up
        for s in range(1, CONV_WIDTH):
            u = u + cw_ref[CONV_WIDTH - 1 - s:CONV_WIDTH - s, :] * pltpu.roll(ext, s, 0)[8:8 + tm]
        tail_sc[...] = up[tm - 8:tm]
        ub = u.astype(BF16)
        r = _sigmoid(_block_diag_dot(ub, wa_ref, (((1,), (0,)), ((), ()))) + ba_ref[...])
        ig = _sigmoid(_block_diag_dot(ub, wx_ref, (((1,), (0,)), ((), ()))) + bx_ref[...])
        _, a, _, mult = _lru_gates(u, r, ig, lam_ref[...])
        acum, hloc = _scan_rows(a, mult * ig * u)
        h = hloc + acum * hprev_sc[0:1, :]
        hprev_sc[0:1, :] = h[tm - 1:tm]
        u_ref[...] = u
        r_ref[...] = r
        i_ref[...] = ig
        h_ref[...] = h
        gel, _ = _gelu(gb_ref[...])
        y_ref[...] = (h * gel).astype(BF16)

    vec = _full((1, w))
    wspec = _full((LRU_BLOCKS, LRU_BLOCK, LRU_BLOCK))
    return pl.pallas_call(
        body, name="lru_fwd", grid=(t // tm,),
        in_specs=[pl.BlockSpec((tm, w), lambda i: (i, 0)), pl.BlockSpec((tm, w), lambda i: (i, 1)),
                  _full((CONV_WIDTH, w)), vec, wspec, vec, wspec, vec, vec],
        out_specs=[_rows(tm, w)] * 5,
        out_shape=[_sds((t, w), BF16)] + [_sds((t, w), F32)] * 4,
        scratch_shapes=[pltpu.VMEM((8, w), F32), pltpu.VMEM((8, w), F32)],
        compiler_params=_params("arbitrary"),
    )(proj, proj, cw, cb, wa, ba, wx, bx, lam)


def _lru_bwd(proj, cw, wa, wx, lam, u, r, ig, h, dy, side):
    t = proj.shape[0]
    w = D_MODEL
    tm = min(LRU_TM, t)
    nt = t // tm

    def body(gb_ref, up_ref, cw_ref, wa_ref, wx_ref, lam_ref, u_ref, r_ref, i_ref, h_ref, hp_ref, dy_ref,
             dp_ref, dcw_ref, dcb_ref, dwa_ref, dba_ref, dwx_ref, dbx_ref, dlam_ref, carry_sc, dutail_sc, dls_sc):
        step = pl.program_id(0)

        @pl.when(step == 0)
        def _():
            carry_sc[...] = jnp.zeros_like(carry_sc)
            dutail_sc[...] = jnp.zeros_like(dutail_sc)
            dls_sc[...] = jnp.zeros_like(dls_sc)
            for ref in (dcw_ref, dcb_ref, dwa_ref, dba_ref, dwx_ref, dbx_ref):
                ref[...] = jnp.zeros_like(ref)

        row = lax.broadcasted_iota(jnp.int32, (tm, w), 0)
        u_t, r_t, i_t, h_t = u_ref[...], r_ref[...], i_ref[...], h_ref[...]
        ls, a, em, mult = _lru_gates(u_t, r_t, i_t, lam_ref[...])
        gb = gb_ref[...]
        gel, th = _gelu(gb)
        dyv = dy_ref[...]
        dgb = dyv * h_t * (0.5 * (1.0 + th) + 0.5 * gb * (1.0 - th * th) * GELU_K * (1.0 + 3.0 * GELU_C * gb * gb))
        dh = jnp.where(row == tm - 1, dyv * gel + carry_sc[0:1, :], dyv * gel)
        a_next = jnp.where(row < tm - 1, pltpu.roll(a, tm - 1, 0), 0.0)
        _, g = _scan_rows(a_next, dh, reverse=True)
        carry_sc[0:1, :] = a[0:1] * g[0:1]
        has_prev = jnp.where(step == nt - 1, 0.0, 1.0)
        h_prev = jnp.where(row >= 1, pltpu.roll(h_t, 1, 0), hp_ref[7:8, :] * has_prev)
        dmult = g * i_t * u_t
        dig = g * mult * u_t
        du = g * mult * i_t
        dla = g * h_prev * a - dmult * (1.0 - em) / mult
        dls_sc[0:1, :] += jnp.sum(dla * r_t, axis=0, keepdims=True) * LRU_C
        dpa = dla * (LRU_C * ls) * r_t * (1.0 - r_t)
        dpx = dig * i_t * (1.0 - i_t)
        dba_ref[...] += jnp.sum(dpa, axis=0, keepdims=True)
        dbx_ref[...] += jnp.sum(dpx, axis=0, keepdims=True)
        ub = u_t.astype(BF16)
        dpab = dpa.astype(BF16)
        dpxb = dpx.astype(BF16)
        for n in range(LRU_BLOCKS):
            cs = slice(LRU_BLOCK * n, LRU_BLOCK * (n + 1))
            dwa_ref[n] += lax.dot_general(ub[:, cs], dpab[:, cs], TN_DIMS, preferred_element_type=F32)
            dwx_ref[n] += lax.dot_general(ub[:, cs], dpxb[:, cs], TN_DIMS, preferred_element_type=F32)
        du = du + _block_diag_dot(dpab, wa_ref, NT_DIMS) + _block_diag_dot(dpxb, wx_ref, NT_DIMS)
        dcb_ref[...] += jnp.sum(du, axis=0, keepdims=True)
        ext = jnp.concatenate([du, dutail_sc[...]], axis=0)
        up = up_ref[...]
        dup = cw_ref[CONV_WIDTH - 1:CONV_WIDTH, :] * du
        dcw_ref[CONV_WIDTH - 1:CONV_WIDTH, :] += jnp.sum(up * du, axis=0, keepdims=True)
        for s in range(1, CONV_WIDTH):
            du_s = pltpu.roll(ext, tm + 8 - s, 0)[0:tm]
            dup = dup + cw_ref[CONV_WIDTH - 1 - s:CONV_WIDTH - s, :] * du_s
            dcw_ref[CONV_WIDTH - 1 - s:CONV_WIDTH - s, :] += jnp.sum(up * du_s, axis=0, keepdims=True)
        dutail_sc[...] = du[0:8]
        dp_ref[:, :w] = dgb.astype(BF16)
        dp_ref[:, w:] = dup.astype(BF16)

        @pl.when(step == nt - 1)
        def _():
            dlam_ref[...] = dls_sc[0:1, :] * _sigmoid(-lam_ref[...])

    rev = lambda col: (lambda i: (nt - 1 - i, col))
    vec = _full((1, w))
    wspec = _full((LRU_BLOCKS, LRU_BLOCK, LRU_BLOCK))
    tile = pl.BlockSpec((tm, w), rev(0))
    prev8 = pl.BlockSpec((8, w), lambda i: (jnp.maximum((nt - 1 - i) * (tm // 8) - 1, 0), 0))
    return _side_call(
        body, side, name="lru_bwd", steps=nt,
        in_specs=[tile, pl.BlockSpec((tm, w), rev(1)), _full((CONV_WIDTH, w)), wspec, wspec, vec,
                  tile, tile, tile, tile, prev8, tile],
        out_specs=[pl.BlockSpec((tm, 2 * w), rev(0)), _full((CONV_WIDTH, w)), vec, wspec, vec, wspec, vec, vec],
        out_shape=[_sds((t, 2 * w), BF16), _sds((CONV_WIDTH, w), F32), _sds((1, w), F32),
                   _sds((LRU_BLOCKS, LRU_BLOCK, LRU_BLOCK), F32), _sds((1, w), F32),
                   _sds((LRU_BLOCKS, LRU_BLOCK, LRU_BLOCK), F32), _sds((1, w), F32), _sds((1, w), F32)],
        scratch_shapes=[pltpu.VMEM((8, w), F32), pltpu.VMEM((8, w), F32), pltpu.VMEM((8, w), F32)],
        args=(proj, proj, cw, wa, wx, lam, u, r, ig, h, h, dy))


LANES = 1024
HALF_FFN = D_FF // N_CHIPS
GROUPS = {
    "ml": (("ml_w_in", 1024, ML_IN // N_CHIPS), ("ml_w_out", 256, 1024)),
    "lru": (("lru_w_in", 1024, 512),
            ("lru_gates", 2 * LRU_BLOCKS * 64, LRU_BLOCK),
            ("lru_w_out", 256, 1024)),
    "ffn0": (("ffn_gu0", 1024, 2 * HALF_FFN), ("ffn_down0", HALF_FFN, 1024)),
    "ffn1": (("ffn_gu1", 1024, 2 * HALF_FFN), ("ffn_down1", HALF_FFN, 1024)),
}

SMALL_ROWS = 24
ROW_LOSS, ROW_HEAD_NORM, ROW_B_IF, ROW_LRU = 8, 9, 10, 11


def _row_tile(rows, cols, itemsize, budget=3 << 19):
    best = 16
    for t in range(16, rows + 1, 16):
        if rows % t == 0 and t * cols * itemsize <= budget:
            best = t
    return best


def _chip_peers():
    x, y, c = lax.axis_index("x"), lax.axis_index("y"), lax.axis_index("c")
    return x, y, c, [(1 - x, y), (x, 1 - y), (1 - x, 1 - y)]


HBM_SPEC = pl.BlockSpec(memory_space=pltpu.HBM)


def _remote(src, dst, send_sems, recv_sems, k, to):
    return pltpu.make_async_remote_copy(src_ref=src, dst_ref=dst, send_sem=send_sems.at[k], recv_sem=recv_sems.at[k],
                                        device_id=to, device_id_type=MESH)


GATHER_SEMS = 7


def _gather_copies(kind, specs, part_refs, out_refs, send_sems, recv_sems):
    x, y, c, chips = _chip_peers()
    me = 2 * x + y
    sib = (x, y, 1 - c)
    copy = functools.partial(_remote, send_sems=send_sems, recv_sems=recv_sems)
    out = []
    for p, (_, rows, _) in enumerate(specs):
        mine = pl.ds(c * (rows // 2), rows // 2)
        theirs = pl.ds((1 - c) * (rows // 2), rows // 2)
        base = GATHER_SEMS * p
        for j, (cx, cy) in enumerate(chips):
            land = out_refs[p].at[2 * cx + cy, mine]
            other = out_refs[p].at[2 * cx + cy, theirs]
            if kind == "first":
                out.append(copy(part_refs[p].at[mine], out_refs[p].at[me, mine], k=base + j, to=(cx, cy, c)))
            elif kind == "landed":
                out.append(copy(land, land, k=base + j, to=sib))
            elif kind == "forward":
                out.append(copy(land, land, k=base + 3 + j, to=sib))
            else:
                out.append(copy(other, other, k=base + 3 + j, to=sib))
        if kind in ("first", "arriving"):
            out.append(copy(part_refs[p], out_refs[p].at[me], k=base + 6, to=sib))
    return out


def _gather_start(specs, *refs):
    for cp in _gather_copies("first", specs, *refs):
        cp.start()


def _gather_finish(specs, *refs):
    forwards = _gather_copies("forward", specs, *refs)
    for land, fwd in zip(_gather_copies("landed", specs, *refs), forwards):
        land.wait_recv()
        fwd.start()
    for cp in _gather_copies("arriving", specs, *refs):
        cp.wait_recv()
    for cp in _gather_copies("first", specs, *refs) + forwards:
        cp.wait_send()


def _gather_shapes(specs):
    return [_sds((N_CHIPS, rows, cols), BF16) for _, rows, cols in specs]


def _gather_side(specs, parts):
    return _Side(inputs=list(parts), out_shape=_gather_shapes(specs), n_sems=GATHER_SEMS * len(specs),
                 start=functools.partial(_gather_start, specs), finish=functools.partial(_gather_finish, specs))


def _gather_weights(specs, parts, small):
    n = len(specs)

    def body(*refs):
        part_refs, small_ref = refs[:n], refs[n]
        out_refs, outs_ref = refs[n + 1:2 * n + 1], refs[2 * n + 1]
        send_sems, recv_sems, small_send, small_recv, loc_sem = refs[2 * n + 2:]
        x, y, c, chips = _chip_peers()
        me = 2 * x + y
        local = pltpu.make_async_copy(small_ref, outs_ref.at[me], loc_sem.at[0])
        local.start()
        _gather_start(specs, part_refs, out_refs, send_sems, recv_sems)
        sent = [_remote(small_ref, outs_ref.at[me], small_send, small_recv, j, (cx, cy, c))
                for j, (cx, cy) in enumerate(chips)]
        for cp in sent:
            cp.start()
        _gather_finish(specs, part_refs, out_refs, send_sems, recv_sems)
        for j, (cx, cy) in enumerate(chips):
            _remote(small_ref, outs_ref.at[2 * cx + cy], small_send, small_recv, j, (cx, cy, c)).wait_recv()
        for cp in sent:
            cp.wait_send()
        local.wait()

    dma = pltpu.SemaphoreType.DMA
    return pl.pallas_call(
        body, name="gather_weights",
        in_specs=[HBM_SPEC] * (n + 1), out_specs=[HBM_SPEC] * (n + 1),
        out_shape=_gather_shapes(specs) + [_sds((N_CHIPS,) + small.shape, small.dtype)],
        scratch_shapes=[dma((GATHER_SEMS * n,)), dma((GATHER_SEMS * n,)), dma((3,)), dma((3,)), dma((1,))],
    )(*parts, small)


def _exchange_halves(specs, gparts, name):
    n = len(specs)

    def body(*refs):
        g_refs, out_refs = refs[:n], refs[n:2 * n]
        send_sems, recv_sems = refs[2 * n:]
        x, y, c, _ = _chip_peers()
        sent = []
        for p, (_, rows, _) in enumerate(specs):
            theirs = pl.ds((1 - c) * (rows // 2), rows // 2)
            sent.append(_remote(g_refs[p].at[:, theirs], out_refs[p], send_sems, recv_sems, p, (x, y, 1 - c)))
            sent[-1].start()
        for cp in sent:
            cp.wait()

    return pl.pallas_call(
        body, name=name, in_specs=[HBM_SPEC] * n, out_specs=[HBM_SPEC] * n,
        out_shape=[_sds((N_CHIPS, rows // 2, cols), BF16) for _, rows, cols in specs],
        scratch_shapes=[pltpu.SemaphoreType.DMA((n,)), pltpu.SemaphoreType.DMA((n,))],
    )(*gparts)


def _add_halves(g, recv, pos, name):
    _, half, cols = recv.shape
    tr = _row_tile(half, cols, 2)
    tiles = half // tr

    def body(pos_ref, a_ref, b_ref, o_ref):
        o_ref[...] = (a_ref[...].astype(F32) + b_ref[...].astype(F32)).astype(BF16)

    spec = pl.BlockSpec((1, tr, cols), lambda k, i, pos_ref: (k, i, 0))
    return pl.pallas_call(
        body, name=name,
        grid_spec=pltpu.PrefetchScalarGridSpec(
            num_scalar_prefetch=1, grid=(N_CHIPS, tiles),
            in_specs=[pl.BlockSpec((1, tr, cols), lambda k, i, pos_ref: (k, pos_ref[0] * tiles + i, 0)), spec],
            out_specs=spec),
        out_shape=_sds((N_CHIPS, half, cols), BF16),
        compiler_params=_params("parallel", "parallel"),
    )(pos, g, recv)


def _scatter_copies(n, s1_refs, recv_refs, send_sems, recv_sems):
    x, y, c, chips = _chip_peers()
    return [_remote(s1_refs[p].at[2 * cx + cy], recv_refs[p].at[j], send_sems, recv_sems, 3 * p + j, (cx, cy, c))
            for p in range(n) for j, (cx, cy) in enumerate(chips)]


def _scatter_start(n, s1_refs, recv_refs, send_sems, recv_sems):
    for cp in _scatter_copies(n, s1_refs, recv_refs, send_sems, recv_sems):
        cp.start()


def _scatter_finish(n, s1_refs, recv_refs, send_sems, recv_sems):
    for cp in _scatter_copies(n, s1_refs, recv_refs, send_sems, recv_sems):
        cp.wait()


def _scatter_shapes(s1):
    return [_sds((3,) + a.shape[1:], a.dtype) for a in s1]


def _scatter_side(s1):
    n = len(s1)
    return _Side(inputs=list(s1), out_shape=_scatter_shapes(s1), n_sems=3 * n,
                 start=functools.partial(_scatter_start, n), finish=functools.partial(_scatter_finish, n))


def _scatter_to_chips(s1, small):
    n = len(s1)
    flips = [(fx, fy, fc) for fx in (0, 1) for fy in (0, 1) for fc in (0, 1)][1:]

    def body(*refs):
        s1_refs, small_ref = refs[:n], refs[n]
        recv_refs, all_ref = refs[n + 1:2 * n + 1], refs[2 * n + 1]
        send_sems, recv_sems, small_send, small_recv, loc_sem = refs[2 * n + 2:]
        x, y, c, _ = _chip_peers()
        my_slot = all_ref.at[4 * x + 2 * y + c]
        local = pltpu.make_async_copy(small_ref, my_slot, loc_sem.at[0])
        local.start()
        _scatter_start(n, s1_refs, recv_refs, send_sems, recv_sems)
        peers = [(1 - x if fx else x, 1 - y if fy else y, 1 - c if fc else c) for fx, fy, fc in flips]
        sent = [_remote(small_ref, my_slot, small_send, small_recv, i, p) for i, p in enumerate(peers)]
        for cp in sent:
            cp.start()
        _scatter_finish(n, s1_refs, recv_refs, send_sems, recv_sems)
        for i, (px, py, pc) in enumerate(peers):
            _remote(small_ref, all_ref.at[4 * px + 2 * py + pc], small_send, small_recv, i, peers[i]).wait_recv()
        for cp in sent:
            cp.wait_send()
        local.wait()

    dma = pltpu.SemaphoreType.DMA
    return pl.pallas_call(
        body, name="scatter_to_chips", in_specs=[HBM_SPEC] * (n + 1), out_specs=[HBM_SPEC] * (n + 1),
        out_shape=_scatter_shapes(s1) + [_sds((8,) + small.shape, small.dtype)],
        scratch_shapes=[dma((3 * n,)), dma((3 * n,)), dma((7,)), dma((7,)), dma((1,))],
    )(*s1, small)


def _sum_chips(s1, recv, pos, name):
    _, half, cols = recv.shape
    tr = _row_tile(half, cols, 4)
    tiles = half // tr

    def body(pos_ref, a_ref, b_ref, o_ref):
        acc = a_ref[0].astype(F32)
        for j in range(3):
            acc = acc + b_ref[j].astype(F32)
        o_ref[...] = acc

    return pl.pallas_call(
        body, name=name,
        grid_spec=pltpu.PrefetchScalarGridSpec(
            num_scalar_prefetch=1, grid=(tiles,),
            in_specs=[pl.BlockSpec((1, tr, cols), lambda i, pos_ref: (pos_ref[1], i, 0)),
                      pl.BlockSpec((3, tr, cols), lambda i, pos_ref: (0, i, 0))],
            out_specs=pl.BlockSpec((tr, cols), lambda i, pos_ref: (pos_ref[0] * tiles + i, 0))),
        out_shape=_sds((2 * half, cols), F32),
        compiler_params=_params("parallel"),
    )(pos, s1, recv)


def _sum_small(small_all):
    def body(a_ref, o_ref):
        acc = a_ref[0]
        for d in range(1, 8):
            acc = acc + a_ref[d]
        o_ref[...] = acc

    return pl.pallas_call(body, name="sum_small", out_shape=_sds(small_all.shape[1:], F32))(small_all)


def _join_halves(specs, s2):
    n = len(specs)

    def body(*refs):
        buf_refs = refs[n:2 * n]
        send_sems, recv_sems = refs[2 * n:]
        x, y, c, _ = _chip_peers()
        sent = []
        for p, (_, rows, _) in enumerate(specs):
            mine = buf_refs[p].at[pl.ds(c * (rows // 2), rows // 2)]
            sent.append(_remote(mine, mine, send_sems, recv_sems, p, (x, y, 1 - c)))
            sent[-1].start()
        for p, (_, rows, _) in enumerate(specs):
            theirs = buf_refs[p].at[pl.ds((1 - c) * (rows // 2), rows // 2)]
            _remote(theirs, theirs, send_sems, recv_sems, p, (x, y, 1 - c)).wait_recv()
        for cp in sent:
            cp.wait_send()

    return pl.pallas_call(
        body, name="join_halves", in_specs=[HBM_SPEC] * n, out_specs=[HBM_SPEC] * n,
        out_shape=[_sds(a.shape, a.dtype) for a in s2],
        input_output_aliases={p: p for p in range(n)},
        scratch_shapes=[pltpu.SemaphoreType.DMA((n,)), pltpu.SemaphoreType.DMA((n,))],
    )(*s2)


def _adamw(w, g, m, v, name):
    rows, cols = w.shape
    tm = rows
    for cand in (512, 256, 128, 64, 32, 16, 8):
        if rows % cand == 0 and rows > cand:
            tm = cand
            break

    def body(w_ref, g_ref, m_ref, v_ref, d_ref, nm_ref, nv_ref):
        gv = g_ref[...]
        nm = ADAM_B1 * m_ref[...] + (1.0 - ADAM_B1) * gv
        nv = ADAM_B2 * v_ref[...] + (1.0 - ADAM_B2) * (gv * gv)
        m_hat = nm / (1.0 - ADAM_B1 ** ADAM_STEP)
        v_hat = nv / (1.0 - ADAM_B2 ** ADAM_STEP)
        d_ref[...] = -ADAM_LR * (m_hat / (jnp.sqrt(v_hat) + ADAM_EPS) + ADAM_WD * w_ref[...])
        nm_ref[...] = nm
        nv_ref[...] = nv

    spec = _rows(tm, cols)
    return pl.pallas_call(
        body, name=name, grid=(rows // tm,), in_specs=[spec] * 4, out_specs=[spec] * 3,
        out_shape=[_sds((rows, cols), F32)] * 3, compiler_params=_params("parallel"),
    )(w, g, m, v)


WEIGHTS = ("ml_w_in", "ml_b_if", "ml_head_norm", "ml_w_out", "lru_w_in", "lru_conv_w", "lru_conv_b", "lru_w_gate_a",
           "lru_b_gate_a", "lru_w_gate_x", "lru_b_gate_x", "lru_lambda", "lru_w_out", "norm_pre_mix", "norm_post_mix",
           "norm_pre_ffn", "norm_post_ffn", "ffn_w_gate", "ffn_w_up", "ffn_w_down")
LRU_VECTORS = ("lru_conv_b", "lru_b_gate_a", "lru_b_gate_x", "lru_lambda")
NORMS = ("norm_pre_mix", "norm_post_mix", "norm_pre_ffn", "norm_post_ffn")


def _by_cols(a):
    return jnp.concatenate([a[k] for k in range(N_CHIPS)], axis=-1)


def _by_chip(a, width):
    return jnp.stack([a[..., k * width:(k + 1) * width] for k in range(N_CHIPS)])


def _weight_shards(w):
    bf = lambda a: a.astype(BF16)
    shards = dict(
        ml_w_in=bf(w["ml_w_in"][0]), ml_w_out=bf(w["ml_w_out"][0]), lru_w_in=bf(w["lru_w_in"][0]),
        lru_gates=bf(jnp.concatenate([w["lru_w_gate_a"][0], w["lru_w_gate_x"][0]], axis=0)).reshape(-1, LRU_BLOCK),
        lru_w_out=bf(w["lru_w_out"][0]))
    for layer in range(2):
        shards[f"ffn_gu{layer}"] = bf(jnp.concatenate([w["ffn_w_gate"][layer], w["ffn_w_up"][layer]], axis=1))
        shards[f"ffn_down{layer}"] = bf(w["ffn_w_down"][layer])
    return shards


def _ffn_weights(gu, down):
    w_gu = jnp.concatenate([_by_cols(gu[..., :HALF_FFN]), _by_cols(gu[..., HALF_FFN:])], axis=-1)
    return w_gu, down.reshape(D_FF, 1024)


def _ffn_grad_parts(dw_gu, dw_down):
    gu = jnp.concatenate([_by_chip(dw_gu[:, :D_FF], HALF_FFN), _by_chip(dw_gu[:, D_FF:], HALF_FFN)], axis=-1)
    return [gu, dw_down.reshape(N_CHIPS, HALF_FFN, 1024)]


def _half_reduce(group, gparts, pos):
    specs = GROUPS[group]
    recv = _exchange_halves(specs, gparts, "exchange_halves_" + group)
    return [_add_halves(g, r, pos, "add_halves_" + n) for (n, _, _), g, r in zip(specs, gparts, recv)]


def _sum_group(group, s1, recv, pos):
    return [_sum_chips(a, r, pos, "sum_chips_" + n) for (n, _, _), a, r in zip(GROUPS[group], s1, recv)]


def _update(w, grads, m, v):
    delta, new_m, new_v = {}, {}, {}
    for n in WEIGHTS:
        shape = w[n].shape
        as2d = lambda a: a.reshape(-1, shape[-1])
        d, nm, nv = _adamw(as2d(w[n]), as2d(grads[n]), as2d(m[n]), as2d(v[n]), "adamw_" + n)
        delta[n], new_m[n], new_v[n] = d.reshape(shape), nm.reshape(shape), nv.reshape(shape)
    return delta, new_m, new_v


def kernel(x, ml_w_in, ml_b_if, ml_head_norm, ml_w_out, lru_w_in, lru_conv_w, lru_conv_b, lru_w_gate_a, lru_b_gate_a, lru_w_gate_x, lru_b_gate_x, lru_lambda, lru_w_out, norm_pre_mix, norm_post_mix, norm_pre_ffn, norm_post_ffn, ffn_w_gate, ffn_w_up, ffn_w_down, loss_target, m_ml_w_in, m_ml_b_if, m_ml_head_norm, m_ml_w_out, m_lru_w_in, m_lru_conv_w, m_lru_conv_b, m_lru_w_gate_a, m_lru_b_gate_a, m_lru_w_gate_x, m_lru_b_gate_x, m_lru_lambda, m_lru_w_out, m_norm_pre_mix, m_norm_post_mix, m_norm_pre_ffn, m_norm_post_ffn, m_ffn_w_gate, m_ffn_w_up, m_ffn_w_down, v_ml_w_in, v_ml_b_if, v_ml_head_norm, v_ml_w_out, v_lru_w_in, v_lru_conv_w, v_lru_conv_b, v_lru_w_gate_a, v_lru_b_gate_a, v_lru_w_gate_x, v_lru_b_gate_x, v_lru_lambda, v_lru_w_out, v_norm_pre_mix, v_norm_post_mix, v_norm_pre_ffn, v_norm_post_ffn, v_ffn_w_gate, v_ffn_w_up, v_ffn_w_down):
    args = locals()
    w = {n: args[n] for n in WEIGHTS}
    m = {n: args["m_" + n] for n in WEIGHTS}
    v = {n: args["v_" + n] for n in WEIGHTS}
    xs, target = x[0], loss_target[0]
    mx, my, mc, _ = _chip_peers()
    chip = 2 * mx + my
    pos = jnp.stack([mc, chip])
    row = lambda a, i: a[i:i + 1]
    npm, nqm, npf, nqf = (w[n] for n in NORMS)
    shards = _weight_shards(w)
    of = lambda group: [shards[n] for n, _, _ in GROUPS[group]]
    bif = jnp.pad(w["ml_b_if"], ((0, 0), (0, 128 - 2 * ML_HEADS)))
    hn = w["ml_head_norm"]

    small = jnp.concatenate([w["lru_conv_w"][0]] + [w[n] for n in LRU_VECTORS], axis=0)
    ml_in_parts, ml_out_parts, smalls = _gather_weights(GROUPS["ml"], of("ml"), small)
    ml_w_in = jnp.pad(_by_cols(ml_in_parts), ((0, 0), (0, ML_IN_PAD - ML_IN)))
    ml_w_out = ml_out_parts.reshape(1024, 1024)
    vec = _by_cols(smalls)
    z0, qkv, og = _norm_matmul(xs, row(npm, 0), ml_w_in, 2 * ML_QK + ML_V, "ml_in")
    (y0, hs, cst, nst, mst), got = _mlstm_fwd(
        qkv, og, bif, hn, _gather_side(GROUPS["lru"] + GROUPS["ffn0"], of("lru") + of("ffn0")))
    lru_w_in = _by_cols(got[0])
    gates = got[1].reshape(N_CHIPS, 2, LRU_BLOCKS, 64, LRU_BLOCK).transpose(1, 2, 0, 3, 4)
    gates = gates.reshape(2, LRU_BLOCKS, LRU_BLOCK, LRU_BLOCK)
    lru_w_out = got[2].reshape(1024, 1024)
    w_gu0, w_down0 = _ffn_weights(got[3], got[4])
    ymix0, h1 = _matmul_postnorm(y0, ml_w_out, row(nqm, 0), xs, "ml_out")
    (zf0, gu0, yf0, h2), got = _ffn_forward(h1, row(npf, 0), w_gu0, w_down0, row(nqf, 0), "ffn0_fwd",
                                            _gather_side(GROUPS["ffn1"], of("ffn1")))
    w_gu1, w_down1 = _ffn_weights(got[0], got[1])
    z1, proj1 = _norm_matmul(h2, row(npm, 1), lru_w_in, 0, "lru_in")
    y1, u, r, ig, hl = _lru_fwd(proj1, vec[0:4], vec[4:5], gates[0], vec[5:6], gates[1], vec[6:7], vec[7:8])
    ymix1, h3 = _matmul_postnorm(y1, lru_w_out, row(nqm, 1), h2, "lru_out")
    (zf1, gu1, yf1, h4), _ = _ffn_forward(h3, row(npf, 1), w_gu1, w_down1, row(nqf, 1), "ffn1_fwd", None)
    dh4, loss_part = _loss_head(h4, target, "loss_head")

    (dyf1, dgu1, act1, dqf1), _ = _bwd_out(dh4, yf1, row(nqf, 1), w_down1, gu1, "ffn1_bwd_down", None)
    dw_down1 = _matmul_tn(act1, dyf1, True, 256, "ffn1_dw_down")
    dw_gu1 = _matmul_tn(zf1, dgu1, False, 512, "ffn1_dw_up")
    dh3, dpf1 = _bwd_in(dgu1, w_gu1, h3, row(npf, 1), dh4, "ffn1_bwd_up")
    s1_ffn1 = _half_reduce("ffn1", _ffn_grad_parts(dw_gu1, dw_down1), pos)

    (dymix1, dy1, dqm1), _ = _bwd_out(dh3, ymix1, row(nqm, 1), lru_w_out, None, "lru_bwd_out", None)
    dw_lru_out = _matmul_tn(y1, dymix1, False, 256, "lru_dw_out")
    (dproj1, dcw, dcb, dwa, dba, dwx, dbx, dlam), recv_ffn1 = _lru_bwd(
        proj1, vec[0:4], gates[0], gates[1], vec[7:8], u, r, ig, hl, dy1, _scatter_side(s1_ffn1))
    dw_lru_in = _matmul_tn(z1, dproj1, False, 512, "lru_dw_in")
    dh2, dpm1 = _bwd_in(dproj1, lru_w_in, h2, row(npm, 1), dh3, "lru_bwd_in")
    dgates = jnp.stack([dwa, dwx]).astype(BF16).reshape(2, LRU_BLOCKS, N_CHIPS, 64, LRU_BLOCK)
    dgates = dgates.transpose(2, 0, 1, 3, 4).reshape(N_CHIPS, -1, LRU_BLOCK)
    s1_lru = _half_reduce("lru", [_by_chip(dw_lru_in, 512), dgates, dw_lru_out.reshape(N_CHIPS, 256, 1024)], pos)

    (dyf0, dgu0, act0, dqf0), recv_lru = _bwd_out(dh2, yf0, row(nqf, 0), w_down0, gu0, "ffn0_bwd_down",
                                                  _scatter_side(s1_lru))
    dw_down0 = _matmul_tn(act0, dyf0, True, 256, "ffn0_dw_down")
    dw_gu0 = _matmul_tn(zf0, dgu0, False, 512, "ffn0_dw_up")
    dh1, dpf0 = _bwd_in(dgu0, w_gu0, h1, row(npf, 0), dh2, "ffn0_bwd_up")
    s1_ffn0 = _half_reduce("ffn0", _ffn_grad_parts(dw_gu0, dw_down0), pos)

    (dymix0, dy0, dqm0), _ = _bwd_out(dh1, ymix0, row(nqm, 0), ml_w_out, None, "ml_bwd_out", None)
    dw_ml_out = _matmul_tn(y0, dymix0, False, 256, "ml_dw_out")
    (dproj0, dhn, dbif), recv_ffn0 = _mlstm_bwd(qkv, og, bif, hn, hs, cst, nst, mst, dy0, _scatter_side(s1_ffn0))
    dw_ml_in = _matmul_tn(z0, dproj0, False, 640, "ml_dw_in")
    dx, dpm0 = _bwd_in(dproj0, ml_w_in, xs, row(npm, 0), dh1, "ml_bwd_in")
    s1_ml = _half_reduce(
        "ml", [_by_chip(dw_ml_in[:, :ML_IN], ML_IN // N_CHIPS), dw_ml_out.reshape(N_CHIPS, 256, 1024)], pos)

    pad_lanes = lambda a: jnp.pad(a, ((0, 0), (0, LANES - a.shape[1])))
    small = jnp.concatenate(
        [jnp.concatenate([dpm0, dpm1]), jnp.concatenate([dqm0, dqm1]), jnp.concatenate([dpf0, dpf1]),
         jnp.concatenate([dqf0, dqf1]), pad_lanes(loss_part), dhn, pad_lanes(dbif), dcw, dcb, dba, dbx, dlam,
         jnp.zeros((SMALL_ROWS - 19, LANES), F32)], axis=0)
    *recv_ml, small_all = _scatter_to_chips(s1_ml, small)
    order = ("ml", "lru", "ffn0", "ffn1")
    s2 = (_sum_group("ml", s1_ml, recv_ml, pos) + _sum_group("lru", s1_lru, recv_lru, pos)
          + _sum_group("ffn0", s1_ffn0, recv_ffn0, pos) + _sum_group("ffn1", s1_ffn1, recv_ffn1, pos))
    specs = sum((GROUPS[g] for g in order), ())
    red = dict(zip([n for n, _, _ in specs], _join_halves(specs, s2)))
    vsum = _sum_small(small_all)

    dgates = red["lru_gates"].reshape(2, LRU_BLOCKS, 64, LRU_BLOCK)
    cols = lambda a: lax.dynamic_slice_in_dim(a, chip * 256, 256, axis=1)
    grads = dict(
        ml_w_in=red["ml_w_in"], ml_w_out=red["ml_w_out"], lru_w_in=red["lru_w_in"], lru_w_gate_a=dgates[0],
        lru_w_gate_x=dgates[1], lru_w_out=red["lru_w_out"],
        ffn_w_gate=jnp.stack([red["ffn_gu0"][:, :HALF_FFN], red["ffn_gu1"][:, :HALF_FFN]]),
        ffn_w_up=jnp.stack([red["ffn_gu0"][:, HALF_FFN:], red["ffn_gu1"][:, HALF_FFN:]]),
        ffn_w_down=jnp.stack([red["ffn_down0"], red["ffn_down1"]]),
        ml_head_norm=vsum[ROW_HEAD_NORM:ROW_HEAD_NORM + 1], ml_b_if=vsum[ROW_B_IF:ROW_B_IF + 1, :2 * ML_HEADS],
        lru_conv_w=cols(vsum[ROW_LRU:ROW_LRU + 4]))
    for i, n in enumerate(NORMS):
        grads[n] = vsum[2 * i:2 * i + 2]
    for i, n in enumerate(LRU_VECTORS):
        grads[n] = cols(vsum[ROW_LRU + 4 + i:ROW_LRU + 5 + i])
    loss = vsum[ROW_LOSS, 0]
    grads = {n: grads[n].reshape(w[n].shape) for n in WEIGHTS}
    delta, new_m, new_v = _update(w, grads, m, v)
    return (loss, dx[None], *[grads[n] for n in WEIGHTS], *[delta[n] for n in WEIGHTS],
            *[new_m[n] for n in WEIGHTS], *[new_v[n] for n in WEIGHTS])
```

```python
import functools
import math
from typing import Callable, NamedTuple

import jax
import jax.numpy as jnp
from jax import lax
from jax.experimental import pallas as pl
from jax.experimental.pallas import tpu as pltpu

F32 = jnp.float32
BF16 = jnp.bfloat16
MESH = pl.DeviceIdType.MESH

D_MODEL = 1024
D_FF = 2816
ML_HEADS = 8
ML_DK = 64
ML_DV = 128
ML_QK = ML_HEADS * ML_DK
ML_V = ML_HEADS * ML_DV
ML_IN = 2 * ML_QK + 2 * ML_V + 2 * ML_HEADS
ML_IN_PAD = 3200
CHUNK = 64
GATE_CAP = 15.0
ML_M_INIT = -1e30
NEG_BIG = -1e30
LRU_BLOCKS = 4
LRU_BLOCK = 256
CONV_WIDTH = 4
LRU_C = 8.0
EPS = 1e-6
QK_SCALE = ML_DK ** -0.5

ADAM_LR = 0.001
ADAM_B1 = 0.9
ADAM_B2 = 0.999
ADAM_EPS = 1e-08
ADAM_WD = 0.01
ADAM_STEP = 10

N_CHIPS = 4
V7X_VMEM_LIMIT = 56 * 1024 * 1024

NT_DIMS = (((1,), (1,)), ((), ()))
TN_DIMS = (((0,), (0,)), ((), ()))


def _params(*semantics):
    return pltpu.CompilerParams(dimension_semantics=semantics, vmem_limit_bytes=V7X_VMEM_LIMIT)


def _sds(shape, dtype):
    return jax.ShapeDtypeStruct(shape, dtype)


def _full(shape):
    return pl.BlockSpec(shape, lambda *_: (0,) * len(shape))


def _rows(tm, n):
    return pl.BlockSpec((tm, n), lambda i: (i, 0))


def _sigmoid(x):
    return 1.0 / (1.0 + jnp.exp(-x))


def _log_sigmoid(x):
    return jnp.minimum(x, 0.0) - jnp.log1p(jnp.exp(-jnp.abs(x)))


def _rms(x):
    r = lax.rsqrt(jnp.mean(x * x, axis=-1, keepdims=True) + EPS)
    return x * r, r


def _rms_bwd(xhat, r, g, dy):
    dxh = dy * g
    return r * (dxh - xhat * jnp.mean(dxh * xhat, axis=-1, keepdims=True))


def _scan_rows(a, b, reverse=False):
    n = a.shape[0]
    row = lax.broadcasted_iota(jnp.int32, a.shape, 0)
    s = 1
    while s < n:
        if reverse:
            keep = row < n - s
            shift = n - s
        else:
            keep = row >= s
            shift = s
        b = b + a * jnp.where(keep, pltpu.roll(b, shift, 0), 0.0)
        a = a * jnp.where(keep, pltpu.roll(a, shift, 0), 1.0)
        s *= 2
    return a, b


def _cumsum_rows(x, reverse=False):
    n = x.shape[0]
    row = lax.broadcasted_iota(jnp.int32, x.shape, 0)
    s = 1
    while s < n:
        if reverse:
            x = x + jnp.where(row < n - s, pltpu.roll(x, n - s, 0), 0.0)
        else:
            x = x + jnp.where(row >= s, pltpu.roll(x, s, 0), 0.0)
        s *= 2
    return x


def _norm_matmul(h, g, w, n_bf16, name):
    t, d = h.shape
    n = w.shape[1]
    tm = min(512, t)

    def body(h_ref, g_ref, w_ref, z_ref, *o_refs):
        xhat, _ = _rms(h_ref[...])
        z = (xhat * g_ref[...]).astype(BF16)
        z_ref[...] = z
        out = jnp.dot(z, w_ref[...], preferred_element_type=F32)
        if n_bf16:
            o_refs[0][...] = out[:, :n_bf16].astype(BF16)
            o_refs[1][...] = out[:, n_bf16:]
        else:
            o_refs[0][...] = out

    if n_bf16:
        out_specs = [_rows(tm, d), _rows(tm, n_bf16), _rows(tm, n - n_bf16)]
        out_shape = [_sds((t, d), BF16), _sds((t, n_bf16), BF16), _sds((t, n - n_bf16), F32)]
    else:
        out_specs = [_rows(tm, d), _rows(tm, n)]
        out_shape = [_sds((t, d), BF16), _sds((t, n), F32)]
    return pl.pallas_call(
        body, name=name, grid=(t // tm,),
        in_specs=[_rows(tm, d), _full((1, d)), _full((d, n))],
        out_specs=out_specs, out_shape=out_shape, compiler_params=_params("parallel"),
    )(h, g, w)


def _matmul_postnorm(a, w, g, res, name):
    t = res.shape[0]
    k, d = w.shape
    tm = 256

    def body(a_ref, w_ref, g_ref, res_ref, y_ref, o_ref):
        y = jnp.dot(a_ref[...], w_ref[...], preferred_element_type=F32)
        y_ref[...] = y
        yhat, _ = _rms(y)
        o_ref[...] = res_ref[...] + yhat * g_ref[...]

    return pl.pallas_call(
        body, name=name, grid=(t // tm,),
        in_specs=[_rows(tm, k), _full((k, d)), _full((1, d)), _rows(tm, d)],
        out_specs=[_rows(tm, d), _rows(tm, d)],
        out_shape=[_sds((t, d), F32), _sds((t, d), F32)],
        compiler_params=_params("parallel"),
    )(a, w, g, res)


class _Side(NamedTuple):
    inputs: list
    out_shape: list
    n_sems: int
    start: Callable
    finish: Callable


def _side_call(body, side, *, name, steps, in_specs, out_specs, out_shape, scratch_shapes, args):
    n_in, n_out, n_scr = len(in_specs), len(out_specs), len(scratch_shapes)
    if side is None:
        outs = pl.pallas_call(
            body, name=name, grid=(steps,), in_specs=in_specs, out_specs=out_specs, out_shape=out_shape,
            scratch_shapes=scratch_shapes, compiler_params=_params("arbitrary"))(*args)
        return list(outs), []
    s_in, s_out = len(side.inputs), len(side.out_shape)

    def carrying(*refs):
        ins, side_ins = refs[:n_in], refs[n_in:n_in + s_in]
        outs = refs[n_in + s_in:n_in + s_in + n_out]
        side_outs = refs[n_in + s_in + n_out:n_in + s_in + n_out + s_out]
        scratch = refs[n_in + s_in + n_out + s_out:]
        own, sems = scratch[:n_scr], scratch[n_scr:]

        @pl.when(pl.program_id(0) == 0)
        def _():
            side.start(side_ins, side_outs, *sems)

        body(*ins, *outs, *own)

        @pl.when(pl.program_id(0) == steps - 1)
        def _():
            side.finish(side_ins, side_outs, *sems)

    outs = pl.pallas_call(
        carrying, name=name, grid=(steps,), in_specs=list(in_specs) + [HBM_SPEC] * s_in,
        out_specs=list(out_specs) + [HBM_SPEC] * s_out, out_shape=list(out_shape) + list(side.out_shape),
        scratch_shapes=list(scratch_shapes) + [pltpu.SemaphoreType.DMA((side.n_sems,))] * 2,
        compiler_params=_params("arbitrary"))(*args, *side.inputs)
    return list(outs[:n_out]), list(outs[n_out:])


def _ffn_forward(h, g_pre, w_gu, w_down, g_post, name, side):
    t, d = h.shape
    k = w_down.shape[0]
    tm = 256

    def body(h_ref, gpre_ref, wgu_ref, wd_ref, gpost_ref, z_ref, gu_ref, y_ref, o_ref):
        hv = h_ref[...]
        xhat, _ = _rms(hv)
        z = (xhat * gpre_ref[...]).astype(BF16)
        z_ref[...] = z
        gu = jnp.dot(z, wgu_ref[...], preferred_element_type=F32).astype(BF16)
        gu_ref[...] = gu
        gate = gu[:, :k].astype(F32)
        act = (gate * _sigmoid(gate) * gu[:, k:].astype(F32)).astype(BF16)
        y = jnp.dot(act, wd_ref[...], preferred_element_type=F32)
        y_ref[...] = y
        yhat, _ = _rms(y)
        o_ref[...] = hv + yhat * gpost_ref[...]

    return _side_call(
        body, side, name=name, steps=t // tm,
        in_specs=[_rows(tm, d), _full((1, d)), _full((d, 2 * k)), _full((k, d)), _full((1, d))],
        out_specs=[_rows(tm, d), _rows(tm, 2 * k), _rows(tm, d), _rows(tm, d)],
        out_shape=[_sds((t, d), BF16), _sds((t, 2 * k), BF16), _sds((t, d), F32), _sds((t, d), F32)],
        scratch_shapes=[], args=(h, g_pre, w_gu, w_down, g_post))


def _loss_head(h, target, name):
    t, d = h.shape
    tm = min(512, t)

    def body(h_ref, t_ref, dh_ref, l_ref):
        @pl.when(pl.program_id(0) == 0)
        def _():
            l_ref[...] = jnp.zeros_like(l_ref)

        err = h_ref[...] - t_ref[...]
        dh_ref[...] = err * (1.0 / d)
        part = jnp.sum(jnp.sum(err * err, axis=1, keepdims=True), axis=0, keepdims=True) * (0.5 / d)
        l_ref[...] += jnp.broadcast_to(part, l_ref.shape)

    return pl.pallas_call(
        body, name=name, grid=(t // tm,),
        in_specs=[_rows(tm, d), _rows(tm, d)],
        out_specs=[_rows(tm, d), _full((1, 128))],
        out_shape=[_sds((t, d), F32), _sds((1, 128), F32)],
        compiler_params=_params("arbitrary"),
    )(h, target)


def _bwd_out(dout, y, g, w, gu, name, side):
    t, d = dout.shape
    k = w.shape[0]
    ffn = gu is not None
    tm = 128 if ffn else 256

    def body(*refs):
        if ffn:
            dout_ref, y_ref, g_ref, w_ref, gu_ref, dy_ref, dact_ref, act_ref, dg_ref = refs
        else:
            dout_ref, y_ref, g_ref, w_ref, dy_ref, dact_ref, dg_ref = refs

        @pl.when(pl.program_id(0) == 0)
        def _():
            dg_ref[...] = jnp.zeros_like(dg_ref)

        do = dout_ref[...]
        yhat, r = _rms(y_ref[...])
        dg_ref[...] += jnp.sum(do * yhat, axis=0, keepdims=True)
        dy = _rms_bwd(yhat, r, g_ref[...], do).astype(BF16)
        dy_ref[...] = dy
        da = lax.dot_general(dy, w_ref[...], NT_DIMS, preferred_element_type=F32)
        if ffn:
            gate = gu_ref[:, :k].astype(F32)
            up = gu_ref[:, k:].astype(F32)
            sg = _sigmoid(gate)
            silu = gate * sg
            act_ref[...] = (silu * up).astype(BF16)
            dact_ref[:, :k] = (da * up * (sg * (1.0 + gate * (1.0 - sg)))).astype(BF16)
            dact_ref[:, k:] = (da * silu).astype(BF16)
        else:
            dact_ref[...] = da

    in_specs = [_rows(tm, d), _rows(tm, d), _full((1, d)), _full((k, d))]
    args = [dout, y, g, w]
    if ffn:
        in_specs.append(_rows(tm, 2 * k))
        args.append(gu)
        out_specs = [_rows(tm, d), _rows(tm, 2 * k), _rows(tm, k), _full((1, d))]
        out_shape = [_sds((t, d), BF16), _sds((t, 2 * k), BF16), _sds((t, k), BF16), _sds((1, d), F32)]
    else:
        out_specs = [_rows(tm, d), _rows(tm, k), _full((1, d))]
        out_shape = [_sds((t, d), BF16), _sds((t, k), F32), _sds((1, d), F32)]
    return _side_call(body, side, name=name, steps=t // tm, in_specs=in_specs, out_specs=out_specs,
                      out_shape=out_shape, scratch_shapes=[], args=args)


def _bwd_in(dp, w, h, g, dout, name):
    t, d = h.shape
    n = w.shape[1]
    tm = 256

    def body(dp_ref, w_ref, h_ref, g_ref, dout_ref, dh_ref, dg_ref):
        @pl.when(pl.program_id(0) == 0)
        def _():
            dg_ref[...] = jnp.zeros_like(dg_ref)

        dz = lax.dot_general(dp_ref[...], w_ref[...], NT_DIMS, preferred_element_type=F32)
        hhat, r = _rms(h_ref[...])
        dg_ref[...] += jnp.sum(dz * hhat, axis=0, keepdims=True)
        dh_ref[...] = dout_ref[...] + _rms_bwd(hhat, r, g_ref[...], dz)

    return pl.pallas_call(
        body, name=name, grid=(t // tm,),
        in_specs=[_rows(tm, n), _full((d, n)), _rows(tm, d), _full((1, d)), _rows(tm, d)],
        out_specs=[_rows(tm, d), _full((1, d))],
        out_shape=[_sds((t, d), F32), _sds((1, d), F32)],
        compiler_params=_params("arbitrary"),
    )(dp, w, h, g, dout)


def _matmul_tn(a, b, tile_a, tile, name):
    t, ka = a.shape
    nb = b.shape[1]

    def body(a_ref, b_ref, o_ref):
        o_ref[...] = lax.dot_general(a_ref[...], b_ref[...], TN_DIMS, preferred_element_type=F32).astype(BF16)

    if tile_a:
        grid = (ka // tile,)
        in_specs = [pl.BlockSpec((t, tile), lambda i: (0, i)), _full((t, nb))]
        out_specs = pl.BlockSpec((tile, nb), lambda i: (i, 0))
    else:
        grid = (nb // tile,)
        in_specs = [_full((t, ka)), pl.BlockSpec((t, tile), lambda i: (0, i))]
        out_specs = pl.BlockSpec((ka, tile), lambda i: (0, i))
    return pl.pallas_call(
        body, name=name, grid=grid, in_specs=in_specs, out_specs=out_specs,
        out_shape=_sds((ka, nb), BF16), compiler_params=_params("parallel"),
    )(a, b)


def _ml_gate_prep(gt, bif):
    th = jnp.tanh((gt + bif) / GATE_CAP)
    act = GATE_CAP * th
    cum = _cumsum_rows(_log_sigmoid(act))
    lane = lax.broadcasted_iota(jnp.int32, gt.shape, 1)
    x = jnp.where(lane < ML_HEADS, act, cum)
    return x, x.T, th, act


HEADS = range(ML_HEADS)


def _each(fn, *per_head):
    return [fn(*a) for a in zip(*per_head)]


def _ml_chunk_fwd(q, k, v, kt, x, xt, c_in, n_in, m_in):
    causal = (lax.broadcasted_iota(jnp.int32, (CHUNK, CHUNK), 0)
              >= lax.broadcasted_iota(jnp.int32, (CHUNK, CHUNK), 1))
    f = {}
    qh = f["qh"] = [q[:, ML_DK * h:ML_DK * (h + 1)] for h in HEADS]
    kh = f["kh"] = [k[:, ML_DK * h:ML_DK * (h + 1)] for h in HEADS]
    f["vh"] = [v[:, ML_DV * h:ML_DV * (h + 1)] for h in HEADS]
    kth = [kt[ML_DK * h:ML_DK * (h + 1), :] for h in HEADS]
    s = _each(lambda a, b: jnp.dot(a, b, preferred_element_type=F32) * QK_SCALE, qh, kth)
    f["qc"] = _each(lambda a, c: jnp.dot(a, c.astype(BF16), preferred_element_type=F32) * QK_SCALE, qh, c_in)
    bcol = f["bcol"] = [x[:, ML_HEADS + h:ML_HEADS + h + 1] for h in HEADS]
    licol = f["licol"] = [x[:, h:h + 1] for h in HEADS]
    brow = f["brow"] = [xt[ML_HEADS + h:ML_HEADS + h + 1, :] for h in HEADS]
    lirow = [xt[h:h + 1, :] for h in HEADS]
    dmat = _each(lambda bc, br, lr: jnp.where(causal, bc - br + lr, NEG_BIG), bcol, brow, lirow)
    inter = _each(lambda bc, m: bc + m, bcol, m_in)
    mt = _each(lambda d, i: jnp.maximum(jnp.max(d, axis=1, keepdims=True), i), dmat, inter)
    wt = f["wt"] = _each(lambda d, m: jnp.exp(d - m), dmat, mt)
    p = f["p"] = _each(lambda a, b: a * b, wt, s)
    winter = f["winter"] = _each(lambda i, m: jnp.exp(i - m), inter, mt)
    qf = f["qf"] = [a.astype(F32) for a in qh]
    qn = f["qn"] = _each(lambda a, n: jnp.sum(a * n, axis=1, keepdims=True) * QK_SCALE, qf, n_in)
    den = f["den"] = _each(lambda a, w, b: jnp.sum(a, axis=1, keepdims=True) + w * b, p, winter, qn)
    emt = f["emt"] = [jnp.exp(-m) for m in mt]
    f["nrm"] = _each(lambda d, e: jnp.maximum(jnp.abs(d), e), den, emt)
    gtot = [bc[CHUNK - 1:CHUNK, :] for bc in bcol]
    a_col = _each(lambda g, bc, lc: g - bc + lc, gtot, bcol, licol)
    a_row = _each(lambda g, br, lr: g - br + lr, gtot, brow, lirow)
    m_new = f["m_new"] = _each(lambda g, m, a: jnp.maximum(g + m, jnp.max(a, axis=1, keepdims=True)),
                               gtot, m_in, a_row)
    f["decay"] = _each(lambda g, m, mn: jnp.exp(g + m - mn), gtot, m_in, m_new)
    wkf = f["wkf"] = _each(lambda a, mn: jnp.exp(a - mn), a_col, m_new)
    f["kw"] = _each(lambda a, w: a.astype(F32) * w, kh, wkf)
    f["ktw"] = _each(lambda a, ar, mn: (a.astype(F32) * jnp.exp(ar - mn)).astype(BF16), kth, a_row, m_new)
    return f


def _ml_specs(nc, rev):
    def at(col):
        if rev:
            return lambda i: (nc - 1 - i, col)
        return lambda i: (i, col)

    return [pl.BlockSpec((CHUNK, ML_QK), at(0)), pl.BlockSpec((CHUNK, ML_QK), at(1)),
            pl.BlockSpec((CHUNK, ML_V), at(1)), pl.BlockSpec((CHUNK, ML_V), at(0)),
            pl.BlockSpec((CHUNK, 128), at(ML_V // 128))]


def _mlstm_fwd(qkv, og, bif, hn, side):
    t = qkv.shape[0]
    nc = t // CHUNK

    def body(q_ref, k_ref, v_ref, o_ref, gt_ref, bif_ref, hn_ref, y_ref, hs_ref, cst_ref, nst_ref, mst_ref,
             c_sc, n_sc, m_sc):
        @pl.when(pl.program_id(0) == 0)
        def _():
            c_sc[...] = jnp.zeros_like(c_sc)
            n_sc[...] = jnp.zeros_like(n_sc)
            m_sc[...] = jnp.full_like(m_sc, ML_M_INIT)

        cst_ref[0] = c_sc[...]
        nst_ref[0] = n_sc[...]
        mst_ref[0] = m_sc[...]
        x, xt, _, _ = _ml_gate_prep(gt_ref[...], bif_ref[...])
        q, k, v = q_ref[...], k_ref[...], v_ref[...]
        c_all, n_all, m_all = c_sc[...], n_sc[...], m_sc[...]
        c_in = [c_all[ML_DK * h:ML_DK * (h + 1), :] for h in HEADS]
        n_in = [n_all[h:h + 1, :] for h in HEADS]
        m_in = [m_all[h:h + 1, 0:1] for h in HEADS]
        f = _ml_chunk_fwd(q, k, v, k.T, x, xt, c_in, n_in, m_in)
        num = _each(lambda p, vh, w, qc: jnp.dot(p.astype(BF16), vh, preferred_element_type=F32) + w * qc,
                    f["p"], f["vh"], f["winter"], f["qc"])
        hh = _each(lambda a, b: a / b, num, f["nrm"])
        hhat = [_rms(a)[0] for a in hh]
        c_new = _each(lambda d, c, kw, vh: d * c + jnp.dot(kw, vh, preferred_element_type=F32),
                      f["decay"], c_in, f["ktw"], f["vh"])
        n_new = _each(lambda d, n, kw: d * n + jnp.sum(kw, axis=0, keepdims=True), f["decay"], n_in, f["kw"])
        for h in HEADS:
            vs = slice(ML_DV * h, ML_DV * (h + 1))
            hs_ref[:, vs] = hh[h]
            y_ref[:, vs] = (hhat[h] * hn_ref[:, vs] * _sigmoid(o_ref[:, vs])).astype(BF16)
            c_sc[ML_DK * h:ML_DK * (h + 1), :] = c_new[h]
            n_sc[h:h + 1, :] = n_new[h]
            m_sc[h:h + 1, :] = jnp.broadcast_to(f["m_new"][h], (1, 128))

    return _side_call(
        body, side, name="mlstm_fwd", steps=nc,
        in_specs=_ml_specs(nc, False) + [_full((1, 128)), _full((1, ML_V))],
        out_specs=[_rows(CHUNK, ML_V), _rows(CHUNK, ML_V),
                   pl.BlockSpec((1, ML_HEADS * ML_DK, ML_DV), lambda i: (i, 0, 0)),
                   pl.BlockSpec((1, ML_HEADS, ML_DK), lambda i: (i, 0, 0)),
                   pl.BlockSpec((1, ML_HEADS, 128), lambda i: (i, 0, 0))],
        out_shape=[_sds((t, ML_V), BF16), _sds((t, ML_V), F32),
                   _sds((nc, ML_HEADS * ML_DK, ML_DV), F32), _sds((nc, ML_HEADS, ML_DK), F32),
                   _sds((nc, ML_HEADS, 128), F32)],
        scratch_shapes=[pltpu.VMEM((ML_HEADS * ML_DK, ML_DV), F32), pltpu.VMEM((ML_HEADS, ML_DK), F32),
                        pltpu.VMEM((ML_HEADS, 128), F32)],
        args=(qkv, qkv, qkv, og, og, bif, hn))


def _mlstm_bwd(qkv, og, bif, hn, hs, cst, nst, mst, dy, side):
    t = qkv.shape[0]
    nc = t // CHUNK

    def body(q_ref, k_ref, v_ref, o_ref, gt_ref, bif_ref, hn_ref, hs_ref, cst_ref, nst_ref, mst_ref, dy_ref,
             dp_ref, dhn_ref, dbif_ref, dc_sc, dct_sc, dn_sc):
        @pl.when(pl.program_id(0) == 0)
        def _():
            dc_sc[...] = jnp.zeros_like(dc_sc)
            dct_sc[...] = jnp.zeros_like(dct_sc)
            dn_sc[...] = jnp.zeros_like(dn_sc)
            dhn_ref[...] = jnp.zeros_like(dhn_ref)
            dbif_ref[...] = jnp.zeros_like(dbif_ref)

        x, xt, th, act = _ml_gate_prep(gt_ref[...], bif_ref[...])
        q, k, v = q_ref[...], k_ref[...], v_ref[...]
        qt, vt = q.T, v.T
        rows = lax.broadcasted_iota(jnp.int32, (CHUNK, CHUNK), 0)
        cols = lax.broadcasted_iota(jnp.int32, (CHUNK, CHUNK), 1)
        lane = lax.broadcasted_iota(jnp.int32, (CHUNK, 128), 1)
        row = lax.broadcasted_iota(jnp.int32, (CHUNK, 1), 0)
        as_row = lambda col: jnp.sum(jnp.where(rows == cols, col, 0.0), axis=0, keepdims=True)
        mm = lambda a, b: jnp.dot(a, b, preferred_element_type=F32)
        bf = lambda a: a.astype(BF16)
        ksl = [slice(ML_DK * h, ML_DK * (h + 1)) for h in HEADS]
        vsl = [slice(ML_DV * h, ML_DV * (h + 1)) for h in HEADS]
        c_in = [cst_ref[0, s, :] for s in ksl]
        n_in = [nst_ref[0, h:h + 1, :] for h in HEADS]
        m_in = [mst_ref[0, h:h + 1, 0:1] for h in HEADS]
        dc_all, dct_all, dn_all = dc_sc[...], dct_sc[...], dn_sc[...]
        dcn = [dc_all[s, :] for s in ksl]
        dcn_t = [dct_all[s, :] for s in vsl]
        dnn = [dn_all[h:h + 1, :] for h in HEADS]
        f = _ml_chunk_fwd(q, k, v, k.T, x, xt, c_in, n_in, m_in)
        qh, kh, vh, p, winter, decay = f["qh"], f["kh"], f["vh"], f["p"], f["winter"], f["decay"]
        qth = [qt[s, :] for s in ksl]
        vth = [vt[s, :] for s in vsl]
        c_t = [bf(c.T) for c in c_in]
        dmat_t = _each(lambda br, bc, lc: jnp.where(rows <= cols, br - bc + lc, NEG_BIG),
                       f["brow"], f["bcol"], f["licol"])
        inter_row = _each(lambda br, m: br + m, f["brow"], m_in)
        mt_row = _each(lambda d, i: jnp.maximum(jnp.max(d, axis=0, keepdims=True), i), dmat_t, inter_row)
        wt_t = _each(lambda d, m: jnp.exp(d - m), dmat_t, mt_row)
        p_t = _each(lambda w, a, b: w * (mm(a, b) * QK_SCALE), wt_t, kh, qth)
        winter_row = _each(lambda i, m: jnp.exp(i - m), inter_row, mt_row)
        hh = [hs_ref[:, s] for s in vsl]
        hn_h = [hn_ref[:, s] for s in vsl]
        sg = [_sigmoid(o_ref[:, s]) for s in vsl]
        dyh = [dy_ref[:, s] for s in vsl]
        norm = [_rms(a) for a in hh]
        hhat, r = [a for a, _ in norm], [b for _, b in norm]
        dyn = _each(lambda a, b: a * b, dyh, sg)
        do = _each(lambda d, hx, g, s: d * hx * g * s * (1.0 - s), dyh, hhat, hn_h, sg)
        dhn = _each(lambda a, b: jnp.sum(a * b, axis=0, keepdims=True), dyn, hhat)
        dh = _each(_rms_bwd, hhat, r, hn_h, dyn)
        inv = [1.0 / a for a in f["nrm"]]
        dnum = _each(lambda a, b: a * b, dh, inv)
        dnrm = _each(lambda a, b, c: -jnp.sum(a * b, axis=1, keepdims=True) * c, dh, hh, inv)
        dden = _each(lambda d, e, g: jnp.where(jnp.abs(d) > e, g * jnp.sign(d), 0.0), f["den"], f["emt"], dnrm)
        dnb = [bf(a) for a in dnum]
        dnt = [bf(a.T) for a in dnum]
        rmat = _each(lambda a, b, d: mm(a, b) + d, dnb, vth, dden)
        rmat_t = _each(lambda a, b, d: mm(a, b) + as_row(d), vh, dnt, dden)
        ds = _each(lambda w, a: bf(w * a), f["wt"], rmat)
        ds_t = _each(lambda w, a: bf(w * a), wt_t, rmat_t)
        dv = _each(lambda a, b: mm(bf(a), b), p_t, dnb)
        dqs = _each(lambda s, kk, w, d, ct, dd, n: mm(s, kk) + w * (mm(d, ct) + dd * n),
                    ds, kh, winter, dnb, c_t, dden, n_in)
        dk = _each(lambda s, a: mm(s, a) * QK_SCALE, ds_t, qh)
        dinter = _each(lambda qc, dn_, qn, dd, w: (jnp.sum(qc * dn_, axis=1, keepdims=True) + qn * dd) * w,
                       f["qc"], dnum, f["qn"], dden, winter)
        wq = _each(lambda w, a: w * a * QK_SCALE, winter, f["qf"])
        wq_t = _each(lambda w, a: bf(w * a.astype(F32) * QK_SCALE), winter_row, qth)
        dc_loc = _each(mm, wq_t, dnb)
        dct_loc = _each(lambda a, b: mm(a, bf(b)), dnt, wq)
        dn_loc = _each(lambda a, d: jnp.sum(a * d, axis=0, keepdims=True), wq, dden)
        cs_q = _each(lambda a, b: jnp.sum(a * b, axis=1, keepdims=True), p_t, rmat_t)
        db = _each(lambda a, b, di, cs: jnp.sum(a * b, axis=1, keepdims=True) + di - cs, p, rmat, dinter, cs_q)
        ddecay = _each(lambda dc_, c, dn_, n: jnp.sum(jnp.sum(dc_ * c, axis=1, keepdims=True), axis=0, keepdims=True)
                       + jnp.sum(dn_ * n, axis=1, keepdims=True), dcn, c_in, dnn, n_in)
        dkw = _each(lambda a, b, n: mm(a, bf(b)) + n, vh, dcn_t, dnn)
        dk = _each(lambda a, w, b: a + w * b, dk, f["wkf"], dkw)
        da = _each(lambda a, kk, w: jnp.sum(a * kk.astype(F32), axis=1, keepdims=True) * w, dkw, kh, f["wkf"])
        dv = _each(lambda a, kw, dc_: a + mm(bf(kw), bf(dc_)), dv, f["kw"], dcn)
        dgtot = _each(lambda a, dd, d: jnp.sum(a, axis=0, keepdims=True) + dd * d, da, ddecay, decay)
        db = _each(lambda a, b, g: a - b + jnp.where(row == CHUNK - 1, g, 0.0), db, da, dgtot)
        dli = _each(lambda a, b: a + b, cs_q, da)
        dc_new = _each(lambda d, a, b: d * a + b, decay, dcn, dc_loc)
        dct_new = _each(lambda d, a, b: d * a + b, decay, dcn_t, dct_loc)
        dn_new = _each(lambda d, a, b: d * a + b, decay, dnn, dn_loc)
        dx = jnp.zeros((CHUNK, 128), F32)
        for h in HEADS:
            dhn_ref[:, vsl[h]] += dhn[h]
            dc_sc[ksl[h], :] = dc_new[h]
            dct_sc[vsl[h], :] = dct_new[h]
            dn_sc[h:h + 1, :] = dn_new[h]
            dp_ref[:, ksl[h]] = bf(dqs[h] * QK_SCALE)
            dp_ref[:, ML_QK + ML_DK * h:ML_QK + ML_DK * (h + 1)] = bf(dk[h])
            dp_ref[:, 2 * ML_QK + ML_DV * h:2 * ML_QK + ML_DV * (h + 1)] = bf(dv[h])
            dp_ref[:, 2 * ML_QK + ML_V + ML_DV * h:2 * ML_QK + ML_V + ML_DV * (h + 1)] = bf(do[h])
            dx = jnp.where(lane == h, dli[h], dx)
            dx = jnp.where(lane == ML_HEADS + h, db[h], dx)
        dlf = _cumsum_rows(dx, reverse=True)
        dact = jnp.where(lane < ML_HEADS, dx, dlf * _sigmoid(-act))
        dz = dact * (1.0 - th * th)
        dp_ref[:, 2 * ML_QK + 2 * ML_V:] = dz.astype(BF16)
        dbif_ref[...] += jnp.sum(dz, axis=0, keepdims=True)

    rev = lambda i: (nc - 1 - i, 0)
    rev3 = lambda i: (nc - 1 - i, 0, 0)
    return _side_call(
        body, side, name="mlstm_bwd", steps=nc,
        in_specs=_ml_specs(nc, True) + [
            _full((1, 128)), _full((1, ML_V)), pl.BlockSpec((CHUNK, ML_V), rev),
            pl.BlockSpec((1, ML_HEADS * ML_DK, ML_DV), rev3), pl.BlockSpec((1, ML_HEADS, ML_DK), rev3),
            pl.BlockSpec((1, ML_HEADS, 128), rev3), pl.BlockSpec((CHUNK, ML_V), rev)],
        out_specs=[pl.BlockSpec((CHUNK, ML_IN_PAD), rev), _full((1, ML_V)), _full((1, 128))],
        out_shape=[_sds((t, ML_IN_PAD), BF16), _sds((1, ML_V), F32), _sds((1, 128), F32)],
        scratch_shapes=[pltpu.VMEM((ML_HEADS * ML_DK, ML_DV), F32), pltpu.VMEM((ML_HEADS * ML_DV, ML_DK), F32),
                        pltpu.VMEM((ML_HEADS, ML_DK), F32)],
        args=(qkv, qkv, qkv, og, og, bif, hn, hs, cst, nst, mst, dy))


LRU_TM = 256
GELU_K = math.sqrt(2.0 / math.pi)
GELU_C = 0.044715


def _gelu(x):
    th = jnp.tanh(GELU_K * (x + GELU_C * x * x * x))
    return 0.5 * x * (1.0 + th), th


def _neg_expm1(x):
    series = -x * (1.0 + x * (0.5 + x * (1.0 / 6.0 + x * (1.0 / 24.0))))
    return jnp.where(x > -0.05, series, 1.0 - jnp.exp(x))


def _block_diag_dot(a, w_ref, dims):
    parts = [lax.dot_general(a[:, LRU_BLOCK * n:LRU_BLOCK * (n + 1)], w_ref[n], dims, preferred_element_type=F32)
             for n in range(LRU_BLOCKS)]
    return jnp.concatenate(parts, axis=1)


def _lru_gates(u, r, ig, lam):
    ls = _log_sigmoid(lam)
    la = LRU_C * r * ls
    a = jnp.exp(la)
    em = _neg_expm1(2.0 * la)
    mult = jnp.sqrt(em)
    return ls, a, em, mult


def _lru_fwd(proj, cw, cb, wa, ba, wx, bx, lam):
    t = proj.shape[0]
    w = D_MODEL
    tm = min(LRU_TM, t)

    def body(gb_ref, up_ref, cw_ref, cb_ref, wa_ref, ba_ref, wx_ref, bx_ref, lam_ref,
             y_ref, u_ref, r_ref, i_ref, h_ref, tail_sc, hprev_sc):
        @pl.when(pl.program_id(0) == 0)
        def _():
            tail_sc[...] = jnp.zeros_like(tail_sc)
            hprev_sc[...] = jnp.zeros_like(hprev_sc)

        up = up_ref[...]
        ext = jnp.concatenate([tail_sc[...], up], axis=0)
        u = cb_ref[...] + cw_ref[CONV_WIDTH - 1:CONV_WIDTH, :] * up
        for s in range(1, CONV_WIDTH):
            u = u + cw_ref[CONV_WIDTH - 1 - s:CONV_WIDTH - s, :] * pltpu.roll(ext, s, 0)[8:8 + tm]
        tail_sc[...] = up[tm - 8:tm]
        ub = u.astype(BF16)
        r = _sigmoid(_block_diag_dot(ub, wa_ref, (((1,), (0,)), ((), ()))) + ba_ref[...])
        ig = _sigmoid(_block_diag_dot(ub, wx_ref, (((1,), (0,)), ((), ()))) + bx_ref[...])
        _, a, _, mult = _lru_gates(u, r, ig, lam_ref[...])
        acum, hloc = _scan_rows(a, mult * ig * u)
        h = hloc + acum * hprev_sc[0:1, :]
        hprev_sc[0:1, :] = h[tm - 1:tm]
        u_ref[...] = u
        r_ref[...] = r
        i_ref[...] = ig
        h_ref[...] = h
        gel, _ = _gelu(gb_ref[...])
        y_ref[...] = (h * gel).astype(BF16)

    vec = _full((1, w))
    wspec = _full((LRU_BLOCKS, LRU_BLOCK, LRU_BLOCK))
    return pl.pallas_call(
        body, name="lru_fwd", grid=(t // tm,),
        in_specs=[pl.BlockSpec((tm, w), lambda i: (i, 0)), pl.BlockSpec((tm, w), lambda i: (i, 1)),
                  _full((CONV_WIDTH, w)), vec, wspec, vec, wspec, vec, vec],
        out_specs=[_rows(tm, w)] * 5,
        out_shape=[_sds((t, w), BF16)] + [_sds((t, w), F32)] * 4,
        scratch_shapes=[pltpu.VMEM((8, w), F32), pltpu.VMEM((8, w), F32)],
        compiler_params=_params("arbitrary"),
    )(proj, proj, cw, cb, wa, ba, wx, bx, lam)


def _lru_bwd(proj, cw, wa, wx, lam, u, r, ig, h, dy, side):
    t = proj.shape[0]
    w = D_MODEL
    tm = min(LRU_TM, t)
    nt = t // tm

    def body(gb_ref, up_ref, cw_ref, wa_ref, wx_ref, lam_ref, u_ref, r_ref, i_ref, h_ref, hp_ref, dy_ref,
             dp_ref, dcw_ref, dcb_ref, dwa_ref, dba_ref, dwx_ref, dbx_ref, dlam_ref, carry_sc, dutail_sc, dls_sc):
        step = pl.program_id(0)

        @pl.when(step == 0)
        def _():
            carry_sc[...] = jnp.zeros_like(carry_sc)
            dutail_sc[...] = jnp.zeros_like(dutail_sc)
            dls_sc[...] = jnp.zeros_like(dls_sc)
            for ref in (dcw_ref, dcb_ref, dwa_ref, dba_ref, dwx_ref, dbx_ref):
                ref[...] = jnp.zeros_like(ref)

        row = lax.broadcasted_iota(jnp.int32, (tm, w), 0)
        u_t, r_t, i_t, h_t = u_ref[...], r_ref[...], i_ref[...], h_ref[...]
        ls, a, em, mult = _lru_gates(u_t, r_t, i_t, lam_ref[...])
        gb = gb_ref[...]
        gel, th = _gelu(gb)
        dyv = dy_ref[...]
        dgb = dyv * h_t * (0.5 * (1.0 + th) + 0.5 * gb * (1.0 - th * th) * GELU_K * (1.0 + 3.0 * GELU_C * gb * gb))
        dh = jnp.where(row == tm - 1, dyv * gel + carry_sc[0:1, :], dyv * gel)
        a_next = jnp.where(row < tm - 1, pltpu.roll(a, tm - 1, 0), 0.0)
        _, g = _scan_rows(a_next, dh, reverse=True)
        carry_sc[0:1, :] = a[0:1] * g[0:1]
        has_prev = jnp.where(step == nt - 1, 0.0, 1.0)
        h_prev = jnp.where(row >= 1, pltpu.roll(h_t, 1, 0), hp_ref[7:8, :] * has_prev)
        dmult = g * i_t * u_t
        dig = g * mult * u_t
        du = g * mult * i_t
        dla = g * h_prev * a - dmult * (1.0 - em) / mult
        dls_sc[0:1, :] += jnp.sum(dla * r_t, axis=0, keepdims=True) * LRU_C
        dpa = dla * (LRU_C * ls) * r_t * (1.0 - r_t)
        dpx = dig * i_t * (1.0 - i_t)
        dba_ref[...] += jnp.sum(dpa, axis=0, keepdims=True)
        dbx_ref[...] += jnp.sum(dpx, axis=0, keepdims=True)
        ub = u_t.astype(BF16)
        dpab = dpa.astype(BF16)
        dpxb = dpx.astype(BF16)
        for n in range(LRU_BLOCKS):
            cs = slice(LRU_BLOCK * n, LRU_BLOCK * (n + 1))
            dwa_ref[n] += lax.dot_general(ub[:, cs], dpab[:, cs], TN_DIMS, preferred_element_type=F32)
            dwx_ref[n] += lax.dot_general(ub[:, cs], dpxb[:, cs], TN_DIMS, preferred_element_type=F32)
        du = du + _block_diag_dot(dpab, wa_ref, NT_DIMS) + _block_diag_dot(dpxb, wx_ref, NT_DIMS)
        dcb_ref[...] += jnp.sum(du, axis=0, keepdims=True)
        ext = jnp.concatenate([du, dutail_sc[...]], axis=0)
        up = up_ref[...]
        dup = cw_ref[CONV_WIDTH - 1:CONV_WIDTH, :] * du
        dcw_ref[CONV_WIDTH - 1:CONV_WIDTH, :] += jnp.sum(up * du, axis=0, keepdims=True)
        for s in range(1, CONV_WIDTH):
            du_s = pltpu.roll(ext, tm + 8 - s, 0)[0:tm]
            dup = dup + cw_ref[CONV_WIDTH - 1 - s:CONV_WIDTH - s, :] * du_s
            dcw_ref[CONV_WIDTH - 1 - s:CONV_WIDTH - s, :] += jnp.sum(up * du_s, axis=0, keepdims=True)
        dutail_sc[...] = du[0:8]
        dp_ref[:, :w] = dgb.astype(BF16)
        dp_ref[:, w:] = dup.astype(BF16)

        @pl.when(step == nt - 1)
        def _():
            dlam_ref[...] = dls_sc[0:1, :] * _sigmoid(-lam_ref[...])

    rev = lambda col: (lambda i: (nt - 1 - i, col))
    vec = _full((1, w))
    wspec = _full((LRU_BLOCKS, LRU_BLOCK, LRU_BLOCK))
    tile = pl.BlockSpec((tm, w), rev(0))
    prev8 = pl.BlockSpec((8, w), lambda i: (jnp.maximum((nt - 1 - i) * (tm // 8) - 1, 0), 0))
    return _side_call(
        body, side, name="lru_bwd", steps=nt,
        in_specs=[tile, pl.BlockSpec((tm, w), rev(1)), _full((CONV_WIDTH, w)), wspec, wspec, vec,
                  tile, tile, tile, tile, prev8, tile],
        out_specs=[pl.BlockSpec((tm, 2 * w), rev(0)), _full((CONV_WIDTH, w)), vec, wspec, vec, wspec, vec, vec],
        out_shape=[_sds((t, 2 * w), BF16), _sds((CONV_WIDTH, w), F32), _sds((1, w), F32),
                   _sds((LRU_BLOCKS, LRU_BLOCK, LRU_BLOCK), F32), _sds((1, w), F32),
                   _sds((LRU_BLOCKS, LRU_BLOCK, LRU_BLOCK), F32), _sds((1, w), F32), _sds((1, w), F32)],
        scratch_shapes=[pltpu.VMEM((8, w), F32), pltpu.VMEM((8, w), F32), pltpu.VMEM((8, w), F32)],
        args=(proj, proj, cw, wa, wx, lam, u, r, ig, h, h, dy))


LANES = 1024
HALF_FFN = D_FF // N_CHIPS
GROUPS = {
    "ml": (("ml_w_in", 1024, ML_IN // N_CHIPS), ("ml_w_out", 256, 1024)),
    "lru": (("lru_w_in", 1024, 512),
            ("lru_gates", 2 * LRU_BLOCKS * 64, LRU_BLOCK),
            ("lru_w_out", 256, 1024)),
    "ffn0": (("ffn_gu0", 1024, 2 * HALF_FFN), ("ffn_down0", HALF_FFN, 1024)),
    "ffn1": (("ffn_gu1", 1024, 2 * HALF_FFN), ("ffn_down1", HALF_FFN, 1024)),
}

SMALL_ROWS = 24
ROW_LOSS, ROW_HEAD_NORM, ROW_B_IF, ROW_LRU = 8, 9, 10, 11


def _row_tile(rows, cols, itemsize, budget=3 << 19):
    best = 16
    for t in range(16, rows + 1, 16):
        if rows % t == 0 and t * cols * itemsize <= budget:
            best = t
    return best


def _chip_peers():
    x, y, c = lax.axis_index("x"), lax.axis_index("y"), lax.axis_index("c")
    return x, y, c, [(1 - x, y), (x, 1 - y), (1 - x, 1 - y)]


HBM_SPEC = pl.BlockSpec(memory_space=pltpu.HBM)


def _remote(src, dst, send_sems, recv_sems, k, to):
    return pltpu.make_async_remote_copy(src_ref=src, dst_ref=dst, send_sem=send_sems.at[k], recv_sem=recv_sems.at[k],
                                        device_id=to, device_id_type=MESH)


GATHER_SEMS = 7


def _gather_copies(kind, specs, part_refs, out_refs, send_sems, recv_sems):
    x, y, c, chips = _chip_peers()
    me = 2 * x + y
    sib = (x, y, 1 - c)
    copy = functools.partial(_remote, send_sems=send_sems, recv_sems=recv_sems)
    out = []
    for p, (_, rows, _) in enumerate(specs):
        mine = pl.ds(c * (rows // 2), rows // 2)
        theirs = pl.ds((1 - c) * (rows // 2), rows // 2)
        base = GATHER_SEMS * p
        for j, (cx, cy) in enumerate(chips):
            land = out_refs[p].at[2 * cx + cy, mine]
            other = out_refs[p].at[2 * cx + cy, theirs]
            if kind == "first":
                out.append(copy(part_refs[p].at[mine], out_refs[p].at[me, mine], k=base + j, to=(cx, cy, c)))
            elif kind == "landed":
                out.append(copy(land, land, k=base + j, to=sib))
            elif kind == "forward":
                out.append(copy(land, land, k=base + 3 + j, to=sib))
            else:
                out.append(copy(other, other, k=base + 3 + j, to=sib))
        if kind in ("first", "arriving"):
            out.append(copy(part_refs[p], out_refs[p].at[me], k=base + 6, to=sib))
    return out


def _gather_start(specs, *refs):
    for cp in _gather_copies("first", specs, *refs):
        cp.start()


def _gather_finish(specs, *refs):
    forwards = _gather_copies("forward", specs, *refs)
    for land, fwd in zip(_gather_copies("landed", specs, *refs), forwards):
        land.wait_recv()
        fwd.start()
    for cp in _gather_copies("arriving", specs, *refs):
        cp.wait_recv()
    for cp in _gather_copies("first", specs, *refs) + forwards:
        cp.wait_send()


def _gather_shapes(specs):
    return [_sds((N_CHIPS, rows, cols), BF16) for _, rows, cols in specs]


def _gather_side(specs, parts):
    return _Side(inputs=list(parts), out_shape=_gather_shapes(specs), n_sems=GATHER_SEMS * len(specs),
                 start=functools.partial(_gather_start, specs), finish=functools.partial(_gather_finish, specs))


def _gather_weights(specs, parts, small):
    n = len(specs)

    def body(*refs):
        part_refs, small_ref = refs[:n], refs[n]
        out_refs, outs_ref = refs[n + 1:2 * n + 1], refs[2 * n + 1]
        send_sems, recv_sems, small_send, small_recv, loc_sem = refs[2 * n + 2:]
        x, y, c, chips = _chip_peers()
        me = 2 * x + y
        local = pltpu.make_async_copy(small_ref, outs_ref.at[me], loc_sem.at[0])
        local.start()
        _gather_start(specs, part_refs, out_refs, send_sems, recv_sems)
        sent = [_remote(small_ref, outs_ref.at[me], small_send, small_recv, j, (cx, cy, c))
                for j, (cx, cy) in enumerate(chips)]
        for cp in sent:
            cp.start()
        _gather_finish(specs, part_refs, out_refs, send_sems, recv_sems)
        for j, (cx, cy) in enumerate(chips):
            _remote(small_ref, outs_ref.at[2 * cx + cy], small_send, small_recv, j, (cx, cy, c)).wait_recv()
        for cp in sent:
            cp.wait_send()
        local.wait()

    dma = pltpu.SemaphoreType.DMA
    return pl.pallas_call(
        body, name="gather_weights",
        in_specs=[HBM_SPEC] * (n + 1), out_specs=[HBM_SPEC] * (n + 1),
        out_shape=_gather_shapes(specs) + [_sds((N_CHIPS,) + small.shape, small.dtype)],
        scratch_shapes=[dma((GATHER_SEMS * n,)), dma((GATHER_SEMS * n,)), dma((3,)), dma((3,)), dma((1,))],
    )(*parts, small)


def _exchange_halves(specs, gparts, name):
    n = len(specs)

    def body(*refs):
        g_refs, out_refs = refs[:n], refs[n:2 * n]
        send_sems, recv_sems = refs[2 * n:]
        x, y, c, _ = _chip_peers()
        sent = []
        for p, (_, rows, _) in enumerate(specs):
            theirs = pl.ds((1 - c) * (rows // 2), rows // 2)
            sent.append(_remote(g_refs[p].at[:, theirs], out_refs[p], send_sems, recv_sems, p, (x, y, 1 - c)))
            sent[-1].start()
        for cp in sent:
            cp.wait()

    return pl.pallas_call(
        body, name=name, in_specs=[HBM_SPEC] * n, out_specs=[HBM_SPEC] * n,
        out_shape=[_sds((N_CHIPS, rows // 2, cols), BF16) for _, rows, cols in specs],
        scratch_shapes=[pltpu.SemaphoreType.DMA((n,)), pltpu.SemaphoreType.DMA((n,))],
    )(*gparts)


def _add_halves(g, recv, pos, name):
    _, half, cols = recv.shape
    tr = _row_tile(half, cols, 2)
    tiles = half // tr

    def body(pos_ref, a_ref, b_ref, o_ref):
        o_ref[...] = (a_ref[...].astype(F32) + b_ref[...].astype(F32)).astype(BF16)

    spec = pl.BlockSpec((1, tr, cols), lambda k, i, pos_ref: (k, i, 0))
    return pl.pallas_call(
        body, name=name,
        grid_spec=pltpu.PrefetchScalarGridSpec(
            num_scalar_prefetch=1, grid=(N_CHIPS, tiles),
            in_specs=[pl.BlockSpec((1, tr, cols), lambda k, i, pos_ref: (k, pos_ref[0] * tiles + i, 0)), spec],
            out_specs=spec),
        out_shape=_sds((N_CHIPS, half, cols), BF16),
        compiler_params=_params("parallel", "parallel"),
    )(pos, g, recv)


def _scatter_copies(n, s1_refs, recv_refs, send_sems, recv_sems):
    x, y, c, chips = _chip_peers()
    return [_remote(s1_refs[p].at[2 * cx + cy], recv_refs[p].at[j], send_sems, recv_sems, 3 * p + j, (cx, cy, c))
            for p in range(n) for j, (cx, cy) in enumerate(chips)]


def _scatter_start(n, s1_refs, recv_refs, send_sems, recv_sems):
    for cp in _scatter_copies(n, s1_refs, recv_refs, send_sems, recv_sems):
        cp.start()


def _scatter_finish(n, s1_refs, recv_refs, send_sems, recv_sems):
    for cp in _scatter_copies(n, s1_refs, recv_refs, send_sems, recv_sems):
        cp.wait()


def _scatter_shapes(s1):
    return [_sds((3,) + a.shape[1:], a.dtype) for a in s1]


def _scatter_side(s1):
    n = len(s1)
    return _Side(inputs=list(s1), out_shape=_scatter_shapes(s1), n_sems=3 * n,
                 start=functools.partial(_scatter_start, n), finish=functools.partial(_scatter_finish, n))


def _scatter_to_chips(s1, small):
    n = len(s1)
    flips = [(fx, fy, fc) for fx in (0, 1) for fy in (0, 1) for fc in (0, 1)][1:]

    def body(*refs):
        s1_refs, small_ref = refs[:n], refs[n]
        recv_refs, all_ref = refs[n + 1:2 * n + 1], refs[2 * n + 1]
        send_sems, recv_sems, small_send, small_recv, loc_sem = refs[2 * n + 2:]
        x, y, c, _ = _chip_peers()
        my_slot = all_ref.at[4 * x + 2 * y + c]
        local = pltpu.make_async_copy(small_ref, my_slot, loc_sem.at[0])
        local.start()
        _scatter_start(n, s1_refs, recv_refs, send_sems, recv_sems)
        peers = [(1 - x if fx else x, 1 - y if fy else y, 1 - c if fc else c) for fx, fy, fc in flips]
        sent = [_remote(small_ref, my_slot, small_send, small_recv, i, p) for i, p in enumerate(peers)]
        for cp in sent:
            cp.start()
        _scatter_finish(n, s1_refs, recv_refs, send_sems, recv_sems)
        for i, (px, py, pc) in enumerate(peers):
            _remote(small_ref, all_ref.at[4 * px + 2 * py + pc], small_send, small_recv, i, peers[i]).wait_recv()
        for cp in sent:
            cp.wait_send()
        local.wait()

    dma = pltpu.SemaphoreType.DMA
    return pl.pallas_call(
        body, name="scatter_to_chips", in_specs=[HBM_SPEC] * (n + 1), out_specs=[HBM_SPEC] * (n + 1),
        out_shape=_scatter_shapes(s1) + [_sds((8,) + small.shape, small.dtype)],
        scratch_shapes=[dma((3 * n,)), dma((3 * n,)), dma((7,)), dma((7,)), dma((1,))],
    )(*s1, small)


def _sum_chips(s1, recv, pos, name):
    _, half, cols = recv.shape
    tr = _row_tile(half, cols, 4)
    tiles = half // tr

    def body(pos_ref, a_ref, b_ref, o_ref):
        acc = a_ref[0].astype(F32)
        for j in range(3):
            acc = acc + b_ref[j].astype(F32)
        o_ref[...] = acc

    return pl.pallas_call(
        body, name=name,
        grid_spec=pltpu.PrefetchScalarGridSpec(
            num_scalar_prefetch=1, grid=(tiles,),
            in_specs=[pl.BlockSpec((1, tr, cols), lambda i, pos_ref: (pos_ref[1], i, 0)),
                      pl.BlockSpec((3, tr, cols), lambda i, pos_ref: (0, i, 0))],
            out_specs=pl.BlockSpec((tr, cols), lambda i, pos_ref: (pos_ref[0] * tiles + i, 0))),
        out_shape=_sds((2 * half, cols), F32),
        compiler_params=_params("parallel"),
    )(pos, s1, recv)


def _sum_small(small_all):
    def body(a_ref, o_ref):
        acc = a_ref[0]
        for d in range(1, 8):
            acc = acc + a_ref[d]
        o_ref[...] = acc

    return pl.pallas_call(body, name="sum_small", out_shape=_sds(small_all.shape[1:], F32))(small_all)


def _join_halves(specs, s2):
    n = len(specs)

    def body(*refs):
        buf_refs = refs[n:2 * n]
        send_sems, recv_sems = refs[2 * n:]
        x, y, c, _ = _chip_peers()
        sent = []
        for p, (_, rows, _) in enumerate(specs):
            mine = buf_refs[p].at[pl.ds(c * (rows // 2), rows // 2)]
            sent.append(_remote(mine, mine, send_sems, recv_sems, p, (x, y, 1 - c)))
            sent[-1].start()
        for p, (_, rows, _) in enumerate(specs):
            theirs = buf_refs[p].at[pl.ds((1 - c) * (rows // 2), rows // 2)]
            _remote(theirs, theirs, send_sems, recv_sems, p, (x, y, 1 - c)).wait_recv()
        for cp in sent:
            cp.wait_send()

    return pl.pallas_call(
        body, name="join_halves", in_specs=[HBM_SPEC] * n, out_specs=[HBM_SPEC] * n,
        out_shape=[_sds(a.shape, a.dtype) for a in s2],
        input_output_aliases={p: p for p in range(n)},
        scratch_shapes=[pltpu.SemaphoreType.DMA((n,)), pltpu.SemaphoreType.DMA((n,))],
    )(*s2)


def _adamw(w, g, m, v, name):
    rows, cols = w.shape
    tm = rows
    for cand in (512, 256, 128, 64, 32, 16, 8):
        if rows % cand == 0 and rows > cand:
            tm = cand
            break

    def body(w_ref, g_ref, m_ref, v_ref, d_ref, nm_ref, nv_ref):
        gv = g_ref[...]
        nm = ADAM_B1 * m_ref[...] + (1.0 - ADAM_B1) * gv
        nv = ADAM_B2 * v_ref[...] + (1.0 - ADAM_B2) * (gv * gv)
        m_hat = nm / (1.0 - ADAM_B1 ** ADAM_STEP)
        v_hat = nv / (1.0 - ADAM_B2 ** ADAM_STEP)
        d_ref[...] = -ADAM_LR * (m_hat / (jnp.sqrt(v_hat) + ADAM_EPS) + ADAM_WD * w_ref[...])
        nm_ref[...] = nm
        nv_ref[...] = nv

    spec = _rows(tm, cols)
    return pl.pallas_call(
        body, name=name, grid=(rows // tm,), in_specs=[spec] * 4, out_specs=[spec] * 3,
        out_shape=[_sds((rows, cols), F32)] * 3, compiler_params=_params("parallel"),
    )(w, g, m, v)


WEIGHTS = ("ml_w_in", "ml_b_if", "ml_head_norm", "ml_w_out", "lru_w_in", "lru_conv_w", "lru_conv_b", "lru_w_gate_a",
           "lru_b_gate_a", "lru_w_gate_x", "lru_b_gate_x", "lru_lambda", "lru_w_out", "norm_pre_mix", "norm_post_mix",
           "norm_pre_ffn", "norm_post_ffn", "ffn_w_gate", "ffn_w_up", "ffn_w_down")
LRU_VECTORS = ("lru_conv_b", "lru_b_gate_a", "lru_b_gate_x", "lru_lambda")
NORMS = ("norm_pre_mix", "norm_post_mix", "norm_pre_ffn", "norm_post_ffn")


def _by_cols(a):
    return jnp.concatenate([a[k] for k in range(N_CHIPS)], axis=-1)


def _by_chip(a, width):
    return jnp.stack([a[..., k * width:(k + 1) * width] for k in range(N_CHIPS)])


def _weight_shards(w):
    bf = lambda a: a.astype(BF16)
    shards = dict(
        ml_w_in=bf(w["ml_w_in"][0]), ml_w_out=bf(w["ml_w_out"][0]), lru_w_in=bf(w["lru_w_in"][0]),
        lru_gates=bf(jnp.concatenate([w["lru_w_gate_a"][0], w["lru_w_gate_x"][0]], axis=0)).reshape(-1, LRU_BLOCK),
        lru_w_out=bf(w["lru_w_out"][0]))
    for layer in range(2):
        shards[f"ffn_gu{layer}"] = bf(jnp.concatenate([w["ffn_w_gate"][layer], w["ffn_w_up"][layer]], axis=1))
        shards[f"ffn_down{layer}"] = bf(w["ffn_w_down"][layer])
    return shards


def _ffn_weights(gu, down):
    w_gu = jnp.concatenate([_by_cols(gu[..., :HALF_FFN]), _by_cols(gu[..., HALF_FFN:])], axis=-1)
    return w_gu, down.reshape(D_FF, 1024)


def _ffn_grad_parts(dw_gu, dw_down):
    gu = jnp.concatenate([_by_chip(dw_gu[:, :D_FF], HALF_FFN), _by_chip(dw_gu[:, D_FF:], HALF_FFN)], axis=-1)
    return [gu, dw_down.reshape(N_CHIPS, HALF_FFN, 1024)]


def _half_reduce(group, gparts, pos):
    specs = GROUPS[group]
    recv = _exchange_halves(specs, gparts, "exchange_halves_" + group)
    return [_add_halves(g, r, pos, "add_halves_" + n) for (n, _, _), g, r in zip(specs, gparts, recv)]


def _sum_group(group, s1, recv, pos):
    return [_sum_chips(a, r, pos, "sum_chips_" + n) for (n, _, _), a, r in zip(GROUPS[group], s1, recv)]


def _update(w, grads, m, v):
    delta, new_m, new_v = {}, {}, {}
    for n in WEIGHTS:
        shape = w[n].shape
        as2d = lambda a: a.reshape(-1, shape[-1])
        d, nm, nv = _adamw(as2d(w[n]), as2d(grads[n]), as2d(m[n]), as2d(v[n]), "adamw_" + n)
        delta[n], new_m[n], new_v[n] = d.reshape(shape), nm.reshape(shape), nv.reshape(shape)
    return delta, new_m, new_v


def kernel(x, ml_w_in, ml_b_if, ml_head_norm, ml_w_out, lru_w_in, lru_conv_w, lru_conv_b, lru_w_gate_a, lru_b_gate_a, lru_w_gate_x, lru_b_gate_x, lru_lambda, lru_w_out, norm_pre_mix, norm_post_mix, norm_pre_ffn, norm_post_ffn, ffn_w_gate, ffn_w_up, ffn_w_down, loss_target, m_ml_w_in, m_ml_b_if, m_ml_head_norm, m_ml_w_out, m_lru_w_in, m_lru_conv_w, m_lru_conv_b, m_lru_w_gate_a, m_lru_b_gate_a, m_lru_w_gate_x, m_lru_b_gate_x, m_lru_lambda, m_lru_w_out, m_norm_pre_mix, m_norm_post_mix, m_norm_pre_ffn, m_norm_post_ffn, m_ffn_w_gate, m_ffn_w_up, m_ffn_w_down, v_ml_w_in, v_ml_b_if, v_ml_head_norm, v_ml_w_out, v_lru_w_in, v_lru_conv_w, v_lru_conv_b, v_lru_w_gate_a, v_lru_b_gate_a, v_lru_w_gate_x, v_lru_b_gate_x, v_lru_lambda, v_lru_w_out, v_norm_pre_mix, v_norm_post_mix, v_norm_pre_ffn, v_norm_post_ffn, v_ffn_w_gate, v_ffn_w_up, v_ffn_w_down):
    args = locals()
    w = {n: args[n] for n in WEIGHTS}
    m = {n: args["m_" + n] for n in WEIGHTS}
    v = {n: args["v_" + n] for n in WEIGHTS}
    xs, target = x[0], loss_target[0]
    mx, my, mc, _ = _chip_peers()
    chip = 2 * mx + my
    pos = jnp.stack([mc, chip])
    row = lambda a, i: a[i:i + 1]
    npm, nqm, npf, nqf = (w[n] for n in NORMS)
    shards = _weight_shards(w)
    of = lambda group: [shards[n] for n, _, _ in GROUPS[group]]
    bif = jnp.pad(w["ml_b_if"], ((0, 0), (0, 128 - 2 * ML_HEADS)))
    hn = w["ml_head_norm"]

    small = jnp.concatenate([w["lru_conv_w"][0]] + [w[n] for n in LRU_VECTORS], axis=0)
    ml_in_parts, ml_out_parts, smalls = _gather_weights(GROUPS["ml"], of("ml"), small)
    ml_w_in = jnp.pad(_by_cols(ml_in_parts), ((0, 0), (0, ML_IN_PAD - ML_IN)))
    ml_w_out = ml_out_parts.reshape(1024, 1024)
    vec = _by_cols(smalls)
    z0, qkv, og = _norm_matmul(xs, row(npm, 0), ml_w_in, 2 * ML_QK + ML_V, "ml_in")
    (y0, hs, cst, nst, mst), got = _mlstm_fwd(
        qkv, og, bif, hn, _gather_side(GROUPS["lru"] + GROUPS["ffn0"], of("lru") + of("ffn0")))
    lru_w_in = _by_cols(got[0])
    gates = got[1].reshape(N_CHIPS, 2, LRU_BLOCKS, 64, LRU_BLOCK).transpose(1, 2, 0, 3, 4)
    gates = gates.reshape(2, LRU_BLOCKS, LRU_BLOCK, LRU_BLOCK)
    lru_w_out = got[2].reshape(1024, 1024)
    w_gu0, w_down0 = _ffn_weights(got[3], got[4])
    ymix0, h1 = _matmul_postnorm(y0, ml_w_out, row(nqm, 0), xs, "ml_out")
    (zf0, gu0, yf0, h2), got = _ffn_forward(h1, row(npf, 0), w_gu0, w_down0, row(nqf, 0), "ffn0_fwd",
                                            _gather_side(GROUPS["ffn1"], of("ffn1")))
    w_gu1, w_down1 = _ffn_weights(got[0], got[1])
    z1, proj1 = _norm_matmul(h2, row(npm, 1), lru_w_in, 0, "lru_in")
    y1, u, r, ig, hl = _lru_fwd(proj1, vec[0:4], vec[4:5], gates[0], vec[5:6], gates[1], vec[6:7], vec[7:8])
    ymix1, h3 = _matmul_postnorm(y1, lru_w_out, row(nqm, 1), h2, "lru_out")
    (zf1, gu1, yf1, h4), _ = _ffn_forward(h3, row(npf, 1), w_gu1, w_down1, row(nqf, 1), "ffn1_fwd", None)
    dh4, loss_part = _loss_head(h4, target, "loss_head")

    (dyf1, dgu1, act1, dqf1), _ = _bwd_out(dh4, yf1, row(nqf, 1), w_down1, gu1, "ffn1_bwd_down", None)
    dw_down1 = _matmul_tn(act1, dyf1, True, 256, "ffn1_dw_down")
    dw_gu1 = _matmul_tn(zf1, dgu1, False, 512, "ffn1_dw_up")
    dh3, dpf1 = _bwd_in(dgu1, w_gu1, h3, row(npf, 1), dh4, "ffn1_bwd_up")
    s1_ffn1 = _half_reduce("ffn1", _ffn_grad_parts(dw_gu1, dw_down1), pos)

    (dymix1, dy1, dqm1), _ = _bwd_out(dh3, ymix1, row(nqm, 1), lru_w_out, None, "lru_bwd_out", None)
    dw_lru_out = _matmul_tn(y1, dymix1, False, 256, "lru_dw_out")
    (dproj1, dcw, dcb, dwa, dba, dwx, dbx, dlam), recv_ffn1 = _lru_bwd(
        proj1, vec[0:4], gates[0], gates[1], vec[7:8], u, r, ig, hl, dy1, _scatter_side(s1_ffn1))
    dw_lru_in = _matmul_tn(z1, dproj1, False, 512, "lru_dw_in")
    dh2, dpm1 = _bwd_in(dproj1, lru_w_in, h2, row(npm, 1), dh3, "lru_bwd_in")
    dgates = jnp.stack([dwa, dwx]).astype(BF16).reshape(2, LRU_BLOCKS, N_CHIPS, 64, LRU_BLOCK)
    dgates = dgates.transpose(2, 0, 1, 3, 4).reshape(N_CHIPS, -1, LRU_BLOCK)
    s1_lru = _half_reduce("lru", [_by_chip(dw_lru_in, 512), dgates, dw_lru_out.reshape(N_CHIPS, 256, 1024)], pos)

    (dyf0, dgu0, act0, dqf0), recv_lru = _bwd_out(dh2, yf0, row(nqf, 0), w_down0, gu0, "ffn0_bwd_down",
                                                  _scatter_side(s1_lru))
    dw_down0 = _matmul_tn(act0, dyf0, True, 256, "ffn0_dw_down")
    dw_gu0 = _matmul_tn(zf0, dgu0, False, 512, "ffn0_dw_up")
    dh1, dpf0 = _bwd_in(dgu0, w_gu0, h1, row(npf, 0), dh2, "ffn0_bwd_up")
    s1_ffn0 = _half_reduce("ffn0", _ffn_grad_parts(dw_gu0, dw_down0), pos)

    (dymix0, dy0, dqm0), _ = _bwd_out(dh1, ymix0, row(nqm, 0), ml_w_out, None, "ml_bwd_out", None)
    dw_ml_out = _matmul_tn(y0, dymix0, False, 256, "ml_dw_out")
    (dproj0, dhn, dbif), recv_ffn0 = _mlstm_bwd(qkv, og, bif, hn, hs, cst, nst, mst, dy0, _scatter_side(s1_ffn0))
    dw_ml_in = _matmul_tn(z0, dproj0, False, 640, "ml_dw_in")
    dx, dpm0 = _bwd_in(dproj0, ml_w_in, xs, row(npm, 0), dh1, "ml_bwd_in")
    s1_ml = _half_reduce(
        "ml", [_by_chip(dw_ml_in[:, :ML_IN], ML_IN // N_CHIPS), dw_ml_out.reshape(N_CHIPS, 256, 1024)], pos)

    pad_lanes = lambda a: jnp.pad(a, ((0, 0), (0, LANES - a.shape[1])))
    small = jnp.concatenate(
        [jnp.concatenate([dpm0, dpm1]), jnp.concatenate([dqm0, dqm1]), jnp.concatenate([dpf0, dpf1]),
         jnp.concatenate([dqf0, dqf1]), pad_lanes(loss_part), dhn, pad_lanes(dbif), dcw, dcb, dba, dbx, dlam,
         jnp.zeros((SMALL_ROWS - 19, LANES), F32)], axis=0)
    *recv_ml, small_all = _scatter_to_chips(s1_ml, small)
    order = ("ml", "lru", "ffn0", "ffn1")
    s2 = (_sum_group("ml", s1_ml, recv_ml, pos) + _sum_group("lru", s1_lru, recv_lru, pos)
          + _sum_group("ffn0", s1_ffn0, recv_ffn0, pos) + _sum_group("ffn1", s1_ffn1, recv_ffn1, pos))
    specs = sum((GROUPS[g] for g in order), ())
    red = dict(zip([n for n, _, _ in specs], _join_halves(specs, s2)))
    vsum = _sum_small(small_all)

    dgates = red["lru_gates"].reshape(2, LRU_BLOCKS, 64, LRU_BLOCK)
    cols = lambda a: lax.dynamic_slice_in_dim(a, chip * 256, 256, axis=1)
    grads = dict(
        ml_w_in=red["ml_w_in"], ml_w_out=red["ml_w_out"], lru_w_in=red["lru_w_in"], lru_w_gate_a=dgates[0],
        lru_w_gate_x=dgates[1], lru_w_out=red["lru_w_out"],
        ffn_w_gate=jnp.stack([red["ffn_gu0"][:, :HALF_FFN], red["ffn_gu1"][:, :HALF_FFN]]),
        ffn_w_up=jnp.stack([red["ffn_gu0"][:, HALF_FFN:], red["ffn_gu1"][:, HALF_FFN:]]),
        ffn_w_down=jnp.stack([red["ffn_down0"], red["ffn_down1"]]),
        ml_head_norm=vsum[ROW_HEAD_NORM:ROW_HEAD_NORM + 1], ml_b_if=vsum[ROW_B_IF:ROW_B_IF + 1, :2 * ML_HEADS],
        lru_conv_w=cols(vsum[ROW_LRU:ROW_LRU + 4]))
    for i, n in enumerate(NORMS):
        grads[n] = vsum[2 * i:2 * i + 2]
    for i, n in enumerate(LRU_VECTORS):
        grads[n] = cols(vsum[ROW_LRU + 4 + i:ROW_LRU + 5 + i])
    loss = vsum[ROW_LOSS, 0]
    grads = {n: grads[n].reshape(w[n].shape) for n in WEIGHTS}
    delta, new_m, new_v = _update(w, grads, m, v)
    return (loss, dx[None], *[grads[n] for n in WEIGHTS], *[delta[n] for n in WEIGHTS],
            *[new_m[n] for n in WEIGHTS], *[new_v[n] for n in WEIGHTS])
```

```python
import functools
import math
from typing import Callable, NamedTuple

import jax
import jax.numpy as jnp
from jax import lax
from jax.experimental import pallas as pl
from jax.experimental.pallas import tpu as pltpu

F32 = jnp.float32
BF16 = jnp.bfloat16
MESH = pl.DeviceIdType.MESH

D_MODEL = 1024
D_FF = 2816
ML_HEADS = 8
ML_DK = 64
ML_DV = 128
ML_QK = ML_HEADS * ML_DK
ML_V = ML_HEADS * ML_DV
ML_IN = 2 * ML_QK + 2 * ML_V + 2 * ML_HEADS
ML_IN_PAD = 3200
CHUNK = 64
GATE_CAP = 15.0
ML_M_INIT = -1e30
NEG_BIG = -1e30
LRU_BLOCKS = 4
LRU_BLOCK = 256
CONV_WIDTH = 4
LRU_C = 8.0
EPS = 1e-6
QK_SCALE = ML_DK ** -0.5

ADAM_LR = 0.001
ADAM_B1 = 0.9
ADAM_B2 = 0.999
ADAM_EPS = 1e-08
ADAM_WD = 0.01
ADAM_STEP = 10

N_CHIPS = 4
V7X_VMEM_LIMIT = 56 * 1024 * 1024

NT_DIMS = (((1,), (1,)), ((), ()))
TN_DIMS = (((0,), (0,)), ((), ()))


def _params(*semantics):
    return pltpu.CompilerParams(dimension_semantics=semantics, vmem_limit_bytes=V7X_VMEM_LIMIT)


def _sds(shape, dtype):
    return jax.ShapeDtypeStruct(shape, dtype)


def _full(shape):
    return pl.BlockSpec(shape, lambda *_: (0,) * len(shape))


def _rows(tm, n):
    return pl.BlockSpec((tm, n), lambda i: (i, 0))


def _sigmoid(x):
    return 1.0 / (1.0 + jnp.exp(-x))


def _log_sigmoid(x):
    return jnp.minimum(x, 0.0) - jnp.log1p(jnp.exp(-jnp.abs(x)))


def _rms(x):
    r = lax.rsqrt(jnp.mean(x * x, axis=-1, keepdims=True) + EPS)
    return x * r, r


def _rms_bwd(xhat, r, g, dy):
    dxh = dy * g
    return r * (dxh - xhat * jnp.mean(dxh * xhat, axis=-1, keepdims=True))


def _scan_rows(a, b, reverse=False):
    n = a.shape[0]
    row = lax.broadcasted_iota(jnp.int32, a.shape, 0)
    s = 1
    while s < n:
        if reverse:
            keep = row < n - s
            shift = n - s
        else:
            keep = row >= s
            shift = s
        b = b + a * jnp.where(keep, pltpu.roll(b, shift, 0), 0.0)
        a = a * jnp.where(keep, pltpu.roll(a, shift, 0), 1.0)
        s *= 2
    return a, b


def _cumsum_rows(x, reverse=False):
    n = x.shape[0]
    row = lax.broadcasted_iota(jnp.int32, x.shape, 0)
    s = 1
    while s < n:
        if reverse:
            x = x + jnp.where(row < n - s, pltpu.roll(x, n - s, 0), 0.0)
        else:
            x = x + jnp.where(row >= s, pltpu.roll(x, s, 0), 0.0)
        s *= 2
    return x


def _norm_matmul(h, g, w, n_bf16, name):
    t, d = h.shape
    n = w.shape[1]
    tm = min(512, t)

    def body(h_ref, g_ref, w_ref, z_ref, *o_refs):
        xhat, _ = _rms(h_ref[...])
        z = (xhat * g_ref[...]).astype(BF16)
        z_ref[...] = z
        out = jnp.dot(z, w_ref[...], preferred_element_type=F32)
        if n_bf16:
            o_refs[0][...] = out[:, :n_bf16].astype(BF16)
            o_refs[1][...] = out[:, n_bf16:]
        else:
            o_refs[0][...] = out

    if n_bf16:
        out_specs = [_rows(tm, d), _rows(tm, n_bf16), _rows(tm, n - n_bf16)]
        out_shape = [_sds((t, d), BF16), _sds((t, n_bf16), BF16), _sds((t, n - n_bf16), F32)]
    else:
        out_specs = [_rows(tm, d), _rows(tm, n)]
        out_shape = [_sds((t, d), BF16), _sds((t, n), F32)]
    return pl.pallas_call(
        body, name=name, grid=(t // tm,),
        in_specs=[_rows(tm, d), _full((1, d)), _full((d, n))],
        out_specs=out_specs, out_shape=out_shape, compiler_params=_params("parallel"),
    )(h, g, w)


def _matmul_postnorm(a, w, g, res, name):
    t = res.shape[0]
    k, d = w.shape
    tm = 256

    def body(a_ref, w_ref, g_ref, res_ref, y_ref, o_ref):
        y = jnp.dot(a_ref[...], w_ref[...], preferred_element_type=F32)
        y_ref[...] = y
        yhat, _ = _rms(y)
        o_ref[...] = res_ref[...] + yhat * g_ref[...]

    return pl.pallas_call(
        body, name=name, grid=(t // tm,),
        in_specs=[_rows(tm, k), _full((k, d)), _full((1, d)), _rows(tm, d)],
        out_specs=[_rows(tm, d), _rows(tm, d)],
        out_shape=[_sds((t, d), F32), _sds((t, d), F32)],
        compiler_params=_params("parallel"),
    )(a, w, g, res)


class _Side(NamedTuple):
    inputs: list
    out_shape: list
    n_sems: int
    start: Callable
    finish: Callable


def _side_call(body, side, *, name, steps, in_specs, out_specs, out_shape, scratch_shapes, args):
    n_in, n_out, n_scr = len(in_specs), len(out_specs), len(scratch_shapes)
    if side is None:
        outs = pl.pallas_call(
            body, name=name, grid=(steps,), in_specs=in_specs, out_specs=out_specs, out_shape=out_shape,
            scratch_shapes=scratch_shapes, compiler_params=_params("arbitrary"))(*args)
        return list(outs), []
    s_in, s_out = len(side.inputs), len(side.out_shape)

    def carrying(*refs):
        ins, side_ins = refs[:n_in], refs[n_in:n_in + s_in]
        outs = refs[n_in + s_in:n_in + s_in + n_out]
        side_outs = refs[n_in + s_in + n_out:n_in + s_in + n_out + s_out]
        scratch = refs[n_in + s_in + n_out + s_out:]
        own, sems = scratch[:n_scr], scratch[n_scr:]

        @pl.when(pl.program_id(0) == 0)
        def _():
            side.start(side_ins, side_outs, *sems)

        body(*ins, *outs, *own)

        @pl.when(pl.program_id(0) == steps - 1)
        def _():
            side.finish(side_ins, side_outs, *sems)

    outs = pl.pallas_call(
        carrying, name=name, grid=(steps,), in_specs=list(in_specs) + [HBM_SPEC] * s_in,
        out_specs=list(out_specs) + [HBM_SPEC] * s_out, out_shape=list(out_shape) + list(side.out_shape),
        scratch_shapes=list(scratch_shapes) + [pltpu.SemaphoreType.DMA((side.n_sems,))] * 2,
        compiler_params=_params("arbitrary"))(*args, *side.inputs)
    return list(outs[:n_out]), list(outs[n_out:])


def _ffn_forward(h, g_pre, wg_t, wu_t, w_down, g_post, name, side):
    t, d = h.shape
    k = w_down.shape[0]
    tm = 256

    def body(h_ref, gpre_ref, wg_ref, wu_ref, wd_ref, gpost_ref, z_ref, gu_ref, y_ref, o_ref):
        hv = h_ref[...]
        xhat, _ = _rms(hv)
        z = (xhat * gpre_ref[...]).astype(BF16)
        z_ref[...] = z
        gate = lax.dot_general(z, wg_ref[...], NT_DIMS, preferred_element_type=F32).astype(BF16)
        up = lax.dot_general(z, wu_ref[...], NT_DIMS, preferred_element_type=F32).astype(BF16)
        gu_ref[:, :k] = gate
        gu_ref[:, k:] = up
        gate = gate.astype(F32)
        act = (gate * _sigmoid(gate) * up.astype(F32)).astype(BF16)
        y = jnp.dot(act, wd_ref[...], preferred_element_type=F32)
        y_ref[...] = y
        yhat, _ = _rms(y)
        o_ref[...] = hv + yhat * gpost_ref[...]

    return _side_call(
        body, side, name=name, steps=t // tm,
        in_specs=[_rows(tm, d), _full((1, d)), _full((k, d)), _full((k, d)), _full((k, d)), _full((1, d))],
        out_specs=[_rows(tm, d), _rows(tm, 2 * k), _rows(tm, d), _rows(tm, d)],
        out_shape=[_sds((t, d), BF16), _sds((t, 2 * k), BF16), _sds((t, d), F32), _sds((t, d), F32)],
        scratch_shapes=[], args=(h, g_pre, wg_t, wu_t, w_down, g_post))


def _loss_head(h, target, name):
    t, d = h.shape
    tm = min(512, t)

    def body(h_ref, t_ref, dh_ref, l_ref):
        @pl.when(pl.program_id(0) == 0)
        def _():
            l_ref[...] = jnp.zeros_like(l_ref)

        err = h_ref[...] - t_ref[...]
        dh_ref[...] = err * (1.0 / d)
        part = jnp.sum(jnp.sum(err * err, axis=1, keepdims=True), axis=0, keepdims=True) * (0.5 / d)
        l_ref[...] += jnp.broadcast_to(part, l_ref.shape)

    return pl.pallas_call(
        body, name=name, grid=(t // tm,),
        in_specs=[_rows(tm, d), _rows(tm, d)],
        out_specs=[_rows(tm, d), _full((1, 128))],
        out_shape=[_sds((t, d), F32), _sds((1, 128), F32)],
        compiler_params=_params("arbitrary"),
    )(h, target)


def _bwd_out(dout, y, g, w, gu, name, side):
    t, d = dout.shape
    k = w.shape[0]
    ffn = gu is not None
    tm = 128 if ffn else 256

    def body(*refs):
        if ffn:
            dout_ref, y_ref, g_ref, w_ref, gu_ref, dy_ref, dact_ref, act_ref, dg_ref = refs
        else:
            dout_ref, y_ref, g_ref, w_ref, dy_ref, dact_ref, dg_ref = refs

        @pl.when(pl.program_id(0) == 0)
        def _():
            dg_ref[...] = jnp.zeros_like(dg_ref)

        do = dout_ref[...]
        yhat, r = _rms(y_ref[...])
        dg_ref[...] += jnp.sum(do * yhat, axis=0, keepdims=True)
        dy = _rms_bwd(yhat, r, g_ref[...], do).astype(BF16)
        dy_ref[...] = dy
        da = lax.dot_general(dy, w_ref[...], NT_DIMS, preferred_element_type=F32)
        if ffn:
            gate = gu_ref[:, :k].astype(F32)
            up = gu_ref[:, k:].astype(F32)
            sg = _sigmoid(gate)
            silu = gate * sg
            act_ref[...] = (silu * up).astype(BF16)
            dact_ref[:, :k] = (da * up * (sg * (1.0 + gate * (1.0 - sg)))).astype(BF16)
            dact_ref[:, k:] = (da * silu).astype(BF16)
        else:
            dact_ref[...] = da

    in_specs = [_rows(tm, d), _rows(tm, d), _full((1, d)), _full((k, d))]
    args = [dout, y, g, w]
    if ffn:
        in_specs.append(_rows(tm, 2 * k))
        args.append(gu)
        out_specs = [_rows(tm, d), _rows(tm, 2 * k), _rows(tm, k), _full((1, d))]
        out_shape = [_sds((t, d), BF16), _sds((t, 2 * k), BF16), _sds((t, k), BF16), _sds((1, d), F32)]
    else:
        out_specs = [_rows(tm, d), _rows(tm, k), _full((1, d))]
        out_shape = [_sds((t, d), BF16), _sds((t, k), F32), _sds((1, d), F32)]
    return _side_call(body, side, name=name, steps=t // tm, in_specs=in_specs, out_specs=out_specs,
                      out_shape=out_shape, scratch_shapes=[], args=args)


def _bwd_in(dp, ws, transposed, h, g, dout, name):
    t, d = h.shape
    n = dp.shape[1]
    tm = 256
    widths = [w.shape[0] if transposed else w.shape[1] for w in ws]

    def body(dp_ref, *refs):
        w_refs = refs[:len(ws)]
        h_ref, g_ref, dout_ref, dh_ref, dg_ref = refs[len(ws):]

        @pl.when(pl.program_id(0) == 0)
        def _():
            dg_ref[...] = jnp.zeros_like(dg_ref)

        dz, at = None, 0
        for w_ref, width in zip(w_refs, widths):
            block = dp_ref[:, at:at + width]
            if transposed:
                part = jnp.dot(block, w_ref[...], preferred_element_type=F32)
            else:
                part = lax.dot_general(block, w_ref[...], NT_DIMS, preferred_element_type=F32)
            dz = part if dz is None else dz + part
            at += width
        hhat, r = _rms(h_ref[...])
        dg_ref[...] += jnp.sum(dz * hhat, axis=0, keepdims=True)
        dh_ref[...] = dout_ref[...] + _rms_bwd(hhat, r, g_ref[...], dz)

    return pl.pallas_call(
        body, name=name, grid=(t // tm,),
        in_specs=[_rows(tm, n)] + [_full(w.shape) for w in ws] + [_rows(tm, d), _full((1, d)), _rows(tm, d)],
        out_specs=[_rows(tm, d), _full((1, d))],
        out_shape=[_sds((t, d), F32), _sds((1, d), F32)],
        compiler_params=_params("arbitrary"),
    )(dp, *ws, h, g, dout)


def _matmul_tn(a, b, tile_a, tile, name):
    t, ka = a.shape
    nb = b.shape[1]

    def body(a_ref, b_ref, o_ref):
        o_ref[...] = lax.dot_general(a_ref[...], b_ref[...], TN_DIMS, preferred_element_type=F32).astype(BF16)

    if tile_a:
        grid = (ka // tile,)
        in_specs = [pl.BlockSpec((t, tile), lambda i: (0, i)), _full((t, nb))]
        out_specs = pl.BlockSpec((tile, nb), lambda i: (i, 0))
    else:
        grid = (nb // tile,)
        in_specs = [_full((t, ka)), pl.BlockSpec((t, tile), lambda i: (0, i))]
        out_specs = pl.BlockSpec((ka, tile), lambda i: (0, i))
    return pl.pallas_call(
        body, name=name, grid=grid, in_specs=in_specs, out_specs=out_specs,
        out_shape=_sds((ka, nb), BF16), compiler_params=_params("parallel"),
    )(a, b)


def _ml_gate_prep(gt, bif):
    th = jnp.tanh((gt + bif) / GATE_CAP)
    act = GATE_CAP * th
    cum = _cumsum_rows(_log_sigmoid(act))
    lane = lax.broadcasted_iota(jnp.int32, gt.shape, 1)
    x = jnp.where(lane < ML_HEADS, act, cum)
    return x, x.T, th, act


HEADS = range(ML_HEADS)


def _each(fn, *per_head):
    return [fn(*a) for a in zip(*per_head)]


def _ml_chunk_fwd(q, k, v, kt, x, xt, c_in, n_in, m_in):
    causal = (lax.broadcasted_iota(jnp.int32, (CHUNK, CHUNK), 0)
              >= lax.broadcasted_iota(jnp.int32, (CHUNK, CHUNK), 1))
    f = {}
    qh = f["qh"] = [q[:, ML_DK * h:ML_DK * (h + 1)] for h in HEADS]
    kh = f["kh"] = [k[:, ML_DK * h:ML_DK * (h + 1)] for h in HEADS]
    f["vh"] = [v[:, ML_DV * h:ML_DV * (h + 1)] for h in HEADS]
    kth = [kt[ML_DK * h:ML_DK * (h + 1), :] for h in HEADS]
    s = _each(lambda a, b: jnp.dot(a, b, preferred_element_type=F32) * QK_SCALE, qh, kth)
    f["qc"] = _each(lambda a, c: jnp.dot(a, c.astype(BF16), preferred_element_type=F32) * QK_SCALE, qh, c_in)
    bcol = f["bcol"] = [x[:, ML_HEADS + h:ML_HEADS + h + 1] for h in HEADS]
    licol = f["licol"] = [x[:, h:h + 1] for h in HEADS]
    brow = f["brow"] = [xt[ML_HEADS + h:ML_HEADS + h + 1, :] for h in HEADS]
    lirow = [xt[h:h + 1, :] for h in HEADS]
    dmat = _each(lambda bc, br, lr: jnp.where(causal, bc - br + lr, NEG_BIG), bcol, brow, lirow)
    inter = _each(lambda bc, m: bc + m, bcol, m_in)
    mt = _each(lambda d, i: jnp.maximum(jnp.max(d, axis=1, keepdims=True), i), dmat, inter)
    wt = f["wt"] = _each(lambda d, m: jnp.exp(d - m), dmat, mt)
    p = f["p"] = _each(lambda a, b: a * b, wt, s)
    winter = f["winter"] = _each(lambda i, m: jnp.exp(i - m), inter, mt)
    qf = f["qf"] = [a.astype(F32) for a in qh]
    qn = f["qn"] = _each(lambda a, n: jnp.sum(a * n, axis=1, keepdims=True) * QK_SCALE, qf, n_in)
    den = f["den"] = _each(lambda a, w, b: jnp.sum(a, axis=1, keepdims=True) + w * b, p, winter, qn)
    emt = f["emt"] = [jnp.exp(-m) for m in mt]
    f["nrm"] = _each(lambda d, e: jnp.maximum(jnp.abs(d), e), den, emt)
    gtot = [bc[CHUNK - 1:CHUNK, :] for bc in bcol]
    a_col = _each(lambda g, bc, lc: g - bc + lc, gtot, bcol, licol)
    a_row = _each(lambda g, br, lr: g - br + lr, gtot, brow, lirow)
    m_new = f["m_new"] = _each(lambda g, m, a: jnp.maximum(g + m, jnp.max(a, axis=1, keepdims=True)),
                               gtot, m_in, a_row)
    f["decay"] = _each(lambda g, m, mn: jnp.exp(g + m - mn), gtot, m_in, m_new)
    wkf = f["wkf"] = _each(lambda a, mn: jnp.exp(a - mn), a_col, m_new)
    f["kw"] = _each(lambda a, w: a.astype(F32) * w, kh, wkf)
    f["ktw"] = _each(lambda a, ar, mn: (a.astype(F32) * jnp.exp(ar - mn)).astype(BF16), kth, a_row, m_new)
    return f


def _ml_specs(nc, rev):
    def at(col):
        if rev:
            return lambda i: (nc - 1 - i, col)
        return lambda i: (i, col)

    return [pl.BlockSpec((CHUNK, ML_QK), at(0)), pl.BlockSpec((CHUNK, ML_QK), at(1)),
            pl.BlockSpec((CHUNK, ML_V), at(1)), pl.BlockSpec((CHUNK, ML_V), at(0)),
            pl.BlockSpec((CHUNK, 128), at(ML_V // 128))]


def _mlstm_fwd(qkv, og, bif, hn, side):
    t = qkv.shape[0]
    nc = t // CHUNK

    def body(q_ref, k_ref, v_ref, o_ref, gt_ref, bif_ref, hn_ref, y_ref, hs_ref, cst_ref, nst_ref, mst_ref,
             c_sc, n_sc, m_sc):
        @pl.when(pl.program_id(0) == 0)
        def _():
            c_sc[...] = jnp.zeros_like(c_sc)
            n_sc[...] = jnp.zeros_like(n_sc)
            m_sc[...] = jnp.full_like(m_sc, ML_M_INIT)

        cst_ref[0] = c_sc[...]
        nst_ref[0] = n_sc[...]
        mst_ref[0] = m_sc[...]
        x, xt, _, _ = _ml_gate_prep(gt_ref[...], bif_ref[...])
        q, k, v = q_ref[...], k_ref[...], v_ref[...]
        c_all, n_all, m_all = c_sc[...], n_sc[...], m_sc[...]
        c_in = [c_all[ML_DK * h:ML_DK * (h + 1), :] for h in HEADS]
        n_in = [n_all[h:h + 1, :] for h in HEADS]
        m_in = [m_all[h:h + 1, 0:1] for h in HEADS]
        f = _ml_chunk_fwd(q, k, v, k.T, x, xt, c_in, n_in, m_in)
        num = _each(lambda p, vh, w, qc: jnp.dot(p.astype(BF16), vh, preferred_element_type=F32) + w * qc,
                    f["p"], f["vh"], f["winter"], f["qc"])
        hh = _each(lambda a, b: a / b, num, f["nrm"])
        hhat = [_rms(a)[0] for a in hh]
        c_new = _each(lambda d, c, kw, vh: d * c + jnp.dot(kw, vh, preferred_element_type=F32),
                      f["decay"], c_in, f["ktw"], f["vh"])
        n_new = _each(lambda d, n, kw: d * n + jnp.sum(kw, axis=0, keepdims=True), f["decay"], n_in, f["kw"])
        for h in HEADS:
            vs = slice(ML_DV * h, ML_DV * (h + 1))
            hs_ref[:, vs] = hh[h]
            y_ref[:, vs] = (hhat[h] * hn_ref[:, vs] * _sigmoid(o_ref[:, vs])).astype(BF16)
            c_sc[ML_DK * h:ML_DK * (h + 1), :] = c_new[h]
            n_sc[h:h + 1, :] = n_new[h]
            m_sc[h:h + 1, :] = jnp.broadcast_to(f["m_new"][h], (1, 128))

    return _side_call(
        body, side, name="mlstm_fwd", steps=nc,
        in_specs=_ml_specs(nc, False) + [_full((1, 128)), _full((1, ML_V))],
        out_specs=[_rows(CHUNK, ML_V), _rows(CHUNK, ML_V),
                   pl.BlockSpec((1, ML_HEADS * ML_DK, ML_DV), lambda i: (i, 0, 0)),
                   pl.BlockSpec((1, ML_HEADS, ML_DK), lambda i: (i, 0, 0)),
                   pl.BlockSpec((1, ML_HEADS, 128), lambda i: (i, 0, 0))],
        out_shape=[_sds((t, ML_V), BF16), _sds((t, ML_V), F32),
                   _sds((nc, ML_HEADS * ML_DK, ML_DV), F32), _sds((nc, ML_HEADS, ML_DK), F32),
                   _sds((nc, ML_HEADS, 128), F32)],
        scratch_shapes=[pltpu.VMEM((ML_HEADS * ML_DK, ML_DV), F32), pltpu.VMEM((ML_HEADS, ML_DK), F32),
                        pltpu.VMEM((ML_HEADS, 128), F32)],
        args=(qkv, qkv, qkv, og, og, bif, hn))


def _mlstm_bwd(qkv, og, bif, hn, hs, cst, nst, mst, dy, side):
    t = qkv.shape[0]
    nc = t // CHUNK

    def body(q_ref, k_ref, v_ref, o_ref, gt_ref, bif_ref, hn_ref, hs_ref, cst_ref, nst_ref, mst_ref, dy_ref,
             dp_ref, dhn_ref, dbif_ref, dc_sc, dct_sc, dn_sc):
        @pl.when(pl.program_id(0) == 0)
        def _():
            dc_sc[...] = jnp.zeros_like(dc_sc)
            dct_sc[...] = jnp.zeros_like(dct_sc)
            dn_sc[...] = jnp.zeros_like(dn_sc)
            dhn_ref[...] = jnp.zeros_like(dhn_ref)
            dbif_ref[...] = jnp.zeros_like(dbif_ref)

        x, xt, th, act = _ml_gate_prep(gt_ref[...], bif_ref[...])
        q, k, v = q_ref[...], k_ref[...], v_ref[...]
        qt, vt = q.T, v.T
        rows = lax.broadcasted_iota(jnp.int32, (CHUNK, CHUNK), 0)
        cols = lax.broadcasted_iota(jnp.int32, (CHUNK, CHUNK), 1)
        lane = lax.broadcasted_iota(jnp.int32, (CHUNK, 128), 1)
        row = lax.broadcasted_iota(jnp.int32, (CHUNK, 1), 0)
        as_row = lambda col: jnp.sum(jnp.where(rows == cols, col, 0.0), axis=0, keepdims=True)
        mm = lambda a, b: jnp.dot(a, b, preferred_element_type=F32)
        bf = lambda a: a.astype(BF16)
        ksl = [slice(ML_DK * h, ML_DK * (h + 1)) for h in HEADS]
        vsl = [slice(ML_DV * h, ML_DV * (h + 1)) for h in HEADS]
        c_in = [cst_ref[0, s, :] for s in ksl]
        n_in = [nst_ref[0, h:h + 1, :] for h in HEADS]
        m_in = [mst_ref[0, h:h + 1, 0:1] for h in HEADS]
        dc_all, dct_all, dn_all = dc_sc[...], dct_sc[...], dn_sc[...]
        dcn = [dc_all[s, :] for s in ksl]
        dcn_t = [dct_all[s, :] for s in vsl]
        dnn = [dn_all[h:h + 1, :] for h in HEADS]
        f = _ml_chunk_fwd(q, k, v, k.T, x, xt, c_in, n_in, m_in)
        qh, kh, vh, p, winter, decay = f["qh"], f["kh"], f["vh"], f["p"], f["winter"], f["decay"]
        qth = [qt[s, :] for s in ksl]
        vth = [vt[s, :] for s in vsl]
        c_t = [bf(c.T) for c in c_in]
        dmat_t = _each(lambda br, bc, lc: jnp.where(rows <= cols, br - bc + lc, NEG_BIG),
                       f["brow"], f["bcol"], f["licol"])
        inter_row = _each(lambda br, m: br + m, f["brow"], m_in)
        mt_row = _each(lambda d, i: jnp.maximum(jnp.max(d, axis=0, keepdims=True), i), dmat_t, inter_row)
        wt_t = _each(lambda d, m: jnp.exp(d - m), dmat_t, mt_row)
        p_t = _each(lambda w, a, b: w * (mm(a, b) * QK_SCALE), wt_t, kh, qth)
        winter_row = _each(lambda i, m: jnp.exp(i - m), inter_row, mt_row)
        hh = [hs_ref[:, s] for s in vsl]
        hn_h = [hn_ref[:, s] for s in vsl]
        sg = [_sigmoid(o_ref[:, s]) for s in vsl]
        dyh = [dy_ref[:, s] for s in vsl]
        norm = [_rms(a) for a in hh]
        hhat, r = [a for a, _ in norm], [b for _, b in norm]
        dyn = _each(lambda a, b: a * b, dyh, sg)
        do = _each(lambda d, hx, g, s: d * hx * g * s * (1.0 - s), dyh, hhat, hn_h, sg)
        dhn = _each(lambda a, b: jnp.sum(a * b, axis=0, keepdims=True), dyn, hhat)
        dh = _each(_rms_bwd, hhat, r, hn_h, dyn)
        inv = [1.0 / a for a in f["nrm"]]
        dnum = _each(lambda a, b: a * b, dh, inv)
        dnrm = _each(lambda a, b, c: -jnp.sum(a * b, axis=1, keepdims=True) * c, dh, hh, inv)
        dden = _each(lambda d, e, g: jnp.where(jnp.abs(d) > e, g * jnp.sign(d), 0.0), f["den"], f["emt"], dnrm)
        dnb = [bf(a) for a in dnum]
        dnt = [bf(a.T) for a in dnum]
        rmat = _each(lambda a, b, d: mm(a, b) + d, dnb, vth, dden)
        rmat_t = _each(lambda a, b, d: mm(a, b) + as_row(d), vh, dnt, dden)
        ds = _each(lambda w, a: bf(w * a), f["wt"], rmat)
        ds_t = _each(lambda w, a: bf(w * a), wt_t, rmat_t)
        dv = _each(lambda a, b: mm(bf(a), b), p_t, dnb)
        dqs = _each(lambda s, kk, w, d, ct, dd, n: mm(s, kk) + w * (mm(d, ct) + dd * n),
                    ds, kh, winter, dnb, c_t, dden, n_in)
        dk = _each(lambda s, a: mm(s, a) * QK_SCALE, ds_t, qh)
        dinter = _each(lambda qc, dn_, qn, dd, w: (jnp.sum(qc * dn_, axis=1, keepdims=True) + qn * dd) * w,
                       f["qc"], dnum, f["qn"], dden, winter)
        wq = _each(lambda w, a: w * a * QK_SCALE, winter, f["qf"])
        wq_t = _each(lambda w, a: bf(w * a.astype(F32) * QK_SCALE), winter_row, qth)
        dc_loc = _each(mm, wq_t, dnb)
        dct_loc = _each(lambda a, b: mm(a, bf(b)), dnt, wq)
        dn_loc = _each(lambda a, d: jnp.sum(a * d, axis=0, keepdims=True), wq, dden)
        cs_q = _each(lambda a, b: jnp.sum(a * b, axis=1, keepdims=True), p_t, rmat_t)
        db = _each(lambda a, b, di, cs: jnp.sum(a * b, axis=1, keepdims=True) + di - cs, p, rmat, dinter, cs_q)
        ddecay = _each(lambda dc_, c, dn_, n: jnp.sum(jnp.sum(dc_ * c, axis=1, keepdims=True), axis=0, keepdims=True)
                       + jnp.sum(dn_ * n, axis=1, keepdims=True), dcn, c_in, dnn, n_in)
        dkw = _each(lambda a, b, n: mm(a, bf(b)) + n, vh, dcn_t, dnn)
        dk = _each(lambda a, w, b: a + w * b, dk, f["wkf"], dkw)
        da = _each(lambda a, kk, w: jnp.sum(a * kk.astype(F32), axis=1, keepdims=True) * w, dkw, kh, f["wkf"])
        dv = _each(lambda a, kw, dc_: a + mm(bf(kw), bf(dc_)), dv, f["kw"], dcn)
        dgtot = _each(lambda a, dd, d: jnp.sum(a, axis=0, keepdims=True) + dd * d, da, ddecay, decay)
        db = _each(lambda a, b, g: a - b + jnp.where(row == CHUNK - 1, g, 0.0), db, da, dgtot)
        dli = _each(lambda a, b: a + b, cs_q, da)
        dc_new = _each(lambda d, a, b: d * a + b, decay, dcn, dc_loc)
        dct_new = _each(lambda d, a, b: d * a + b, decay, dcn_t, dct_loc)
        dn_new = _each(lambda d, a, b: d * a + b, decay, dnn, dn_loc)
        dx = jnp.zeros((CHUNK, 128), F32)
        for h in HEADS:
            dhn_ref[:, vsl[h]] += dhn[h]
            dc_sc[ksl[h], :] = dc_new[h]
            dct_sc[vsl[h], :] = dct_new[h]
            dn_sc[h:h + 1, :] = dn_new[h]
            dp_ref[:, ksl[h]] = bf(dqs[h] * QK_SCALE)
            dp_ref[:, ML_QK + ML_DK * h:ML_QK + ML_DK * (h + 1)] = bf(dk[h])
            dp_ref[:, 2 * ML_QK + ML_DV * h:2 * ML_QK + ML_DV * (h + 1)] = bf(dv[h])
            dp_ref[:, 2 * ML_QK + ML_V + ML_DV * h:2 * ML_QK + ML_V + ML_DV * (h + 1)] = bf(do[h])
            dx = jnp.where(lane == h, dli[h], dx)
            dx = jnp.where(lane == ML_HEADS + h, db[h], dx)
        dlf = _cumsum_rows(dx, reverse=True)
        dact = jnp.where(lane < ML_HEADS, dx, dlf * _sigmoid(-act))
        dz = dact * (1.0 - th * th)
        dp_ref[:, 2 * ML_QK + 2 * ML_V:] = dz.astype(BF16)
        dbif_ref[...] += jnp.sum(dz, axis=0, keepdims=True)

    rev = lambda i: (nc - 1 - i, 0)
    rev3 = lambda i: (nc - 1 - i, 0, 0)
    return _side_call(
        body, side, name="mlstm_bwd", steps=nc,
        in_specs=_ml_specs(nc, True) + [
            _full((1, 128)), _full((1, ML_V)), pl.BlockSpec((CHUNK, ML_V), rev),
            pl.BlockSpec((1, ML_HEADS * ML_DK, ML_DV), rev3), pl.BlockSpec((1, ML_HEADS, ML_DK), rev3),
            pl.BlockSpec((1, ML_HEADS, 128), rev3), pl.BlockSpec((CHUNK, ML_V), rev)],
        out_specs=[pl.BlockSpec((CHUNK, ML_IN_PAD), rev), _full((1, ML_V)), _full((1, 128))],
        out_shape=[_sds((t, ML_IN_PAD), BF16), _sds((1, ML_V), F32), _sds((1, 128), F32)],
        scratch_shapes=[pltpu.VMEM((ML_HEADS * ML_DK, ML_DV), F32), pltpu.VMEM((ML_HEADS * ML_DV, ML_DK), F32),
                        pltpu.VMEM((ML_HEADS, ML_DK), F32)],
        args=(qkv, qkv, qkv, og, og, bif, hn, hs, cst, nst, mst, dy))


LRU_TM = 256
GELU_K = math.sqrt(2.0 / math.pi)
GELU_C = 0.044715


def _gelu(x):
    th = jnp.tanh(GELU_K * (x + GELU_C * x * x * x))
    return 0.5 * x * (1.0 + th), th


def _neg_expm1(x):
    series = -x * (1.0 + x * (0.5 + x * (1.0 / 6.0 + x * (1.0 / 24.0))))
    return jnp.where(x > -0.05, series, 1.0 - jnp.exp(x))


def _block_diag_dot(a, w_ref, dims):
    parts = [lax.dot_general(a[:, LRU_BLOCK * n:LRU_BLOCK * (n + 1)], w_ref[n], dims, preferred_element_type=F32)
             for n in range(LRU_BLOCKS)]
    return jnp.concatenate(parts, axis=1)


def _lru_gates(u, r, ig, lam):
    ls = _log_sigmoid(lam)
    la = LRU_C * r * ls
    a = jnp.exp(la)
    em = _neg_expm1(2.0 * la)
    mult = jnp.sqrt(em)
    return ls, a, em, mult


def _lru_fwd(proj, cw, cb, wa, ba, wx, bx, lam):
    t = proj.shape[0]
    w = D_MODEL
    tm = min(LRU_TM, t)

    def body(gb_ref, up_ref, cw_ref, cb_ref, wa_ref, ba_ref, wx_ref, bx_ref, lam_ref,
             y_ref, u_ref, r_ref, i_ref, h_ref, tail_sc, hprev_sc):
        @pl.when(pl.program_id(0) == 0)
        def _():
            tail_sc[...] = jnp.zeros_like(tail_sc)
            hprev_sc[...] = jnp.zeros_like(hprev_sc)

        up = up_ref[...]
        ext = jnp.concatenate([tail_sc[...], up], axis=0)
        u = cb_ref[...] + cw_ref[CONV_WIDTH - 1:CONV_WIDTH, :] * up
        for s in range(1, CONV_WIDTH):
            u = u + cw_ref[CONV_WIDTH - 1 - s:CONV_WIDTH - s, :] * pltpu.roll(ext, s, 0)[8:8 + tm]
        tail_sc[...] = up[tm - 8:tm]
        ub = u.astype(BF16)
        r = _sigmoid(_block_diag_dot(ub, wa_ref, (((1,), (0,)), ((), ()))) + ba_ref[...])
        ig = _sigmoid(_block_diag_dot(ub, wx_ref, (((1,), (0,)), ((), ()))) + bx_ref[...])
        _, a, _, mult = _lru_gates(u, r, ig, lam_ref[...])
        acum, hloc = _scan_rows(a, mult * ig * u)
        h = hloc + acum * hprev_sc[0:1, :]
        hprev_sc[0:1, :] = h[tm - 1:tm]
        u_ref[...] = u
        r_ref[...] = r
        i_ref[...] = ig
        h_ref[...] = h
        gel, _ = _gelu(gb_ref[...])
        y_ref[...] = (h * gel).astype(BF16)

    vec = _full((1, w))
    wspec = _full((LRU_BLOCKS, LRU_BLOCK, LRU_BLOCK))
    return pl.pallas_call(
        body, name="lru_fwd", grid=(t // tm,),
        in_specs=[pl.BlockSpec((tm, w), lambda i: (i, 0)), pl.BlockSpec((tm, w), lambda i: (i, 1)),
                  _full((CONV_WIDTH, w)), vec, wspec, vec, wspec, vec, vec],
        out_specs=[_rows(tm, w)] * 5,
        out_shape=[_sds((t, w), BF16)] + [_sds((t, w), F32)] * 4,
        scratch_shapes=[pltpu.VMEM((8, w), F32), pltpu.VMEM((8, w), F32)],
        compiler_params=_params("arbitrary"),
    )(proj, proj, cw, cb, wa, ba, wx, bx, lam)


def _lru_bwd(proj, cw, wa, wx, lam, u, r, ig, h, dy, side):
    t = proj.shape[0]
    w = D_MODEL
    tm = min(LRU_TM, t)
    nt = t // tm

    def body(gb_ref, up_ref, cw_ref, wa_ref, wx_ref, lam_ref, u_ref, r_ref, i_ref, h_ref, hp_ref, dy_ref,
             dp_ref, dcw_ref, dcb_ref, dwa_ref, dba_ref, dwx_ref, dbx_ref, dlam_ref, carry_sc, dutail_sc, dls_sc):
        step = pl.program_id(0)

        @pl.when(step == 0)
        def _():
            carry_sc[...] = jnp.zeros_like(carry_sc)
            dutail_sc[...] = jnp.zeros_like(dutail_sc)
            dls_sc[...] = jnp.zeros_like(dls_sc)
            for ref in (dcw_ref, dcb_ref, dwa_ref, dba_ref, dwx_ref, dbx_ref):
                ref[...] = jnp.zeros_like(ref)

        row = lax.broadcasted_iota(jnp.int32, (tm, w), 0)
        u_t, r_t, i_t, h_t = u_ref[...], r_ref[...], i_ref[...], h_ref[...]
        ls, a, em, mult = _lru_gates(u_t, r_t, i_t, lam_ref[...])
        gb = gb_ref[...]
        gel, th = _gelu(gb)
        dyv = dy_ref[...]
        dgb = dyv * h_t * (0.5 * (1.0 + th) + 0.5 * gb * (1.0 - th * th) * GELU_K * (1.0 + 3.0 * GELU_C * gb * gb))
        dh = jnp.where(row == tm - 1, dyv * gel + carry_sc[0:1, :], dyv * gel)
        a_next = jnp.where(row < tm - 1, pltpu.roll(a, tm - 1, 0), 0.0)
        _, g = _scan_rows(a_next, dh, reverse=True)
        carry_sc[0:1, :] = a[0:1] * g[0:1]
        has_prev = jnp.where(step == nt - 1, 0.0, 1.0)
        h_prev = jnp.where(row >= 1, pltpu.roll(h_t, 1, 0), hp_ref[7:8, :] * has_prev)
        dmult = g * i_t * u_t
        dig = g * mult * u_t
        du = g * mult * i_t
        dla = g * h_prev * a - dmult * (1.0 - em) / mult
        dls_sc[0:1, :] += jnp.sum(dla * r_t, axis=0, keepdims=True) * LRU_C
        dpa = dla * (LRU_C * ls) * r_t * (1.0 - r_t)
        dpx = dig * i_t * (1.0 - i_t)
        dba_ref[...] += jnp.sum(dpa, axis=0, keepdims=True)
        dbx_ref[...] += jnp.sum(dpx, axis=0, keepdims=True)
        ub = u_t.astype(BF16)
        dpab = dpa.astype(BF16)
        dpxb = dpx.astype(BF16)
        for n in range(LRU_BLOCKS):
            cs = slice(LRU_BLOCK * n, LRU_BLOCK * (n + 1))
            dwa_ref[n] += lax.dot_general(ub[:, cs], dpab[:, cs], TN_DIMS, preferred_element_type=F32)
            dwx_ref[n] += lax.dot_general(ub[:, cs], dpxb[:, cs], TN_DIMS, preferred_element_type=F32)
        du = du + _block_diag_dot(dpab, wa_ref, NT_DIMS) + _block_diag_dot(dpxb, wx_ref, NT_DIMS)
        dcb_ref[...] += jnp.sum(du, axis=0, keepdims=True)
        ext = jnp.concatenate([du, dutail_sc[...]], axis=0)
        up = up_ref[...]
        dup = cw_ref[CONV_WIDTH - 1:CONV_WIDTH, :] * du
        dcw_ref[CONV_WIDTH - 1:CONV_WIDTH, :] += jnp.sum(up * du, axis=0, keepdims=True)
        for s in range(1, CONV_WIDTH):
            du_s = pltpu.roll(ext, tm + 8 - s, 0)[0:tm]
            dup = dup + cw_ref[CONV_WIDTH - 1 - s:CONV_WIDTH - s, :] * du_s
            dcw_ref[CONV_WIDTH - 1 - s:CONV_WIDTH - s, :] += jnp.sum(up * du_s, axis=0, keepdims=True)
        dutail_sc[...] = du[0:8]
        dp_ref[:, :w] = dgb.astype(BF16)
        dp_ref[:, w:] = dup.astype(BF16)

        @pl.when(step == nt - 1)
        def _():
            dlam_ref[...] = dls_sc[0:1, :] * _sigmoid(-lam_ref[...])

    rev = lambda col: (lambda i: (nt - 1 - i, col))
    vec = _full((1, w))
    wspec = _full((LRU_BLOCKS, LRU_BLOCK, LRU_BLOCK))
    tile = pl.BlockSpec((tm, w), rev(0))
    prev8 = pl.BlockSpec((8, w), lambda i: (jnp.maximum((nt - 1 - i) * (tm // 8) - 1, 0), 0))
    return _side_call(
        body, side, name="lru_bwd", steps=nt,
        in_specs=[tile, pl.BlockSpec((tm, w), rev(1)), _full((CONV_WIDTH, w)), wspec, wspec, vec,
                  tile, tile, tile, tile, prev8, tile],
        out_specs=[pl.BlockSpec((tm, 2 * w), rev(0)), _full((CONV_WIDTH, w)), vec, wspec, vec, wspec, vec, vec],
        out_shape=[_sds((t, 2 * w), BF16), _sds((CONV_WIDTH, w), F32), _sds((1, w), F32),
                   _sds((LRU_BLOCKS, LRU_BLOCK, LRU_BLOCK), F32), _sds((1, w), F32),
                   _sds((LRU_BLOCKS, LRU_BLOCK, LRU_BLOCK), F32), _sds((1, w), F32), _sds((1, w), F32)],
        scratch_shapes=[pltpu.VMEM((8, w), F32), pltpu.VMEM((8, w), F32), pltpu.VMEM((8, w), F32)],
        args=(proj, proj, cw, wa, wx, lam, u, r, ig, h, h, dy))


LANES = 1024
HALF_FFN = D_FF // N_CHIPS
GROUPS = {
    "ml": (("ml_w_in", 1024, ML_IN // N_CHIPS), ("ml_w_out", 256, 1024)),
    "lru": (("lru_w_in", 1024, 512),
            ("lru_gates", 2 * LRU_BLOCKS * 64, LRU_BLOCK),
            ("lru_w_out", 256, 1024)),
    "ffn0": (("ffn_g0", HALF_FFN, 1024), ("ffn_u0", HALF_FFN, 1024), ("ffn_down0", HALF_FFN, 1024)),
    "ffn1": (("ffn_g1", HALF_FFN, 1024), ("ffn_u1", HALF_FFN, 1024), ("ffn_down1", HALF_FFN, 1024)),
}

SMALL_ROWS = 24
ROW_LOSS, ROW_HEAD_NORM, ROW_B_IF, ROW_LRU = 8, 9, 10, 11


def _row_tile(rows, cols, itemsize, budget=3 << 19):
    best = 16
    for t in range(16, rows + 1, 16):
        if rows % t == 0 and t * cols * itemsize <= budget:
            best = t
    return best


def _chip_peers():
    x, y, c = lax.axis_index("x"), lax.axis_index("y"), lax.axis_index("c")
    return x, y, c, [(1 - x, y), (x, 1 - y), (1 - x, 1 - y)]


HBM_SPEC = pl.BlockSpec(memory_space=pltpu.HBM)


def _remote(src, dst, send_sems, recv_sems, k, to):
    return pltpu.make_async_remote_copy(src_ref=src, dst_ref=dst, send_sem=send_sems.at[k], recv_sem=recv_sems.at[k],
                                        device_id=to, device_id_type=MESH)


GATHER_SEMS = 7


def _gather_copies(kind, specs, part_refs, out_refs, send_sems, recv_sems):
    x, y, c, chips = _chip_peers()
    me = 2 * x + y
    sib = (x, y, 1 - c)
    copy = functools.partial(_remote, send_sems=send_sems, recv_sems=recv_sems)
    out = []
    for p, (_, rows, _) in enumerate(specs):
        mine = pl.ds(c * (rows // 2), rows // 2)
        theirs = pl.ds((1 - c) * (rows // 2), rows // 2)
        base = GATHER_SEMS * p
        for j, (cx, cy) in enumerate(chips):
            land = out_refs[p].at[2 * cx + cy, mine]
            other = out_refs[p].at[2 * cx + cy, theirs]
            if kind == "first":
                out.append(copy(part_refs[p].at[mine], out_refs[p].at[me, mine], k=base + j, to=(cx, cy, c)))
            elif kind == "landed":
                out.append(copy(land, land, k=base + j, to=sib))
            elif kind == "forward":
                out.append(copy(land, land, k=base + 3 + j, to=sib))
            else:
                out.append(copy(other, other, k=base + 3 + j, to=sib))
        if kind in ("first", "arriving"):
            out.append(copy(part_refs[p], out_refs[p].at[me], k=base + 6, to=sib))
    return out


def _gather_start(specs, *refs):
    for cp in _gather_copies("first", specs, *refs):
        cp.start()


def _gather_finish(specs, *refs):
    forwards = _gather_copies("forward", specs, *refs)
    for land, fwd in zip(_gather_copies("landed", specs, *refs), forwards):
        land.wait_recv()
        fwd.start()
    for cp in _gather_copies("arriving", specs, *refs):
        cp.wait_recv()
    for cp in _gather_copies("first", specs, *refs) + forwards:
        cp.wait_send()


def _gather_shapes(specs):
    return [_sds((N_CHIPS, rows, cols), BF16) for _, rows, cols in specs]


def _gather_side(specs, parts):
    return _Side(inputs=list(parts), out_shape=_gather_shapes(specs), n_sems=GATHER_SEMS * len(specs),
                 start=functools.partial(_gather_start, specs), finish=functools.partial(_gather_finish, specs))


def _gather_weights(specs, parts, small):
    n = len(specs)

    def body(*refs):
        part_refs, small_ref = refs[:n], refs[n]
        out_refs, outs_ref = refs[n + 1:2 * n + 1], refs[2 * n + 1]
        send_sems, recv_sems, small_send, small_recv, loc_sem = refs[2 * n + 2:]
        x, y, c, chips = _chip_peers()
        me = 2 * x + y
        local = pltpu.make_async_copy(small_ref, outs_ref.at[me], loc_sem.at[0])
        local.start()
        _gather_start(specs, part_refs, out_refs, send_sems, recv_sems)
        sent = [_remote(small_ref, outs_ref.at[me], small_send, small_recv, j, (cx, cy, c))
                for j, (cx, cy) in enumerate(chips)]
        for cp in sent:
            cp.start()
        _gather_finish(specs, part_refs, out_refs, send_sems, recv_sems)
        for j, (cx, cy) in enumerate(chips):
            _remote(small_ref, outs_ref.at[2 * cx + cy], small_send, small_recv, j, (cx, cy, c)).wait_recv()
        for cp in sent:
            cp.wait_send()
        local.wait()

    dma = pltpu.SemaphoreType.DMA
    return pl.pallas_call(
        body, name="gather_weights",
        in_specs=[HBM_SPEC] * (n + 1), out_specs=[HBM_SPEC] * (n + 1),
        out_shape=_gather_shapes(specs) + [_sds((N_CHIPS,) + small.shape, small.dtype)],
        scratch_shapes=[dma((GATHER_SEMS * n,)), dma((GATHER_SEMS * n,)), dma((3,)), dma((3,)), dma((1,))],
    )(*parts, small)


def _exchange_halves(specs, gparts, name):
    n = len(specs)

    def body(*refs):
        g_refs, out_refs = refs[:n], refs[n:2 * n]
        send_sems, recv_sems = refs[2 * n:]
        x, y, c, _ = _chip_peers()
        sent = []
        for p, (_, rows, _) in enumerate(specs):
            theirs = pl.ds((1 - c) * (rows // 2), rows // 2)
            sent.append(_remote(g_refs[p].at[:, theirs], out_refs[p], send_sems, recv_sems, p, (x, y, 1 - c)))
            sent[-1].start()
        for cp in sent:
            cp.wait()

    return pl.pallas_call(
        body, name=name, in_specs=[HBM_SPEC] * n, out_specs=[HBM_SPEC] * n,
        out_shape=[_sds((N_CHIPS, rows // 2, cols), BF16) for _, rows, cols in specs],
        scratch_shapes=[pltpu.SemaphoreType.DMA((n,)), pltpu.SemaphoreType.DMA((n,))],
    )(*gparts)


def _add_halves(g, recv, pos, name):
    _, half, cols = recv.shape
    tr = _row_tile(half, cols, 2)
    tiles = half // tr

    def body(pos_ref, a_ref, b_ref, o_ref):
        o_ref[...] = (a_ref[...].astype(F32) + b_ref[...].astype(F32)).astype(BF16)

    spec = pl.BlockSpec((1, tr, cols), lambda k, i, pos_ref: (k, i, 0))
    return pl.pallas_call(
        body, name=name,
        grid_spec=pltpu.PrefetchScalarGridSpec(
            num_scalar_prefetch=1, grid=(N_CHIPS, tiles),
            in_specs=[pl.BlockSpec((1, tr, cols), lambda k, i, pos_ref: (k, pos_ref[0] * tiles + i, 0)), spec],
            out_specs=spec),
        out_shape=_sds((N_CHIPS, half, cols), BF16),
        compiler_params=_params("parallel", "parallel"),
    )(pos, g, recv)


def _scatter_copies(n, s1_refs, recv_refs, send_sems, recv_sems):
    x, y, c, chips = _chip_peers()
    return [_remote(s1_refs[p].at[2 * cx + cy], recv_refs[p].at[j], send_sems, recv_sems, 3 * p + j, (cx, cy, c))
            for p in range(n) for j, (cx, cy) in enumerate(chips)]


def _scatter_start(n, s1_refs, recv_refs, send_sems, recv_sems):
    for cp in _scatter_copies(n, s1_refs, recv_refs, send_sems, recv_sems):
        cp.start()


def _scatter_finish(n, s1_refs, recv_refs, send_sems, recv_sems):
    for cp in _scatter_copies(n, s1_refs, recv_refs, send_sems, recv_sems):
        cp.wait()


def _scatter_shapes(s1):
    return [_sds((3,) + a.shape[1:], a.dtype) for a in s1]


def _scatter_side(s1):
    n = len(s1)
    return _Side(inputs=list(s1), out_shape=_scatter_shapes(s1), n_sems=3 * n,
                 start=functools.partial(_scatter_start, n), finish=functools.partial(_scatter_finish, n))


def _scatter_to_chips(s1, small):
    n = len(s1)
    flips = [(fx, fy, fc) for fx in (0, 1) for fy in (0, 1) for fc in (0, 1)][1:]

    def body(*refs):
        s1_refs, small_ref = refs[:n], refs[n]
        recv_refs, all_ref = refs[n + 1:2 * n + 1], refs[2 * n + 1]
        send_sems, recv_sems, small_send, small_recv, loc_sem = refs[2 * n + 2:]
        x, y, c, _ = _chip_peers()
        my_slot = all_ref.at[4 * x + 2 * y + c]
        local = pltpu.make_async_copy(small_ref, my_slot, loc_sem.at[0])
        local.start()
        _scatter_start(n, s1_refs, recv_refs, send_sems, recv_sems)
        peers = [(1 - x if fx else x, 1 - y if fy else y, 1 - c if fc else c) for fx, fy, fc in flips]
        sent = [_remote(small_ref, my_slot, small_send, small_recv, i, p) for i, p in enumerate(peers)]
        for cp in sent:
            cp.start()
        _scatter_finish(n, s1_refs, recv_refs, send_sems, recv_sems)
        for i, (px, py, pc) in enumerate(peers):
            _remote(small_ref, all_ref.at[4 * px + 2 * py + pc], small_send, small_recv, i, peers[i]).wait_recv()
        for cp in sent:
            cp.wait_send()
        local.wait()

    dma = pltpu.SemaphoreType.DMA
    return pl.pallas_call(
        body, name="scatter_to_chips", in_specs=[HBM_SPEC] * (n + 1), out_specs=[HBM_SPEC] * (n + 1),
        out_shape=_scatter_shapes(s1) + [_sds((8,) + small.shape, small.dtype)],
        scratch_shapes=[dma((3 * n,)), dma((3 * n,)), dma((7,)), dma((7,)), dma((1,))],
    )(*s1, small)


def _sum_chips(s1, recv, pos, name):
    _, half, cols = recv.shape
    tr = _row_tile(half, cols, 4)
    tiles = half // tr

    def body(pos_ref, a_ref, b_ref, o_ref):
        acc = a_ref[0].astype(F32)
        for j in range(3):
            acc = acc + b_ref[j].astype(F32)
        o_ref[...] = acc

    return pl.pallas_call(
        body, name=name,
        grid_spec=pltpu.PrefetchScalarGridSpec(
            num_scalar_prefetch=1, grid=(tiles,),
            in_specs=[pl.BlockSpec((1, tr, cols), lambda i, pos_ref: (pos_ref[1], i, 0)),
                      pl.BlockSpec((3, tr, cols), lambda i, pos_ref: (0, i, 0))],
            out_specs=pl.BlockSpec((tr, cols), lambda i, pos_ref: (pos_ref[0] * tiles + i, 0))),
        out_shape=_sds((2 * half, cols), F32),
        compiler_params=_params("parallel"),
    )(pos, s1, recv)


def _sum_small(small_all):
    def body(a_ref, o_ref):
        acc = a_ref[0]
        for d in range(1, 8):
            acc = acc + a_ref[d]
        o_ref[...] = acc

    return pl.pallas_call(body, name="sum_small", out_shape=_sds(small_all.shape[1:], F32))(small_all)


def _join_halves(specs, s2):
    n = len(specs)

    def body(*refs):
        buf_refs = refs[n:2 * n]
        send_sems, recv_sems = refs[2 * n:]
        x, y, c, _ = _chip_peers()
        sent = []
        for p, (_, rows, _) in enumerate(specs):
            mine = buf_refs[p].at[pl.ds(c * (rows // 2), rows // 2)]
            sent.append(_remote(mine, mine, send_sems, recv_sems, p, (x, y, 1 - c)))
            sent[-1].start()
        for p, (_, rows, _) in enumerate(specs):
            theirs = buf_refs[p].at[pl.ds((1 - c) * (rows // 2), rows // 2)]
            _remote(theirs, theirs, send_sems, recv_sems, p, (x, y, 1 - c)).wait_recv()
        for cp in sent:
            cp.wait_send()

    return pl.pallas_call(
        body, name="join_halves", in_specs=[HBM_SPEC] * n, out_specs=[HBM_SPEC] * n,
        out_shape=[_sds(a.shape, a.dtype) for a in s2],
        input_output_aliases={p: p for p in range(n)},
        scratch_shapes=[pltpu.SemaphoreType.DMA((n,)), pltpu.SemaphoreType.DMA((n,))],
    )(*s2)


def _adamw(w, g, m, v, name):
    rows, cols = w.shape
    tm = rows
    for cand in (512, 256, 128, 64, 32, 16, 8):
        if rows % cand == 0 and rows > cand:
            tm = cand
            break

    def body(w_ref, g_ref, m_ref, v_ref, d_ref, nm_ref, nv_ref):
        gv = g_ref[...]
        nm = ADAM_B1 * m_ref[...] + (1.0 - ADAM_B1) * gv
        nv = ADAM_B2 * v_ref[...] + (1.0 - ADAM_B2) * (gv * gv)
        m_hat = nm / (1.0 - ADAM_B1 ** ADAM_STEP)
        v_hat = nv / (1.0 - ADAM_B2 ** ADAM_STEP)
        d_ref[...] = -ADAM_LR * (m_hat / (jnp.sqrt(v_hat) + ADAM_EPS) + ADAM_WD * w_ref[...])
        nm_ref[...] = nm
        nv_ref[...] = nv

    spec = _rows(tm, cols)
    return pl.pallas_call(
        body, name=name, grid=(rows // tm,), in_specs=[spec] * 4, out_specs=[spec] * 3,
        out_shape=[_sds((rows, cols), F32)] * 3, compiler_params=_params("parallel"),
    )(w, g, m, v)


WEIGHTS = ("ml_w_in", "ml_b_if", "ml_head_norm", "ml_w_out", "lru_w_in", "lru_conv_w", "lru_conv_b", "lru_w_gate_a",
           "lru_b_gate_a", "lru_w_gate_x", "lru_b_gate_x", "lru_lambda", "lru_w_out", "norm_pre_mix", "norm_post_mix",
           "norm_pre_ffn", "norm_post_ffn", "ffn_w_gate", "ffn_w_up", "ffn_w_down")
LRU_VECTORS = ("lru_conv_b", "lru_b_gate_a", "lru_b_gate_x", "lru_lambda")
NORMS = ("norm_pre_mix", "norm_post_mix", "norm_pre_ffn", "norm_post_ffn")


def _by_cols(a):
    return jnp.concatenate([a[k] for k in range(N_CHIPS)], axis=-1)


def _by_chip(a, width):
    return jnp.stack([a[..., k * width:(k + 1) * width] for k in range(N_CHIPS)])


def _weight_shards(w):
    bf = lambda a: a.astype(BF16)
    shards = dict(
        ml_w_in=bf(w["ml_w_in"][0]), ml_w_out=bf(w["ml_w_out"][0]), lru_w_in=bf(w["lru_w_in"][0]),
        lru_gates=bf(jnp.concatenate([w["lru_w_gate_a"][0], w["lru_w_gate_x"][0]], axis=0)).reshape(-1, LRU_BLOCK),
        lru_w_out=bf(w["lru_w_out"][0]))
    gate_t, up_t = bf(jnp.swapaxes(w["ffn_w_gate"], 1, 2)), bf(jnp.swapaxes(w["ffn_w_up"], 1, 2))
    for layer in range(2):
        shards[f"ffn_g{layer}"] = gate_t[layer]
        shards[f"ffn_u{layer}"] = up_t[layer]
        shards[f"ffn_down{layer}"] = bf(w["ffn_w_down"][layer])
    return shards


def _ffn_weights(parts):
    return [a.reshape(D_FF, 1024) for a in parts]


def _ffn_grad_parts(dw_gu_t, dw_down):
    return [dw_gu_t[:D_FF].reshape(N_CHIPS, HALF_FFN, 1024), dw_gu_t[D_FF:].reshape(N_CHIPS, HALF_FFN, 1024),
            dw_down.reshape(N_CHIPS, HALF_FFN, 1024)]


def _half_reduce(group, gparts, pos):
    specs = GROUPS[group]
    recv = _exchange_halves(specs, gparts, "exchange_halves_" + group)
    return [_add_halves(g, r, pos, "add_halves_" + n) for (n, _, _), g, r in zip(specs, gparts, recv)]


def _sum_group(group, s1, recv, pos):
    return [_sum_chips(a, r, pos, "sum_chips_" + n) for (n, _, _), a, r in zip(GROUPS[group], s1, recv)]


def _update(w, grads, m, v):
    delta, new_m, new_v = {}, {}, {}
    for n in WEIGHTS:
        flip = (lambda a: jnp.swapaxes(a, 1, 2)) if n in ("ffn_w_gate", "ffn_w_up") else (lambda a: a)
        shape = flip(w[n]).shape
        as2d = lambda a: flip(a).reshape(-1, shape[-1])
        back = lambda a: flip(a.reshape(shape))
        d, nm, nv = _adamw(as2d(w[n]), as2d(grads[n]), as2d(m[n]), as2d(v[n]), "adamw_" + n)
        delta[n], new_m[n], new_v[n] = back(d), back(nm), back(nv)
    return delta, new_m, new_v


def kernel(x, ml_w_in, ml_b_if, ml_head_norm, ml_w_out, lru_w_in, lru_conv_w, lru_conv_b, lru_w_gate_a, lru_b_gate_a, lru_w_gate_x, lru_b_gate_x, lru_lambda, lru_w_out, norm_pre_mix, norm_post_mix, norm_pre_ffn, norm_post_ffn, ffn_w_gate, ffn_w_up, ffn_w_down, loss_target, m_ml_w_in, m_ml_b_if, m_ml_head_norm, m_ml_w_out, m_lru_w_in, m_lru_conv_w, m_lru_conv_b, m_lru_w_gate_a, m_lru_b_gate_a, m_lru_w_gate_x, m_lru_b_gate_x, m_lru_lambda, m_lru_w_out, m_norm_pre_mix, m_norm_post_mix, m_norm_pre_ffn, m_norm_post_ffn, m_ffn_w_gate, m_ffn_w_up, m_ffn_w_down, v_ml_w_in, v_ml_b_if, v_ml_head_norm, v_ml_w_out, v_lru_w_in, v_lru_conv_w, v_lru_conv_b, v_lru_w_gate_a, v_lru_b_gate_a, v_lru_w_gate_x, v_lru_b_gate_x, v_lru_lambda, v_lru_w_out, v_norm_pre_mix, v_norm_post_mix, v_norm_pre_ffn, v_norm_post_ffn, v_ffn_w_gate, v_ffn_w_up, v_ffn_w_down):
    args = locals()
    w = {n: args[n] for n in WEIGHTS}
    m = {n: args["m_" + n] for n in WEIGHTS}
    v = {n: args["v_" + n] for n in WEIGHTS}
    xs, target = x[0], loss_target[0]
    mx, my, mc, _ = _chip_peers()
    chip = 2 * mx + my
    pos = jnp.stack([mc, chip])
    row = lambda a, i: a[i:i + 1]
    npm, nqm, npf, nqf = (w[n] for n in NORMS)
    shards = _weight_shards(w)
    of = lambda group: [shards[n] for n, _, _ in GROUPS[group]]
    bif = jnp.pad(w["ml_b_if"], ((0, 0), (0, 128 - 2 * ML_HEADS)))
    hn = w["ml_head_norm"]

    small = jnp.concatenate([w["lru_conv_w"][0]] + [w[n] for n in LRU_VECTORS], axis=0)
    ml_in_parts, ml_out_parts, smalls = _gather_weights(GROUPS["ml"], of("ml"), small)
    ml_w_in = jnp.pad(_by_cols(ml_in_parts), ((0, 0), (0, ML_IN_PAD - ML_IN)))
    ml_w_out = ml_out_parts.reshape(1024, 1024)
    vec = _by_cols(smalls)
    z0, qkv, og = _norm_matmul(xs, row(npm, 0), ml_w_in, 2 * ML_QK + ML_V, "ml_in")
    (y0, hs, cst, nst, mst), got = _mlstm_fwd(
        qkv, og, bif, hn, _gather_side(GROUPS["lru"] + GROUPS["ffn0"], of("lru") + of("ffn0")))
    lru_w_in = _by_cols(got[0])
    gates = got[1].reshape(N_CHIPS, 2, LRU_BLOCKS, 64, LRU_BLOCK).transpose(1, 2, 0, 3, 4)
    gates = gates.reshape(2, LRU_BLOCKS, LRU_BLOCK, LRU_BLOCK)
    lru_w_out = got[2].reshape(1024, 1024)
    wg0, wu0, w_down0 = _ffn_weights(got[3:6])
    ymix0, h1 = _matmul_postnorm(y0, ml_w_out, row(nqm, 0), xs, "ml_out")
    (zf0, gu0, yf0, h2), got = _ffn_forward(h1, row(npf, 0), wg0, wu0, w_down0, row(nqf, 0), "ffn0_fwd",
                                            _gather_side(GROUPS["ffn1"], of("ffn1")))
    wg1, wu1, w_down1 = _ffn_weights(got)
    z1, proj1 = _norm_matmul(h2, row(npm, 1), lru_w_in, 0, "lru_in")
    y1, u, r, ig, hl = _lru_fwd(proj1, vec[0:4], vec[4:5], gates[0], vec[5:6], gates[1], vec[6:7], vec[7:8])
    ymix1, h3 = _matmul_postnorm(y1, lru_w_out, row(nqm, 1), h2, "lru_out")
    (zf1, gu1, yf1, h4), _ = _ffn_forward(h3, row(npf, 1), wg1, wu1, w_down1, row(nqf, 1), "ffn1_fwd", None)
    dh4, loss_part = _loss_head(h4, target, "loss_head")

    (dyf1, dgu1, act1, dqf1), _ = _bwd_out(dh4, yf1, row(nqf, 1), w_down1, gu1, "ffn1_bwd_down", None)
    dw_down1 = _matmul_tn(act1, dyf1, True, 256, "ffn1_dw_down")
    dw_gu1 = _matmul_tn(dgu1, zf1, True, 512, "ffn1_dw_up")
    dh3, dpf1 = _bwd_in(dgu1, [wg1, wu1], True, h3, row(npf, 1), dh4, "ffn1_bwd_up")
    s1_ffn1 = _half_reduce("ffn1", _ffn_grad_parts(dw_gu1, dw_down1), pos)

    (dymix1, dy1, dqm1), _ = _bwd_out(dh3, ymix1, row(nqm, 1), lru_w_out, None, "lru_bwd_out", None)
    dw_lru_out = _matmul_tn(y1, dymix1, False, 256, "lru_dw_out")
    (dproj1, dcw, dcb, dwa, dba, dwx, dbx, dlam), recv_ffn1 = _lru_bwd(
        proj1, vec[0:4], gates[0], gates[1], vec[7:8], u, r, ig, hl, dy1, _scatter_side(s1_ffn1))
    dw_lru_in = _matmul_tn(z1, dproj1, False, 512, "lru_dw_in")
    dh2, dpm1 = _bwd_in(dproj1, [lru_w_in], False, h2, row(npm, 1), dh3, "lru_bwd_in")
    dgates = jnp.stack([dwa, dwx]).astype(BF16).reshape(2, LRU_BLOCKS, N_CHIPS, 64, LRU_BLOCK)
    dgates = dgates.transpose(2, 0, 1, 3, 4).reshape(N_CHIPS, -1, LRU_BLOCK)
    s1_lru = _half_reduce("lru", [_by_chip(dw_lru_in, 512), dgates, dw_lru_out.reshape(N_CHIPS, 256, 1024)], pos)

    (dyf0, dgu0, act0, dqf0), recv_lru = _bwd_out(dh2, yf0, row(nqf, 0), w_down0, gu0, "ffn0_bwd_down",
                                                  _scatter_side(s1_lru))
    dw_down0 = _matmul_tn(act0, dyf0, True, 256, "ffn0_dw_down")
    dw_gu0 = _matmul_tn(dgu0, zf0, True, 512, "ffn0_dw_up")
    dh1, dpf0 = _bwd_in(dgu0, [wg0, wu0], True, h1, row(npf, 0), dh2, "ffn0_bwd_up")
    s1_ffn0 = _half_reduce("ffn0", _ffn_grad_parts(dw_gu0, dw_down0), pos)

    (dymix0, dy0, dqm0), _ = _bwd_out(dh1, ymix0, row(nqm, 0), ml_w_out, None, "ml_bwd_out", None)
    dw_ml_out = _matmul_tn(y0, dymix0, False, 256, "ml_dw_out")
    (dproj0, dhn, dbif), recv_ffn0 = _mlstm_bwd(qkv, og, bif, hn, hs, cst, nst, mst, dy0, _scatter_side(s1_ffn0))
    dw_ml_in = _matmul_tn(z0, dproj0, False, 640, "ml_dw_in")
    dx, dpm0 = _bwd_in(dproj0, [ml_w_in], False, xs, row(npm, 0), dh1, "ml_bwd_in")
    s1_ml = _half_reduce(
        "ml", [_by_chip(dw_ml_in[:, :ML_IN], ML_IN // N_CHIPS), dw_ml_out.reshape(N_CHIPS, 256, 1024)], pos)

    pad_lanes = lambda a: jnp.pad(a, ((0, 0), (0, LANES - a.shape[1])))
    small = jnp.concatenate(
        [jnp.concatenate([dpm0, dpm1]), jnp.concatenate([dqm0, dqm1]), jnp.concatenate([dpf0, dpf1]),
         jnp.concatenate([dqf0, dqf1]), pad_lanes(loss_part), dhn, pad_lanes(dbif), dcw, dcb, dba, dbx, dlam,
         jnp.zeros((SMALL_ROWS - 19, LANES), F32)], axis=0)
    *recv_ml, small_all = _scatter_to_chips(s1_ml, small)
    order = ("ml", "lru", "ffn0", "ffn1")
    s2 = (_sum_group("ml", s1_ml, recv_ml, pos) + _sum_group("lru", s1_lru, recv_lru, pos)
          + _sum_group("ffn0", s1_ffn0, recv_ffn0, pos) + _sum_group("ffn1", s1_ffn1, recv_ffn1, pos))
    specs = sum((GROUPS[g] for g in order), ())
    red = dict(zip([n for n, _, _ in specs], _join_halves(specs, s2)))
    vsum = _sum_small(small_all)

    dgates = red["lru_gates"].reshape(2, LRU_BLOCKS, 64, LRU_BLOCK)
    cols = lambda a: lax.dynamic_slice_in_dim(a, chip * 256, 256, axis=1)
    grads = dict(
        ml_w_in=red["ml_w_in"], ml_w_out=red["ml_w_out"], lru_w_in=red["lru_w_in"], lru_w_gate_a=dgates[0],
        lru_w_gate_x=dgates[1], lru_w_out=red["lru_w_out"],
        ffn_w_gate=jnp.swapaxes(jnp.stack([red["ffn_g0"], red["ffn_g1"]]), 1, 2),
        ffn_w_up=jnp.swapaxes(jnp.stack([red["ffn_u0"], red["ffn_u1"]]), 1, 2),
        ffn_w_down=jnp.stack([red["ffn_down0"], red["ffn_down1"]]),
        ml_head_norm=vsum[ROW_HEAD_NORM:ROW_HEAD_NORM + 1], ml_b_if=vsum[ROW_B_IF:ROW_B_IF + 1, :2 * ML_HEADS],
        lru_conv_w=cols(vsum[ROW_LRU:ROW_LRU + 4]))
    for i, n in enumerate(NORMS):
        grads[n] = vsum[2 * i:2 * i + 2]
    for i, n in enumerate(LRU_VECTORS):
        grads[n] = cols(vsum[ROW_LRU + 4 + i:ROW_LRU + 5 + i])
    loss = vsum[ROW_LOSS, 0]
    grads = {n: grads[n].reshape(w[n].shape) for n in WEIGHTS}
    delta, new_m, new_v = _update(w, grads, m, v)
    return (loss, dx[None], *[grads[n] for n in WEIGHTS], *[delta[n] for n in WEIGHTS],
            *[new_m[n] for n in WEIGHTS], *[new_v[n] for n in WEIGHTS])
```

```python
import functools
import math
from typing import Callable, NamedTuple

import jax
import jax.numpy as jnp
from jax import lax
from jax.experimental import pallas as pl
from jax.experimental.pallas import tpu as pltpu

F32 = jnp.float32
BF16 = jnp.bfloat16
MESH = pl.DeviceIdType.MESH

D_MODEL = 1024
D_FF = 2816
ML_HEADS = 8
ML_DK = 64
ML_DV = 128
ML_QK = ML_HEADS * ML_DK
ML_V = ML_HEADS * ML_DV
ML_IN = 2 * ML_QK + 2 * ML_V + 2 * ML_HEADS
ML_IN_PAD = 3200
CHUNK = 64
GATE_CAP = 15.0
ML_M_INIT = -1e30
NEG_BIG = -1e30
LRU_BLOCKS = 4
LRU_BLOCK = 256
CONV_WIDTH = 4
LRU_C = 8.0
EPS = 1e-6
QK_SCALE = ML_DK ** -0.5

ADAM_LR = 0.001
ADAM_B1 = 0.9
ADAM_B2 = 0.999
ADAM_EPS = 1e-08
ADAM_WD = 0.01
ADAM_STEP = 10

N_CHIPS = 4
V7X_VMEM_LIMIT = 56 * 1024 * 1024

NT_DIMS = (((1,), (1,)), ((), ()))
TN_DIMS = (((0,), (0,)), ((), ()))


def _params(*semantics):
    return pltpu.CompilerParams(dimension_semantics=semantics, vmem_limit_bytes=V7X_VMEM_LIMIT)


def _sds(shape, dtype):
    return jax.ShapeDtypeStruct(shape, dtype)


def _full(shape):
    return pl.BlockSpec(shape, lambda *_: (0,) * len(shape))


def _rows(tm, n):
    return pl.BlockSpec((tm, n), lambda i: (i, 0))


def _sigmoid(x):
    return 1.0 / (1.0 + jnp.exp(-x))


def _log_sigmoid(x):
    return jnp.minimum(x, 0.0) - jnp.log1p(jnp.exp(-jnp.abs(x)))


def _rms(x):
    r = lax.rsqrt(jnp.mean(x * x, axis=-1, keepdims=True) + EPS)
    return x * r, r


def _rms_bwd(xhat, r, g, dy):
    dxh = dy * g
    return r * (dxh - xhat * jnp.mean(dxh * xhat, axis=-1, keepdims=True))


def _scan_rows(a, b, reverse=False):
    n = a.shape[0]
    row = lax.broadcasted_iota(jnp.int32, a.shape, 0)
    s = 1
    while s < n:
        if reverse:
            keep = row < n - s
            shift = n - s
        else:
            keep = row >= s
            shift = s
        b = b + a * jnp.where(keep, pltpu.roll(b, shift, 0), 0.0)
        a = a * jnp.where(keep, pltpu.roll(a, shift, 0), 1.0)
        s *= 2
    return a, b


def _cumsum_rows(x, reverse=False):
    n = x.shape[0]
    row = lax.broadcasted_iota(jnp.int32, x.shape, 0)
    s = 1
    while s < n:
        if reverse:
            x = x + jnp.where(row < n - s, pltpu.roll(x, n - s, 0), 0.0)
        else:
            x = x + jnp.where(row >= s, pltpu.roll(x, s, 0), 0.0)
        s *= 2
    return x


def _norm_matmul(h, g, w, n_bf16, name):
    t, d = h.shape
    n = w.shape[1]
    tm = min(512, t)

    def body(h_ref, g_ref, w_ref, z_ref, *o_refs):
        xhat, _ = _rms(h_ref[...])
        z = (xhat * g_ref[...]).astype(BF16)
        z_ref[...] = z
        out = jnp.dot(z, w_ref[...], preferred_element_type=F32)
        if n_bf16:
            o_refs[0][...] = out[:, :n_bf16].astype(BF16)
            o_refs[1][...] = out[:, n_bf16:]
        else:
            o_refs[0][...] = out

    if n_bf16:
        out_specs = [_rows(tm, d), _rows(tm, n_bf16), _rows(tm, n - n_bf16)]
        out_shape = [_sds((t, d), BF16), _sds((t, n_bf16), BF16), _sds((t, n - n_bf16), F32)]
    else:
        out_specs = [_rows(tm, d), _rows(tm, n)]
        out_shape = [_sds((t, d), BF16), _sds((t, n), F32)]
    return pl.pallas_call(
        body, name=name, grid=(t // tm,),
        in_specs=[_rows(tm, d), _full((1, d)), _full((d, n))],
        out_specs=out_specs, out_shape=out_shape, compiler_params=_params("parallel"),
    )(h, g, w)


def _matmul_postnorm(a, w, g, res, name):
    t = res.shape[0]
    k, d = w.shape
    tm = 256

    def body(a_ref, w_ref, g_ref, res_ref, y_ref, o_ref):
        y = jnp.dot(a_ref[...], w_ref[...], preferred_element_type=F32)
        y_ref[...] = y
        yhat, _ = _rms(y)
        o_ref[...] = res_ref[...] + yhat * g_ref[...]

    return pl.pallas_call(
        body, name=name, grid=(t // tm,),
        in_specs=[_rows(tm, k), _full((k, d)), _full((1, d)), _rows(tm, d)],
        out_specs=[_rows(tm, d), _rows(tm, d)],
        out_shape=[_sds((t, d), F32), _sds((t, d), F32)],
        compiler_params=_params("parallel"),
    )(a, w, g, res)


class _Side(NamedTuple):
    inputs: list
    out_shape: list
    n_sems: int
    start: Callable
    finish: Callable


def _side_call(body, side, *, name, steps, in_specs, out_specs, out_shape, scratch_shapes, args):
    n_in, n_out, n_scr = len(in_specs), len(out_specs), len(scratch_shapes)
    if side is None:
        outs = pl.pallas_call(
            body, name=name, grid=(steps,), in_specs=in_specs, out_specs=out_specs, out_shape=out_shape,
            scratch_shapes=scratch_shapes, compiler_params=_params("arbitrary"))(*args)
        return list(outs), []
    s_in, s_out = len(side.inputs), len(side.out_shape)

    def carrying(*refs):
        ins, side_ins = refs[:n_in], refs[n_in:n_in + s_in]
        outs = refs[n_in + s_in:n_in + s_in + n_out]
        side_outs = refs[n_in + s_in + n_out:n_in + s_in + n_out + s_out]
        scratch = refs[n_in + s_in + n_out + s_out:]
        own, sems = scratch[:n_scr], scratch[n_scr:]

        @pl.when(pl.program_id(0) == 0)
        def _():
            side.start(side_ins, side_outs, *sems)

        body(*ins, *outs, *own)

        @pl.when(pl.program_id(0) == steps - 1)
        def _():
            side.finish(side_ins, side_outs, *sems)

    outs = pl.pallas_call(
        carrying, name=name, grid=(steps,), in_specs=list(in_specs) + [HBM_SPEC] * s_in,
        out_specs=list(out_specs) + [HBM_SPEC] * s_out, out_shape=list(out_shape) + list(side.out_shape),
        scratch_shapes=list(scratch_shapes) + [pltpu.SemaphoreType.DMA((side.n_sems,))] * 2,
        compiler_params=_params("arbitrary"))(*args, *side.inputs)
    return list(outs[:n_out]), list(outs[n_out:])


def _ffn_forward(h, g_pre, wg_t, wu_t, w_down, g_post, name, side):
    t, d = h.shape
    k = w_down.shape[0]
    tm = 256

    def body(h_ref, gpre_ref, wg_ref, wu_ref, wd_ref, gpost_ref, z_ref, gu_ref, y_ref, o_ref):
        hv = h_ref[...]
        xhat, _ = _rms(hv)
        z = (xhat * gpre_ref[...]).astype(BF16)
        z_ref[...] = z
        gate = lax.dot_general(z, wg_ref[...], NT_DIMS, preferred_element_type=F32).astype(BF16)
        up = lax.dot_general(z, wu_ref[...], NT_DIMS, preferred_element_type=F32).astype(BF16)
        gu_ref[:, :k] = gate
        gu_ref[:, k:] = up
        gate = gate.astype(F32)
        act = (gate * _sigmoid(gate) * up.astype(F32)).astype(BF16)
        y = jnp.dot(act, wd_ref[...], preferred_element_type=F32)
        y_ref[...] = y
        yhat, _ = _rms(y)
        o_ref[...] = hv + yhat * gpost_ref[...]

    return _side_call(
        body, side, name=name, steps=t // tm,
        in_specs=[_rows(tm, d), _full((1, d)), _full((k, d)), _full((k, d)), _full((k, d)), _full((1, d))],
        out_specs=[_rows(tm, d), _rows(tm, 2 * k), _rows(tm, d), _rows(tm, d)],
        out_shape=[_sds((t, d), BF16), _sds((t, 2 * k), BF16), _sds((t, d), F32), _sds((t, d), F32)],
        scratch_shapes=[], args=(h, g_pre, wg_t, wu_t, w_down, g_post))


def _loss_head(h, target, name):
    t, d = h.shape
    tm = min(512, t)

    def body(h_ref, t_ref, dh_ref, l_ref):
        @pl.when(pl.program_id(0) == 0)
        def _():
            l_ref[...] = jnp.zeros_like(l_ref)

        err = h_ref[...] - t_ref[...]
        dh_ref[...] = err * (1.0 / d)
        part = jnp.sum(jnp.sum(err * err, axis=1, keepdims=True), axis=0, keepdims=True) * (0.5 / d)
        l_ref[...] += jnp.broadcast_to(part, l_ref.shape)

    return pl.pallas_call(
        body, name=name, grid=(t // tm,),
        in_specs=[_rows(tm, d), _rows(tm, d)],
        out_specs=[_rows(tm, d), _full((1, 128))],
        out_shape=[_sds((t, d), F32), _sds((1, 128), F32)],
        compiler_params=_params("arbitrary"),
    )(h, target)


def _bwd_out(dout, y, g, w, name, side):
    t, d = dout.shape
    k = w.shape[0]
    tm = 256

    def body(dout_ref, y_ref, g_ref, w_ref, dy_ref, da_ref, dg_ref):
        @pl.when(pl.program_id(0) == 0)
        def _():
            dg_ref[...] = jnp.zeros_like(dg_ref)

        do = dout_ref[...]
        yhat, r = _rms(y_ref[...])
        dg_ref[...] += jnp.sum(do * yhat, axis=0, keepdims=True)
        dy = _rms_bwd(yhat, r, g_ref[...], do).astype(BF16)
        dy_ref[...] = dy
        da_ref[...] = lax.dot_general(dy, w_ref[...], NT_DIMS, preferred_element_type=F32)

    return _side_call(
        body, side, name=name, steps=t // tm,
        in_specs=[_rows(tm, d), _rows(tm, d), _full((1, d)), _full((k, d))],
        out_specs=[_rows(tm, d), _rows(tm, k), _full((1, d))],
        out_shape=[_sds((t, d), BF16), _sds((t, k), F32), _sds((1, d), F32)],
        scratch_shapes=[], args=(dout, y, g, w))


FFN_CHUNK = 1408


def _ffn_backward(dout, y, g_post, w_down, gu, wg_t, wu_t, h, g_pre, name, side):
    t, d = dout.shape
    k = w_down.shape[0]
    tm = 256

    def body(dout_ref, y_ref, gpost_ref, wd_ref, gu_ref, wg_ref, wu_ref, h_ref, gpre_ref,
             dy_ref, dgu_ref, act_ref, dh_ref, dgpost_ref, dgpre_ref):
        @pl.when(pl.program_id(0) == 0)
        def _():
            dgpost_ref[...] = jnp.zeros_like(dgpost_ref)
            dgpre_ref[...] = jnp.zeros_like(dgpre_ref)

        do = dout_ref[...]
        yhat, r = _rms(y_ref[...])
        dgpost_ref[...] += jnp.sum(do * yhat, axis=0, keepdims=True)
        dy = _rms_bwd(yhat, r, gpost_ref[...], do).astype(BF16)
        dy_ref[...] = dy
        dz = jnp.zeros((tm, d), F32)
        for c0 in range(0, k, FFN_CHUNK):
            cs = slice(c0, c0 + FFN_CHUNK)
            us = slice(k + c0, k + c0 + FFN_CHUNK)
            da = lax.dot_general(dy, wd_ref[cs, :], NT_DIMS, preferred_element_type=F32)
            gate = gu_ref[:, cs].astype(F32)
            up = gu_ref[:, us].astype(F32)
            sg = _sigmoid(gate)
            silu = gate * sg
            act_ref[:, cs] = (silu * up).astype(BF16)
            dgate = (da * up * (sg * (1.0 + gate * (1.0 - sg)))).astype(BF16)
            dup = (da * silu).astype(BF16)
            dgu_ref[:, cs] = dgate
            dgu_ref[:, us] = dup
            dz = (dz + jnp.dot(dgate, wg_ref[cs, :], preferred_element_type=F32)
                  + jnp.dot(dup, wu_ref[cs, :], preferred_element_type=F32))
        hhat, r2 = _rms(h_ref[...])
        dgpre_ref[...] += jnp.sum(dz * hhat, axis=0, keepdims=True)
        dh_ref[...] = do + _rms_bwd(hhat, r2, gpre_ref[...], dz)

    vec = _full((1, d))
    wspec = _full((k, d))
    return _side_call(
        body, side, name=name, steps=t // tm,
        in_specs=[_rows(tm, d), _rows(tm, d), vec, wspec, _rows(tm, 2 * k), wspec, wspec, _rows(tm, d), vec],
        out_specs=[_rows(tm, d), _rows(tm, 2 * k), _rows(tm, k), _rows(tm, d), vec, vec],
        out_shape=[_sds((t, d), BF16), _sds((t, 2 * k), BF16), _sds((t, k), BF16), _sds((t, d), F32),
                   _sds((1, d), F32), _sds((1, d), F32)],
        scratch_shapes=[], args=(dout, y, g_post, w_down, gu, wg_t, wu_t, h, g_pre))


def _bwd_in(dp, ws, transposed, h, g, dout, name, side):
    t, d = h.shape
    n = dp.shape[1]
    tm = 256
    widths = [w.shape[0] if transposed else w.shape[1] for w in ws]

    def body(dp_ref, *refs):
        w_refs = refs[:len(ws)]
        h_ref, g_ref, dout_ref, dh_ref, dg_ref = refs[len(ws):]

        @pl.when(pl.program_id(0) == 0)
        def _():
            dg_ref[...] = jnp.zeros_like(dg_ref)

        dz, at = None, 0
        for w_ref, width in zip(w_refs, widths):
            block = dp_ref[:, at:at + width]
            if transposed:
                part = jnp.dot(block, w_ref[...], preferred_element_type=F32)
            else:
                part = lax.dot_general(block, w_ref[...], NT_DIMS, preferred_element_type=F32)
            dz = part if dz is None else dz + part
            at += width
        hhat, r = _rms(h_ref[...])
        dg_ref[...] += jnp.sum(dz * hhat, axis=0, keepdims=True)
        dh_ref[...] = dout_ref[...] + _rms_bwd(hhat, r, g_ref[...], dz)

    return _side_call(
        body, side, name=name, steps=t // tm,
        in_specs=[_rows(tm, n)] + [_full(w.shape) for w in ws] + [_rows(tm, d), _full((1, d)), _rows(tm, d)],
        out_specs=[_rows(tm, d), _full((1, d))],
        out_shape=[_sds((t, d), F32), _sds((1, d), F32)],
        scratch_shapes=[], args=(dp, *ws, h, g, dout))


def _matmul_tn(a, b, tile_a, tile, name):
    t, ka = a.shape
    nb = b.shape[1]

    def body(a_ref, b_ref, o_ref):
        o_ref[...] = lax.dot_general(a_ref[...], b_ref[...], TN_DIMS, preferred_element_type=F32).astype(BF16)

    if tile_a:
        grid = (ka // tile,)
        in_specs = [pl.BlockSpec((t, tile), lambda i: (0, i)), _full((t, nb))]
        out_specs = pl.BlockSpec((tile, nb), lambda i: (i, 0))
    else:
        grid = (nb // tile,)
        in_specs = [_full((t, ka)), pl.BlockSpec((t, tile), lambda i: (0, i))]
        out_specs = pl.BlockSpec((ka, tile), lambda i: (0, i))
    return pl.pallas_call(
        body, name=name, grid=grid, in_specs=in_specs, out_specs=out_specs,
        out_shape=_sds((ka, nb), BF16), compiler_params=_params("parallel"),
    )(a, b)


def _ml_gate_prep(gt, bif):
    th = jnp.tanh((gt + bif) / GATE_CAP)
    act = GATE_CAP * th
    cum = _cumsum_rows(_log_sigmoid(act))
    lane = lax.broadcasted_iota(jnp.int32, gt.shape, 1)
    x = jnp.where(lane < ML_HEADS, act, cum)
    return x, x.T, th, act


HEADS = range(ML_HEADS)


def _each(fn, *per_head):
    return [fn(*a) for a in zip(*per_head)]


def _ml_chunk_fwd(q, k, v, kt, x, xt, c_in, n_in, m_in):
    causal = (lax.broadcasted_iota(jnp.int32, (CHUNK, CHUNK), 0)
              >= lax.broadcasted_iota(jnp.int32, (CHUNK, CHUNK), 1))
    f = {}
    qh = f["qh"] = [q[:, ML_DK * h:ML_DK * (h + 1)] for h in HEADS]
    kh = f["kh"] = [k[:, ML_DK * h:ML_DK * (h + 1)] for h in HEADS]
    f["vh"] = [v[:, ML_DV * h:ML_DV * (h + 1)] for h in HEADS]
    kth = [kt[ML_DK * h:ML_DK * (h + 1), :] for h in HEADS]
    s = _each(lambda a, b: jnp.dot(a, b, preferred_element_type=F32) * QK_SCALE, qh, kth)
    f["qc"] = _each(lambda a, c: jnp.dot(a, c.astype(BF16), preferred_element_type=F32) * QK_SCALE, qh, c_in)
    bcol = f["bcol"] = [x[:, ML_HEADS + h:ML_HEADS + h + 1] for h in HEADS]
    licol = f["licol"] = [x[:, h:h + 1] for h in HEADS]
    brow = f["brow"] = [xt[ML_HEADS + h:ML_HEADS + h + 1, :] for h in HEADS]
    lirow = [xt[h:h + 1, :] for h in HEADS]
    dmat = _each(lambda bc, br, lr: jnp.where(causal, bc - br + lr, NEG_BIG), bcol, brow, lirow)
    inter = _each(lambda bc, m: bc + m, bcol, m_in)
    mt = _each(lambda d, i: jnp.maximum(jnp.max(d, axis=1, keepdims=True), i), dmat, inter)
    wt = f["wt"] = _each(lambda d, m: jnp.exp(d - m), dmat, mt)
    p = f["p"] = _each(lambda a, b: a * b, wt, s)
    winter = f["winter"] = _each(lambda i, m: jnp.exp(i - m), inter, mt)
    qf = f["qf"] = [a.astype(F32) for a in qh]
    qn = f["qn"] = _each(lambda a, n: jnp.sum(a * n, axis=1, keepdims=True) * QK_SCALE, qf, n_in)
    den = f["den"] = _each(lambda a, w, b: jnp.sum(a, axis=1, keepdims=True) + w * b, p, winter, qn)
    emt = f["emt"] = [jnp.exp(-m) for m in mt]
    f["nrm"] = _each(lambda d, e: jnp.maximum(jnp.abs(d), e), den, emt)
    gtot = [bc[CHUNK - 1:CHUNK, :] for bc in bcol]
    a_col = _each(lambda g, bc, lc: g - bc + lc, gtot, bcol, licol)
    a_row = _each(lambda g, br, lr: g - br + lr, gtot, brow, lirow)
    m_new = f["m_new"] = _each(lambda g, m, a: jnp.maximum(g + m, jnp.max(a, axis=1, keepdims=True)),
                               gtot, m_in, a_row)
    f["decay"] = _each(lambda g, m, mn: jnp.exp(g + m - mn), gtot, m_in, m_new)
    wkf = f["wkf"] = _each(lambda a, mn: jnp.exp(a - mn), a_col, m_new)
    f["kw"] = _each(lambda a, w: a.astype(F32) * w, kh, wkf)
    f["ktw"] = _each(lambda a, ar, mn: (a.astype(F32) * jnp.exp(ar - mn)).astype(BF16), kth, a_row, m_new)
    return f


def _ml_specs(nc, rev):
    def at(col):
        if rev:
            return lambda i: (nc - 1 - i, col)
        return lambda i: (i, col)

    return [pl.BlockSpec((CHUNK, ML_QK), at(0)), pl.BlockSpec((CHUNK, ML_QK), at(1)),
            pl.BlockSpec((CHUNK, ML_V), at(1)), pl.BlockSpec((CHUNK, ML_V), at(0)),
            pl.BlockSpec((CHUNK, 128), at(ML_V // 128))]


def _mlstm_fwd(qkv, og, bif, hn, side):
    t = qkv.shape[0]
    nc = t // CHUNK

    def body(q_ref, k_ref, v_ref, o_ref, gt_ref, bif_ref, hn_ref, y_ref, hs_ref, cst_ref, nst_ref, mst_ref,
             c_sc, n_sc, m_sc):
        @pl.when(pl.program_id(0) == 0)
        def _():
            c_sc[...] = jnp.zeros_like(c_sc)
            n_sc[...] = jnp.zeros_like(n_sc)
            m_sc[...] = jnp.full_like(m_sc, ML_M_INIT)

        cst_ref[0] = c_sc[...]
        nst_ref[0] = n_sc[...]
        mst_ref[0] = m_sc[...]
        x, xt, _, _ = _ml_gate_prep(gt_ref[...], bif_ref[...])
        q, k, v = q_ref[...], k_ref[...], v_ref[...]
        c_all, n_all, m_all = c_sc[...], n_sc[...], m_sc[...]
        c_in = [c_all[ML_DK * h:ML_DK * (h + 1), :] for h in HEADS]
        n_in = [n_all[h:h + 1, :] for h in HEADS]
        m_in = [m_all[h:h + 1, 0:1] for h in HEADS]
        f = _ml_chunk_fwd(q, k, v, k.T, x, xt, c_in, n_in, m_in)
        num = _each(lambda p, vh, w, qc: jnp.dot(p.astype(BF16), vh, preferred_element_type=F32) + w * qc,
                    f["p"], f["vh"], f["winter"], f["qc"])
        hh = _each(lambda a, b: a / b, num, f["nrm"])
        hhat = [_rms(a)[0] for a in hh]
        c_new = _each(lambda d, c, kw, vh: d * c + jnp.dot(kw, vh, preferred_element_type=F32),
                      f["decay"], c_in, f["ktw"], f["vh"])
        n_new = _each(lambda d, n, kw: d * n + jnp.sum(kw, axis=0, keepdims=True), f["decay"], n_in, f["kw"])
        for h in HEADS:
            vs = slice(ML_DV * h, ML_DV * (h + 1))
            hs_ref[:, vs] = hh[h]
            y_ref[:, vs] = (hhat[h] * hn_ref[:, vs] * _sigmoid(o_ref[:, vs])).astype(BF16)
            c_sc[ML_DK * h:ML_DK * (h + 1), :] = c_new[h]
            n_sc[h:h + 1, :] = n_new[h]
            m_sc[h:h + 1, :] = jnp.broadcast_to(f["m_new"][h], (1, 128))

    return _side_call(
        body, side, name="mlstm_fwd", steps=nc,
        in_specs=_ml_specs(nc, False) + [_full((1, 128)), _full((1, ML_V))],
        out_specs=[_rows(CHUNK, ML_V), _rows(CHUNK, ML_V),
                   pl.BlockSpec((1, ML_HEADS * ML_DK, ML_DV), lambda i: (i, 0, 0)),
                   pl.BlockSpec((1, ML_HEADS, ML_DK), lambda i: (i, 0, 0)),
                   pl.BlockSpec((1, ML_HEADS, 128), lambda i: (i, 0, 0))],
        out_shape=[_sds((t, ML_V), BF16), _sds((t, ML_V), F32),
                   _sds((nc, ML_HEADS * ML_DK, ML_DV), F32), _sds((nc, ML_HEADS, ML_DK), F32),
                   _sds((nc, ML_HEADS, 128), F32)],
        scratch_shapes=[pltpu.VMEM((ML_HEADS * ML_DK, ML_DV), F32), pltpu.VMEM((ML_HEADS, ML_DK), F32),
                        pltpu.VMEM((ML_HEADS, 128), F32)],
        args=(qkv, qkv, qkv, og, og, bif, hn))


def _mlstm_bwd(qkv, og, bif, hn, hs, cst, nst, mst, dy, side):
    t = qkv.shape[0]
    nc = t // CHUNK

    def body(q_ref, k_ref, v_ref, o_ref, gt_ref, bif_ref, hn_ref, hs_ref, cst_ref, nst_ref, mst_ref, dy_ref,
             dp_ref, dhn_ref, dbif_ref, dc_sc, dct_sc, dn_sc):
        @pl.when(pl.program_id(0) == 0)
        def _():
            dc_sc[...] = jnp.zeros_like(dc_sc)
            dct_sc[...] = jnp.zeros_like(dct_sc)
            dn_sc[...] = jnp.zeros_like(dn_sc)
            dhn_ref[...] = jnp.zeros_like(dhn_ref)
            dbif_ref[...] = jnp.zeros_like(dbif_ref)

        x, xt, th, act = _ml_gate_prep(gt_ref[...], bif_ref[...])
        q, k, v = q_ref[...], k_ref[...], v_ref[...]
        qt, vt = q.T, v.T
        rows = lax.broadcasted_iota(jnp.int32, (CHUNK, CHUNK), 0)
        cols = lax.broadcasted_iota(jnp.int32, (CHUNK, CHUNK), 1)
        lane = lax.broadcasted_iota(jnp.int32, (CHUNK, 128), 1)
        row = lax.broadcasted_iota(jnp.int32, (CHUNK, 1), 0)
        as_row = lambda col: jnp.sum(jnp.where(rows == cols, col, 0.0), axis=0, keepdims=True)
        mm = lambda a, b: jnp.dot(a, b, preferred_element_type=F32)
        bf = lambda a: a.astype(BF16)
        ksl = [slice(ML_DK * h, ML_DK * (h + 1)) for h in HEADS]
        vsl = [slice(ML_DV * h, ML_DV * (h + 1)) for h in HEADS]
        c_in = [cst_ref[0, s, :] for s in ksl]
        n_in = [nst_ref[0, h:h + 1, :] for h in HEADS]
        m_in = [mst_ref[0, h:h + 1, 0:1] for h in HEADS]
        dc_all, dct_all, dn_all = dc_sc[...], dct_sc[...], dn_sc[...]
        dcn = [dc_all[s, :] for s in ksl]
        dcn_t = [dct_all[s, :] for s in vsl]
        dnn = [dn_all[h:h + 1, :] for h in HEADS]
        f = _ml_chunk_fwd(q, k, v, k.T, x, xt, c_in, n_in, m_in)
        qh, kh, vh, p, winter, decay = f["qh"], f["kh"], f["vh"], f["p"], f["winter"], f["decay"]
        qth = [qt[s, :] for s in ksl]
        vth = [vt[s, :] for s in vsl]
        c_t = [bf(c.T) for c in c_in]
        dmat_t = _each(lambda br, bc, lc: jnp.where(rows <= cols, br - bc + lc, NEG_BIG),
                       f["brow"], f["bcol"], f["licol"])
        inter_row = _each(lambda br, m: br + m, f["brow"], m_in)
        mt_row = _each(lambda d, i: jnp.maximum(jnp.max(d, axis=0, keepdims=True), i), dmat_t, inter_row)
        wt_t = _each(lambda d, m: jnp.exp(d - m), dmat_t, mt_row)
        p_t = _each(lambda w, a, b: w * (mm(a, b) * QK_SCALE), wt_t, kh, qth)
        winter_row = _each(lambda i, m: jnp.exp(i - m), inter_row, mt_row)
        hh = [hs_ref[:, s] for s in vsl]
        hn_h = [hn_ref[:, s] for s in vsl]
        sg = [_sigmoid(o_ref[:, s]) for s in vsl]
        dyh = [dy_ref[:, s] for s in vsl]
        norm = [_rms(a) for a in hh]
        hhat, r = [a for a, _ in norm], [b for _, b in norm]
        dyn = _each(lambda a, b: a * b, dyh, sg)
        do = _each(lambda d, hx, g, s: d * hx * g * s * (1.0 - s), dyh, hhat, hn_h, sg)
        dhn = _each(lambda a, b: jnp.sum(a * b, axis=0, keepdims=True), dyn, hhat)
        dh = _each(_rms_bwd, hhat, r, hn_h, dyn)
        inv = [1.0 / a for a in f["nrm"]]
        dnum = _each(lambda a, b: a * b, dh, inv)
        dnrm = _each(lambda a, b, c: -jnp.sum(a * b, axis=1, keepdims=True) * c, dh, hh, inv)
        dden = _each(lambda d, e, g: jnp.where(jnp.abs(d) > e, g * jnp.sign(d), 0.0), f["den"], f["emt"], dnrm)
        dnb = [bf(a) for a in dnum]
        dnt = [bf(a.T) for a in dnum]
        rmat = _each(lambda a, b, d: mm(a, b) + d, dnb, vth, dden)
        rmat_t = _each(lambda a, b, d: mm(a, b) + as_row(d), vh, dnt, dden)
        ds = _each(lambda w, a: bf(w * a), f["wt"], rmat)
        ds_t = _each(lambda w, a: bf(w * a), wt_t, rmat_t)
        dv = _each(lambda a, b: mm(bf(a), b), p_t, dnb)
        dqs = _each(lambda s, kk, w, d, ct, dd, n: mm(s, kk) + w * (mm(d, ct) + dd * n),
                    ds, kh, winter, dnb, c_t, dden, n_in)
        dk = _each(lambda s, a: mm(s, a) * QK_SCALE, ds_t, qh)
        dinter = _each(lambda qc, dn_, qn, dd, w: (jnp.sum(qc * dn_, axis=1, keepdims=True) + qn * dd) * w,
                       f["qc"], dnum, f["qn"], dden, winter)
        wq = _each(lambda w, a: w * a * QK_SCALE, winter, f["qf"])
        wq_t = _each(lambda w, a: bf(w * a.astype(F32) * QK_SCALE), winter_row, qth)
        dc_loc = _each(mm, wq_t, dnb)
        dct_loc = _each(lambda a, b: mm(a, bf(b)), dnt, wq)
        dn_loc = _each(lambda a, d: jnp.sum(a * d, axis=0, keepdims=True), wq, dden)
        cs_q = _each(lambda a, b: jnp.sum(a * b, axis=1, keepdims=True), p_t, rmat_t)
        db = _each(lambda a, b, di, cs: jnp.sum(a * b, axis=1, keepdims=True) + di - cs, p, rmat, dinter, cs_q)
        ddecay = _each(lambda dc_, c, dn_, n: jnp.sum(jnp.sum(dc_ * c, axis=1, keepdims=True), axis=0, keepdims=True)
                       + jnp.sum(dn_ * n, axis=1, keepdims=True), dcn, c_in, dnn, n_in)
        dkw = _each(lambda a, b, n: mm(a, bf(b)) + n, vh, dcn_t, dnn)
        dk = _each(lambda a, w, b: a + w * b, dk, f["wkf"], dkw)
        da = _each(lambda a, kk, w: jnp.sum(a * kk.astype(F32), axis=1, keepdims=True) * w, dkw, kh, f["wkf"])
        dv = _each(lambda a, kw, dc_: a + mm(bf(kw), bf(dc_)), dv, f["kw"], dcn)
        dgtot = _each(lambda a, dd, d: jnp.sum(a, axis=0, keepdims=True) + dd * d, da, ddecay, decay)
        db = _each(lambda a, b, g: a - b + jnp.where(row == CHUNK - 1, g, 0.0), db, da, dgtot)
        dli = _each(lambda a, b: a + b, cs_q, da)
        dc_new = _each(lambda d, a, b: d * a + b, decay, dcn, dc_loc)
        dct_new = _each(lambda d, a, b: d * a + b, decay, dcn_t, dct_loc)
        dn_new = _each(lambda d, a, b: d * a + b, decay, dnn, dn_loc)
        dx = jnp.zeros((CHUNK, 128), F32)
        for h in HEADS:
            dhn_ref[:, vsl[h]] += dhn[h]
            dc_sc[ksl[h], :] = dc_new[h]
            dct_sc[vsl[h], :] = dct_new[h]
            dn_sc[h:h + 1, :] = dn_new[h]
            dp_ref[:, ksl[h]] = bf(dqs[h] * QK_SCALE)
            dp_ref[:, ML_QK + ML_DK * h:ML_QK + ML_DK * (h + 1)] = bf(dk[h])
            dp_ref[:, 2 * ML_QK + ML_DV * h:2 * ML_QK + ML_DV * (h + 1)] = bf(dv[h])
            dp_ref[:, 2 * ML_QK + ML_V + ML_DV * h:2 * ML_QK + ML_V + ML_DV * (h + 1)] = bf(do[h])
            dx = jnp.where(lane == h, dli[h], dx)
            dx = jnp.where(lane == ML_HEADS + h, db[h], dx)
        dlf = _cumsum_rows(dx, reverse=True)
        dact = jnp.where(lane < ML_HEADS, dx, dlf * _sigmoid(-act))
        dz = dact * (1.0 - th * th)
        dp_ref[:, 2 * ML_QK + 2 * ML_V:] = dz.astype(BF16)
        dbif_ref[...] += jnp.sum(dz, axis=0, keepdims=True)

    rev = lambda i: (nc - 1 - i, 0)
    rev3 = lambda i: (nc - 1 - i, 0, 0)
    return _side_call(
        body, side, name="mlstm_bwd", steps=nc,
        in_specs=_ml_specs(nc, True) + [
            _full((1, 128)), _full((1, ML_V)), pl.BlockSpec((CHUNK, ML_V), rev),
            pl.BlockSpec((1, ML_HEADS * ML_DK, ML_DV), rev3), pl.BlockSpec((1, ML_HEADS, ML_DK), rev3),
            pl.BlockSpec((1, ML_HEADS, 128), rev3), pl.BlockSpec((CHUNK, ML_V), rev)],
        out_specs=[pl.BlockSpec((CHUNK, ML_IN_PAD), rev), _full((1, ML_V)), _full((1, 128))],
        out_shape=[_sds((t, ML_IN_PAD), BF16), _sds((1, ML_V), F32), _sds((1, 128), F32)],
        scratch_shapes=[pltpu.VMEM((ML_HEADS * ML_DK, ML_DV), F32), pltpu.VMEM((ML_HEADS * ML_DV, ML_DK), F32),
                        pltpu.VMEM((ML_HEADS, ML_DK), F32)],
        args=(qkv, qkv, qkv, og, og, bif, hn, hs, cst, nst, mst, dy))


LRU_TM = 256
GELU_K = math.sqrt(2.0 / math.pi)
GELU_C = 0.044715


def _gelu(x):
    th = jnp.tanh(GELU_K * (x + GELU_C * x * x * x))
    return 0.5 * x * (1.0 + th), th


def _neg_expm1(x):
    series = -x * (1.0 + x * (0.5 + x * (1.0 / 6.0 + x * (1.0 / 24.0))))
    return jnp.where(x > -0.05, series, 1.0 - jnp.exp(x))


def _block_diag_dot(a, w_ref, dims):
    parts = [lax.dot_general(a[:, LRU_BLOCK * n:LRU_BLOCK * (n + 1)], w_ref[n], dims, preferred_element_type=F32)
             for n in range(LRU_BLOCKS)]
    return jnp.concatenate(parts, axis=1)


def _lru_gates(u, r, ig, lam):
    ls = _log_sigmoid(lam)
    la = LRU_C * r * ls
    a = jnp.exp(la)
    em = _neg_expm1(2.0 * la)
    mult = jnp.sqrt(em)
    return ls, a, em, mult


def _lru_fwd(proj, cw, cb, wa, ba, wx, bx, lam):
    t = proj.shape[0]
    w = D_MODEL
    tm = min(LRU_TM, t)

    def body(gb_ref, up_ref, cw_ref, cb_ref, wa_ref, ba_ref, wx_ref, bx_ref, lam_ref,
             y_ref, u_ref, r_ref, i_ref, h_ref, tail_sc, hprev_sc):
        @pl.when(pl.program_id(0) == 0)
        def _():
            tail_sc[...] = jnp.zeros_like(tail_sc)
            hprev_sc[...] = jnp.zeros_like(hprev_sc)

        up = up_ref[...]
        ext = jnp.concatenate([tail_sc[...], up], axis=0)
        u = cb_ref[...] + cw_ref[CONV_WIDTH - 1:CONV_WIDTH, :] * up
        for s in range(1, CONV_WIDTH):
            u = u + cw_ref[CONV_WIDTH - 1 - s:CONV_WIDTH - s, :] * pltpu.roll(ext, s, 0)[8:8 + tm]
        tail_sc[...] = up[tm - 8:tm]
        ub = u.astype(BF16)
        r = _sigmoid(_block_diag_dot(ub, wa_ref, (((1,), (0,)), ((), ()))) + ba_ref[...])
        ig = _sigmoid(_block_diag_dot(ub, wx_ref, (((1,), (0,)), ((), ()))) + bx_ref[...])
        _, a, _, mult = _lru_gates(u, r, ig, lam_ref[...])
        acum, hloc = _scan_rows(a, mult * ig * u)
        h = hloc + acum * hprev_sc[0:1, :]
        hprev_sc[0:1, :] = h[tm - 1:tm]
        u_ref[...] = u
        r_ref[...] = r
        i_ref[...] = ig
        h_ref[...] = h
        gel, _ = _gelu(gb_ref[...])
        y_ref[...] = (h * gel).astype(BF16)

    vec = _full((1, w))
    wspec = _full((LRU_BLOCKS, LRU_BLOCK, LRU_BLOCK))
    return pl.pallas_call(
        body, name="lru_fwd", grid=(t // tm,),
        in_specs=[pl.BlockSpec((tm, w), lambda i: (i, 0)), pl.BlockSpec((tm, w), lambda i: (i, 1)),
                  _full((CONV_WIDTH, w)), vec, wspec, vec, wspec, vec, vec],
        out_specs=[_rows(tm, w)] * 5,
        out_shape=[_sds((t, w), BF16)] + [_sds((t, w), F32)] * 4,
        scratch_shapes=[pltpu.VMEM((8, w), F32), pltpu.VMEM((8, w), F32)],
        compiler_params=_params("arbitrary"),
    )(proj, proj, cw, cb, wa, ba, wx, bx, lam)


def _lru_bwd(proj, cw, wa, wx, lam, u, r, ig, h, dy, side):
    t = proj.shape[0]
    w = D_MODEL
    tm = min(LRU_TM, t)
    nt = t // tm

    def body(gb_ref, up_ref, cw_ref, wa_ref, wx_ref, lam_ref, u_ref, r_ref, i_ref, h_ref, hp_ref, dy_ref,
             dp_ref, dcw_ref, dcb_ref, dwa_ref, dba_ref, dwx_ref, dbx_ref, dlam_ref, carry_sc, dutail_sc, dls_sc):
        step = pl.program_id(0)

        @pl.when(step == 0)
        def _():
            carry_sc[...] = jnp.zeros_like(carry_sc)
            dutail_sc[...] = jnp.zeros_like(dutail_sc)
            dls_sc[...] = jnp.zeros_like(dls_sc)
            for ref in (dcw_ref, dcb_ref, dwa_ref, dba_ref, dwx_ref, dbx_ref):
                ref[...] = jnp.zeros_like(ref)

        row = lax.broadcasted_iota(jnp.int32, (tm, w), 0)
        u_t, r_t, i_t, h_t = u_ref[...], r_ref[...], i_ref[...], h_ref[...]
        ls, a, em, mult = _lru_gates(u_t, r_t, i_t, lam_ref[...])
        gb = gb_ref[...]
        gel, th = _gelu(gb)
        dyv = dy_ref[...]
        dgb = dyv * h_t * (0.5 * (1.0 + th) + 0.5 * gb * (1.0 - th * th) * GELU_K * (1.0 + 3.0 * GELU_C * gb * gb))
        dh = jnp.where(row == tm - 1, dyv * gel + carry_sc[0:1, :], dyv * gel)
        a_next = jnp.where(row < tm - 1, pltpu.roll(a, tm - 1, 0), 0.0)
        _, g = _scan_rows(a_next, dh, reverse=True)
        carry_sc[0:1, :] = a[0:1] * g[0:1]
        has_prev = jnp.where(step == nt - 1, 0.0, 1.0)
        h_prev = jnp.where(row >= 1, pltpu.roll(h_t, 1, 0), hp_ref[7:8, :] * has_prev)
        dmult = g * i_t * u_t
        dig = g * mult * u_t
        du = g * mult * i_t
        dla = g * h_prev * a - dmult * (1.0 - em) / mult
        dls_sc[0:1, :] += jnp.sum(dla * r_t, axis=0, keepdims=True) * LRU_C
        dpa = dla * (LRU_C * ls) * r_t * (1.0 - r_t)
        dpx = dig * i_t * (1.0 - i_t)
        dba_ref[...] += jnp.sum(dpa, axis=0, keepdims=True)
        dbx_ref[...] += jnp.sum(dpx, axis=0, keepdims=True)
        ub = u_t.astype(BF16)
        dpab = dpa.astype(BF16)
        dpxb = dpx.astype(BF16)
        for n in range(LRU_BLOCKS):
            cs = slice(LRU_BLOCK * n, LRU_BLOCK * (n + 1))
            dwa_ref[n] += lax.dot_general(ub[:, cs], dpab[:, cs], TN_DIMS, preferred_element_type=F32)
            dwx_ref[n] += lax.dot_general(ub[:, cs], dpxb[:, cs], TN_DIMS, preferred_element_type=F32)
        du = du + _block_diag_dot(dpab, wa_ref, NT_DIMS) + _block_diag_dot(dpxb, wx_ref, NT_DIMS)
        dcb_ref[...] += jnp.sum(du, axis=0, keepdims=True)
        ext = jnp.concatenate([du, dutail_sc[...]], axis=0)
        up = up_ref[...]
        dup = cw_ref[CONV_WIDTH - 1:CONV_WIDTH, :] * du
        dcw_ref[CONV_WIDTH - 1:CONV_WIDTH, :] += jnp.sum(up * du, axis=0, keepdims=True)
        for s in range(1, CONV_WIDTH):
            du_s = pltpu.roll(ext, tm + 8 - s, 0)[0:tm]
            dup = dup + cw_ref[CONV_WIDTH - 1 - s:CONV_WIDTH - s, :] * du_s
            dcw_ref[CONV_WIDTH - 1 - s:CONV_WIDTH - s, :] += jnp.sum(up * du_s, axis=0, keepdims=True)
        dutail_sc[...] = du[0:8]
        dp_ref[:, :w] = dgb.astype(BF16)
        dp_ref[:, w:] = dup.astype(BF16)

        @pl.when(step == nt - 1)
        def _():
            dlam_ref[...] = dls_sc[0:1, :] * _sigmoid(-lam_ref[...])

    rev = lambda col: (lambda i: (nt - 1 - i, col))
    vec = _full((1, w))
    wspec = _full((LRU_BLOCKS, LRU_BLOCK, LRU_BLOCK))
    tile = pl.BlockSpec((tm, w), rev(0))
    prev8 = pl.BlockSpec((8, w), lambda i: (jnp.maximum((nt - 1 - i) * (tm // 8) - 1, 0), 0))
    return _side_call(
        body, side, name="lru_bwd", steps=nt,
        in_specs=[tile, pl.BlockSpec((tm, w), rev(1)), _full((CONV_WIDTH, w)), wspec, wspec, vec,
                  tile, tile, tile, tile, prev8, tile],
        out_specs=[pl.BlockSpec((tm, 2 * w), rev(0)), _full((CONV_WIDTH, w)), vec, wspec, vec, wspec, vec, vec],
        out_shape=[_sds((t, 2 * w), BF16), _sds((CONV_WIDTH, w), F32), _sds((1, w), F32),
                   _sds((LRU_BLOCKS, LRU_BLOCK, LRU_BLOCK), F32), _sds((1, w), F32),
                   _sds((LRU_BLOCKS, LRU_BLOCK, LRU_BLOCK), F32), _sds((1, w), F32), _sds((1, w), F32)],
        scratch_shapes=[pltpu.VMEM((8, w), F32), pltpu.VMEM((8, w), F32), pltpu.VMEM((8, w), F32)],
        args=(proj, proj, cw, wa, wx, lam, u, r, ig, h, h, dy))


LANES = 1024
HALF_FFN = D_FF // N_CHIPS
GROUPS = {
    "ml": (("ml_w_in", 1024, ML_IN // N_CHIPS), ("ml_w_out", 256, 1024)),
    "lru": (("lru_w_in", 1024, 512),
            ("lru_gates", 2 * LRU_BLOCKS * 64, LRU_BLOCK),
            ("lru_w_out", 256, 1024)),
    "ffn0": (("ffn_g0", HALF_FFN, 1024), ("ffn_u0", HALF_FFN, 1024), ("ffn_down0", HALF_FFN, 1024)),
    "ffn1": (("ffn_g1", HALF_FFN, 1024), ("ffn_u1", HALF_FFN, 1024), ("ffn_down1", HALF_FFN, 1024)),
}

SMALL_ROWS = 24
ROW_LOSS, ROW_HEAD_NORM, ROW_B_IF, ROW_LRU = 8, 9, 10, 11


def _row_tile(rows, cols, itemsize, budget=3 << 19):
    best = 16
    for t in range(16, rows + 1, 16):
        if rows % t == 0 and t * cols * itemsize <= budget:
            best = t
    return best


def _chip_peers():
    x, y, c = lax.axis_index("x"), lax.axis_index("y"), lax.axis_index("c")
    return x, y, c, [(1 - x, y), (x, 1 - y), (1 - x, 1 - y)]


HBM_SPEC = pl.BlockSpec(memory_space=pltpu.HBM)


def _remote(src, dst, send_sems, recv_sems, k, to):
    return pltpu.make_async_remote_copy(src_ref=src, dst_ref=dst, send_sem=send_sems.at[k], recv_sem=recv_sems.at[k],
                                        device_id=to, device_id_type=MESH)


GATHER_SEMS = 7


def _gather_copies(kind, specs, part_refs, out_refs, send_sems, recv_sems):
    x, y, c, chips = _chip_peers()
    me = 2 * x + y
    sib = (x, y, 1 - c)
    copy = functools.partial(_remote, send_sems=send_sems, recv_sems=recv_sems)
    out = []
    for p, (_, rows, _) in enumerate(specs):
        mine = pl.ds(c * (rows // 2), rows // 2)
        theirs = pl.ds((1 - c) * (rows // 2), rows // 2)
        base = GATHER_SEMS * p
        for j, (cx, cy) in enumerate(chips):
            land = out_refs[p].at[2 * cx + cy, mine]
            other = out_refs[p].at[2 * cx + cy, theirs]
            if kind == "first":
                out.append(copy(part_refs[p].at[mine], out_refs[p].at[me, mine], k=base + j, to=(cx, cy, c)))
            elif kind == "landed":
                out.append(copy(land, land, k=base + j, to=sib))
            elif kind == "forward":
                out.append(copy(land, land, k=base + 3 + j, to=sib))
            else:
                out.append(copy(other, other, k=base + 3 + j, to=sib))
        if kind in ("first", "arriving"):
            out.append(copy(part_refs[p], out_refs[p].at[me], k=base + 6, to=sib))
    return out


def _gather_start(specs, *refs):
    for cp in _gather_copies("first", specs, *refs):
        cp.start()


def _gather_finish(specs, *refs):
    forwards = _gather_copies("forward", specs, *refs)
    for land, fwd in zip(_gather_copies("landed", specs, *refs), forwards):
        land.wait_recv()
        fwd.start()
    for cp in _gather_copies("arriving", specs, *refs):
        cp.wait_recv()
    for cp in _gather_copies("first", specs, *refs) + forwards:
        cp.wait_send()


def _gather_shapes(specs):
    return [_sds((N_CHIPS, rows, cols), BF16) for _, rows, cols in specs]


def _gather_side(specs, parts):
    return _Side(inputs=list(parts), out_shape=_gather_shapes(specs), n_sems=GATHER_SEMS * len(specs),
                 start=functools.partial(_gather_start, specs), finish=functools.partial(_gather_finish, specs))


def _gather_weights(specs, parts, small):
    n = len(specs)

    def body(*refs):
        part_refs, small_ref = refs[:n], refs[n]
        out_refs, outs_ref = refs[n + 1:2 * n + 1], refs[2 * n + 1]
        send_sems, recv_sems, small_send, small_recv, loc_sem = refs[2 * n + 2:]
        x, y, c, chips = _chip_peers()
        me = 2 * x + y
        local = pltpu.make_async_copy(small_ref, outs_ref.at[me], loc_sem.at[0])
        local.start()
        _gather_start(specs, part_refs, out_refs, send_sems, recv_sems)
        sent = [_remote(small_ref, outs_ref.at[me], small_send, small_recv, j, (cx, cy, c))
                for j, (cx, cy) in enumerate(chips)]
        for cp in sent:
            cp.start()
        _gather_finish(specs, part_refs, out_refs, send_sems, recv_sems)
        for j, (cx, cy) in enumerate(chips):
            _remote(small_ref, outs_ref.at[2 * cx + cy], small_send, small_recv, j, (cx, cy, c)).wait_recv()
        for cp in sent:
            cp.wait_send()
        local.wait()

    dma = pltpu.SemaphoreType.DMA
    return pl.pallas_call(
        body, name="gather_weights",
        in_specs=[HBM_SPEC] * (n + 1), out_specs=[HBM_SPEC] * (n + 1),
        out_shape=_gather_shapes(specs) + [_sds((N_CHIPS,) + small.shape, small.dtype)],
        scratch_shapes=[dma((GATHER_SEMS * n,)), dma((GATHER_SEMS * n,)), dma((3,)), dma((3,)), dma((1,))],
    )(*parts, small)


def _exchange_copies(specs, g_refs, out_refs, send_sems, recv_sems):
    x, y, c, _ = _chip_peers()
    return [_remote(g_refs[p].at[:, pl.ds((1 - c) * (rows // 2), rows // 2)], out_refs[p], send_sems, recv_sems, p,
                    (x, y, 1 - c)) for p, (_, rows, _) in enumerate(specs)]


def _exchange_start(specs, *refs):
    for cp in _exchange_copies(specs, *refs):
        cp.start()


def _exchange_finish(specs, *refs):
    for cp in _exchange_copies(specs, *refs):
        cp.wait()


def _exchange_shapes(specs):
    return [_sds((N_CHIPS, rows // 2, cols), BF16) for _, rows, cols in specs]


def _exchange_side(specs, gparts):
    return _Side(inputs=list(gparts), out_shape=_exchange_shapes(specs), n_sems=len(specs),
                 start=functools.partial(_exchange_start, specs), finish=functools.partial(_exchange_finish, specs))


def _exchange_halves(specs, gparts, name):
    n = len(specs)

    def body(*refs):
        _exchange_start(specs, refs[:n], refs[n:2 * n], *refs[2 * n:])
        _exchange_finish(specs, refs[:n], refs[n:2 * n], *refs[2 * n:])

    return pl.pallas_call(
        body, name=name, in_specs=[HBM_SPEC] * n, out_specs=[HBM_SPEC] * n, out_shape=_exchange_shapes(specs),
        scratch_shapes=[pltpu.SemaphoreType.DMA((n,)), pltpu.SemaphoreType.DMA((n,))],
    )(*gparts)


def _add_halves(g, recv, pos, name):
    _, half, cols = recv.shape
    tr = _row_tile(half, cols, 2)
    tiles = half // tr

    def body(pos_ref, a_ref, b_ref, o_ref):
        o_ref[...] = (a_ref[...].astype(F32) + b_ref[...].astype(F32)).astype(BF16)

    spec = pl.BlockSpec((1, tr, cols), lambda k, i, pos_ref: (k, i, 0))
    return pl.pallas_call(
        body, name=name,
        grid_spec=pltpu.PrefetchScalarGridSpec(
            num_scalar_prefetch=1, grid=(N_CHIPS, tiles),
            in_specs=[pl.BlockSpec((1, tr, cols), lambda k, i, pos_ref: (k, pos_ref[0] * tiles + i, 0)), spec],
            out_specs=spec),
        out_shape=_sds((N_CHIPS, half, cols), BF16),
        compiler_params=_params("parallel", "parallel"),
    )(pos, g, recv)


def _scatter_copies(n, s1_refs, recv_refs, send_sems, recv_sems):
    x, y, c, chips = _chip_peers()
    return [_remote(s1_refs[p].at[2 * cx + cy], recv_refs[p].at[j], send_sems, recv_sems, 3 * p + j, (cx, cy, c))
            for p in range(n) for j, (cx, cy) in enumerate(chips)]


def _scatter_start(n, s1_refs, recv_refs, send_sems, recv_sems):
    for cp in _scatter_copies(n, s1_refs, recv_refs, send_sems, recv_sems):
        cp.start()


def _scatter_finish(n, s1_refs, recv_refs, send_sems, recv_sems):
    for cp in _scatter_copies(n, s1_refs, recv_refs, send_sems, recv_sems):
        cp.wait()


def _scatter_shapes(s1):
    return [_sds((3,) + a.shape[1:], a.dtype) for a in s1]


def _scatter_side(s1):
    n = len(s1)
    return _Side(inputs=list(s1), out_shape=_scatter_shapes(s1), n_sems=3 * n,
                 start=functools.partial(_scatter_start, n), finish=functools.partial(_scatter_finish, n))


def _share_small(small):
    flips = [(fx, fy, fc) for fx in (0, 1) for fy in (0, 1) for fc in (0, 1)][1:]

    def body(small_ref, all_ref, send_sems, recv_sems, loc_sem):
        x, y, c, _ = _chip_peers()
        my_slot = all_ref.at[4 * x + 2 * y + c]
        local = pltpu.make_async_copy(small_ref, my_slot, loc_sem.at[0])
        local.start()
        peers = [(1 - x if fx else x, 1 - y if fy else y, 1 - c if fc else c) for fx, fy, fc in flips]
        sent = [_remote(small_ref, my_slot, send_sems, recv_sems, i, p) for i, p in enumerate(peers)]
        for cp in sent:
            cp.start()
        for i, (px, py, pc) in enumerate(peers):
            _remote(small_ref, all_ref.at[4 * px + 2 * py + pc], send_sems, recv_sems, i, peers[i]).wait_recv()
        for cp in sent:
            cp.wait_send()
        local.wait()

    dma = pltpu.SemaphoreType.DMA
    return pl.pallas_call(
        body, name="share_small", in_specs=[HBM_SPEC], out_specs=HBM_SPEC,
        out_shape=_sds((8,) + small.shape, small.dtype), scratch_shapes=[dma((7,)), dma((7,)), dma((1,))],
    )(small)


def _sum_chips(s1, recv, pos, name):
    _, half, cols = recv.shape
    tr = _row_tile(half, cols, 4)
    tiles = half // tr

    def body(pos_ref, a_ref, b_ref, o_ref):
        acc = a_ref[0].astype(F32)
        for j in range(3):
            acc = acc + b_ref[j].astype(F32)
        o_ref[...] = acc

    return pl.pallas_call(
        body, name=name,
        grid_spec=pltpu.PrefetchScalarGridSpec(
            num_scalar_prefetch=1, grid=(tiles,),
            in_specs=[pl.BlockSpec((1, tr, cols), lambda i, pos_ref: (pos_ref[1], i, 0)),
                      pl.BlockSpec((3, tr, cols), lambda i, pos_ref: (0, i, 0))],
            out_specs=pl.BlockSpec((tr, cols), lambda i, pos_ref: (pos_ref[0] * tiles + i, 0))),
        out_shape=_sds((2 * half, cols), F32),
        compiler_params=_params("parallel"),
    )(pos, s1, recv)


def _sum_small(small_all):
    def body(a_ref, o_ref):
        acc = a_ref[0]
        for d in range(1, 8):
            acc = acc + a_ref[d]
        o_ref[...] = acc

    return pl.pallas_call(body, name="sum_small", out_shape=_sds(small_all.shape[1:], F32))(small_all)


def _join_halves(specs, s2):
    n = len(specs)

    def body(*refs):
        buf_refs = refs[n:2 * n]
        send_sems, recv_sems = refs[2 * n:]
        x, y, c, _ = _chip_peers()
        sent = []
        for p, (_, rows, _) in enumerate(specs):
            mine = buf_refs[p].at[pl.ds(c * (rows // 2), rows // 2)]
            sent.append(_remote(mine, mine, send_sems, recv_sems, p, (x, y, 1 - c)))
            sent[-1].start()
        for p, (_, rows, _) in enumerate(specs):
            theirs = buf_refs[p].at[pl.ds((1 - c) * (rows // 2), rows // 2)]
            _remote(theirs, theirs, send_sems, recv_sems, p, (x, y, 1 - c)).wait_recv()
        for cp in sent:
            cp.wait_send()

    return pl.pallas_call(
        body, name="join_halves", in_specs=[HBM_SPEC] * n, out_specs=[HBM_SPEC] * n,
        out_shape=[_sds(a.shape, a.dtype) for a in s2],
        input_output_aliases={p: p for p in range(n)},
        scratch_shapes=[pltpu.SemaphoreType.DMA((n,)), pltpu.SemaphoreType.DMA((n,))],
    )(*s2)


def _adamw(w, g, m, v, name):
    rows, cols = w.shape
    tm = rows
    for cand in (512, 256, 128, 64, 32, 16, 8):
        if rows % cand == 0 and rows > cand:
            tm = cand
            break

    def body(w_ref, g_ref, m_ref, v_ref, d_ref, nm_ref, nv_ref):
        gv = g_ref[...]
        nm = ADAM_B1 * m_ref[...] + (1.0 - ADAM_B1) * gv
        nv = ADAM_B2 * v_ref[...] + (1.0 - ADAM_B2) * (gv * gv)
        m_hat = nm / (1.0 - ADAM_B1 ** ADAM_STEP)
        v_hat = nv / (1.0 - ADAM_B2 ** ADAM_STEP)
        d_ref[...] = -ADAM_LR * (m_hat / (jnp.sqrt(v_hat) + ADAM_EPS) + ADAM_WD * w_ref[...])
        nm_ref[...] = nm
        nv_ref[...] = nv

    spec = _rows(tm, cols)
    return pl.pallas_call(
        body, name=name, grid=(rows // tm,), in_specs=[spec] * 4, out_specs=[spec] * 3,
        out_shape=[_sds((rows, cols), F32)] * 3, compiler_params=_params("parallel"),
    )(w, g, m, v)


WEIGHTS = ("ml_w_in", "ml_b_if", "ml_head_norm", "ml_w_out", "lru_w_in", "lru_conv_w", "lru_conv_b", "lru_w_gate_a",
           "lru_b_gate_a", "lru_w_gate_x", "lru_b_gate_x", "lru_lambda", "lru_w_out", "norm_pre_mix", "norm_post_mix",
           "norm_pre_ffn", "norm_post_ffn", "ffn_w_gate", "ffn_w_up", "ffn_w_down")
LRU_VECTORS = ("lru_conv_b", "lru_b_gate_a", "lru_b_gate_x", "lru_lambda")
NORMS = ("norm_pre_mix", "norm_post_mix", "norm_pre_ffn", "norm_post_ffn")


def _by_cols(a):
    return jnp.concatenate([a[k] for k in range(N_CHIPS)], axis=-1)


def _by_chip(a, width):
    return jnp.stack([a[..., k * width:(k + 1) * width] for k in range(N_CHIPS)])


def _weight_shards(w):
    bf = lambda a: a.astype(BF16)
    shards = dict(
        ml_w_in=bf(w["ml_w_in"][0]), ml_w_out=bf(w["ml_w_out"][0]), lru_w_in=bf(w["lru_w_in"][0]),
        lru_gates=bf(jnp.concatenate([w["lru_w_gate_a"][0], w["lru_w_gate_x"][0]], axis=0)).reshape(-1, LRU_BLOCK),
        lru_w_out=bf(w["lru_w_out"][0]))
    gate_t, up_t = bf(jnp.swapaxes(w["ffn_w_gate"], 1, 2)), bf(jnp.swapaxes(w["ffn_w_up"], 1, 2))
    for layer in range(2):
        shards[f"ffn_g{layer}"] = gate_t[layer]
        shards[f"ffn_u{layer}"] = up_t[layer]
        shards[f"ffn_down{layer}"] = bf(w["ffn_w_down"][layer])
    return shards


def _ffn_weights(parts):
    return [a.reshape(D_FF, 1024) for a in parts]


def _ffn_grad_parts(dw_gu_t, dw_down):
    return [dw_gu_t[:D_FF].reshape(N_CHIPS, HALF_FFN, 1024), dw_gu_t[D_FF:].reshape(N_CHIPS, HALF_FFN, 1024),
            dw_down.reshape(N_CHIPS, HALF_FFN, 1024)]


def _sum_group(group, s1, recv, pos):
    return [_sum_chips(a, r, pos, "sum_chips_" + n) for (n, _, _), a, r in zip(GROUPS[group], s1, recv)]


def _update(w, grads, m, v):
    delta, new_m, new_v = {}, {}, {}
    for n in WEIGHTS:
        flip = (lambda a: jnp.swapaxes(a, 1, 2)) if n in ("ffn_w_gate", "ffn_w_up") else (lambda a: a)
        shape = flip(w[n]).shape
        as2d = lambda a: flip(a).reshape(-1, shape[-1])
        back = lambda a: flip(a.reshape(shape))
        d, nm, nv = _adamw(as2d(w[n]), as2d(grads[n]), as2d(m[n]), as2d(v[n]), "adamw_" + n)
        delta[n], new_m[n], new_v[n] = back(d), back(nm), back(nv)
    return delta, new_m, new_v


def kernel(x, ml_w_in, ml_b_if, ml_head_norm, ml_w_out, lru_w_in, lru_conv_w, lru_conv_b, lru_w_gate_a, lru_b_gate_a, lru_w_gate_x, lru_b_gate_x, lru_lambda, lru_w_out, norm_pre_mix, norm_post_mix, norm_pre_ffn, norm_post_ffn, ffn_w_gate, ffn_w_up, ffn_w_down, loss_target, m_ml_w_in, m_ml_b_if, m_ml_head_norm, m_ml_w_out, m_lru_w_in, m_lru_conv_w, m_lru_conv_b, m_lru_w_gate_a, m_lru_b_gate_a, m_lru_w_gate_x, m_lru_b_gate_x, m_lru_lambda, m_lru_w_out, m_norm_pre_mix, m_norm_post_mix, m_norm_pre_ffn, m_norm_post_ffn, m_ffn_w_gate, m_ffn_w_up, m_ffn_w_down, v_ml_w_in, v_ml_b_if, v_ml_head_norm, v_ml_w_out, v_lru_w_in, v_lru_conv_w, v_lru_conv_b, v_lru_w_gate_a, v_lru_b_gate_a, v_lru_w_gate_x, v_lru_b_gate_x, v_lru_lambda, v_lru_w_out, v_norm_pre_mix, v_norm_post_mix, v_norm_pre_ffn, v_norm_post_ffn, v_ffn_w_gate, v_ffn_w_up, v_ffn_w_down):
    args = locals()
    w = {n: args[n] for n in WEIGHTS}
    m = {n: args["m_" + n] for n in WEIGHTS}
    v = {n: args["v_" + n] for n in WEIGHTS}
    xs, target = x[0], loss_target[0]
    mx, my, mc, _ = _chip_peers()
    chip = 2 * mx + my
    pos = jnp.stack([mc, chip])
    row = lambda a, i: a[i:i + 1]
    npm, nqm, npf, nqf = (w[n] for n in NORMS)
    shards = _weight_shards(w)
    of = lambda group: [shards[n] for n, _, _ in GROUPS[group]]
    bif = jnp.pad(w["ml_b_if"], ((0, 0), (0, 128 - 2 * ML_HEADS)))
    hn = w["ml_head_norm"]

    small = jnp.concatenate([w["lru_conv_w"][0]] + [w[n] for n in LRU_VECTORS], axis=0)
    ml_in_spec, ml_out_spec = GROUPS["ml"][:1], GROUPS["ml"][1:]
    ml_in_parts, smalls = _gather_weights(ml_in_spec, of("ml")[:1], small)
    ml_w_in = jnp.pad(_by_cols(ml_in_parts), ((0, 0), (0, ML_IN_PAD - ML_IN)))
    vec = _by_cols(smalls)
    z0, qkv, og = _norm_matmul(xs, row(npm, 0), ml_w_in, 2 * ML_QK + ML_V, "ml_in")
    (y0, hs, cst, nst, mst), got = _mlstm_fwd(
        qkv, og, bif, hn, _gather_side(ml_out_spec + GROUPS["lru"] + GROUPS["ffn0"],
                                       of("ml")[1:] + of("lru") + of("ffn0")))
    ml_w_out = got[0].reshape(1024, 1024)
    lru_w_in = _by_cols(got[1])
    gates = got[2].reshape(N_CHIPS, 2, LRU_BLOCKS, 64, LRU_BLOCK).transpose(1, 2, 0, 3, 4)
    gates = gates.reshape(2, LRU_BLOCKS, LRU_BLOCK, LRU_BLOCK)
    lru_w_out = got[3].reshape(1024, 1024)
    wg0, wu0, w_down0 = _ffn_weights(got[4:7])
    ymix0, h1 = _matmul_postnorm(y0, ml_w_out, row(nqm, 0), xs, "ml_out")
    (zf0, gu0, yf0, h2), got = _ffn_forward(h1, row(npf, 0), wg0, wu0, w_down0, row(nqf, 0), "ffn0_fwd",
                                            _gather_side(GROUPS["ffn1"], of("ffn1")))
    wg1, wu1, w_down1 = _ffn_weights(got)
    z1, proj1 = _norm_matmul(h2, row(npm, 1), lru_w_in, 0, "lru_in")
    y1, u, r, ig, hl = _lru_fwd(proj1, vec[0:4], vec[4:5], gates[0], vec[5:6], gates[1], vec[6:7], vec[7:8])
    ymix1, h3 = _matmul_postnorm(y1, lru_w_out, row(nqm, 1), h2, "lru_out")
    (zf1, gu1, yf1, h4), _ = _ffn_forward(h3, row(npf, 1), wg1, wu1, w_down1, row(nqf, 1), "ffn1_fwd", None)
    dh4, loss_part = _loss_head(h4, target, "loss_head")

    add = lambda group, gparts, recv: [_add_halves(g, r, pos, "add_halves_" + n)
                                       for (n, _, _), g, r in zip(GROUPS[group], gparts, recv)]
    (dyf1, dgu1, act1, dh3, dqf1, dpf1), _ = _ffn_backward(
        dh4, yf1, row(nqf, 1), w_down1, gu1, wg1, wu1, h3, row(npf, 1), "ffn1_bwd", None)
    dw_down1 = _matmul_tn(act1, dyf1, True, 256, "ffn1_dw_down")
    dw_gu1 = _matmul_tn(dgu1, zf1, True, 512, "ffn1_dw_up")
    g_ffn1 = _ffn_grad_parts(dw_gu1, dw_down1)

    (dymix1, dy1, dqm1), recv = _bwd_out(dh3, ymix1, row(nqm, 1), lru_w_out, "lru_bwd_out",
                                         _exchange_side(GROUPS["ffn1"], g_ffn1))
    s1_ffn1 = add("ffn1", g_ffn1, recv)
    dw_lru_out = _matmul_tn(y1, dymix1, False, 256, "lru_dw_out")
    (dproj1, dcw, dcb, dwa, dba, dwx, dbx, dlam), recv_ffn1 = _lru_bwd(
        proj1, vec[0:4], gates[0], gates[1], vec[7:8], u, r, ig, hl, dy1, _scatter_side(s1_ffn1))
    dw_lru_in = _matmul_tn(z1, dproj1, False, 512, "lru_dw_in")
    dgates = jnp.stack([dwa, dwx]).astype(BF16).reshape(2, LRU_BLOCKS, N_CHIPS, 64, LRU_BLOCK)
    dgates = dgates.transpose(2, 0, 1, 3, 4).reshape(N_CHIPS, -1, LRU_BLOCK)
    g_lru = [_by_chip(dw_lru_in, 512), dgates, dw_lru_out.reshape(N_CHIPS, 256, 1024)]
    (dh2, dpm1), recv = _bwd_in(dproj1, [lru_w_in], False, h2, row(npm, 1), dh3, "lru_bwd_in",
                                _exchange_side(GROUPS["lru"], g_lru))
    s1_lru = add("lru", g_lru, recv)

    (dyf0, dgu0, act0, dh1, dqf0, dpf0), recv_lru = _ffn_backward(
        dh2, yf0, row(nqf, 0), w_down0, gu0, wg0, wu0, h1, row(npf, 0), "ffn0_bwd", _scatter_side(s1_lru))
    dw_down0 = _matmul_tn(act0, dyf0, True, 256, "ffn0_dw_down")
    dw_gu0 = _matmul_tn(dgu0, zf0, True, 512, "ffn0_dw_up")
    g_ffn0 = _ffn_grad_parts(dw_gu0, dw_down0)

    (dymix0, dy0, dqm0), recv = _bwd_out(dh1, ymix0, row(nqm, 0), ml_w_out, "ml_bwd_out",
                                         _exchange_side(GROUPS["ffn0"], g_ffn0))
    s1_ffn0 = add("ffn0", g_ffn0, recv)
    dw_ml_out = _matmul_tn(y0, dymix0, False, 256, "ml_dw_out")
    (dproj0, dhn, dbif), recv_ffn0 = _mlstm_bwd(qkv, og, bif, hn, hs, cst, nst, mst, dy0, _scatter_side(s1_ffn0))
    dw_ml_in = _matmul_tn(z0, dproj0, False, 640, "ml_dw_in")
    g_ml = [_by_chip(dw_ml_in[:, :ML_IN], ML_IN // N_CHIPS), dw_ml_out.reshape(N_CHIPS, 256, 1024)]
    s1_ml = add("ml", g_ml, _exchange_halves(GROUPS["ml"], g_ml, "exchange_halves_ml"))
    (dx, dpm0), recv_ml = _bwd_in(dproj0, [ml_w_in], False, xs, row(npm, 0), dh1, "ml_bwd_in", _scatter_side(s1_ml))

    pad_lanes = lambda a: jnp.pad(a, ((0, 0), (0, LANES - a.shape[1])))
    small = jnp.concatenate(
        [jnp.concatenate([dpm0, dpm1]), jnp.concatenate([dqm0, dqm1]), jnp.concatenate([dpf0, dpf1]),
         jnp.concatenate([dqf0, dqf1]), pad_lanes(loss_part), dhn, pad_lanes(dbif), dcw, dcb, dba, dbx, dlam,
         jnp.zeros((SMALL_ROWS - 19, LANES), F32)], axis=0)
    small_all = _share_small(small)
    order = ("ml", "lru", "ffn0", "ffn1")
    s2 = (_sum_group("ml", s1_ml, recv_ml, pos) + _sum_group("lru", s1_lru, recv_lru, pos)
          + _sum_group("ffn0", s1_ffn0, recv_ffn0, pos) + _sum_group("ffn1", s1_ffn1, recv_ffn1, pos))
    specs = sum((GROUPS[g] for g in order), ())
    red = dict(zip([n for n, _, _ in specs], _join_halves(specs, s2)))
    vsum = _sum_small(small_all)

    dgates = red["lru_gates"].reshape(2, LRU_BLOCKS, 64, LRU_BLOCK)
    cols = lambda a: lax.dynamic_slice_in_dim(a, chip * 256, 256, axis=1)
    grads = dict(
        ml_w_in=red["ml_w_in"], ml_w_out=red["ml_w_out"], lru_w_in=red["lru_w_in"], lru_w_gate_a=dgates[0],
        lru_w_gate_x=dgates[1], lru_w_out=red["lru_w_out"],
        ffn_w_gate=jnp.swapaxes(jnp.stack([red["ffn_g0"], red["ffn_g1"]]), 1, 2),
        ffn_w_up=jnp.swapaxes(jnp.stack([red["ffn_u0"], red["ffn_u1"]]), 1, 2),
        ffn_w_down=jnp.stack([red["ffn_down0"], red["ffn_down1"]]),
        ml_head_norm=vsum[ROW_HEAD_NORM:ROW_HEAD_NORM + 1], ml_b_if=vsum[ROW_B_IF:ROW_B_IF + 1, :2 * ML_HEADS],
        lru_conv_w=cols(vsum[ROW_LRU:ROW_LRU + 4]))
    for i, n in enumerate(NORMS):
        grads[n] = vsum[2 * i:2 * i + 2]
    for i, n in enumerate(LRU_VECTORS):
        grads[n] = cols(vsum[ROW_LRU + 4 + i:ROW_LRU + 5 + i])
    loss = vsum[ROW_LOSS, 0]
    grads = {n: grads[n].reshape(w[n].shape) for n in WEIGHTS}
    delta, new_m, new_v = _update(w, grads, m, v)
    return (loss, dx[None], *[grads[n] for n in WEIGHTS], *[delta[n] for n in WEIGHTS],
            *[new_m[n] for n in WEIGHTS], *[new_v[n] for n in WEIGHTS])
```

```python
import functools
import math
from typing import Callable, NamedTuple

import jax
import jax.numpy as jnp
from jax import lax
from jax.experimental import pallas as pl
from jax.experimental.pallas import tpu as pltpu

F32 = jnp.float32
BF16 = jnp.bfloat16
MESH = pl.DeviceIdType.MESH

D_MODEL = 1024
D_FF = 2816
ML_HEADS = 8
ML_DK = 64
ML_DV = 128
ML_QK = ML_HEADS * ML_DK
ML_V = ML_HEADS * ML_DV
ML_IN = 2 * ML_QK + 2 * ML_V + 2 * ML_HEADS
ML_IN_PAD = 3200
CHUNK = 64
GATE_CAP = 15.0
ML_M_INIT = -1e30
NEG_BIG = -1e30
LRU_BLOCKS = 4
LRU_BLOCK = 256
CONV_WIDTH = 4
LRU_C = 8.0
EPS = 1e-6
QK_SCALE = ML_DK ** -0.5

ADAM_LR = 0.001
ADAM_B1 = 0.9
ADAM_B2 = 0.999
ADAM_EPS = 1e-08
ADAM_WD = 0.01
ADAM_STEP = 10

N_CHIPS = 4
V7X_VMEM_LIMIT = 56 * 1024 * 1024

NT_DIMS = (((1,), (1,)), ((), ()))
TN_DIMS = (((0,), (0,)), ((), ()))


def _params(*semantics):
    return pltpu.CompilerParams(dimension_semantics=semantics, vmem_limit_bytes=V7X_VMEM_LIMIT)


def _sds(shape, dtype):
    return jax.ShapeDtypeStruct(shape, dtype)


def _full(shape):
    return pl.BlockSpec(shape, lambda *_: (0,) * len(shape))


def _rows(tm, n):
    return pl.BlockSpec((tm, n), lambda i: (i, 0))


def _sigmoid(x):
    return 1.0 / (1.0 + jnp.exp(-x))


def _log_sigmoid(x):
    return jnp.minimum(x, 0.0) - jnp.log1p(jnp.exp(-jnp.abs(x)))


def _rms(x):
    r = lax.rsqrt(jnp.mean(x * x, axis=-1, keepdims=True) + EPS)
    return x * r, r


def _rms_bwd(xhat, r, g, dy):
    dxh = dy * g
    return r * (dxh - xhat * jnp.mean(dxh * xhat, axis=-1, keepdims=True))


def _scan_rows(a, b, reverse=False):
    n = a.shape[0]
    row = lax.broadcasted_iota(jnp.int32, a.shape, 0)
    s = 1
    while s < n:
        if reverse:
            keep = row < n - s
            shift = n - s
        else:
            keep = row >= s
            shift = s
        b = b + a * jnp.where(keep, pltpu.roll(b, shift, 0), 0.0)
        a = a * jnp.where(keep, pltpu.roll(a, shift, 0), 1.0)
        s *= 2
    return a, b


def _cumsum_rows(x, reverse=False):
    n = x.shape[0]
    row = lax.broadcasted_iota(jnp.int32, x.shape, 0)
    s = 1
    while s < n:
        if reverse:
            x = x + jnp.where(row < n - s, pltpu.roll(x, n - s, 0), 0.0)
        else:
            x = x + jnp.where(row >= s, pltpu.roll(x, s, 0), 0.0)
        s *= 2
    return x


def _norm_matmul(h, g, w, n_bf16, name, side):
    t, d = h.shape
    n = w.shape[1]
    tm = min(512, t)

    def body(h_ref, g_ref, w_ref, z_ref, *o_refs):
        xhat, _ = _rms(h_ref[...])
        z = (xhat * g_ref[...]).astype(BF16)
        z_ref[...] = z
        out = jnp.dot(z, w_ref[...], preferred_element_type=F32)
        if n_bf16:
            o_refs[0][...] = out[:, :n_bf16].astype(BF16)
            o_refs[1][...] = out[:, n_bf16:]
        else:
            o_refs[0][...] = out

    if n_bf16:
        out_specs = [_rows(tm, d), _rows(tm, n_bf16), _rows(tm, n - n_bf16)]
        out_shape = [_sds((t, d), BF16), _sds((t, n_bf16), BF16), _sds((t, n - n_bf16), F32)]
    else:
        out_specs = [_rows(tm, d), _rows(tm, n)]
        out_shape = [_sds((t, d), BF16), _sds((t, n), F32)]
    return _side_call(body, side, name=name, steps=t // tm, in_specs=[_rows(tm, d), _full((1, d)), _full((d, n))],
                      out_specs=out_specs, out_shape=out_shape, scratch_shapes=[], args=(h, g, w))


def _matmul_postnorm(a, w, g, res, name):
    t = res.shape[0]
    k, d = w.shape
    tm = 256

    def body(a_ref, w_ref, g_ref, res_ref, y_ref, o_ref):
        y = jnp.dot(a_ref[...], w_ref[...], preferred_element_type=F32)
        y_ref[...] = y
        yhat, _ = _rms(y)
        o_ref[...] = res_ref[...] + yhat * g_ref[...]

    return pl.pallas_call(
        body, name=name, grid=(t // tm,),
        in_specs=[_rows(tm, k), _full((k, d)), _full((1, d)), _rows(tm, d)],
        out_specs=[_rows(tm, d), _rows(tm, d)],
        out_shape=[_sds((t, d), F32), _sds((t, d), F32)],
        compiler_params=_params("parallel"),
    )(a, w, g, res)


class _Side(NamedTuple):
    inputs: list
    out_shape: list
    n_sems: int
    start: Callable
    finish: Callable


def _side_call(body, side, *, name, steps, in_specs, out_specs, out_shape, scratch_shapes, args):
    n_in, n_out, n_scr = len(in_specs), len(out_specs), len(scratch_shapes)
    if side is None:
        outs = pl.pallas_call(
            body, name=name, grid=(steps,), in_specs=in_specs, out_specs=out_specs, out_shape=out_shape,
            scratch_shapes=scratch_shapes, compiler_params=_params("arbitrary"))(*args)
        return list(outs), []
    s_in, s_out = len(side.inputs), len(side.out_shape)

    def carrying(*refs):
        ins, side_ins = refs[:n_in], refs[n_in:n_in + s_in]
        outs = refs[n_in + s_in:n_in + s_in + n_out]
        side_outs = refs[n_in + s_in + n_out:n_in + s_in + n_out + s_out]
        scratch = refs[n_in + s_in + n_out + s_out:]
        own, sems = scratch[:n_scr], scratch[n_scr:]

        @pl.when(pl.program_id(0) == 0)
        def _():
            side.start(side_ins, side_outs, *sems)

        body(*ins, *outs, *own)

        @pl.when(pl.program_id(0) == steps - 1)
        def _():
            side.finish(side_ins, side_outs, *sems)

    outs = pl.pallas_call(
        carrying, name=name, grid=(steps,), in_specs=list(in_specs) + [HBM_SPEC] * s_in,
        out_specs=list(out_specs) + [HBM_SPEC] * s_out, out_shape=list(out_shape) + list(side.out_shape),
        scratch_shapes=list(scratch_shapes) + [pltpu.SemaphoreType.DMA((side.n_sems,))] * 2,
        compiler_params=_params("arbitrary"))(*args, *side.inputs)
    return list(outs[:n_out]), list(outs[n_out:])


def _ffn_forward(h, g_pre, wg_t, wu_t, w_down, g_post, name, side):
    t, d = h.shape
    k = w_down.shape[0]
    tm = 256

    def body(h_ref, gpre_ref, wg_ref, wu_ref, wd_ref, gpost_ref, z_ref, gu_ref, y_ref, o_ref):
        hv = h_ref[...]
        xhat, _ = _rms(hv)
        z = (xhat * gpre_ref[...]).astype(BF16)
        z_ref[...] = z
        gate = lax.dot_general(z, wg_ref[...], NT_DIMS, preferred_element_type=F32).astype(BF16)
        up = lax.dot_general(z, wu_ref[...], NT_DIMS, preferred_element_type=F32).astype(BF16)
        gu_ref[:, :k] = gate
        gu_ref[:, k:] = up
        gate = gate.astype(F32)
        act = (gate * _sigmoid(gate) * up.astype(F32)).astype(BF16)
        y = jnp.dot(act, wd_ref[...], preferred_element_type=F32)
        y_ref[...] = y
        yhat, _ = _rms(y)
        o_ref[...] = hv + yhat * gpost_ref[...]

    return _side_call(
        body, side, name=name, steps=t // tm,
        in_specs=[_rows(tm, d), _full((1, d)), _full((k, d)), _full((k, d)), _full((k, d)), _full((1, d))],
        out_specs=[_rows(tm, d), _rows(tm, 2 * k), _rows(tm, d), _rows(tm, d)],
        out_shape=[_sds((t, d), BF16), _sds((t, 2 * k), BF16), _sds((t, d), F32), _sds((t, d), F32)],
        scratch_shapes=[], args=(h, g_pre, wg_t, wu_t, w_down, g_post))


def _loss_head(h, target, name):
    t, d = h.shape
    tm = min(512, t)

    def body(h_ref, t_ref, dh_ref, l_ref):
        @pl.when(pl.program_id(0) == 0)
        def _():
            l_ref[...] = jnp.zeros_like(l_ref)

        err = h_ref[...] - t_ref[...]
        dh_ref[...] = err * (1.0 / d)
        part = jnp.sum(jnp.sum(err * err, axis=1, keepdims=True), axis=0, keepdims=True) * (0.5 / d)
        l_ref[...] += jnp.broadcast_to(part, l_ref.shape)

    return pl.pallas_call(
        body, name=name, grid=(t // tm,),
        in_specs=[_rows(tm, d), _rows(tm, d)],
        out_specs=[_rows(tm, d), _full((1, 128))],
        out_shape=[_sds((t, d), F32), _sds((1, 128), F32)],
        compiler_params=_params("arbitrary"),
    )(h, target)


def _bwd_out(dout, y, g, w, name, side):
    t, d = dout.shape
    k = w.shape[0]
    tm = 256

    def body(dout_ref, y_ref, g_ref, w_ref, dy_ref, da_ref, dg_ref):
        @pl.when(pl.program_id(0) == 0)
        def _():
            dg_ref[...] = jnp.zeros_like(dg_ref)

        do = dout_ref[...]
        yhat, r = _rms(y_ref[...])
        dg_ref[...] += jnp.sum(do * yhat, axis=0, keepdims=True)
        dy = _rms_bwd(yhat, r, g_ref[...], do).astype(BF16)
        dy_ref[...] = dy
        da_ref[...] = lax.dot_general(dy, w_ref[...], NT_DIMS, preferred_element_type=F32)

    return _side_call(
        body, side, name=name, steps=t // tm,
        in_specs=[_rows(tm, d), _rows(tm, d), _full((1, d)), _full((k, d))],
        out_specs=[_rows(tm, d), _rows(tm, k), _full((1, d))],
        out_shape=[_sds((t, d), BF16), _sds((t, k), F32), _sds((1, d), F32)],
        scratch_shapes=[], args=(dout, y, g, w))


def _ffn_backward(dout, y, g_post, w_down, gu, wg_t, wu_t, h, g_pre, name, side):
    t, d = dout.shape
    k = w_down.shape[0]
    tm = 256

    def body(dout_ref, y_ref, gpost_ref, wd_ref, gu_ref, wg_ref, wu_ref, h_ref, gpre_ref,
             dy_ref, dgu_ref, act_ref, dh_ref, dgpost_ref, dgpre_ref):
        @pl.when(pl.program_id(0) == 0)
        def _():
            dgpost_ref[...] = jnp.zeros_like(dgpost_ref)
            dgpre_ref[...] = jnp.zeros_like(dgpre_ref)

        do = dout_ref[...]
        yhat, r = _rms(y_ref[...])
        dgpost_ref[...] += jnp.sum(do * yhat, axis=0, keepdims=True)
        dy = _rms_bwd(yhat, r, gpost_ref[...], do).astype(BF16)
        dy_ref[...] = dy
        da = lax.dot_general(dy, wd_ref[...], NT_DIMS, preferred_element_type=F32)
        gate = gu_ref[:, :k].astype(F32)
        up = gu_ref[:, k:].astype(F32)
        sg = _sigmoid(gate)
        silu = gate * sg
        act_ref[...] = (silu * up).astype(BF16)
        dgate = (da * up * (sg * (1.0 + gate * (1.0 - sg)))).astype(BF16)
        dup = (da * silu).astype(BF16)
        dgu_ref[:, :k] = dgate
        dgu_ref[:, k:] = dup
        dz = (jnp.dot(dgate, wg_ref[...], preferred_element_type=F32)
              + jnp.dot(dup, wu_ref[...], preferred_element_type=F32))
        hhat, r2 = _rms(h_ref[...])
        dgpre_ref[...] += jnp.sum(dz * hhat, axis=0, keepdims=True)
        dh_ref[...] = do + _rms_bwd(hhat, r2, gpre_ref[...], dz)

    vec = _full((1, d))
    wspec = _full((k, d))
    return _side_call(
        body, side, name=name, steps=t // tm,
        in_specs=[_rows(tm, d), _rows(tm, d), vec, wspec, _rows(tm, 2 * k), wspec, wspec, _rows(tm, d), vec],
        out_specs=[_rows(tm, d), _rows(tm, 2 * k), _rows(tm, k), _rows(tm, d), vec, vec],
        out_shape=[_sds((t, d), BF16), _sds((t, 2 * k), BF16), _sds((t, k), BF16), _sds((t, d), F32),
                   _sds((1, d), F32), _sds((1, d), F32)],
        scratch_shapes=[], args=(dout, y, g_post, w_down, gu, wg_t, wu_t, h, g_pre))


def _bwd_in(dp, ws, transposed, h, g, dout, name, side):
    t, d = h.shape
    n = dp.shape[1]
    tm = 256
    widths = [w.shape[0] if transposed else w.shape[1] for w in ws]

    def body(dp_ref, *refs):
        w_refs = refs[:len(ws)]
        h_ref, g_ref, dout_ref, dh_ref, dg_ref = refs[len(ws):]

        @pl.when(pl.program_id(0) == 0)
        def _():
            dg_ref[...] = jnp.zeros_like(dg_ref)

        dz, at = None, 0
        for w_ref, width in zip(w_refs, widths):
            block = dp_ref[:, at:at + width]
            if transposed:
                part = jnp.dot(block, w_ref[...], preferred_element_type=F32)
            else:
                part = lax.dot_general(block, w_ref[...], NT_DIMS, preferred_element_type=F32)
            dz = part if dz is None else dz + part
            at += width
        hhat, r = _rms(h_ref[...])
        dg_ref[...] += jnp.sum(dz * hhat, axis=0, keepdims=True)
        dh_ref[...] = dout_ref[...] + _rms_bwd(hhat, r, g_ref[...], dz)

    return _side_call(
        body, side, name=name, steps=t // tm,
        in_specs=[_rows(tm, n)] + [_full(w.shape) for w in ws] + [_rows(tm, d), _full((1, d)), _rows(tm, d)],
        out_specs=[_rows(tm, d), _full((1, d))],
        out_shape=[_sds((t, d), F32), _sds((1, d), F32)],
        scratch_shapes=[], args=(dp, *ws, h, g, dout))


def _matmul_tn(a, b, tile_a, tile, name, window=None):
    t, ka = a.shape
    nb = b.shape[1]
    col0 = 0
    if window is not None:
        col0, ka = window

    def body(a_ref, b_ref, o_ref):
        o_ref[...] = lax.dot_general(a_ref[...], b_ref[...], TN_DIMS, preferred_element_type=F32).astype(BF16)

    if tile_a:
        grid = (ka // tile,)
        in_specs = [pl.BlockSpec((t, tile), lambda i: (0, col0 // tile + i)), _full((t, nb))]
        out_specs = pl.BlockSpec((tile, nb), lambda i: (i, 0))
    else:
        grid = (nb // tile,)
        in_specs = [_full((t, ka)), pl.BlockSpec((t, tile), lambda i: (0, i))]
        out_specs = pl.BlockSpec((ka, tile), lambda i: (0, i))
    return pl.pallas_call(
        body, name=name, grid=grid, in_specs=in_specs, out_specs=out_specs,
        out_shape=_sds((ka, nb), BF16), compiler_params=_params("parallel"),
    )(a, b)


def _ml_gate_prep(gt, bif):
    th = jnp.tanh((gt + bif) / GATE_CAP)
    act = GATE_CAP * th
    cum = _cumsum_rows(_log_sigmoid(act))
    lane = lax.broadcasted_iota(jnp.int32, gt.shape, 1)
    x = jnp.where(lane < ML_HEADS, act, cum)
    return x, x.T, th, act


HEADS = range(ML_HEADS)


def _each(fn, *per_head):
    return [fn(*a) for a in zip(*per_head)]


def _ml_chunk_fwd(q, k, v, kt, x, xt, c_in, n_in, m_in):
    causal = (lax.broadcasted_iota(jnp.int32, (CHUNK, CHUNK), 0)
              >= lax.broadcasted_iota(jnp.int32, (CHUNK, CHUNK), 1))
    f = {}
    qh = f["qh"] = [q[:, ML_DK * h:ML_DK * (h + 1)] for h in HEADS]
    kh = f["kh"] = [k[:, ML_DK * h:ML_DK * (h + 1)] for h in HEADS]
    f["vh"] = [v[:, ML_DV * h:ML_DV * (h + 1)] for h in HEADS]
    kth = [kt[ML_DK * h:ML_DK * (h + 1), :] for h in HEADS]
    s = _each(lambda a, b: jnp.dot(a, b, preferred_element_type=F32) * QK_SCALE, qh, kth)
    f["qc"] = _each(lambda a, c: jnp.dot(a, c.astype(BF16), preferred_element_type=F32) * QK_SCALE, qh, c_in)
    bcol = f["bcol"] = [x[:, ML_HEADS + h:ML_HEADS + h + 1] for h in HEADS]
    licol = f["licol"] = [x[:, h:h + 1] for h in HEADS]
    brow = f["brow"] = [xt[ML_HEADS + h:ML_HEADS + h + 1, :] for h in HEADS]
    lirow = [xt[h:h + 1, :] for h in HEADS]
    dmat = _each(lambda bc, br, lr: jnp.where(causal, bc - br + lr, NEG_BIG), bcol, brow, lirow)
    inter = _each(lambda bc, m: bc + m, bcol, m_in)
    mt = _each(lambda d, i: jnp.maximum(jnp.max(d, axis=1, keepdims=True), i), dmat, inter)
    wt = f["wt"] = _each(lambda d, m: jnp.exp(d - m), dmat, mt)
    p = f["p"] = _each(lambda a, b: a * b, wt, s)
    winter = f["winter"] = _each(lambda i, m: jnp.exp(i - m), inter, mt)
    qf = f["qf"] = [a.astype(F32) for a in qh]
    qn = f["qn"] = _each(lambda a, n: jnp.sum(a * n, axis=1, keepdims=True) * QK_SCALE, qf, n_in)
    den = f["den"] = _each(lambda a, w, b: jnp.sum(a, axis=1, keepdims=True) + w * b, p, winter, qn)
    emt = f["emt"] = [jnp.exp(-m) for m in mt]
    f["nrm"] = _each(lambda d, e: jnp.maximum(jnp.abs(d), e), den, emt)
    gtot = [bc[CHUNK - 1:CHUNK, :] for bc in bcol]
    a_col = _each(lambda g, bc, lc: g - bc + lc, gtot, bcol, licol)
    a_row = _each(lambda g, br, lr: g - br + lr, gtot, brow, lirow)
    m_new = f["m_new"] = _each(lambda g, m, a: jnp.maximum(g + m, jnp.max(a, axis=1, keepdims=True)),
                               gtot, m_in, a_row)
    f["decay"] = _each(lambda g, m, mn: jnp.exp(g + m - mn), gtot, m_in, m_new)
    wkf = f["wkf"] = _each(lambda a, mn: jnp.exp(a - mn), a_col, m_new)
    f["kw"] = _each(lambda a, w: a.astype(F32) * w, kh, wkf)
    f["ktw"] = _each(lambda a, ar, mn: (a.astype(F32) * jnp.exp(ar - mn)).astype(BF16), kth, a_row, m_new)
    return f


def _ml_specs(nc, rev):
    def at(col):
        if rev:
            return lambda i: (nc - 1 - i, col)
        return lambda i: (i, col)

    return [pl.BlockSpec((CHUNK, ML_QK), at(0)), pl.BlockSpec((CHUNK, ML_QK), at(1)),
            pl.BlockSpec((CHUNK, ML_V), at(1)), pl.BlockSpec((CHUNK, ML_V), at(0)),
            pl.BlockSpec((CHUNK, 128), at(ML_V // 128))]


def _mlstm_fwd(qkv, og, bif, hn, side):
    t = qkv.shape[0]
    nc = t // CHUNK

    def body(q_ref, k_ref, v_ref, o_ref, gt_ref, bif_ref, hn_ref, y_ref, hs_ref, cst_ref, nst_ref, mst_ref,
             c_sc, n_sc, m_sc):
        @pl.when(pl.program_id(0) == 0)
        def _():
            c_sc[...] = jnp.zeros_like(c_sc)
            n_sc[...] = jnp.zeros_like(n_sc)
            m_sc[...] = jnp.full_like(m_sc, ML_M_INIT)

        cst_ref[0] = c_sc[...]
        nst_ref[0] = n_sc[...]
        mst_ref[0] = m_sc[...]
        x, xt, _, _ = _ml_gate_prep(gt_ref[...], bif_ref[...])
        q, k, v = q_ref[...], k_ref[...], v_ref[...]
        c_all, n_all, m_all = c_sc[...], n_sc[...], m_sc[...]
        c_in = [c_all[ML_DK * h:ML_DK * (h + 1), :] for h in HEADS]
        n_in = [n_all[h:h + 1, :] for h in HEADS]
        m_in = [m_all[h:h + 1, 0:1] for h in HEADS]
        f = _ml_chunk_fwd(q, k, v, k.T, x, xt, c_in, n_in, m_in)
        num = _each(lambda p, vh, w, qc: jnp.dot(p.astype(BF16), vh, preferred_element_type=F32) + w * qc,
                    f["p"], f["vh"], f["winter"], f["qc"])
        hh = _each(lambda a, b: a / b, num, f["nrm"])
        hhat = [_rms(a)[0] for a in hh]
        c_new = _each(lambda d, c, kw, vh: d * c + jnp.dot(kw, vh, preferred_element_type=F32),
                      f["decay"], c_in, f["ktw"], f["vh"])
        n_new = _each(lambda d, n, kw: d * n + jnp.sum(kw, axis=0, keepdims=True), f["decay"], n_in, f["kw"])
        for h in HEADS:
            vs = slice(ML_DV * h, ML_DV * (h + 1))
            hs_ref[:, vs] = hh[h]
            y_ref[:, vs] = (hhat[h] * hn_ref[:, vs] * _sigmoid(o_ref[:, vs])).astype(BF16)
            c_sc[ML_DK * h:ML_DK * (h + 1), :] = c_new[h]
            n_sc[h:h + 1, :] = n_new[h]
            m_sc[h:h + 1, :] = jnp.broadcast_to(f["m_new"][h], (1, 128))

    return _side_call(
        body, side, name="mlstm_fwd", steps=nc,
        in_specs=_ml_specs(nc, False) + [_full((1, 128)), _full((1, ML_V))],
        out_specs=[_rows(CHUNK, ML_V), _rows(CHUNK, ML_V),
                   pl.BlockSpec((1, ML_HEADS * ML_DK, ML_DV), lambda i: (i, 0, 0)),
                   pl.BlockSpec((1, ML_HEADS, ML_DK), lambda i: (i, 0, 0)),
                   pl.BlockSpec((1, ML_HEADS, 128), lambda i: (i, 0, 0))],
        out_shape=[_sds((t, ML_V), BF16), _sds((t, ML_V), F32),
                   _sds((nc, ML_HEADS * ML_DK, ML_DV), F32), _sds((nc, ML_HEADS, ML_DK), F32),
                   _sds((nc, ML_HEADS, 128), F32)],
        scratch_shapes=[pltpu.VMEM((ML_HEADS * ML_DK, ML_DV), F32), pltpu.VMEM((ML_HEADS, ML_DK), F32),
                        pltpu.VMEM((ML_HEADS, 128), F32)],
        args=(qkv, qkv, qkv, og, og, bif, hn))


def _mlstm_bwd(qkv, og, bif, hn, hs, cst, nst, mst, dy, side):
    t = qkv.shape[0]
    nc = t // CHUNK

    def body(q_ref, k_ref, v_ref, o_ref, gt_ref, bif_ref, hn_ref, hs_ref, cst_ref, nst_ref, mst_ref, dy_ref,
             dp_ref, dhn_ref, dbif_ref, dc_sc, dct_sc, dn_sc):
        @pl.when(pl.program_id(0) == 0)
        def _():
            dc_sc[...] = jnp.zeros_like(dc_sc)
            dct_sc[...] = jnp.zeros_like(dct_sc)
            dn_sc[...] = jnp.zeros_like(dn_sc)
            dhn_ref[...] = jnp.zeros_like(dhn_ref)
            dbif_ref[...] = jnp.zeros_like(dbif_ref)

        x, xt, th, act = _ml_gate_prep(gt_ref[...], bif_ref[...])
        q, k, v = q_ref[...], k_ref[...], v_ref[...]
        qt, vt = q.T, v.T
        rows = lax.broadcasted_iota(jnp.int32, (CHUNK, CHUNK), 0)
        cols = lax.broadcasted_iota(jnp.int32, (CHUNK, CHUNK), 1)
        lane = lax.broadcasted_iota(jnp.int32, (CHUNK, 128), 1)
        row = lax.broadcasted_iota(jnp.int32, (CHUNK, 1), 0)
        as_row = lambda col: jnp.sum(jnp.where(rows == cols, col, 0.0), axis=0, keepdims=True)
        mm = lambda a, b: jnp.dot(a, b, preferred_element_type=F32)
        bf = lambda a: a.astype(BF16)
        ksl = [slice(ML_DK * h, ML_DK * (h + 1)) for h in HEADS]
        vsl = [slice(ML_DV * h, ML_DV * (h + 1)) for h in HEADS]
        c_in = [cst_ref[0, s, :] for s in ksl]
        n_in = [nst_ref[0, h:h + 1, :] for h in HEADS]
        m_in = [mst_ref[0, h:h + 1, 0:1] for h in HEADS]
        dc_all, dct_all, dn_all = dc_sc[...], dct_sc[...], dn_sc[...]
        dcn = [dc_all[s, :] for s in ksl]
        dcn_t = [dct_all[s, :] for s in vsl]
        dnn = [dn_all[h:h + 1, :] for h in HEADS]
        f = _ml_chunk_fwd(q, k, v, k.T, x, xt, c_in, n_in, m_in)
        qh, kh, vh, p, winter, decay = f["qh"], f["kh"], f["vh"], f["p"], f["winter"], f["decay"]
        qth = [qt[s, :] for s in ksl]
        vth = [vt[s, :] for s in vsl]
        c_t = [bf(c.T) for c in c_in]
        dmat_t = _each(lambda br, bc, lc: jnp.where(rows <= cols, br - bc + lc, NEG_BIG),
                       f["brow"], f["bcol"], f["licol"])
        inter_row = _each(lambda br, m: br + m, f["brow"], m_in)
        mt_row = _each(lambda d, i: jnp.maximum(jnp.max(d, axis=0, keepdims=True), i), dmat_t, inter_row)
        wt_t = _each(lambda d, m: jnp.exp(d - m), dmat_t, mt_row)
        p_t = _each(lambda w, a, b: w * (mm(a, b) * QK_SCALE), wt_t, kh, qth)
        winter_row = _each(lambda i, m: jnp.exp(i - m), inter_row, mt_row)
        hh = [hs_ref[:, s] for s in vsl]
        hn_h = [hn_ref[:, s] for s in vsl]
        sg = [_sigmoid(o_ref[:, s]) for s in vsl]
        dyh = [dy_ref[:, s] for s in vsl]
        norm = [_rms(a) for a in hh]
        hhat, r = [a for a, _ in norm], [b for _, b in norm]
        dyn = _each(lambda a, b: a * b, dyh, sg)
        do = _each(lambda d, hx, g, s: d * hx * g * s * (1.0 - s), dyh, hhat, hn_h, sg)
        dhn = _each(lambda a, b: jnp.sum(a * b, axis=0, keepdims=True), dyn, hhat)
        dh = _each(_rms_bwd, hhat, r, hn_h, dyn)
        inv = [1.0 / a for a in f["nrm"]]
        dnum = _each(lambda a, b: a * b, dh, inv)
        dnrm = _each(lambda a, b, c: -jnp.sum(a * b, axis=1, keepdims=True) * c, dh, hh, inv)
        dden = _each(lambda d, e, g: jnp.where(jnp.abs(d) > e, g * jnp.sign(d), 0.0), f["den"], f["emt"], dnrm)
        dnb = [bf(a) for a in dnum]
        dnt = [bf(a.T) for a in dnum]
        rmat = _each(lambda a, b, d: mm(a, b) + d, dnb, vth, dden)
        rmat_t = _each(lambda a, b, d: mm(a, b) + as_row(d), vh, dnt, dden)
        ds = _each(lambda w, a: bf(w * a), f["wt"], rmat)
        ds_t = _each(lambda w, a: bf(w * a), wt_t, rmat_t)
        dv = _each(lambda a, b: mm(bf(a), b), p_t, dnb)
        dqs = _each(lambda s, kk, w, d, ct, dd, n: mm(s, kk) + w * (mm(d, ct) + dd * n),
                    ds, kh, winter, dnb, c_t, dden, n_in)
        dk = _each(lambda s, a: mm(s, a) * QK_SCALE, ds_t, qh)
        dinter = _each(lambda qc, dn_, qn, dd, w: (jnp.sum(qc * dn_, axis=1, keepdims=True) + qn * dd) * w,
                       f["qc"], dnum, f["qn"], dden, winter)
        wq = _each(lambda w, a: w * a * QK_SCALE, winter, f["qf"])
        wq_t = _each(lambda w, a: bf(w * a.astype(F32) * QK_SCALE), winter_row, qth)
        dc_loc = _each(mm, wq_t, dnb)
        dct_loc = _each(lambda a, b: mm(a, bf(b)), dnt, wq)
        dn_loc = _each(lambda a, d: jnp.sum(a * d, axis=0, keepdims=True), wq, dden)
        cs_q = _each(lambda a, b: jnp.sum(a * b, axis=1, keepdims=True), p_t, rmat_t)
        db = _each(lambda a, b, di, cs: jnp.sum(a * b, axis=1, keepdims=True) + di - cs, p, rmat, dinter, cs_q)
        ddecay = _each(lambda dc_, c, dn_, n: jnp.sum(jnp.sum(dc_ * c, axis=1, keepdims=True), axis=0, keepdims=True)
                       + jnp.sum(dn_ * n, axis=1, keepdims=True), dcn, c_in, dnn, n_in)
        dkw = _each(lambda a, b, n: mm(a, bf(b)) + n, vh, dcn_t, dnn)
        dk = _each(lambda a, w, b: a + w * b, dk, f["wkf"], dkw)
        da = _each(lambda a, kk, w: jnp.sum(a * kk.astype(F32), axis=1, keepdims=True) * w, dkw, kh, f["wkf"])
        dv = _each(lambda a, kw, dc_: a + mm(bf(kw), bf(dc_)), dv, f["kw"], dcn)
        dgtot = _each(lambda a, dd, d: jnp.sum(a, axis=0, keepdims=True) + dd * d, da, ddecay, decay)
        db = _each(lambda a, b, g: a - b + jnp.where(row == CHUNK - 1, g, 0.0), db, da, dgtot)
        dli = _each(lambda a, b: a + b, cs_q, da)
        dc_new = _each(lambda d, a, b: d * a + b, decay, dcn, dc_loc)
        dct_new = _each(lambda d, a, b: d * a + b, decay, dcn_t, dct_loc)
        dn_new = _each(lambda d, a, b: d * a + b, decay, dnn, dn_loc)
        dx = jnp.zeros((CHUNK, 128), F32)
        for h in HEADS:
            dhn_ref[:, vsl[h]] += dhn[h]
            dc_sc[ksl[h], :] = dc_new[h]
            dct_sc[vsl[h], :] = dct_new[h]
            dn_sc[h:h + 1, :] = dn_new[h]
            dp_ref[:, ksl[h]] = bf(dqs[h] * QK_SCALE)
            dp_ref[:, ML_QK + ML_DK * h:ML_QK + ML_DK * (h + 1)] = bf(dk[h])
            dp_ref[:, 2 * ML_QK + ML_DV * h:2 * ML_QK + ML_DV * (h + 1)] = bf(dv[h])
            dp_ref[:, 2 * ML_QK + ML_V + ML_DV * h:2 * ML_QK + ML_V + ML_DV * (h + 1)] = bf(do[h])
            dx = jnp.where(lane == h, dli[h], dx)
            dx = jnp.where(lane == ML_HEADS + h, db[h], dx)
        dlf = _cumsum_rows(dx, reverse=True)
        dact = jnp.where(lane < ML_HEADS, dx, dlf * _sigmoid(-act))
        dz = dact * (1.0 - th * th)
        dp_ref[:, 2 * ML_QK + 2 * ML_V:] = dz.astype(BF16)
        dbif_ref[...] += jnp.sum(dz, axis=0, keepdims=True)

    rev = lambda i: (nc - 1 - i, 0)
    rev3 = lambda i: (nc - 1 - i, 0, 0)
    return _side_call(
        body, side, name="mlstm_bwd", steps=nc,
        in_specs=_ml_specs(nc, True) + [
            _full((1, 128)), _full((1, ML_V)), pl.BlockSpec((CHUNK, ML_V), rev),
            pl.BlockSpec((1, ML_HEADS * ML_DK, ML_DV), rev3), pl.BlockSpec((1, ML_HEADS, ML_DK), rev3),
            pl.BlockSpec((1, ML_HEADS, 128), rev3), pl.BlockSpec((CHUNK, ML_V), rev)],
        out_specs=[pl.BlockSpec((CHUNK, ML_IN_PAD), rev), _full((1, ML_V)), _full((1, 128))],
        out_shape=[_sds((t, ML_IN_PAD), BF16), _sds((1, ML_V), F32), _sds((1, 128), F32)],
        scratch_shapes=[pltpu.VMEM((ML_HEADS * ML_DK, ML_DV), F32), pltpu.VMEM((ML_HEADS * ML_DV, ML_DK), F32),
                        pltpu.VMEM((ML_HEADS, ML_DK), F32)],
        args=(qkv, qkv, qkv, og, og, bif, hn, hs, cst, nst, mst, dy))


LRU_TM = 256
GELU_K = math.sqrt(2.0 / math.pi)
GELU_C = 0.044715


def _gelu(x):
    th = jnp.tanh(GELU_K * (x + GELU_C * x * x * x))
    return 0.5 * x * (1.0 + th), th


def _neg_expm1(x):
    series = -x * (1.0 + x * (0.5 + x * (1.0 / 6.0 + x * (1.0 / 24.0))))
    return jnp.where(x > -0.05, series, 1.0 - jnp.exp(x))


def _block_diag_dot(a, w_ref, dims):
    parts = [lax.dot_general(a[:, LRU_BLOCK * n:LRU_BLOCK * (n + 1)], w_ref[n], dims, preferred_element_type=F32)
             for n in range(LRU_BLOCKS)]
    return jnp.concatenate(parts, axis=1)


def _lru_gates(u, r, ig, lam):
    ls = _log_sigmoid(lam)
    la = LRU_C * r * ls
    a = jnp.exp(la)
    em = _neg_expm1(2.0 * la)
    mult = jnp.sqrt(em)
    return ls, a, em, mult


def _lru_fwd(proj, cw, cb, wa, ba, wx, bx, lam):
    t = proj.shape[0]
    w = D_MODEL
    tm = min(LRU_TM, t)

    def body(gb_ref, up_ref, cw_ref, cb_ref, wa_ref, ba_ref, wx_ref, bx_ref, lam_ref,
             y_ref, u_ref, r_ref, i_ref, h_ref, tail_sc, hprev_sc):
        @pl.when(pl.program_id(0) == 0)
        def _():
            tail_sc[...] = jnp.zeros_like(tail_sc)
            hprev_sc[...] = jnp.zeros_like(hprev_sc)

        up = up_ref[...]
        ext = jnp.concatenate([tail_sc[...], up], axis=0)
        u = cb_ref[...] + cw_ref[CONV_WIDTH - 1:CONV_WIDTH, :] * up
        for s in range(1, CONV_WIDTH):
            u = u + cw_ref[CONV_WIDTH - 1 - s:CONV_WIDTH - s, :] * pltpu.roll(ext, s, 0)[8:8 + tm]
        tail_sc[...] = up[tm - 8:tm]
        ub = u.astype(BF16)
        r = _sigmoid(_block_diag_dot(ub, wa_ref, (((1,), (0,)), ((), ()))) + ba_ref[...])
        ig = _sigmoid(_block_diag_dot(ub, wx_ref, (((1,), (0,)), ((), ()))) + bx_ref[...])
        _, a, _, mult = _lru_gates(u, r, ig, lam_ref[...])
        acum, hloc = _scan_rows(a, mult * ig * u)
        h = hloc + acum * hprev_sc[0:1, :]
        hprev_sc[0:1, :] = h[tm - 1:tm]
        u_ref[...] = u
        r_ref[...] = r
        i_ref[...] = ig
        h_ref[...] = h
        gel, _ = _gelu(gb_ref[...])
        y_ref[...] = (h * gel).astype(BF16)

    vec = _full((1, w))
    wspec = _full((LRU_BLOCKS, LRU_BLOCK, LRU_BLOCK))
    return pl.pallas_call(
        body, name="lru_fwd", grid=(t // tm,),
        in_specs=[pl.BlockSpec((tm, w), lambda i: (i, 0)), pl.BlockSpec((tm, w), lambda i: (i, 1)),
                  _full((CONV_WIDTH, w)), vec, wspec, vec, wspec, vec, vec],
        out_specs=[_rows(tm, w)] * 5,
        out_shape=[_sds((t, w), BF16)] + [_sds((t, w), F32)] * 4,
        scratch_shapes=[pltpu.VMEM((8, w), F32), pltpu.VMEM((8, w), F32)],
        compiler_params=_params("arbitrary"),
    )(proj, proj, cw, cb, wa, ba, wx, bx, lam)


def _lru_bwd(proj, cw, wa, wx, lam, u, r, ig, h, dy, side):
    t = proj.shape[0]
    w = D_MODEL
    tm = min(LRU_TM, t)
    nt = t // tm

    def body(gb_ref, up_ref, cw_ref, wa_ref, wx_ref, lam_ref, u_ref, r_ref, i_ref, h_ref, hp_ref, dy_ref,
             dp_ref, dcw_ref, dcb_ref, dwa_ref, dba_ref, dwx_ref, dbx_ref, dlam_ref, carry_sc, dutail_sc, dls_sc):
        step = pl.program_id(0)

        @pl.when(step == 0)
        def _():
            carry_sc[...] = jnp.zeros_like(carry_sc)
            dutail_sc[...] = jnp.zeros_like(dutail_sc)
            dls_sc[...] = jnp.zeros_like(dls_sc)
            for ref in (dcw_ref, dcb_ref, dwa_ref, dba_ref, dwx_ref, dbx_ref):
                ref[...] = jnp.zeros_like(ref)

        row = lax.broadcasted_iota(jnp.int32, (tm, w), 0)
        u_t, r_t, i_t, h_t = u_ref[...], r_ref[...], i_ref[...], h_ref[...]
        ls, a, em, mult = _lru_gates(u_t, r_t, i_t, lam_ref[...])
        gb = gb_ref[...]
        gel, th = _gelu(gb)
        dyv = dy_ref[...]
        dgb = dyv * h_t * (0.5 * (1.0 + th) + 0.5 * gb * (1.0 - th * th) * GELU_K * (1.0 + 3.0 * GELU_C * gb * gb))
        dh = jnp.where(row == tm - 1, dyv * gel + carry_sc[0:1, :], dyv * gel)
        a_next = jnp.where(row < tm - 1, pltpu.roll(a, tm - 1, 0), 0.0)
        _, g = _scan_rows(a_next, dh, reverse=True)
        carry_sc[0:1, :] = a[0:1] * g[0:1]
        has_prev = jnp.where(step == nt - 1, 0.0, 1.0)
        h_prev = jnp.where(row >= 1, pltpu.roll(h_t, 1, 0), hp_ref[7:8, :] * has_prev)
        dmult = g * i_t * u_t
        dig = g * mult * u_t
        du = g * mult * i_t
        dla = g * h_prev * a - dmult * (1.0 - em) / mult
        dls_sc[0:1, :] += jnp.sum(dla * r_t, axis=0, keepdims=True) * LRU_C
        dpa = dla * (LRU_C * ls) * r_t * (1.0 - r_t)
        dpx = dig * i_t * (1.0 - i_t)
        dba_ref[...] += jnp.sum(dpa, axis=0, keepdims=True)
        dbx_ref[...] += jnp.sum(dpx, axis=0, keepdims=True)
        ub = u_t.astype(BF16)
        dpab = dpa.astype(BF16)
        dpxb = dpx.astype(BF16)
        for n in range(LRU_BLOCKS):
            cs = slice(LRU_BLOCK * n, LRU_BLOCK * (n + 1))
            dwa_ref[n] += lax.dot_general(ub[:, cs], dpab[:, cs], TN_DIMS, preferred_element_type=F32)
            dwx_ref[n] += lax.dot_general(ub[:, cs], dpxb[:, cs], TN_DIMS, preferred_element_type=F32)
        du = du + _block_diag_dot(dpab, wa_ref, NT_DIMS) + _block_diag_dot(dpxb, wx_ref, NT_DIMS)
        dcb_ref[...] += jnp.sum(du, axis=0, keepdims=True)
        ext = jnp.concatenate([du, dutail_sc[...]], axis=0)
        up = up_ref[...]
        dup = cw_ref[CONV_WIDTH - 1:CONV_WIDTH, :] * du
        dcw_ref[CONV_WIDTH - 1:CONV_WIDTH, :] += jnp.sum(up * du, axis=0, keepdims=True)
        for s in range(1, CONV_WIDTH):
            du_s = pltpu.roll(ext, tm + 8 - s, 0)[0:tm]
            dup = dup + cw_ref[CONV_WIDTH - 1 - s:CONV_WIDTH - s, :] * du_s
            dcw_ref[CONV_WIDTH - 1 - s:CONV_WIDTH - s, :] += jnp.sum(up * du_s, axis=0, keepdims=True)
        dutail_sc[...] = du[0:8]
        dp_ref[:, :w] = dgb.astype(BF16)
        dp_ref[:, w:] = dup.astype(BF16)

        @pl.when(step == nt - 1)
        def _():
            dlam_ref[...] = dls_sc[0:1, :] * _sigmoid(-lam_ref[...])

    rev = lambda col: (lambda i: (nt - 1 - i, col))
    vec = _full((1, w))
    wspec = _full((LRU_BLOCKS, LRU_BLOCK, LRU_BLOCK))
    tile = pl.BlockSpec((tm, w), rev(0))
    prev8 = pl.BlockSpec((8, w), lambda i: (jnp.maximum((nt - 1 - i) * (tm // 8) - 1, 0), 0))
    return _side_call(
        body, side, name="lru_bwd", steps=nt,
        in_specs=[tile, pl.BlockSpec((tm, w), rev(1)), _full((CONV_WIDTH, w)), wspec, wspec, vec,
                  tile, tile, tile, tile, prev8, tile],
        out_specs=[pl.BlockSpec((tm, 2 * w), rev(0)), _full((CONV_WIDTH, w)), vec, wspec, vec, wspec, vec, vec],
        out_shape=[_sds((t, 2 * w), BF16), _sds((CONV_WIDTH, w), F32), _sds((1, w), F32),
                   _sds((LRU_BLOCKS, LRU_BLOCK, LRU_BLOCK), F32), _sds((1, w), F32),
                   _sds((LRU_BLOCKS, LRU_BLOCK, LRU_BLOCK), F32), _sds((1, w), F32), _sds((1, w), F32)],
        scratch_shapes=[pltpu.VMEM((8, w), F32), pltpu.VMEM((8, w), F32), pltpu.VMEM((8, w), F32)],
        args=(proj, proj, cw, wa, wx, lam, u, r, ig, h, h, dy))


LANES = 1024
HALF_FFN = D_FF // N_CHIPS
GROUPS = {
    "ml": (("ml_w_in", 1024, ML_IN // N_CHIPS), ("ml_w_out", 256, 1024)),
    "lru": (("lru_w_in", 1024, 512),
            ("lru_gates", 2 * LRU_BLOCKS * 64, LRU_BLOCK),
            ("lru_w_out", 256, 1024)),
    "ffn0": (("ffn_g0", HALF_FFN, 1024), ("ffn_u0", HALF_FFN, 1024), ("ffn_down0", HALF_FFN, 1024)),
    "ffn1": (("ffn_g1", HALF_FFN, 1024), ("ffn_u1", HALF_FFN, 1024), ("ffn_down1", HALF_FFN, 1024)),
}

SMALL_ROWS = 24
ROW_LOSS, ROW_HEAD_NORM, ROW_B_IF, ROW_LRU = 8, 9, 10, 11


def _row_tile(rows, cols, itemsize, budget=3 << 19):
    best = 16
    for t in range(16, rows + 1, 16):
        if rows % t == 0 and t * cols * itemsize <= budget:
            best = t
    return best


def _chip_peers():
    x, y, c = lax.axis_index("x"), lax.axis_index("y"), lax.axis_index("c")
    return x, y, c, [(1 - x, y), (x, 1 - y), (1 - x, 1 - y)]


HBM_SPEC = pl.BlockSpec(memory_space=pltpu.HBM)


def _remote(src, dst, send_sems, recv_sems, k, to):
    return pltpu.make_async_remote_copy(src_ref=src, dst_ref=dst, send_sem=send_sems.at[k], recv_sem=recv_sems.at[k],
                                        device_id=to, device_id_type=MESH)


GATHER_SEMS = 7


def _gather_copies(kind, specs, part_refs, out_refs, send_sems, recv_sems):
    x, y, c, chips = _chip_peers()
    me = 2 * x + y
    sib = (x, y, 1 - c)
    copy = functools.partial(_remote, send_sems=send_sems, recv_sems=recv_sems)
    out = []
    for p, (_, rows, _) in enumerate(specs):
        mine = pl.ds(c * (rows // 2), rows // 2)
        theirs = pl.ds((1 - c) * (rows // 2), rows // 2)
        base = GATHER_SEMS * p
        for j, (cx, cy) in enumerate(chips):
            land = out_refs[p].at[2 * cx + cy, mine]
            other = out_refs[p].at[2 * cx + cy, theirs]
            if kind == "first":
                out.append(copy(part_refs[p].at[mine], out_refs[p].at[me, mine], k=base + j, to=(cx, cy, c)))
            elif kind == "landed":
                out.append(copy(land, land, k=base + j, to=sib))
            elif kind == "forward":
                out.append(copy(land, land, k=base + 3 + j, to=sib))
            else:
                out.append(copy(other, other, k=base + 3 + j, to=sib))
        if kind in ("first", "arriving"):
            out.append(copy(part_refs[p], out_refs[p].at[me], k=base + 6, to=sib))
    return out


def _gather_start(specs, *refs):
    for cp in _gather_copies("first", specs, *refs):
        cp.start()


def _gather_finish(specs, *refs):
    forwards = _gather_copies("forward", specs, *refs)
    for land, fwd in zip(_gather_copies("landed", specs, *refs), forwards):
        land.wait_recv()
        fwd.start()
    for cp in _gather_copies("arriving", specs, *refs):
        cp.wait_recv()
    for cp in _gather_copies("first", specs, *refs) + forwards:
        cp.wait_send()


def _gather_shapes(specs):
    return [_sds((N_CHIPS, rows, cols), BF16) for _, rows, cols in specs]


def _gather_side(specs, parts):
    return _Side(inputs=list(parts), out_shape=_gather_shapes(specs), n_sems=GATHER_SEMS * len(specs),
                 start=functools.partial(_gather_start, specs), finish=functools.partial(_gather_finish, specs))


def _gather_weights(specs, parts, small):
    n = len(specs)

    def body(*refs):
        part_refs, small_ref = refs[:n], refs[n]
        out_refs, outs_ref = refs[n + 1:2 * n + 1], refs[2 * n + 1]
        send_sems, recv_sems, small_send, small_recv, loc_sem = refs[2 * n + 2:]
        x, y, c, chips = _chip_peers()
        me = 2 * x + y
        local = pltpu.make_async_copy(small_ref, outs_ref.at[me], loc_sem.at[0])
        local.start()
        _gather_start(specs, part_refs, out_refs, send_sems, recv_sems)
        sent = [_remote(small_ref, outs_ref.at[me], small_send, small_recv, j, (cx, cy, c))
                for j, (cx, cy) in enumerate(chips)]
        for cp in sent:
            cp.start()
        _gather_finish(specs, part_refs, out_refs, send_sems, recv_sems)
        for j, (cx, cy) in enumerate(chips):
            _remote(small_ref, outs_ref.at[2 * cx + cy], small_send, small_recv, j, (cx, cy, c)).wait_recv()
        for cp in sent:
            cp.wait_send()
        local.wait()

    dma = pltpu.SemaphoreType.DMA
    return pl.pallas_call(
        body, name="gather_weights",
        in_specs=[HBM_SPEC] * (n + 1), out_specs=[HBM_SPEC] * (n + 1),
        out_shape=_gather_shapes(specs) + [_sds((N_CHIPS,) + small.shape, small.dtype)],
        scratch_shapes=[dma((GATHER_SEMS * n,)), dma((GATHER_SEMS * n,)), dma((3,)), dma((3,)), dma((1,))],
    )(*parts, small)


def _exchange_copies(specs, g_refs, out_refs, send_sems, recv_sems):
    x, y, c, _ = _chip_peers()
    return [_remote(g_refs[p].at[:, pl.ds((1 - c) * (rows // 2), rows // 2)], out_refs[p], send_sems, recv_sems, p,
                    (x, y, 1 - c)) for p, (_, rows, _) in enumerate(specs)]


def _exchange_start(specs, *refs):
    for cp in _exchange_copies(specs, *refs):
        cp.start()


def _exchange_finish(specs, *refs):
    for cp in _exchange_copies(specs, *refs):
        cp.wait()


def _exchange_shapes(specs):
    return [_sds((N_CHIPS, rows // 2, cols), BF16) for _, rows, cols in specs]


def _exchange_side(specs, gparts):
    return _Side(inputs=list(gparts), out_shape=_exchange_shapes(specs), n_sems=len(specs),
                 start=functools.partial(_exchange_start, specs), finish=functools.partial(_exchange_finish, specs))


def _exchange_halves(specs, gparts, name):
    n = len(specs)

    def body(*refs):
        _exchange_start(specs, refs[:n], refs[n:2 * n], *refs[2 * n:])
        _exchange_finish(specs, refs[:n], refs[n:2 * n], *refs[2 * n:])

    return pl.pallas_call(
        body, name=name, in_specs=[HBM_SPEC] * n, out_specs=[HBM_SPEC] * n, out_shape=_exchange_shapes(specs),
        scratch_shapes=[pltpu.SemaphoreType.DMA((n,)), pltpu.SemaphoreType.DMA((n,))],
    )(*gparts)


def _add_halves(g, recv, pos, name):
    _, half, cols = recv.shape
    tr = _row_tile(half, cols, 2)
    tiles = half // tr

    def body(pos_ref, a_ref, b_ref, o_ref):
        o_ref[...] = (a_ref[...].astype(F32) + b_ref[...].astype(F32)).astype(BF16)

    spec = pl.BlockSpec((1, tr, cols), lambda k, i, pos_ref: (k, i, 0))
    return pl.pallas_call(
        body, name=name,
        grid_spec=pltpu.PrefetchScalarGridSpec(
            num_scalar_prefetch=1, grid=(N_CHIPS, tiles),
            in_specs=[pl.BlockSpec((1, tr, cols), lambda k, i, pos_ref: (k, pos_ref[0] * tiles + i, 0)), spec],
            out_specs=spec),
        out_shape=_sds((N_CHIPS, half, cols), BF16),
        compiler_params=_params("parallel", "parallel"),
    )(pos, g, recv)


def _scatter_copies(n, s1_refs, recv_refs, send_sems, recv_sems):
    x, y, c, chips = _chip_peers()
    return [_remote(s1_refs[p].at[2 * cx + cy], recv_refs[p].at[j], send_sems, recv_sems, 3 * p + j, (cx, cy, c))
            for p in range(n) for j, (cx, cy) in enumerate(chips)]


def _scatter_start(n, s1_refs, recv_refs, send_sems, recv_sems):
    for cp in _scatter_copies(n, s1_refs, recv_refs, send_sems, recv_sems):
        cp.start()


def _scatter_finish(n, s1_refs, recv_refs, send_sems, recv_sems):
    for cp in _scatter_copies(n, s1_refs, recv_refs, send_sems, recv_sems):
        cp.wait()


def _scatter_shapes(s1):
    return [_sds((3,) + a.shape[1:], a.dtype) for a in s1]


def _scatter_side(s1):
    n = len(s1)
    return _Side(inputs=list(s1), out_shape=_scatter_shapes(s1), n_sems=3 * n,
                 start=functools.partial(_scatter_start, n), finish=functools.partial(_scatter_finish, n))


def _share_small(small):
    flips = [(fx, fy, fc) for fx in (0, 1) for fy in (0, 1) for fc in (0, 1)][1:]

    def body(small_ref, all_ref, send_sems, recv_sems, loc_sem):
        x, y, c, _ = _chip_peers()
        my_slot = all_ref.at[4 * x + 2 * y + c]
        local = pltpu.make_async_copy(small_ref, my_slot, loc_sem.at[0])
        local.start()
        peers = [(1 - x if fx else x, 1 - y if fy else y, 1 - c if fc else c) for fx, fy, fc in flips]
        sent = [_remote(small_ref, my_slot, send_sems, recv_sems, i, p) for i, p in enumerate(peers)]
        for cp in sent:
            cp.start()
        for i, (px, py, pc) in enumerate(peers):
            _remote(small_ref, all_ref.at[4 * px + 2 * py + pc], send_sems, recv_sems, i, peers[i]).wait_recv()
        for cp in sent:
            cp.wait_send()
        local.wait()

    dma = pltpu.SemaphoreType.DMA
    return pl.pallas_call(
        body, name="share_small", in_specs=[HBM_SPEC], out_specs=HBM_SPEC,
        out_shape=_sds((8,) + small.shape, small.dtype), scratch_shapes=[dma((7,)), dma((7,)), dma((1,))],
    )(small)


def _sum_chips(s1, recv, pos, name):
    _, half, cols = recv.shape
    tr = _row_tile(half, cols, 4)
    tiles = half // tr

    def body(pos_ref, a_ref, b_ref, o_ref):
        acc = a_ref[0].astype(F32)
        for j in range(3):
            acc = acc + b_ref[j].astype(F32)
        o_ref[...] = acc

    return pl.pallas_call(
        body, name=name,
        grid_spec=pltpu.PrefetchScalarGridSpec(
            num_scalar_prefetch=1, grid=(tiles,),
            in_specs=[pl.BlockSpec((1, tr, cols), lambda i, pos_ref: (pos_ref[1], i, 0)),
                      pl.BlockSpec((3, tr, cols), lambda i, pos_ref: (0, i, 0))],
            out_specs=pl.BlockSpec((tr, cols), lambda i, pos_ref: (pos_ref[0] * tiles + i, 0))),
        out_shape=_sds((2 * half, cols), F32),
        compiler_params=_params("parallel"),
    )(pos, s1, recv)


def _sum_small(small_all):
    def body(a_ref, o_ref):
        acc = a_ref[0]
        for d in range(1, 8):
            acc = acc + a_ref[d]
        o_ref[...] = acc

    return pl.pallas_call(body, name="sum_small", out_shape=_sds(small_all.shape[1:], F32))(small_all)


def _join_halves(specs, s2):
    n = len(specs)

    def body(*refs):
        buf_refs = refs[n:2 * n]
        send_sems, recv_sems = refs[2 * n:]
        x, y, c, _ = _chip_peers()
        sent = []
        for p, (_, rows, _) in enumerate(specs):
            mine = buf_refs[p].at[pl.ds(c * (rows // 2), rows // 2)]
            sent.append(_remote(mine, mine, send_sems, recv_sems, p, (x, y, 1 - c)))
            sent[-1].start()
        for p, (_, rows, _) in enumerate(specs):
            theirs = buf_refs[p].at[pl.ds((1 - c) * (rows // 2), rows // 2)]
            _remote(theirs, theirs, send_sems, recv_sems, p, (x, y, 1 - c)).wait_recv()
        for cp in sent:
            cp.wait_send()

    return pl.pallas_call(
        body, name="join_halves", in_specs=[HBM_SPEC] * n, out_specs=[HBM_SPEC] * n,
        out_shape=[_sds(a.shape, a.dtype) for a in s2],
        input_output_aliases={p: p for p in range(n)},
        scratch_shapes=[pltpu.SemaphoreType.DMA((n,)), pltpu.SemaphoreType.DMA((n,))],
    )(*s2)


def _adamw(w, g, m, v, name):
    rows, cols = w.shape
    tm = rows
    for cand in (512, 256, 128, 64, 32, 16, 8):
        if rows % cand == 0 and rows > cand:
            tm = cand
            break

    def body(w_ref, g_ref, m_ref, v_ref, d_ref, nm_ref, nv_ref):
        gv = g_ref[...]
        nm = ADAM_B1 * m_ref[...] + (1.0 - ADAM_B1) * gv
        nv = ADAM_B2 * v_ref[...] + (1.0 - ADAM_B2) * (gv * gv)
        m_hat = nm / (1.0 - ADAM_B1 ** ADAM_STEP)
        v_hat = nv / (1.0 - ADAM_B2 ** ADAM_STEP)
        d_ref[...] = -ADAM_LR * (m_hat / (jnp.sqrt(v_hat) + ADAM_EPS) + ADAM_WD * w_ref[...])
        nm_ref[...] = nm
        nv_ref[...] = nv

    spec = _rows(tm, cols)
    return pl.pallas_call(
        body, name=name, grid=(rows // tm,), in_specs=[spec] * 4, out_specs=[spec] * 3,
        out_shape=[_sds((rows, cols), F32)] * 3, compiler_params=_params("parallel"),
    )(w, g, m, v)


WEIGHTS = ("ml_w_in", "ml_b_if", "ml_head_norm", "ml_w_out", "lru_w_in", "lru_conv_w", "lru_conv_b", "lru_w_gate_a",
           "lru_b_gate_a", "lru_w_gate_x", "lru_b_gate_x", "lru_lambda", "lru_w_out", "norm_pre_mix", "norm_post_mix",
           "norm_pre_ffn", "norm_post_ffn", "ffn_w_gate", "ffn_w_up", "ffn_w_down")
LRU_VECTORS = ("lru_conv_b", "lru_b_gate_a", "lru_b_gate_x", "lru_lambda")
NORMS = ("norm_pre_mix", "norm_post_mix", "norm_pre_ffn", "norm_post_ffn")


def _by_cols(a):
    return jnp.concatenate([a[k] for k in range(N_CHIPS)], axis=-1)


def _by_chip(a, width):
    return jnp.stack([a[..., k * width:(k + 1) * width] for k in range(N_CHIPS)])


def _weight_shards(w):
    bf = lambda a: a.astype(BF16)
    shards = dict(
        ml_w_in=bf(w["ml_w_in"][0]), ml_w_out=bf(w["ml_w_out"][0]), lru_w_in=bf(w["lru_w_in"][0]),
        lru_gates=bf(jnp.concatenate([w["lru_w_gate_a"][0], w["lru_w_gate_x"][0]], axis=0)).reshape(-1, LRU_BLOCK),
        lru_w_out=bf(w["lru_w_out"][0]))
    gate_t, up_t = bf(jnp.swapaxes(w["ffn_w_gate"], 1, 2)), bf(jnp.swapaxes(w["ffn_w_up"], 1, 2))
    for layer in range(2):
        shards[f"ffn_g{layer}"] = gate_t[layer]
        shards[f"ffn_u{layer}"] = up_t[layer]
        shards[f"ffn_down{layer}"] = bf(w["ffn_w_down"][layer])
    return shards


def _ffn_weights(parts):
    return [a.reshape(D_FF, 1024) for a in parts]


def _ffn_grads(dgu, z, act, dy, layer):
    dws = [_matmul_tn(dgu, z, True, 256, f"ffn{layer}_dw_gate", window=(0, D_FF)),
           _matmul_tn(dgu, z, True, 256, f"ffn{layer}_dw_up", window=(D_FF, D_FF)),
           _matmul_tn(act, dy, True, 256, f"ffn{layer}_dw_down")]
    return [a.reshape(N_CHIPS, HALF_FFN, 1024) for a in dws]


def _sum_group(group, s1, recv, pos):
    return [_sum_chips(a, r, pos, "sum_chips_" + n) for (n, _, _), a, r in zip(GROUPS[group], s1, recv)]


def _update(w, grads, m, v):
    delta, new_m, new_v = {}, {}, {}
    for n in WEIGHTS:
        flip = (lambda a: jnp.swapaxes(a, 1, 2)) if n in ("ffn_w_gate", "ffn_w_up") else (lambda a: a)
        shape = flip(w[n]).shape
        as2d = lambda a: flip(a).reshape(-1, shape[-1])
        back = lambda a: flip(a.reshape(shape))
        d, nm, nv = _adamw(as2d(w[n]), as2d(grads[n]), as2d(m[n]), as2d(v[n]), "adamw_" + n)
        delta[n], new_m[n], new_v[n] = back(d), back(nm), back(nv)
    return delta, new_m, new_v


def kernel(x, ml_w_in, ml_b_if, ml_head_norm, ml_w_out, lru_w_in, lru_conv_w, lru_conv_b, lru_w_gate_a, lru_b_gate_a, lru_w_gate_x, lru_b_gate_x, lru_lambda, lru_w_out, norm_pre_mix, norm_post_mix, norm_pre_ffn, norm_post_ffn, ffn_w_gate, ffn_w_up, ffn_w_down, loss_target, m_ml_w_in, m_ml_b_if, m_ml_head_norm, m_ml_w_out, m_lru_w_in, m_lru_conv_w, m_lru_conv_b, m_lru_w_gate_a, m_lru_b_gate_a, m_lru_w_gate_x, m_lru_b_gate_x, m_lru_lambda, m_lru_w_out, m_norm_pre_mix, m_norm_post_mix, m_norm_pre_ffn, m_norm_post_ffn, m_ffn_w_gate, m_ffn_w_up, m_ffn_w_down, v_ml_w_in, v_ml_b_if, v_ml_head_norm, v_ml_w_out, v_lru_w_in, v_lru_conv_w, v_lru_conv_b, v_lru_w_gate_a, v_lru_b_gate_a, v_lru_w_gate_x, v_lru_b_gate_x, v_lru_lambda, v_lru_w_out, v_norm_pre_mix, v_norm_post_mix, v_norm_pre_ffn, v_norm_post_ffn, v_ffn_w_gate, v_ffn_w_up, v_ffn_w_down):
    args = locals()
    w = {n: args[n] for n in WEIGHTS}
    m = {n: args["m_" + n] for n in WEIGHTS}
    v = {n: args["v_" + n] for n in WEIGHTS}
    xs, target = x[0], loss_target[0]
    mx, my, mc, _ = _chip_peers()
    chip = 2 * mx + my
    pos = jnp.stack([mc, chip])
    row = lambda a, i: a[i:i + 1]
    npm, nqm, npf, nqf = (w[n] for n in NORMS)
    shards = _weight_shards(w)
    of = lambda group: [shards[n] for n, _, _ in GROUPS[group]]
    bif = jnp.pad(w["ml_b_if"], ((0, 0), (0, 128 - 2 * ML_HEADS)))
    hn = w["ml_head_norm"]

    small = jnp.concatenate([w["lru_conv_w"][0]] + [w[n] for n in LRU_VECTORS], axis=0)
    ml_in_spec, ml_out_spec = GROUPS["ml"][:1], GROUPS["ml"][1:]
    ml_in_parts, smalls = _gather_weights(ml_in_spec, of("ml")[:1], small)
    ml_w_in = jnp.pad(_by_cols(ml_in_parts), ((0, 0), (0, ML_IN_PAD - ML_IN)))
    vec = _by_cols(smalls)
    (z0, qkv, og), got = _norm_matmul(xs, row(npm, 0), ml_w_in, 2 * ML_QK + ML_V, "ml_in",
                                      _gather_side(ml_out_spec + GROUPS["lru"], of("ml")[1:] + of("lru")))
    ml_w_out = got[0].reshape(1024, 1024)
    lru_w_in = _by_cols(got[1])
    gates = got[2].reshape(N_CHIPS, 2, LRU_BLOCKS, 64, LRU_BLOCK).transpose(1, 2, 0, 3, 4)
    gates = gates.reshape(2, LRU_BLOCKS, LRU_BLOCK, LRU_BLOCK)
    lru_w_out = got[3].reshape(1024, 1024)
    (y0, hs, cst, nst, mst), got = _mlstm_fwd(qkv, og, bif, hn, _gather_side(GROUPS["ffn0"], of("ffn0")))
    wg0, wu0, w_down0 = _ffn_weights(got)
    ymix0, h1 = _matmul_postnorm(y0, ml_w_out, row(nqm, 0), xs, "ml_out")
    (zf0, gu0, yf0, h2), got = _ffn_forward(h1, row(npf, 0), wg0, wu0, w_down0, row(nqf, 0), "ffn0_fwd",
                                            _gather_side(GROUPS["ffn1"], of("ffn1")))
    wg1, wu1, w_down1 = _ffn_weights(got)
    (z1, proj1), _ = _norm_matmul(h2, row(npm, 1), lru_w_in, 0, "lru_in", None)
    y1, u, r, ig, hl = _lru_fwd(proj1, vec[0:4], vec[4:5], gates[0], vec[5:6], gates[1], vec[6:7], vec[7:8])
    ymix1, h3 = _matmul_postnorm(y1, lru_w_out, row(nqm, 1), h2, "lru_out")
    (zf1, gu1, yf1, h4), _ = _ffn_forward(h3, row(npf, 1), wg1, wu1, w_down1, row(nqf, 1), "ffn1_fwd", None)
    dh4, loss_part = _loss_head(h4, target, "loss_head")

    add = lambda group, gparts, recv: [_add_halves(g, r, pos, "add_halves_" + n)
                                       for (n, _, _), g, r in zip(GROUPS[group], gparts, recv)]
    (dyf1, dgu1, act1, dh3, dqf1, dpf1), _ = _ffn_backward(
        dh4, yf1, row(nqf, 1), w_down1, gu1, wg1, wu1, h3, row(npf, 1), "ffn1_bwd", None)
    g_ffn1 = _ffn_grads(dgu1, zf1, act1, dyf1, 1)

    (dymix1, dy1, dqm1), recv = _bwd_out(dh3, ymix1, row(nqm, 1), lru_w_out, "lru_bwd_out",
                                         _exchange_side(GROUPS["ffn1"], g_ffn1))
    s1_ffn1 = add("ffn1", g_ffn1, recv)
    dw_lru_out = _matmul_tn(y1, dymix1, False, 256, "lru_dw_out")
    (dproj1, dcw, dcb, dwa, dba, dwx, dbx, dlam), recv_ffn1 = _lru_bwd(
        proj1, vec[0:4], gates[0], gates[1], vec[7:8], u, r, ig, hl, dy1, _scatter_side(s1_ffn1))
    dw_lru_in = _matmul_tn(z1, dproj1, False, 512, "lru_dw_in")
    dgates = jnp.stack([dwa, dwx]).astype(BF16).reshape(2, LRU_BLOCKS, N_CHIPS, 64, LRU_BLOCK)
    dgates = dgates.transpose(2, 0, 1, 3, 4).reshape(N_CHIPS, -1, LRU_BLOCK)
    g_lru = [_by_chip(dw_lru_in, 512), dgates, dw_lru_out.reshape(N_CHIPS, 256, 1024)]
    (dh2, dpm1), recv = _bwd_in(dproj1, [lru_w_in], False, h2, row(npm, 1), dh3, "lru_bwd_in",
                                _exchange_side(GROUPS["lru"], g_lru))
    s1_lru = add("lru", g_lru, recv)

    (dyf0, dgu0, act0, dh1, dqf0, dpf0), recv_lru = _ffn_backward(
        dh2, yf0, row(nqf, 0), w_down0, gu0, wg0, wu0, h1, row(npf, 0), "ffn0_bwd", _scatter_side(s1_lru))
    g_ffn0 = _ffn_grads(dgu0, zf0, act0, dyf0, 0)

    (dymix0, dy0, dqm0), recv = _bwd_out(dh1, ymix0, row(nqm, 0), ml_w_out, "ml_bwd_out",
                                         _exchange_side(GROUPS["ffn0"], g_ffn0))
    s1_ffn0 = add("ffn0", g_ffn0, recv)
    dw_ml_out = _matmul_tn(y0, dymix0, False, 256, "ml_dw_out")
    (dproj0, dhn, dbif), recv_ffn0 = _mlstm_bwd(qkv, og, bif, hn, hs, cst, nst, mst, dy0, _scatter_side(s1_ffn0))
    dw_ml_in = _matmul_tn(z0, dproj0, False, 640, "ml_dw_in")
    g_ml = [_by_chip(dw_ml_in[:, :ML_IN], ML_IN // N_CHIPS), dw_ml_out.reshape(N_CHIPS, 256, 1024)]
    s1_ml = add("ml", g_ml, _exchange_halves(GROUPS["ml"], g_ml, "exchange_halves_ml"))
    (dx, dpm0), recv_ml = _bwd_in(dproj0, [ml_w_in], False, xs, row(npm, 0), dh1, "ml_bwd_in", _scatter_side(s1_ml))

    pad_lanes = lambda a: jnp.pad(a, ((0, 0), (0, LANES - a.shape[1])))
    small = jnp.concatenate(
        [jnp.concatenate([dpm0, dpm1]), jnp.concatenate([dqm0, dqm1]), jnp.concatenate([dpf0, dpf1]),
         jnp.concatenate([dqf0, dqf1]), pad_lanes(loss_part), dhn, pad_lanes(dbif), dcw, dcb, dba, dbx, dlam,
         jnp.zeros((SMALL_ROWS - 19, LANES), F32)], axis=0)
    small_all = _share_small(small)
    order = ("ml", "lru", "ffn0", "ffn1")
    s2 = (_sum_group("ml", s1_ml, recv_ml, pos) + _sum_group("lru", s1_lru, recv_lru, pos)
          + _sum_group("ffn0", s1_ffn0, recv_ffn0, pos) + _sum_group("ffn1", s1_ffn1, recv_ffn1, pos))
    specs = sum((GROUPS[g] for g in order), ())
    red = dict(zip([n for n, _, _ in specs], _join_halves(specs, s2)))
    vsum = _sum_small(small_all)

    dgates = red["lru_gates"].reshape(2, LRU_BLOCKS, 64, LRU_BLOCK)
    cols = lambda a: lax.dynamic_slice_in_dim(a, chip * 256, 256, axis=1)
    grads = dict(
        ml_w_in=red["ml_w_in"], ml_w_out=red["ml_w_out"], lru_w_in=red["lru_w_in"], lru_w_gate_a=dgates[0],
        lru_w_gate_x=dgates[1], lru_w_out=red["lru_w_out"],
        ffn_w_gate=jnp.swapaxes(jnp.stack([red["ffn_g0"], red["ffn_g1"]]), 1, 2),
        ffn_w_up=jnp.swapaxes(jnp.stack([red["ffn_u0"], red["ffn_u1"]]), 1, 2),
        ffn_w_down=jnp.stack([red["ffn_down0"], red["ffn_down1"]]),
        ml_head_norm=vsum[ROW_HEAD_NORM:ROW_HEAD_NORM + 1], ml_b_if=vsum[ROW_B_IF:ROW_B_IF + 1, :2 * ML_HEADS],
        lru_conv_w=cols(vsum[ROW_LRU:ROW_LRU + 4]))
    for i, n in enumerate(NORMS):
        grads[n] = vsum[2 * i:2 * i + 2]
    for i, n in enumerate(LRU_VECTORS):
        grads[n] = cols(vsum[ROW_LRU + 4 + i:ROW_LRU + 5 + i])
    loss = vsum[ROW_LOSS, 0]
    grads = {n: grads[n].reshape(w[n].shape) for n in WEIGHTS}
    delta, new_m, new_v = _update(w, grads, m, v)
    return (loss, dx[None], *[grads[n] for n in WEIGHTS], *[delta[n] for n in WEIGHTS],
            *[new_m[n] for n in WEIGHTS], *[new_v[n] for n in WEIGHTS])
```

```python
import functools
import math
from typing import Callable, NamedTuple

import jax
import jax.numpy as jnp
from jax import lax
from jax.experimental import pallas as pl
from jax.experimental.pallas import tpu as pltpu

F32 = jnp.float32
BF16 = jnp.bfloat16
MESH = pl.DeviceIdType.MESH

D_MODEL = 1024
D_FF = 2816
ML_HEADS = 8
ML_DK = 64
ML_DV = 128
ML_QK = ML_HEADS * ML_DK
ML_V = ML_HEADS * ML_DV
ML_IN = 2 * ML_QK + 2 * ML_V + 2 * ML_HEADS
ML_IN_PAD = 3200
CHUNK = 64
GATE_CAP = 15.0
ML_M_INIT = -1e30
NEG_BIG = -1e30
LRU_BLOCKS = 4
LRU_BLOCK = 256
CONV_WIDTH = 4
LRU_C = 8.0
EPS = 1e-6
QK_SCALE = ML_DK ** -0.5

ADAM_LR = 0.001
ADAM_B1 = 0.9
ADAM_B2 = 0.999
ADAM_EPS = 1e-08
ADAM_WD = 0.01
ADAM_STEP = 10

N_CHIPS = 4
V7X_VMEM_LIMIT = 56 * 1024 * 1024

NT_DIMS = (((1,), (1,)), ((), ()))
TN_DIMS = (((0,), (0,)), ((), ()))


def _params(*semantics):
    return pltpu.CompilerParams(dimension_semantics=semantics, vmem_limit_bytes=V7X_VMEM_LIMIT)


def _sds(shape, dtype):
    return jax.ShapeDtypeStruct(shape, dtype)


def _full(shape):
    return pl.BlockSpec(shape, lambda *_: (0,) * len(shape))


def _rows(tm, n):
    return pl.BlockSpec((tm, n), lambda i: (i, 0))


def _sigmoid(x):
    return 1.0 / (1.0 + jnp.exp(-x))


def _log_sigmoid(x):
    return jnp.minimum(x, 0.0) - jnp.log1p(jnp.exp(-jnp.abs(x)))


def _rms(x):
    r = lax.rsqrt(jnp.mean(x * x, axis=-1, keepdims=True) + EPS)
    return x * r, r


def _rms_bwd(xhat, r, g, dy):
    dxh = dy * g
    return r * (dxh - xhat * jnp.mean(dxh * xhat, axis=-1, keepdims=True))


SUBLANES = 8


def _scan_rows(a, b, carry, reverse=False):
    n, w = a.shape
    groups = n // SUBLANES
    a3, b3 = a.reshape(groups, SUBLANES, w), b.reshape(groups, SUBLANES, w)
    sub = lax.broadcasted_iota(jnp.int32, a3.shape, 1)
    s = 1
    while s < SUBLANES:
        keep = sub < SUBLANES - s if reverse else sub >= s
        shift = SUBLANES - s if reverse else s
        b3 = b3 + a3 * jnp.where(keep, pltpu.roll(b3, shift, 1), 0.0)
        a3 = a3 * jnp.where(keep, pltpu.roll(a3, shift, 1), 1.0)
        s *= 2
    out = [None] * groups
    for g in (reversed(range(groups)) if reverse else range(groups)):
        out[g] = b3[g] + a3[g] * carry
        carry = out[g][0:1] if reverse else out[g][SUBLANES - 1:SUBLANES]
    return jnp.concatenate(out, axis=0)


def _cumsum_rows(x, reverse=False):
    n = x.shape[0]
    row = lax.broadcasted_iota(jnp.int32, x.shape, 0)
    s = 1
    while s < n:
        if reverse:
            x = x + jnp.where(row < n - s, pltpu.roll(x, n - s, 0), 0.0)
        else:
            x = x + jnp.where(row >= s, pltpu.roll(x, s, 0), 0.0)
        s *= 2
    return x


def _norm_matmul(h, g, w, n_bf16, name, side):
    t, d = h.shape
    n = w.shape[1]
    tm = min(512, t)

    def body(h_ref, g_ref, w_ref, z_ref, *o_refs):
        xhat, _ = _rms(h_ref[...])
        z = (xhat * g_ref[...]).astype(BF16)
        z_ref[...] = z
        out = jnp.dot(z, w_ref[...], preferred_element_type=F32)
        if n_bf16:
            o_refs[0][...] = out[:, :n_bf16].astype(BF16)
            o_refs[1][...] = out[:, n_bf16:]
        else:
            o_refs[0][...] = out

    if n_bf16:
        out_specs = [_rows(tm, d), _rows(tm, n_bf16), _rows(tm, n - n_bf16)]
        out_shape = [_sds((t, d), BF16), _sds((t, n_bf16), BF16), _sds((t, n - n_bf16), F32)]
    else:
        out_specs = [_rows(tm, d), _rows(tm, n)]
        out_shape = [_sds((t, d), BF16), _sds((t, n), F32)]
    return _side_call(body, side, name=name, steps=t // tm, in_specs=[_rows(tm, d), _full((1, d)), _full((d, n))],
                      out_specs=out_specs, out_shape=out_shape, scratch_shapes=[], args=(h, g, w))


def _matmul_postnorm(a, w, g, res, name, side):
    t = res.shape[0]
    k, d = w.shape
    tm = 256

    def body(a_ref, w_ref, g_ref, res_ref, y_ref, o_ref):
        y = jnp.dot(a_ref[...], w_ref[...], preferred_element_type=F32)
        y_ref[...] = y
        yhat, _ = _rms(y)
        o_ref[...] = res_ref[...] + yhat * g_ref[...]

    return _side_call(
        body, side, name=name, steps=t // tm,
        in_specs=[_rows(tm, k), _full((k, d)), _full((1, d)), _rows(tm, d)],
        out_specs=[_rows(tm, d), _rows(tm, d)],
        out_shape=[_sds((t, d), F32), _sds((t, d), F32)],
        scratch_shapes=[], args=(a, w, g, res))


class _Side(NamedTuple):
    inputs: list
    out_shape: list
    n_sems: int
    start: Callable
    finish: Callable


def _side_call(body, side, *, name, steps, in_specs, out_specs, out_shape, scratch_shapes, args):
    n_in, n_out, n_scr = len(in_specs), len(out_specs), len(scratch_shapes)
    if side is None:
        outs = pl.pallas_call(
            body, name=name, grid=(steps,), in_specs=in_specs, out_specs=out_specs, out_shape=out_shape,
            scratch_shapes=scratch_shapes, compiler_params=_params("arbitrary"))(*args)
        return list(outs), []
    s_in, s_out = len(side.inputs), len(side.out_shape)

    def carrying(*refs):
        ins, side_ins = refs[:n_in], refs[n_in:n_in + s_in]
        outs = refs[n_in + s_in:n_in + s_in + n_out]
        side_outs = refs[n_in + s_in + n_out:n_in + s_in + n_out + s_out]
        scratch = refs[n_in + s_in + n_out + s_out:]
        own, sems = scratch[:n_scr], scratch[n_scr:]

        @pl.when(pl.program_id(0) == 0)
        def _():
            side.start(side_ins, side_outs, *sems)

        body(*ins, *outs, *own)

        @pl.when(pl.program_id(0) == steps - 1)
        def _():
            side.finish(side_ins, side_outs, *sems)

    outs = pl.pallas_call(
        carrying, name=name, grid=(steps,), in_specs=list(in_specs) + [HBM_SPEC] * s_in,
        out_specs=list(out_specs) + [HBM_SPEC] * s_out, out_shape=list(out_shape) + list(side.out_shape),
        scratch_shapes=list(scratch_shapes) + [pltpu.SemaphoreType.DMA((side.n_sems,))] * 2,
        compiler_params=_params("arbitrary"))(*args, *side.inputs)
    return list(outs[:n_out]), list(outs[n_out:])


def _ffn_forward(h, g_pre, wg_t, wu_t, w_down, g_post, name, side):
    t, d = h.shape
    k = w_down.shape[0]
    tm = 256

    def body(h_ref, gpre_ref, wg_ref, wu_ref, wd_ref, gpost_ref, z_ref, gu_ref, y_ref, o_ref):
        hv = h_ref[...]
        xhat, _ = _rms(hv)
        z = (xhat * gpre_ref[...]).astype(BF16)
        z_ref[...] = z
        gate = lax.dot_general(z, wg_ref[...], NT_DIMS, preferred_element_type=F32).astype(BF16)
        up = lax.dot_general(z, wu_ref[...], NT_DIMS, preferred_element_type=F32).astype(BF16)
        gu_ref[:, :k] = gate
        gu_ref[:, k:] = up
        gate = gate.astype(F32)
        act = (gate * _sigmoid(gate) * up.astype(F32)).astype(BF16)
        y = jnp.dot(act, wd_ref[...], preferred_element_type=F32)
        y_ref[...] = y
        yhat, _ = _rms(y)
        o_ref[...] = hv + yhat * gpost_ref[...]

    return _side_call(
        body, side, name=name, steps=t // tm,
        in_specs=[_rows(tm, d), _full((1, d)), _full((k, d)), _full((k, d)), _full((k, d)), _full((1, d))],
        out_specs=[_rows(tm, d), _rows(tm, 2 * k), _rows(tm, d), _rows(tm, d)],
        out_shape=[_sds((t, d), BF16), _sds((t, 2 * k), BF16), _sds((t, d), F32), _sds((t, d), F32)],
        scratch_shapes=[], args=(h, g_pre, wg_t, wu_t, w_down, g_post))


def _loss_head(h, target, name):
    t, d = h.shape
    tm = min(512, t)

    def body(h_ref, t_ref, dh_ref, l_ref):
        @pl.when(pl.program_id(0) == 0)
        def _():
            l_ref[...] = jnp.zeros_like(l_ref)

        err = h_ref[...] - t_ref[...]
        dh_ref[...] = err * (1.0 / d)
        part = jnp.sum(jnp.sum(err * err, axis=1, keepdims=True), axis=0, keepdims=True) * (0.5 / d)
        l_ref[...] += jnp.broadcast_to(part, l_ref.shape)

    return pl.pallas_call(
        body, name=name, grid=(t // tm,),
        in_specs=[_rows(tm, d), _rows(tm, d)],
        out_specs=[_rows(tm, d), _full((1, 128))],
        out_shape=[_sds((t, d), F32), _sds((1, 128), F32)],
        compiler_params=_params("arbitrary"),
    )(h, target)


def _bwd_out(dout, y, g, w, name, side):
    t, d = dout.shape
    k = w.shape[0]
    tm = 256

    def body(dout_ref, y_ref, g_ref, w_ref, dy_ref, da_ref, dg_ref):
        @pl.when(pl.program_id(0) == 0)
        def _():
            dg_ref[...] = jnp.zeros_like(dg_ref)

        do = dout_ref[...]
        yhat, r = _rms(y_ref[...])
        dg_ref[...] += jnp.sum(do * yhat, axis=0, keepdims=True)
        dy = _rms_bwd(yhat, r, g_ref[...], do).astype(BF16)
        dy_ref[...] = dy
        da_ref[...] = lax.dot_general(dy, w_ref[...], NT_DIMS, preferred_element_type=F32)

    return _side_call(
        body, side, name=name, steps=t // tm,
        in_specs=[_rows(tm, d), _rows(tm, d), _full((1, d)), _full((k, d))],
        out_specs=[_rows(tm, d), _rows(tm, k), _full((1, d))],
        out_shape=[_sds((t, d), BF16), _sds((t, k), F32), _sds((1, d), F32)],
        scratch_shapes=[], args=(dout, y, g, w))


def _ffn_backward(dout, y, g_post, w_down, gu, wg_t, wu_t, h, g_pre, name, side):
    t, d = dout.shape
    k = w_down.shape[0]
    tm = 256

    def body(dout_ref, y_ref, gpost_ref, wd_ref, gu_ref, wg_ref, wu_ref, h_ref, gpre_ref,
             dy_ref, dgu_ref, act_ref, dh_ref, dgpost_ref, dgpre_ref):
        @pl.when(pl.program_id(0) == 0)
        def _():
            dgpost_ref[...] = jnp.zeros_like(dgpost_ref)
            dgpre_ref[...] = jnp.zeros_like(dgpre_ref)

        do = dout_ref[...]
        yhat, r = _rms(y_ref[...])
        dgpost_ref[...] += jnp.sum(do * yhat, axis=0, keepdims=True)
        dy = _rms_bwd(yhat, r, gpost_ref[...], do).astype(BF16)
        dy_ref[...] = dy
        da = lax.dot_general(dy, wd_ref[...], NT_DIMS, preferred_element_type=F32)
        gate = gu_ref[:, :k].astype(F32)
        up = gu_ref[:, k:].astype(F32)
        sg = _sigmoid(gate)
        silu = gate * sg
        act_ref[...] = (silu * up).astype(BF16)
        dgate = (da * up * (sg * (1.0 + gate * (1.0 - sg)))).astype(BF16)
        dup = (da * silu).astype(BF16)
        dgu_ref[:, :k] = dgate
        dgu_ref[:, k:] = dup
        dz = (jnp.dot(dgate, wg_ref[...], preferred_element_type=F32)
              + jnp.dot(dup, wu_ref[...], preferred_element_type=F32))
        hhat, r2 = _rms(h_ref[...])
        dgpre_ref[...] += jnp.sum(dz * hhat, axis=0, keepdims=True)
        dh_ref[...] = do + _rms_bwd(hhat, r2, gpre_ref[...], dz)

    vec = _full((1, d))
    wspec = _full((k, d))
    return _side_call(
        body, side, name=name, steps=t // tm,
        in_specs=[_rows(tm, d), _rows(tm, d), vec, wspec, _rows(tm, 2 * k), wspec, wspec, _rows(tm, d), vec],
        out_specs=[_rows(tm, d), _rows(tm, 2 * k), _rows(tm, k), _rows(tm, d), vec, vec],
        out_shape=[_sds((t, d), BF16), _sds((t, 2 * k), BF16), _sds((t, k), BF16), _sds((t, d), F32),
                   _sds((1, d), F32), _sds((1, d), F32)],
        scratch_shapes=[], args=(dout, y, g_post, w_down, gu, wg_t, wu_t, h, g_pre))


def _bwd_in(dp, ws, transposed, h, g, dout, name, side):
    t, d = h.shape
    n = dp.shape[1]
    tm = 256
    widths = [w.shape[0] if transposed else w.shape[1] for w in ws]

    def body(dp_ref, *refs):
        w_refs = refs[:len(ws)]
        h_ref, g_ref, dout_ref, dh_ref, dg_ref = refs[len(ws):]

        @pl.when(pl.program_id(0) == 0)
        def _():
            dg_ref[...] = jnp.zeros_like(dg_ref)

        dz, at = None, 0
        for w_ref, width in zip(w_refs, widths):
            block = dp_ref[:, at:at + width]
            if transposed:
                part = jnp.dot(block, w_ref[...], preferred_element_type=F32)
            else:
                part = lax.dot_general(block, w_ref[...], NT_DIMS, preferred_element_type=F32)
            dz = part if dz is None else dz + part
            at += width
        hhat, r = _rms(h_ref[...])
        dg_ref[...] += jnp.sum(dz * hhat, axis=0, keepdims=True)
        dh_ref[...] = dout_ref[...] + _rms_bwd(hhat, r, g_ref[...], dz)

    return _side_call(
        body, side, name=name, steps=t // tm,
        in_specs=[_rows(tm, n)] + [_full(w.shape) for w in ws] + [_rows(tm, d), _full((1, d)), _rows(tm, d)],
        out_specs=[_rows(tm, d), _full((1, d))],
        out_shape=[_sds((t, d), F32), _sds((1, d), F32)],
        scratch_shapes=[], args=(dp, *ws, h, g, dout))


def _matmul_tn(a, b, tile_a, tile, name, window=None):
    t, ka = a.shape
    nb = b.shape[1]
    col0 = 0
    if window is not None:
        col0, ka = window

    def body(a_ref, b_ref, o_ref):
        o_ref[...] = lax.dot_general(a_ref[...], b_ref[...], TN_DIMS, preferred_element_type=F32).astype(BF16)

    if tile_a:
        grid = (ka // tile,)
        in_specs = [pl.BlockSpec((t, tile), lambda i: (0, col0 // tile + i)), _full((t, nb))]
        out_specs = pl.BlockSpec((tile, nb), lambda i: (i, 0))
    else:
        grid = (nb // tile,)
        in_specs = [_full((t, ka)), pl.BlockSpec((t, tile), lambda i: (0, i))]
        out_specs = pl.BlockSpec((ka, tile), lambda i: (0, i))
    return pl.pallas_call(
        body, name=name, grid=grid, in_specs=in_specs, out_specs=out_specs,
        out_shape=_sds((ka, nb), BF16), compiler_params=_params("parallel"),
    )(a, b)


def _ml_gate_prep(gt, bif):
    th = jnp.tanh((gt + bif) / GATE_CAP)
    act = GATE_CAP * th
    cum = _cumsum_rows(_log_sigmoid(act))
    lane = lax.broadcasted_iota(jnp.int32, gt.shape, 1)
    x = jnp.where(lane < ML_HEADS, act, cum)
    return x, x.T, th, act


HEADS = range(ML_HEADS)


def _each(fn, *per_head):
    return [fn(*a) for a in zip(*per_head)]


def _ml_chunk_fwd(q, k, v, kt, x, xt, c_in, n_in, m_in):
    causal = (lax.broadcasted_iota(jnp.int32, (CHUNK, CHUNK), 0)
              >= lax.broadcasted_iota(jnp.int32, (CHUNK, CHUNK), 1))
    f = {}
    qh = f["qh"] = [q[:, ML_DK * h:ML_DK * (h + 1)] for h in HEADS]
    kh = f["kh"] = [k[:, ML_DK * h:ML_DK * (h + 1)] for h in HEADS]
    f["vh"] = [v[:, ML_DV * h:ML_DV * (h + 1)] for h in HEADS]
    kth = [kt[ML_DK * h:ML_DK * (h + 1), :] for h in HEADS]
    s = _each(lambda a, b: jnp.dot(a, b, preferred_element_type=F32) * QK_SCALE, qh, kth)
    f["qc"] = _each(lambda a, c: jnp.dot(a, c.astype(BF16), preferred_element_type=F32) * QK_SCALE, qh, c_in)
    bcol = f["bcol"] = [x[:, ML_HEADS + h:ML_HEADS + h + 1] for h in HEADS]
    licol = f["licol"] = [x[:, h:h + 1] for h in HEADS]
    brow = f["brow"] = [xt[ML_HEADS + h:ML_HEADS + h + 1, :] for h in HEADS]
    lirow = [xt[h:h + 1, :] for h in HEADS]
    dmat = _each(lambda bc, br, lr: jnp.where(causal, bc - br + lr, NEG_BIG), bcol, brow, lirow)
    inter = _each(lambda bc, m: bc + m, bcol, m_in)
    mt = _each(lambda d, i: jnp.maximum(jnp.max(d, axis=1, keepdims=True), i), dmat, inter)
    wt = f["wt"] = _each(lambda d, m: jnp.exp(d - m), dmat, mt)
    p = f["p"] = _each(lambda a, b: a * b, wt, s)
    winter = f["winter"] = _each(lambda i, m: jnp.exp(i - m), inter, mt)
    qf = f["qf"] = [a.astype(F32) for a in qh]
    qn = f["qn"] = _each(lambda a, n: jnp.sum(a * n, axis=1, keepdims=True) * QK_SCALE, qf, n_in)
    den = f["den"] = _each(lambda a, w, b: jnp.sum(a, axis=1, keepdims=True) + w * b, p, winter, qn)
    emt = f["emt"] = [jnp.exp(-m) for m in mt]
    f["nrm"] = _each(lambda d, e: jnp.maximum(jnp.abs(d), e), den, emt)
    gtot = [bc[CHUNK - 1:CHUNK, :] for bc in bcol]
    a_col = _each(lambda g, bc, lc: g - bc + lc, gtot, bcol, licol)
    a_row = _each(lambda g, br, lr: g - br + lr, gtot, brow, lirow)
    m_new = f["m_new"] = _each(lambda g, m, a: jnp.maximum(g + m, jnp.max(a, axis=1, keepdims=True)),
                               gtot, m_in, a_row)
    f["decay"] = _each(lambda g, m, mn: jnp.exp(g + m - mn), gtot, m_in, m_new)
    wkf = f["wkf"] = _each(lambda a, mn: jnp.exp(a - mn), a_col, m_new)
    f["kw"] = _each(lambda a, w: a.astype(F32) * w, kh, wkf)
    f["ktw"] = _each(lambda a, ar, mn: (a.astype(F32) * jnp.exp(ar - mn)).astype(BF16), kth, a_row, m_new)
    return f


def _ml_specs(nc, rev):
    def at(col):
        if rev:
            return lambda i: (nc - 1 - i, col)
        return lambda i: (i, col)

    return [pl.BlockSpec((CHUNK, ML_QK), at(0)), pl.BlockSpec((CHUNK, ML_QK), at(1)),
            pl.BlockSpec((CHUNK, ML_V), at(1)), pl.BlockSpec((CHUNK, ML_V), at(0)),
            pl.BlockSpec((CHUNK, 128), at(ML_V // 128))]


def _mlstm_fwd(qkv, og, bif, hn, side):
    t = qkv.shape[0]
    nc = t // CHUNK

    def body(q_ref, k_ref, v_ref, o_ref, gt_ref, bif_ref, hn_ref, y_ref, hs_ref, cst_ref, nst_ref, mst_ref,
             c_sc, n_sc, m_sc):
        @pl.when(pl.program_id(0) == 0)
        def _():
            c_sc[...] = jnp.zeros_like(c_sc)
            n_sc[...] = jnp.zeros_like(n_sc)
            m_sc[...] = jnp.full_like(m_sc, ML_M_INIT)

        cst_ref[0] = c_sc[...]
        nst_ref[0] = n_sc[...]
        mst_ref[0] = m_sc[...]
        x, xt, _, _ = _ml_gate_prep(gt_ref[...], bif_ref[...])
        q, k, v = q_ref[...], k_ref[...], v_ref[...]
        c_all, n_all, m_all = c_sc[...], n_sc[...], m_sc[...]
        c_in = [c_all[ML_DK * h:ML_DK * (h + 1), :] for h in HEADS]
        n_in = [n_all[h:h + 1, :] for h in HEADS]
        m_in = [m_all[h:h + 1, 0:1] for h in HEADS]
        f = _ml_chunk_fwd(q, k, v, k.T, x, xt, c_in, n_in, m_in)
        num = _each(lambda p, vh, w, qc: jnp.dot(p.astype(BF16), vh, preferred_element_type=F32) + w * qc,
                    f["p"], f["vh"], f["winter"], f["qc"])
        hh = _each(lambda a, b: a / b, num, f["nrm"])
        hhat = [_rms(a)[0] for a in hh]
        c_new = _each(lambda d, c, kw, vh: d * c + jnp.dot(kw, vh, preferred_element_type=F32),
                      f["decay"], c_in, f["ktw"], f["vh"])
        n_new = _each(lambda d, n, kw: d * n + jnp.sum(kw, axis=0, keepdims=True), f["decay"], n_in, f["kw"])
        for h in HEADS:
            vs = slice(ML_DV * h, ML_DV * (h + 1))
            hs_ref[:, vs] = hh[h]
            y_ref[:, vs] = (hhat[h] * hn_ref[:, vs] * _sigmoid(o_ref[:, vs])).astype(BF16)
            c_sc[ML_DK * h:ML_DK * (h + 1), :] = c_new[h]
            n_sc[h:h + 1, :] = n_new[h]
            m_sc[h:h + 1, :] = jnp.broadcast_to(f["m_new"][h], (1, 128))

    return _side_call(
        body, side, name="mlstm_fwd", steps=nc,
        in_specs=_ml_specs(nc, False) + [_full((1, 128)), _full((1, ML_V))],
        out_specs=[_rows(CHUNK, ML_V), _rows(CHUNK, ML_V),
                   pl.BlockSpec((1, ML_HEADS * ML_DK, ML_DV), lambda i: (i, 0, 0)),
                   pl.BlockSpec((1, ML_HEADS, ML_DK), lambda i: (i, 0, 0)),
                   pl.BlockSpec((1, ML_HEADS, 128), lambda i: (i, 0, 0))],
        out_shape=[_sds((t, ML_V), BF16), _sds((t, ML_V), F32),
                   _sds((nc, ML_HEADS * ML_DK, ML_DV), F32), _sds((nc, ML_HEADS, ML_DK), F32),
                   _sds((nc, ML_HEADS, 128), F32)],
        scratch_shapes=[pltpu.VMEM((ML_HEADS * ML_DK, ML_DV), F32), pltpu.VMEM((ML_HEADS, ML_DK), F32),
                        pltpu.VMEM((ML_HEADS, 128), F32)],
        args=(qkv, qkv, qkv, og, og, bif, hn))


def _mlstm_bwd(qkv, og, bif, hn, hs, cst, nst, mst, dy, side):
    t = qkv.shape[0]
    nc = t // CHUNK

    def body(q_ref, k_ref, v_ref, o_ref, gt_ref, bif_ref, hn_ref, hs_ref, cst_ref, nst_ref, mst_ref, dy_ref,
             dp_ref, dhn_ref, dbif_ref, dc_sc, dct_sc, dn_sc):
        @pl.when(pl.program_id(0) == 0)
        def _():
            dc_sc[...] = jnp.zeros_like(dc_sc)
            dct_sc[...] = jnp.zeros_like(dct_sc)
            dn_sc[...] = jnp.zeros_like(dn_sc)
            dhn_ref[...] = jnp.zeros_like(dhn_ref)
            dbif_ref[...] = jnp.zeros_like(dbif_ref)

        x, xt, th, act = _ml_gate_prep(gt_ref[...], bif_ref[...])
        q, k, v = q_ref[...], k_ref[...], v_ref[...]
        qt, vt = q.T, v.T
        rows = lax.broadcasted_iota(jnp.int32, (CHUNK, CHUNK), 0)
        cols = lax.broadcasted_iota(jnp.int32, (CHUNK, CHUNK), 1)
        lane = lax.broadcasted_iota(jnp.int32, (CHUNK, 128), 1)
        row = lax.broadcasted_iota(jnp.int32, (CHUNK, 1), 0)
        as_row = lambda col: jnp.sum(jnp.where(rows == cols, col, 0.0), axis=0, keepdims=True)
        mm = lambda a, b: jnp.dot(a, b, preferred_element_type=F32)
        bf = lambda a: a.astype(BF16)
        ksl = [slice(ML_DK * h, ML_DK * (h + 1)) for h in HEADS]
        vsl = [slice(ML_DV * h, ML_DV * (h + 1)) for h in HEADS]
        c_in = [cst_ref[0, s, :] for s in ksl]
        n_in = [nst_ref[0, h:h + 1, :] for h in HEADS]
        m_in = [mst_ref[0, h:h + 1, 0:1] for h in HEADS]
        dc_all, dct_all, dn_all = dc_sc[...], dct_sc[...], dn_sc[...]
        dcn = [dc_all[s, :] for s in ksl]
        dcn_t = [dct_all[s, :] for s in vsl]
        dnn = [dn_all[h:h + 1, :] for h in HEADS]
        f = _ml_chunk_fwd(q, k, v, k.T, x, xt, c_in, n_in, m_in)
        qh, kh, vh, p, winter, decay = f["qh"], f["kh"], f["vh"], f["p"], f["winter"], f["decay"]
        qth = [qt[s, :] for s in ksl]
        vth = [vt[s, :] for s in vsl]
        c_t = [bf(c.T) for c in c_in]
        dmat_t = _each(lambda br, bc, lc: jnp.where(rows <= cols, br - bc + lc, NEG_BIG),
                       f["brow"], f["bcol"], f["licol"])
        inter_row = _each(lambda br, m: br + m, f["brow"], m_in)
        mt_row = _each(lambda d, i: jnp.maximum(jnp.max(d, axis=0, keepdims=True), i), dmat_t, inter_row)
        wt_t = _each(lambda d, m: jnp.exp(d - m), dmat_t, mt_row)
        p_t = _each(lambda w, a, b: w * (mm(a, b) * QK_SCALE), wt_t, kh, qth)
        winter_row = _each(lambda i, m: jnp.exp(i - m), inter_row, mt_row)
        hh = [hs_ref[:, s] for s in vsl]
        hn_h = [hn_ref[:, s] for s in vsl]
        sg = [_sigmoid(o_ref[:, s]) for s in vsl]
        dyh = [dy_ref[:, s] for s in vsl]
        norm = [_rms(a) for a in hh]
        hhat, r = [a for a, _ in norm], [b for _, b in norm]
        dyn = _each(lambda a, b: a * b, dyh, sg)
        do = _each(lambda d, hx, g, s: d * hx * g * s * (1.0 - s), dyh, hhat, hn_h, sg)
        dhn = _each(lambda a, b: jnp.sum(a * b, axis=0, keepdims=True), dyn, hhat)
        dh = _each(_rms_bwd, hhat, r, hn_h, dyn)
        inv = [1.0 / a for a in f["nrm"]]
        dnum = _each(lambda a, b: a * b, dh, inv)
        dnrm = _each(lambda a, b, c: -jnp.sum(a * b, axis=1, keepdims=True) * c, dh, hh, inv)
        dden = _each(lambda d, e, g: jnp.where(jnp.abs(d) > e, g * jnp.sign(d), 0.0), f["den"], f["emt"], dnrm)
        dnb = [bf(a) for a in dnum]
        dnt = [bf(a.T) for a in dnum]
        rmat = _each(lambda a, b, d: mm(a, b) + d, dnb, vth, dden)
        rmat_t = _each(lambda a, b, d: mm(a, b) + as_row(d), vh, dnt, dden)
        ds = _each(lambda w, a: bf(w * a), f["wt"], rmat)
        ds_t = _each(lambda w, a: bf(w * a), wt_t, rmat_t)
        dv = _each(lambda a, b: mm(bf(a), b), p_t, dnb)
        dqs = _each(lambda s, kk, w, d, ct, dd, n: mm(s, kk) + w * (mm(d, ct) + dd * n),
                    ds, kh, winter, dnb, c_t, dden, n_in)
        dk = _each(lambda s, a: mm(s, a) * QK_SCALE, ds_t, qh)
        dinter = _each(lambda qc, dn_, qn, dd, w: (jnp.sum(qc * dn_, axis=1, keepdims=True) + qn * dd) * w,
                       f["qc"], dnum, f["qn"], dden, winter)
        wq = _each(lambda w, a: w * a * QK_SCALE, winter, f["qf"])
        wq_t = _each(lambda w, a: bf(w * a.astype(F32) * QK_SCALE), winter_row, qth)
        dc_loc = _each(mm, wq_t, dnb)
        dct_loc = _each(lambda a, b: mm(a, bf(b)), dnt, wq)
        dn_loc = _each(lambda a, d: jnp.sum(a * d, axis=0, keepdims=True), wq, dden)
        cs_q = _each(lambda a, b: jnp.sum(a * b, axis=1, keepdims=True), p_t, rmat_t)
        db = _each(lambda a, b, di, cs: jnp.sum(a * b, axis=1, keepdims=True) + di - cs, p, rmat, dinter, cs_q)
        ddecay = _each(lambda dc_, c, dn_, n: jnp.sum(jnp.sum(dc_ * c, axis=1, keepdims=True), axis=0, keepdims=True)
                       + jnp.sum(dn_ * n, axis=1, keepdims=True), dcn, c_in, dnn, n_in)
        dkw = _each(lambda a, b, n: mm(a, bf(b)) + n, vh, dcn_t, dnn)
        dk = _each(lambda a, w, b: a + w * b, dk, f["wkf"], dkw)
        da = _each(lambda a, kk, w: jnp.sum(a * kk.astype(F32), axis=1, keepdims=True) * w, dkw, kh, f["wkf"])
        dv = _each(lambda a, kw, dc_: a + mm(bf(kw), bf(dc_)), dv, f["kw"], dcn)
        dgtot = _each(lambda a, dd, d: jnp.sum(a, axis=0, keepdims=True) + dd * d, da, ddecay, decay)
        db = _each(lambda a, b, g: a - b + jnp.where(row == CHUNK - 1, g, 0.0), db, da, dgtot)
        dli = _each(lambda a, b: a + b, cs_q, da)
        dc_new = _each(lambda d, a, b: d * a + b, decay, dcn, dc_loc)
        dct_new = _each(lambda d, a, b: d * a + b, decay, dcn_t, dct_loc)
        dn_new = _each(lambda d, a, b: d * a + b, decay, dnn, dn_loc)
        dx = jnp.zeros((CHUNK, 128), F32)
        for h in HEADS:
            dhn_ref[:, vsl[h]] += dhn[h]
            dc_sc[ksl[h], :] = dc_new[h]
            dct_sc[vsl[h], :] = dct_new[h]
            dn_sc[h:h + 1, :] = dn_new[h]
            dp_ref[:, ksl[h]] = bf(dqs[h] * QK_SCALE)
            dp_ref[:, ML_QK + ML_DK * h:ML_QK + ML_DK * (h + 1)] = bf(dk[h])
            dp_ref[:, 2 * ML_QK + ML_DV * h:2 * ML_QK + ML_DV * (h + 1)] = bf(dv[h])
            dp_ref[:, 2 * ML_QK + ML_V + ML_DV * h:2 * ML_QK + ML_V + ML_DV * (h + 1)] = bf(do[h])
            dx = jnp.where(lane == h, dli[h], dx)
            dx = jnp.where(lane == ML_HEADS + h, db[h], dx)
        dlf = _cumsum_rows(dx, reverse=True)
        dact = jnp.where(lane < ML_HEADS, dx, dlf * _sigmoid(-act))
        dz = dact * (1.0 - th * th)
        dp_ref[:, 2 * ML_QK + 2 * ML_V:] = dz.astype(BF16)
        dbif_ref[...] += jnp.sum(dz, axis=0, keepdims=True)

    rev = lambda i: (nc - 1 - i, 0)
    rev3 = lambda i: (nc - 1 - i, 0, 0)
    return _side_call(
        body, side, name="mlstm_bwd", steps=nc,
        in_specs=_ml_specs(nc, True) + [
            _full((1, 128)), _full((1, ML_V)), pl.BlockSpec((CHUNK, ML_V), rev),
            pl.BlockSpec((1, ML_HEADS * ML_DK, ML_DV), rev3), pl.BlockSpec((1, ML_HEADS, ML_DK), rev3),
            pl.BlockSpec((1, ML_HEADS, 128), rev3), pl.BlockSpec((CHUNK, ML_V), rev)],
        out_specs=[pl.BlockSpec((CHUNK, ML_IN_PAD), rev), _full((1, ML_V)), _full((1, 128))],
        out_shape=[_sds((t, ML_IN_PAD), BF16), _sds((1, ML_V), F32), _sds((1, 128), F32)],
        scratch_shapes=[pltpu.VMEM((ML_HEADS * ML_DK, ML_DV), F32), pltpu.VMEM((ML_HEADS * ML_DV, ML_DK), F32),
                        pltpu.VMEM((ML_HEADS, ML_DK), F32)],
        args=(qkv, qkv, qkv, og, og, bif, hn, hs, cst, nst, mst, dy))


LRU_TM = 256
GELU_K = math.sqrt(2.0 / math.pi)
GELU_C = 0.044715


def _gelu(x):
    th = jnp.tanh(GELU_K * (x + GELU_C * x * x * x))
    return 0.5 * x * (1.0 + th), th


def _neg_expm1(x):
    series = -x * (1.0 + x * (0.5 + x * (1.0 / 6.0 + x * (1.0 / 24.0))))
    return jnp.where(x > -0.05, series, 1.0 - jnp.exp(x))


def _block_diag_dot(a, w_ref, dims):
    parts = [lax.dot_general(a[:, LRU_BLOCK * n:LRU_BLOCK * (n + 1)], w_ref[n], dims, preferred_element_type=F32)
             for n in range(LRU_BLOCKS)]
    return jnp.concatenate(parts, axis=1)


def _lru_gates(u, r, ig, lam):
    ls = _log_sigmoid(lam)
    la = LRU_C * r * ls
    a = jnp.exp(la)
    em = _neg_expm1(2.0 * la)
    mult = jnp.sqrt(em)
    return ls, a, em, mult


def _lru_fwd(proj, cw, cb, wa, ba, wx, bx, lam, side):
    t = proj.shape[0]
    w = D_MODEL
    tm = min(LRU_TM, t)

    def body(gb_ref, up_ref, cw_ref, cb_ref, wa_ref, ba_ref, wx_ref, bx_ref, lam_ref,
             y_ref, u_ref, r_ref, i_ref, h_ref, tail_sc, hprev_sc):
        @pl.when(pl.program_id(0) == 0)
        def _():
            tail_sc[...] = jnp.zeros_like(tail_sc)
            hprev_sc[...] = jnp.zeros_like(hprev_sc)

        up = up_ref[...]
        ext = jnp.concatenate([tail_sc[...], up], axis=0)
        u = cb_ref[...] + cw_ref[CONV_WIDTH - 1:CONV_WIDTH, :] * up
        for s in range(1, CONV_WIDTH):
            u = u + cw_ref[CONV_WIDTH - 1 - s:CONV_WIDTH - s, :] * pltpu.roll(ext, s, 0)[8:8 + tm]
        tail_sc[...] = up[tm - 8:tm]
        ub = u.astype(BF16)
        r = _sigmoid(_block_diag_dot(ub, wa_ref, (((1,), (0,)), ((), ()))) + ba_ref[...])
        ig = _sigmoid(_block_diag_dot(ub, wx_ref, (((1,), (0,)), ((), ()))) + bx_ref[...])
        _, a, _, mult = _lru_gates(u, r, ig, lam_ref[...])
        h = _scan_rows(a, mult * ig * u, hprev_sc[0:1, :])
        hprev_sc[0:1, :] = h[tm - 1:tm]
        u_ref[...] = u
        r_ref[...] = r
        i_ref[...] = ig
        h_ref[...] = h
        gel, _ = _gelu(gb_ref[...])
        y_ref[...] = (h * gel).astype(BF16)

    vec = _full((1, w))
    wspec = _full((LRU_BLOCKS, LRU_BLOCK, LRU_BLOCK))
    return _side_call(
        body, side, name="lru_fwd", steps=t // tm,
        in_specs=[pl.BlockSpec((tm, w), lambda i: (i, 0)), pl.BlockSpec((tm, w), lambda i: (i, 1)),
                  _full((CONV_WIDTH, w)), vec, wspec, vec, wspec, vec, vec],
        out_specs=[_rows(tm, w)] * 5,
        out_shape=[_sds((t, w), BF16)] + [_sds((t, w), F32)] * 4,
        scratch_shapes=[pltpu.VMEM((8, w), F32), pltpu.VMEM((8, w), F32)],
        args=(proj, proj, cw, cb, wa, ba, wx, bx, lam))


def _lru_bwd(proj, cw, wa, wx, lam, u, r, ig, h, dy, side):
    t = proj.shape[0]
    w = D_MODEL
    tm = min(LRU_TM, t)
    nt = t // tm

    def body(gb_ref, up_ref, cw_ref, wa_ref, wx_ref, lam_ref, u_ref, r_ref, i_ref, h_ref, hp_ref, dy_ref,
             dp_ref, dcw_ref, dcb_ref, dwa_ref, dba_ref, dwx_ref, dbx_ref, dlam_ref, carry_sc, dutail_sc, dls_sc):
        step = pl.program_id(0)

        @pl.when(step == 0)
        def _():
            carry_sc[...] = jnp.zeros_like(carry_sc)
            dutail_sc[...] = jnp.zeros_like(dutail_sc)
            dls_sc[...] = jnp.zeros_like(dls_sc)
            for ref in (dcw_ref, dcb_ref, dwa_ref, dba_ref, dwx_ref, dbx_ref):
                ref[...] = jnp.zeros_like(ref)

        row = lax.broadcasted_iota(jnp.int32, (tm, w), 0)
        u_t, r_t, i_t, h_t = u_ref[...], r_ref[...], i_ref[...], h_ref[...]
        ls, a, em, mult = _lru_gates(u_t, r_t, i_t, lam_ref[...])
        gb = gb_ref[...]
        gel, th = _gelu(gb)
        dyv = dy_ref[...]
        dgb = dyv * h_t * (0.5 * (1.0 + th) + 0.5 * gb * (1.0 - th * th) * GELU_K * (1.0 + 3.0 * GELU_C * gb * gb))
        a_next = jnp.where(row < tm - 1, pltpu.roll(a, tm - 1, 0), 1.0)
        g = _scan_rows(a_next, dyv * gel, carry_sc[0:1, :], reverse=True)
        carry_sc[0:1, :] = a[0:1] * g[0:1]
        has_prev = jnp.where(step == nt - 1, 0.0, 1.0)
        h_prev = jnp.where(row >= 1, pltpu.roll(h_t, 1, 0), hp_ref[7:8, :] * has_prev)
        dmult = g * i_t * u_t
        dig = g * mult * u_t
        du = g * mult * i_t
        dla = g * h_prev * a - dmult * (1.0 - em) / mult
        dls_sc[0:1, :] += jnp.sum(dla * r_t, axis=0, keepdims=True) * LRU_C
        dpa = dla * (LRU_C * ls) * r_t * (1.0 - r_t)
        dpx = dig * i_t * (1.0 - i_t)
        dba_ref[...] += jnp.sum(dpa, axis=0, keepdims=True)
        dbx_ref[...] += jnp.sum(dpx, axis=0, keepdims=True)
        ub = u_t.astype(BF16)
        dpab = dpa.astype(BF16)
        dpxb = dpx.astype(BF16)
        for n in range(LRU_BLOCKS):
            cs = slice(LRU_BLOCK * n, LRU_BLOCK * (n + 1))
            dwa_ref[n] += lax.dot_general(ub[:, cs], dpab[:, cs], TN_DIMS, preferred_element_type=F32)
            dwx_ref[n] += lax.dot_general(ub[:, cs], dpxb[:, cs], TN_DIMS, preferred_element_type=F32)
        du = du + _block_diag_dot(dpab, wa_ref, NT_DIMS) + _block_diag_dot(dpxb, wx_ref, NT_DIMS)
        dcb_ref[...] += jnp.sum(du, axis=0, keepdims=True)
        ext = jnp.concatenate([du, dutail_sc[...]], axis=0)
        up = up_ref[...]
        dup = cw_ref[CONV_WIDTH - 1:CONV_WIDTH, :] * du
        dcw_ref[CONV_WIDTH - 1:CONV_WIDTH, :] += jnp.sum(up * du, axis=0, keepdims=True)
        for s in range(1, CONV_WIDTH):
            du_s = pltpu.roll(ext, tm + 8 - s, 0)[0:tm]
            dup = dup + cw_ref[CONV_WIDTH - 1 - s:CONV_WIDTH - s, :] * du_s
            dcw_ref[CONV_WIDTH - 1 - s:CONV_WIDTH - s, :] += jnp.sum(up * du_s, axis=0, keepdims=True)
        dutail_sc[...] = du[0:8]
        dp_ref[:, :w] = dgb.astype(BF16)
        dp_ref[:, w:] = dup.astype(BF16)

        @pl.when(step == nt - 1)
        def _():
            dlam_ref[...] = dls_sc[0:1, :] * _sigmoid(-lam_ref[...])

    rev = lambda col: (lambda i: (nt - 1 - i, col))
    vec = _full((1, w))
    wspec = _full((LRU_BLOCKS, LRU_BLOCK, LRU_BLOCK))
    tile = pl.BlockSpec((tm, w), rev(0))
    prev8 = pl.BlockSpec((8, w), lambda i: (jnp.maximum((nt - 1 - i) * (tm // 8) - 1, 0), 0))
    return _side_call(
        body, side, name="lru_bwd", steps=nt,
        in_specs=[tile, pl.BlockSpec((tm, w), rev(1)), _full((CONV_WIDTH, w)), wspec, wspec, vec,
                  tile, tile, tile, tile, prev8, tile],
        out_specs=[pl.BlockSpec((tm, 2 * w), rev(0)), _full((CONV_WIDTH, w)), vec, wspec, vec, wspec, vec, vec],
        out_shape=[_sds((t, 2 * w), BF16), _sds((CONV_WIDTH, w), F32), _sds((1, w), F32),
                   _sds((LRU_BLOCKS, LRU_BLOCK, LRU_BLOCK), F32), _sds((1, w), F32),
                   _sds((LRU_BLOCKS, LRU_BLOCK, LRU_BLOCK), F32), _sds((1, w), F32), _sds((1, w), F32)],
        scratch_shapes=[pltpu.VMEM((8, w), F32), pltpu.VMEM((8, w), F32), pltpu.VMEM((8, w), F32)],
        args=(proj, proj, cw, wa, wx, lam, u, r, ig, h, h, dy))


LANES = 1024
HALF_FFN = D_FF // N_CHIPS
GROUPS = {
    "ml": (("ml_w_in", 1024, ML_IN // N_CHIPS), ("ml_w_out", 256, 1024)),
    "lru": (("lru_w_in", 1024, 512),
            ("lru_gates", 2 * LRU_BLOCKS * 64, LRU_BLOCK),
            ("lru_w_out", 256, 1024)),
    "ffn0": (("ffn_g0", HALF_FFN, 1024), ("ffn_u0", HALF_FFN, 1024), ("ffn_down0", HALF_FFN, 1024)),
    "ffn1": (("ffn_g1", HALF_FFN, 1024), ("ffn_u1", HALF_FFN, 1024), ("ffn_down1", HALF_FFN, 1024)),
}

SMALL_ROWS = 24
ROW_LOSS, ROW_HEAD_NORM, ROW_B_IF, ROW_LRU = 8, 9, 10, 11


def _row_tile(rows, cols, itemsize, budget=3 << 19):
    best = 16
    for t in range(16, rows + 1, 16):
        if rows % t == 0 and t * cols * itemsize <= budget:
            best = t
    return best


def _chip_peers():
    x, y, c = lax.axis_index("x"), lax.axis_index("y"), lax.axis_index("c")
    return x, y, c, [(1 - x, y), (x, 1 - y), (1 - x, 1 - y)]


HBM_SPEC = pl.BlockSpec(memory_space=pltpu.HBM)


def _remote(src, dst, send_sems, recv_sems, k, to):
    return pltpu.make_async_remote_copy(src_ref=src, dst_ref=dst, send_sem=send_sems.at[k], recv_sem=recv_sems.at[k],
                                        device_id=to, device_id_type=MESH)


GATHER_SEMS = 7


def _gather_copies(kind, specs, part_refs, out_refs, send_sems, recv_sems):
    x, y, c, chips = _chip_peers()
    me = 2 * x + y
    sib = (x, y, 1 - c)
    copy = functools.partial(_remote, send_sems=send_sems, recv_sems=recv_sems)
    out = []
    for p, (_, rows, _) in enumerate(specs):
        mine = pl.ds(c * (rows // 2), rows // 2)
        theirs = pl.ds((1 - c) * (rows // 2), rows // 2)
        base = GATHER_SEMS * p
        for j, (cx, cy) in enumerate(chips):
            land = out_refs[p].at[2 * cx + cy, mine]
            other = out_refs[p].at[2 * cx + cy, theirs]
            if kind == "first":
                out.append(copy(part_refs[p].at[mine], out_refs[p].at[me, mine], k=base + j, to=(cx, cy, c)))
            elif kind == "landed":
                out.append(copy(land, land, k=base + j, to=sib))
            elif kind == "forward":
                out.append(copy(land, land, k=base + 3 + j, to=sib))
            else:
                out.append(copy(other, other, k=base + 3 + j, to=sib))
        if kind in ("first", "arriving"):
            out.append(copy(part_refs[p], out_refs[p].at[me], k=base + 6, to=sib))
    return out


def _gather_start(specs, *refs):
    for cp in _gather_copies("first", specs, *refs):
        cp.start()


def _gather_finish(specs, *refs):
    forwards = _gather_copies("forward", specs, *refs)
    for land, fwd in zip(_gather_copies("landed", specs, *refs), forwards):
        land.wait_recv()
        fwd.start()
    for cp in _gather_copies("arriving", specs, *refs):
        cp.wait_recv()
    for cp in _gather_copies("first", specs, *refs) + forwards:
        cp.wait_send()


def _gather_shapes(specs):
    return [_sds((N_CHIPS, rows, cols), BF16) for _, rows, cols in specs]


def _gather_side(specs, parts):
    return _Side(inputs=list(parts), out_shape=_gather_shapes(specs), n_sems=GATHER_SEMS * len(specs),
                 start=functools.partial(_gather_start, specs), finish=functools.partial(_gather_finish, specs))


def _gather_weights(specs, parts, small):
    n = len(specs)

    def body(*refs):
        part_refs, small_ref = refs[:n], refs[n]
        out_refs, outs_ref = refs[n + 1:2 * n + 1], refs[2 * n + 1]
        send_sems, recv_sems, small_send, small_recv, loc_sem = refs[2 * n + 2:]
        x, y, c, chips = _chip_peers()
        me = 2 * x + y
        local = pltpu.make_async_copy(small_ref, outs_ref.at[me], loc_sem.at[0])
        local.start()
        _gather_start(specs, part_refs, out_refs, send_sems, recv_sems)
        sent = [_remote(small_ref, outs_ref.at[me], small_send, small_recv, j, (cx, cy, c))
                for j, (cx, cy) in enumerate(chips)]
        for cp in sent:
            cp.start()
        _gather_finish(specs, part_refs, out_refs, send_sems, recv_sems)
        for j, (cx, cy) in enumerate(chips):
            _remote(small_ref, outs_ref.at[2 * cx + cy], small_send, small_recv, j, (cx, cy, c)).wait_recv()
        for cp in sent:
            cp.wait_send()
        local.wait()

    dma = pltpu.SemaphoreType.DMA
    return pl.pallas_call(
        body, name="gather_weights",
        in_specs=[HBM_SPEC] * (n + 1), out_specs=[HBM_SPEC] * (n + 1),
        out_shape=_gather_shapes(specs) + [_sds((N_CHIPS,) + small.shape, small.dtype)],
        scratch_shapes=[dma((GATHER_SEMS * n,)), dma((GATHER_SEMS * n,)), dma((3,)), dma((3,)), dma((1,))],
    )(*parts, small)


def _exchange_copies(specs, g_refs, out_refs, send_sems, recv_sems):
    x, y, c, _ = _chip_peers()
    return [_remote(g_refs[p].at[:, pl.ds((1 - c) * (rows // 2), rows // 2)], out_refs[p], send_sems, recv_sems, p,
                    (x, y, 1 - c)) for p, (_, rows, _) in enumerate(specs)]


def _exchange_start(specs, *refs):
    for cp in _exchange_copies(specs, *refs):
        cp.start()


def _exchange_finish(specs, *refs):
    for cp in _exchange_copies(specs, *refs):
        cp.wait()


def _exchange_shapes(specs):
    return [_sds((N_CHIPS, rows // 2, cols), BF16) for _, rows, cols in specs]


def _exchange_side(specs, gparts):
    return _Side(inputs=list(gparts), out_shape=_exchange_shapes(specs), n_sems=len(specs),
                 start=functools.partial(_exchange_start, specs), finish=functools.partial(_exchange_finish, specs))


def _exchange_halves(specs, gparts, name):
    n = len(specs)

    def body(*refs):
        _exchange_start(specs, refs[:n], refs[n:2 * n], *refs[2 * n:])
        _exchange_finish(specs, refs[:n], refs[n:2 * n], *refs[2 * n:])

    return pl.pallas_call(
        body, name=name, in_specs=[HBM_SPEC] * n, out_specs=[HBM_SPEC] * n, out_shape=_exchange_shapes(specs),
        scratch_shapes=[pltpu.SemaphoreType.DMA((n,)), pltpu.SemaphoreType.DMA((n,))],
    )(*gparts)


def _add_halves(g, recv, pos, name):
    _, half, cols = recv.shape
    tr = _row_tile(half, cols, 2)
    tiles = half // tr

    def body(pos_ref, a_ref, b_ref, o_ref):
        o_ref[...] = (a_ref[...].astype(F32) + b_ref[...].astype(F32)).astype(BF16)

    spec = pl.BlockSpec((1, tr, cols), lambda k, i, pos_ref: (k, i, 0))
    return pl.pallas_call(
        body, name=name,
        grid_spec=pltpu.PrefetchScalarGridSpec(
            num_scalar_prefetch=1, grid=(N_CHIPS, tiles),
            in_specs=[pl.BlockSpec((1, tr, cols), lambda k, i, pos_ref: (k, pos_ref[0] * tiles + i, 0)), spec],
            out_specs=spec),
        out_shape=_sds((N_CHIPS, half, cols), BF16),
        compiler_params=_params("parallel", "parallel"),
    )(pos, g, recv)


def _scatter_copies(n, s1_refs, recv_refs, send_sems, recv_sems):
    x, y, c, chips = _chip_peers()
    return [_remote(s1_refs[p].at[2 * cx + cy], recv_refs[p].at[j], send_sems, recv_sems, 3 * p + j, (cx, cy, c))
            for p in range(n) for j, (cx, cy) in enumerate(chips)]


def _scatter_start(n, s1_refs, recv_refs, send_sems, recv_sems):
    for cp in _scatter_copies(n, s1_refs, recv_refs, send_sems, recv_sems):
        cp.start()


def _scatter_finish(n, s1_refs, recv_refs, send_sems, recv_sems):
    for cp in _scatter_copies(n, s1_refs, recv_refs, send_sems, recv_sems):
        cp.wait()


def _scatter_shapes(s1):
    return [_sds((3,) + a.shape[1:], a.dtype) for a in s1]


def _scatter_side(s1):
    n = len(s1)
    return _Side(inputs=list(s1), out_shape=_scatter_shapes(s1), n_sems=3 * n,
                 start=functools.partial(_scatter_start, n), finish=functools.partial(_scatter_finish, n))


def _share_small(small):
    flips = [(fx, fy, fc) for fx in (0, 1) for fy in (0, 1) for fc in (0, 1)][1:]

    def body(small_ref, all_ref, send_sems, recv_sems, loc_sem):
        x, y, c, _ = _chip_peers()
        my_slot = all_ref.at[4 * x + 2 * y + c]
        local = pltpu.make_async_copy(small_ref, my_slot, loc_sem.at[0])
        local.start()
        peers = [(1 - x if fx else x, 1 - y if fy else y, 1 - c if fc else c) for fx, fy, fc in flips]
        sent = [_remote(small_ref, my_slot, send_sems, recv_sems, i, p) for i, p in enumerate(peers)]
        for cp in sent:
            cp.start()
        for i, (px, py, pc) in enumerate(peers):
            _remote(small_ref, all_ref.at[4 * px + 2 * py + pc], send_sems, recv_sems, i, peers[i]).wait_recv()
        for cp in sent:
            cp.wait_send()
        local.wait()

    dma = pltpu.SemaphoreType.DMA
    return pl.pallas_call(
        body, name="share_small", in_specs=[HBM_SPEC], out_specs=HBM_SPEC,
        out_shape=_sds((8,) + small.shape, small.dtype), scratch_shapes=[dma((7,)), dma((7,)), dma((1,))],
    )(small)


def _sum_chips(s1, recv, pos, name):
    _, half, cols = recv.shape
    tr = _row_tile(half, cols, 4)
    tiles = half // tr

    def body(pos_ref, a_ref, b_ref, o_ref):
        acc = a_ref[0].astype(F32)
        for j in range(3):
            acc = acc + b_ref[j].astype(F32)
        o_ref[...] = acc

    return pl.pallas_call(
        body, name=name,
        grid_spec=pltpu.PrefetchScalarGridSpec(
            num_scalar_prefetch=1, grid=(tiles,),
            in_specs=[pl.BlockSpec((1, tr, cols), lambda i, pos_ref: (pos_ref[1], i, 0)),
                      pl.BlockSpec((3, tr, cols), lambda i, pos_ref: (0, i, 0))],
            out_specs=pl.BlockSpec((tr, cols), lambda i, pos_ref: (pos_ref[0] * tiles + i, 0))),
        out_shape=_sds((2 * half, cols), F32),
        compiler_params=_params("parallel"),
    )(pos, s1, recv)


def _sum_small(small_all):
    def body(a_ref, o_ref):
        acc = a_ref[0]
        for d in range(1, 8):
            acc = acc + a_ref[d]
        o_ref[...] = acc

    return pl.pallas_call(body, name="sum_small", out_shape=_sds(small_all.shape[1:], F32))(small_all)


def _join_halves(specs, s2):
    n = len(specs)

    def body(*refs):
        buf_refs = refs[n:2 * n]
        send_sems, recv_sems = refs[2 * n:]
        x, y, c, _ = _chip_peers()
        sent = []
        for p, (_, rows, _) in enumerate(specs):
            mine = buf_refs[p].at[pl.ds(c * (rows // 2), rows // 2)]
            sent.append(_remote(mine, mine, send_sems, recv_sems, p, (x, y, 1 - c)))
            sent[-1].start()
        for p, (_, rows, _) in enumerate(specs):
            theirs = buf_refs[p].at[pl.ds((1 - c) * (rows // 2), rows // 2)]
            _remote(theirs, theirs, send_sems, recv_sems, p, (x, y, 1 - c)).wait_recv()
        for cp in sent:
            cp.wait_send()

    return pl.pallas_call(
        body, name="join_halves", in_specs=[HBM_SPEC] * n, out_specs=[HBM_SPEC] * n,
        out_shape=[_sds(a.shape, a.dtype) for a in s2],
        input_output_aliases={p: p for p in range(n)},
        scratch_shapes=[pltpu.SemaphoreType.DMA((n,)), pltpu.SemaphoreType.DMA((n,))],
    )(*s2)


def _adamw(w, g, m, v, name):
    rows, cols = w.shape
    tm = rows
    for cand in (512, 256, 128, 64, 32, 16, 8):
        if rows % cand == 0 and rows > cand:
            tm = cand
            break

    def body(w_ref, g_ref, m_ref, v_ref, d_ref, nm_ref, nv_ref):
        gv = g_ref[...]
        nm = ADAM_B1 * m_ref[...] + (1.0 - ADAM_B1) * gv
        nv = ADAM_B2 * v_ref[...] + (1.0 - ADAM_B2) * (gv * gv)
        m_hat = nm / (1.0 - ADAM_B1 ** ADAM_STEP)
        v_hat = nv / (1.0 - ADAM_B2 ** ADAM_STEP)
        d_ref[...] = -ADAM_LR * (m_hat / (jnp.sqrt(v_hat) + ADAM_EPS) + ADAM_WD * w_ref[...])
        nm_ref[...] = nm
        nv_ref[...] = nv

    spec = _rows(tm, cols)
    return pl.pallas_call(
        body, name=name, grid=(rows // tm,), in_specs=[spec] * 4, out_specs=[spec] * 3,
        out_shape=[_sds((rows, cols), F32)] * 3, compiler_params=_params("parallel"),
    )(w, g, m, v)


WEIGHTS = ("ml_w_in", "ml_b_if", "ml_head_norm", "ml_w_out", "lru_w_in", "lru_conv_w", "lru_conv_b", "lru_w_gate_a",
           "lru_b_gate_a", "lru_w_gate_x", "lru_b_gate_x", "lru_lambda", "lru_w_out", "norm_pre_mix", "norm_post_mix",
           "norm_pre_ffn", "norm_post_ffn", "ffn_w_gate", "ffn_w_up", "ffn_w_down")
LRU_VECTORS = ("lru_conv_b", "lru_b_gate_a", "lru_b_gate_x", "lru_lambda")
NORMS = ("norm_pre_mix", "norm_post_mix", "norm_pre_ffn", "norm_post_ffn")


def _by_cols(a):
    return jnp.concatenate([a[k] for k in range(N_CHIPS)], axis=-1)


def _by_chip(a, width):
    return jnp.stack([a[..., k * width:(k + 1) * width] for k in range(N_CHIPS)])


def _weight_shards(w):
    bf = lambda a: a.astype(BF16)
    shards = dict(
        ml_w_in=bf(w["ml_w_in"][0]), ml_w_out=bf(w["ml_w_out"][0]), lru_w_in=bf(w["lru_w_in"][0]),
        lru_gates=bf(jnp.concatenate([w["lru_w_gate_a"][0], w["lru_w_gate_x"][0]], axis=0)).reshape(-1, LRU_BLOCK),
        lru_w_out=bf(w["lru_w_out"][0]))
    gate_t, up_t = bf(jnp.swapaxes(w["ffn_w_gate"], 1, 2)), bf(jnp.swapaxes(w["ffn_w_up"], 1, 2))
    for layer in range(2):
        shards[f"ffn_g{layer}"] = gate_t[layer]
        shards[f"ffn_u{layer}"] = up_t[layer]
        shards[f"ffn_down{layer}"] = bf(w["ffn_w_down"][layer])
    return shards


def _ffn_weights(parts):
    return [a.reshape(D_FF, 1024) for a in parts]


def _ffn_grads(dgu, z, act, dy, layer):
    dws = [_matmul_tn(dgu, z, True, 256, f"ffn{layer}_dw_gate", window=(0, D_FF)),
           _matmul_tn(dgu, z, True, 256, f"ffn{layer}_dw_up", window=(D_FF, D_FF)),
           _matmul_tn(act, dy, True, 256, f"ffn{layer}_dw_down")]
    return [a.reshape(N_CHIPS, HALF_FFN, 1024) for a in dws]


def _sum_group(group, s1, recv, pos):
    return [_sum_chips(a, r, pos, "sum_chips_" + n) for (n, _, _), a, r in zip(GROUPS[group], s1, recv)]


def _update(w, grads, m, v):
    delta, new_m, new_v = {}, {}, {}
    for n in WEIGHTS:
        flip = (lambda a: jnp.swapaxes(a, 1, 2)) if n in ("ffn_w_gate", "ffn_w_up") else (lambda a: a)
        shape = flip(w[n]).shape
        as2d = lambda a: flip(a).reshape(-1, shape[-1])
        back = lambda a: flip(a.reshape(shape))
        d, nm, nv = _adamw(as2d(w[n]), as2d(grads[n]), as2d(m[n]), as2d(v[n]), "adamw_" + n)
        delta[n], new_m[n], new_v[n] = back(d), back(nm), back(nv)
    return delta, new_m, new_v


def kernel(x, ml_w_in, ml_b_if, ml_head_norm, ml_w_out, lru_w_in, lru_conv_w, lru_conv_b, lru_w_gate_a, lru_b_gate_a, lru_w_gate_x, lru_b_gate_x, lru_lambda, lru_w_out, norm_pre_mix, norm_post_mix, norm_pre_ffn, norm_post_ffn, ffn_w_gate, ffn_w_up, ffn_w_down, loss_target, m_ml_w_in, m_ml_b_if, m_ml_head_norm, m_ml_w_out, m_lru_w_in, m_lru_conv_w, m_lru_conv_b, m_lru_w_gate_a, m_lru_b_gate_a, m_lru_w_gate_x, m_lru_b_gate_x, m_lru_lambda, m_lru_w_out, m_norm_pre_mix, m_norm_post_mix, m_norm_pre_ffn, m_norm_post_ffn, m_ffn_w_gate, m_ffn_w_up, m_ffn_w_down, v_ml_w_in, v_ml_b_if, v_ml_head_norm, v_ml_w_out, v_lru_w_in, v_lru_conv_w, v_lru_conv_b, v_lru_w_gate_a, v_lru_b_gate_a, v_lru_w_gate_x, v_lru_b_gate_x, v_lru_lambda, v_lru_w_out, v_norm_pre_mix, v_norm_post_mix, v_norm_pre_ffn, v_norm_post_ffn, v_ffn_w_gate, v_ffn_w_up, v_ffn_w_down):
    args = locals()
    w = {n: args[n] for n in WEIGHTS}
    m = {n: args["m_" + n] for n in WEIGHTS}
    v = {n: args["v_" + n] for n in WEIGHTS}
    xs, target = x[0], loss_target[0]
    mx, my, mc, _ = _chip_peers()
    chip = 2 * mx + my
    pos = jnp.stack([mc, chip])
    row = lambda a, i: a[i:i + 1]
    npm, nqm, npf, nqf = (w[n] for n in NORMS)
    shards = _weight_shards(w)
    of = lambda group: [shards[n] for n, _, _ in GROUPS[group]]
    bif = jnp.pad(w["ml_b_if"], ((0, 0), (0, 128 - 2 * ML_HEADS)))
    hn = w["ml_head_norm"]

    small = jnp.concatenate([w["lru_conv_w"][0]] + [w[n] for n in LRU_VECTORS], axis=0)
    ml_in_spec, ml_out_spec = GROUPS["ml"][:1], GROUPS["ml"][1:]
    ml_in_parts, smalls = _gather_weights(ml_in_spec, of("ml")[:1], small)
    ml_w_in = jnp.pad(_by_cols(ml_in_parts), ((0, 0), (0, ML_IN_PAD - ML_IN)))
    vec = _by_cols(smalls)
    lru_spec, ffn1_spec = GROUPS["lru"], GROUPS["ffn1"]
    (z0, qkv, og), got = _norm_matmul(xs, row(npm, 0), ml_w_in, 2 * ML_QK + ML_V, "ml_in",
                                      _gather_side(ml_out_spec, of("ml")[1:]))
    ml_w_out = got[0].reshape(1024, 1024)
    (y0, hs, cst, nst, mst), got = _mlstm_fwd(qkv, og, bif, hn, _gather_side(GROUPS["ffn0"], of("ffn0")))
    wg0, wu0, w_down0 = _ffn_weights(got)
    (ymix0, h1), got = _matmul_postnorm(y0, ml_w_out, row(nqm, 0), xs, "ml_out",
                                        _gather_side(lru_spec[:1], of("lru")[:1]))
    lru_w_in = _by_cols(got[0])
    (zf0, gu0, yf0, h2), got = _ffn_forward(
        h1, row(npf, 0), wg0, wu0, w_down0, row(nqf, 0), "ffn0_fwd",
        _gather_side(lru_spec[1:] + ffn1_spec[:2], of("lru")[1:] + of("ffn1")[:2]))
    gates = got[0].reshape(N_CHIPS, 2, LRU_BLOCKS, 64, LRU_BLOCK).transpose(1, 2, 0, 3, 4)
    gates = gates.reshape(2, LRU_BLOCKS, LRU_BLOCK, LRU_BLOCK)
    lru_w_out = got[1].reshape(1024, 1024)
    ffn1_gu = got[2:4]
    (z1, proj1), _ = _norm_matmul(h2, row(npm, 1), lru_w_in, 0, "lru_in", None)
    (y1, u, r, ig, hl), got = _lru_fwd(proj1, vec[0:4], vec[4:5], gates[0], vec[5:6], gates[1], vec[6:7], vec[7:8],
                                       _gather_side(ffn1_spec[2:], of("ffn1")[2:]))
    wg1, wu1, w_down1 = _ffn_weights(ffn1_gu + got)
    (ymix1, h3), _ = _matmul_postnorm(y1, lru_w_out, row(nqm, 1), h2, "lru_out", None)
    (zf1, gu1, yf1, h4), _ = _ffn_forward(h3, row(npf, 1), wg1, wu1, w_down1, row(nqf, 1), "ffn1_fwd", None)
    dh4, loss_part = _loss_head(h4, target, "loss_head")

    add = lambda group, gparts, recv: [_add_halves(g, r, pos, "add_halves_" + n)
                                       for (n, _, _), g, r in zip(GROUPS[group], gparts, recv)]
    (dyf1, dgu1, act1, dh3, dqf1, dpf1), _ = _ffn_backward(
        dh4, yf1, row(nqf, 1), w_down1, gu1, wg1, wu1, h3, row(npf, 1), "ffn1_bwd", None)
    g_ffn1 = _ffn_grads(dgu1, zf1, act1, dyf1, 1)

    (dymix1, dy1, dqm1), recv = _bwd_out(dh3, ymix1, row(nqm, 1), lru_w_out, "lru_bwd_out",
                                         _exchange_side(GROUPS["ffn1"], g_ffn1))
    s1_ffn1 = add("ffn1", g_ffn1, recv)
    dw_lru_out = _matmul_tn(y1, dymix1, False, 256, "lru_dw_out")
    (dproj1, dcw, dcb, dwa, dba, dwx, dbx, dlam), recv_ffn1 = _lru_bwd(
        proj1, vec[0:4], gates[0], gates[1], vec[7:8], u, r, ig, hl, dy1, _scatter_side(s1_ffn1))
    dw_lru_in = _matmul_tn(z1, dproj1, False, 512, "lru_dw_in")
    dgates = jnp.stack([dwa, dwx]).astype(BF16).reshape(2, LRU_BLOCKS, N_CHIPS, 64, LRU_BLOCK)
    dgates = dgates.transpose(2, 0, 1, 3, 4).reshape(N_CHIPS, -1, LRU_BLOCK)
    g_lru = [_by_chip(dw_lru_in, 512), dgates, dw_lru_out.reshape(N_CHIPS, 256, 1024)]
    (dh2, dpm1), recv = _bwd_in(dproj1, [lru_w_in], False, h2, row(npm, 1), dh3, "lru_bwd_in",
                                _exchange_side(GROUPS["lru"], g_lru))
    s1_lru = add("lru", g_lru, recv)

    (dyf0, dgu0, act0, dh1, dqf0, dpf0), recv_lru = _ffn_backward(
        dh2, yf0, row(nqf, 0), w_down0, gu0, wg0, wu0, h1, row(npf, 0), "ffn0_bwd", _scatter_side(s1_lru))
    g_ffn0 = _ffn_grads(dgu0, zf0, act0, dyf0, 0)

    (dymix0, dy0, dqm0), recv = _bwd_out(dh1, ymix0, row(nqm, 0), ml_w_out, "ml_bwd_out",
                                         _exchange_side(GROUPS["ffn0"], g_ffn0))
    s1_ffn0 = add("ffn0", g_ffn0, recv)
    dw_ml_out = _matmul_tn(y0, dymix0, False, 256, "ml_dw_out")
    (dproj0, dhn, dbif), recv_ffn0 = _mlstm_bwd(qkv, og, bif, hn, hs, cst, nst, mst, dy0, _scatter_side(s1_ffn0))
    dw_ml_in = _matmul_tn(z0, dproj0, False, 640, "ml_dw_in")
    g_ml = [_by_chip(dw_ml_in[:, :ML_IN], ML_IN // N_CHIPS), dw_ml_out.reshape(N_CHIPS, 256, 1024)]
    s1_ml = add("ml", g_ml, _exchange_halves(GROUPS["ml"], g_ml, "exchange_halves_ml"))
    (dx, dpm0), recv_ml = _bwd_in(dproj0, [ml_w_in], False, xs, row(npm, 0), dh1, "ml_bwd_in", _scatter_side(s1_ml))

    pad_lanes = lambda a: jnp.pad(a, ((0, 0), (0, LANES - a.shape[1])))
    small = jnp.concatenate(
        [jnp.concatenate([dpm0, dpm1]), jnp.concatenate([dqm0, dqm1]), jnp.concatenate([dpf0, dpf1]),
         jnp.concatenate([dqf0, dqf1]), pad_lanes(loss_part), dhn, pad_lanes(dbif), dcw, dcb, dba, dbx, dlam,
         jnp.zeros((SMALL_ROWS - 19, LANES), F32)], axis=0)
    small_all = _share_small(small)
    order = ("ml", "lru", "ffn0", "ffn1")
    s2 = (_sum_group("ml", s1_ml, recv_ml, pos) + _sum_group("lru", s1_lru, recv_lru, pos)
          + _sum_group("ffn0", s1_ffn0, recv_ffn0, pos) + _sum_group("ffn1", s1_ffn1, recv_ffn1, pos))
    specs = sum((GROUPS[g] for g in order), ())
    red = dict(zip([n for n, _, _ in specs], _join_halves(specs, s2)))
    vsum = _sum_small(small_all)

    dgates = red["lru_gates"].reshape(2, LRU_BLOCKS, 64, LRU_BLOCK)
    cols = lambda a: lax.dynamic_slice_in_dim(a, chip * 256, 256, axis=1)
    grads = dict(
        ml_w_in=red["ml_w_in"], ml_w_out=red["ml_w_out"], lru_w_in=red["lru_w_in"], lru_w_gate_a=dgates[0],
        lru_w_gate_x=dgates[1], lru_w_out=red["lru_w_out"],
        ffn_w_gate=jnp.swapaxes(jnp.stack([red["ffn_g0"], red["ffn_g1"]]), 1, 2),
        ffn_w_up=jnp.swapaxes(jnp.stack([red["ffn_u0"], red["ffn_u1"]]), 1, 2),
        ffn_w_down=jnp.stack([red["ffn_down0"], red["ffn_down1"]]),
        ml_head_norm=vsum[ROW_HEAD_NORM:ROW_HEAD_NORM + 1], ml_b_if=vsum[ROW_B_IF:ROW_B_IF + 1, :2 * ML_HEADS],
        lru_conv_w=cols(vsum[ROW_LRU:ROW_LRU + 4]))
    for i, n in enumerate(NORMS):
        grads[n] = vsum[2 * i:2 * i + 2]
    for i, n in enumerate(LRU_VECTORS):
        grads[n] = cols(vsum[ROW_LRU + 4 + i:ROW_LRU + 5 + i])
    loss = vsum[ROW_LOSS, 0]
    grads = {n: grads[n].reshape(w[n].shape) for n in WEIGHTS}
    delta, new_m, new_v = _update(w, grads, m, v)
    return (loss, dx[None], *[grads[n] for n in WEIGHTS], *[delta[n] for n in WEIGHTS],
            *[new_m[n] for n in WEIGHTS], *[new_v[n] for n in WEIGHTS])
```

```python
import functools
import math
from typing import Callable, NamedTuple

import jax
import jax.numpy as jnp
from jax import lax
from jax.experimental import pallas as pl
from jax.experimental.pallas import tpu as pltpu

F32 = jnp.float32
BF16 = jnp.bfloat16
MESH = pl.DeviceIdType.MESH

D_MODEL = 1024
D_FF = 2816
ML_HEADS = 8
ML_DK = 64
ML_DV = 128
ML_QK = ML_HEADS * ML_DK
ML_V = ML_HEADS * ML_DV
ML_IN = 2 * ML_QK + 2 * ML_V + 2 * ML_HEADS
ML_IN_PAD = 3200
CHUNK = 64
GATE_CAP = 15.0
ML_M_INIT = -1e30
NEG_BIG = -1e30
LRU_BLOCKS = 4
LRU_BLOCK = 256
CONV_WIDTH = 4
LRU_C = 8.0
EPS = 1e-6
QK_SCALE = ML_DK ** -0.5

ADAM_LR = 0.001
ADAM_B1 = 0.9
ADAM_B2 = 0.999
ADAM_EPS = 1e-08
ADAM_WD = 0.01
ADAM_STEP = 10

N_CHIPS = 4
V7X_VMEM_LIMIT = 56 * 1024 * 1024

NT_DIMS = (((1,), (1,)), ((), ()))
TN_DIMS = (((0,), (0,)), ((), ()))


def _params(*semantics):
    return pltpu.CompilerParams(dimension_semantics=semantics, vmem_limit_bytes=V7X_VMEM_LIMIT)


def _sds(shape, dtype):
    return jax.ShapeDtypeStruct(shape, dtype)


def _full(shape):
    return pl.BlockSpec(shape, lambda *_: (0,) * len(shape))


def _rows(tm, n):
    return pl.BlockSpec((tm, n), lambda i: (i, 0))


def _sigmoid(x):
    return 1.0 / (1.0 + jnp.exp(-x))


def _log_sigmoid(x):
    return jnp.minimum(x, 0.0) - jnp.log1p(jnp.exp(-jnp.abs(x)))


def _rms(x):
    r = lax.rsqrt(jnp.mean(x * x, axis=-1, keepdims=True) + EPS)
    return x * r, r


def _rms_bwd(xhat, r, g, dy):
    dxh = dy * g
    return r * (dxh - xhat * jnp.mean(dxh * xhat, axis=-1, keepdims=True))


SUBLANES = 8


def _scan_rows(a, b, carry, reverse=False):
    n, w = a.shape
    groups = n // SUBLANES
    a3, b3 = a.reshape(groups, SUBLANES, w), b.reshape(groups, SUBLANES, w)
    sub = lax.broadcasted_iota(jnp.int32, a3.shape, 1)
    s = 1
    while s < SUBLANES:
        keep = sub < SUBLANES - s if reverse else sub >= s
        shift = SUBLANES - s if reverse else s
        b3 = b3 + a3 * jnp.where(keep, pltpu.roll(b3, shift, 1), 0.0)
        a3 = a3 * jnp.where(keep, pltpu.roll(a3, shift, 1), 1.0)
        s *= 2
    out = [None] * groups
    for g in (reversed(range(groups)) if reverse else range(groups)):
        out[g] = b3[g] + a3[g] * carry
        carry = out[g][0:1] if reverse else out[g][SUBLANES - 1:SUBLANES]
    return jnp.concatenate(out, axis=0)


def _cumsum_rows(x, reverse=False):
    n = x.shape[0]
    row = lax.broadcasted_iota(jnp.int32, x.shape, 0)
    s = 1
    while s < n:
        if reverse:
            x = x + jnp.where(row < n - s, pltpu.roll(x, n - s, 0), 0.0)
        else:
            x = x + jnp.where(row >= s, pltpu.roll(x, s, 0), 0.0)
        s *= 2
    return x


def _norm_matmul(h, g, w, n_bf16, name, side):
    t, d = h.shape
    n = w.shape[1]
    tm = min(512, t)

    def body(h_ref, g_ref, w_ref, z_ref, *o_refs):
        xhat, _ = _rms(h_ref[...])
        z = (xhat * g_ref[...]).astype(BF16)
        z_ref[...] = z
        out = jnp.dot(z, w_ref[...], preferred_element_type=F32)
        if n_bf16:
            o_refs[0][...] = out[:, :n_bf16].astype(BF16)
            o_refs[1][...] = out[:, n_bf16:]
        else:
            o_refs[0][...] = out

    if n_bf16:
        out_specs = [_rows(tm, d), _rows(tm, n_bf16), _rows(tm, n - n_bf16)]
        out_shape = [_sds((t, d), BF16), _sds((t, n_bf16), BF16), _sds((t, n - n_bf16), F32)]
    else:
        out_specs = [_rows(tm, d), _rows(tm, n)]
        out_shape = [_sds((t, d), BF16), _sds((t, n), F32)]
    return _side_call(body, side, name=name, steps=t // tm, in_specs=[_rows(tm, d), _full((1, d)), _full((d, n))],
                      out_specs=out_specs, out_shape=out_shape, scratch_shapes=[], args=(h, g, w))


def _matmul_postnorm(a, w, g, res, name, side):
    t = res.shape[0]
    k, d = w.shape
    tm = 256

    def body(a_ref, w_ref, g_ref, res_ref, y_ref, o_ref):
        y = jnp.dot(a_ref[...], w_ref[...], preferred_element_type=F32)
        y_ref[...] = y
        yhat, _ = _rms(y)
        o_ref[...] = res_ref[...] + yhat * g_ref[...]

    return _side_call(
        body, side, name=name, steps=t // tm,
        in_specs=[_rows(tm, k), _full((k, d)), _full((1, d)), _rows(tm, d)],
        out_specs=[_rows(tm, d), _rows(tm, d)],
        out_shape=[_sds((t, d), F32), _sds((t, d), F32)],
        scratch_shapes=[], args=(a, w, g, res))


class _Side(NamedTuple):
    inputs: list
    out_shape: list
    n_sems: int
    start: Callable
    finish: Callable


def _side_call(body, side, *, name, steps, in_specs, out_specs, out_shape, scratch_shapes, args):
    n_in, n_out, n_scr = len(in_specs), len(out_specs), len(scratch_shapes)
    if side is None:
        outs = pl.pallas_call(
            body, name=name, grid=(steps,), in_specs=in_specs, out_specs=out_specs, out_shape=out_shape,
            scratch_shapes=scratch_shapes, compiler_params=_params("arbitrary"))(*args)
        return list(outs), []
    s_in, s_out = len(side.inputs), len(side.out_shape)

    def carrying(*refs):
        ins, side_ins = refs[:n_in], refs[n_in:n_in + s_in]
        outs = refs[n_in + s_in:n_in + s_in + n_out]
        side_outs = refs[n_in + s_in + n_out:n_in + s_in + n_out + s_out]
        scratch = refs[n_in + s_in + n_out + s_out:]
        own, sems = scratch[:n_scr], scratch[n_scr:]

        @pl.when(pl.program_id(0) == 0)
        def _():
            side.start(side_ins, side_outs, *sems)

        body(*ins, *outs, *own)

        @pl.when(pl.program_id(0) == steps - 1)
        def _():
            side.finish(side_ins, side_outs, *sems)

    outs = pl.pallas_call(
        carrying, name=name, grid=(steps,), in_specs=list(in_specs) + [HBM_SPEC] * s_in,
        out_specs=list(out_specs) + [HBM_SPEC] * s_out, out_shape=list(out_shape) + list(side.out_shape),
        scratch_shapes=list(scratch_shapes) + [pltpu.SemaphoreType.DMA((side.n_sems,))] * 2,
        compiler_params=_params("arbitrary"))(*args, *side.inputs)
    return list(outs[:n_out]), list(outs[n_out:])


def _ffn_forward(h, g_pre, wg_t, wu_t, w_down, g_post, name, side):
    t, d = h.shape
    k = w_down.shape[0]
    tm = 256

    def body(h_ref, gpre_ref, wg_ref, wu_ref, wd_ref, gpost_ref, z_ref, gu_ref, y_ref, o_ref):
        hv = h_ref[...]
        xhat, _ = _rms(hv)
        z = (xhat * gpre_ref[...]).astype(BF16)
        z_ref[...] = z
        gate = lax.dot_general(z, wg_ref[...], NT_DIMS, preferred_element_type=F32).astype(BF16)
        up = lax.dot_general(z, wu_ref[...], NT_DIMS, preferred_element_type=F32).astype(BF16)
        gu_ref[:, :k] = gate
        gu_ref[:, k:] = up
        gate = gate.astype(F32)
        act = (gate * _sigmoid(gate) * up.astype(F32)).astype(BF16)
        y = jnp.dot(act, wd_ref[...], preferred_element_type=F32)
        y_ref[...] = y
        yhat, _ = _rms(y)
        o_ref[...] = hv + yhat * gpost_ref[...]

    return _side_call(
        body, side, name=name, steps=t // tm,
        in_specs=[_rows(tm, d), _full((1, d)), _full((k, d)), _full((k, d)), _full((k, d)), _full((1, d))],
        out_specs=[_rows(tm, d), _rows(tm, 2 * k), _rows(tm, d), _rows(tm, d)],
        out_shape=[_sds((t, d), BF16), _sds((t, 2 * k), BF16), _sds((t, d), F32), _sds((t, d), F32)],
        scratch_shapes=[], args=(h, g_pre, wg_t, wu_t, w_down, g_post))


def _loss_head(h, target, name):
    t, d = h.shape
    tm = min(512, t)

    def body(h_ref, t_ref, dh_ref, l_ref):
        @pl.when(pl.program_id(0) == 0)
        def _():
            l_ref[...] = jnp.zeros_like(l_ref)

        err = h_ref[...] - t_ref[...]
        dh_ref[...] = err * (1.0 / d)
        part = jnp.sum(jnp.sum(err * err, axis=1, keepdims=True), axis=0, keepdims=True) * (0.5 / d)
        l_ref[...] += jnp.broadcast_to(part, l_ref.shape)

    return pl.pallas_call(
        body, name=name, grid=(t // tm,),
        in_specs=[_rows(tm, d), _rows(tm, d)],
        out_specs=[_rows(tm, d), _full((1, 128))],
        out_shape=[_sds((t, d), F32), _sds((1, 128), F32)],
        compiler_params=_params("arbitrary"),
    )(h, target)


def _bwd_out(dout, y, g, w, name, side):
    t, d = dout.shape
    k = w.shape[0]
    tm = 256

    def body(dout_ref, y_ref, g_ref, w_ref, dy_ref, da_ref, dg_ref):
        @pl.when(pl.program_id(0) == 0)
        def _():
            dg_ref[...] = jnp.zeros_like(dg_ref)

        do = dout_ref[...]
        yhat, r = _rms(y_ref[...])
        dg_ref[...] += jnp.sum(do * yhat, axis=0, keepdims=True)
        dy = _rms_bwd(yhat, r, g_ref[...], do).astype(BF16)
        dy_ref[...] = dy
        da_ref[...] = lax.dot_general(dy, w_ref[...], NT_DIMS, preferred_element_type=F32)

    return _side_call(
        body, side, name=name, steps=t // tm,
        in_specs=[_rows(tm, d), _rows(tm, d), _full((1, d)), _full((k, d))],
        out_specs=[_rows(tm, d), _rows(tm, k), _full((1, d))],
        out_shape=[_sds((t, d), BF16), _sds((t, k), F32), _sds((1, d), F32)],
        scratch_shapes=[], args=(dout, y, g, w))


def _ffn_backward(dout, y, g_post, w_down, gu, wg_t, wu_t, h, g_pre, name, side):
    t, d = dout.shape
    k = w_down.shape[0]
    tm = 256

    def body(dout_ref, y_ref, gpost_ref, wd_ref, gu_ref, wg_ref, wu_ref, h_ref, gpre_ref,
             dy_ref, dgu_ref, act_ref, dh_ref, dgpost_ref, dgpre_ref):
        @pl.when(pl.program_id(0) == 0)
        def _():
            dgpost_ref[...] = jnp.zeros_like(dgpost_ref)
            dgpre_ref[...] = jnp.zeros_like(dgpre_ref)

        do = dout_ref[...]
        yhat, r = _rms(y_ref[...])
        dgpost_ref[...] += jnp.sum(do * yhat, axis=0, keepdims=True)
        dy = _rms_bwd(yhat, r, gpost_ref[...], do).astype(BF16)
        dy_ref[...] = dy
        da = lax.dot_general(dy, wd_ref[...], NT_DIMS, preferred_element_type=F32)
        gate = gu_ref[:, :k].astype(F32)
        up = gu_ref[:, k:].astype(F32)
        sg = _sigmoid(gate)
        silu = gate * sg
        act_ref[...] = (silu * up).astype(BF16)
        dgate = (da * up * (sg * (1.0 + gate * (1.0 - sg)))).astype(BF16)
        dup = (da * silu).astype(BF16)
        dgu_ref[:, :k] = dgate
        dgu_ref[:, k:] = dup
        dz = (jnp.dot(dgate, wg_ref[...], preferred_element_type=F32)
              + jnp.dot(dup, wu_ref[...], preferred_element_type=F32))
        hhat, r2 = _rms(h_ref[...])
        dgpre_ref[...] += jnp.sum(dz * hhat, axis=0, keepdims=True)
        dh_ref[...] = do + _rms_bwd(hhat, r2, gpre_ref[...], dz)

    vec = _full((1, d))
    wspec = _full((k, d))
    return _side_call(
        body, side, name=name, steps=t // tm,
        in_specs=[_rows(tm, d), _rows(tm, d), vec, wspec, _rows(tm, 2 * k), wspec, wspec, _rows(tm, d), vec],
        out_specs=[_rows(tm, d), _rows(tm, 2 * k), _rows(tm, k), _rows(tm, d), vec, vec],
        out_shape=[_sds((t, d), BF16), _sds((t, 2 * k), BF16), _sds((t, k), BF16), _sds((t, d), F32),
                   _sds((1, d), F32), _sds((1, d), F32)],
        scratch_shapes=[], args=(dout, y, g_post, w_down, gu, wg_t, wu_t, h, g_pre))


def _bwd_in(dp, ws, transposed, h, g, dout, name, side):
    t, d = h.shape
    n = dp.shape[1]
    tm = 256
    widths = [w.shape[0] if transposed else w.shape[1] for w in ws]

    def body(dp_ref, *refs):
        w_refs = refs[:len(ws)]
        h_ref, g_ref, dout_ref, dh_ref, dg_ref = refs[len(ws):]

        @pl.when(pl.program_id(0) == 0)
        def _():
            dg_ref[...] = jnp.zeros_like(dg_ref)

        dz, at = None, 0
        for w_ref, width in zip(w_refs, widths):
            block = dp_ref[:, at:at + width]
            if transposed:
                part = jnp.dot(block, w_ref[...], preferred_element_type=F32)
            else:
                part = lax.dot_general(block, w_ref[...], NT_DIMS, preferred_element_type=F32)
            dz = part if dz is None else dz + part
            at += width
        hhat, r = _rms(h_ref[...])
        dg_ref[...] += jnp.sum(dz * hhat, axis=0, keepdims=True)
        dh_ref[...] = dout_ref[...] + _rms_bwd(hhat, r, g_ref[...], dz)

    return _side_call(
        body, side, name=name, steps=t // tm,
        in_specs=[_rows(tm, n)] + [_full(w.shape) for w in ws] + [_rows(tm, d), _full((1, d)), _rows(tm, d)],
        out_specs=[_rows(tm, d), _full((1, d))],
        out_shape=[_sds((t, d), F32), _sds((1, d), F32)],
        scratch_shapes=[], args=(dp, *ws, h, g, dout))


def _matmul_tn(a, b, tile_a, tile, name, window=None):
    t, ka = a.shape
    nb = b.shape[1]
    col0 = 0
    if window is not None:
        col0, ka = window

    def body(a_ref, b_ref, o_ref):
        o_ref[...] = lax.dot_general(a_ref[...], b_ref[...], TN_DIMS, preferred_element_type=F32).astype(BF16)

    if tile_a:
        grid = (ka // tile,)
        in_specs = [pl.BlockSpec((t, tile), lambda i: (0, col0 // tile + i)), _full((t, nb))]
        out_specs = pl.BlockSpec((tile, nb), lambda i: (i, 0))
    else:
        grid = (nb // tile,)
        in_specs = [_full((t, ka)), pl.BlockSpec((t, tile), lambda i: (0, i))]
        out_specs = pl.BlockSpec((ka, tile), lambda i: (0, i))
    return pl.pallas_call(
        body, name=name, grid=grid, in_specs=in_specs, out_specs=out_specs,
        out_shape=_sds((ka, nb), BF16), compiler_params=_params("parallel"),
    )(a, b)


def _ml_gate_prep(gt, bif):
    th = jnp.tanh((gt + bif) / GATE_CAP)
    act = GATE_CAP * th
    cum = _cumsum_rows(_log_sigmoid(act))
    lane = lax.broadcasted_iota(jnp.int32, gt.shape, 1)
    x = jnp.where(lane < ML_HEADS, act, cum)
    return x, x.T, th, act


HEADS = range(ML_HEADS)


def _each(fn, *per_head):
    return [fn(*a) for a in zip(*per_head)]


def _ml_chunk_fwd(q, k, v, kt, x, xt, c_in, n_in, m_in):
    causal = (lax.broadcasted_iota(jnp.int32, (CHUNK, CHUNK), 0)
              >= lax.broadcasted_iota(jnp.int32, (CHUNK, CHUNK), 1))
    f = {}
    qh = f["qh"] = [q[:, ML_DK * h:ML_DK * (h + 1)] for h in HEADS]
    kh = f["kh"] = [k[:, ML_DK * h:ML_DK * (h + 1)] for h in HEADS]
    f["vh"] = [v[:, ML_DV * h:ML_DV * (h + 1)] for h in HEADS]
    kth = [kt[ML_DK * h:ML_DK * (h + 1), :] for h in HEADS]
    s = _each(lambda a, b: jnp.dot(a, b, preferred_element_type=F32) * QK_SCALE, qh, kth)
    f["qc"] = _each(lambda a, c: jnp.dot(a, c.astype(BF16), preferred_element_type=F32) * QK_SCALE, qh, c_in)
    bcol = f["bcol"] = [x[:, ML_HEADS + h:ML_HEADS + h + 1] for h in HEADS]
    licol = f["licol"] = [x[:, h:h + 1] for h in HEADS]
    brow = f["brow"] = [xt[ML_HEADS + h:ML_HEADS + h + 1, :] for h in HEADS]
    lirow = [xt[h:h + 1, :] for h in HEADS]
    dmat = _each(lambda bc, br, lr: jnp.where(causal, bc - br + lr, NEG_BIG), bcol, brow, lirow)
    inter = _each(lambda bc, m: bc + m, bcol, m_in)
    mt = _each(lambda d, i: jnp.maximum(jnp.max(d, axis=1, keepdims=True), i), dmat, inter)
    wt = f["wt"] = _each(lambda d, m: jnp.exp(d - m), dmat, mt)
    p = f["p"] = _each(lambda a, b: a * b, wt, s)
    winter = f["winter"] = _each(lambda i, m: jnp.exp(i - m), inter, mt)
    qf = f["qf"] = [a.astype(F32) for a in qh]
    qn = f["qn"] = _each(lambda a, n: jnp.sum(a * n, axis=1, keepdims=True) * QK_SCALE, qf, n_in)
    den = f["den"] = _each(lambda a, w, b: jnp.sum(a, axis=1, keepdims=True) + w * b, p, winter, qn)
    emt = f["emt"] = [jnp.exp(-m) for m in mt]
    f["nrm"] = _each(lambda d, e: jnp.maximum(jnp.abs(d), e), den, emt)
    gtot = [bc[CHUNK - 1:CHUNK, :] for bc in bcol]
    a_col = _each(lambda g, bc, lc: g - bc + lc, gtot, bcol, licol)
    a_row = _each(lambda g, br, lr: g - br + lr, gtot, brow, lirow)
    m_new = f["m_new"] = _each(lambda g, m, a: jnp.maximum(g + m, jnp.max(a, axis=1, keepdims=True)),
                               gtot, m_in, a_row)
    f["decay"] = _each(lambda g, m, mn: jnp.exp(g + m - mn), gtot, m_in, m_new)
    wkf = f["wkf"] = _each(lambda a, mn: jnp.exp(a - mn), a_col, m_new)
    f["kw"] = _each(lambda a, w: a.astype(F32) * w, kh, wkf)
    f["ktw"] = _each(lambda a, ar, mn: (a.astype(F32) * jnp.exp(ar - mn)).astype(BF16), kth, a_row, m_new)
    return f


def _ml_specs(nc, rev):
    def at(col):
        if rev:
            return lambda i: (nc - 1 - i, col)
        return lambda i: (i, col)

    return [pl.BlockSpec((CHUNK, ML_QK), at(0)), pl.BlockSpec((CHUNK, ML_QK), at(1)),
            pl.BlockSpec((CHUNK, ML_V), at(1)), pl.BlockSpec((CHUNK, ML_V), at(0)),
            pl.BlockSpec((CHUNK, 128), at(ML_V // 128))]


def _mlstm_fwd(qkv, og, bif, hn, side):
    t = qkv.shape[0]
    nc = t // CHUNK

    def body(q_ref, k_ref, v_ref, o_ref, gt_ref, bif_ref, hn_ref, y_ref, hs_ref, cst_ref, nst_ref, mst_ref,
             c_sc, n_sc, m_sc):
        @pl.when(pl.program_id(0) == 0)
        def _():
            c_sc[...] = jnp.zeros_like(c_sc)
            n_sc[...] = jnp.zeros_like(n_sc)
            m_sc[...] = jnp.full_like(m_sc, ML_M_INIT)

        cst_ref[0] = c_sc[...]
        nst_ref[0] = n_sc[...]
        mst_ref[0] = m_sc[...]
        x, xt, _, _ = _ml_gate_prep(gt_ref[...], bif_ref[...])
        q, k, v = q_ref[...], k_ref[...], v_ref[...]
        c_all, n_all, m_all = c_sc[...], n_sc[...], m_sc[...]
        c_in = [c_all[ML_DK * h:ML_DK * (h + 1), :] for h in HEADS]
        n_in = [n_all[h:h + 1, :] for h in HEADS]
        m_in = [m_all[h:h + 1, 0:1] for h in HEADS]
        f = _ml_chunk_fwd(q, k, v, k.T, x, xt, c_in, n_in, m_in)
        num = _each(lambda p, vh, w, qc: jnp.dot(p.astype(BF16), vh, preferred_element_type=F32) + w * qc,
                    f["p"], f["vh"], f["winter"], f["qc"])
        hh = _each(lambda a, b: a / b, num, f["nrm"])
        hhat = [_rms(a)[0] for a in hh]
        c_new = _each(lambda d, c, kw, vh: d * c + jnp.dot(kw, vh, preferred_element_type=F32),
                      f["decay"], c_in, f["ktw"], f["vh"])
        n_new = _each(lambda d, n, kw: d * n + jnp.sum(kw, axis=0, keepdims=True), f["decay"], n_in, f["kw"])
        for h in HEADS:
            vs = slice(ML_DV * h, ML_DV * (h + 1))
            hs_ref[:, vs] = hh[h]
            y_ref[:, vs] = (hhat[h] * hn_ref[:, vs] * _sigmoid(o_ref[:, vs])).astype(BF16)
            c_sc[ML_DK * h:ML_DK * (h + 1), :] = c_new[h]
            n_sc[h:h + 1, :] = n_new[h]
            m_sc[h:h + 1, :] = jnp.broadcast_to(f["m_new"][h], (1, 128))

    return _side_call(
        body, side, name="mlstm_fwd", steps=nc,
        in_specs=_ml_specs(nc, False) + [_full((1, 128)), _full((1, ML_V))],
        out_specs=[_rows(CHUNK, ML_V), _rows(CHUNK, ML_V),
                   pl.BlockSpec((1, ML_HEADS * ML_DK, ML_DV), lambda i: (i, 0, 0)),
                   pl.BlockSpec((1, ML_HEADS, ML_DK), lambda i: (i, 0, 0)),
                   pl.BlockSpec((1, ML_HEADS, 128), lambda i: (i, 0, 0))],
        out_shape=[_sds((t, ML_V), BF16), _sds((t, ML_V), F32),
                   _sds((nc, ML_HEADS * ML_DK, ML_DV), F32), _sds((nc, ML_HEADS, ML_DK), F32),
                   _sds((nc, ML_HEADS, 128), F32)],
        scratch_shapes=[pltpu.VMEM((ML_HEADS * ML_DK, ML_DV), F32), pltpu.VMEM((ML_HEADS, ML_DK), F32),
                        pltpu.VMEM((ML_HEADS, 128), F32)],
        args=(qkv, qkv, qkv, og, og, bif, hn))


def _mlstm_bwd(qkv, og, bif, hn, hs, cst, nst, mst, dy, side):
    t = qkv.shape[0]
    nc = t // CHUNK

    def body(q_ref, k_ref, v_ref, o_ref, gt_ref, bif_ref, hn_ref, hs_ref, cst_ref, nst_ref, mst_ref, dy_ref,
             dp_ref, dhn_ref, dbif_ref, dc_sc, dct_sc, dn_sc):
        @pl.when(pl.program_id(0) == 0)
        def _():
            dc_sc[...] = jnp.zeros_like(dc_sc)
            dct_sc[...] = jnp.zeros_like(dct_sc)
            dn_sc[...] = jnp.zeros_like(dn_sc)
            dhn_ref[...] = jnp.zeros_like(dhn_ref)
            dbif_ref[...] = jnp.zeros_like(dbif_ref)

        x, xt, th, act = _ml_gate_prep(gt_ref[...], bif_ref[...])
        q, k, v = q_ref[...], k_ref[...], v_ref[...]
        qt, vt = q.T, v.T
        rows = lax.broadcasted_iota(jnp.int32, (CHUNK, CHUNK), 0)
        cols = lax.broadcasted_iota(jnp.int32, (CHUNK, CHUNK), 1)
        lane = lax.broadcasted_iota(jnp.int32, (CHUNK, 128), 1)
        row = lax.broadcasted_iota(jnp.int32, (CHUNK, 1), 0)
        as_row = lambda col: jnp.sum(jnp.where(rows == cols, col, 0.0), axis=0, keepdims=True)
        mm = lambda a, b: jnp.dot(a, b, preferred_element_type=F32)
        bf = lambda a: a.astype(BF16)
        ksl = [slice(ML_DK * h, ML_DK * (h + 1)) for h in HEADS]
        vsl = [slice(ML_DV * h, ML_DV * (h + 1)) for h in HEADS]
        c_in = [cst_ref[0, s, :] for s in ksl]
        n_in = [nst_ref[0, h:h + 1, :] for h in HEADS]
        m_in = [mst_ref[0, h:h + 1, 0:1] for h in HEADS]
        dc_all, dct_all, dn_all = dc_sc[...], dct_sc[...], dn_sc[...]
        dcn = [dc_all[s, :] for s in ksl]
        dcn_t = [dct_all[s, :] for s in vsl]
        dnn = [dn_all[h:h + 1, :] for h in HEADS]
        f = _ml_chunk_fwd(q, k, v, k.T, x, xt, c_in, n_in, m_in)
        qh, kh, vh, p, winter, decay = f["qh"], f["kh"], f["vh"], f["p"], f["winter"], f["decay"]
        qth = [qt[s, :] for s in ksl]
        vth = [vt[s, :] for s in vsl]
        c_t = [bf(c.T) for c in c_in]
        dmat_t = _each(lambda br, bc, lc: jnp.where(rows <= cols, br - bc + lc, NEG_BIG),
                       f["brow"], f["bcol"], f["licol"])
        inter_row = _each(lambda br, m: br + m, f["brow"], m_in)
        mt_row = _each(lambda d, i: jnp.maximum(jnp.max(d, axis=0, keepdims=True), i), dmat_t, inter_row)
        wt_t = _each(lambda d, m: jnp.exp(d - m), dmat_t, mt_row)
        p_t = _each(lambda w, a, b: w * (mm(a, b) * QK_SCALE), wt_t, kh, qth)
        winter_row = _each(lambda i, m: jnp.exp(i - m), inter_row, mt_row)
        hh = [hs_ref[:, s] for s in vsl]
        hn_h = [hn_ref[:, s] for s in vsl]
        sg = [_sigmoid(o_ref[:, s]) for s in vsl]
        dyh = [dy_ref[:, s] for s in vsl]
        norm = [_rms(a) for a in hh]
        hhat, r = [a for a, _ in norm], [b for _, b in norm]
        dyn = _each(lambda a, b: a * b, dyh, sg)
        do = _each(lambda d, hx, g, s: d * hx * g * s * (1.0 - s), dyh, hhat, hn_h, sg)
        dhn = _each(lambda a, b: jnp.sum(a * b, axis=0, keepdims=True), dyn, hhat)
        dh = _each(_rms_bwd, hhat, r, hn_h, dyn)
        inv = [1.0 / a for a in f["nrm"]]
        dnum = _each(lambda a, b: a * b, dh, inv)
        dnrm = _each(lambda a, b, c: -jnp.sum(a * b, axis=1, keepdims=True) * c, dh, hh, inv)
        dden = _each(lambda d, e, g: jnp.where(jnp.abs(d) > e, g * jnp.sign(d), 0.0), f["den"], f["emt"], dnrm)
        dnb = [bf(a) for a in dnum]
        dnt = [bf(a.T) for a in dnum]
        rmat = _each(lambda a, b, d: mm(a, b) + d, dnb, vth, dden)
        rmat_t = _each(lambda a, b, d: mm(a, b) + as_row(d), vh, dnt, dden)
        ds = _each(lambda w, a: bf(w * a), f["wt"], rmat)
        ds_t = _each(lambda w, a: bf(w * a), wt_t, rmat_t)
        dv = _each(lambda a, b: mm(bf(a), b), p_t, dnb)
        dqs = _each(lambda s, kk, w, d, ct, dd, n: mm(s, kk) + w * (mm(d, ct) + dd * n),
                    ds, kh, winter, dnb, c_t, dden, n_in)
        dk = _each(lambda s, a: mm(s, a) * QK_SCALE, ds_t, qh)
        dinter = _each(lambda qc, dn_, qn, dd, w: (jnp.sum(qc * dn_, axis=1, keepdims=True) + qn * dd) * w,
                       f["qc"], dnum, f["qn"], dden, winter)
        wq = _each(lambda w, a: w * a * QK_SCALE, winter, f["qf"])
        wq_t = _each(lambda w, a: bf(w * a.astype(F32) * QK_SCALE), winter_row, qth)
        dc_loc = _each(mm, wq_t, dnb)
        dct_loc = _each(lambda a, b: mm(a, bf(b)), dnt, wq)
        dn_loc = _each(lambda a, d: jnp.sum(a * d, axis=0, keepdims=True), wq, dden)
        cs_q = _each(lambda a, b: jnp.sum(a * b, axis=1, keepdims=True), p_t, rmat_t)
        db = _each(lambda a, b, di, cs: jnp.sum(a * b, axis=1, keepdims=True) + di - cs, p, rmat, dinter, cs_q)
        ddecay = _each(lambda dc_, c, dn_, n: jnp.sum(jnp.sum(dc_ * c, axis=1, keepdims=True), axis=0, keepdims=True)
                       + jnp.sum(dn_ * n, axis=1, keepdims=True), dcn, c_in, dnn, n_in)
        dkw = _each(lambda a, b, n: mm(a, bf(b)) + n, vh, dcn_t, dnn)
        dk = _each(lambda a, w, b: a + w * b, dk, f["wkf"], dkw)
        da = _each(lambda a, kk, w: jnp.sum(a * kk.astype(F32), axis=1, keepdims=True) * w, dkw, kh, f["wkf"])
        dv = _each(lambda a, kw, dc_: a + mm(bf(kw), bf(dc_)), dv, f["kw"], dcn)
        dgtot = _each(lambda a, dd, d: jnp.sum(a, axis=0, keepdims=True) + dd * d, da, ddecay, decay)
        db = _each(lambda a, b, g: a - b + jnp.where(row == CHUNK - 1, g, 0.0), db, da, dgtot)
        dli = _each(lambda a, b: a + b, cs_q, da)
        dc_new = _each(lambda d, a, b: d * a + b, decay, dcn, dc_loc)
        dct_new = _each(lambda d, a, b: d * a + b, decay, dcn_t, dct_loc)
        dn_new = _each(lambda d, a, b: d * a + b, decay, dnn, dn_loc)
        dx = jnp.zeros((CHUNK, 128), F32)
        for h in HEADS:
            dhn_ref[:, vsl[h]] += dhn[h]
            dc_sc[ksl[h], :] = dc_new[h]
            dct_sc[vsl[h], :] = dct_new[h]
            dn_sc[h:h + 1, :] = dn_new[h]
            dp_ref[:, ksl[h]] = bf(dqs[h] * QK_SCALE)
            dp_ref[:, ML_QK + ML_DK * h:ML_QK + ML_DK * (h + 1)] = bf(dk[h])
            dp_ref[:, 2 * ML_QK + ML_DV * h:2 * ML_QK + ML_DV * (h + 1)] = bf(dv[h])
            dp_ref[:, 2 * ML_QK + ML_V + ML_DV * h:2 * ML_QK + ML_V + ML_DV * (h + 1)] = bf(do[h])
            dx = jnp.where(lane == h, dli[h], dx)
            dx = jnp.where(lane == ML_HEADS + h, db[h], dx)
        dlf = _cumsum_rows(dx, reverse=True)
        dact = jnp.where(lane < ML_HEADS, dx, dlf * _sigmoid(-act))
        dz = dact * (1.0 - th * th)
        dp_ref[:, 2 * ML_QK + 2 * ML_V:] = dz.astype(BF16)
        dbif_ref[...] += jnp.sum(dz, axis=0, keepdims=True)

    rev = lambda i: (nc - 1 - i, 0)
    rev3 = lambda i: (nc - 1 - i, 0, 0)
    return _side_call(
        body, side, name="mlstm_bwd", steps=nc,
        in_specs=_ml_specs(nc, True) + [
            _full((1, 128)), _full((1, ML_V)), pl.BlockSpec((CHUNK, ML_V), rev),
            pl.BlockSpec((1, ML_HEADS * ML_DK, ML_DV), rev3), pl.BlockSpec((1, ML_HEADS, ML_DK), rev3),
            pl.BlockSpec((1, ML_HEADS, 128), rev3), pl.BlockSpec((CHUNK, ML_V), rev)],
        out_specs=[pl.BlockSpec((CHUNK, ML_IN_PAD), rev), _full((1, ML_V)), _full((1, 128))],
        out_shape=[_sds((t, ML_IN_PAD), BF16), _sds((1, ML_V), F32), _sds((1, 128), F32)],
        scratch_shapes=[pltpu.VMEM((ML_HEADS * ML_DK, ML_DV), F32), pltpu.VMEM((ML_HEADS * ML_DV, ML_DK), F32),
                        pltpu.VMEM((ML_HEADS, ML_DK), F32)],
        args=(qkv, qkv, qkv, og, og, bif, hn, hs, cst, nst, mst, dy))


LRU_TM = 256
GELU_K = math.sqrt(2.0 / math.pi)
GELU_C = 0.044715


def _gelu(x):
    th = jnp.tanh(GELU_K * (x + GELU_C * x * x * x))
    return 0.5 * x * (1.0 + th), th


def _neg_expm1(x):
    series = -x * (1.0 + x * (0.5 + x * (1.0 / 6.0 + x * (1.0 / 24.0))))
    return jnp.where(x > -0.05, series, 1.0 - jnp.exp(x))


def _block_diag_dot(a, w_ref, dims):
    parts = [lax.dot_general(a[:, LRU_BLOCK * n:LRU_BLOCK * (n + 1)], w_ref[n], dims, preferred_element_type=F32)
             for n in range(LRU_BLOCKS)]
    return jnp.concatenate(parts, axis=1)


def _lru_gates(u, r, ig, lam):
    ls = _log_sigmoid(lam)
    la = LRU_C * r * ls
    a = jnp.exp(la)
    em = _neg_expm1(2.0 * la)
    mult = jnp.sqrt(em)
    return ls, a, em, mult


def _lru_fwd(proj, cw, cb, wa, ba, wx, bx, lam, side):
    t = proj.shape[0]
    w = D_MODEL
    tm = min(LRU_TM, t)

    def body(gb_ref, up_ref, cw_ref, cb_ref, wa_ref, ba_ref, wx_ref, bx_ref, lam_ref,
             y_ref, u_ref, r_ref, i_ref, h_ref, tail_sc, hprev_sc):
        @pl.when(pl.program_id(0) == 0)
        def _():
            tail_sc[...] = jnp.zeros_like(tail_sc)
            hprev_sc[...] = jnp.zeros_like(hprev_sc)

        up = up_ref[...]
        ext = jnp.concatenate([tail_sc[...], up], axis=0)
        u = cb_ref[...] + cw_ref[CONV_WIDTH - 1:CONV_WIDTH, :] * up
        for s in range(1, CONV_WIDTH):
            u = u + cw_ref[CONV_WIDTH - 1 - s:CONV_WIDTH - s, :] * pltpu.roll(ext, s, 0)[8:8 + tm]
        tail_sc[...] = up[tm - 8:tm]
        ub = u.astype(BF16)
        r = _sigmoid(_block_diag_dot(ub, wa_ref, (((1,), (0,)), ((), ()))) + ba_ref[...])
        ig = _sigmoid(_block_diag_dot(ub, wx_ref, (((1,), (0,)), ((), ()))) + bx_ref[...])
        _, a, _, mult = _lru_gates(u, r, ig, lam_ref[...])
        h = _scan_rows(a, mult * ig * u, hprev_sc[0:1, :])
        hprev_sc[0:1, :] = h[tm - 1:tm]
        u_ref[...] = u
        r_ref[...] = r
        i_ref[...] = ig
        h_ref[...] = h
        gel, _ = _gelu(gb_ref[...])
        y_ref[...] = (h * gel).astype(BF16)

    vec = _full((1, w))
    wspec = _full((LRU_BLOCKS, LRU_BLOCK, LRU_BLOCK))
    return _side_call(
        body, side, name="lru_fwd", steps=t // tm,
        in_specs=[pl.BlockSpec((tm, w), lambda i: (i, 0)), pl.BlockSpec((tm, w), lambda i: (i, 1)),
                  _full((CONV_WIDTH, w)), vec, wspec, vec, wspec, vec, vec],
        out_specs=[_rows(tm, w)] * 5,
        out_shape=[_sds((t, w), BF16)] + [_sds((t, w), F32)] * 4,
        scratch_shapes=[pltpu.VMEM((8, w), F32), pltpu.VMEM((8, w), F32)],
        args=(proj, proj, cw, cb, wa, ba, wx, bx, lam))


def _lru_bwd(proj, cw, wa, wx, lam, u, r, ig, h, dy, side):
    t = proj.shape[0]
    w = D_MODEL
    tm = min(LRU_TM, t)
    nt = t // tm

    def body(gb_ref, up_ref, cw_ref, wa_ref, wx_ref, lam_ref, u_ref, r_ref, i_ref, h_ref, hp_ref, dy_ref,
             dp_ref, dcw_ref, dcb_ref, dwa_ref, dba_ref, dwx_ref, dbx_ref, dlam_ref, carry_sc, dutail_sc, dls_sc):
        step = pl.program_id(0)

        @pl.when(step == 0)
        def _():
            carry_sc[...] = jnp.zeros_like(carry_sc)
            dutail_sc[...] = jnp.zeros_like(dutail_sc)
            dls_sc[...] = jnp.zeros_like(dls_sc)
            for ref in (dcw_ref, dcb_ref, dwa_ref, dba_ref, dwx_ref, dbx_ref):
                ref[...] = jnp.zeros_like(ref)

        row = lax.broadcasted_iota(jnp.int32, (tm, w), 0)
        u_t, r_t, i_t, h_t = u_ref[...], r_ref[...], i_ref[...], h_ref[...]
        ls, a, em, mult = _lru_gates(u_t, r_t, i_t, lam_ref[...])
        gb = gb_ref[...]
        gel, th = _gelu(gb)
        dyv = dy_ref[...]
        dgb = dyv * h_t * (0.5 * (1.0 + th) + 0.5 * gb * (1.0 - th * th) * GELU_K * (1.0 + 3.0 * GELU_C * gb * gb))
        a_next = jnp.where(row < tm - 1, pltpu.roll(a, tm - 1, 0), 1.0)
        g = _scan_rows(a_next, dyv * gel, carry_sc[0:1, :], reverse=True)
        carry_sc[0:1, :] = a[0:1] * g[0:1]
        has_prev = jnp.where(step == nt - 1, 0.0, 1.0)
        h_prev = jnp.where(row >= 1, pltpu.roll(h_t, 1, 0), hp_ref[7:8, :] * has_prev)
        dmult = g * i_t * u_t
        dig = g * mult * u_t
        du = g * mult * i_t
        dla = g * h_prev * a - dmult * (1.0 - em) / mult
        dls_sc[0:1, :] += jnp.sum(dla * r_t, axis=0, keepdims=True) * LRU_C
        dpa = dla * (LRU_C * ls) * r_t * (1.0 - r_t)
        dpx = dig * i_t * (1.0 - i_t)
        dba_ref[...] += jnp.sum(dpa, axis=0, keepdims=True)
        dbx_ref[...] += jnp.sum(dpx, axis=0, keepdims=True)
        ub = u_t.astype(BF16)
        dpab = dpa.astype(BF16)
        dpxb = dpx.astype(BF16)
        for n in range(LRU_BLOCKS):
            cs = slice(LRU_BLOCK * n, LRU_BLOCK * (n + 1))
            dwa_ref[n] += lax.dot_general(ub[:, cs], dpab[:, cs], TN_DIMS, preferred_element_type=F32)
            dwx_ref[n] += lax.dot_general(ub[:, cs], dpxb[:, cs], TN_DIMS, preferred_element_type=F32)
        du = du + _block_diag_dot(dpab, wa_ref, NT_DIMS) + _block_diag_dot(dpxb, wx_ref, NT_DIMS)
        dcb_ref[...] += jnp.sum(du, axis=0, keepdims=True)
        ext = jnp.concatenate([du, dutail_sc[...]], axis=0)
        up = up_ref[...]
        dup = cw_ref[CONV_WIDTH - 1:CONV_WIDTH, :] * du
        dcw_ref[CONV_WIDTH - 1:CONV_WIDTH, :] += jnp.sum(up * du, axis=0, keepdims=True)
        for s in range(1, CONV_WIDTH):
            du_s = pltpu.roll(ext, tm + 8 - s, 0)[0:tm]
            dup = dup + cw_ref[CONV_WIDTH - 1 - s:CONV_WIDTH - s, :] * du_s
            dcw_ref[CONV_WIDTH - 1 - s:CONV_WIDTH - s, :] += jnp.sum(up * du_s, axis=0, keepdims=True)
        dutail_sc[...] = du[0:8]
        dp_ref[:, :w] = dgb.astype(BF16)
        dp_ref[:, w:] = dup.astype(BF16)

        @pl.when(step == nt - 1)
        def _():
            dlam_ref[...] = dls_sc[0:1, :] * _sigmoid(-lam_ref[...])

    rev = lambda col: (lambda i: (nt - 1 - i, col))
    vec = _full((1, w))
    wspec = _full((LRU_BLOCKS, LRU_BLOCK, LRU_BLOCK))
    tile = pl.BlockSpec((tm, w), rev(0))
    prev8 = pl.BlockSpec((8, w), lambda i: (jnp.maximum((nt - 1 - i) * (tm // 8) - 1, 0), 0))
    return _side_call(
        body, side, name="lru_bwd", steps=nt,
        in_specs=[tile, pl.BlockSpec((tm, w), rev(1)), _full((CONV_WIDTH, w)), wspec, wspec, vec,
                  tile, tile, tile, tile, prev8, tile],
        out_specs=[pl.BlockSpec((tm, 2 * w), rev(0)), _full((CONV_WIDTH, w)), vec, wspec, vec, wspec, vec, vec],
        out_shape=[_sds((t, 2 * w), BF16), _sds((CONV_WIDTH, w), F32), _sds((1, w), F32),
                   _sds((LRU_BLOCKS, LRU_BLOCK, LRU_BLOCK), F32), _sds((1, w), F32),
                   _sds((LRU_BLOCKS, LRU_BLOCK, LRU_BLOCK), F32), _sds((1, w), F32), _sds((1, w), F32)],
        scratch_shapes=[pltpu.VMEM((8, w), F32), pltpu.VMEM((8, w), F32), pltpu.VMEM((8, w), F32)],
        args=(proj, proj, cw, wa, wx, lam, u, r, ig, h, h, dy))


LANES = 1024
HALF_FFN = D_FF // N_CHIPS
GROUPS = {
    "ml": (("ml_w_in", 1024, ML_IN // N_CHIPS), ("ml_w_out", 256, 1024)),
    "lru": (("lru_w_in", 1024, 512),
            ("lru_gates", 2 * LRU_BLOCKS * 64, LRU_BLOCK),
            ("lru_w_out", 256, 1024)),
    "ffn0": (("ffn_g0", HALF_FFN, 1024), ("ffn_u0", HALF_FFN, 1024), ("ffn_down0", HALF_FFN, 1024)),
    "ffn1": (("ffn_g1", HALF_FFN, 1024), ("ffn_u1", HALF_FFN, 1024), ("ffn_down1", HALF_FFN, 1024)),
}

SMALL_ROWS = 24
ROW_LOSS, ROW_HEAD_NORM, ROW_B_IF, ROW_LRU = 8, 9, 10, 11


def _row_tile(rows, cols, itemsize, budget=3 << 19):
    best = 16
    for t in range(16, rows + 1, 16):
        if rows % t == 0 and t * cols * itemsize <= budget:
            best = t
    return best


def _chip_peers():
    x, y, c = lax.axis_index("x"), lax.axis_index("y"), lax.axis_index("c")
    return x, y, c, [(1 - x, y), (x, 1 - y), (1 - x, 1 - y)]


HBM_SPEC = pl.BlockSpec(memory_space=pltpu.HBM)


def _remote(src, dst, send_sems, recv_sems, k, to):
    return pltpu.make_async_remote_copy(src_ref=src, dst_ref=dst, send_sem=send_sems.at[k], recv_sem=recv_sems.at[k],
                                        device_id=to, device_id_type=MESH)


GATHER_SEMS = 7


def _gather_copies(kind, specs, part_refs, out_refs, send_sems, recv_sems):
    x, y, c, chips = _chip_peers()
    me = 2 * x + y
    sib = (x, y, 1 - c)
    copy = functools.partial(_remote, send_sems=send_sems, recv_sems=recv_sems)
    out = []
    for p, (_, rows, _) in enumerate(specs):
        mine = pl.ds(c * (rows // 2), rows // 2)
        theirs = pl.ds((1 - c) * (rows // 2), rows // 2)
        base = GATHER_SEMS * p
        for j, (cx, cy) in enumerate(chips):
            land = out_refs[p].at[2 * cx + cy, mine]
            other = out_refs[p].at[2 * cx + cy, theirs]
            if kind == "first":
                out.append(copy(part_refs[p].at[mine], out_refs[p].at[me, mine], k=base + j, to=(cx, cy, c)))
            elif kind == "landed":
                out.append(copy(land, land, k=base + j, to=sib))
            elif kind == "forward":
                out.append(copy(land, land, k=base + 3 + j, to=sib))
            else:
                out.append(copy(other, other, k=base + 3 + j, to=sib))
        if kind in ("first", "arriving"):
            out.append(copy(part_refs[p], out_refs[p].at[me], k=base + 6, to=sib))
    return out


def _gather_start(specs, *refs):
    for cp in _gather_copies("first", specs, *refs):
        cp.start()


def _gather_finish(specs, *refs):
    forwards = _gather_copies("forward", specs, *refs)
    for land, fwd in zip(_gather_copies("landed", specs, *refs), forwards):
        land.wait_recv()
        fwd.start()
    for cp in _gather_copies("arriving", specs, *refs):
        cp.wait_recv()
    for cp in _gather_copies("first", specs, *refs) + forwards:
        cp.wait_send()


def _gather_shapes(specs):
    return [_sds((N_CHIPS, rows, cols), BF16) for _, rows, cols in specs]


def _gather_side(specs, parts):
    return _Side(inputs=list(parts), out_shape=_gather_shapes(specs), n_sems=GATHER_SEMS * len(specs),
                 start=functools.partial(_gather_start, specs), finish=functools.partial(_gather_finish, specs))


def _gather_weights(specs, parts, small):
    n = len(specs)

    def body(*refs):
        part_refs, small_ref = refs[:n], refs[n]
        out_refs, outs_ref = refs[n + 1:2 * n + 1], refs[2 * n + 1]
        send_sems, recv_sems, small_send, small_recv, loc_sem = refs[2 * n + 2:]
        x, y, c, chips = _chip_peers()
        me = 2 * x + y
        local = pltpu.make_async_copy(small_ref, outs_ref.at[me], loc_sem.at[0])
        local.start()
        _gather_start(specs, part_refs, out_refs, send_sems, recv_sems)
        sent = [_remote(small_ref, outs_ref.at[me], small_send, small_recv, j, (cx, cy, c))
                for j, (cx, cy) in enumerate(chips)]
        for cp in sent:
            cp.start()
        _gather_finish(specs, part_refs, out_refs, send_sems, recv_sems)
        for j, (cx, cy) in enumerate(chips):
            _remote(small_ref, outs_ref.at[2 * cx + cy], small_send, small_recv, j, (cx, cy, c)).wait_recv()
        for cp in sent:
            cp.wait_send()
        local.wait()

    dma = pltpu.SemaphoreType.DMA
    return pl.pallas_call(
        body, name="gather_weights",
        in_specs=[HBM_SPEC] * (n + 1), out_specs=[HBM_SPEC] * (n + 1),
        out_shape=_gather_shapes(specs) + [_sds((N_CHIPS,) + small.shape, small.dtype)],
        scratch_shapes=[dma((GATHER_SEMS * n,)), dma((GATHER_SEMS * n,)), dma((3,)), dma((3,)), dma((1,))],
    )(*parts, small)


def _exchange_copies(specs, g_refs, out_refs, send_sems, recv_sems):
    x, y, c, _ = _chip_peers()
    return [_remote(g_refs[p].at[:, pl.ds((1 - c) * (rows // 2), rows // 2)], out_refs[p], send_sems, recv_sems, p,
                    (x, y, 1 - c)) for p, (_, rows, _) in enumerate(specs)]


def _exchange_start(specs, *refs):
    for cp in _exchange_copies(specs, *refs):
        cp.start()


def _exchange_finish(specs, *refs):
    for cp in _exchange_copies(specs, *refs):
        cp.wait()


def _exchange_shapes(specs):
    return [_sds((N_CHIPS, rows // 2, cols), BF16) for _, rows, cols in specs]


def _exchange_side(specs, gparts):
    return _Side(inputs=list(gparts), out_shape=_exchange_shapes(specs), n_sems=len(specs),
                 start=functools.partial(_exchange_start, specs), finish=functools.partial(_exchange_finish, specs))


def _exchange_halves(specs, gparts, name):
    n = len(specs)

    def body(*refs):
        _exchange_start(specs, refs[:n], refs[n:2 * n], *refs[2 * n:])
        _exchange_finish(specs, refs[:n], refs[n:2 * n], *refs[2 * n:])

    return pl.pallas_call(
        body, name=name, in_specs=[HBM_SPEC] * n, out_specs=[HBM_SPEC] * n, out_shape=_exchange_shapes(specs),
        scratch_shapes=[pltpu.SemaphoreType.DMA((n,)), pltpu.SemaphoreType.DMA((n,))],
    )(*gparts)


def _add_halves(g, recv, pos, name):
    _, half, cols = recv.shape
    tr = _row_tile(half, cols, 2)
    tiles = half // tr

    def body(pos_ref, a_ref, b_ref, o_ref):
        o_ref[...] = (a_ref[...].astype(F32) + b_ref[...].astype(F32)).astype(BF16)

    spec = pl.BlockSpec((1, tr, cols), lambda k, i, pos_ref: (k, i, 0))
    return pl.pallas_call(
        body, name=name,
        grid_spec=pltpu.PrefetchScalarGridSpec(
            num_scalar_prefetch=1, grid=(N_CHIPS, tiles),
            in_specs=[pl.BlockSpec((1, tr, cols), lambda k, i, pos_ref: (k, pos_ref[0] * tiles + i, 0)), spec],
            out_specs=spec),
        out_shape=_sds((N_CHIPS, half, cols), BF16),
        compiler_params=_params("parallel", "parallel"),
    )(pos, g, recv)


def _scatter_copies(n, s1_refs, recv_refs, send_sems, recv_sems):
    x, y, c, chips = _chip_peers()
    return [_remote(s1_refs[p].at[2 * cx + cy], recv_refs[p].at[j], send_sems, recv_sems, 3 * p + j, (cx, cy, c))
            for p in range(n) for j, (cx, cy) in enumerate(chips)]


def _scatter_start(n, s1_refs, recv_refs, send_sems, recv_sems):
    for cp in _scatter_copies(n, s1_refs, recv_refs, send_sems, recv_sems):
        cp.start()


def _scatter_finish(n, s1_refs, recv_refs, send_sems, recv_sems):
    for cp in _scatter_copies(n, s1_refs, recv_refs, send_sems, recv_sems):
        cp.wait()


def _scatter_shapes(s1):
    return [_sds((3,) + a.shape[1:], a.dtype) for a in s1]


def _scatter_side(s1):
    n = len(s1)
    return _Side(inputs=list(s1), out_shape=_scatter_shapes(s1), n_sems=3 * n,
                 start=functools.partial(_scatter_start, n), finish=functools.partial(_scatter_finish, n))


def _share_small(small):
    flips = [(fx, fy, fc) for fx in (0, 1) for fy in (0, 1) for fc in (0, 1)][1:]

    def body(small_ref, all_ref, send_sems, recv_sems, loc_sem):
        x, y, c, _ = _chip_peers()
        my_slot = all_ref.at[4 * x + 2 * y + c]
        local = pltpu.make_async_copy(small_ref, my_slot, loc_sem.at[0])
        local.start()
        peers = [(1 - x if fx else x, 1 - y if fy else y, 1 - c if fc else c) for fx, fy, fc in flips]
        sent = [_remote(small_ref, my_slot, send_sems, recv_sems, i, p) for i, p in enumerate(peers)]
        for cp in sent:
            cp.start()
        for i, (px, py, pc) in enumerate(peers):
            _remote(small_ref, all_ref.at[4 * px + 2 * py + pc], send_sems, recv_sems, i, peers[i]).wait_recv()
        for cp in sent:
            cp.wait_send()
        local.wait()

    dma = pltpu.SemaphoreType.DMA
    return pl.pallas_call(
        body, name="share_small", in_specs=[HBM_SPEC], out_specs=HBM_SPEC,
        out_shape=_sds((8,) + small.shape, small.dtype), scratch_shapes=[dma((7,)), dma((7,)), dma((1,))],
    )(small)


def _sum_chips(s1, recv, pos, name):
    _, half, cols = recv.shape
    tr = _row_tile(half, cols, 4)
    tiles = half // tr

    def body(pos_ref, a_ref, b_ref, o_ref):
        acc = a_ref[0].astype(F32)
        for j in range(3):
            acc = acc + b_ref[j].astype(F32)
        o_ref[...] = acc

    return pl.pallas_call(
        body, name=name,
        grid_spec=pltpu.PrefetchScalarGridSpec(
            num_scalar_prefetch=1, grid=(tiles,),
            in_specs=[pl.BlockSpec((1, tr, cols), lambda i, pos_ref: (pos_ref[1], i, 0)),
                      pl.BlockSpec((3, tr, cols), lambda i, pos_ref: (0, i, 0))],
            out_specs=pl.BlockSpec((tr, cols), lambda i, pos_ref: (pos_ref[0] * tiles + i, 0))),
        out_shape=_sds((2 * half, cols), F32),
        compiler_params=_params("parallel"),
    )(pos, s1, recv)


def _sum_small(small_all):
    def body(a_ref, o_ref):
        acc = a_ref[0]
        for d in range(1, 8):
            acc = acc + a_ref[d]
        o_ref[...] = acc

    return pl.pallas_call(body, name="sum_small", out_shape=_sds(small_all.shape[1:], F32))(small_all)


def _join_halves(specs, s2):
    n = len(specs)

    def body(*refs):
        buf_refs = refs[n:2 * n]
        send_sems, recv_sems = refs[2 * n:]
        x, y, c, _ = _chip_peers()
        sent = []
        for p, (_, rows, _) in enumerate(specs):
            mine = buf_refs[p].at[pl.ds(c * (rows // 2), rows // 2)]
            sent.append(_remote(mine, mine, send_sems, recv_sems, p, (x, y, 1 - c)))
            sent[-1].start()
        for p, (_, rows, _) in enumerate(specs):
            theirs = buf_refs[p].at[pl.ds((1 - c) * (rows // 2), rows // 2)]
            _remote(theirs, theirs, send_sems, recv_sems, p, (x, y, 1 - c)).wait_recv()
        for cp in sent:
            cp.wait_send()

    return pl.pallas_call(
        body, name="join_halves", in_specs=[HBM_SPEC] * n, out_specs=[HBM_SPEC] * n,
        out_shape=[_sds(a.shape, a.dtype) for a in s2],
        input_output_aliases={p: p for p in range(n)},
        scratch_shapes=[pltpu.SemaphoreType.DMA((n,)), pltpu.SemaphoreType.DMA((n,))],
    )(*s2)


def _adamw(w, g, m, v, name):
    rows, cols = w.shape[0], w.shape[-1]
    tm = rows
    for cand in (512, 256, 128, 64, 32, 16, 8) if w.ndim == 2 else (rows // 4,):
        if rows % cand == 0 and rows > cand:
            tm = cand
            break

    def body(w_ref, g_ref, m_ref, v_ref, d_ref, nm_ref, nv_ref):
        gv = g_ref[...]
        nm = ADAM_B1 * m_ref[...] + (1.0 - ADAM_B1) * gv
        nv = ADAM_B2 * v_ref[...] + (1.0 - ADAM_B2) * (gv * gv)
        m_hat = nm / (1.0 - ADAM_B1 ** ADAM_STEP)
        v_hat = nv / (1.0 - ADAM_B2 ** ADAM_STEP)
        d_ref[...] = -ADAM_LR * (m_hat / (jnp.sqrt(v_hat) + ADAM_EPS) + ADAM_WD * w_ref[...])
        nm_ref[...] = nm
        nv_ref[...] = nv

    spec = _rows(tm, cols) if w.ndim == 2 else pl.BlockSpec((tm, 1, cols), lambda i: (i, 0, 0))
    return pl.pallas_call(
        body, name=name, grid=(rows // tm,), in_specs=[spec] * 4, out_specs=[spec] * 3,
        out_shape=[_sds(w.shape, F32)] * 3, compiler_params=_params("parallel"),
    )(w, g, m, v)


WEIGHTS = ("ml_w_in", "ml_b_if", "ml_head_norm", "ml_w_out", "lru_w_in", "lru_conv_w", "lru_conv_b", "lru_w_gate_a",
           "lru_b_gate_a", "lru_w_gate_x", "lru_b_gate_x", "lru_lambda", "lru_w_out", "norm_pre_mix", "norm_post_mix",
           "norm_pre_ffn", "norm_post_ffn", "ffn_w_gate", "ffn_w_up", "ffn_w_down")
LRU_VECTORS = ("lru_conv_b", "lru_b_gate_a", "lru_b_gate_x", "lru_lambda")
NORMS = ("norm_pre_mix", "norm_post_mix", "norm_pre_ffn", "norm_post_ffn")


def _by_cols(a):
    return jnp.concatenate([a[k] for k in range(N_CHIPS)], axis=-1)


def _by_chip(a, width):
    return jnp.stack([a[..., k * width:(k + 1) * width] for k in range(N_CHIPS)])


def _weight_shards(w):
    bf = lambda a: a.astype(BF16)
    shards = dict(
        ml_w_in=bf(w["ml_w_in"][0]), ml_w_out=bf(w["ml_w_out"][0]), lru_w_in=bf(w["lru_w_in"][0]),
        lru_gates=bf(jnp.concatenate([w["lru_w_gate_a"][0], w["lru_w_gate_x"][0]], axis=0)).reshape(-1, LRU_BLOCK),
        lru_w_out=bf(w["lru_w_out"][0]))
    gate_t, up_t = bf(jnp.swapaxes(w["ffn_w_gate"], 1, 2)), bf(jnp.swapaxes(w["ffn_w_up"], 1, 2))
    for layer in range(2):
        shards[f"ffn_g{layer}"] = gate_t[layer]
        shards[f"ffn_u{layer}"] = up_t[layer]
        shards[f"ffn_down{layer}"] = bf(w["ffn_w_down"][layer])
    return shards


def _ffn_weights(parts):
    return [a.reshape(D_FF, 1024) for a in parts]


def _ffn_grads(dgu, z, act, dy, layer):
    dws = [_matmul_tn(dgu, z, True, 256, f"ffn{layer}_dw_gate", window=(0, D_FF)),
           _matmul_tn(dgu, z, True, 256, f"ffn{layer}_dw_up", window=(D_FF, D_FF)),
           _matmul_tn(act, dy, True, 256, f"ffn{layer}_dw_down")]
    return [a.reshape(N_CHIPS, HALF_FFN, 1024) for a in dws]


def _sum_group(group, s1, recv, pos):
    return [_sum_chips(a, r, pos, "sum_chips_" + n) for (n, _, _), a, r in zip(GROUPS[group], s1, recv)]


def _update(w, grads, m, v):
    delta, new_m, new_v = {}, {}, {}
    for n in WEIGHTS:
        if n == "ml_w_in":
            view = lambda a: jnp.transpose(a, (2, 0, 1))
            back = lambda a: jnp.transpose(a, (1, 2, 0))
        else:
            flip = (lambda a: jnp.swapaxes(a, 1, 2)) if n in ("ffn_w_gate", "ffn_w_up") else (lambda a: a)
            shape = flip(w[n]).shape
            view = lambda a: flip(a).reshape(-1, shape[-1])
            back = lambda a: flip(a.reshape(shape))
        d, nm, nv = _adamw(view(w[n]), view(grads[n]), view(m[n]), view(v[n]), "adamw_" + n)
        delta[n], new_m[n], new_v[n] = back(d), back(nm), back(nv)
    return delta, new_m, new_v


def kernel(x, ml_w_in, ml_b_if, ml_head_norm, ml_w_out, lru_w_in, lru_conv_w, lru_conv_b, lru_w_gate_a, lru_b_gate_a, lru_w_gate_x, lru_b_gate_x, lru_lambda, lru_w_out, norm_pre_mix, norm_post_mix, norm_pre_ffn, norm_post_ffn, ffn_w_gate, ffn_w_up, ffn_w_down, loss_target, m_ml_w_in, m_ml_b_if, m_ml_head_norm, m_ml_w_out, m_lru_w_in, m_lru_conv_w, m_lru_conv_b, m_lru_w_gate_a, m_lru_b_gate_a, m_lru_w_gate_x, m_lru_b_gate_x, m_lru_lambda, m_lru_w_out, m_norm_pre_mix, m_norm_post_mix, m_norm_pre_ffn, m_norm_post_ffn, m_ffn_w_gate, m_ffn_w_up, m_ffn_w_down, v_ml_w_in, v_ml_b_if, v_ml_head_norm, v_ml_w_out, v_lru_w_in, v_lru_conv_w, v_lru_conv_b, v_lru_w_gate_a, v_lru_b_gate_a, v_lru_w_gate_x, v_lru_b_gate_x, v_lru_lambda, v_lru_w_out, v_norm_pre_mix, v_norm_post_mix, v_norm_pre_ffn, v_norm_post_ffn, v_ffn_w_gate, v_ffn_w_up, v_ffn_w_down):
    args = locals()
    w = {n: args[n] for n in WEIGHTS}
    m = {n: args["m_" + n] for n in WEIGHTS}
    v = {n: args["v_" + n] for n in WEIGHTS}
    xs, target = x[0], loss_target[0]
    mx, my, mc, _ = _chip_peers()
    chip = 2 * mx + my
    pos = jnp.stack([mc, chip])
    row = lambda a, i: a[i:i + 1]
    npm, nqm, npf, nqf = (w[n] for n in NORMS)
    shards = _weight_shards(w)
    of = lambda group: [shards[n] for n, _, _ in GROUPS[group]]
    bif = jnp.pad(w["ml_b_if"], ((0, 0), (0, 128 - 2 * ML_HEADS)))
    hn = w["ml_head_norm"]

    small = jnp.concatenate([w["lru_conv_w"][0]] + [w[n] for n in LRU_VECTORS], axis=0)
    ml_in_spec, ml_out_spec = GROUPS["ml"][:1], GROUPS["ml"][1:]
    ml_in_parts, smalls = _gather_weights(ml_in_spec, of("ml")[:1], small)
    ml_w_in = jnp.pad(_by_cols(ml_in_parts), ((0, 0), (0, ML_IN_PAD - ML_IN)))
    vec = _by_cols(smalls)
    lru_spec, ffn1_spec = GROUPS["lru"], GROUPS["ffn1"]
    (z0, qkv, og), got = _norm_matmul(xs, row(npm, 0), ml_w_in, 2 * ML_QK + ML_V, "ml_in",
                                      _gather_side(ml_out_spec, of("ml")[1:]))
    ml_w_out = got[0].reshape(1024, 1024)
    (y0, hs, cst, nst, mst), got = _mlstm_fwd(qkv, og, bif, hn, _gather_side(GROUPS["ffn0"], of("ffn0")))
    wg0, wu0, w_down0 = _ffn_weights(got)
    (ymix0, h1), got = _matmul_postnorm(y0, ml_w_out, row(nqm, 0), xs, "ml_out",
                                        _gather_side(lru_spec[:1], of("lru")[:1]))
    lru_w_in = _by_cols(got[0])
    (zf0, gu0, yf0, h2), got = _ffn_forward(
        h1, row(npf, 0), wg0, wu0, w_down0, row(nqf, 0), "ffn0_fwd",
        _gather_side(lru_spec[1:] + ffn1_spec[:2], of("lru")[1:] + of("ffn1")[:2]))
    gates = got[0].reshape(N_CHIPS, 2, LRU_BLOCKS, 64, LRU_BLOCK).transpose(1, 2, 0, 3, 4)
    gates = gates.reshape(2, LRU_BLOCKS, LRU_BLOCK, LRU_BLOCK)
    lru_w_out = got[1].reshape(1024, 1024)
    ffn1_gu = got[2:4]
    (z1, proj1), _ = _norm_matmul(h2, row(npm, 1), lru_w_in, 0, "lru_in", None)
    (y1, u, r, ig, hl), got = _lru_fwd(proj1, vec[0:4], vec[4:5], gates[0], vec[5:6], gates[1], vec[6:7], vec[7:8],
                                       _gather_side(ffn1_spec[2:], of("ffn1")[2:]))
    wg1, wu1, w_down1 = _ffn_weights(ffn1_gu + got)
    (ymix1, h3), _ = _matmul_postnorm(y1, lru_w_out, row(nqm, 1), h2, "lru_out", None)
    (zf1, gu1, yf1, h4), _ = _ffn_forward(h3, row(npf, 1), wg1, wu1, w_down1, row(nqf, 1), "ffn1_fwd", None)
    dh4, loss_part = _loss_head(h4, target, "loss_head")

    add = lambda group, gparts, recv: [_add_halves(g, r, pos, "add_halves_" + n)
                                       for (n, _, _), g, r in zip(GROUPS[group], gparts, recv)]
    (dyf1, dgu1, act1, dh3, dqf1, dpf1), _ = _ffn_backward(
        dh4, yf1, row(nqf, 1), w_down1, gu1, wg1, wu1, h3, row(npf, 1), "ffn1_bwd", None)
    g_ffn1 = _ffn_grads(dgu1, zf1, act1, dyf1, 1)

    (dymix1, dy1, dqm1), recv = _bwd_out(dh3, ymix1, row(nqm, 1), lru_w_out, "lru_bwd_out",
                                         _exchange_side(GROUPS["ffn1"], g_ffn1))
    s1_ffn1 = add("ffn1", g_ffn1, recv)
    dw_lru_out = _matmul_tn(y1, dymix1, False, 256, "lru_dw_out")
    (dproj1, dcw, dcb, dwa, dba, dwx, dbx, dlam), recv_ffn1 = _lru_bwd(
        proj1, vec[0:4], gates[0], gates[1], vec[7:8], u, r, ig, hl, dy1, _scatter_side(s1_ffn1))
    dw_lru_in = _matmul_tn(z1, dproj1, False, 512, "lru_dw_in")
    dgates = jnp.stack([dwa, dwx]).astype(BF16).reshape(2, LRU_BLOCKS, N_CHIPS, 64, LRU_BLOCK)
    dgates = dgates.transpose(2, 0, 1, 3, 4).reshape(N_CHIPS, -1, LRU_BLOCK)
    g_lru = [_by_chip(dw_lru_in, 512), dgates, dw_lru_out.reshape(N_CHIPS, 256, 1024)]
    (dh2, dpm1), recv = _bwd_in(dproj1, [lru_w_in], False, h2, row(npm, 1), dh3, "lru_bwd_in",
                                _exchange_side(GROUPS["lru"], g_lru))
    s1_lru = add("lru", g_lru, recv)

    (dyf0, dgu0, act0, dh1, dqf0, dpf0), recv_lru = _ffn_backward(
        dh2, yf0, row(nqf, 0), w_down0, gu0, wg0, wu0, h1, row(npf, 0), "ffn0_bwd", _scatter_side(s1_lru))
    g_ffn0 = _ffn_grads(dgu0, zf0, act0, dyf0, 0)

    (dymix0, dy0, dqm0), recv = _bwd_out(dh1, ymix0, row(nqm, 0), ml_w_out, "ml_bwd_out",
                                         _exchange_side(GROUPS["ffn0"], g_ffn0))
    s1_ffn0 = add("ffn0", g_ffn0, recv)
    dw_ml_out = _matmul_tn(y0, dymix0, False, 256, "ml_dw_out")
    (dproj0, dhn, dbif), recv_ffn0 = _mlstm_bwd(qkv, og, bif, hn, hs, cst, nst, mst, dy0, _scatter_side(s1_ffn0))
    dw_ml_in = _matmul_tn(z0, dproj0, False, 640, "ml_dw_in")
    g_ml = [_by_chip(dw_ml_in[:, :ML_IN], ML_IN // N_CHIPS), dw_ml_out.reshape(N_CHIPS, 256, 1024)]
    s1_ml = add("ml", g_ml, _exchange_halves(GROUPS["ml"], g_ml, "exchange_halves_ml"))
    (dx, dpm0), recv_ml = _bwd_in(dproj0, [ml_w_in], False, xs, row(npm, 0), dh1, "ml_bwd_in", _scatter_side(s1_ml))

    pad_lanes = lambda a: jnp.pad(a, ((0, 0), (0, LANES - a.shape[1])))
    small = jnp.concatenate(
        [jnp.concatenate([dpm0, dpm1]), jnp.concatenate([dqm0, dqm1]), jnp.concatenate([dpf0, dpf1]),
         jnp.concatenate([dqf0, dqf1]), pad_lanes(loss_part), dhn, pad_lanes(dbif), dcw, dcb, dba, dbx, dlam,
         jnp.zeros((SMALL_ROWS - 19, LANES), F32)], axis=0)
    small_all = _share_small(small)
    order = ("ml", "lru", "ffn0", "ffn1")
    s2 = (_sum_group("ml", s1_ml, recv_ml, pos) + _sum_group("lru", s1_lru, recv_lru, pos)
          + _sum_group("ffn0", s1_ffn0, recv_ffn0, pos) + _sum_group("ffn1", s1_ffn1, recv_ffn1, pos))
    specs = sum((GROUPS[g] for g in order), ())
    red = dict(zip([n for n, _, _ in specs], _join_halves(specs, s2)))
    vsum = _sum_small(small_all)

    dgates = red["lru_gates"].reshape(2, LRU_BLOCKS, 64, LRU_BLOCK)
    cols = lambda a: lax.dynamic_slice_in_dim(a, chip * 256, 256, axis=1)
    grads = dict(
        ml_w_in=red["ml_w_in"], ml_w_out=red["ml_w_out"], lru_w_in=red["lru_w_in"], lru_w_gate_a=dgates[0],
        lru_w_gate_x=dgates[1], lru_w_out=red["lru_w_out"],
        ffn_w_gate=jnp.swapaxes(jnp.stack([red["ffn_g0"], red["ffn_g1"]]), 1, 2),
        ffn_w_up=jnp.swapaxes(jnp.stack([red["ffn_u0"], red["ffn_u1"]]), 1, 2),
        ffn_w_down=jnp.stack([red["ffn_down0"], red["ffn_down1"]]),
        ml_head_norm=vsum[ROW_HEAD_NORM:ROW_HEAD_NORM + 1], ml_b_if=vsum[ROW_B_IF:ROW_B_IF + 1, :2 * ML_HEADS],
        lru_conv_w=cols(vsum[ROW_LRU:ROW_LRU + 4]))
    for i, n in enumerate(NORMS):
        grads[n] = vsum[2 * i:2 * i + 2]
    for i, n in enumerate(LRU_VECTORS):
        grads[n] = cols(vsum[ROW_LRU + 4 + i:ROW_LRU + 5 + i])
    loss = vsum[ROW_LOSS, 0]
    grads = {n: grads[n].reshape(w[n].shape) for n in WEIGHTS}
    delta, new_m, new_v = _update(w, grads, m, v)
    return (loss, dx[None], *[grads[n] for n in WEIGHTS], *[delta[n] for n in WEIGHTS],
            *[new_m[n] for n in WEIGHTS], *[new_v[n] for n in WEIGHTS])
```

```python
import functools
import math
from typing import Callable, NamedTuple

import jax
import jax.numpy as jnp
from jax import lax
from jax.experimental import pallas as pl
from jax.experimental.pallas import tpu as pltpu

F32 = jnp.float32
BF16 = jnp.bfloat16
MESH = pl.DeviceIdType.MESH

D_MODEL = 1024
D_FF = 2816
ML_HEADS = 8
ML_DK = 64
ML_DV = 128
ML_QK = ML_HEADS * ML_DK
ML_V = ML_HEADS * ML_DV
ML_IN = 2 * ML_QK + 2 * ML_V + 2 * ML_HEADS
ML_IN_PAD = 3200
CHUNK = 64
GATE_CAP = 15.0
ML_M_INIT = -1e30
NEG_BIG = -1e30
LRU_BLOCKS = 4
LRU_BLOCK = 256
CONV_WIDTH = 4
LRU_C = 8.0
EPS = 1e-6
QK_SCALE = ML_DK ** -0.5

ADAM_LR = 0.001
ADAM_B1 = 0.9
ADAM_B2 = 0.999
ADAM_EPS = 1e-08
ADAM_WD = 0.01
ADAM_STEP = 10

N_CHIPS = 4
V7X_VMEM_LIMIT = 56 * 1024 * 1024

NT_DIMS = (((1,), (1,)), ((), ()))
TN_DIMS = (((0,), (0,)), ((), ()))


def _params(*semantics):
    return pltpu.CompilerParams(dimension_semantics=semantics, vmem_limit_bytes=V7X_VMEM_LIMIT)


def _sds(shape, dtype):
    return jax.ShapeDtypeStruct(shape, dtype)


def _full(shape):
    return pl.BlockSpec(shape, lambda *_: (0,) * len(shape))


def _rows(tm, n):
    return pl.BlockSpec((tm, n), lambda i: (i, 0))


def _sigmoid(x):
    return 1.0 / (1.0 + jnp.exp(-x))


def _log_sigmoid(x):
    return jnp.minimum(x, 0.0) - jnp.log1p(jnp.exp(-jnp.abs(x)))


def _rms(x):
    r = lax.rsqrt(jnp.mean(x * x, axis=-1, keepdims=True) + EPS)
    return x * r, r


def _rms_bwd(xhat, r, g, dy):
    dxh = dy * g
    return r * (dxh - xhat * jnp.mean(dxh * xhat, axis=-1, keepdims=True))


SUBLANES = 8


def _scan_rows(a, b, carry, reverse=False):
    n, w = a.shape
    groups = n // SUBLANES
    a3, b3 = a.reshape(groups, SUBLANES, w), b.reshape(groups, SUBLANES, w)
    sub = lax.broadcasted_iota(jnp.int32, a3.shape, 1)
    s = 1
    while s < SUBLANES:
        keep = sub < SUBLANES - s if reverse else sub >= s
        shift = SUBLANES - s if reverse else s
        b3 = b3 + a3 * jnp.where(keep, pltpu.roll(b3, shift, 1), 0.0)
        a3 = a3 * jnp.where(keep, pltpu.roll(a3, shift, 1), 1.0)
        s *= 2
    out = [None] * groups
    for g in (reversed(range(groups)) if reverse else range(groups)):
        out[g] = b3[g] + a3[g] * carry
        carry = out[g][0:1] if reverse else out[g][SUBLANES - 1:SUBLANES]
    return jnp.concatenate(out, axis=0)


def _cumsum_rows(x, reverse=False):
    n = x.shape[0]
    row = lax.broadcasted_iota(jnp.int32, x.shape, 0)
    s = 1
    while s < n:
        if reverse:
            x = x + jnp.where(row < n - s, pltpu.roll(x, n - s, 0), 0.0)
        else:
            x = x + jnp.where(row >= s, pltpu.roll(x, s, 0), 0.0)
        s *= 2
    return x


def _norm_matmul(h, g, w, n_bf16, name, side):
    t, d = h.shape
    n = w.shape[1]
    tm = min(512, t)

    def body(h_ref, g_ref, w_ref, z_ref, *o_refs):
        xhat, _ = _rms(h_ref[...])
        z = (xhat * g_ref[...]).astype(BF16)
        z_ref[...] = z
        out = jnp.dot(z, w_ref[...], preferred_element_type=F32)
        if n_bf16:
            o_refs[0][...] = out[:, :n_bf16].astype(BF16)
            o_refs[1][...] = out[:, n_bf16:]
        else:
            o_refs[0][...] = out

    if n_bf16:
        out_specs = [_rows(tm, d), _rows(tm, n_bf16), _rows(tm, n - n_bf16)]
        out_shape = [_sds((t, d), BF16), _sds((t, n_bf16), BF16), _sds((t, n - n_bf16), F32)]
    else:
        out_specs = [_rows(tm, d), _rows(tm, n)]
        out_shape = [_sds((t, d), BF16), _sds((t, n), F32)]
    return _side_call(body, side, name=name, steps=t // tm, in_specs=[_rows(tm, d), _full((1, d)), _full((d, n))],
                      out_specs=out_specs, out_shape=out_shape, scratch_shapes=[], args=(h, g, w))


def _matmul_postnorm(a, w, g, res, name, side):
    t = res.shape[0]
    k, d = w.shape
    tm = 256

    def body(a_ref, w_ref, g_ref, res_ref, y_ref, o_ref):
        y = jnp.dot(a_ref[...], w_ref[...], preferred_element_type=F32)
        y_ref[...] = y
        yhat, _ = _rms(y)
        o_ref[...] = res_ref[...] + yhat * g_ref[...]

    return _side_call(
        body, side, name=name, steps=t // tm,
        in_specs=[_rows(tm, k), _full((k, d)), _full((1, d)), _rows(tm, d)],
        out_specs=[_rows(tm, d), _rows(tm, d)],
        out_shape=[_sds((t, d), F32), _sds((t, d), F32)],
        scratch_shapes=[], args=(a, w, g, res))


class _Side(NamedTuple):
    inputs: list
    out_shape: list
    n_sems: int
    start: Callable
    finish: Callable


def _side_call(body, side, *, name, steps, in_specs, out_specs, out_shape, scratch_shapes, args):
    n_in, n_out, n_scr = len(in_specs), len(out_specs), len(scratch_shapes)
    if side is None:
        outs = pl.pallas_call(
            body, name=name, grid=(steps,), in_specs=in_specs, out_specs=out_specs, out_shape=out_shape,
            scratch_shapes=scratch_shapes, compiler_params=_params("arbitrary"))(*args)
        return list(outs), []
    s_in, s_out = len(side.inputs), len(side.out_shape)

    def carrying(*refs):
        ins, side_ins = refs[:n_in], refs[n_in:n_in + s_in]
        outs = refs[n_in + s_in:n_in + s_in + n_out]
        side_outs = refs[n_in + s_in + n_out:n_in + s_in + n_out + s_out]
        scratch = refs[n_in + s_in + n_out + s_out:]
        own, sems = scratch[:n_scr], scratch[n_scr:]

        @pl.when(pl.program_id(0) == 0)
        def _():
            side.start(side_ins, side_outs, *sems)

        body(*ins, *outs, *own)

        @pl.when(pl.program_id(0) == steps - 1)
        def _():
            side.finish(side_ins, side_outs, *sems)

    outs = pl.pallas_call(
        carrying, name=name, grid=(steps,), in_specs=list(in_specs) + [HBM_SPEC] * s_in,
        out_specs=list(out_specs) + [HBM_SPEC] * s_out, out_shape=list(out_shape) + list(side.out_shape),
        scratch_shapes=list(scratch_shapes) + [pltpu.SemaphoreType.DMA((side.n_sems,))] * 2,
        compiler_params=_params("arbitrary"))(*args, *side.inputs)
    return list(outs[:n_out]), list(outs[n_out:])


def _ffn_forward(h, g_pre, wg_t, wu_t, w_down, g_post, target, name, side):
    t, d = h.shape
    k = w_down.shape[0]
    tm = 256

    def body(*refs):
        if target is None:
            h_ref, gpre_ref, wg_ref, wu_ref, wd_ref, gpost_ref, z_ref, gu_ref, y_ref, o_ref = refs
        else:
            h_ref, gpre_ref, wg_ref, wu_ref, wd_ref, gpost_ref, t_ref, z_ref, gu_ref, y_ref, o_ref, l_ref = refs

            @pl.when(pl.program_id(0) == 0)
            def _():
                l_ref[...] = jnp.zeros_like(l_ref)

        hv = h_ref[...]
        xhat, _ = _rms(hv)
        z = (xhat * gpre_ref[...]).astype(BF16)
        z_ref[...] = z
        gate = lax.dot_general(z, wg_ref[...], NT_DIMS, preferred_element_type=F32).astype(BF16)
        up = lax.dot_general(z, wu_ref[...], NT_DIMS, preferred_element_type=F32).astype(BF16)
        gu_ref[:, :k] = gate
        gu_ref[:, k:] = up
        gate = gate.astype(F32)
        act = (gate * _sigmoid(gate) * up.astype(F32)).astype(BF16)
        y = jnp.dot(act, wd_ref[...], preferred_element_type=F32)
        y_ref[...] = y
        yhat, _ = _rms(y)
        out = hv + yhat * gpost_ref[...]
        if target is None:
            o_ref[...] = out
        else:
            err = out - t_ref[...]
            o_ref[...] = err * (1.0 / d)
            part = jnp.sum(jnp.sum(err * err, axis=1, keepdims=True), axis=0, keepdims=True) * (0.5 / d)
            l_ref[...] += jnp.broadcast_to(part, l_ref.shape)

    in_specs = [_rows(tm, d), _full((1, d)), _full((k, d)), _full((k, d)), _full((k, d)), _full((1, d))]
    out_specs = [_rows(tm, d), _rows(tm, 2 * k), _rows(tm, d), _rows(tm, d)]
    out_shape = [_sds((t, d), BF16), _sds((t, 2 * k), BF16), _sds((t, d), F32), _sds((t, d), F32)]
    args = (h, g_pre, wg_t, wu_t, w_down, g_post)
    if target is not None:
        in_specs, args = in_specs + [_rows(tm, d)], args + (target,)
        out_specs, out_shape = out_specs + [_full((1, 128))], out_shape + [_sds((1, 128), F32)]
    return _side_call(body, side, name=name, steps=t // tm, in_specs=in_specs, out_specs=out_specs,
                      out_shape=out_shape, scratch_shapes=[], args=args)


def _bwd_out(dout, y, g, w, name, side):
    t, d = dout.shape
    k = w.shape[0]
    tm = 256

    def body(dout_ref, y_ref, g_ref, w_ref, dy_ref, da_ref, dg_ref):
        @pl.when(pl.program_id(0) == 0)
        def _():
            dg_ref[...] = jnp.zeros_like(dg_ref)

        do = dout_ref[...]
        yhat, r = _rms(y_ref[...])
        dg_ref[...] += jnp.sum(do * yhat, axis=0, keepdims=True)
        dy = _rms_bwd(yhat, r, g_ref[...], do).astype(BF16)
        dy_ref[...] = dy
        da_ref[...] = lax.dot_general(dy, w_ref[...], NT_DIMS, preferred_element_type=F32)

    return _side_call(
        body, side, name=name, steps=t // tm,
        in_specs=[_rows(tm, d), _rows(tm, d), _full((1, d)), _full((k, d))],
        out_specs=[_rows(tm, d), _rows(tm, k), _full((1, d))],
        out_shape=[_sds((t, d), BF16), _sds((t, k), F32), _sds((1, d), F32)],
        scratch_shapes=[], args=(dout, y, g, w))


def _ffn_backward(dout, y, g_post, w_down, gu, wg_t, wu_t, h, g_pre, name, side):
    t, d = dout.shape
    k = w_down.shape[0]
    tm = 256

    def body(dout_ref, y_ref, gpost_ref, wd_ref, gu_ref, wg_ref, wu_ref, h_ref, gpre_ref,
             dy_ref, dgu_ref, act_ref, dh_ref, dgpost_ref, dgpre_ref):
        @pl.when(pl.program_id(0) == 0)
        def _():
            dgpost_ref[...] = jnp.zeros_like(dgpost_ref)
            dgpre_ref[...] = jnp.zeros_like(dgpre_ref)

        do = dout_ref[...]
        yhat, r = _rms(y_ref[...])
        dgpost_ref[...] += jnp.sum(do * yhat, axis=0, keepdims=True)
        dy = _rms_bwd(yhat, r, gpost_ref[...], do).astype(BF16)
        dy_ref[...] = dy
        da = lax.dot_general(dy, wd_ref[...], NT_DIMS, preferred_element_type=F32)
        gate = gu_ref[:, :k].astype(F32)
        up = gu_ref[:, k:].astype(F32)
        sg = _sigmoid(gate)
        silu = gate * sg
        act_ref[...] = (silu * up).astype(BF16)
        dgate = (da * up * (sg * (1.0 + gate * (1.0 - sg)))).astype(BF16)
        dup = (da * silu).astype(BF16)
        dgu_ref[:, :k] = dgate
        dgu_ref[:, k:] = dup
        dz = (jnp.dot(dgate, wg_ref[...], preferred_element_type=F32)
              + jnp.dot(dup, wu_ref[...], preferred_element_type=F32))
        hhat, r2 = _rms(h_ref[...])
        dgpre_ref[...] += jnp.sum(dz * hhat, axis=0, keepdims=True)
        dh_ref[...] = do + _rms_bwd(hhat, r2, gpre_ref[...], dz)

    vec = _full((1, d))
    wspec = _full((k, d))
    return _side_call(
        body, side, name=name, steps=t // tm,
        in_specs=[_rows(tm, d), _rows(tm, d), vec, wspec, _rows(tm, 2 * k), wspec, wspec, _rows(tm, d), vec],
        out_specs=[_rows(tm, d), _rows(tm, 2 * k), _rows(tm, k), _rows(tm, d), vec, vec],
        out_shape=[_sds((t, d), BF16), _sds((t, 2 * k), BF16), _sds((t, k), BF16), _sds((t, d), F32),
                   _sds((1, d), F32), _sds((1, d), F32)],
        scratch_shapes=[], args=(dout, y, g_post, w_down, gu, wg_t, wu_t, h, g_pre))


def _bwd_in(dp, ws, transposed, h, g, dout, name, side):
    t, d = h.shape
    n = dp.shape[1]
    tm = 256
    widths = [w.shape[0] if transposed else w.shape[1] for w in ws]

    def body(dp_ref, *refs):
        w_refs = refs[:len(ws)]
        h_ref, g_ref, dout_ref, dh_ref, dg_ref = refs[len(ws):]

        @pl.when(pl.program_id(0) == 0)
        def _():
            dg_ref[...] = jnp.zeros_like(dg_ref)

        dz, at = None, 0
        for w_ref, width in zip(w_refs, widths):
            block = dp_ref[:, at:at + width]
            if transposed:
                part = jnp.dot(block, w_ref[...], preferred_element_type=F32)
            else:
                part = lax.dot_general(block, w_ref[...], NT_DIMS, preferred_element_type=F32)
            dz = part if dz is None else dz + part
            at += width
        hhat, r = _rms(h_ref[...])
        dg_ref[...] += jnp.sum(dz * hhat, axis=0, keepdims=True)
        dh_ref[...] = dout_ref[...] + _rms_bwd(hhat, r, g_ref[...], dz)

    return _side_call(
        body, side, name=name, steps=t // tm,
        in_specs=[_rows(tm, n)] + [_full(w.shape) for w in ws] + [_rows(tm, d), _full((1, d)), _rows(tm, d)],
        out_specs=[_rows(tm, d), _full((1, d))],
        out_shape=[_sds((t, d), F32), _sds((1, d), F32)],
        scratch_shapes=[], args=(dp, *ws, h, g, dout))


def _matmul_tn(a, b, tile_a, tile, name, window=None):
    t, ka = a.shape
    nb = b.shape[1]
    col0 = 0
    if window is not None:
        col0, ka = window

    def body(a_ref, b_ref, o_ref):
        o_ref[...] = lax.dot_general(a_ref[...], b_ref[...], TN_DIMS, preferred_element_type=F32).astype(BF16)

    if tile_a:
        grid = (ka // tile,)
        in_specs = [pl.BlockSpec((t, tile), lambda i: (0, col0 // tile + i)), _full((t, nb))]
        out_specs = pl.BlockSpec((tile, nb), lambda i: (i, 0))
    else:
        grid = (nb // tile,)
        in_specs = [_full((t, ka)), pl.BlockSpec((t, tile), lambda i: (0, i))]
        out_specs = pl.BlockSpec((ka, tile), lambda i: (0, i))
    return pl.pallas_call(
        body, name=name, grid=grid, in_specs=in_specs, out_specs=out_specs,
        out_shape=_sds((ka, nb), BF16), compiler_params=_params("parallel"),
    )(a, b)


def _ml_gate_prep(gt, bif):
    th = jnp.tanh((gt + bif) / GATE_CAP)
    act = GATE_CAP * th
    cum = _cumsum_rows(_log_sigmoid(act))
    lane = lax.broadcasted_iota(jnp.int32, gt.shape, 1)
    x = jnp.where(lane < ML_HEADS, act, cum)
    return x, x.T, th, act


HEADS = range(ML_HEADS)


def _each(fn, *per_head):
    return [fn(*a) for a in zip(*per_head)]


def _ml_chunk_fwd(q, k, v, kt, x, xt, c_in, n_in, m_in):
    causal = (lax.broadcasted_iota(jnp.int32, (CHUNK, CHUNK), 0)
              >= lax.broadcasted_iota(jnp.int32, (CHUNK, CHUNK), 1))
    f = {}
    qh = f["qh"] = [q[:, ML_DK * h:ML_DK * (h + 1)] for h in HEADS]
    kh = f["kh"] = [k[:, ML_DK * h:ML_DK * (h + 1)] for h in HEADS]
    f["vh"] = [v[:, ML_DV * h:ML_DV * (h + 1)] for h in HEADS]
    kth = [kt[ML_DK * h:ML_DK * (h + 1), :] for h in HEADS]
    s = _each(lambda a, b: jnp.dot(a, b, preferred_element_type=F32) * QK_SCALE, qh, kth)
    f["qc"] = _each(lambda a, c: jnp.dot(a, c.astype(BF16), preferred_element_type=F32) * QK_SCALE, qh, c_in)
    bcol = f["bcol"] = [x[:, ML_HEADS + h:ML_HEADS + h + 1] for h in HEADS]
    licol = f["licol"] = [x[:, h:h + 1] for h in HEADS]
    brow = f["brow"] = [xt[ML_HEADS + h:ML_HEADS + h + 1, :] for h in HEADS]
    lirow = [xt[h:h + 1, :] for h in HEADS]
    dmat = _each(lambda bc, br, lr: jnp.where(causal, bc - br + lr, NEG_BIG), bcol, brow, lirow)
    inter = _each(lambda bc, m: bc + m, bcol, m_in)
    mt = _each(lambda d, i: jnp.maximum(jnp.max(d, axis=1, keepdims=True), i), dmat, inter)
    wt = f["wt"] = _each(lambda d, m: jnp.exp(d - m), dmat, mt)
    p = f["p"] = _each(lambda a, b: a * b, wt, s)
    winter = f["winter"] = _each(lambda i, m: jnp.exp(i - m), inter, mt)
    qf = f["qf"] = [a.astype(F32) for a in qh]
    qn = f["qn"] = _each(lambda a, n: jnp.sum(a * n, axis=1, keepdims=True) * QK_SCALE, qf, n_in)
    den = f["den"] = _each(lambda a, w, b: jnp.sum(a, axis=1, keepdims=True) + w * b, p, winter, qn)
    emt = f["emt"] = [jnp.exp(-m) for m in mt]
    f["nrm"] = _each(lambda d, e: jnp.maximum(jnp.abs(d), e), den, emt)
    gtot = [bc[CHUNK - 1:CHUNK, :] for bc in bcol]
    a_col = _each(lambda g, bc, lc: g - bc + lc, gtot, bcol, licol)
    a_row = _each(lambda g, br, lr: g - br + lr, gtot, brow, lirow)
    m_new = f["m_new"] = _each(lambda g, m, a: jnp.maximum(g + m, jnp.max(a, axis=1, keepdims=True)),
                               gtot, m_in, a_row)
    f["decay"] = _each(lambda g, m, mn: jnp.exp(g + m - mn), gtot, m_in, m_new)
    wkf = f["wkf"] = _each(lambda a, mn: jnp.exp(a - mn), a_col, m_new)
    f["kw"] = _each(lambda a, w: a.astype(F32) * w, kh, wkf)
    f["ktw"] = _each(lambda a, ar, mn: (a.astype(F32) * jnp.exp(ar - mn)).astype(BF16), kth, a_row, m_new)
    return f


ML_STEP_CHUNKS = 4
ML_STEP = CHUNK * ML_STEP_CHUNKS


def _ml_specs(steps, rev):
    def at(col):
        if rev:
            return lambda i: (steps - 1 - i, col)
        return lambda i: (i, col)

    return [pl.BlockSpec((ML_STEP, ML_QK), at(0)), pl.BlockSpec((ML_STEP, ML_QK), at(1)),
            pl.BlockSpec((ML_STEP, ML_V), at(1)), pl.BlockSpec((ML_STEP, ML_V), at(0)),
            pl.BlockSpec((ML_STEP, 128), at(ML_V // 128))]


def _mlstm_fwd(qkv, og, bif, hn, side):
    t = qkv.shape[0]
    nc = t // CHUNK
    steps = t // ML_STEP

    def body(q_ref, k_ref, v_ref, o_ref, gt_ref, bif_ref, hn_ref, y_ref, hs_ref, cst_ref, nst_ref, mst_ref,
             c_sc, n_sc, m_sc):
        @pl.when(pl.program_id(0) == 0)
        def _():
            c_sc[...] = jnp.zeros_like(c_sc)
            n_sc[...] = jnp.zeros_like(n_sc)
            m_sc[...] = jnp.full_like(m_sc, ML_M_INIT)

        c_all, n_all, m_all = c_sc[...], n_sc[...], m_sc[...]
        c_in = [c_all[ML_DK * h:ML_DK * (h + 1), :] for h in HEADS]
        n_in = [n_all[h:h + 1, :] for h in HEADS]
        m_in = [m_all[h:h + 1, 0:1] for h in HEADS]
        for sub in range(ML_STEP_CHUNKS):
            rs = slice(CHUNK * sub, CHUNK * (sub + 1))
            for h in HEADS:
                cst_ref[sub, ML_DK * h:ML_DK * (h + 1), :] = c_in[h]
                nst_ref[sub, h:h + 1, :] = n_in[h]
                mst_ref[sub, h:h + 1, :] = jnp.broadcast_to(m_in[h], (1, 128))
            x, xt, _, _ = _ml_gate_prep(gt_ref[rs, :], bif_ref[...])
            q, k, v = q_ref[rs, :], k_ref[rs, :], v_ref[rs, :]
            f = _ml_chunk_fwd(q, k, v, k.T, x, xt, c_in, n_in, m_in)
            num = _each(lambda p, vh, w, qc: jnp.dot(p.astype(BF16), vh, preferred_element_type=F32) + w * qc,
                        f["p"], f["vh"], f["winter"], f["qc"])
            hh = _each(lambda a, b: a / b, num, f["nrm"])
            hhat = [_rms(a)[0] for a in hh]
            c_in = _each(lambda d, c, kw, vh: d * c + jnp.dot(kw, vh, preferred_element_type=F32),
                         f["decay"], c_in, f["ktw"], f["vh"])
            n_in = _each(lambda d, n, kw: d * n + jnp.sum(kw, axis=0, keepdims=True), f["decay"], n_in, f["kw"])
            m_in = f["m_new"]
            for h in HEADS:
                vs = slice(ML_DV * h, ML_DV * (h + 1))
                hs_ref[rs, vs] = hh[h]
                y_ref[rs, vs] = (hhat[h] * hn_ref[:, vs] * _sigmoid(o_ref[rs, vs])).astype(BF16)
        for h in HEADS:
            c_sc[ML_DK * h:ML_DK * (h + 1), :] = c_in[h]
            n_sc[h:h + 1, :] = n_in[h]
            m_sc[h:h + 1, :] = jnp.broadcast_to(m_in[h], (1, 128))

    return _side_call(
        body, side, name="mlstm_fwd", steps=steps,
        in_specs=_ml_specs(steps, False) + [_full((1, 128)), _full((1, ML_V))],
        out_specs=[_rows(ML_STEP, ML_V), _rows(ML_STEP, ML_V),
                   pl.BlockSpec((ML_STEP_CHUNKS, ML_HEADS * ML_DK, ML_DV), lambda i: (i, 0, 0)),
                   pl.BlockSpec((ML_STEP_CHUNKS, ML_HEADS, ML_DK), lambda i: (i, 0, 0)),
                   pl.BlockSpec((ML_STEP_CHUNKS, ML_HEADS, 128), lambda i: (i, 0, 0))],
        out_shape=[_sds((t, ML_V), BF16), _sds((t, ML_V), F32),
                   _sds((nc, ML_HEADS * ML_DK, ML_DV), F32), _sds((nc, ML_HEADS, ML_DK), F32),
                   _sds((nc, ML_HEADS, 128), F32)],
        scratch_shapes=[pltpu.VMEM((ML_HEADS * ML_DK, ML_DV), F32), pltpu.VMEM((ML_HEADS, ML_DK), F32),
                        pltpu.VMEM((ML_HEADS, 128), F32)],
        args=(qkv, qkv, qkv, og, og, bif, hn))


def _mlstm_bwd(qkv, og, bif, hn, hs, cst, nst, mst, dy, side):
    t = qkv.shape[0]
    nc = t // CHUNK
    steps = t // ML_STEP

    def chunk(q_ref, k_ref, v_ref, o_ref, gt_ref, bif_ref, hn_ref, hs_ref, cst_ref, nst_ref, mst_ref, dy_ref,
              dp_ref, dhn_ref, dbif_ref, carried):
        x, xt, th, act = _ml_gate_prep(gt_ref[...], bif_ref[...])
        q, k, v = q_ref[...], k_ref[...], v_ref[...]
        qt, vt = q.T, v.T
        rows = lax.broadcasted_iota(jnp.int32, (CHUNK, CHUNK), 0)
        cols = lax.broadcasted_iota(jnp.int32, (CHUNK, CHUNK), 1)
        lane = lax.broadcasted_iota(jnp.int32, (CHUNK, 128), 1)
        row = lax.broadcasted_iota(jnp.int32, (CHUNK, 1), 0)
        as_row = lambda col: jnp.sum(jnp.where(rows == cols, col, 0.0), axis=0, keepdims=True)
        mm = lambda a, b: jnp.dot(a, b, preferred_element_type=F32)
        bf = lambda a: a.astype(BF16)
        ksl = [slice(ML_DK * h, ML_DK * (h + 1)) for h in HEADS]
        vsl = [slice(ML_DV * h, ML_DV * (h + 1)) for h in HEADS]
        c_in = [cst_ref[0, s, :] for s in ksl]
        n_in = [nst_ref[0, h:h + 1, :] for h in HEADS]
        m_in = [mst_ref[0, h:h + 1, 0:1] for h in HEADS]
        dcn, dcn_t, dnn = carried
        f = _ml_chunk_fwd(q, k, v, k.T, x, xt, c_in, n_in, m_in)
        qh, kh, vh, p, winter, decay = f["qh"], f["kh"], f["vh"], f["p"], f["winter"], f["decay"]
        qth = [qt[s, :] for s in ksl]
        vth = [vt[s, :] for s in vsl]
        c_t = [bf(c.T) for c in c_in]
        dmat_t = _each(lambda br, bc, lc: jnp.where(rows <= cols, br - bc + lc, NEG_BIG),
                       f["brow"], f["bcol"], f["licol"])
        inter_row = _each(lambda br, m: br + m, f["brow"], m_in)
        mt_row = _each(lambda d, i: jnp.maximum(jnp.max(d, axis=0, keepdims=True), i), dmat_t, inter_row)
        wt_t = _each(lambda d, m: jnp.exp(d - m), dmat_t, mt_row)
        p_t = _each(lambda w, a, b: w * (mm(a, b) * QK_SCALE), wt_t, kh, qth)
        winter_row = _each(lambda i, m: jnp.exp(i - m), inter_row, mt_row)
        hh = [hs_ref[:, s] for s in vsl]
        hn_h = [hn_ref[:, s] for s in vsl]
        sg = [_sigmoid(o_ref[:, s]) for s in vsl]
        dyh = [dy_ref[:, s] for s in vsl]
        norm = [_rms(a) for a in hh]
        hhat, r = [a for a, _ in norm], [b for _, b in norm]
        dyn = _each(lambda a, b: a * b, dyh, sg)
        do = _each(lambda d, hx, g, s: d * hx * g * s * (1.0 - s), dyh, hhat, hn_h, sg)
        dhn = _each(lambda a, b: jnp.sum(a * b, axis=0, keepdims=True), dyn, hhat)
        dh = _each(_rms_bwd, hhat, r, hn_h, dyn)
        inv = [1.0 / a for a in f["nrm"]]
        dnum = _each(lambda a, b: a * b, dh, inv)
        dnrm = _each(lambda a, b, c: -jnp.sum(a * b, axis=1, keepdims=True) * c, dh, hh, inv)
        dden = _each(lambda d, e, g: jnp.where(jnp.abs(d) > e, g * jnp.sign(d), 0.0), f["den"], f["emt"], dnrm)
        dnb = [bf(a) for a in dnum]
        dnt = [bf(a.T) for a in dnum]
        rmat = _each(lambda a, b, d: mm(a, b) + d, dnb, vth, dden)
        rmat_t = _each(lambda a, b, d: mm(a, b) + as_row(d), vh, dnt, dden)
        ds = _each(lambda w, a: bf(w * a), f["wt"], rmat)
        ds_t = _each(lambda w, a: bf(w * a), wt_t, rmat_t)
        dv = _each(lambda a, b: mm(bf(a), b), p_t, dnb)
        dqs = _each(lambda s, kk, w, d, ct, dd, n: mm(s, kk) + w * (mm(d, ct) + dd * n),
                    ds, kh, winter, dnb, c_t, dden, n_in)
        dk = _each(lambda s, a: mm(s, a) * QK_SCALE, ds_t, qh)
        dinter = _each(lambda qc, dn_, qn, dd, w: (jnp.sum(qc * dn_, axis=1, keepdims=True) + qn * dd) * w,
                       f["qc"], dnum, f["qn"], dden, winter)
        wq = _each(lambda w, a: w * a * QK_SCALE, winter, f["qf"])
        wq_t = _each(lambda w, a: bf(w * a.astype(F32) * QK_SCALE), winter_row, qth)
        dc_loc = _each(mm, wq_t, dnb)
        dct_loc = _each(lambda a, b: mm(a, bf(b)), dnt, wq)
        dn_loc = _each(lambda a, d: jnp.sum(a * d, axis=0, keepdims=True), wq, dden)
        cs_q = _each(lambda a, b: jnp.sum(a * b, axis=1, keepdims=True), p_t, rmat_t)
        db = _each(lambda a, b, di, cs: jnp.sum(a * b, axis=1, keepdims=True) + di - cs, p, rmat, dinter, cs_q)
        ddecay = _each(lambda dc_, c, dn_, n: jnp.sum(jnp.sum(dc_ * c, axis=1, keepdims=True), axis=0, keepdims=True)
                       + jnp.sum(dn_ * n, axis=1, keepdims=True), dcn, c_in, dnn, n_in)
        dkw = _each(lambda a, b, n: mm(a, bf(b)) + n, vh, dcn_t, dnn)
        dk = _each(lambda a, w, b: a + w * b, dk, f["wkf"], dkw)
        da = _each(lambda a, kk, w: jnp.sum(a * kk.astype(F32), axis=1, keepdims=True) * w, dkw, kh, f["wkf"])
        dv = _each(lambda a, kw, dc_: a + mm(bf(kw), bf(dc_)), dv, f["kw"], dcn)
        dgtot = _each(lambda a, dd, d: jnp.sum(a, axis=0, keepdims=True) + dd * d, da, ddecay, decay)
        db = _each(lambda a, b, g: a - b + jnp.where(row == CHUNK - 1, g, 0.0), db, da, dgtot)
        dli = _each(lambda a, b: a + b, cs_q, da)
        dc_new = _each(lambda d, a, b: d * a + b, decay, dcn, dc_loc)
        dct_new = _each(lambda d, a, b: d * a + b, decay, dcn_t, dct_loc)
        dn_new = _each(lambda d, a, b: d * a + b, decay, dnn, dn_loc)
        dx = jnp.zeros((CHUNK, 128), F32)
        for h in HEADS:
            dhn_ref[:, vsl[h]] += dhn[h]
            dp_ref[:, ksl[h]] = bf(dqs[h] * QK_SCALE)
            dp_ref[:, ML_QK + ML_DK * h:ML_QK + ML_DK * (h + 1)] = bf(dk[h])
            dp_ref[:, 2 * ML_QK + ML_DV * h:2 * ML_QK + ML_DV * (h + 1)] = bf(dv[h])
            dp_ref[:, 2 * ML_QK + ML_V + ML_DV * h:2 * ML_QK + ML_V + ML_DV * (h + 1)] = bf(do[h])
            dx = jnp.where(lane == h, dli[h], dx)
            dx = jnp.where(lane == ML_HEADS + h, db[h], dx)
        dlf = _cumsum_rows(dx, reverse=True)
        dact = jnp.where(lane < ML_HEADS, dx, dlf * _sigmoid(-act))
        dz = dact * (1.0 - th * th)
        dp_ref[:, 2 * ML_QK + 2 * ML_V:] = dz.astype(BF16)
        dbif_ref[...] += jnp.sum(dz, axis=0, keepdims=True)
        return dc_new, dct_new, dn_new

    def body(q_ref, k_ref, v_ref, o_ref, gt_ref, bif_ref, hn_ref, hs_ref, cst_ref, nst_ref, mst_ref, dy_ref,
             dp_ref, dhn_ref, dbif_ref, dc_sc, dct_sc, dn_sc):
        @pl.when(pl.program_id(0) == 0)
        def _():
            dc_sc[...] = jnp.zeros_like(dc_sc)
            dct_sc[...] = jnp.zeros_like(dct_sc)
            dn_sc[...] = jnp.zeros_like(dn_sc)
            dhn_ref[...] = jnp.zeros_like(dhn_ref)
            dbif_ref[...] = jnp.zeros_like(dbif_ref)

        dc_all, dct_all, dn_all = dc_sc[...], dct_sc[...], dn_sc[...]
        carried = ([dc_all[ML_DK * h:ML_DK * (h + 1), :] for h in HEADS],
                   [dct_all[ML_DV * h:ML_DV * (h + 1), :] for h in HEADS],
                   [dn_all[h:h + 1, :] for h in HEADS])
        for sub in reversed(range(ML_STEP_CHUNKS)):
            rs = pl.ds(CHUNK * sub, CHUNK)
            one = pl.ds(sub, 1)
            carried = chunk(q_ref.at[rs], k_ref.at[rs], v_ref.at[rs], o_ref.at[rs], gt_ref.at[rs], bif_ref, hn_ref,
                            hs_ref.at[rs], cst_ref.at[one], nst_ref.at[one], mst_ref.at[one], dy_ref.at[rs],
                            dp_ref.at[rs], dhn_ref, dbif_ref, carried)
        for h in HEADS:
            dc_sc[ML_DK * h:ML_DK * (h + 1), :] = carried[0][h]
            dct_sc[ML_DV * h:ML_DV * (h + 1), :] = carried[1][h]
            dn_sc[h:h + 1, :] = carried[2][h]

    rev = lambda i: (steps - 1 - i, 0)
    rev3 = lambda i: (steps - 1 - i, 0, 0)
    return _side_call(
        body, side, name="mlstm_bwd", steps=steps,
        in_specs=_ml_specs(steps, True) + [
            _full((1, 128)), _full((1, ML_V)), pl.BlockSpec((ML_STEP, ML_V), rev),
            pl.BlockSpec((ML_STEP_CHUNKS, ML_HEADS * ML_DK, ML_DV), rev3),
            pl.BlockSpec((ML_STEP_CHUNKS, ML_HEADS, ML_DK), rev3),
            pl.BlockSpec((ML_STEP_CHUNKS, ML_HEADS, 128), rev3), pl.BlockSpec((ML_STEP, ML_V), rev)],
        out_specs=[pl.BlockSpec((ML_STEP, ML_IN_PAD), rev), _full((1, ML_V)), _full((1, 128))],
        out_shape=[_sds((t, ML_IN_PAD), BF16), _sds((1, ML_V), F32), _sds((1, 128), F32)],
        scratch_shapes=[pltpu.VMEM((ML_HEADS * ML_DK, ML_DV), F32), pltpu.VMEM((ML_HEADS * ML_DV, ML_DK), F32),
                        pltpu.VMEM((ML_HEADS, ML_DK), F32)],
        args=(qkv, qkv, qkv, og, og, bif, hn, hs, cst, nst, mst, dy))


LRU_TM = 256
GELU_K = math.sqrt(2.0 / math.pi)
GELU_C = 0.044715


def _gelu(x):
    th = jnp.tanh(GELU_K * (x + GELU_C * x * x * x))
    return 0.5 * x * (1.0 + th), th


def _neg_expm1(x):
    series = -x * (1.0 + x * (0.5 + x * (1.0 / 6.0 + x * (1.0 / 24.0))))
    return jnp.where(x > -0.05, series, 1.0 - jnp.exp(x))


def _block_diag_dot(a, w_ref, dims):
    parts = [lax.dot_general(a[:, LRU_BLOCK * n:LRU_BLOCK * (n + 1)], w_ref[n], dims, preferred_element_type=F32)
             for n in range(LRU_BLOCKS)]
    return jnp.concatenate(parts, axis=1)


def _lru_gates(u, r, ig, lam):
    ls = _log_sigmoid(lam)
    la = LRU_C * r * ls
    a = jnp.exp(la)
    em = _neg_expm1(2.0 * la)
    mult = jnp.sqrt(em)
    return ls, a, em, mult


def _lru_fwd(proj, cw, cb, wa, ba, wx, bx, lam, side):
    t = proj.shape[0]
    w = D_MODEL
    tm = min(LRU_TM, t)

    def body(gb_ref, up_ref, cw_ref, cb_ref, wa_ref, ba_ref, wx_ref, bx_ref, lam_ref,
             y_ref, u_ref, r_ref, i_ref, h_ref, tail_sc, hprev_sc):
        @pl.when(pl.program_id(0) == 0)
        def _():
            tail_sc[...] = jnp.zeros_like(tail_sc)
            hprev_sc[...] = jnp.zeros_like(hprev_sc)

        up = up_ref[...]
        ext = jnp.concatenate([tail_sc[...], up], axis=0)
        u = cb_ref[...] + cw_ref[CONV_WIDTH - 1:CONV_WIDTH, :] * up
        for s in range(1, CONV_WIDTH):
            u = u + cw_ref[CONV_WIDTH - 1 - s:CONV_WIDTH - s, :] * pltpu.roll(ext, s, 0)[8:8 + tm]
        tail_sc[...] = up[tm - 8:tm]
        ub = u.astype(BF16)
        r = _sigmoid(_block_diag_dot(ub, wa_ref, (((1,), (0,)), ((), ()))) + ba_ref[...])
        ig = _sigmoid(_block_diag_dot(ub, wx_ref, (((1,), (0,)), ((), ()))) + bx_ref[...])
        _, a, _, mult = _lru_gates(u, r, ig, lam_ref[...])
        h = _scan_rows(a, mult * ig * u, hprev_sc[0:1, :])
        hprev_sc[0:1, :] = h[tm - 1:tm]
        u_ref[...] = u
        r_ref[...] = r
        i_ref[...] = ig
        h_ref[...] = h
        gel, _ = _gelu(gb_ref[...])
        y_ref[...] = (h * gel).astype(BF16)

    vec = _full((1, w))
    wspec = _full((LRU_BLOCKS, LRU_BLOCK, LRU_BLOCK))
    return _side_call(
        body, side, name="lru_fwd", steps=t // tm,
        in_specs=[pl.BlockSpec((tm, w), lambda i: (i, 0)), pl.BlockSpec((tm, w), lambda i: (i, 1)),
                  _full((CONV_WIDTH, w)), vec, wspec, vec, wspec, vec, vec],
        out_specs=[_rows(tm, w)] * 5,
        out_shape=[_sds((t, w), BF16)] + [_sds((t, w), F32)] * 4,
        scratch_shapes=[pltpu.VMEM((8, w), F32), pltpu.VMEM((8, w), F32)],
        args=(proj, proj, cw, cb, wa, ba, wx, bx, lam))


def _lru_bwd(proj, cw, wa, wx, lam, u, r, ig, h, dy, side):
    t = proj.shape[0]
    w = D_MODEL
    tm = min(LRU_TM, t)
    nt = t // tm

    def body(gb_ref, up_ref, cw_ref, wa_ref, wx_ref, lam_ref, u_ref, r_ref, i_ref, h_ref, hp_ref, dy_ref,
             dp_ref, dcw_ref, dcb_ref, dwa_ref, dba_ref, dwx_ref, dbx_ref, dlam_ref, carry_sc, dutail_sc, dls_sc):
        step = pl.program_id(0)

        @pl.when(step == 0)
        def _():
            carry_sc[...] = jnp.zeros_like(carry_sc)
            dutail_sc[...] = jnp.zeros_like(dutail_sc)
            dls_sc[...] = jnp.zeros_like(dls_sc)
            for ref in (dcw_ref, dcb_ref, dwa_ref, dba_ref, dwx_ref, dbx_ref):
                ref[...] = jnp.zeros_like(ref)

        row = lax.broadcasted_iota(jnp.int32, (tm, w), 0)
        u_t, r_t, i_t, h_t = u_ref[...], r_ref[...], i_ref[...], h_ref[...]
        ls, a, em, mult = _lru_gates(u_t, r_t, i_t, lam_ref[...])
        gb = gb_ref[...]
        gel, th = _gelu(gb)
        dyv = dy_ref[...]
        dgb = dyv * h_t * (0.5 * (1.0 + th) + 0.5 * gb * (1.0 - th * th) * GELU_K * (1.0 + 3.0 * GELU_C * gb * gb))
        a_next = jnp.where(row < tm - 1, pltpu.roll(a, tm - 1, 0), 1.0)
        g = _scan_rows(a_next, dyv * gel, carry_sc[0:1, :], reverse=True)
        carry_sc[0:1, :] = a[0:1] * g[0:1]
        has_prev = jnp.where(step == nt - 1, 0.0, 1.0)
        h_prev = jnp.where(row >= 1, pltpu.roll(h_t, 1, 0), hp_ref[7:8, :] * has_prev)
        dmult = g * i_t * u_t
        dig = g * mult * u_t
        du = g * mult * i_t
        dla = g * h_prev * a - dmult * (1.0 - em) / mult
        dls_sc[0:1, :] += jnp.sum(dla * r_t, axis=0, keepdims=True) * LRU_C
        dpa = dla * (LRU_C * ls) * r_t * (1.0 - r_t)
        dpx = dig * i_t * (1.0 - i_t)
        dba_ref[...] += jnp.sum(dpa, axis=0, keepdims=True)
        dbx_ref[...] += jnp.sum(dpx, axis=0, keepdims=True)
        ub = u_t.astype(BF16)
        dpab = dpa.astype(BF16)
        dpxb = dpx.astype(BF16)
        for n in range(LRU_BLOCKS):
            cs = slice(LRU_BLOCK * n, LRU_BLOCK * (n + 1))
            dwa_ref[n] += lax.dot_general(ub[:, cs], dpab[:, cs], TN_DIMS, preferred_element_type=F32)
            dwx_ref[n] += lax.dot_general(ub[:, cs], dpxb[:, cs], TN_DIMS, preferred_element_type=F32)
        du = du + _block_diag_dot(dpab, wa_ref, NT_DIMS) + _block_diag_dot(dpxb, wx_ref, NT_DIMS)
        dcb_ref[...] += jnp.sum(du, axis=0, keepdims=True)
        ext = jnp.concatenate([du, dutail_sc[...]], axis=0)
        up = up_ref[...]
        dup = cw_ref[CONV_WIDTH - 1:CONV_WIDTH, :] * du
        dcw_ref[CONV_WIDTH - 1:CONV_WIDTH, :] += jnp.sum(up * du, axis=0, keepdims=True)
        for s in range(1, CONV_WIDTH):
            du_s = pltpu.roll(ext, tm + 8 - s, 0)[0:tm]
            dup = dup + cw_ref[CONV_WIDTH - 1 - s:CONV_WIDTH - s, :] * du_s
            dcw_ref[CONV_WIDTH - 1 - s:CONV_WIDTH - s, :] += jnp.sum(up * du_s, axis=0, keepdims=True)
        dutail_sc[...] = du[0:8]
        dp_ref[:, :w] = dgb.astype(BF16)
        dp_ref[:, w:] = dup.astype(BF16)

        @pl.when(step == nt - 1)
        def _():
            dlam_ref[...] = dls_sc[0:1, :] * _sigmoid(-lam_ref[...])

    rev = lambda col: (lambda i: (nt - 1 - i, col))
    vec = _full((1, w))
    wspec = _full((LRU_BLOCKS, LRU_BLOCK, LRU_BLOCK))
    tile = pl.BlockSpec((tm, w), rev(0))
    prev8 = pl.BlockSpec((8, w), lambda i: (jnp.maximum((nt - 1 - i) * (tm // 8) - 1, 0), 0))
    return _side_call(
        body, side, name="lru_bwd", steps=nt,
        in_specs=[tile, pl.BlockSpec((tm, w), rev(1)), _full((CONV_WIDTH, w)), wspec, wspec, vec,
                  tile, tile, tile, tile, prev8, tile],
        out_specs=[pl.BlockSpec((tm, 2 * w), rev(0)), _full((CONV_WIDTH, w)), vec, wspec, vec, wspec, vec, vec],
        out_shape=[_sds((t, 2 * w), BF16), _sds((CONV_WIDTH, w), F32), _sds((1, w), F32),
                   _sds((LRU_BLOCKS, LRU_BLOCK, LRU_BLOCK), F32), _sds((1, w), F32),
                   _sds((LRU_BLOCKS, LRU_BLOCK, LRU_BLOCK), F32), _sds((1, w), F32), _sds((1, w), F32)],
        scratch_shapes=[pltpu.VMEM((8, w), F32), pltpu.VMEM((8, w), F32), pltpu.VMEM((8, w), F32)],
        args=(proj, proj, cw, wa, wx, lam, u, r, ig, h, h, dy))


LANES = 1024
HALF_FFN = D_FF // N_CHIPS
GROUPS = {
    "ml": (("ml_w_in", 1024, ML_IN // N_CHIPS), ("ml_w_out", 256, 1024)),
    "lru": (("lru_w_in", 1024, 512),
            ("lru_gates", 2 * LRU_BLOCKS * 64, LRU_BLOCK),
            ("lru_w_out", 256, 1024)),
    "ffn0": (("ffn_g0", HALF_FFN, 1024), ("ffn_u0", HALF_FFN, 1024), ("ffn_down0", HALF_FFN, 1024)),
    "ffn1": (("ffn_g1", HALF_FFN, 1024), ("ffn_u1", HALF_FFN, 1024), ("ffn_down1", HALF_FFN, 1024)),
}

SMALL_ROWS = 24
ROW_LOSS, ROW_HEAD_NORM, ROW_B_IF, ROW_LRU = 8, 9, 10, 11


def _row_tile(rows, cols, itemsize, budget=3 << 19):
    best = 16
    for t in range(16, rows + 1, 16):
        if rows % t == 0 and t * cols * itemsize <= budget:
            best = t
    return best


def _chip_peers():
    x, y, c = lax.axis_index("x"), lax.axis_index("y"), lax.axis_index("c")
    return x, y, c, [(1 - x, y), (x, 1 - y), (1 - x, 1 - y)]


HBM_SPEC = pl.BlockSpec(memory_space=pltpu.HBM)


def _remote(src, dst, send_sems, recv_sems, k, to):
    return pltpu.make_async_remote_copy(src_ref=src, dst_ref=dst, send_sem=send_sems.at[k], recv_sem=recv_sems.at[k],
                                        device_id=to, device_id_type=MESH)


GATHER_SEMS = 7


def _gather_copies(kind, specs, part_refs, out_refs, send_sems, recv_sems):
    x, y, c, chips = _chip_peers()
    me = 2 * x + y
    sib = (x, y, 1 - c)
    copy = functools.partial(_remote, send_sems=send_sems, recv_sems=recv_sems)
    out = []
    for p, (_, rows, _) in enumerate(specs):
        mine = pl.ds(c * (rows // 2), rows // 2)
        theirs = pl.ds((1 - c) * (rows // 2), rows // 2)
        base = GATHER_SEMS * p
        for j, (cx, cy) in enumerate(chips):
            land = out_refs[p].at[2 * cx + cy, mine]
            other = out_refs[p].at[2 * cx + cy, theirs]
            if kind == "first":
                out.append(copy(part_refs[p].at[mine], out_refs[p].at[me, mine], k=base + j, to=(cx, cy, c)))
            elif kind == "landed":
                out.append(copy(land, land, k=base + j, to=sib))
            elif kind == "forward":
                out.append(copy(land, land, k=base + 3 + j, to=sib))
            else:
                out.append(copy(other, other, k=base + 3 + j, to=sib))
        if kind in ("first", "arriving"):
            out.append(copy(part_refs[p], out_refs[p].at[me], k=base + 6, to=sib))
    return out


def _gather_start(specs, *refs):
    for cp in _gather_copies("first", specs, *refs):
        cp.start()


def _gather_finish(specs, *refs):
    forwards = _gather_copies("forward", specs, *refs)
    for land, fwd in zip(_gather_copies("landed", specs, *refs), forwards):
        land.wait_recv()
        fwd.start()
    for cp in _gather_copies("arriving", specs, *refs):
        cp.wait_recv()
    for cp in _gather_copies("first", specs, *refs) + forwards:
        cp.wait_send()


def _gather_shapes(specs):
    return [_sds((N_CHIPS, rows, cols), BF16) for _, rows, cols in specs]


def _gather_side(specs, parts):
    return _Side(inputs=list(parts), out_shape=_gather_shapes(specs), n_sems=GATHER_SEMS * len(specs),
                 start=functools.partial(_gather_start, specs), finish=functools.partial(_gather_finish, specs))


def _gather_weights(specs, parts, small):
    n = len(specs)

    def body(*refs):
        part_refs, small_ref = refs[:n], refs[n]
        out_refs, outs_ref = refs[n + 1:2 * n + 1], refs[2 * n + 1]
        send_sems, recv_sems, small_send, small_recv, loc_sem = refs[2 * n + 2:]
        x, y, c, chips = _chip_peers()
        me = 2 * x + y
        local = pltpu.make_async_copy(small_ref, outs_ref.at[me], loc_sem.at[0])
        local.start()
        _gather_start(specs, part_refs, out_refs, send_sems, recv_sems)
        sent = [_remote(small_ref, outs_ref.at[me], small_send, small_recv, j, (cx, cy, c))
                for j, (cx, cy) in enumerate(chips)]
        for cp in sent:
            cp.start()
        _gather_finish(specs, part_refs, out_refs, send_sems, recv_sems)
        for j, (cx, cy) in enumerate(chips):
            _remote(small_ref, outs_ref.at[2 * cx + cy], small_send, small_recv, j, (cx, cy, c)).wait_recv()
        for cp in sent:
            cp.wait_send()
        local.wait()

    dma = pltpu.SemaphoreType.DMA
    return pl.pallas_call(
        body, name="gather_weights",
        in_specs=[HBM_SPEC] * (n + 1), out_specs=[HBM_SPEC] * (n + 1),
        out_shape=_gather_shapes(specs) + [_sds((N_CHIPS,) + small.shape, small.dtype)],
        scratch_shapes=[dma((GATHER_SEMS * n,)), dma((GATHER_SEMS * n,)), dma((3,)), dma((3,)), dma((1,))],
    )(*parts, small)


def _exchange_copies(specs, g_refs, out_refs, send_sems, recv_sems):
    x, y, c, _ = _chip_peers()
    return [_remote(g_refs[p].at[:, pl.ds((1 - c) * (rows // 2), rows // 2)], out_refs[p], send_sems, recv_sems, p,
                    (x, y, 1 - c)) for p, (_, rows, _) in enumerate(specs)]


def _exchange_start(specs, *refs):
    for cp in _exchange_copies(specs, *refs):
        cp.start()


def _exchange_finish(specs, *refs):
    for cp in _exchange_copies(specs, *refs):
        cp.wait()


def _exchange_shapes(specs):
    return [_sds((N_CHIPS, rows // 2, cols), BF16) for _, rows, cols in specs]


def _exchange_side(specs, gparts):
    return _Side(inputs=list(gparts), out_shape=_exchange_shapes(specs), n_sems=len(specs),
                 start=functools.partial(_exchange_start, specs), finish=functools.partial(_exchange_finish, specs))


def _exchange_halves(specs, gparts, name):
    n = len(specs)

    def body(*refs):
        _exchange_start(specs, refs[:n], refs[n:2 * n], *refs[2 * n:])
        _exchange_finish(specs, refs[:n], refs[n:2 * n], *refs[2 * n:])

    return pl.pallas_call(
        body, name=name, in_specs=[HBM_SPEC] * n, out_specs=[HBM_SPEC] * n, out_shape=_exchange_shapes(specs),
        scratch_shapes=[pltpu.SemaphoreType.DMA((n,)), pltpu.SemaphoreType.DMA((n,))],
    )(*gparts)


def _add_halves(g, recv, pos, name):
    _, half, cols = recv.shape
    tr = _row_tile(half, cols, 2)
    tiles = half // tr

    def body(pos_ref, a_ref, b_ref, o_ref):
        o_ref[...] = (a_ref[...].astype(F32) + b_ref[...].astype(F32)).astype(BF16)

    spec = pl.BlockSpec((1, tr, cols), lambda k, i, pos_ref: (k, i, 0))
    return pl.pallas_call(
        body, name=name,
        grid_spec=pltpu.PrefetchScalarGridSpec(
            num_scalar_prefetch=1, grid=(N_CHIPS, tiles),
            in_specs=[pl.BlockSpec((1, tr, cols), lambda k, i, pos_ref: (k, pos_ref[0] * tiles + i, 0)), spec],
            out_specs=spec),
        out_shape=_sds((N_CHIPS, half, cols), BF16),
        compiler_params=_params("parallel", "parallel"),
    )(pos, g, recv)


def _scatter_copies(n, s1_refs, recv_refs, send_sems, recv_sems):
    x, y, c, chips = _chip_peers()
    return [_remote(s1_refs[p].at[2 * cx + cy], recv_refs[p].at[j], send_sems, recv_sems, 3 * p + j, (cx, cy, c))
            for p in range(n) for j, (cx, cy) in enumerate(chips)]


def _scatter_start(n, s1_refs, recv_refs, send_sems, recv_sems):
    for cp in _scatter_copies(n, s1_refs, recv_refs, send_sems, recv_sems):
        cp.start()


def _scatter_finish(n, s1_refs, recv_refs, send_sems, recv_sems):
    for cp in _scatter_copies(n, s1_refs, recv_refs, send_sems, recv_sems):
        cp.wait()


def _scatter_shapes(s1):
    return [_sds((3,) + a.shape[1:], a.dtype) for a in s1]


def _scatter_side(s1):
    n = len(s1)
    return _Side(inputs=list(s1), out_shape=_scatter_shapes(s1), n_sems=3 * n,
                 start=functools.partial(_scatter_start, n), finish=functools.partial(_scatter_finish, n))


def _share_small(small):
    flips = [(fx, fy, fc) for fx in (0, 1) for fy in (0, 1) for fc in (0, 1)][1:]

    def body(small_ref, all_ref, send_sems, recv_sems, loc_sem):
        x, y, c, _ = _chip_peers()
        my_slot = all_ref.at[4 * x + 2 * y + c]
        local = pltpu.make_async_copy(small_ref, my_slot, loc_sem.at[0])
        local.start()
        peers = [(1 - x if fx else x, 1 - y if fy else y, 1 - c if fc else c) for fx, fy, fc in flips]
        sent = [_remote(small_ref, my_slot, send_sems, recv_sems, i, p) for i, p in enumerate(peers)]
        for cp in sent:
            cp.start()
        for i, (px, py, pc) in enumerate(peers):
            _remote(small_ref, all_ref.at[4 * px + 2 * py + pc], send_sems, recv_sems, i, peers[i]).wait_recv()
        for cp in sent:
            cp.wait_send()
        local.wait()

    dma = pltpu.SemaphoreType.DMA
    return pl.pallas_call(
        body, name="share_small", in_specs=[HBM_SPEC], out_specs=HBM_SPEC,
        out_shape=_sds((8,) + small.shape, small.dtype), scratch_shapes=[dma((7,)), dma((7,)), dma((1,))],
    )(small)


def _sum_chips(s1, recv, pos, name):
    _, half, cols = recv.shape
    tr = _row_tile(half, cols, 4)
    tiles = half // tr

    def body(pos_ref, a_ref, b_ref, o_ref):
        acc = a_ref[0].astype(F32)
        for j in range(3):
            acc = acc + b_ref[j].astype(F32)
        o_ref[...] = acc

    return pl.pallas_call(
        body, name=name,
        grid_spec=pltpu.PrefetchScalarGridSpec(
            num_scalar_prefetch=1, grid=(tiles,),
            in_specs=[pl.BlockSpec((1, tr, cols), lambda i, pos_ref: (pos_ref[1], i, 0)),
                      pl.BlockSpec((3, tr, cols), lambda i, pos_ref: (0, i, 0))],
            out_specs=pl.BlockSpec((tr, cols), lambda i, pos_ref: (pos_ref[0] * tiles + i, 0))),
        out_shape=_sds((2 * half, cols), F32),
        compiler_params=_params("parallel"),
    )(pos, s1, recv)


def _sum_small(small_all):
    def body(a_ref, o_ref):
        acc = a_ref[0]
        for d in range(1, 8):
            acc = acc + a_ref[d]
        o_ref[...] = acc

    return pl.pallas_call(body, name="sum_small", out_shape=_sds(small_all.shape[1:], F32))(small_all)


def _join_halves(specs, s2):
    n = len(specs)

    def body(*refs):
        buf_refs = refs[n:2 * n]
        send_sems, recv_sems = refs[2 * n:]
        x, y, c, _ = _chip_peers()
        sent = []
        for p, (_, rows, _) in enumerate(specs):
            mine = buf_refs[p].at[pl.ds(c * (rows // 2), rows // 2)]
            sent.append(_remote(mine, mine, send_sems, recv_sems, p, (x, y, 1 - c)))
            sent[-1].start()
        for p, (_, rows, _) in enumerate(specs):
            theirs = buf_refs[p].at[pl.ds((1 - c) * (rows // 2), rows // 2)]
            _remote(theirs, theirs, send_sems, recv_sems, p, (x, y, 1 - c)).wait_recv()
        for cp in sent:
            cp.wait_send()

    return pl.pallas_call(
        body, name="join_halves", in_specs=[HBM_SPEC] * n, out_specs=[HBM_SPEC] * n,
        out_shape=[_sds(a.shape, a.dtype) for a in s2],
        input_output_aliases={p: p for p in range(n)},
        scratch_shapes=[pltpu.SemaphoreType.DMA((n,)), pltpu.SemaphoreType.DMA((n,))],
    )(*s2)


def _adamw(w, g, m, v, name):
    rows, cols = w.shape[0], w.shape[-1]
    tm = rows
    for cand in (512, 256, 128, 64, 32, 16, 8) if w.ndim == 2 else (rows // 4,):
        if rows % cand == 0 and rows > cand:
            tm = cand
            break

    def body(w_ref, g_ref, m_ref, v_ref, d_ref, nm_ref, nv_ref):
        gv = g_ref[...]
        nm = ADAM_B1 * m_ref[...] + (1.0 - ADAM_B1) * gv
        nv = ADAM_B2 * v_ref[...] + (1.0 - ADAM_B2) * (gv * gv)
        m_hat = nm / (1.0 - ADAM_B1 ** ADAM_STEP)
        v_hat = nv / (1.0 - ADAM_B2 ** ADAM_STEP)
        d_ref[...] = -ADAM_LR * (m_hat / (jnp.sqrt(v_hat) + ADAM_EPS) + ADAM_WD * w_ref[...])
        nm_ref[...] = nm
        nv_ref[...] = nv

    spec = _rows(tm, cols) if w.ndim == 2 else pl.BlockSpec((tm, 1, cols), lambda i: (i, 0, 0))
    return pl.pallas_call(
        body, name=name, grid=(rows // tm,), in_specs=[spec] * 4, out_specs=[spec] * 3,
        out_shape=[_sds(w.shape, F32)] * 3, compiler_params=_params("parallel"),
    )(w, g, m, v)


WEIGHTS = ("ml_w_in", "ml_b_if", "ml_head_norm", "ml_w_out", "lru_w_in", "lru_conv_w", "lru_conv_b", "lru_w_gate_a",
           "lru_b_gate_a", "lru_w_gate_x", "lru_b_gate_x", "lru_lambda", "lru_w_out", "norm_pre_mix", "norm_post_mix",
           "norm_pre_ffn", "norm_post_ffn", "ffn_w_gate", "ffn_w_up", "ffn_w_down")
LRU_VECTORS = ("lru_conv_b", "lru_b_gate_a", "lru_b_gate_x", "lru_lambda")
NORMS = ("norm_pre_mix", "norm_post_mix", "norm_pre_ffn", "norm_post_ffn")


def _by_cols(a):
    return jnp.concatenate([a[k] for k in range(N_CHIPS)], axis=-1)


def _by_chip(a, width):
    return jnp.stack([a[..., k * width:(k + 1) * width] for k in range(N_CHIPS)])


def _weight_shards(w):
    bf = lambda a: a.astype(BF16)
    shards = dict(
        ml_w_in=bf(w["ml_w_in"][0]), ml_w_out=bf(w["ml_w_out"][0]), lru_w_in=bf(w["lru_w_in"][0]),
        lru_gates=bf(jnp.concatenate([w["lru_w_gate_a"][0], w["lru_w_gate_x"][0]], axis=0)).reshape(-1, LRU_BLOCK),
        lru_w_out=bf(w["lru_w_out"][0]))
    gate_t, up_t = bf(jnp.swapaxes(w["ffn_w_gate"], 1, 2)), bf(jnp.swapaxes(w["ffn_w_up"], 1, 2))
    for layer in range(2):
        shards[f"ffn_g{layer}"] = gate_t[layer]
        shards[f"ffn_u{layer}"] = up_t[layer]
        shards[f"ffn_down{layer}"] = bf(w["ffn_w_down"][layer])
    return shards


def _ffn_weights(parts):
    return [a.reshape(D_FF, 1024) for a in parts]


def _ffn_grads(dgu, z, act, dy, layer):
    dws = [_matmul_tn(dgu, z, True, 256, f"ffn{layer}_dw_gate", window=(0, D_FF)),
           _matmul_tn(dgu, z, True, 256, f"ffn{layer}_dw_up", window=(D_FF, D_FF)),
           _matmul_tn(act, dy, True, 256, f"ffn{layer}_dw_down")]
    return [a.reshape(N_CHIPS, HALF_FFN, 1024) for a in dws]


def _sum_group(group, s1, recv, pos):
    return [_sum_chips(a, r, pos, "sum_chips_" + n) for (n, _, _), a, r in zip(GROUPS[group], s1, recv)]


def _update(w, grads, m, v):
    delta, new_m, new_v = {}, {}, {}
    for n in WEIGHTS:
        if n == "ml_w_in":
            view = lambda a: jnp.transpose(a, (2, 0, 1))
            back = lambda a: jnp.transpose(a, (1, 2, 0))
        else:
            flip = (lambda a: jnp.swapaxes(a, 1, 2)) if n in ("ffn_w_gate", "ffn_w_up") else (lambda a: a)
            shape = flip(w[n]).shape
            view = lambda a: flip(a).reshape(-1, shape[-1])
            back = lambda a: flip(a.reshape(shape))
        d, nm, nv = _adamw(view(w[n]), view(grads[n]), view(m[n]), view(v[n]), "adamw_" + n)
        delta[n], new_m[n], new_v[n] = back(d), back(nm), back(nv)
    return delta, new_m, new_v


def kernel(x, ml_w_in, ml_b_if, ml_head_norm, ml_w_out, lru_w_in, lru_conv_w, lru_conv_b, lru_w_gate_a, lru_b_gate_a, lru_w_gate_x, lru_b_gate_x, lru_lambda, lru_w_out, norm_pre_mix, norm_post_mix, norm_pre_ffn, norm_post_ffn, ffn_w_gate, ffn_w_up, ffn_w_down, loss_target, m_ml_w_in, m_ml_b_if, m_ml_head_norm, m_ml_w_out, m_lru_w_in, m_lru_conv_w, m_lru_conv_b, m_lru_w_gate_a, m_lru_b_gate_a, m_lru_w_gate_x, m_lru_b_gate_x, m_lru_lambda, m_lru_w_out, m_norm_pre_mix, m_norm_post_mix, m_norm_pre_ffn, m_norm_post_ffn, m_ffn_w_gate, m_ffn_w_up, m_ffn_w_down, v_ml_w_in, v_ml_b_if, v_ml_head_norm, v_ml_w_out, v_lru_w_in, v_lru_conv_w, v_lru_conv_b, v_lru_w_gate_a, v_lru_b_gate_a, v_lru_w_gate_x, v_lru_b_gate_x, v_lru_lambda, v_lru_w_out, v_norm_pre_mix, v_norm_post_mix, v_norm_pre_ffn, v_norm_post_ffn, v_ffn_w_gate, v_ffn_w_up, v_ffn_w_down):
    args = locals()
    w = {n: args[n] for n in WEIGHTS}
    m = {n: args["m_" + n] for n in WEIGHTS}
    v = {n: args["v_" + n] for n in WEIGHTS}
    xs, target = x[0], loss_target[0]
    mx, my, mc, _ = _chip_peers()
    chip = 2 * mx + my
    pos = jnp.stack([mc, chip])
    row = lambda a, i: a[i:i + 1]
    npm, nqm, npf, nqf = (w[n] for n in NORMS)
    shards = _weight_shards(w)
    of = lambda group: [shards[n] for n, _, _ in GROUPS[group]]
    bif = jnp.pad(w["ml_b_if"], ((0, 0), (0, 128 - 2 * ML_HEADS)))
    hn = w["ml_head_norm"]

    small = jnp.concatenate([w["lru_conv_w"][0]] + [w[n] for n in LRU_VECTORS], axis=0)
    ml_in_spec, ml_out_spec = GROUPS["ml"][:1], GROUPS["ml"][1:]
    ml_in_parts, smalls = _gather_weights(ml_in_spec, of("ml")[:1], small)
    ml_w_in = jnp.pad(_by_cols(ml_in_parts), ((0, 0), (0, ML_IN_PAD - ML_IN)))
    vec = _by_cols(smalls)
    lru_spec, ffn1_spec = GROUPS["lru"], GROUPS["ffn1"]
    (z0, qkv, og), got = _norm_matmul(xs, row(npm, 0), ml_w_in, 2 * ML_QK + ML_V, "ml_in",
                                      _gather_side(ml_out_spec, of("ml")[1:]))
    ml_w_out = got[0].reshape(1024, 1024)
    (y0, hs, cst, nst, mst), got = _mlstm_fwd(qkv, og, bif, hn, _gather_side(GROUPS["ffn0"], of("ffn0")))
    wg0, wu0, w_down0 = _ffn_weights(got)
    (ymix0, h1), got = _matmul_postnorm(y0, ml_w_out, row(nqm, 0), xs, "ml_out",
                                        _gather_side(lru_spec[:1], of("lru")[:1]))
    lru_w_in = _by_cols(got[0])
    (zf0, gu0, yf0, h2), got = _ffn_forward(
        h1, row(npf, 0), wg0, wu0, w_down0, row(nqf, 0), None, "ffn0_fwd",
        _gather_side(lru_spec[1:] + ffn1_spec[:2], of("lru")[1:] + of("ffn1")[:2]))
    gates = got[0].reshape(N_CHIPS, 2, LRU_BLOCKS, 64, LRU_BLOCK).transpose(1, 2, 0, 3, 4)
    gates = gates.reshape(2, LRU_BLOCKS, LRU_BLOCK, LRU_BLOCK)
    lru_w_out = got[1].reshape(1024, 1024)
    ffn1_gu = got[2:4]
    (z1, proj1), _ = _norm_matmul(h2, row(npm, 1), lru_w_in, 0, "lru_in", None)
    (y1, u, r, ig, hl), got = _lru_fwd(proj1, vec[0:4], vec[4:5], gates[0], vec[5:6], gates[1], vec[6:7], vec[7:8],
                                       _gather_side(ffn1_spec[2:], of("ffn1")[2:]))
    wg1, wu1, w_down1 = _ffn_weights(ffn1_gu + got)
    (ymix1, h3), _ = _matmul_postnorm(y1, lru_w_out, row(nqm, 1), h2, "lru_out", None)
    (zf1, gu1, yf1, dh4, loss_part), _ = _ffn_forward(
        h3, row(npf, 1), wg1, wu1, w_down1, row(nqf, 1), target, "ffn1_fwd", None)

    add = lambda group, gparts, recv: [_add_halves(g, r, pos, "add_halves_" + n)
                                       for (n, _, _), g, r in zip(GROUPS[group], gparts, recv)]
    (dyf1, dgu1, act1, dh3, dqf1, dpf1), _ = _ffn_backward(
        dh4, yf1, row(nqf, 1), w_down1, gu1, wg1, wu1, h3, row(npf, 1), "ffn1_bwd", None)
    g_ffn1 = _ffn_grads(dgu1, zf1, act1, dyf1, 1)

    (dymix1, dy1, dqm1), recv = _bwd_out(dh3, ymix1, row(nqm, 1), lru_w_out, "lru_bwd_out",
                                         _exchange_side(GROUPS["ffn1"], g_ffn1))
    s1_ffn1 = add("ffn1", g_ffn1, recv)
    dw_lru_out = _matmul_tn(y1, dymix1, False, 256, "lru_dw_out")
    (dproj1, dcw, dcb, dwa, dba, dwx, dbx, dlam), recv_ffn1 = _lru_bwd(
        proj1, vec[0:4], gates[0], gates[1], vec[7:8], u, r, ig, hl, dy1, _scatter_side(s1_ffn1))
    dw_lru_in = _matmul_tn(z1, dproj1, False, 512, "lru_dw_in")
    dgates = jnp.stack([dwa, dwx]).astype(BF16).reshape(2, LRU_BLOCKS, N_CHIPS, 64, LRU_BLOCK)
    dgates = dgates.transpose(2, 0, 1, 3, 4).reshape(N_CHIPS, -1, LRU_BLOCK)
    g_lru = [_by_chip(dw_lru_in, 512), dgates, dw_lru_out.reshape(N_CHIPS, 256, 1024)]
    (dh2, dpm1), recv = _bwd_in(dproj1, [lru_w_in], False, h2, row(npm, 1), dh3, "lru_bwd_in",
                                _exchange_side(GROUPS["lru"], g_lru))
    s1_lru = add("lru", g_lru, recv)

    (dyf0, dgu0, act0, dh1, dqf0, dpf0), recv_lru = _ffn_backward(
        dh2, yf0, row(nqf, 0), w_down0, gu0, wg0, wu0, h1, row(npf, 0), "ffn0_bwd", _scatter_side(s1_lru))
    g_ffn0 = _ffn_grads(dgu0, zf0, act0, dyf0, 0)

    (dymix0, dy0, dqm0), recv = _bwd_out(dh1, ymix0, row(nqm, 0), ml_w_out, "ml_bwd_out",
                                         _exchange_side(GROUPS["ffn0"], g_ffn0))
    s1_ffn0 = add("ffn0", g_ffn0, recv)
    dw_ml_out = _matmul_tn(y0, dymix0, False, 256, "ml_dw_out")
    (dproj0, dhn, dbif), recv_ffn0 = _mlstm_bwd(qkv, og, bif, hn, hs, cst, nst, mst, dy0, _scatter_side(s1_ffn0))
    dw_ml_in = _matmul_tn(z0, dproj0, False, 640, "ml_dw_in")
    g_ml = [_by_chip(dw_ml_in[:, :ML_IN], ML_IN // N_CHIPS), dw_ml_out.reshape(N_CHIPS, 256, 1024)]
    s1_ml = add("ml", g_ml, _exchange_halves(GROUPS["ml"], g_ml, "exchange_halves_ml"))
    (dx, dpm0), recv_ml = _bwd_in(dproj0, [ml_w_in], False, xs, row(npm, 0), dh1, "ml_bwd_in", _scatter_side(s1_ml))

    pad_lanes = lambda a: jnp.pad(a, ((0, 0), (0, LANES - a.shape[1])))
    small = jnp.concatenate(
        [jnp.concatenate([dpm0, dpm1]), jnp.concatenate([dqm0, dqm1]), jnp.concatenate([dpf0, dpf1]),
         jnp.concatenate([dqf0, dqf1]), pad_lanes(loss_part), dhn, pad_lanes(dbif), dcw, dcb, dba, dbx, dlam,
         jnp.zeros((SMALL_ROWS - 19, LANES), F32)], axis=0)
    small_all = _share_small(small)
    order = ("ml", "lru", "ffn0", "ffn1")
    s2 = (_sum_group("ml", s1_ml, recv_ml, pos) + _sum_group("lru", s1_lru, recv_lru, pos)
          + _sum_group("ffn0", s1_ffn0, recv_ffn0, pos) + _sum_group("ffn1", s1_ffn1, recv_ffn1, pos))
    specs = sum((GROUPS[g] for g in order), ())
    red = dict(zip([n for n, _, _ in specs], _join_halves(specs, s2)))
    vsum = _sum_small(small_all)

    dgates = red["lru_gates"].reshape(2, LRU_BLOCKS, 64, LRU_BLOCK)
    cols = lambda a: lax.dynamic_slice_in_dim(a, chip * 256, 256, axis=1)
    grads = dict(
        ml_w_in=red["ml_w_in"], ml_w_out=red["ml_w_out"], lru_w_in=red["lru_w_in"], lru_w_gate_a=dgates[0],
        lru_w_gate_x=dgates[1], lru_w_out=red["lru_w_out"],
        ffn_w_gate=jnp.swapaxes(jnp.stack([red["ffn_g0"], red["ffn_g1"]]), 1, 2),
        ffn_w_up=jnp.swapaxes(jnp.stack([red["ffn_u0"], red["ffn_u1"]]), 1, 2),
        ffn_w_down=jnp.stack([red["ffn_down0"], red["ffn_down1"]]),
        ml_head_norm=vsum[ROW_HEAD_NORM:ROW_HEAD_NORM + 1], ml_b_if=vsum[ROW_B_IF:ROW_B_IF + 1, :2 * ML_HEADS],
        lru_conv_w=cols(vsum[ROW_LRU:ROW_LRU + 4]))
    for i, n in enumerate(NORMS):
        grads[n] = vsum[2 * i:2 * i + 2]
    for i, n in enumerate(LRU_VECTORS):
        grads[n] = cols(vsum[ROW_LRU + 4 + i:ROW_LRU + 5 + i])
    loss = vsum[ROW_LOSS, 0]
    grads = {n: grads[n].reshape(w[n].shape) for n in WEIGHTS}
    delta, new_m, new_v = _update(w, grads, m, v)
    return (loss, dx[None], *[grads[n] for n in WEIGHTS], *[delta[n] for n in WEIGHTS],
            *[new_m[n] for n in WEIGHTS], *[new_v[n] for n in WEIGHTS])
```

```python
import functools
import math
from typing import Callable, NamedTuple

import jax
import jax.numpy as jnp
from jax import lax
from jax.experimental import pallas as pl
from jax.experimental.pallas import tpu as pltpu

F32 = jnp.float32
BF16 = jnp.bfloat16
MESH = pl.DeviceIdType.MESH

D_MODEL = 1024
D_FF = 2816
ML_HEADS = 8
ML_DK = 64
ML_DV = 128
ML_QK = ML_HEADS * ML_DK
ML_V = ML_HEADS * ML_DV
ML_IN = 2 * ML_QK + 2 * ML_V + 2 * ML_HEADS
ML_IN_PAD = 3200
CHUNK = 64
GATE_CAP = 15.0
ML_M_INIT = -1e30
NEG_BIG = -1e30
LRU_BLOCKS = 4
LRU_BLOCK = 256
CONV_WIDTH = 4
LRU_C = 8.0
EPS = 1e-6
QK_SCALE = ML_DK ** -0.5

ADAM_LR = 0.001
ADAM_B1 = 0.9
ADAM_B2 = 0.999
ADAM_EPS = 1e-08
ADAM_WD = 0.01
ADAM_STEP = 10

N_CHIPS = 4
V7X_VMEM_LIMIT = 56 * 1024 * 1024

NT_DIMS = (((1,), (1,)), ((), ()))
TN_DIMS = (((0,), (0,)), ((), ()))


def _params(*semantics):
    return pltpu.CompilerParams(dimension_semantics=semantics, vmem_limit_bytes=V7X_VMEM_LIMIT)


def _sds(shape, dtype):
    return jax.ShapeDtypeStruct(shape, dtype)


def _full(shape):
    return pl.BlockSpec(shape, lambda *_: (0,) * len(shape))


def _rows(tm, n):
    return pl.BlockSpec((tm, n), lambda i: (i, 0))


def _sigmoid(x):
    return 1.0 / (1.0 + jnp.exp(-x))


def _log_sigmoid(x):
    return jnp.minimum(x, 0.0) - jnp.log1p(jnp.exp(-jnp.abs(x)))


def _rms(x):
    r = lax.rsqrt(jnp.mean(x * x, axis=-1, keepdims=True) + EPS)
    return x * r, r


def _rms_bwd(xhat, r, g, dy):
    dxh = dy * g
    return r * (dxh - xhat * jnp.mean(dxh * xhat, axis=-1, keepdims=True))


SUBLANES = 8


def _scan_rows(a, b, carry, reverse=False):
    n, w = a.shape
    groups = n // SUBLANES
    a3, b3 = a.reshape(groups, SUBLANES, w), b.reshape(groups, SUBLANES, w)
    sub = lax.broadcasted_iota(jnp.int32, a3.shape, 1)
    s = 1
    while s < SUBLANES:
        keep = sub < SUBLANES - s if reverse else sub >= s
        shift = SUBLANES - s if reverse else s
        b3 = b3 + a3 * jnp.where(keep, pltpu.roll(b3, shift, 1), 0.0)
        a3 = a3 * jnp.where(keep, pltpu.roll(a3, shift, 1), 1.0)
        s *= 2
    out = [None] * groups
    for g in (reversed(range(groups)) if reverse else range(groups)):
        out[g] = b3[g] + a3[g] * carry
        carry = out[g][0:1] if reverse else out[g][SUBLANES - 1:SUBLANES]
    return jnp.concatenate(out, axis=0)


def _cumsum_rows(x, reverse=False):
    n = x.shape[0]
    row = lax.broadcasted_iota(jnp.int32, x.shape, 0)
    s = 1
    while s < n:
        if reverse:
            x = x + jnp.where(row < n - s, pltpu.roll(x, n - s, 0), 0.0)
        else:
            x = x + jnp.where(row >= s, pltpu.roll(x, s, 0), 0.0)
        s *= 2
    return x


def _norm_matmul(h, g, w, n_bf16, name, side):
    t, d = h.shape
    n = w.shape[1]
    tm = min(512, t)

    def body(h_ref, g_ref, w_ref, z_ref, *o_refs):
        xhat, _ = _rms(h_ref[...])
        z = (xhat * g_ref[...]).astype(BF16)
        z_ref[...] = z
        out = jnp.dot(z, w_ref[...], preferred_element_type=F32)
        if n_bf16:
            o_refs[0][...] = out[:, :n_bf16].astype(BF16)
            o_refs[1][...] = out[:, n_bf16:]
        else:
            o_refs[0][...] = out

    if n_bf16:
        out_specs = [_rows(tm, d), _rows(tm, n_bf16), _rows(tm, n - n_bf16)]
        out_shape = [_sds((t, d), BF16), _sds((t, n_bf16), BF16), _sds((t, n - n_bf16), F32)]
    else:
        out_specs = [_rows(tm, d), _rows(tm, n)]
        out_shape = [_sds((t, d), BF16), _sds((t, n), F32)]
    return _side_call(body, side, name=name, steps=t // tm, in_specs=[_rows(tm, d), _full((1, d)), _full((d, n))],
                      out_specs=out_specs, out_shape=out_shape, scratch_shapes=[], args=(h, g, w))


def _matmul_postnorm(a, w, g, res, name, side):
    t = res.shape[0]
    k, d = w.shape
    tm = 256

    def body(a_ref, w_ref, g_ref, res_ref, y_ref, o_ref):
        y = jnp.dot(a_ref[...], w_ref[...], preferred_element_type=F32)
        y_ref[...] = y
        yhat, _ = _rms(y)
        o_ref[...] = res_ref[...] + yhat * g_ref[...]

    return _side_call(
        body, side, name=name, steps=t // tm,
        in_specs=[_rows(tm, k), _full((k, d)), _full((1, d)), _rows(tm, d)],
        out_specs=[_rows(tm, d), _rows(tm, d)],
        out_shape=[_sds((t, d), F32), _sds((t, d), F32)],
        scratch_shapes=[], args=(a, w, g, res))


class _Side(NamedTuple):
    inputs: list
    out_shape: list
    n_sems: int
    start: Callable
    finish: Callable


def _side_call(body, side, *, name, steps, in_specs, out_specs, out_shape, scratch_shapes, args):
    n_in, n_out, n_scr = len(in_specs), len(out_specs), len(scratch_shapes)
    if side is None:
        outs = pl.pallas_call(
            body, name=name, grid=(steps,), in_specs=in_specs, out_specs=out_specs, out_shape=out_shape,
            scratch_shapes=scratch_shapes, compiler_params=_params("arbitrary"))(*args)
        return list(outs), []
    s_in, s_out = len(side.inputs), len(side.out_shape)

    def carrying(*refs):
        ins, side_ins = refs[:n_in], refs[n_in:n_in + s_in]
        outs = refs[n_in + s_in:n_in + s_in + n_out]
        side_outs = refs[n_in + s_in + n_out:n_in + s_in + n_out + s_out]
        scratch = refs[n_in + s_in + n_out + s_out:]
        own, sems = scratch[:n_scr], scratch[n_scr:]

        @pl.when(pl.program_id(0) == 0)
        def _():
            side.start(side_ins, side_outs, *sems)

        body(*ins, *outs, *own)

        @pl.when(pl.program_id(0) == steps - 1)
        def _():
            side.finish(side_ins, side_outs, *sems)

    outs = pl.pallas_call(
        carrying, name=name, grid=(steps,), in_specs=list(in_specs) + [HBM_SPEC] * s_in,
        out_specs=list(out_specs) + [HBM_SPEC] * s_out, out_shape=list(out_shape) + list(side.out_shape),
        scratch_shapes=list(scratch_shapes) + [pltpu.SemaphoreType.DMA((side.n_sems,))] * 2,
        compiler_params=_params("arbitrary"))(*args, *side.inputs)
    return list(outs[:n_out]), list(outs[n_out:])


def _ffn_forward(h, g_pre, wg_t, wu_t, w_down, g_post, target, name, side):
    t, d = h.shape
    k = w_down.shape[0]
    tm = 256

    def body(*refs):
        if target is None:
            h_ref, gpre_ref, wg_ref, wu_ref, wd_ref, gpost_ref, z_ref, gu_ref, y_ref, o_ref = refs
        else:
            h_ref, gpre_ref, wg_ref, wu_ref, wd_ref, gpost_ref, t_ref, z_ref, gu_ref, y_ref, o_ref, l_ref = refs

            @pl.when(pl.program_id(0) == 0)
            def _():
                l_ref[...] = jnp.zeros_like(l_ref)

        hv = h_ref[...]
        xhat, _ = _rms(hv)
        z = (xhat * gpre_ref[...]).astype(BF16)
        z_ref[...] = z
        gate = lax.dot_general(z, wg_ref[...], NT_DIMS, preferred_element_type=F32).astype(BF16)
        up = lax.dot_general(z, wu_ref[...], NT_DIMS, preferred_element_type=F32).astype(BF16)
        gu_ref[:, :k] = gate
        gu_ref[:, k:] = up
        gate = gate.astype(F32)
        act = (gate * _sigmoid(gate) * up.astype(F32)).astype(BF16)
        y = jnp.dot(act, wd_ref[...], preferred_element_type=F32)
        y_ref[...] = y
        yhat, _ = _rms(y)
        out = hv + yhat * gpost_ref[...]
        if target is None:
            o_ref[...] = out
        else:
            err = out - t_ref[...]
            o_ref[...] = err * (1.0 / d)
            part = jnp.sum(jnp.sum(err * err, axis=1, keepdims=True), axis=0, keepdims=True) * (0.5 / d)
            l_ref[...] += jnp.broadcast_to(part, l_ref.shape)

    in_specs = [_rows(tm, d), _full((1, d)), _full((k, d)), _full((k, d)), _full((k, d)), _full((1, d))]
    out_specs = [_rows(tm, d), _rows(tm, 2 * k), _rows(tm, d), _rows(tm, d)]
    out_shape = [_sds((t, d), BF16), _sds((t, 2 * k), BF16), _sds((t, d), F32), _sds((t, d), F32)]
    args = (h, g_pre, wg_t, wu_t, w_down, g_post)
    if target is not None:
        in_specs, args = in_specs + [_rows(tm, d)], args + (target,)
        out_specs, out_shape = out_specs + [_full((1, 128))], out_shape + [_sds((1, 128), F32)]
    return _side_call(body, side, name=name, steps=t // tm, in_specs=in_specs, out_specs=out_specs,
                      out_shape=out_shape, scratch_shapes=[], args=args)


def _bwd_out(dout, y, g, w, name, side):
    t, d = dout.shape
    k = w.shape[0]
    tm = 256

    def body(dout_ref, y_ref, g_ref, w_ref, dy_ref, da_ref, dg_ref):
        @pl.when(pl.program_id(0) == 0)
        def _():
            dg_ref[...] = jnp.zeros_like(dg_ref)

        do = dout_ref[...]
        yhat, r = _rms(y_ref[...])
        dg_ref[...] += jnp.sum(do * yhat, axis=0, keepdims=True)
        dy = _rms_bwd(yhat, r, g_ref[...], do).astype(BF16)
        dy_ref[...] = dy
        da_ref[...] = lax.dot_general(dy, w_ref[...], NT_DIMS, preferred_element_type=F32)

    return _side_call(
        body, side, name=name, steps=t // tm,
        in_specs=[_rows(tm, d), _rows(tm, d), _full((1, d)), _full((k, d))],
        out_specs=[_rows(tm, d), _rows(tm, k), _full((1, d))],
        out_shape=[_sds((t, d), BF16), _sds((t, k), F32), _sds((1, d), F32)],
        scratch_shapes=[], args=(dout, y, g, w))


def _ffn_backward(dout, y, g_post, w_down, gu, wg_t, wu_t, h, g_pre, name, side):
    t, d = dout.shape
    k = w_down.shape[0]
    tm = 256

    def body(dout_ref, y_ref, gpost_ref, wd_ref, gu_ref, wg_ref, wu_ref, h_ref, gpre_ref,
             dy_ref, dgu_ref, act_ref, dh_ref, dgpost_ref, dgpre_ref):
        @pl.when(pl.program_id(0) == 0)
        def _():
            dgpost_ref[...] = jnp.zeros_like(dgpost_ref)
            dgpre_ref[...] = jnp.zeros_like(dgpre_ref)

        do = dout_ref[...]
        yhat, r = _rms(y_ref[...])
        dgpost_ref[...] += jnp.sum(do * yhat, axis=0, keepdims=True)
        dy = _rms_bwd(yhat, r, gpost_ref[...], do).astype(BF16)
        dy_ref[...] = dy
        da = lax.dot_general(dy, wd_ref[...], NT_DIMS, preferred_element_type=F32)
        gate = gu_ref[:, :k].astype(F32)
        up = gu_ref[:, k:].astype(F32)
        sg = _sigmoid(gate)
        silu = gate * sg
        act_ref[...] = (silu * up).astype(BF16)
        dgate = (da * up * (sg * (1.0 + gate * (1.0 - sg)))).astype(BF16)
        dup = (da * silu).astype(BF16)
        dgu_ref[:, :k] = dgate
        dgu_ref[:, k:] = dup
        dz = (jnp.dot(dgate, wg_ref[...], preferred_element_type=F32)
              + jnp.dot(dup, wu_ref[...], preferred_element_type=F32))
        hhat, r2 = _rms(h_ref[...])
        dgpre_ref[...] += jnp.sum(dz * hhat, axis=0, keepdims=True)
        dh_ref[...] = do + _rms_bwd(hhat, r2, gpre_ref[...], dz)

    vec = _full((1, d))
    wspec = _full((k, d))
    return _side_call(
        body, side, name=name, steps=t // tm,
        in_specs=[_rows(tm, d), _rows(tm, d), vec, wspec, _rows(tm, 2 * k), wspec, wspec, _rows(tm, d), vec],
        out_specs=[_rows(tm, d), _rows(tm, 2 * k), _rows(tm, k), _rows(tm, d), vec, vec],
        out_shape=[_sds((t, d), BF16), _sds((t, 2 * k), BF16), _sds((t, k), BF16), _sds((t, d), F32),
                   _sds((1, d), F32), _sds((1, d), F32)],
        scratch_shapes=[], args=(dout, y, g_post, w_down, gu, wg_t, wu_t, h, g_pre))


def _bwd_in(dp, ws, transposed, h, g, dout, name, side):
    t, d = h.shape
    n = dp.shape[1]
    tm = 256
    widths = [w.shape[0] if transposed else w.shape[1] for w in ws]

    def body(dp_ref, *refs):
        w_refs = refs[:len(ws)]
        h_ref, g_ref, dout_ref, dh_ref, dg_ref = refs[len(ws):]

        @pl.when(pl.program_id(0) == 0)
        def _():
            dg_ref[...] = jnp.zeros_like(dg_ref)

        dz, at = None, 0
        for w_ref, width in zip(w_refs, widths):
            block = dp_ref[:, at:at + width]
            if transposed:
                part = jnp.dot(block, w_ref[...], preferred_element_type=F32)
            else:
                part = lax.dot_general(block, w_ref[...], NT_DIMS, preferred_element_type=F32)
            dz = part if dz is None else dz + part
            at += width
        hhat, r = _rms(h_ref[...])
        dg_ref[...] += jnp.sum(dz * hhat, axis=0, keepdims=True)
        dh_ref[...] = dout_ref[...] + _rms_bwd(hhat, r, g_ref[...], dz)

    return _side_call(
        body, side, name=name, steps=t // tm,
        in_specs=[_rows(tm, n)] + [_full(w.shape) for w in ws] + [_rows(tm, d), _full((1, d)), _rows(tm, d)],
        out_specs=[_rows(tm, d), _full((1, d))],
        out_shape=[_sds((t, d), F32), _sds((1, d), F32)],
        scratch_shapes=[], args=(dp, *ws, h, g, dout))


def _matmul_tn(a, b, tile_a, tile, name, window=None):
    t, ka = a.shape
    nb = b.shape[1]
    col0 = 0
    if window is not None:
        col0, ka = window

    def body(a_ref, b_ref, o_ref):
        o_ref[...] = lax.dot_general(a_ref[...], b_ref[...], TN_DIMS, preferred_element_type=F32).astype(BF16)

    if tile_a:
        grid = (ka // tile,)
        in_specs = [pl.BlockSpec((t, tile), lambda i: (0, col0 // tile + i)), _full((t, nb))]
        out_specs = pl.BlockSpec((tile, nb), lambda i: (i, 0))
    else:
        grid = (nb // tile,)
        in_specs = [_full((t, ka)), pl.BlockSpec((t, tile), lambda i: (0, i))]
        out_specs = pl.BlockSpec((ka, tile), lambda i: (0, i))
    return pl.pallas_call(
        body, name=name, grid=grid, in_specs=in_specs, out_specs=out_specs,
        out_shape=_sds((ka, nb), BF16), compiler_params=_params("parallel"),
    )(a, b)


def _ml_gate_prep(gt, bif):
    th = jnp.tanh((gt + bif) / GATE_CAP)
    act = GATE_CAP * th
    cum = _cumsum_rows(_log_sigmoid(act))
    lane = lax.broadcasted_iota(jnp.int32, gt.shape, 1)
    x = jnp.where(lane < ML_HEADS, act, cum)
    return x, x.T, th, act


HEADS = range(ML_HEADS)


def _each(fn, *per_head):
    return [fn(*a) for a in zip(*per_head)]


def _ml_chunk_fwd(q, k, v, kt, x, xt, c_in, n_in, m_in):
    causal = (lax.broadcasted_iota(jnp.int32, (CHUNK, CHUNK), 0)
              >= lax.broadcasted_iota(jnp.int32, (CHUNK, CHUNK), 1))
    f = {}
    qh = f["qh"] = [q[:, ML_DK * h:ML_DK * (h + 1)] for h in HEADS]
    kh = f["kh"] = [k[:, ML_DK * h:ML_DK * (h + 1)] for h in HEADS]
    f["vh"] = [v[:, ML_DV * h:ML_DV * (h + 1)] for h in HEADS]
    kth = [kt[ML_DK * h:ML_DK * (h + 1), :] for h in HEADS]
    s = _each(lambda a, b: jnp.dot(a, b, preferred_element_type=F32) * QK_SCALE, qh, kth)
    f["qc"] = _each(lambda a, c: jnp.dot(a, c.astype(BF16), preferred_element_type=F32) * QK_SCALE, qh, c_in)
    bcol = f["bcol"] = [x[:, ML_HEADS + h:ML_HEADS + h + 1] for h in HEADS]
    licol = f["licol"] = [x[:, h:h + 1] for h in HEADS]
    brow = f["brow"] = [xt[ML_HEADS + h:ML_HEADS + h + 1, :] for h in HEADS]
    lirow = [xt[h:h + 1, :] for h in HEADS]
    dmat = _each(lambda bc, br, lr: jnp.where(causal, bc - br + lr, NEG_BIG), bcol, brow, lirow)
    inter = _each(lambda bc, m: bc + m, bcol, m_in)
    mt = _each(lambda d, i: jnp.maximum(jnp.max(d, axis=1, keepdims=True), i), dmat, inter)
    wt = f["wt"] = _each(lambda d, m: jnp.exp(d - m), dmat, mt)
    p = f["p"] = _each(lambda a, b: a * b, wt, s)
    winter = f["winter"] = _each(lambda i, m: jnp.exp(i - m), inter, mt)
    qf = f["qf"] = [a.astype(F32) for a in qh]
    qn = f["qn"] = _each(lambda a, n: jnp.sum(a * n, axis=1, keepdims=True) * QK_SCALE, qf, n_in)
    den = f["den"] = _each(lambda a, w, b: jnp.sum(a, axis=1, keepdims=True) + w * b, p, winter, qn)
    emt = f["emt"] = [jnp.exp(-m) for m in mt]
    f["nrm"] = _each(lambda d, e: jnp.maximum(jnp.abs(d), e), den, emt)
    gtot = [bc[CHUNK - 1:CHUNK, :] for bc in bcol]
    a_col = _each(lambda g, bc, lc: g - bc + lc, gtot, bcol, licol)
    a_row = _each(lambda g, br, lr: g - br + lr, gtot, brow, lirow)
    m_new = f["m_new"] = _each(lambda g, m, a: jnp.maximum(g + m, jnp.max(a, axis=1, keepdims=True)),
                               gtot, m_in, a_row)
    f["decay"] = _each(lambda g, m, mn: jnp.exp(g + m - mn), gtot, m_in, m_new)
    wkf = f["wkf"] = _each(lambda a, mn: jnp.exp(a - mn), a_col, m_new)
    f["kw"] = _each(lambda a, w: a.astype(F32) * w, kh, wkf)
    f["ktw"] = _each(lambda a, ar, mn: (a.astype(F32) * jnp.exp(ar - mn)).astype(BF16), kth, a_row, m_new)
    return f


ML_STEP_CHUNKS = 4
ML_STEP = CHUNK * ML_STEP_CHUNKS


def _ml_specs(steps, rev):
    def at(col):
        if rev:
            return lambda i: (steps - 1 - i, col)
        return lambda i: (i, col)

    return [pl.BlockSpec((ML_STEP, ML_QK), at(0)), pl.BlockSpec((ML_STEP, ML_QK), at(1)),
            pl.BlockSpec((ML_STEP, ML_V), at(1)), pl.BlockSpec((ML_STEP, ML_V), at(0)),
            pl.BlockSpec((ML_STEP, 128), at(ML_V // 128))]


def _mlstm_fwd(qkv, og, bif, hn, side):
    t = qkv.shape[0]
    nc = t // CHUNK
    steps = t // ML_STEP

    def body(q_ref, k_ref, v_ref, o_ref, gt_ref, bif_ref, hn_ref, y_ref, hs_ref, cst_ref, nst_ref, mst_ref,
             c_sc, n_sc, m_sc):
        @pl.when(pl.program_id(0) == 0)
        def _():
            c_sc[...] = jnp.zeros_like(c_sc)
            n_sc[...] = jnp.zeros_like(n_sc)
            m_sc[...] = jnp.full_like(m_sc, ML_M_INIT)

        c_all, n_all, m_all = c_sc[...], n_sc[...], m_sc[...]
        c_in = [c_all[ML_DK * h:ML_DK * (h + 1), :] for h in HEADS]
        n_in = [n_all[h:h + 1, :] for h in HEADS]
        m_in = [m_all[h:h + 1, 0:1] for h in HEADS]
        for sub in range(ML_STEP_CHUNKS):
            rs = slice(CHUNK * sub, CHUNK * (sub + 1))
            for h in HEADS:
                cst_ref[sub, ML_DK * h:ML_DK * (h + 1), :] = c_in[h]
                nst_ref[sub, h:h + 1, :] = n_in[h]
                mst_ref[sub, h:h + 1, :] = jnp.broadcast_to(m_in[h], (1, 128))
            x, xt, _, _ = _ml_gate_prep(gt_ref[rs, :], bif_ref[...])
            q, k, v = q_ref[rs, :], k_ref[rs, :], v_ref[rs, :]
            f = _ml_chunk_fwd(q, k, v, k.T, x, xt, c_in, n_in, m_in)
            num = _each(lambda p, vh, w, qc: jnp.dot(p.astype(BF16), vh, preferred_element_type=F32) + w * qc,
                        f["p"], f["vh"], f["winter"], f["qc"])
            hh = _each(lambda a, b: a / b, num, f["nrm"])
            hhat = [_rms(a)[0] for a in hh]
            c_in = _each(lambda d, c, kw, vh: d * c + jnp.dot(kw, vh, preferred_element_type=F32),
                         f["decay"], c_in, f["ktw"], f["vh"])
            n_in = _each(lambda d, n, kw: d * n + jnp.sum(kw, axis=0, keepdims=True), f["decay"], n_in, f["kw"])
            m_in = f["m_new"]
            for h in HEADS:
                vs = slice(ML_DV * h, ML_DV * (h + 1))
                hs_ref[rs, vs] = hh[h]
                y_ref[rs, vs] = (hhat[h] * hn_ref[:, vs] * _sigmoid(o_ref[rs, vs])).astype(BF16)
        for h in HEADS:
            c_sc[ML_DK * h:ML_DK * (h + 1), :] = c_in[h]
            n_sc[h:h + 1, :] = n_in[h]
            m_sc[h:h + 1, :] = jnp.broadcast_to(m_in[h], (1, 128))

    return _side_call(
        body, side, name="mlstm_fwd", steps=steps,
        in_specs=_ml_specs(steps, False) + [_full((1, 128)), _full((1, ML_V))],
        out_specs=[_rows(ML_STEP, ML_V), _rows(ML_STEP, ML_V),
                   pl.BlockSpec((ML_STEP_CHUNKS, ML_HEADS * ML_DK, ML_DV), lambda i: (i, 0, 0)),
                   pl.BlockSpec((ML_STEP_CHUNKS, ML_HEADS, ML_DK), lambda i: (i, 0, 0)),
                   pl.BlockSpec((ML_STEP_CHUNKS, ML_HEADS, 128), lambda i: (i, 0, 0))],
        out_shape=[_sds((t, ML_V), BF16), _sds((t, ML_V), F32),
                   _sds((nc, ML_HEADS * ML_DK, ML_DV), F32), _sds((nc, ML_HEADS, ML_DK), F32),
                   _sds((nc, ML_HEADS, 128), F32)],
        scratch_shapes=[pltpu.VMEM((ML_HEADS * ML_DK, ML_DV), F32), pltpu.VMEM((ML_HEADS, ML_DK), F32),
                        pltpu.VMEM((ML_HEADS, 128), F32)],
        args=(qkv, qkv, qkv, og, og, bif, hn))


def _mlstm_bwd(qkv, og, bif, hn, hs, cst, nst, mst, dy, side):
    t = qkv.shape[0]
    steps = t // ML_STEP

    def chunk(q_ref, k_ref, v_ref, o_ref, gt_ref, bif_ref, hn_ref, hs_ref, cst_ref, nst_ref, mst_ref, dy_ref,
              dp_ref, dhn_ref, dbif_ref, carried):
        x, xt, th, act = _ml_gate_prep(gt_ref[...], bif_ref[...])
        q, k, v = q_ref[...], k_ref[...], v_ref[...]
        qt, vt = q.T, v.T
        rows = lax.broadcasted_iota(jnp.int32, (CHUNK, CHUNK), 0)
        cols = lax.broadcasted_iota(jnp.int32, (CHUNK, CHUNK), 1)
        lane = lax.broadcasted_iota(jnp.int32, (CHUNK, 128), 1)
        row = lax.broadcasted_iota(jnp.int32, (CHUNK, 1), 0)
        as_row = lambda col: jnp.sum(jnp.where(rows == cols, col, 0.0), axis=0, keepdims=True)
        mm = lambda a, b: jnp.dot(a, b, preferred_element_type=F32)
        bf = lambda a: a.astype(BF16)
        ksl = [slice(ML_DK * h, ML_DK * (h + 1)) for h in HEADS]
        vsl = [slice(ML_DV * h, ML_DV * (h + 1)) for h in HEADS]
        c_in = [cst_ref[0, s, :] for s in ksl]
        n_in = [nst_ref[0, h:h + 1, :] for h in HEADS]
        m_in = [mst_ref[0, h:h + 1, 0:1] for h in HEADS]
        dcn, dcn_t, dnn = carried
        f = _ml_chunk_fwd(q, k, v, k.T, x, xt, c_in, n_in, m_in)
        qh, kh, vh, p, winter, decay = f["qh"], f["kh"], f["vh"], f["p"], f["winter"], f["decay"]
        qth = [qt[s, :] for s in ksl]
        vth = [vt[s, :] for s in vsl]
        c_t = [bf(c.T) for c in c_in]
        dmat_t = _each(lambda br, bc, lc: jnp.where(rows <= cols, br - bc + lc, NEG_BIG),
                       f["brow"], f["bcol"], f["licol"])
        inter_row = _each(lambda br, m: br + m, f["brow"], m_in)
        mt_row = _each(lambda d, i: jnp.maximum(jnp.max(d, axis=0, keepdims=True), i), dmat_t, inter_row)
        wt_t = _each(lambda d, m: jnp.exp(d - m), dmat_t, mt_row)
        p_t = _each(lambda w, a, b: w * (mm(a, b) * QK_SCALE), wt_t, kh, qth)
        winter_row = _each(lambda i, m: jnp.exp(i - m), inter_row, mt_row)
        hh = [hs_ref[:, s] for s in vsl]
        hn_h = [hn_ref[:, s] for s in vsl]
        sg = [_sigmoid(o_ref[:, s]) for s in vsl]
        dyh = [dy_ref[:, s] for s in vsl]
        norm = [_rms(a) for a in hh]
        hhat, r = [a for a, _ in norm], [b for _, b in norm]
        dyn = _each(lambda a, b: a * b, dyh, sg)
        do = _each(lambda d, hx, g, s: d * hx * g * s * (1.0 - s), dyh, hhat, hn_h, sg)
        dhn = _each(lambda a, b: jnp.sum(a * b, axis=0, keepdims=True), dyn, hhat)
        dh = _each(_rms_bwd, hhat, r, hn_h, dyn)
        inv = [1.0 / a for a in f["nrm"]]
        dnum = _each(lambda a, b: a * b, dh, inv)
        dnrm = _each(lambda a, b, c: -jnp.sum(a * b, axis=1, keepdims=True) * c, dh, hh, inv)
        dden = _each(lambda d, e, g: jnp.where(jnp.abs(d) > e, g * jnp.sign(d), 0.0), f["den"], f["emt"], dnrm)
        dnb = [bf(a) for a in dnum]
        dnt = [bf(a.T) for a in dnum]
        rmat = _each(lambda a, b, d: mm(a, b) + d, dnb, vth, dden)
        rmat_t = _each(lambda a, b, d: mm(a, b) + as_row(d), vh, dnt, dden)
        ds = _each(lambda w, a: bf(w * a), f["wt"], rmat)
        ds_t = _each(lambda w, a: bf(w * a), wt_t, rmat_t)
        dv = _each(lambda a, b: mm(bf(a), b), p_t, dnb)
        dqs = _each(lambda s, kk, w, d, ct, dd, n: mm(s, kk) + w * (mm(d, ct) + dd * n),
                    ds, kh, winter, dnb, c_t, dden, n_in)
        dk = _each(lambda s, a: mm(s, a) * QK_SCALE, ds_t, qh)
        dinter = _each(lambda qc, dn_, qn, dd, w: (jnp.sum(qc * dn_, axis=1, keepdims=True) + qn * dd) * w,
                       f["qc"], dnum, f["qn"], dden, winter)
        wq = _each(lambda w, a: w * a * QK_SCALE, winter, f["qf"])
        wq_t = _each(lambda w, a: bf(w * a.astype(F32) * QK_SCALE), winter_row, qth)
        dc_loc = _each(mm, wq_t, dnb)
        dct_loc = _each(lambda a, b: mm(a, bf(b)), dnt, wq)
        dn_loc = _each(lambda a, d: jnp.sum(a * d, axis=0, keepdims=True), wq, dden)
        cs_q = _each(lambda a, b: jnp.sum(a * b, axis=1, keepdims=True), p_t, rmat_t)
        db = _each(lambda a, b, di, cs: jnp.sum(a * b, axis=1, keepdims=True) + di - cs, p, rmat, dinter, cs_q)
        ddecay = _each(lambda dc_, c, dn_, n: jnp.sum(jnp.sum(dc_ * c, axis=1, keepdims=True), axis=0, keepdims=True)
                       + jnp.sum(dn_ * n, axis=1, keepdims=True), dcn, c_in, dnn, n_in)
        dkw = _each(lambda a, b, n: mm(a, bf(b)) + n, vh, dcn_t, dnn)
        dk = _each(lambda a, w, b: a + w * b, dk, f["wkf"], dkw)
        da = _each(lambda a, kk, w: jnp.sum(a * kk.astype(F32), axis=1, keepdims=True) * w, dkw, kh, f["wkf"])
        dv = _each(lambda a, kw, dc_: a + mm(bf(kw), bf(dc_)), dv, f["kw"], dcn)
        dgtot = _each(lambda a, dd, d: jnp.sum(a, axis=0, keepdims=True) + dd * d, da, ddecay, decay)
        db = _each(lambda a, b, g: a - b + jnp.where(row == CHUNK - 1, g, 0.0), db, da, dgtot)
        dli = _each(lambda a, b: a + b, cs_q, da)
        dc_new = _each(lambda d, a, b: d * a + b, decay, dcn, dc_loc)
        dct_new = _each(lambda d, a, b: d * a + b, decay, dcn_t, dct_loc)
        dn_new = _each(lambda d, a, b: d * a + b, decay, dnn, dn_loc)
        dx = jnp.zeros((CHUNK, 128), F32)
        for h in HEADS:
            dhn_ref[:, vsl[h]] += dhn[h]
            dp_ref[:, ksl[h]] = bf(dqs[h] * QK_SCALE)
            dp_ref[:, ML_QK + ML_DK * h:ML_QK + ML_DK * (h + 1)] = bf(dk[h])
            dp_ref[:, 2 * ML_QK + ML_DV * h:2 * ML_QK + ML_DV * (h + 1)] = bf(dv[h])
            dp_ref[:, 2 * ML_QK + ML_V + ML_DV * h:2 * ML_QK + ML_V + ML_DV * (h + 1)] = bf(do[h])
            dx = jnp.where(lane == h, dli[h], dx)
            dx = jnp.where(lane == ML_HEADS + h, db[h], dx)
        dlf = _cumsum_rows(dx, reverse=True)
        dact = jnp.where(lane < ML_HEADS, dx, dlf * _sigmoid(-act))
        dz = dact * (1.0 - th * th)
        dp_ref[:, 2 * ML_QK + 2 * ML_V:] = dz.astype(BF16)
        dbif_ref[...] += jnp.sum(dz, axis=0, keepdims=True)
        return dc_new, dct_new, dn_new

    def body(q_ref, k_ref, v_ref, o_ref, gt_ref, bif_ref, hn_ref, hs_ref, cst_ref, nst_ref, mst_ref, dy_ref,
             dp_ref, dhn_ref, dbif_ref, dc_sc, dct_sc, dn_sc):
        @pl.when(pl.program_id(0) == 0)
        def _():
            dc_sc[...] = jnp.zeros_like(dc_sc)
            dct_sc[...] = jnp.zeros_like(dct_sc)
            dn_sc[...] = jnp.zeros_like(dn_sc)
            dhn_ref[...] = jnp.zeros_like(dhn_ref)
            dbif_ref[...] = jnp.zeros_like(dbif_ref)

        dc_all, dct_all, dn_all = dc_sc[...], dct_sc[...], dn_sc[...]
        carried = ([dc_all[ML_DK * h:ML_DK * (h + 1), :] for h in HEADS],
                   [dct_all[ML_DV * h:ML_DV * (h + 1), :] for h in HEADS],
                   [dn_all[h:h + 1, :] for h in HEADS])
        for sub in reversed(range(ML_STEP_CHUNKS)):
            rs = pl.ds(CHUNK * sub, CHUNK)
            one = pl.ds(sub, 1)
            carried = chunk(q_ref.at[rs], k_ref.at[rs], v_ref.at[rs], o_ref.at[rs], gt_ref.at[rs], bif_ref, hn_ref,
                            hs_ref.at[rs], cst_ref.at[one], nst_ref.at[one], mst_ref.at[one], dy_ref.at[rs],
                            dp_ref.at[rs], dhn_ref, dbif_ref, carried)
        for h in HEADS:
            dc_sc[ML_DK * h:ML_DK * (h + 1), :] = carried[0][h]
            dct_sc[ML_DV * h:ML_DV * (h + 1), :] = carried[1][h]
            dn_sc[h:h + 1, :] = carried[2][h]

    rev = lambda i: (steps - 1 - i, 0)
    rev3 = lambda i: (steps - 1 - i, 0, 0)
    return _side_call(
        body, side, name="mlstm_bwd", steps=steps,
        in_specs=_ml_specs(steps, True) + [
            _full((1, 128)), _full((1, ML_V)), pl.BlockSpec((ML_STEP, ML_V), rev),
            pl.BlockSpec((ML_STEP_CHUNKS, ML_HEADS * ML_DK, ML_DV), rev3),
            pl.BlockSpec((ML_STEP_CHUNKS, ML_HEADS, ML_DK), rev3),
            pl.BlockSpec((ML_STEP_CHUNKS, ML_HEADS, 128), rev3), pl.BlockSpec((ML_STEP, ML_V), rev)],
        out_specs=[pl.BlockSpec((ML_STEP, ML_IN_PAD), rev), _full((1, ML_V)), _full((1, 128))],
        out_shape=[_sds((t, ML_IN_PAD), BF16), _sds((1, ML_V), F32), _sds((1, 128), F32)],
        scratch_shapes=[pltpu.VMEM((ML_HEADS * ML_DK, ML_DV), F32), pltpu.VMEM((ML_HEADS * ML_DV, ML_DK), F32),
                        pltpu.VMEM((ML_HEADS, ML_DK), F32)],
        args=(qkv, qkv, qkv, og, og, bif, hn, hs, cst, nst, mst, dy))


LRU_TM = 256
GELU_K = math.sqrt(2.0 / math.pi)
GELU_C = 0.044715


def _gelu(x):
    th = jnp.tanh(GELU_K * (x + GELU_C * x * x * x))
    return 0.5 * x * (1.0 + th), th


def _neg_expm1(x):
    series = -x * (1.0 + x * (0.5 + x * (1.0 / 6.0 + x * (1.0 / 24.0))))
    return jnp.where(x > -0.05, series, 1.0 - jnp.exp(x))


def _block_diag_dot(a, w_ref, dims):
    parts = [lax.dot_general(a[:, LRU_BLOCK * n:LRU_BLOCK * (n + 1)], w_ref[n], dims, preferred_element_type=F32)
             for n in range(LRU_BLOCKS)]
    return jnp.concatenate(parts, axis=1)


def _lru_gates(r, lam):
    ls = _log_sigmoid(lam)
    la = LRU_C * r * ls
    a = jnp.exp(la)
    em = _neg_expm1(2.0 * la)
    mult = jnp.sqrt(em)
    return ls, a, em, mult


def _lru_fwd(proj, cw, cb, wa, ba, wx, bx, lam, side):
    t = proj.shape[0]
    w = D_MODEL
    tm = min(LRU_TM, t)

    def body(gb_ref, up_ref, cw_ref, cb_ref, wa_ref, ba_ref, wx_ref, bx_ref, lam_ref,
             y_ref, u_ref, r_ref, i_ref, h_ref, tail_sc, hprev_sc):
        @pl.when(pl.program_id(0) == 0)
        def _():
            tail_sc[...] = jnp.zeros_like(tail_sc)
            hprev_sc[...] = jnp.zeros_like(hprev_sc)

        up = up_ref[...]
        ext = jnp.concatenate([tail_sc[...], up], axis=0)
        u = cb_ref[...] + cw_ref[CONV_WIDTH - 1:CONV_WIDTH, :] * up
        for s in range(1, CONV_WIDTH):
            u = u + cw_ref[CONV_WIDTH - 1 - s:CONV_WIDTH - s, :] * pltpu.roll(ext, s, 0)[8:8 + tm]
        tail_sc[...] = up[tm - 8:tm]
        ub = u.astype(BF16)
        r = _sigmoid(_block_diag_dot(ub, wa_ref, (((1,), (0,)), ((), ()))) + ba_ref[...])
        ig = _sigmoid(_block_diag_dot(ub, wx_ref, (((1,), (0,)), ((), ()))) + bx_ref[...])
        _, a, _, mult = _lru_gates(r, lam_ref[...])
        h = _scan_rows(a, mult * ig * u, hprev_sc[0:1, :])
        hprev_sc[0:1, :] = h[tm - 1:tm]
        u_ref[...] = u
        r_ref[...] = r
        i_ref[...] = ig
        h_ref[...] = h
        gel, _ = _gelu(gb_ref[...])
        y_ref[...] = (h * gel).astype(BF16)

    vec = _full((1, w))
    wspec = _full((LRU_BLOCKS, LRU_BLOCK, LRU_BLOCK))
    return _side_call(
        body, side, name="lru_fwd", steps=t // tm,
        in_specs=[pl.BlockSpec((tm, w), lambda i: (i, 0)), pl.BlockSpec((tm, w), lambda i: (i, 1)),
                  _full((CONV_WIDTH, w)), vec, wspec, vec, wspec, vec, vec],
        out_specs=[_rows(tm, w)] * 5,
        out_shape=[_sds((t, w), BF16)] + [_sds((t, w), F32)] * 4,
        scratch_shapes=[pltpu.VMEM((8, w), F32), pltpu.VMEM((8, w), F32)],
        args=(proj, proj, cw, cb, wa, ba, wx, bx, lam))


def _lru_bwd(proj, cw, wa, wx, lam, u, r, ig, h, dy, side):
    t = proj.shape[0]
    w = D_MODEL
    tm = min(LRU_TM, t)
    nt = t // tm

    def body(gb_ref, up_ref, cw_ref, wa_ref, wx_ref, lam_ref, u_ref, r_ref, i_ref, h_ref, hp_ref, dy_ref,
             dp_ref, dcw_ref, dcb_ref, dwa_ref, dba_ref, dwx_ref, dbx_ref, dlam_ref, carry_sc, dutail_sc, dls_sc):
        step = pl.program_id(0)

        @pl.when(step == 0)
        def _():
            carry_sc[...] = jnp.zeros_like(carry_sc)
            dutail_sc[...] = jnp.zeros_like(dutail_sc)
            dls_sc[...] = jnp.zeros_like(dls_sc)
            for ref in (dcw_ref, dcb_ref, dwa_ref, dba_ref, dwx_ref, dbx_ref):
                ref[...] = jnp.zeros_like(ref)

        row = lax.broadcasted_iota(jnp.int32, (tm, w), 0)
        u_t, r_t, i_t, h_t = u_ref[...], r_ref[...], i_ref[...], h_ref[...]
        ls, a, em, mult = _lru_gates(r_t, lam_ref[...])
        gb = gb_ref[...]
        gel, th = _gelu(gb)
        dyv = dy_ref[...]
        dgb = dyv * h_t * (0.5 * (1.0 + th) + 0.5 * gb * (1.0 - th * th) * GELU_K * (1.0 + 3.0 * GELU_C * gb * gb))
        a_next = jnp.where(row < tm - 1, pltpu.roll(a, tm - 1, 0), 1.0)
        g = _scan_rows(a_next, dyv * gel, carry_sc[0:1, :], reverse=True)
        carry_sc[0:1, :] = a[0:1] * g[0:1]
        has_prev = jnp.where(step == nt - 1, 0.0, 1.0)
        h_prev = jnp.where(row >= 1, pltpu.roll(h_t, 1, 0), hp_ref[7:8, :] * has_prev)
        dmult = g * i_t * u_t
        dig = g * mult * u_t
        du = g * mult * i_t
        dla = g * h_prev * a - dmult * (1.0 - em) / mult
        dls_sc[0:1, :] += jnp.sum(dla * r_t, axis=0, keepdims=True) * LRU_C
        dpa = dla * (LRU_C * ls) * r_t * (1.0 - r_t)
        dpx = dig * i_t * (1.0 - i_t)
        dba_ref[...] += jnp.sum(dpa, axis=0, keepdims=True)
        dbx_ref[...] += jnp.sum(dpx, axis=0, keepdims=True)
        ub = u_t.astype(BF16)
        dpab = dpa.astype(BF16)
        dpxb = dpx.astype(BF16)
        for n in range(LRU_BLOCKS):
            cs = slice(LRU_BLOCK * n, LRU_BLOCK * (n + 1))
            dwa_ref[n] += lax.dot_general(ub[:, cs], dpab[:, cs], TN_DIMS, preferred_element_type=F32)
            dwx_ref[n] += lax.dot_general(ub[:, cs], dpxb[:, cs], TN_DIMS, preferred_element_type=F32)
        du = du + _block_diag_dot(dpab, wa_ref, NT_DIMS) + _block_diag_dot(dpxb, wx_ref, NT_DIMS)
        dcb_ref[...] += jnp.sum(du, axis=0, keepdims=True)
        ext = jnp.concatenate([du, dutail_sc[...]], axis=0)
        up = up_ref[...]
        dup = cw_ref[CONV_WIDTH - 1:CONV_WIDTH, :] * du
        dcw_ref[CONV_WIDTH - 1:CONV_WIDTH, :] += jnp.sum(up * du, axis=0, keepdims=True)
        for s in range(1, CONV_WIDTH):
            du_s = pltpu.roll(ext, tm + 8 - s, 0)[0:tm]
            dup = dup + cw_ref[CONV_WIDTH - 1 - s:CONV_WIDTH - s, :] * du_s
            dcw_ref[CONV_WIDTH - 1 - s:CONV_WIDTH - s, :] += jnp.sum(up * du_s, axis=0, keepdims=True)
        dutail_sc[...] = du[0:8]
        dp_ref[:, :w] = dgb.astype(BF16)
        dp_ref[:, w:] = dup.astype(BF16)

        @pl.when(step == nt - 1)
        def _():
            dlam_ref[...] = dls_sc[0:1, :] * _sigmoid(-lam_ref[...])

    rev = lambda col: (lambda i: (nt - 1 - i, col))
    vec = _full((1, w))
    wspec = _full((LRU_BLOCKS, LRU_BLOCK, LRU_BLOCK))
    tile = pl.BlockSpec((tm, w), rev(0))
    prev8 = pl.BlockSpec((8, w), lambda i: (jnp.maximum((nt - 1 - i) * (tm // 8) - 1, 0), 0))
    return _side_call(
        body, side, name="lru_bwd", steps=nt,
        in_specs=[tile, pl.BlockSpec((tm, w), rev(1)), _full((CONV_WIDTH, w)), wspec, wspec, vec,
                  tile, tile, tile, tile, prev8, tile],
        out_specs=[pl.BlockSpec((tm, 2 * w), rev(0)), _full((CONV_WIDTH, w)), vec, wspec, vec, wspec, vec, vec],
        out_shape=[_sds((t, 2 * w), BF16), _sds((CONV_WIDTH, w), F32), _sds((1, w), F32),
                   _sds((LRU_BLOCKS, LRU_BLOCK, LRU_BLOCK), F32), _sds((1, w), F32),
                   _sds((LRU_BLOCKS, LRU_BLOCK, LRU_BLOCK), F32), _sds((1, w), F32), _sds((1, w), F32)],
        scratch_shapes=[pltpu.VMEM((8, w), F32), pltpu.VMEM((8, w), F32), pltpu.VMEM((8, w), F32)],
        args=(proj, proj, cw, wa, wx, lam, u, r, ig, h, h, dy))


LANES = 1024
HALF_FFN = D_FF // N_CHIPS
GROUPS = {
    "ml": (("ml_w_in", 1024, ML_IN // N_CHIPS), ("ml_w_out", 256, 1024)),
    "lru": (("lru_w_in", 1024, 512),
            ("lru_gates", 2 * LRU_BLOCKS * 64, LRU_BLOCK),
            ("lru_w_out", 256, 1024)),
    "ffn0": (("ffn_g0", HALF_FFN, 1024), ("ffn_u0", HALF_FFN, 1024), ("ffn_down0", HALF_FFN, 1024)),
    "ffn1": (("ffn_g1", HALF_FFN, 1024), ("ffn_u1", HALF_FFN, 1024), ("ffn_down1", HALF_FFN, 1024)),
}

SMALL_ROWS = 24
ROW_LOSS, ROW_HEAD_NORM, ROW_B_IF, ROW_LRU = 8, 9, 10, 11


def _row_tile(rows, cols, itemsize, budget=3 << 19):
    best = 16
    for t in range(16, rows + 1, 16):
        if rows % t == 0 and t * cols * itemsize <= budget:
            best = t
    return best


def _chip_peers():
    x, y, c = lax.axis_index("x"), lax.axis_index("y"), lax.axis_index("c")
    return x, y, c, [(1 - x, y), (x, 1 - y), (1 - x, 1 - y)]


HBM_SPEC = pl.BlockSpec(memory_space=pltpu.HBM)


def _remote(src, dst, send_sems, recv_sems, k, to):
    return pltpu.make_async_remote_copy(src_ref=src, dst_ref=dst, send_sem=send_sems.at[k], recv_sem=recv_sems.at[k],
                                        device_id=to, device_id_type=MESH)


GATHER_SEMS = 7


def _gather_copies(kind, specs, part_refs, out_refs, send_sems, recv_sems):
    x, y, c, chips = _chip_peers()
    me = 2 * x + y
    sib = (x, y, 1 - c)
    copy = functools.partial(_remote, send_sems=send_sems, recv_sems=recv_sems)
    out = []
    for p, (_, rows, _) in enumerate(specs):
        mine = pl.ds(c * (rows // 2), rows // 2)
        theirs = pl.ds((1 - c) * (rows // 2), rows // 2)
        base = GATHER_SEMS * p
        for j, (cx, cy) in enumerate(chips):
            land = out_refs[p].at[2 * cx + cy, mine]
            other = out_refs[p].at[2 * cx + cy, theirs]
            if kind == "first":
                out.append(copy(part_refs[p].at[mine], out_refs[p].at[me, mine], k=base + j, to=(cx, cy, c)))
            elif kind == "landed":
                out.append(copy(land, land, k=base + j, to=sib))
            elif kind == "forward":
                out.append(copy(land, land, k=base + 3 + j, to=sib))
            else:
                out.append(copy(other, other, k=base + 3 + j, to=sib))
        if kind in ("first", "arriving"):
            out.append(copy(part_refs[p], out_refs[p].at[me], k=base + 6, to=sib))
    return out


def _gather_start(specs, *refs):
    for cp in _gather_copies("first", specs, *refs):
        cp.start()


def _gather_finish(specs, *refs):
    forwards = _gather_copies("forward", specs, *refs)
    for land, fwd in zip(_gather_copies("landed", specs, *refs), forwards):
        land.wait_recv()
        fwd.start()
    for cp in _gather_copies("arriving", specs, *refs):
        cp.wait_recv()
    for cp in _gather_copies("first", specs, *refs) + forwards:
        cp.wait_send()


def _gather_shapes(specs):
    return [_sds((N_CHIPS, rows, cols), BF16) for _, rows, cols in specs]


def _gather_side(specs, parts):
    return _Side(inputs=list(parts), out_shape=_gather_shapes(specs), n_sems=GATHER_SEMS * len(specs),
                 start=functools.partial(_gather_start, specs), finish=functools.partial(_gather_finish, specs))


def _gather_weights(specs, parts, small):
    n = len(specs)

    def body(*refs):
        part_refs, small_ref = refs[:n], refs[n]
        out_refs, outs_ref = refs[n + 1:2 * n + 1], refs[2 * n + 1]
        send_sems, recv_sems, small_send, small_recv, loc_sem = refs[2 * n + 2:]
        x, y, c, chips = _chip_peers()
        me = 2 * x + y
        local = pltpu.make_async_copy(small_ref, outs_ref.at[me], loc_sem.at[0])
        local.start()
        _gather_start(specs, part_refs, out_refs, send_sems, recv_sems)
        sent = [_remote(small_ref, outs_ref.at[me], small_send, small_recv, j, (cx, cy, c))
                for j, (cx, cy) in enumerate(chips)]
        for cp in sent:
            cp.start()
        _gather_finish(specs, part_refs, out_refs, send_sems, recv_sems)
        for j, (cx, cy) in enumerate(chips):
            _remote(small_ref, outs_ref.at[2 * cx + cy], small_send, small_recv, j, (cx, cy, c)).wait_recv()
        for cp in sent:
            cp.wait_send()
        local.wait()

    dma = pltpu.SemaphoreType.DMA
    return pl.pallas_call(
        body, name="gather_weights",
        in_specs=[HBM_SPEC] * (n + 1), out_specs=[HBM_SPEC] * (n + 1),
        out_shape=_gather_shapes(specs) + [_sds((N_CHIPS,) + small.shape, small.dtype)],
        scratch_shapes=[dma((GATHER_SEMS * n,)), dma((GATHER_SEMS * n,)), dma((3,)), dma((3,)), dma((1,))],
    )(*parts, small)


def _exchange_copies(specs, g_refs, out_refs, send_sems, recv_sems):
    x, y, c, _ = _chip_peers()
    return [_remote(g_refs[p].at[:, pl.ds((1 - c) * (rows // 2), rows // 2)], out_refs[p], send_sems, recv_sems, p,
                    (x, y, 1 - c)) for p, (_, rows, _) in enumerate(specs)]


def _exchange_start(specs, *refs):
    for cp in _exchange_copies(specs, *refs):
        cp.start()


def _exchange_finish(specs, *refs):
    for cp in _exchange_copies(specs, *refs):
        cp.wait()


def _exchange_shapes(specs):
    return [_sds((N_CHIPS, rows // 2, cols), BF16) for _, rows, cols in specs]


def _exchange_side(specs, gparts):
    return _Side(inputs=list(gparts), out_shape=_exchange_shapes(specs), n_sems=len(specs),
                 start=functools.partial(_exchange_start, specs), finish=functools.partial(_exchange_finish, specs))


def _exchange_halves(specs, gparts, name):
    n = len(specs)

    def body(*refs):
        _exchange_start(specs, refs[:n], refs[n:2 * n], *refs[2 * n:])
        _exchange_finish(specs, refs[:n], refs[n:2 * n], *refs[2 * n:])

    return pl.pallas_call(
        body, name=name, in_specs=[HBM_SPEC] * n, out_specs=[HBM_SPEC] * n, out_shape=_exchange_shapes(specs),
        scratch_shapes=[pltpu.SemaphoreType.DMA((n,)), pltpu.SemaphoreType.DMA((n,))],
    )(*gparts)


def _add_halves(g, recv, pos, name):
    _, half, cols = recv.shape
    tr = _row_tile(half, cols, 2)
    tiles = half // tr

    def body(pos_ref, a_ref, b_ref, o_ref):
        o_ref[...] = (a_ref[...].astype(F32) + b_ref[...].astype(F32)).astype(BF16)

    spec = pl.BlockSpec((1, tr, cols), lambda k, i, pos_ref: (k, i, 0))
    return pl.pallas_call(
        body, name=name,
        grid_spec=pltpu.PrefetchScalarGridSpec(
            num_scalar_prefetch=1, grid=(N_CHIPS, tiles),
            in_specs=[pl.BlockSpec((1, tr, cols), lambda k, i, pos_ref: (k, pos_ref[0] * tiles + i, 0)), spec],
            out_specs=spec),
        out_shape=_sds((N_CHIPS, half, cols), BF16),
        compiler_params=_params("parallel", "parallel"),
    )(pos, g, recv)


def _scatter_copies(n, s1_refs, recv_refs, send_sems, recv_sems):
    x, y, c, chips = _chip_peers()
    return [_remote(s1_refs[p].at[2 * cx + cy], recv_refs[p].at[j], send_sems, recv_sems, 3 * p + j, (cx, cy, c))
            for p in range(n) for j, (cx, cy) in enumerate(chips)]


def _scatter_start(n, s1_refs, recv_refs, send_sems, recv_sems):
    for cp in _scatter_copies(n, s1_refs, recv_refs, send_sems, recv_sems):
        cp.start()


def _scatter_finish(n, s1_refs, recv_refs, send_sems, recv_sems):
    for cp in _scatter_copies(n, s1_refs, recv_refs, send_sems, recv_sems):
        cp.wait()


def _scatter_shapes(s1):
    return [_sds((3,) + a.shape[1:], a.dtype) for a in s1]


def _scatter_side(s1):
    n = len(s1)
    return _Side(inputs=list(s1), out_shape=_scatter_shapes(s1), n_sems=3 * n,
                 start=functools.partial(_scatter_start, n), finish=functools.partial(_scatter_finish, n))


def _sum_chips(s1, recv, pos, name):
    _, half, cols = recv.shape
    tr = _row_tile(half, cols, 4)
    tiles = half // tr

    def body(pos_ref, a_ref, b_ref, o_ref):
        acc = a_ref[0].astype(F32)
        for j in range(3):
            acc = acc + b_ref[j].astype(F32)
        o_ref[...] = acc

    return pl.pallas_call(
        body, name=name,
        grid_spec=pltpu.PrefetchScalarGridSpec(
            num_scalar_prefetch=1, grid=(tiles,),
            in_specs=[pl.BlockSpec((1, tr, cols), lambda i, pos_ref: (pos_ref[1], i, 0)),
                      pl.BlockSpec((3, tr, cols), lambda i, pos_ref: (0, i, 0))],
            out_specs=pl.BlockSpec((tr, cols), lambda i, pos_ref: (pos_ref[0] * tiles + i, 0))),
        out_shape=_sds((2 * half, cols), F32),
        compiler_params=_params("parallel"),
    )(pos, s1, recv)


def _sum_small(small_all):
    def body(a_ref, o_ref):
        acc = a_ref[0]
        for d in range(1, 8):
            acc = acc + a_ref[d]
        o_ref[...] = acc

    return pl.pallas_call(body, name="sum_small", out_shape=_sds(small_all.shape[1:], F32))(small_all)


def _join_halves(specs, s2, small):
    n = len(specs)
    flips = [(fx, fy, fc) for fx in (0, 1) for fy in (0, 1) for fc in (0, 1)][1:]

    def body(*refs):
        small_ref, all_ref = refs[n], refs[2 * n + 1]
        buf_refs = refs[n + 1:2 * n + 1]
        send_sems, recv_sems, small_send, small_recv, loc_sem = refs[2 * n + 2:]
        x, y, c, _ = _chip_peers()
        my_slot = all_ref.at[4 * x + 2 * y + c]
        local = pltpu.make_async_copy(small_ref, my_slot, loc_sem.at[0])
        local.start()
        sent = []
        for p, (_, rows, _) in enumerate(specs):
            mine = buf_refs[p].at[pl.ds(c * (rows // 2), rows // 2)]
            sent.append(_remote(mine, mine, send_sems, recv_sems, p, (x, y, 1 - c)))
        peers = [(1 - x if fx else x, 1 - y if fy else y, 1 - c if fc else c) for fx, fy, fc in flips]
        sent += [_remote(small_ref, my_slot, small_send, small_recv, i, p) for i, p in enumerate(peers)]
        for cp in sent:
            cp.start()
        for p, (_, rows, _) in enumerate(specs):
            theirs = buf_refs[p].at[pl.ds((1 - c) * (rows // 2), rows // 2)]
            _remote(theirs, theirs, send_sems, recv_sems, p, (x, y, 1 - c)).wait_recv()
        for i, (px, py, pc) in enumerate(peers):
            _remote(small_ref, all_ref.at[4 * px + 2 * py + pc], small_send, small_recv, i, peers[i]).wait_recv()
        for cp in sent:
            cp.wait_send()
        local.wait()

    dma = pltpu.SemaphoreType.DMA
    *joined, small_all = pl.pallas_call(
        body, name="join_halves", in_specs=[HBM_SPEC] * (n + 1), out_specs=[HBM_SPEC] * (n + 1),
        out_shape=[_sds(a.shape, a.dtype) for a in s2] + [_sds((8,) + small.shape, small.dtype)],
        input_output_aliases={p: p for p in range(n)},
        scratch_shapes=[dma((n,)), dma((n,)), dma((7,)), dma((7,)), dma((1,))],
    )(*s2, small)
    return joined, small_all


def _adamw(w, g, m, v, name):
    rows, cols = w.shape[0], w.shape[-1]
    tm = rows
    for cand in (512, 256, 128, 64, 32, 16, 8) if w.ndim == 2 else (rows // 4,):
        if rows % cand == 0 and rows > cand:
            tm = cand
            break

    def body(w_ref, g_ref, m_ref, v_ref, d_ref, nm_ref, nv_ref):
        gv = g_ref[...]
        nm = ADAM_B1 * m_ref[...] + (1.0 - ADAM_B1) * gv
        nv = ADAM_B2 * v_ref[...] + (1.0 - ADAM_B2) * (gv * gv)
        m_hat = nm / (1.0 - ADAM_B1 ** ADAM_STEP)
        v_hat = nv / (1.0 - ADAM_B2 ** ADAM_STEP)
        d_ref[...] = -ADAM_LR * (m_hat / (jnp.sqrt(v_hat) + ADAM_EPS) + ADAM_WD * w_ref[...])
        nm_ref[...] = nm
        nv_ref[...] = nv

    spec = _rows(tm, cols) if w.ndim == 2 else pl.BlockSpec((tm, 1, cols), lambda i: (i, 0, 0))
    return pl.pallas_call(
        body, name=name, grid=(rows // tm,), in_specs=[spec] * 4, out_specs=[spec] * 3,
        out_shape=[_sds(w.shape, F32)] * 3, compiler_params=_params("parallel"),
    )(w, g, m, v)


WEIGHTS = ("ml_w_in", "ml_b_if", "ml_head_norm", "ml_w_out", "lru_w_in", "lru_conv_w", "lru_conv_b", "lru_w_gate_a",
           "lru_b_gate_a", "lru_w_gate_x", "lru_b_gate_x", "lru_lambda", "lru_w_out", "norm_pre_mix", "norm_post_mix",
           "norm_pre_ffn", "norm_post_ffn", "ffn_w_gate", "ffn_w_up", "ffn_w_down")
LRU_VECTORS = ("lru_conv_b", "lru_b_gate_a", "lru_b_gate_x", "lru_lambda")
NORMS = ("norm_pre_mix", "norm_post_mix", "norm_pre_ffn", "norm_post_ffn")


def _by_cols(a):
    return jnp.concatenate([a[k] for k in range(N_CHIPS)], axis=-1)


def _by_chip(a, width):
    return jnp.stack([a[..., k * width:(k + 1) * width] for k in range(N_CHIPS)])


def _weight_shards(w):
    bf = lambda a: a.astype(BF16)
    shards = dict(
        ml_w_in=bf(w["ml_w_in"][0]), ml_w_out=bf(w["ml_w_out"][0]), lru_w_in=bf(w["lru_w_in"][0]),
        lru_gates=bf(jnp.concatenate([w["lru_w_gate_a"][0], w["lru_w_gate_x"][0]], axis=0)).reshape(-1, LRU_BLOCK),
        lru_w_out=bf(w["lru_w_out"][0]))
    gate_t, up_t = bf(jnp.swapaxes(w["ffn_w_gate"], 1, 2)), bf(jnp.swapaxes(w["ffn_w_up"], 1, 2))
    for layer in range(2):
        shards[f"ffn_g{layer}"] = gate_t[layer]
        shards[f"ffn_u{layer}"] = up_t[layer]
        shards[f"ffn_down{layer}"] = bf(w["ffn_w_down"][layer])
    return shards


def _ffn_weights(parts):
    return [a.reshape(D_FF, 1024) for a in parts]


def _ffn_grads(dgu, z, act, dy, layer):
    dws = [_matmul_tn(dgu, z, True, 256, f"ffn{layer}_dw_gate", window=(0, D_FF)),
           _matmul_tn(dgu, z, True, 256, f"ffn{layer}_dw_up", window=(D_FF, D_FF)),
           _matmul_tn(act, dy, True, 256, f"ffn{layer}_dw_down")]
    return [a.reshape(N_CHIPS, HALF_FFN, 1024) for a in dws]


def _sum_group(group, s1, recv, pos):
    return [_sum_chips(a, r, pos, "sum_chips_" + n) for (n, _, _), a, r in zip(GROUPS[group], s1, recv)]


def _update(w, grads, m, v):
    delta, new_m, new_v = {}, {}, {}
    for n in WEIGHTS:
        if n == "ml_w_in":
            view = lambda a: jnp.transpose(a, (2, 0, 1))
            back = lambda a: jnp.transpose(a, (1, 2, 0))
        else:
            flip = (lambda a: jnp.swapaxes(a, 1, 2)) if n in ("ffn_w_gate", "ffn_w_up") else (lambda a: a)
            shape = flip(w[n]).shape
            view = lambda a: flip(a).reshape(-1, shape[-1])
            back = lambda a: flip(a.reshape(shape))
        d, nm, nv = _adamw(view(w[n]), view(grads[n]), view(m[n]), view(v[n]), "adamw_" + n)
        delta[n], new_m[n], new_v[n] = back(d), back(nm), back(nv)
    return delta, new_m, new_v


def kernel(x, ml_w_in, ml_b_if, ml_head_norm, ml_w_out, lru_w_in, lru_conv_w, lru_conv_b, lru_w_gate_a, lru_b_gate_a, lru_w_gate_x, lru_b_gate_x, lru_lambda, lru_w_out, norm_pre_mix, norm_post_mix, norm_pre_ffn, norm_post_ffn, ffn_w_gate, ffn_w_up, ffn_w_down, loss_target, m_ml_w_in, m_ml_b_if, m_ml_head_norm, m_ml_w_out, m_lru_w_in, m_lru_conv_w, m_lru_conv_b, m_lru_w_gate_a, m_lru_b_gate_a, m_lru_w_gate_x, m_lru_b_gate_x, m_lru_lambda, m_lru_w_out, m_norm_pre_mix, m_norm_post_mix, m_norm_pre_ffn, m_norm_post_ffn, m_ffn_w_gate, m_ffn_w_up, m_ffn_w_down, v_ml_w_in, v_ml_b_if, v_ml_head_norm, v_ml_w_out, v_lru_w_in, v_lru_conv_w, v_lru_conv_b, v_lru_w_gate_a, v_lru_b_gate_a, v_lru_w_gate_x, v_lru_b_gate_x, v_lru_lambda, v_lru_w_out, v_norm_pre_mix, v_norm_post_mix, v_norm_pre_ffn, v_norm_post_ffn, v_ffn_w_gate, v_ffn_w_up, v_ffn_w_down):
    args = locals()
    w = {n: args[n] for n in WEIGHTS}
    m = {n: args["m_" + n] for n in WEIGHTS}
    v = {n: args["v_" + n] for n in WEIGHTS}
    xs, target = x[0], loss_target[0]
    mx, my, mc, _ = _chip_peers()
    chip = 2 * mx + my
    pos = jnp.stack([mc, chip])
    row = lambda a, i: a[i:i + 1]
    npm, nqm, npf, nqf = (w[n] for n in NORMS)
    shards = _weight_shards(w)
    of = lambda group: [shards[n] for n, _, _ in GROUPS[group]]
    bif = jnp.pad(w["ml_b_if"], ((0, 0), (0, 128 - 2 * ML_HEADS)))
    hn = w["ml_head_norm"]

    small = jnp.concatenate([w["lru_conv_w"][0]] + [w[n] for n in LRU_VECTORS], axis=0)
    ml_in_spec, ml_out_spec = GROUPS["ml"][:1], GROUPS["ml"][1:]
    ml_in_parts, smalls = _gather_weights(ml_in_spec, of("ml")[:1], small)
    ml_w_in = jnp.pad(_by_cols(ml_in_parts), ((0, 0), (0, ML_IN_PAD - ML_IN)))
    vec = _by_cols(smalls)
    lru_spec, ffn1_spec = GROUPS["lru"], GROUPS["ffn1"]
    (z0, qkv, og), got = _norm_matmul(xs, row(npm, 0), ml_w_in, 2 * ML_QK + ML_V, "ml_in",
                                      _gather_side(ml_out_spec, of("ml")[1:]))
    ml_w_out = got[0].reshape(1024, 1024)
    (y0, hs, cst, nst, mst), got = _mlstm_fwd(qkv, og, bif, hn, _gather_side(GROUPS["ffn0"], of("ffn0")))
    wg0, wu0, w_down0 = _ffn_weights(got)
    (ymix0, h1), got = _matmul_postnorm(y0, ml_w_out, row(nqm, 0), xs, "ml_out",
                                        _gather_side(lru_spec[:1], of("lru")[:1]))
    lru_w_in = _by_cols(got[0])
    (zf0, gu0, yf0, h2), got = _ffn_forward(
        h1, row(npf, 0), wg0, wu0, w_down0, row(nqf, 0), None, "ffn0_fwd",
        _gather_side(lru_spec[1:] + ffn1_spec[:2], of("lru")[1:] + of("ffn1")[:2]))
    gates = got[0].reshape(N_CHIPS, 2, LRU_BLOCKS, 64, LRU_BLOCK).transpose(1, 2, 0, 3, 4)
    gates = gates.reshape(2, LRU_BLOCKS, LRU_BLOCK, LRU_BLOCK)
    lru_w_out = got[1].reshape(1024, 1024)
    ffn1_gu = got[2:4]
    (z1, proj1), _ = _norm_matmul(h2, row(npm, 1), lru_w_in, 0, "lru_in", None)
    (y1, u, r, ig, hl), got = _lru_fwd(proj1, vec[0:4], vec[4:5], gates[0], vec[5:6], gates[1], vec[6:7], vec[7:8],
                                       _gather_side(ffn1_spec[2:], of("ffn1")[2:]))
    wg1, wu1, w_down1 = _ffn_weights(ffn1_gu + got)
    (ymix1, h3), _ = _matmul_postnorm(y1, lru_w_out, row(nqm, 1), h2, "lru_out", None)
    (zf1, gu1, yf1, dh4, loss_part), _ = _ffn_forward(
        h3, row(npf, 1), wg1, wu1, w_down1, row(nqf, 1), target, "ffn1_fwd", None)

    add = lambda group, gparts, recv: [_add_halves(g, r, pos, "add_halves_" + n)
                                       for (n, _, _), g, r in zip(GROUPS[group], gparts, recv)]
    (dyf1, dgu1, act1, dh3, dqf1, dpf1), _ = _ffn_backward(
        dh4, yf1, row(nqf, 1), w_down1, gu1, wg1, wu1, h3, row(npf, 1), "ffn1_bwd", None)
    g_ffn1 = _ffn_grads(dgu1, zf1, act1, dyf1, 1)

    (dymix1, dy1, dqm1), recv = _bwd_out(dh3, ymix1, row(nqm, 1), lru_w_out, "lru_bwd_out",
                                         _exchange_side(GROUPS["ffn1"], g_ffn1))
    s1_ffn1 = add("ffn1", g_ffn1, recv)
    dw_lru_out = _matmul_tn(y1, dymix1, False, 256, "lru_dw_out")
    (dproj1, dcw, dcb, dwa, dba, dwx, dbx, dlam), recv_ffn1 = _lru_bwd(
        proj1, vec[0:4], gates[0], gates[1], vec[7:8], u, r, ig, hl, dy1, _scatter_side(s1_ffn1))
    dw_lru_in = _matmul_tn(z1, dproj1, False, 512, "lru_dw_in")
    dgates = jnp.stack([dwa, dwx]).astype(BF16).reshape(2, LRU_BLOCKS, N_CHIPS, 64, LRU_BLOCK)
    dgates = dgates.transpose(2, 0, 1, 3, 4).reshape(N_CHIPS, -1, LRU_BLOCK)
    g_lru = [_by_chip(dw_lru_in, 512), dgates, dw_lru_out.reshape(N_CHIPS, 256, 1024)]
    (dh2, dpm1), recv = _bwd_in(dproj1, [lru_w_in], False, h2, row(npm, 1), dh3, "lru_bwd_in",
                                _exchange_side(GROUPS["lru"], g_lru))
    s1_lru = add("lru", g_lru, recv)

    (dyf0, dgu0, act0, dh1, dqf0, dpf0), recv_lru = _ffn_backward(
        dh2, yf0, row(nqf, 0), w_down0, gu0, wg0, wu0, h1, row(npf, 0), "ffn0_bwd", _scatter_side(s1_lru))
    g_ffn0 = _ffn_grads(dgu0, zf0, act0, dyf0, 0)

    (dymix0, dy0, dqm0), recv = _bwd_out(dh1, ymix0, row(nqm, 0), ml_w_out, "ml_bwd_out",
                                         _exchange_side(GROUPS["ffn0"], g_ffn0))
    s1_ffn0 = add("ffn0", g_ffn0, recv)
    dw_ml_out = _matmul_tn(y0, dymix0, False, 256, "ml_dw_out")
    (dproj0, dhn, dbif), recv_ffn0 = _mlstm_bwd(qkv, og, bif, hn, hs, cst, nst, mst, dy0, _scatter_side(s1_ffn0))
    dw_ml_in = _matmul_tn(z0, dproj0, False, 640, "ml_dw_in")
    g_ml = [_by_chip(dw_ml_in[:, :ML_IN], ML_IN // N_CHIPS), dw_ml_out.reshape(N_CHIPS, 256, 1024)]
    s1_ml = add("ml", g_ml, _exchange_halves(GROUPS["ml"], g_ml, "exchange_halves_ml"))
    (dx, dpm0), recv_ml = _bwd_in(dproj0, [ml_w_in], False, xs, row(npm, 0), dh1, "ml_bwd_in", _scatter_side(s1_ml))

    pad_lanes = lambda a: jnp.pad(a, ((0, 0), (0, LANES - a.shape[1])))
    small = jnp.concatenate(
        [jnp.concatenate([dpm0, dpm1]), jnp.concatenate([dqm0, dqm1]), jnp.concatenate([dpf0, dpf1]),
         jnp.concatenate([dqf0, dqf1]), pad_lanes(loss_part), dhn, pad_lanes(dbif), dcw, dcb, dba, dbx, dlam,
         jnp.zeros((SMALL_ROWS - 19, LANES), F32)], axis=0)
    order = ("ml", "lru", "ffn0", "ffn1")
    s2 = (_sum_group("ml", s1_ml, recv_ml, pos) + _sum_group("lru", s1_lru, recv_lru, pos)
          + _sum_group("ffn0", s1_ffn0, recv_ffn0, pos) + _sum_group("ffn1", s1_ffn1, recv_ffn1, pos))
    specs = sum((GROUPS[g] for g in order), ())
    joined, small_all = _join_halves(specs, s2, small)
    red = dict(zip([n for n, _, _ in specs], joined))
    vsum = _sum_small(small_all)

    dgates = red["lru_gates"].reshape(2, LRU_BLOCKS, 64, LRU_BLOCK)
    cols = lambda a: lax.dynamic_slice_in_dim(a, chip * 256, 256, axis=1)
    grads = dict(
        ml_w_in=red["ml_w_in"], ml_w_out=red["ml_w_out"], lru_w_in=red["lru_w_in"], lru_w_gate_a=dgates[0],
        lru_w_gate_x=dgates[1], lru_w_out=red["lru_w_out"],
        ffn_w_gate=jnp.swapaxes(jnp.stack([red["ffn_g0"], red["ffn_g1"]]), 1, 2),
        ffn_w_up=jnp.swapaxes(jnp.stack([red["ffn_u0"], red["ffn_u1"]]), 1, 2),
        ffn_w_down=jnp.stack([red["ffn_down0"], red["ffn_down1"]]),
        ml_head_norm=vsum[ROW_HEAD_NORM:ROW_HEAD_NORM + 1], ml_b_if=vsum[ROW_B_IF:ROW_B_IF + 1, :2 * ML_HEADS],
        lru_conv_w=cols(vsum[ROW_LRU:ROW_LRU + 4]))
    for i, n in enumerate(NORMS):
        grads[n] = vsum[2 * i:2 * i + 2]
    for i, n in enumerate(LRU_VECTORS):
        grads[n] = cols(vsum[ROW_LRU + 4 + i:ROW_LRU + 5 + i])
    loss = vsum[ROW_LOSS, 0]
    grads = {n: grads[n].reshape(w[n].shape) for n in WEIGHTS}
    delta, new_m, new_v = _update(w, grads, m, v)
    return (loss, dx[None], *[grads[n] for n in WEIGHTS], *[delta[n] for n in WEIGHTS],
            *[new_m[n] for n in WEIGHTS], *[new_v[n] for n in WEIGHTS])
```

```python
import functools
import math
from typing import Callable, NamedTuple

import jax
import jax.numpy as jnp
from jax import lax
from jax.experimental import pallas as pl
from jax.experimental.pallas import tpu as pltpu

F32 = jnp.float32
BF16 = jnp.bfloat16
MESH = pl.DeviceIdType.MESH

D_MODEL = 1024
D_FF = 2816
ML_HEADS = 8
ML_DK = 64
ML_DV = 128
ML_QK = ML_HEADS * ML_DK
ML_V = ML_HEADS * ML_DV
ML_IN = 2 * ML_QK + 2 * ML_V + 2 * ML_HEADS
ML_IN_PAD = 3200
CHUNK = 64
GATE_CAP = 15.0
ML_M_INIT = -1e30
NEG_BIG = -1e30
LRU_BLOCKS = 4
LRU_BLOCK = 256
CONV_WIDTH = 4
LRU_C = 8.0
EPS = 1e-6
QK_SCALE = ML_DK ** -0.5

ADAM_LR = 0.001
ADAM_B1 = 0.9
ADAM_B2 = 0.999
ADAM_EPS = 1e-08
ADAM_WD = 0.01
ADAM_STEP = 10

N_CHIPS = 4
V7X_VMEM_LIMIT = 56 * 1024 * 1024

NT_DIMS = (((1,), (1,)), ((), ()))
TN_DIMS = (((0,), (0,)), ((), ()))


def _params(*semantics):
    return pltpu.CompilerParams(dimension_semantics=semantics, vmem_limit_bytes=V7X_VMEM_LIMIT)


def _sds(shape, dtype):
    return jax.ShapeDtypeStruct(shape, dtype)


def _full(shape):
    return pl.BlockSpec(shape, lambda *_: (0,) * len(shape))


def _rows(tm, n):
    return pl.BlockSpec((tm, n), lambda i: (i, 0))


def _sigmoid(x):
    return 1.0 / (1.0 + jnp.exp(-x))


def _log_sigmoid(x):
    return jnp.minimum(x, 0.0) - jnp.log1p(jnp.exp(-jnp.abs(x)))


def _rms(x):
    r = lax.rsqrt(jnp.mean(x * x, axis=-1, keepdims=True) + EPS)
    return x * r, r


def _rms_bwd(xhat, r, g, dy):
    dxh = dy * g
    return r * (dxh - xhat * jnp.mean(dxh * xhat, axis=-1, keepdims=True))


SUBLANES = 8


def _scan_rows(a, b, carry, reverse=False):
    n, w = a.shape
    groups = n // SUBLANES
    a3, b3 = a.reshape(groups, SUBLANES, w), b.reshape(groups, SUBLANES, w)
    sub = lax.broadcasted_iota(jnp.int32, a3.shape, 1)
    s = 1
    while s < SUBLANES:
        keep = sub < SUBLANES - s if reverse else sub >= s
        shift = SUBLANES - s if reverse else s
        b3 = b3 + a3 * jnp.where(keep, pltpu.roll(b3, shift, 1), 0.0)
        a3 = a3 * jnp.where(keep, pltpu.roll(a3, shift, 1), 1.0)
        s *= 2
    out = [None] * groups
    for g in (reversed(range(groups)) if reverse else range(groups)):
        out[g] = b3[g] + a3[g] * carry
        carry = out[g][0:1] if reverse else out[g][SUBLANES - 1:SUBLANES]
    return jnp.concatenate(out, axis=0)


def _cumsum_rows(x, reverse=False):
    n = x.shape[0]
    row = lax.broadcasted_iota(jnp.int32, x.shape, 0)
    s = 1
    while s < n:
        if reverse:
            x = x + jnp.where(row < n - s, pltpu.roll(x, n - s, 0), 0.0)
        else:
            x = x + jnp.where(row >= s, pltpu.roll(x, s, 0), 0.0)
        s *= 2
    return x


def _norm_matmul(h, g, w, n_bf16, name, side):
    t, d = h.shape
    n = w.shape[1]
    tm = min(512, t)

    def body(h_ref, g_ref, w_ref, z_ref, *o_refs):
        xhat, _ = _rms(h_ref[...])
        z = (xhat * g_ref[...]).astype(BF16)
        z_ref[...] = z
        out = jnp.dot(z, w_ref[...], preferred_element_type=F32)
        if n_bf16:
            o_refs[0][...] = out[:, :n_bf16].astype(BF16)
            o_refs[1][...] = out[:, n_bf16:]
        else:
            o_refs[0][...] = out

    if n_bf16:
        out_specs = [_rows(tm, d), _rows(tm, n_bf16), _rows(tm, n - n_bf16)]
        out_shape = [_sds((t, d), BF16), _sds((t, n_bf16), BF16), _sds((t, n - n_bf16), F32)]
    else:
        out_specs = [_rows(tm, d), _rows(tm, n)]
        out_shape = [_sds((t, d), BF16), _sds((t, n), F32)]
    return _side_call(body, side, name=name, steps=t // tm, in_specs=[_rows(tm, d), _full((1, d)), _full((d, n))],
                      out_specs=out_specs, out_shape=out_shape, scratch_shapes=[], args=(h, g, w))


def _matmul_postnorm(a, w, g, res, name, side):
    t = res.shape[0]
    k, d = w.shape
    tm = min(512, t)

    def body(a_ref, w_ref, g_ref, res_ref, y_ref, o_ref):
        y = jnp.dot(a_ref[...], w_ref[...], preferred_element_type=F32)
        y_ref[...] = y
        yhat, _ = _rms(y)
        o_ref[...] = res_ref[...] + yhat * g_ref[...]

    return _side_call(
        body, side, name=name, steps=t // tm,
        in_specs=[_rows(tm, k), _full((k, d)), _full((1, d)), _rows(tm, d)],
        out_specs=[_rows(tm, d), _rows(tm, d)],
        out_shape=[_sds((t, d), F32), _sds((t, d), F32)],
        scratch_shapes=[], args=(a, w, g, res))


class _Side(NamedTuple):
    inputs: list
    out_shape: list
    n_sems: int
    start: Callable
    finish: Callable


def _side_call(body, side, *, name, steps, in_specs, out_specs, out_shape, scratch_shapes, args):
    n_in, n_out, n_scr = len(in_specs), len(out_specs), len(scratch_shapes)
    if side is None:
        outs = pl.pallas_call(
            body, name=name, grid=(steps,), in_specs=in_specs, out_specs=out_specs, out_shape=out_shape,
            scratch_shapes=scratch_shapes, compiler_params=_params("arbitrary"))(*args)
        return list(outs), []
    s_in, s_out = len(side.inputs), len(side.out_shape)

    def carrying(*refs):
        ins, side_ins = refs[:n_in], refs[n_in:n_in + s_in]
        outs = refs[n_in + s_in:n_in + s_in + n_out]
        side_outs = refs[n_in + s_in + n_out:n_in + s_in + n_out + s_out]
        scratch = refs[n_in + s_in + n_out + s_out:]
        own, sems = scratch[:n_scr], scratch[n_scr:]

        @pl.when(pl.program_id(0) == 0)
        def _():
            side.start(side_ins, side_outs, *sems)

        body(*ins, *outs, *own)

        @pl.when(pl.program_id(0) == steps - 1)
        def _():
            side.finish(side_ins, side_outs, *sems)

    outs = pl.pallas_call(
        carrying, name=name, grid=(steps,), in_specs=list(in_specs) + [HBM_SPEC] * s_in,
        out_specs=list(out_specs) + [HBM_SPEC] * s_out, out_shape=list(out_shape) + list(side.out_shape),
        scratch_shapes=list(scratch_shapes) + [pltpu.SemaphoreType.DMA((side.n_sems,))] * 2,
        compiler_params=_params("arbitrary"))(*args, *side.inputs)
    return list(outs[:n_out]), list(outs[n_out:])


def _ffn_forward(h, g_pre, wg_t, wu_t, w_down, g_post, target, name, side):
    t, d = h.shape
    k = w_down.shape[0]
    tm = 256

    def body(*refs):
        if target is None:
            h_ref, gpre_ref, wg_ref, wu_ref, wd_ref, gpost_ref, z_ref, gu_ref, y_ref, o_ref = refs
        else:
            h_ref, gpre_ref, wg_ref, wu_ref, wd_ref, gpost_ref, t_ref, z_ref, gu_ref, y_ref, o_ref, l_ref = refs

            @pl.when(pl.program_id(0) == 0)
            def _():
                l_ref[...] = jnp.zeros_like(l_ref)

        hv = h_ref[...]
        xhat, _ = _rms(hv)
        z = (xhat * gpre_ref[...]).astype(BF16)
        z_ref[...] = z
        gate = lax.dot_general(z, wg_ref[...], NT_DIMS, preferred_element_type=F32).astype(BF16)
        up = lax.dot_general(z, wu_ref[...], NT_DIMS, preferred_element_type=F32).astype(BF16)
        gu_ref[:, :k] = gate
        gu_ref[:, k:] = up
        gate = gate.astype(F32)
        act = (gate * _sigmoid(gate) * up.astype(F32)).astype(BF16)
        y = jnp.dot(act, wd_ref[...], preferred_element_type=F32)
        y_ref[...] = y
        yhat, _ = _rms(y)
        out = hv + yhat * gpost_ref[...]
        if target is None:
            o_ref[...] = out
        else:
            err = out - t_ref[...]
            o_ref[...] = err * (1.0 / d)
            part = jnp.sum(jnp.sum(err * err, axis=1, keepdims=True), axis=0, keepdims=True) * (0.5 / d)
            l_ref[...] += jnp.broadcast_to(part, l_ref.shape)

    in_specs = [_rows(tm, d), _full((1, d)), _full((k, d)), _full((k, d)), _full((k, d)), _full((1, d))]
    out_specs = [_rows(tm, d), _rows(tm, 2 * k), _rows(tm, d), _rows(tm, d)]
    out_shape = [_sds((t, d), BF16), _sds((t, 2 * k), BF16), _sds((t, d), F32), _sds((t, d), F32)]
    args = (h, g_pre, wg_t, wu_t, w_down, g_post)
    if target is not None:
        in_specs, args = in_specs + [_rows(tm, d)], args + (target,)
        out_specs, out_shape = out_specs + [_full((1, 128))], out_shape + [_sds((1, 128), F32)]
    return _side_call(body, side, name=name, steps=t // tm, in_specs=in_specs, out_specs=out_specs,
                      out_shape=out_shape, scratch_shapes=[], args=args)


def _bwd_out(dout, y, g, w, name, side):
    t, d = dout.shape
    k = w.shape[0]
    tm = min(512, t)

    def body(dout_ref, y_ref, g_ref, w_ref, dy_ref, da_ref, dg_ref):
        @pl.when(pl.program_id(0) == 0)
        def _():
            dg_ref[...] = jnp.zeros_like(dg_ref)

        do = dout_ref[...]
        yhat, r = _rms(y_ref[...])
        dg_ref[...] += jnp.sum(do * yhat, axis=0, keepdims=True)
        dy = _rms_bwd(yhat, r, g_ref[...], do).astype(BF16)
        dy_ref[...] = dy
        da_ref[...] = lax.dot_general(dy, w_ref[...], NT_DIMS, preferred_element_type=F32)

    return _side_call(
        body, side, name=name, steps=t // tm,
        in_specs=[_rows(tm, d), _rows(tm, d), _full((1, d)), _full((k, d))],
        out_specs=[_rows(tm, d), _rows(tm, k), _full((1, d))],
        out_shape=[_sds((t, d), BF16), _sds((t, k), F32), _sds((1, d), F32)],
        scratch_shapes=[], args=(dout, y, g, w))


def _ffn_backward(dout, y, g_post, w_down, gu, wg_t, wu_t, h, g_pre, name, side):
    t, d = dout.shape
    k = w_down.shape[0]
    tm = 256

    def body(dout_ref, y_ref, gpost_ref, wd_ref, gu_ref, wg_ref, wu_ref, h_ref, gpre_ref,
             dy_ref, dgu_ref, act_ref, dh_ref, dgpost_ref, dgpre_ref):
        @pl.when(pl.program_id(0) == 0)
        def _():
            dgpost_ref[...] = jnp.zeros_like(dgpost_ref)
            dgpre_ref[...] = jnp.zeros_like(dgpre_ref)

        do = dout_ref[...]
        yhat, r = _rms(y_ref[...])
        dgpost_ref[...] += jnp.sum(do * yhat, axis=0, keepdims=True)
        dy = _rms_bwd(yhat, r, gpost_ref[...], do).astype(BF16)
        dy_ref[...] = dy
        da = lax.dot_general(dy, wd_ref[...], NT_DIMS, preferred_element_type=F32)
        gate = gu_ref[:, :k].astype(F32)
        up = gu_ref[:, k:].astype(F32)
        sg = _sigmoid(gate)
        silu = gate * sg
        act_ref[...] = (silu * up).astype(BF16)
        dgate = (da * up * (sg * (1.0 + gate * (1.0 - sg)))).astype(BF16)
        dup = (da * silu).astype(BF16)
        dgu_ref[:, :k] = dgate
        dgu_ref[:, k:] = dup
        dz = (jnp.dot(dgate, wg_ref[...], preferred_element_type=F32)
              + jnp.dot(dup, wu_ref[...], preferred_element_type=F32))
        hhat, r2 = _rms(h_ref[...])
        dgpre_ref[...] += jnp.sum(dz * hhat, axis=0, keepdims=True)
        dh_ref[...] = do + _rms_bwd(hhat, r2, gpre_ref[...], dz)

    vec = _full((1, d))
    wspec = _full((k, d))
    return _side_call(
        body, side, name=name, steps=t // tm,
        in_specs=[_rows(tm, d), _rows(tm, d), vec, wspec, _rows(tm, 2 * k), wspec, wspec, _rows(tm, d), vec],
        out_specs=[_rows(tm, d), _rows(tm, 2 * k), _rows(tm, k), _rows(tm, d), vec, vec],
        out_shape=[_sds((t, d), BF16), _sds((t, 2 * k), BF16), _sds((t, k), BF16), _sds((t, d), F32),
                   _sds((1, d), F32), _sds((1, d), F32)],
        scratch_shapes=[], args=(dout, y, g_post, w_down, gu, wg_t, wu_t, h, g_pre))


def _bwd_in(dp, ws, transposed, h, g, dout, name, side):
    t, d = h.shape
    n = dp.shape[1]
    tm = min(512, t)
    widths = [w.shape[0] if transposed else w.shape[1] for w in ws]

    def body(dp_ref, *refs):
        w_refs = refs[:len(ws)]
        h_ref, g_ref, dout_ref, dh_ref, dg_ref = refs[len(ws):]

        @pl.when(pl.program_id(0) == 0)
        def _():
            dg_ref[...] = jnp.zeros_like(dg_ref)

        dz, at = None, 0
        for w_ref, width in zip(w_refs, widths):
            block = dp_ref[:, at:at + width]
            if transposed:
                part = jnp.dot(block, w_ref[...], preferred_element_type=F32)
            else:
                part = lax.dot_general(block, w_ref[...], NT_DIMS, preferred_element_type=F32)
            dz = part if dz is None else dz + part
            at += width
        hhat, r = _rms(h_ref[...])
        dg_ref[...] += jnp.sum(dz * hhat, axis=0, keepdims=True)
        dh_ref[...] = dout_ref[...] + _rms_bwd(hhat, r, g_ref[...], dz)

    return _side_call(
        body, side, name=name, steps=t // tm,
        in_specs=[_rows(tm, n)] + [_full(w.shape) for w in ws] + [_rows(tm, d), _full((1, d)), _rows(tm, d)],
        out_specs=[_rows(tm, d), _full((1, d))],
        out_shape=[_sds((t, d), F32), _sds((1, d), F32)],
        scratch_shapes=[], args=(dp, *ws, h, g, dout))


def _matmul_tn(a, b, tile_a, tile, name, window=None):
    t, ka = a.shape
    nb = b.shape[1]
    col0 = 0
    if window is not None:
        col0, ka = window

    def body(a_ref, b_ref, o_ref):
        o_ref[...] = lax.dot_general(a_ref[...], b_ref[...], TN_DIMS, preferred_element_type=F32).astype(BF16)

    if tile_a:
        grid = (ka // tile,)
        in_specs = [pl.BlockSpec((t, tile), lambda i: (0, col0 // tile + i)), _full((t, nb))]
        out_specs = pl.BlockSpec((tile, nb), lambda i: (i, 0))
    else:
        grid = (nb // tile,)
        in_specs = [_full((t, ka)), pl.BlockSpec((t, tile), lambda i: (0, i))]
        out_specs = pl.BlockSpec((ka, tile), lambda i: (0, i))
    return pl.pallas_call(
        body, name=name, grid=grid, in_specs=in_specs, out_specs=out_specs,
        out_shape=_sds((ka, nb), BF16), compiler_params=_params("parallel"),
    )(a, b)


def _ml_gate_prep(gt, bif):
    th = jnp.tanh((gt + bif) / GATE_CAP)
    act = GATE_CAP * th
    cum = _cumsum_rows(_log_sigmoid(act))
    lane = lax.broadcasted_iota(jnp.int32, gt.shape, 1)
    x = jnp.where(lane < ML_HEADS, act, cum)
    return x, x.T, th, act


HEADS = range(ML_HEADS)


def _each(fn, *per_head):
    return [fn(*a) for a in zip(*per_head)]


def _ml_chunk_fwd(q, k, v, kt, x, xt, c_in, n_in, m_in):
    causal = (lax.broadcasted_iota(jnp.int32, (CHUNK, CHUNK), 0)
              >= lax.broadcasted_iota(jnp.int32, (CHUNK, CHUNK), 1))
    f = {}
    qh = f["qh"] = [q[:, ML_DK * h:ML_DK * (h + 1)] for h in HEADS]
    kh = f["kh"] = [k[:, ML_DK * h:ML_DK * (h + 1)] for h in HEADS]
    f["vh"] = [v[:, ML_DV * h:ML_DV * (h + 1)] for h in HEADS]
    kth = [kt[ML_DK * h:ML_DK * (h + 1), :] for h in HEADS]
    s = _each(lambda a, b: jnp.dot(a, b, preferred_element_type=F32) * QK_SCALE, qh, kth)
    f["qc"] = _each(lambda a, c: jnp.dot(a, c.astype(BF16), preferred_element_type=F32) * QK_SCALE, qh, c_in)
    bcol = f["bcol"] = [x[:, ML_HEADS + h:ML_HEADS + h + 1] for h in HEADS]
    licol = f["licol"] = [x[:, h:h + 1] for h in HEADS]
    brow = f["brow"] = [xt[ML_HEADS + h:ML_HEADS + h + 1, :] for h in HEADS]
    lirow = [xt[h:h + 1, :] for h in HEADS]
    dmat = _each(lambda bc, br, lr: jnp.where(causal, bc - br + lr, NEG_BIG), bcol, brow, lirow)
    inter = _each(lambda bc, m: bc + m, bcol, m_in)
    mt = _each(lambda d, i: jnp.maximum(jnp.max(d, axis=1, keepdims=True), i), dmat, inter)
    wt = f["wt"] = _each(lambda d, m: jnp.exp(d - m), dmat, mt)
    p = f["p"] = _each(lambda a, b: a * b, wt, s)
    winter = f["winter"] = _each(lambda i, m: jnp.exp(i - m), inter, mt)
    qf = f["qf"] = [a.astype(F32) for a in qh]
    qn = f["qn"] = _each(lambda a, n: jnp.sum(a * n, axis=1, keepdims=True) * QK_SCALE, qf, n_in)
    den = f["den"] = _each(lambda a, w, b: jnp.sum(a, axis=1, keepdims=True) + w * b, p, winter, qn)
    emt = f["emt"] = [jnp.exp(-m) for m in mt]
    f["nrm"] = _each(lambda d, e: jnp.maximum(jnp.abs(d), e), den, emt)
    gtot = [bc[CHUNK - 1:CHUNK, :] for bc in bcol]
    a_col = _each(lambda g, bc, lc: g - bc + lc, gtot, bcol, licol)
    a_row = _each(lambda g, br, lr: g - br + lr, gtot, brow, lirow)
    m_new = f["m_new"] = _each(lambda g, m, a: jnp.maximum(g + m, jnp.max(a, axis=1, keepdims=True)),
                               gtot, m_in, a_row)
    f["decay"] = _each(lambda g, m, mn: jnp.exp(g + m - mn), gtot, m_in, m_new)
    wkf = f["wkf"] = _each(lambda a, mn: jnp.exp(a - mn), a_col, m_new)
    f["kw"] = _each(lambda a, w: a.astype(F32) * w, kh, wkf)
    f["ktw"] = _each(lambda a, ar, mn: (a.astype(F32) * jnp.exp(ar - mn)).astype(BF16), kth, a_row, m_new)
    return f


ML_STEP_CHUNKS = 4
ML_STEP = CHUNK * ML_STEP_CHUNKS


def _ml_specs(steps, rev):
    def at(col):
        if rev:
            return lambda i: (steps - 1 - i, col)
        return lambda i: (i, col)

    return [pl.BlockSpec((ML_STEP, ML_QK), at(0)), pl.BlockSpec((ML_STEP, ML_QK), at(1)),
            pl.BlockSpec((ML_STEP, ML_V), at(1)), pl.BlockSpec((ML_STEP, ML_V), at(0)),
            pl.BlockSpec((ML_STEP, 128), at(ML_V // 128))]


def _mlstm_fwd(qkv, og, bif, hn, side):
    t = qkv.shape[0]
    nc = t // CHUNK
    steps = t // ML_STEP

    def body(q_ref, k_ref, v_ref, o_ref, gt_ref, bif_ref, hn_ref, y_ref, hs_ref, cst_ref, nst_ref, mst_ref,
             c_sc, n_sc, m_sc):
        @pl.when(pl.program_id(0) == 0)
        def _():
            c_sc[...] = jnp.zeros_like(c_sc)
            n_sc[...] = jnp.zeros_like(n_sc)
            m_sc[...] = jnp.full_like(m_sc, ML_M_INIT)

        c_all, n_all, m_all = c_sc[...], n_sc[...], m_sc[...]
        c_in = [c_all[ML_DK * h:ML_DK * (h + 1), :] for h in HEADS]
        n_in = [n_all[h:h + 1, :] for h in HEADS]
        m_in = [m_all[h:h + 1, 0:1] for h in HEADS]
        for sub in range(ML_STEP_CHUNKS):
            rs = slice(CHUNK * sub, CHUNK * (sub + 1))
            for h in HEADS:
                cst_ref[sub, ML_DK * h:ML_DK * (h + 1), :] = c_in[h]
                nst_ref[sub, h:h + 1, :] = n_in[h]
                mst_ref[sub, h:h + 1, :] = jnp.broadcast_to(m_in[h], (1, 128))
            x, xt, _, _ = _ml_gate_prep(gt_ref[rs, :], bif_ref[...])
            q, k, v = q_ref[rs, :], k_ref[rs, :], v_ref[rs, :]
            f = _ml_chunk_fwd(q, k, v, k.T, x, xt, c_in, n_in, m_in)
            num = _each(lambda p, vh, w, qc: jnp.dot(p.astype(BF16), vh, preferred_element_type=F32) + w * qc,
                        f["p"], f["vh"], f["winter"], f["qc"])
            hh = _each(lambda a, b: a / b, num, f["nrm"])
            hhat = [_rms(a)[0] for a in hh]
            c_in = _each(lambda d, c, kw, vh: d * c + jnp.dot(kw, vh, preferred_element_type=F32),
                         f["decay"], c_in, f["ktw"], f["vh"])
            n_in = _each(lambda d, n, kw: d * n + jnp.sum(kw, axis=0, keepdims=True), f["decay"], n_in, f["kw"])
            m_in = f["m_new"]
            for h in HEADS:
                vs = slice(ML_DV * h, ML_DV * (h + 1))
                hs_ref[rs, vs] = hh[h]
                y_ref[rs, vs] = (hhat[h] * hn_ref[:, vs] * _sigmoid(o_ref[rs, vs])).astype(BF16)
        for h in HEADS:
            c_sc[ML_DK * h:ML_DK * (h + 1), :] = c_in[h]
            n_sc[h:h + 1, :] = n_in[h]
            m_sc[h:h + 1, :] = jnp.broadcast_to(m_in[h], (1, 128))

    return _side_call(
        body, side, name="mlstm_fwd", steps=steps,
        in_specs=_ml_specs(steps, False) + [_full((1, 128)), _full((1, ML_V))],
        out_specs=[_rows(ML_STEP, ML_V), _rows(ML_STEP, ML_V),
                   pl.BlockSpec((ML_STEP_CHUNKS, ML_HEADS * ML_DK, ML_DV), lambda i: (i, 0, 0)),
                   pl.BlockSpec((ML_STEP_CHUNKS, ML_HEADS, ML_DK), lambda i: (i, 0, 0)),
                   pl.BlockSpec((ML_STEP_CHUNKS, ML_HEADS, 128), lambda i: (i, 0, 0))],
        out_shape=[_sds((t, ML_V), BF16), _sds((t, ML_V), F32),
                   _sds((nc, ML_HEADS * ML_DK, ML_DV), F32), _sds((nc, ML_HEADS, ML_DK), F32),
                   _sds((nc, ML_HEADS, 128), F32)],
        scratch_shapes=[pltpu.VMEM((ML_HEADS * ML_DK, ML_DV), F32), pltpu.VMEM((ML_HEADS, ML_DK), F32),
                        pltpu.VMEM((ML_HEADS, 128), F32)],
        args=(qkv, qkv, qkv, og, og, bif, hn))


def _mlstm_bwd(qkv, og, bif, hn, hs, cst, nst, mst, dy, side):
    t = qkv.shape[0]
    steps = t // ML_STEP

    def chunk(q_ref, k_ref, v_ref, o_ref, gt_ref, bif_ref, hn_ref, hs_ref, cst_ref, nst_ref, mst_ref, dy_ref,
              dp_ref, dhn_ref, dbif_ref, carried):
        x, xt, th, act = _ml_gate_prep(gt_ref[...], bif_ref[...])
        q, k, v = q_ref[...], k_ref[...], v_ref[...]
        qt, vt = q.T, v.T
        rows = lax.broadcasted_iota(jnp.int32, (CHUNK, CHUNK), 0)
        cols = lax.broadcasted_iota(jnp.int32, (CHUNK, CHUNK), 1)
        lane = lax.broadcasted_iota(jnp.int32, (CHUNK, 128), 1)
        row = lax.broadcasted_iota(jnp.int32, (CHUNK, 1), 0)
        as_row = lambda col: jnp.sum(jnp.where(rows == cols, col, 0.0), axis=0, keepdims=True)
        mm = lambda a, b: jnp.dot(a, b, preferred_element_type=F32)
        bf = lambda a: a.astype(BF16)
        ksl = [slice(ML_DK * h, ML_DK * (h + 1)) for h in HEADS]
        vsl = [slice(ML_DV * h, ML_DV * (h + 1)) for h in HEADS]
        c_in = [cst_ref[0, s, :] for s in ksl]
        n_in = [nst_ref[0, h:h + 1, :] for h in HEADS]
        m_in = [mst_ref[0, h:h + 1, 0:1] for h in HEADS]
        dcn, dcn_t, dnn = carried
        f = _ml_chunk_fwd(q, k, v, k.T, x, xt, c_in, n_in, m_in)
        qh, kh, vh, p, winter, decay = f["qh"], f["kh"], f["vh"], f["p"], f["winter"], f["decay"]
        qth = [qt[s, :] for s in ksl]
        vth = [vt[s, :] for s in vsl]
        c_t = [bf(c.T) for c in c_in]
        dmat_t = _each(lambda br, bc, lc: jnp.where(rows <= cols, br - bc + lc, NEG_BIG),
                       f["brow"], f["bcol"], f["licol"])
        inter_row = _each(lambda br, m: br + m, f["brow"], m_in)
        mt_row = _each(lambda d, i: jnp.maximum(jnp.max(d, axis=0, keepdims=True), i), dmat_t, inter_row)
        wt_t = _each(lambda d, m: jnp.exp(d - m), dmat_t, mt_row)
        p_t = _each(lambda w, a, b: w * (mm(a, b) * QK_SCALE), wt_t, kh, qth)
        winter_row = _each(lambda i, m: jnp.exp(i - m), inter_row, mt_row)
        hh = [hs_ref[:, s] for s in vsl]
        hn_h = [hn_ref[:, s] for s in vsl]
        sg = [_sigmoid(o_ref[:, s]) for s in vsl]
        dyh = [dy_ref[:, s] for s in vsl]
        norm = [_rms(a) for a in hh]
        hhat, r = [a for a, _ in norm], [b for _, b in norm]
        dyn = _each(lambda a, b: a * b, dyh, sg)
        do = _each(lambda d, hx, g, s: d * hx * g * s * (1.0 - s), dyh, hhat, hn_h, sg)
        dhn = _each(lambda a, b: jnp.sum(a * b, axis=0, keepdims=True), dyn, hhat)
        dh = _each(_rms_bwd, hhat, r, hn_h, dyn)
        inv = [1.0 / a for a in f["nrm"]]
        dnum = _each(lambda a, b: a * b, dh, inv)
        dnrm = _each(lambda a, b, c: -jnp.sum(a * b, axis=1, keepdims=True) * c, dh, hh, inv)
        dden = _each(lambda d, e, g: jnp.where(jnp.abs(d) > e, g * jnp.sign(d), 0.0), f["den"], f["emt"], dnrm)
        dnb = [bf(a) for a in dnum]
        dnt = [bf(a.T) for a in dnum]
        rmat = _each(lambda a, b, d: mm(a, b) + d, dnb, vth, dden)
        rmat_t = _each(lambda a, b, d: mm(a, b) + as_row(d), vh, dnt, dden)
        ds = _each(lambda w, a: bf(w * a), f["wt"], rmat)
        ds_t = _each(lambda w, a: bf(w * a), wt_t, rmat_t)
        dv = _each(lambda a, b: mm(bf(a), b), p_t, dnb)
        dqs = _each(lambda s, kk, w, d, ct, dd, n: mm(s, kk) + w * (mm(d, ct) + dd * n),
                    ds, kh, winter, dnb, c_t, dden, n_in)
        dk = _each(lambda s, a: mm(s, a) * QK_SCALE, ds_t, qh)
        dinter = _each(lambda qc, dn_, qn, dd, w: (jnp.sum(qc * dn_, axis=1, keepdims=True) + qn * dd) * w,
                       f["qc"], dnum, f["qn"], dden, winter)
        wq = _each(lambda w, a: w * a * QK_SCALE, winter, f["qf"])
        wq_t = _each(lambda w, a: bf(w * a.astype(F32) * QK_SCALE), winter_row, qth)
        dc_loc = _each(mm, wq_t, dnb)
        dct_loc = _each(lambda a, b: mm(a, bf(b)), dnt, wq)
        dn_loc = _each(lambda a, d: jnp.sum(a * d, axis=0, keepdims=True), wq, dden)
        cs_q = _each(lambda a, b: jnp.sum(a * b, axis=1, keepdims=True), p_t, rmat_t)
        db = _each(lambda a, b, di, cs: jnp.sum(a * b, axis=1, keepdims=True) + di - cs, p, rmat, dinter, cs_q)
        ddecay = _each(lambda dc_, c, dn_, n: jnp.sum(jnp.sum(dc_ * c, axis=1, keepdims=True), axis=0, keepdims=True)
                       + jnp.sum(dn_ * n, axis=1, keepdims=True), dcn, c_in, dnn, n_in)
        dkw = _each(lambda a, b, n: mm(a, bf(b)) + n, vh, dcn_t, dnn)
        dk = _each(lambda a, w, b: a + w * b, dk, f["wkf"], dkw)
        da = _each(lambda a, kk, w: jnp.sum(a * kk.astype(F32), axis=1, keepdims=True) * w, dkw, kh, f["wkf"])
        dv = _each(lambda a, kw, dc_: a + mm(bf(kw), bf(dc_)), dv, f["kw"], dcn)
        dgtot = _each(lambda a, dd, d: jnp.sum(a, axis=0, keepdims=True) + dd * d, da, ddecay, decay)
        db = _each(lambda a, b, g: a - b + jnp.where(row == CHUNK - 1, g, 0.0), db, da, dgtot)
        dli = _each(lambda a, b: a + b, cs_q, da)
        dc_new = _each(lambda d, a, b: d * a + b, decay, dcn, dc_loc)
        dct_new = _each(lambda d, a, b: d * a + b, decay, dcn_t, dct_loc)
        dn_new = _each(lambda d, a, b: d * a + b, decay, dnn, dn_loc)
        dx = jnp.zeros((CHUNK, 128), F32)
        for h in HEADS:
            dhn_ref[:, vsl[h]] += dhn[h]
            dp_ref[:, ksl[h]] = bf(dqs[h] * QK_SCALE)
            dp_ref[:, ML_QK + ML_DK * h:ML_QK + ML_DK * (h + 1)] = bf(dk[h])
            dp_ref[:, 2 * ML_QK + ML_DV * h:2 * ML_QK + ML_DV * (h + 1)] = bf(dv[h])
            dp_ref[:, 2 * ML_QK + ML_V + ML_DV * h:2 * ML_QK + ML_V + ML_DV * (h + 1)] = bf(do[h])
            dx = jnp.where(lane == h, dli[h], dx)
            dx = jnp.where(lane == ML_HEADS + h, db[h], dx)
        dlf = _cumsum_rows(dx, reverse=True)
        dact = jnp.where(lane < ML_HEADS, dx, dlf * _sigmoid(-act))
        dz = dact * (1.0 - th * th)
        dp_ref[:, 2 * ML_QK + 2 * ML_V:] = dz.astype(BF16)
        dbif_ref[...] += jnp.sum(dz, axis=0, keepdims=True)
        return dc_new, dct_new, dn_new

    def body(q_ref, k_ref, v_ref, o_ref, gt_ref, bif_ref, hn_ref, hs_ref, cst_ref, nst_ref, mst_ref, dy_ref,
             dp_ref, dhn_ref, dbif_ref, dc_sc, dct_sc, dn_sc):
        @pl.when(pl.program_id(0) == 0)
        def _():
            dc_sc[...] = jnp.zeros_like(dc_sc)
            dct_sc[...] = jnp.zeros_like(dct_sc)
            dn_sc[...] = jnp.zeros_like(dn_sc)
            dhn_ref[...] = jnp.zeros_like(dhn_ref)
            dbif_ref[...] = jnp.zeros_like(dbif_ref)

        dc_all, dct_all, dn_all = dc_sc[...], dct_sc[...], dn_sc[...]
        carried = ([dc_all[ML_DK * h:ML_DK * (h + 1), :] for h in HEADS],
                   [dct_all[ML_DV * h:ML_DV * (h + 1), :] for h in HEADS],
                   [dn_all[h:h + 1, :] for h in HEADS])
        for sub in reversed(range(ML_STEP_CHUNKS)):
            rs = pl.ds(CHUNK * sub, CHUNK)
            one = pl.ds(sub, 1)
            carried = chunk(q_ref.at[rs], k_ref.at[rs], v_ref.at[rs], o_ref.at[rs], gt_ref.at[rs], bif_ref, hn_ref,
                            hs_ref.at[rs], cst_ref.at[one], nst_ref.at[one], mst_ref.at[one], dy_ref.at[rs],
                            dp_ref.at[rs], dhn_ref, dbif_ref, carried)
        for h in HEADS:
            dc_sc[ML_DK * h:ML_DK * (h + 1), :] = carried[0][h]
            dct_sc[ML_DV * h:ML_DV * (h + 1), :] = carried[1][h]
            dn_sc[h:h + 1, :] = carried[2][h]

    rev = lambda i: (steps - 1 - i, 0)
    rev3 = lambda i: (steps - 1 - i, 0, 0)
    return _side_call(
        body, side, name="mlstm_bwd", steps=steps,
        in_specs=_ml_specs(steps, True) + [
            _full((1, 128)), _full((1, ML_V)), pl.BlockSpec((ML_STEP, ML_V), rev),
            pl.BlockSpec((ML_STEP_CHUNKS, ML_HEADS * ML_DK, ML_DV), rev3),
            pl.BlockSpec((ML_STEP_CHUNKS, ML_HEADS, ML_DK), rev3),
            pl.BlockSpec((ML_STEP_CHUNKS, ML_HEADS, 128), rev3), pl.BlockSpec((ML_STEP, ML_V), rev)],
        out_specs=[pl.BlockSpec((ML_STEP, ML_IN_PAD), rev), _full((1, ML_V)), _full((1, 128))],
        out_shape=[_sds((t, ML_IN_PAD), BF16), _sds((1, ML_V), F32), _sds((1, 128), F32)],
        scratch_shapes=[pltpu.VMEM((ML_HEADS * ML_DK, ML_DV), F32), pltpu.VMEM((ML_HEADS * ML_DV, ML_DK), F32),
                        pltpu.VMEM((ML_HEADS, ML_DK), F32)],
        args=(qkv, qkv, qkv, og, og, bif, hn, hs, cst, nst, mst, dy))


LRU_TM = 256
GELU_K = math.sqrt(2.0 / math.pi)
GELU_C = 0.044715


def _gelu(x):
    th = jnp.tanh(GELU_K * (x + GELU_C * x * x * x))
    return 0.5 * x * (1.0 + th), th


def _neg_expm1(x):
    series = -x * (1.0 + x * (0.5 + x * (1.0 / 6.0 + x * (1.0 / 24.0))))
    return jnp.where(x > -0.05, series, 1.0 - jnp.exp(x))


def _block_diag_dot(a, w_ref, dims):
    parts = [lax.dot_general(a[:, LRU_BLOCK * n:LRU_BLOCK * (n + 1)], w_ref[n], dims, preferred_element_type=F32)
             for n in range(LRU_BLOCKS)]
    return jnp.concatenate(parts, axis=1)


def _lru_gates(r, lam):
    ls = _log_sigmoid(lam)
    la = LRU_C * r * ls
    a = jnp.exp(la)
    em = _neg_expm1(2.0 * la)
    mult = jnp.sqrt(em)
    return ls, a, em, mult


def _lru_fwd(proj, cw, cb, wa, ba, wx, bx, lam, side):
    t = proj.shape[0]
    w = D_MODEL
    tm = min(LRU_TM, t)

    def body(gb_ref, up_ref, cw_ref, cb_ref, wa_ref, ba_ref, wx_ref, bx_ref, lam_ref,
             y_ref, u_ref, r_ref, i_ref, h_ref, tail_sc, hprev_sc):
        @pl.when(pl.program_id(0) == 0)
        def _():
            tail_sc[...] = jnp.zeros_like(tail_sc)
            hprev_sc[...] = jnp.zeros_like(hprev_sc)

        up = up_ref[...]
        ext = jnp.concatenate([tail_sc[...], up], axis=0)
        u = cb_ref[...] + cw_ref[CONV_WIDTH - 1:CONV_WIDTH, :] * up
        for s in range(1, CONV_WIDTH):
            u = u + cw_ref[CONV_WIDTH - 1 - s:CONV_WIDTH - s, :] * pltpu.roll(ext, s, 0)[8:8 + tm]
        tail_sc[...] = up[tm - 8:tm]
        ub = u.astype(BF16)
        r = _sigmoid(_block_diag_dot(ub, wa_ref, (((1,), (0,)), ((), ()))) + ba_ref[...])
        ig = _sigmoid(_block_diag_dot(ub, wx_ref, (((1,), (0,)), ((), ()))) + bx_ref[...])
        _, a, _, mult = _lru_gates(r, lam_ref[...])
        h = _scan_rows(a, mult * ig * u, hprev_sc[0:1, :])
        hprev_sc[0:1, :] = h[tm - 1:tm]
        u_ref[...] = u
        r_ref[...] = r
        i_ref[...] = ig
        h_ref[...] = h
        gel, _ = _gelu(gb_ref[...])
        y_ref[...] = (h * gel).astype(BF16)

    vec = _full((1, w))
    wspec = _full((LRU_BLOCKS, LRU_BLOCK, LRU_BLOCK))
    return _side_call(
        body, side, name="lru_fwd", steps=t // tm,
        in_specs=[pl.BlockSpec((tm, w), lambda i: (i, 0)), pl.BlockSpec((tm, w), lambda i: (i, 1)),
                  _full((CONV_WIDTH, w)), vec, wspec, vec, wspec, vec, vec],
        out_specs=[_rows(tm, w)] * 5,
        out_shape=[_sds((t, w), BF16)] + [_sds((t, w), F32)] * 4,
        scratch_shapes=[pltpu.VMEM((8, w), F32), pltpu.VMEM((8, w), F32)],
        args=(proj, proj, cw, cb, wa, ba, wx, bx, lam))


def _lru_bwd(proj, cw, wa, wx, lam, u, r, ig, h, dy, side):
    t = proj.shape[0]
    w = D_MODEL
    tm = min(LRU_TM, t)
    nt = t // tm

    def body(gb_ref, up_ref, cw_ref, wa_ref, wx_ref, lam_ref, u_ref, r_ref, i_ref, h_ref, hp_ref, dy_ref,
             dp_ref, dcw_ref, dcb_ref, dwa_ref, dba_ref, dwx_ref, dbx_ref, dlam_ref, carry_sc, dutail_sc, dls_sc):
        step = pl.program_id(0)

        @pl.when(step == 0)
        def _():
            carry_sc[...] = jnp.zeros_like(carry_sc)
            dutail_sc[...] = jnp.zeros_like(dutail_sc)
            dls_sc[...] = jnp.zeros_like(dls_sc)
            for ref in (dcw_ref, dcb_ref, dwa_ref, dba_ref, dwx_ref, dbx_ref):
                ref[...] = jnp.zeros_like(ref)

        row = lax.broadcasted_iota(jnp.int32, (tm, w), 0)
        u_t, r_t, i_t, h_t = u_ref[...], r_ref[...], i_ref[...], h_ref[...]
        ls, a, em, mult = _lru_gates(r_t, lam_ref[...])
        gb = gb_ref[...]
        gel, th = _gelu(gb)
        dyv = dy_ref[...]
        dgb = dyv * h_t * (0.5 * (1.0 + th) + 0.5 * gb * (1.0 - th * th) * GELU_K * (1.0 + 3.0 * GELU_C * gb * gb))
        a_next = jnp.where(row < tm - 1, pltpu.roll(a, tm - 1, 0), 1.0)
        g = _scan_rows(a_next, dyv * gel, carry_sc[0:1, :], reverse=True)
        carry_sc[0:1, :] = a[0:1] * g[0:1]
        has_prev = jnp.where(step == nt - 1, 0.0, 1.0)
        h_prev = jnp.where(row >= 1, pltpu.roll(h_t, 1, 0), hp_ref[7:8, :] * has_prev)
        dmult = g * i_t * u_t
        dig = g * mult * u_t
        du = g * mult * i_t
        dla = g * h_prev * a - dmult * (1.0 - em) / mult
        dls_sc[0:1, :] += jnp.sum(dla * r_t, axis=0, keepdims=True) * LRU_C
        dpa = dla * (LRU_C * ls) * r_t * (1.0 - r_t)
        dpx = dig * i_t * (1.0 - i_t)
        dba_ref[...] += jnp.sum(dpa, axis=0, keepdims=True)
        dbx_ref[...] += jnp.sum(dpx, axis=0, keepdims=True)
        ub = u_t.astype(BF16)
        dpab = dpa.astype(BF16)
        dpxb = dpx.astype(BF16)
        for n in range(LRU_BLOCKS):
            cs = slice(LRU_BLOCK * n, LRU_BLOCK * (n + 1))
            dwa_ref[n] += lax.dot_general(ub[:, cs], dpab[:, cs], TN_DIMS, preferred_element_type=F32)
            dwx_ref[n] += lax.dot_general(ub[:, cs], dpxb[:, cs], TN_DIMS, preferred_element_type=F32)
        du = du + _block_diag_dot(dpab, wa_ref, NT_DIMS) + _block_diag_dot(dpxb, wx_ref, NT_DIMS)
        dcb_ref[...] += jnp.sum(du, axis=0, keepdims=True)
        ext = jnp.concatenate([du, dutail_sc[...]], axis=0)
        up = up_ref[...]
        dup = cw_ref[CONV_WIDTH - 1:CONV_WIDTH, :] * du
        dcw_ref[CONV_WIDTH - 1:CONV_WIDTH, :] += jnp.sum(up * du, axis=0, keepdims=True)
        for s in range(1, CONV_WIDTH):
            du_s = pltpu.roll(ext, tm + 8 - s, 0)[0:tm]
            dup = dup + cw_ref[CONV_WIDTH - 1 - s:CONV_WIDTH - s, :] * du_s
            dcw_ref[CONV_WIDTH - 1 - s:CONV_WIDTH - s, :] += jnp.sum(up * du_s, axis=0, keepdims=True)
        dutail_sc[...] = du[0:8]
        dp_ref[:, :w] = dgb.astype(BF16)
        dp_ref[:, w:] = dup.astype(BF16)

        @pl.when(step == nt - 1)
        def _():
            dlam_ref[...] = dls_sc[0:1, :] * _sigmoid(-lam_ref[...])

    rev = lambda col: (lambda i: (nt - 1 - i, col))
    vec = _full((1, w))
    wspec = _full((LRU_BLOCKS, LRU_BLOCK, LRU_BLOCK))
    tile = pl.BlockSpec((tm, w), rev(0))
    prev8 = pl.BlockSpec((8, w), lambda i: (jnp.maximum((nt - 1 - i) * (tm // 8) - 1, 0), 0))
    return _side_call(
        body, side, name="lru_bwd", steps=nt,
        in_specs=[tile, pl.BlockSpec((tm, w), rev(1)), _full((CONV_WIDTH, w)), wspec, wspec, vec,
                  tile, tile, tile, tile, prev8, tile],
        out_specs=[pl.BlockSpec((tm, 2 * w), rev(0)), _full((CONV_WIDTH, w)), vec, wspec, vec, wspec, vec, vec],
        out_shape=[_sds((t, 2 * w), BF16), _sds((CONV_WIDTH, w), F32), _sds((1, w), F32),
                   _sds((LRU_BLOCKS, LRU_BLOCK, LRU_BLOCK), F32), _sds((1, w), F32),
                   _sds((LRU_BLOCKS, LRU_BLOCK, LRU_BLOCK), F32), _sds((1, w), F32), _sds((1, w), F32)],
        scratch_shapes=[pltpu.VMEM((8, w), F32), pltpu.VMEM((8, w), F32), pltpu.VMEM((8, w), F32)],
        args=(proj, proj, cw, wa, wx, lam, u, r, ig, h, h, dy))


LANES = 1024
HALF_FFN = D_FF // N_CHIPS
GROUPS = {
    "ml": (("ml_w_in", 1024, ML_IN // N_CHIPS), ("ml_w_out", 256, 1024)),
    "lru": (("lru_w_in", 1024, 512),
            ("lru_gates", 2 * LRU_BLOCKS * 64, LRU_BLOCK),
            ("lru_w_out", 256, 1024)),
    "ffn0": (("ffn_g0", HALF_FFN, 1024), ("ffn_u0", HALF_FFN, 1024), ("ffn_down0", HALF_FFN, 1024)),
    "ffn1": (("ffn_g1", HALF_FFN, 1024), ("ffn_u1", HALF_FFN, 1024), ("ffn_down1", HALF_FFN, 1024)),
}

SMALL_ROWS = 24
ROW_LOSS, ROW_HEAD_NORM, ROW_B_IF, ROW_LRU = 8, 9, 10, 11


def _row_tile(rows, cols, itemsize, budget=3 << 19):
    best = 16
    for t in range(16, rows + 1, 16):
        if rows % t == 0 and t * cols * itemsize <= budget:
            best = t
    return best


def _chip_peers():
    x, y, c = lax.axis_index("x"), lax.axis_index("y"), lax.axis_index("c")
    return x, y, c, [(1 - x, y), (x, 1 - y), (1 - x, 1 - y)]


HBM_SPEC = pl.BlockSpec(memory_space=pltpu.HBM)


def _remote(src, dst, send_sems, recv_sems, k, to):
    return pltpu.make_async_remote_copy(src_ref=src, dst_ref=dst, send_sem=send_sems.at[k], recv_sem=recv_sems.at[k],
                                        device_id=to, device_id_type=MESH)


GATHER_SEMS = 7


def _gather_copies(kind, specs, part_refs, out_refs, send_sems, recv_sems):
    x, y, c, chips = _chip_peers()
    me = 2 * x + y
    sib = (x, y, 1 - c)
    copy = functools.partial(_remote, send_sems=send_sems, recv_sems=recv_sems)
    out = []
    for p, (_, rows, _) in enumerate(specs):
        mine = pl.ds(c * (rows // 2), rows // 2)
        theirs = pl.ds((1 - c) * (rows // 2), rows // 2)
        base = GATHER_SEMS * p
        for j, (cx, cy) in enumerate(chips):
            land = out_refs[p].at[2 * cx + cy, mine]
            other = out_refs[p].at[2 * cx + cy, theirs]
            if kind == "first":
                out.append(copy(part_refs[p].at[mine], out_refs[p].at[me, mine], k=base + j, to=(cx, cy, c)))
            elif kind == "landed":
                out.append(copy(land, land, k=base + j, to=sib))
            elif kind == "forward":
                out.append(copy(land, land, k=base + 3 + j, to=sib))
            else:
                out.append(copy(other, other, k=base + 3 + j, to=sib))
        if kind in ("first", "arriving"):
            out.append(copy(part_refs[p], out_refs[p].at[me], k=base + 6, to=sib))
    return out


def _gather_start(specs, *refs):
    for cp in _gather_copies("first", specs, *refs):
        cp.start()


def _gather_finish(specs, *refs):
    forwards = _gather_copies("forward", specs, *refs)
    for land, fwd in zip(_gather_copies("landed", specs, *refs), forwards):
        land.wait_recv()
        fwd.start()
    for cp in _gather_copies("arriving", specs, *refs):
        cp.wait_recv()
    for cp in _gather_copies("first", specs, *refs) + forwards:
        cp.wait_send()


def _gather_shapes(specs):
    return [_sds((N_CHIPS, rows, cols), BF16) for _, rows, cols in specs]


def _gather_side(specs, parts):
    return _Side(inputs=list(parts), out_shape=_gather_shapes(specs), n_sems=GATHER_SEMS * len(specs),
                 start=functools.partial(_gather_start, specs), finish=functools.partial(_gather_finish, specs))


def _gather_weights(specs, parts, small):
    n = len(specs)

    def body(*refs):
        part_refs, small_ref = refs[:n], refs[n]
        out_refs, outs_ref = refs[n + 1:2 * n + 1], refs[2 * n + 1]
        send_sems, recv_sems, small_send, small_recv, loc_sem = refs[2 * n + 2:]
        x, y, c, chips = _chip_peers()
        me = 2 * x + y
        local = pltpu.make_async_copy(small_ref, outs_ref.at[me], loc_sem.at[0])
        local.start()
        _gather_start(specs, part_refs, out_refs, send_sems, recv_sems)
        sent = [_remote(small_ref, outs_ref.at[me], small_send, small_recv, j, (cx, cy, c))
                for j, (cx, cy) in enumerate(chips)]
        for cp in sent:
            cp.start()
        _gather_finish(specs, part_refs, out_refs, send_sems, recv_sems)
        for j, (cx, cy) in enumerate(chips):
            _remote(small_ref, outs_ref.at[2 * cx + cy], small_send, small_recv, j, (cx, cy, c)).wait_recv()
        for cp in sent:
            cp.wait_send()
        local.wait()

    dma = pltpu.SemaphoreType.DMA
    return pl.pallas_call(
        body, name="gather_weights",
        in_specs=[HBM_SPEC] * (n + 1), out_specs=[HBM_SPEC] * (n + 1),
        out_shape=_gather_shapes(specs) + [_sds((N_CHIPS,) + small.shape, small.dtype)],
        scratch_shapes=[dma((GATHER_SEMS * n,)), dma((GATHER_SEMS * n,)), dma((3,)), dma((3,)), dma((1,))],
    )(*parts, small)


def _exchange_copies(specs, g_refs, out_refs, send_sems, recv_sems):
    x, y, c, _ = _chip_peers()
    return [_remote(g_refs[p].at[:, pl.ds((1 - c) * (rows // 2), rows // 2)], out_refs[p], send_sems, recv_sems, p,
                    (x, y, 1 - c)) for p, (_, rows, _) in enumerate(specs)]


def _exchange_start(specs, *refs):
    for cp in _exchange_copies(specs, *refs):
        cp.start()


def _exchange_finish(specs, *refs):
    for cp in _exchange_copies(specs, *refs):
        cp.wait()


def _exchange_shapes(specs):
    return [_sds((N_CHIPS, rows // 2, cols), BF16) for _, rows, cols in specs]


def _exchange_side(specs, gparts):
    return _Side(inputs=list(gparts), out_shape=_exchange_shapes(specs), n_sems=len(specs),
                 start=functools.partial(_exchange_start, specs), finish=functools.partial(_exchange_finish, specs))


def _exchange_halves(specs, gparts, name):
    n = len(specs)

    def body(*refs):
        _exchange_start(specs, refs[:n], refs[n:2 * n], *refs[2 * n:])
        _exchange_finish(specs, refs[:n], refs[n:2 * n], *refs[2 * n:])

    return pl.pallas_call(
        body, name=name, in_specs=[HBM_SPEC] * n, out_specs=[HBM_SPEC] * n, out_shape=_exchange_shapes(specs),
        scratch_shapes=[pltpu.SemaphoreType.DMA((n,)), pltpu.SemaphoreType.DMA((n,))],
    )(*gparts)


def _add_halves(g, recv, pos, name):
    _, half, cols = recv.shape
    tr = _row_tile(half, cols, 2)
    tiles = half // tr

    def body(pos_ref, a_ref, b_ref, o_ref):
        o_ref[...] = (a_ref[...].astype(F32) + b_ref[...].astype(F32)).astype(BF16)

    spec = pl.BlockSpec((1, tr, cols), lambda k, i, pos_ref: (k, i, 0))
    return pl.pallas_call(
        body, name=name,
        grid_spec=pltpu.PrefetchScalarGridSpec(
            num_scalar_prefetch=1, grid=(N_CHIPS, tiles),
            in_specs=[pl.BlockSpec((1, tr, cols), lambda k, i, pos_ref: (k, pos_ref[0] * tiles + i, 0)), spec],
            out_specs=spec),
        out_shape=_sds((N_CHIPS, half, cols), BF16),
        compiler_params=_params("parallel", "parallel"),
    )(pos, g, recv)


def _scatter_copies(n, s1_refs, recv_refs, send_sems, recv_sems):
    x, y, c, chips = _chip_peers()
    return [_remote(s1_refs[p].at[2 * cx + cy], recv_refs[p].at[j], send_sems, recv_sems, 3 * p + j, (cx, cy, c))
            for p in range(n) for j, (cx, cy) in enumerate(chips)]


def _scatter_start(n, s1_refs, recv_refs, send_sems, recv_sems):
    for cp in _scatter_copies(n, s1_refs, recv_refs, send_sems, recv_sems):
        cp.start()


def _scatter_finish(n, s1_refs, recv_refs, send_sems, recv_sems):
    for cp in _scatter_copies(n, s1_refs, recv_refs, send_sems, recv_sems):
        cp.wait()


def _scatter_shapes(s1):
    return [_sds((3,) + a.shape[1:], a.dtype) for a in s1]


def _scatter_side(s1):
    n = len(s1)
    return _Side(inputs=list(s1), out_shape=_scatter_shapes(s1), n_sems=3 * n,
                 start=functools.partial(_scatter_start, n), finish=functools.partial(_scatter_finish, n))


def _sum_chips(s1, recv, pos, name):
    _, half, cols = recv.shape
    tr = _row_tile(half, cols, 4)
    tiles = half // tr

    def body(pos_ref, a_ref, b_ref, o_ref):
        acc = a_ref[0].astype(F32)
        for j in range(3):
            acc = acc + b_ref[j].astype(F32)
        o_ref[...] = acc

    return pl.pallas_call(
        body, name=name,
        grid_spec=pltpu.PrefetchScalarGridSpec(
            num_scalar_prefetch=1, grid=(tiles,),
            in_specs=[pl.BlockSpec((1, tr, cols), lambda i, pos_ref: (pos_ref[1], i, 0)),
                      pl.BlockSpec((3, tr, cols), lambda i, pos_ref: (0, i, 0))],
            out_specs=pl.BlockSpec((tr, cols), lambda i, pos_ref: (pos_ref[0] * tiles + i, 0))),
        out_shape=_sds((2 * half, cols), F32),
        compiler_params=_params("parallel"),
    )(pos, s1, recv)


def _sum_small(small_all):
    def body(a_ref, o_ref):
        acc = a_ref[0]
        for d in range(1, 8):
            acc = acc + a_ref[d]
        o_ref[...] = acc

    return pl.pallas_call(body, name="sum_small", out_shape=_sds(small_all.shape[1:], F32))(small_all)


def _join_halves(specs, s2, small):
    n = len(specs)
    flips = [(fx, fy, fc) for fx in (0, 1) for fy in (0, 1) for fc in (0, 1)][1:]

    def body(*refs):
        small_ref, all_ref = refs[n], refs[2 * n + 1]
        buf_refs = refs[n + 1:2 * n + 1]
        send_sems, recv_sems, small_send, small_recv, loc_sem = refs[2 * n + 2:]
        x, y, c, _ = _chip_peers()
        my_slot = all_ref.at[4 * x + 2 * y + c]
        local = pltpu.make_async_copy(small_ref, my_slot, loc_sem.at[0])
        local.start()
        sent = []
        for p, (_, rows, _) in enumerate(specs):
            mine = buf_refs[p].at[pl.ds(c * (rows // 2), rows // 2)]
            sent.append(_remote(mine, mine, send_sems, recv_sems, p, (x, y, 1 - c)))
        peers = [(1 - x if fx else x, 1 - y if fy else y, 1 - c if fc else c) for fx, fy, fc in flips]
        sent += [_remote(small_ref, my_slot, small_send, small_recv, i, p) for i, p in enumerate(peers)]
        for cp in sent:
            cp.start()
        for p, (_, rows, _) in enumerate(specs):
            theirs = buf_refs[p].at[pl.ds((1 - c) * (rows // 2), rows // 2)]
            _remote(theirs, theirs, send_sems, recv_sems, p, (x, y, 1 - c)).wait_recv()
        for i, (px, py, pc) in enumerate(peers):
            _remote(small_ref, all_ref.at[4 * px + 2 * py + pc], small_send, small_recv, i, peers[i]).wait_recv()
        for cp in sent:
            cp.wait_send()
        local.wait()

    dma = pltpu.SemaphoreType.DMA
    *joined, small_all = pl.pallas_call(
        body, name="join_halves", in_specs=[HBM_SPEC] * (n + 1), out_specs=[HBM_SPEC] * (n + 1),
        out_shape=[_sds(a.shape, a.dtype) for a in s2] + [_sds((8,) + small.shape, small.dtype)],
        input_output_aliases={p: p for p in range(n)},
        scratch_shapes=[dma((n,)), dma((n,)), dma((7,)), dma((7,)), dma((1,))],
    )(*s2, small)
    return joined, small_all


def _adamw(w, g, m, v, name):
    rows, cols = w.shape[0], w.shape[-1]
    tm = rows
    for cand in (512, 256, 128, 64, 32, 16, 8) if w.ndim == 2 else (rows // 4,):
        if rows % cand == 0 and rows > cand:
            tm = cand
            break

    def body(w_ref, g_ref, m_ref, v_ref, d_ref, nm_ref, nv_ref):
        gv = g_ref[...]
        nm = ADAM_B1 * m_ref[...] + (1.0 - ADAM_B1) * gv
        nv = ADAM_B2 * v_ref[...] + (1.0 - ADAM_B2) * (gv * gv)
        m_hat = nm / (1.0 - ADAM_B1 ** ADAM_STEP)
        v_hat = nv / (1.0 - ADAM_B2 ** ADAM_STEP)
        d_ref[...] = -ADAM_LR * (m_hat / (jnp.sqrt(v_hat) + ADAM_EPS) + ADAM_WD * w_ref[...])
        nm_ref[...] = nm
        nv_ref[...] = nv

    spec = _rows(tm, cols) if w.ndim == 2 else pl.BlockSpec((tm, 1, cols), lambda i: (i, 0, 0))
    return pl.pallas_call(
        body, name=name, grid=(rows // tm,), in_specs=[spec] * 4, out_specs=[spec] * 3,
        out_shape=[_sds(w.shape, F32)] * 3, compiler_params=_params("parallel"),
    )(w, g, m, v)


WEIGHTS = ("ml_w_in", "ml_b_if", "ml_head_norm", "ml_w_out", "lru_w_in", "lru_conv_w", "lru_conv_b", "lru_w_gate_a",
           "lru_b_gate_a", "lru_w_gate_x", "lru_b_gate_x", "lru_lambda", "lru_w_out", "norm_pre_mix", "norm_post_mix",
           "norm_pre_ffn", "norm_post_ffn", "ffn_w_gate", "ffn_w_up", "ffn_w_down")
LRU_VECTORS = ("lru_conv_b", "lru_b_gate_a", "lru_b_gate_x", "lru_lambda")
NORMS = ("norm_pre_mix", "norm_post_mix", "norm_pre_ffn", "norm_post_ffn")


def _by_cols(a):
    return jnp.concatenate([a[k] for k in range(N_CHIPS)], axis=-1)


def _by_chip(a, width):
    return jnp.stack([a[..., k * width:(k + 1) * width] for k in range(N_CHIPS)])


def _weight_shards(w):
    bf = lambda a: a.astype(BF16)
    shards = dict(
        ml_w_in=bf(w["ml_w_in"][0]), ml_w_out=bf(w["ml_w_out"][0]), lru_w_in=bf(w["lru_w_in"][0]),
        lru_gates=bf(jnp.concatenate([w["lru_w_gate_a"][0], w["lru_w_gate_x"][0]], axis=0)).reshape(-1, LRU_BLOCK),
        lru_w_out=bf(w["lru_w_out"][0]))
    gate_t, up_t = bf(jnp.swapaxes(w["ffn_w_gate"], 1, 2)), bf(jnp.swapaxes(w["ffn_w_up"], 1, 2))
    for layer in range(2):
        shards[f"ffn_g{layer}"] = gate_t[layer]
        shards[f"ffn_u{layer}"] = up_t[layer]
        shards[f"ffn_down{layer}"] = bf(w["ffn_w_down"][layer])
    return shards


def _ffn_weights(parts):
    return [a.reshape(D_FF, 1024) for a in parts]


def _ffn_grads(dgu, z, act, dy, layer):
    dws = [_matmul_tn(dgu, z, True, 256, f"ffn{layer}_dw_gate", window=(0, D_FF)),
           _matmul_tn(dgu, z, True, 256, f"ffn{layer}_dw_up", window=(D_FF, D_FF)),
           _matmul_tn(act, dy, True, 256, f"ffn{layer}_dw_down")]
    return [a.reshape(N_CHIPS, HALF_FFN, 1024) for a in dws]


def _sum_group(group, s1, recv, pos):
    return [_sum_chips(a, r, pos, "sum_chips_" + n) for (n, _, _), a, r in zip(GROUPS[group], s1, recv)]


def _update(w, grads, m, v):
    delta, new_m, new_v = {}, {}, {}
    for n in WEIGHTS:
        if n == "ml_w_in":
            view = lambda a: jnp.transpose(a, (2, 0, 1))
            back = lambda a: jnp.transpose(a, (1, 2, 0))
        else:
            flip = (lambda a: jnp.swapaxes(a, 1, 2)) if n in ("ffn_w_gate", "ffn_w_up") else (lambda a: a)
            shape = flip(w[n]).shape
            view = lambda a: flip(a).reshape(-1, shape[-1])
            back = lambda a: flip(a.reshape(shape))
        d, nm, nv = _adamw(view(w[n]), view(grads[n]), view(m[n]), view(v[n]), "adamw_" + n)
        delta[n], new_m[n], new_v[n] = back(d), back(nm), back(nv)
    return delta, new_m, new_v


def kernel(x, ml_w_in, ml_b_if, ml_head_norm, ml_w_out, lru_w_in, lru_conv_w, lru_conv_b, lru_w_gate_a, lru_b_gate_a, lru_w_gate_x, lru_b_gate_x, lru_lambda, lru_w_out, norm_pre_mix, norm_post_mix, norm_pre_ffn, norm_post_ffn, ffn_w_gate, ffn_w_up, ffn_w_down, loss_target, m_ml_w_in, m_ml_b_if, m_ml_head_norm, m_ml_w_out, m_lru_w_in, m_lru_conv_w, m_lru_conv_b, m_lru_w_gate_a, m_lru_b_gate_a, m_lru_w_gate_x, m_lru_b_gate_x, m_lru_lambda, m_lru_w_out, m_norm_pre_mix, m_norm_post_mix, m_norm_pre_ffn, m_norm_post_ffn, m_ffn_w_gate, m_ffn_w_up, m_ffn_w_down, v_ml_w_in, v_ml_b_if, v_ml_head_norm, v_ml_w_out, v_lru_w_in, v_lru_conv_w, v_lru_conv_b, v_lru_w_gate_a, v_lru_b_gate_a, v_lru_w_gate_x, v_lru_b_gate_x, v_lru_lambda, v_lru_w_out, v_norm_pre_mix, v_norm_post_mix, v_norm_pre_ffn, v_norm_post_ffn, v_ffn_w_gate, v_ffn_w_up, v_ffn_w_down):
    args = locals()
    w = {n: args[n] for n in WEIGHTS}
    m = {n: args["m_" + n] for n in WEIGHTS}
    v = {n: args["v_" + n] for n in WEIGHTS}
    xs, target = x[0], loss_target[0]
    mx, my, mc, _ = _chip_peers()
    chip = 2 * mx + my
    pos = jnp.stack([mc, chip])
    row = lambda a, i: a[i:i + 1]
    npm, nqm, npf, nqf = (w[n] for n in NORMS)
    shards = _weight_shards(w)
    of = lambda group: [shards[n] for n, _, _ in GROUPS[group]]
    bif = jnp.pad(w["ml_b_if"], ((0, 0), (0, 128 - 2 * ML_HEADS)))
    hn = w["ml_head_norm"]

    small = jnp.concatenate([w["lru_conv_w"][0]] + [w[n] for n in LRU_VECTORS], axis=0)
    ml_in_spec, ml_out_spec = GROUPS["ml"][:1], GROUPS["ml"][1:]
    ml_in_parts, smalls = _gather_weights(ml_in_spec, of("ml")[:1], small)
    ml_w_in = jnp.concatenate([ml_in_parts[k] for k in range(N_CHIPS)]
                              + [jnp.zeros((D_MODEL, ML_IN_PAD - ML_IN), BF16)], axis=1)
    vec = _by_cols(smalls)
    lru_spec, ffn1_spec = GROUPS["lru"], GROUPS["ffn1"]
    (z0, qkv, og), got = _norm_matmul(xs, row(npm, 0), ml_w_in, 2 * ML_QK + ML_V, "ml_in",
                                      _gather_side(ml_out_spec, of("ml")[1:]))
    ml_w_out = got[0].reshape(1024, 1024)
    (y0, hs, cst, nst, mst), got = _mlstm_fwd(qkv, og, bif, hn, _gather_side(GROUPS["ffn0"], of("ffn0")))
    wg0, wu0, w_down0 = _ffn_weights(got)
    (ymix0, h1), got = _matmul_postnorm(y0, ml_w_out, row(nqm, 0), xs, "ml_out",
                                        _gather_side(lru_spec[:1], of("lru")[:1]))
    lru_w_in = _by_cols(got[0])
    (zf0, gu0, yf0, h2), got = _ffn_forward(
        h1, row(npf, 0), wg0, wu0, w_down0, row(nqf, 0), None, "ffn0_fwd",
        _gather_side(lru_spec[1:] + ffn1_spec[:2], of("lru")[1:] + of("ffn1")[:2]))
    gates = got[0].reshape(N_CHIPS, 2, LRU_BLOCKS, 64, LRU_BLOCK).transpose(1, 2, 0, 3, 4)
    gates = gates.reshape(2, LRU_BLOCKS, LRU_BLOCK, LRU_BLOCK)
    lru_w_out = got[1].reshape(1024, 1024)
    ffn1_gu = got[2:4]
    (z1, proj1), _ = _norm_matmul(h2, row(npm, 1), lru_w_in, 0, "lru_in", None)
    (y1, u, r, ig, hl), got = _lru_fwd(proj1, vec[0:4], vec[4:5], gates[0], vec[5:6], gates[1], vec[6:7], vec[7:8],
                                       _gather_side(ffn1_spec[2:], of("ffn1")[2:]))
    wg1, wu1, w_down1 = _ffn_weights(ffn1_gu + got)
    (ymix1, h3), _ = _matmul_postnorm(y1, lru_w_out, row(nqm, 1), h2, "lru_out", None)
    (zf1, gu1, yf1, dh4, loss_part), _ = _ffn_forward(
        h3, row(npf, 1), wg1, wu1, w_down1, row(nqf, 1), target, "ffn1_fwd", None)

    add = lambda group, gparts, recv: [_add_halves(g, r, pos, "add_halves_" + n)
                                       for (n, _, _), g, r in zip(GROUPS[group], gparts, recv)]
    (dyf1, dgu1, act1, dh3, dqf1, dpf1), _ = _ffn_backward(
        dh4, yf1, row(nqf, 1), w_down1, gu1, wg1, wu1, h3, row(npf, 1), "ffn1_bwd", None)
    g_ffn1 = _ffn_grads(dgu1, zf1, act1, dyf1, 1)

    (dymix1, dy1, dqm1), recv = _bwd_out(dh3, ymix1, row(nqm, 1), lru_w_out, "lru_bwd_out",
                                         _exchange_side(GROUPS["ffn1"], g_ffn1))
    s1_ffn1 = add("ffn1", g_ffn1, recv)
    dw_lru_out = _matmul_tn(y1, dymix1, False, 256, "lru_dw_out")
    (dproj1, dcw, dcb, dwa, dba, dwx, dbx, dlam), recv_ffn1 = _lru_bwd(
        proj1, vec[0:4], gates[0], gates[1], vec[7:8], u, r, ig, hl, dy1, _scatter_side(s1_ffn1))
    dw_lru_in = _matmul_tn(z1, dproj1, False, 512, "lru_dw_in")
    dgates = jnp.stack([dwa, dwx]).astype(BF16).reshape(2, LRU_BLOCKS, N_CHIPS, 64, LRU_BLOCK)
    dgates = dgates.transpose(2, 0, 1, 3, 4).reshape(N_CHIPS, -1, LRU_BLOCK)
    g_lru = [_by_chip(dw_lru_in, 512), dgates, dw_lru_out.reshape(N_CHIPS, 256, 1024)]
    (dh2, dpm1), recv = _bwd_in(dproj1, [lru_w_in], False, h2, row(npm, 1), dh3, "lru_bwd_in",
                                _exchange_side(GROUPS["lru"], g_lru))
    s1_lru = add("lru", g_lru, recv)

    (dyf0, dgu0, act0, dh1, dqf0, dpf0), recv_lru = _ffn_backward(
        dh2, yf0, row(nqf, 0), w_down0, gu0, wg0, wu0, h1, row(npf, 0), "ffn0_bwd", _scatter_side(s1_lru))
    g_ffn0 = _ffn_grads(dgu0, zf0, act0, dyf0, 0)

    (dymix0, dy0, dqm0), recv = _bwd_out(dh1, ymix0, row(nqm, 0), ml_w_out, "ml_bwd_out",
                                         _exchange_side(GROUPS["ffn0"], g_ffn0))
    s1_ffn0 = add("ffn0", g_ffn0, recv)
    dw_ml_out = _matmul_tn(y0, dymix0, False, 256, "ml_dw_out")
    (dproj0, dhn, dbif), recv_ffn0 = _mlstm_bwd(qkv, og, bif, hn, hs, cst, nst, mst, dy0, _scatter_side(s1_ffn0))
    dw_ml_in = _matmul_tn(z0, dproj0, False, 640, "ml_dw_in")
    g_ml = [_by_chip(dw_ml_in[:, :ML_IN], ML_IN // N_CHIPS), dw_ml_out.reshape(N_CHIPS, 256, 1024)]
    s1_ml = add("ml", g_ml, _exchange_halves(GROUPS["ml"], g_ml, "exchange_halves_ml"))
    (dx, dpm0), recv_ml = _bwd_in(dproj0, [ml_w_in], False, xs, row(npm, 0), dh1, "ml_bwd_in", _scatter_side(s1_ml))

    pad_lanes = lambda a: jnp.pad(a, ((0, 0), (0, LANES - a.shape[1])))
    small = jnp.concatenate(
        [jnp.concatenate([dpm0, dpm1]), jnp.concatenate([dqm0, dqm1]), jnp.concatenate([dpf0, dpf1]),
         jnp.concatenate([dqf0, dqf1]), pad_lanes(loss_part), dhn, pad_lanes(dbif), dcw, dcb, dba, dbx, dlam,
         jnp.zeros((SMALL_ROWS - 19, LANES), F32)], axis=0)
    order = ("ml", "lru", "ffn0", "ffn1")
    s2 = (_sum_group("ml", s1_ml, recv_ml, pos) + _sum_group("lru", s1_lru, recv_lru, pos)
          + _sum_group("ffn0", s1_ffn0, recv_ffn0, pos) + _sum_group("ffn1", s1_ffn1, recv_ffn1, pos))
    specs = sum((GROUPS[g] for g in order), ())
    joined, small_all = _join_halves(specs, s2, small)
    red = dict(zip([n for n, _, _ in specs], joined))
    vsum = _sum_small(small_all)

    dgates = red["lru_gates"].reshape(2, LRU_BLOCKS, 64, LRU_BLOCK)
    cols = lambda a: lax.dynamic_slice_in_dim(a, chip * 256, 256, axis=1)
    grads = dict(
        ml_w_in=red["ml_w_in"], ml_w_out=red["ml_w_out"], lru_w_in=red["lru_w_in"], lru_w_gate_a=dgates[0],
        lru_w_gate_x=dgates[1], lru_w_out=red["lru_w_out"],
        ffn_w_gate=jnp.swapaxes(jnp.stack([red["ffn_g0"], red["ffn_g1"]]), 1, 2),
        ffn_w_up=jnp.swapaxes(jnp.stack([red["ffn_u0"], red["ffn_u1"]]), 1, 2),
        ffn_w_down=jnp.stack([red["ffn_down0"], red["ffn_down1"]]),
        ml_head_norm=vsum[ROW_HEAD_NORM:ROW_HEAD_NORM + 1], ml_b_if=vsum[ROW_B_IF:ROW_B_IF + 1, :2 * ML_HEADS],
        lru_conv_w=cols(vsum[ROW_LRU:ROW_LRU + 4]))
    for i, n in enumerate(NORMS):
        grads[n] = vsum[2 * i:2 * i + 2]
    for i, n in enumerate(LRU_VECTORS):
        grads[n] = cols(vsum[ROW_LRU + 4 + i:ROW_LRU + 5 + i])
    loss = vsum[ROW_LOSS, 0]
    grads = {n: grads[n].reshape(w[n].shape) for n in WEIGHTS}
    delta, new_m, new_v = _update(w, grads, m, v)
    return (loss, dx[None], *[grads[n] for n in WEIGHTS], *[delta[n] for n in WEIGHTS],
            *[new_m[n] for n in WEIGHTS], *[new_v[n] for n in WEIGHTS])
```

```python
import functools
import math
from typing import Callable, NamedTuple

import jax
import jax.numpy as jnp
from jax import lax
from jax.experimental import pallas as pl
from jax.experimental.pallas import tpu as pltpu

F32 = jnp.float32
BF16 = jnp.bfloat16
MESH = pl.DeviceIdType.MESH

D_MODEL = 1024
D_FF = 2816
ML_HEADS = 8
ML_DK = 64
ML_DV = 128
ML_QK = ML_HEADS * ML_DK
ML_V = ML_HEADS * ML_DV
ML_IN = 2 * ML_QK + 2 * ML_V + 2 * ML_HEADS
ML_IN_PAD = 3200
CHUNK = 64
GATE_CAP = 15.0
ML_M_INIT = -1e30
NEG_BIG = -1e30
LRU_BLOCKS = 4
LRU_BLOCK = 256
CONV_WIDTH = 4
LRU_C = 8.0
EPS = 1e-6
QK_SCALE = ML_DK ** -0.5

ADAM_LR = 0.001
ADAM_B1 = 0.9
ADAM_B2 = 0.999
ADAM_EPS = 1e-08
ADAM_WD = 0.01
ADAM_STEP = 10

N_CHIPS = 4
V7X_VMEM_LIMIT = 56 * 1024 * 1024

NT_DIMS = (((1,), (1,)), ((), ()))
TN_DIMS = (((0,), (0,)), ((), ()))


def _params(*semantics):
    return pltpu.CompilerParams(dimension_semantics=semantics, vmem_limit_bytes=V7X_VMEM_LIMIT)


def _sds(shape, dtype):
    return jax.ShapeDtypeStruct(shape, dtype)


def _full(shape):
    return pl.BlockSpec(shape, lambda *_: (0,) * len(shape))


def _rows(tm, n):
    return pl.BlockSpec((tm, n), lambda i: (i, 0))


def _sigmoid(x):
    return 1.0 / (1.0 + jnp.exp(-x))


def _log_sigmoid(x):
    return jnp.minimum(x, 0.0) - jnp.log1p(jnp.exp(-jnp.abs(x)))


def _rms(x):
    r = lax.rsqrt(jnp.mean(x * x, axis=-1, keepdims=True) + EPS)
    return x * r, r


def _rms_bwd(xhat, r, g, dy):
    dxh = dy * g
    return r * (dxh - xhat * jnp.mean(dxh * xhat, axis=-1, keepdims=True))


SUBLANES = 8


def _scan_rows(a, b, carry, reverse=False):
    n, w = a.shape
    groups = n // SUBLANES
    a3, b3 = a.reshape(groups, SUBLANES, w), b.reshape(groups, SUBLANES, w)
    sub = lax.broadcasted_iota(jnp.int32, a3.shape, 1)
    s = 1
    while s < SUBLANES:
        keep = sub < SUBLANES - s if reverse else sub >= s
        shift = SUBLANES - s if reverse else s
        b3 = b3 + a3 * jnp.where(keep, pltpu.roll(b3, shift, 1), 0.0)
        a3 = a3 * jnp.where(keep, pltpu.roll(a3, shift, 1), 1.0)
        s *= 2
    out = [None] * groups
    for g in (reversed(range(groups)) if reverse else range(groups)):
        out[g] = b3[g] + a3[g] * carry
        carry = out[g][0:1] if reverse else out[g][SUBLANES - 1:SUBLANES]
    return jnp.concatenate(out, axis=0)


def _cumsum_rows(x, reverse=False):
    n = x.shape[0]
    row = lax.broadcasted_iota(jnp.int32, x.shape, 0)
    s = 1
    while s < n:
        if reverse:
            x = x + jnp.where(row < n - s, pltpu.roll(x, n - s, 0), 0.0)
        else:
            x = x + jnp.where(row >= s, pltpu.roll(x, s, 0), 0.0)
        s *= 2
    return x


def _norm_matmul(h, g, w, n_bf16, name, side):
    t, d = h.shape
    n = w.shape[1]
    tm = min(512, t)

    def body(h_ref, g_ref, w_ref, z_ref, *o_refs):
        xhat, _ = _rms(h_ref[...])
        z = (xhat * g_ref[...]).astype(BF16)
        z_ref[...] = z
        out = jnp.dot(z, w_ref[...], preferred_element_type=F32)
        if n_bf16:
            o_refs[0][...] = out[:, :n_bf16].astype(BF16)
            o_refs[1][...] = out[:, n_bf16:]
        else:
            o_refs[0][...] = out

    if n_bf16:
        out_specs = [_rows(tm, d), _rows(tm, n_bf16), _rows(tm, n - n_bf16)]
        out_shape = [_sds((t, d), BF16), _sds((t, n_bf16), BF16), _sds((t, n - n_bf16), F32)]
    else:
        out_specs = [_rows(tm, d), _rows(tm, n)]
        out_shape = [_sds((t, d), BF16), _sds((t, n), F32)]
    return _side_call(body, side, name=name, steps=t // tm, in_specs=[_rows(tm, d), _full((1, d)), _full((d, n))],
                      out_specs=out_specs, out_shape=out_shape, scratch_shapes=[], args=(h, g, w))


def _matmul_postnorm(a, w, g, res, name, side):
    t = res.shape[0]
    k, d = w.shape
    tm = min(512, t)

    def body(a_ref, w_ref, g_ref, res_ref, y_ref, o_ref):
        y = jnp.dot(a_ref[...], w_ref[...], preferred_element_type=F32)
        y_ref[...] = y
        yhat, _ = _rms(y)
        o_ref[...] = res_ref[...] + yhat * g_ref[...]

    return _side_call(
        body, side, name=name, steps=t // tm,
        in_specs=[_rows(tm, k), _full((k, d)), _full((1, d)), _rows(tm, d)],
        out_specs=[_rows(tm, d), _rows(tm, d)],
        out_shape=[_sds((t, d), F32), _sds((t, d), F32)],
        scratch_shapes=[], args=(a, w, g, res))


class _Side(NamedTuple):
    inputs: list
    out_shape: list
    n_sems: int
    start: Callable
    finish: Callable


def _side_call(body, side, *, name, steps, in_specs, out_specs, out_shape, scratch_shapes, args):
    n_in, n_out, n_scr = len(in_specs), len(out_specs), len(scratch_shapes)
    if side is None:
        outs = pl.pallas_call(
            body, name=name, grid=(steps,), in_specs=in_specs, out_specs=out_specs, out_shape=out_shape,
            scratch_shapes=scratch_shapes, compiler_params=_params("arbitrary"))(*args)
        return list(outs), []
    s_in, s_out = len(side.inputs), len(side.out_shape)

    def carrying(*refs):
        ins, side_ins = refs[:n_in], refs[n_in:n_in + s_in]
        outs = refs[n_in + s_in:n_in + s_in + n_out]
        side_outs = refs[n_in + s_in + n_out:n_in + s_in + n_out + s_out]
        scratch = refs[n_in + s_in + n_out + s_out:]
        own, sems = scratch[:n_scr], scratch[n_scr:]

        @pl.when(pl.program_id(0) == 0)
        def _():
            side.start(side_ins, side_outs, *sems)

        body(*ins, *outs, *own)

        @pl.when(pl.program_id(0) == steps - 1)
        def _():
            side.finish(side_ins, side_outs, *sems)

    outs = pl.pallas_call(
        carrying, name=name, grid=(steps,), in_specs=list(in_specs) + [HBM_SPEC] * s_in,
        out_specs=list(out_specs) + [HBM_SPEC] * s_out, out_shape=list(out_shape) + list(side.out_shape),
        scratch_shapes=list(scratch_shapes) + [pltpu.SemaphoreType.DMA((side.n_sems,))] * 2,
        compiler_params=_params("arbitrary"))(*args, *side.inputs)
    return list(outs[:n_out]), list(outs[n_out:])


def _ffn_forward(h, g_pre, wg_t, wu_t, w_down, g_post, target, name, side):
    t, d = h.shape
    k = w_down.shape[0]
    tm = 256

    def body(*refs):
        if target is None:
            h_ref, gpre_ref, wg_ref, wu_ref, wd_ref, gpost_ref, z_ref, gu_ref, y_ref, o_ref = refs
        else:
            h_ref, gpre_ref, wg_ref, wu_ref, wd_ref, gpost_ref, t_ref, z_ref, gu_ref, y_ref, o_ref, l_ref = refs

            @pl.when(pl.program_id(0) == 0)
            def _():
                l_ref[...] = jnp.zeros_like(l_ref)

        hv = h_ref[...]
        xhat, _ = _rms(hv)
        z = (xhat * gpre_ref[...]).astype(BF16)
        z_ref[...] = z
        gate = lax.dot_general(z, wg_ref[...], NT_DIMS, preferred_element_type=F32).astype(BF16)
        up = lax.dot_general(z, wu_ref[...], NT_DIMS, preferred_element_type=F32).astype(BF16)
        gu_ref[:, :k] = gate
        gu_ref[:, k:] = up
        gate = gate.astype(F32)
        act = (gate * _sigmoid(gate) * up.astype(F32)).astype(BF16)
        y = jnp.dot(act, wd_ref[...], preferred_element_type=F32)
        y_ref[...] = y
        yhat, _ = _rms(y)
        out = hv + yhat * gpost_ref[...]
        if target is None:
            o_ref[...] = out
        else:
            err = out - t_ref[...]
            o_ref[...] = err * (1.0 / d)
            part = jnp.sum(jnp.sum(err * err, axis=1, keepdims=True), axis=0, keepdims=True) * (0.5 / d)
            l_ref[...] += jnp.broadcast_to(part, l_ref.shape)

    in_specs = [_rows(tm, d), _full((1, d)), _full((k, d)), _full((k, d)), _full((k, d)), _full((1, d))]
    out_specs = [_rows(tm, d), _rows(tm, 2 * k), _rows(tm, d), _rows(tm, d)]
    out_shape = [_sds((t, d), BF16), _sds((t, 2 * k), BF16), _sds((t, d), F32), _sds((t, d), F32)]
    args = (h, g_pre, wg_t, wu_t, w_down, g_post)
    if target is not None:
        in_specs, args = in_specs + [_rows(tm, d)], args + (target,)
        out_specs, out_shape = out_specs + [_full((1, 128))], out_shape + [_sds((1, 128), F32)]
    return _side_call(body, side, name=name, steps=t // tm, in_specs=in_specs, out_specs=out_specs,
                      out_shape=out_shape, scratch_shapes=[], args=args)


def _bwd_out(dout, y, g, w, name, side):
    t, d = dout.shape
    k = w.shape[0]
    tm = min(512, t)

    def body(dout_ref, y_ref, g_ref, w_ref, dy_ref, da_ref, dg_ref):
        @pl.when(pl.program_id(0) == 0)
        def _():
            dg_ref[...] = jnp.zeros_like(dg_ref)

        do = dout_ref[...]
        yhat, r = _rms(y_ref[...])
        dg_ref[...] += jnp.sum(do * yhat, axis=0, keepdims=True)
        dy = _rms_bwd(yhat, r, g_ref[...], do).astype(BF16)
        dy_ref[...] = dy
        da_ref[...] = lax.dot_general(dy, w_ref[...], NT_DIMS, preferred_element_type=F32)

    return _side_call(
        body, side, name=name, steps=t // tm,
        in_specs=[_rows(tm, d), _rows(tm, d), _full((1, d)), _full((k, d))],
        out_specs=[_rows(tm, d), _rows(tm, k), _full((1, d))],
        out_shape=[_sds((t, d), BF16), _sds((t, k), F32), _sds((1, d), F32)],
        scratch_shapes=[], args=(dout, y, g, w))


def _ffn_backward(dout, y, g_post, w_down, gu, wg_t, wu_t, h, g_pre, name, side):
    t, d = dout.shape
    k = w_down.shape[0]
    tm = 256

    def body(dout_ref, y_ref, gpost_ref, wd_ref, gu_ref, wg_ref, wu_ref, h_ref, gpre_ref,
             dy_ref, dgu_ref, act_ref, dh_ref, dgpost_ref, dgpre_ref):
        @pl.when(pl.program_id(0) == 0)
        def _():
            dgpost_ref[...] = jnp.zeros_like(dgpost_ref)
            dgpre_ref[...] = jnp.zeros_like(dgpre_ref)

        do = dout_ref[...]
        yhat, r = _rms(y_ref[...])
        dgpost_ref[...] += jnp.sum(do * yhat, axis=0, keepdims=True)
        dy = _rms_bwd(yhat, r, gpost_ref[...], do).astype(BF16)
        dy_ref[...] = dy
        da = lax.dot_general(dy, wd_ref[...], NT_DIMS, preferred_element_type=F32)
        gate = gu_ref[:, :k].astype(F32)
        up = gu_ref[:, k:].astype(F32)
        sg = _sigmoid(gate)
        silu = gate * sg
        act_ref[...] = (silu * up).astype(BF16)
        dgate = (da * up * (sg * (1.0 + gate * (1.0 - sg)))).astype(BF16)
        dup = (da * silu).astype(BF16)
        dgu_ref[:, :k] = dgate
        dgu_ref[:, k:] = dup
        dz = (jnp.dot(dgate, wg_ref[...], preferred_element_type=F32)
              + jnp.dot(dup, wu_ref[...], preferred_element_type=F32))
        hhat, r2 = _rms(h_ref[...])
        dgpre_ref[...] += jnp.sum(dz * hhat, axis=0, keepdims=True)
        dh_ref[...] = do + _rms_bwd(hhat, r2, gpre_ref[...], dz)

    vec = _full((1, d))
    wspec = _full((k, d))
    return _side_call(
        body, side, name=name, steps=t // tm,
        in_specs=[_rows(tm, d), _rows(tm, d), vec, wspec, _rows(tm, 2 * k), wspec, wspec, _rows(tm, d), vec],
        out_specs=[_rows(tm, d), _rows(tm, 2 * k), _rows(tm, k), _rows(tm, d), vec, vec],
        out_shape=[_sds((t, d), BF16), _sds((t, 2 * k), BF16), _sds((t, k), BF16), _sds((t, d), F32),
                   _sds((1, d), F32), _sds((1, d), F32)],
        scratch_shapes=[], args=(dout, y, g_post, w_down, gu, wg_t, wu_t, h, g_pre))


def _bwd_in(dp, ws, transposed, h, g, dout, name, side):
    t, d = h.shape
    n = dp.shape[1]
    tm = min(512, t)
    widths = [w.shape[0] if transposed else w.shape[1] for w in ws]

    def body(dp_ref, *refs):
        w_refs = refs[:len(ws)]
        h_ref, g_ref, dout_ref, dh_ref, dg_ref = refs[len(ws):]

        @pl.when(pl.program_id(0) == 0)
        def _():
            dg_ref[...] = jnp.zeros_like(dg_ref)

        dz, at = None, 0
        for w_ref, width in zip(w_refs, widths):
            block = dp_ref[:, at:at + width]
            if transposed:
                part = jnp.dot(block, w_ref[...], preferred_element_type=F32)
            else:
                part = lax.dot_general(block, w_ref[...], NT_DIMS, preferred_element_type=F32)
            dz = part if dz is None else dz + part
            at += width
        hhat, r = _rms(h_ref[...])
        dg_ref[...] += jnp.sum(dz * hhat, axis=0, keepdims=True)
        dh_ref[...] = dout_ref[...] + _rms_bwd(hhat, r, g_ref[...], dz)

    return _side_call(
        body, side, name=name, steps=t // tm,
        in_specs=[_rows(tm, n)] + [_full(w.shape) for w in ws] + [_rows(tm, d), _full((1, d)), _rows(tm, d)],
        out_specs=[_rows(tm, d), _full((1, d))],
        out_shape=[_sds((t, d), F32), _sds((1, d), F32)],
        scratch_shapes=[], args=(dp, *ws, h, g, dout))


def _matmul_tn(a, b, tile_a, tile, name, window=None):
    t, ka = a.shape
    nb = b.shape[1]
    col0 = 0
    if window is not None:
        col0, ka = window

    def body(a_ref, b_ref, o_ref):
        o_ref[...] = lax.dot_general(a_ref[...], b_ref[...], TN_DIMS, preferred_element_type=F32).astype(BF16)

    if tile_a:
        grid = (ka // tile,)
        in_specs = [pl.BlockSpec((t, tile), lambda i: (0, col0 // tile + i)), _full((t, nb))]
        out_specs = pl.BlockSpec((tile, nb), lambda i: (i, 0))
    else:
        grid = (nb // tile,)
        in_specs = [_full((t, ka)), pl.BlockSpec((t, tile), lambda i: (0, i))]
        out_specs = pl.BlockSpec((ka, tile), lambda i: (0, i))
    return pl.pallas_call(
        body, name=name, grid=grid, in_specs=in_specs, out_specs=out_specs,
        out_shape=_sds((ka, nb), BF16), compiler_params=_params("parallel"),
    )(a, b)


def _ml_gate_prep(gt, bif):
    th = jnp.tanh((gt + bif) / GATE_CAP)
    act = GATE_CAP * th
    cum = _cumsum_rows(_log_sigmoid(act))
    lane = lax.broadcasted_iota(jnp.int32, gt.shape, 1)
    x = jnp.where(lane < ML_HEADS, act, cum)
    return x, x.T, th, act


HEADS = range(ML_HEADS)


def _each(fn, *per_head):
    return [fn(*a) for a in zip(*per_head)]


def _ml_chunk_fwd(q, k, v, kt, x, xt, c_in, n_in, m_in):
    causal = (lax.broadcasted_iota(jnp.int32, (CHUNK, CHUNK), 0)
              >= lax.broadcasted_iota(jnp.int32, (CHUNK, CHUNK), 1))
    f = {}
    qh = f["qh"] = [q[:, ML_DK * h:ML_DK * (h + 1)] for h in HEADS]
    kh = f["kh"] = [k[:, ML_DK * h:ML_DK * (h + 1)] for h in HEADS]
    f["vh"] = [v[:, ML_DV * h:ML_DV * (h + 1)] for h in HEADS]
    kth = [kt[ML_DK * h:ML_DK * (h + 1), :] for h in HEADS]
    s = _each(lambda a, b: jnp.dot(a, b, preferred_element_type=F32) * QK_SCALE, qh, kth)
    f["qc"] = _each(lambda a, c: jnp.dot(a, c.astype(BF16), preferred_element_type=F32) * QK_SCALE, qh, c_in)
    bcol = f["bcol"] = [x[:, ML_HEADS + h:ML_HEADS + h + 1] for h in HEADS]
    licol = f["licol"] = [x[:, h:h + 1] for h in HEADS]
    brow = f["brow"] = [xt[ML_HEADS + h:ML_HEADS + h + 1, :] for h in HEADS]
    lirow = [xt[h:h + 1, :] for h in HEADS]
    dmat = _each(lambda bc, br, lr: jnp.where(causal, bc - br + lr, NEG_BIG), bcol, brow, lirow)
    inter = _each(lambda bc, m: bc + m, bcol, m_in)
    mt = _each(lambda d, i: jnp.maximum(jnp.max(d, axis=1, keepdims=True), i), dmat, inter)
    wt = f["wt"] = _each(lambda d, m: jnp.exp(d - m), dmat, mt)
    p = f["p"] = _each(lambda a, b: a * b, wt, s)
    winter = f["winter"] = _each(lambda i, m: jnp.exp(i - m), inter, mt)
    qf = f["qf"] = [a.astype(F32) for a in qh]
    qn = f["qn"] = _each(lambda a, n: jnp.sum(a * n, axis=1, keepdims=True) * QK_SCALE, qf, n_in)
    den = f["den"] = _each(lambda a, w, b: jnp.sum(a, axis=1, keepdims=True) + w * b, p, winter, qn)
    emt = f["emt"] = [jnp.exp(-m) for m in mt]
    f["nrm"] = _each(lambda d, e: jnp.maximum(jnp.abs(d), e), den, emt)
    gtot = [bc[CHUNK - 1:CHUNK, :] for bc in bcol]
    a_col = _each(lambda g, bc, lc: g - bc + lc, gtot, bcol, licol)
    a_row = _each(lambda g, br, lr: g - br + lr, gtot, brow, lirow)
    m_new = f["m_new"] = _each(lambda g, m, a: jnp.maximum(g + m, jnp.max(a, axis=1, keepdims=True)),
                               gtot, m_in, a_row)
    f["decay"] = _each(lambda g, m, mn: jnp.exp(g + m - mn), gtot, m_in, m_new)
    wkf = f["wkf"] = _each(lambda a, mn: jnp.exp(a - mn), a_col, m_new)
    f["kw"] = _each(lambda a, w: a.astype(F32) * w, kh, wkf)
    f["ktw"] = _each(lambda a, ar, mn: (a.astype(F32) * jnp.exp(ar - mn)).astype(BF16), kth, a_row, m_new)
    return f


ML_STEP_CHUNKS = 8
ML_STEP = CHUNK * ML_STEP_CHUNKS


def _ml_specs(steps, rev):
    def at(col):
        if rev:
            return lambda i: (steps - 1 - i, col)
        return lambda i: (i, col)

    return [pl.BlockSpec((ML_STEP, ML_QK), at(0)), pl.BlockSpec((ML_STEP, ML_QK), at(1)),
            pl.BlockSpec((ML_STEP, ML_V), at(1)), pl.BlockSpec((ML_STEP, ML_V), at(0)),
            pl.BlockSpec((ML_STEP, 128), at(ML_V // 128))]


def _mlstm_fwd(qkv, og, bif, hn, side):
    t = qkv.shape[0]
    nc = t // CHUNK
    steps = t // ML_STEP

    def body(q_ref, k_ref, v_ref, o_ref, gt_ref, bif_ref, hn_ref, y_ref, hs_ref, cst_ref, nst_ref, mst_ref,
             c_sc, n_sc, m_sc):
        @pl.when(pl.program_id(0) == 0)
        def _():
            c_sc[...] = jnp.zeros_like(c_sc)
            n_sc[...] = jnp.zeros_like(n_sc)
            m_sc[...] = jnp.full_like(m_sc, ML_M_INIT)

        c_all, n_all, m_all = c_sc[...], n_sc[...], m_sc[...]
        c_in = [c_all[ML_DK * h:ML_DK * (h + 1), :] for h in HEADS]
        n_in = [n_all[h:h + 1, :] for h in HEADS]
        m_in = [m_all[h:h + 1, 0:1] for h in HEADS]
        for sub in range(ML_STEP_CHUNKS):
            rs = slice(CHUNK * sub, CHUNK * (sub + 1))
            for h in HEADS:
                cst_ref[sub, ML_DK * h:ML_DK * (h + 1), :] = c_in[h]
                nst_ref[sub, h:h + 1, :] = n_in[h]
                mst_ref[sub, h:h + 1, :] = jnp.broadcast_to(m_in[h], (1, 128))
            x, xt, _, _ = _ml_gate_prep(gt_ref[rs, :], bif_ref[...])
            q, k, v = q_ref[rs, :], k_ref[rs, :], v_ref[rs, :]
            f = _ml_chunk_fwd(q, k, v, k.T, x, xt, c_in, n_in, m_in)
            num = _each(lambda p, vh, w, qc: jnp.dot(p.astype(BF16), vh, preferred_element_type=F32) + w * qc,
                        f["p"], f["vh"], f["winter"], f["qc"])
            hh = _each(lambda a, b: a / b, num, f["nrm"])
            hhat = [_rms(a)[0] for a in hh]
            c_in = _each(lambda d, c, kw, vh: d * c + jnp.dot(kw, vh, preferred_element_type=F32),
                         f["decay"], c_in, f["ktw"], f["vh"])
            n_in = _each(lambda d, n, kw: d * n + jnp.sum(kw, axis=0, keepdims=True), f["decay"], n_in, f["kw"])
            m_in = f["m_new"]
            for h in HEADS:
                vs = slice(ML_DV * h, ML_DV * (h + 1))
                hs_ref[rs, vs] = hh[h]
                y_ref[rs, vs] = (hhat[h] * hn_ref[:, vs] * _sigmoid(o_ref[rs, vs])).astype(BF16)
        for h in HEADS:
            c_sc[ML_DK * h:ML_DK * (h + 1), :] = c_in[h]
            n_sc[h:h + 1, :] = n_in[h]
            m_sc[h:h + 1, :] = jnp.broadcast_to(m_in[h], (1, 128))

    return _side_call(
        body, side, name="mlstm_fwd", steps=steps,
        in_specs=_ml_specs(steps, False) + [_full((1, 128)), _full((1, ML_V))],
        out_specs=[_rows(ML_STEP, ML_V), _rows(ML_STEP, ML_V),
                   pl.BlockSpec((ML_STEP_CHUNKS, ML_HEADS * ML_DK, ML_DV), lambda i: (i, 0, 0)),
                   pl.BlockSpec((ML_STEP_CHUNKS, ML_HEADS, ML_DK), lambda i: (i, 0, 0)),
                   pl.BlockSpec((ML_STEP_CHUNKS, ML_HEADS, 128), lambda i: (i, 0, 0))],
        out_shape=[_sds((t, ML_V), BF16), _sds((t, ML_V), F32),
                   _sds((nc, ML_HEADS * ML_DK, ML_DV), F32), _sds((nc, ML_HEADS, ML_DK), F32),
                   _sds((nc, ML_HEADS, 128), F32)],
        scratch_shapes=[pltpu.VMEM((ML_HEADS * ML_DK, ML_DV), F32), pltpu.VMEM((ML_HEADS, ML_DK), F32),
                        pltpu.VMEM((ML_HEADS, 128), F32)],
        args=(qkv, qkv, qkv, og, og, bif, hn))


def _mlstm_bwd(qkv, og, bif, hn, hs, cst, nst, mst, dy, side):
    t = qkv.shape[0]
    steps = t // ML_STEP

    def chunk(q_ref, k_ref, v_ref, o_ref, gt_ref, bif_ref, hn_ref, hs_ref, cst_ref, nst_ref, mst_ref, dy_ref,
              dp_ref, dhn_ref, dbif_ref, carried):
        x, xt, th, act = _ml_gate_prep(gt_ref[...], bif_ref[...])
        q, k, v = q_ref[...], k_ref[...], v_ref[...]
        qt, vt = q.T, v.T
        rows = lax.broadcasted_iota(jnp.int32, (CHUNK, CHUNK), 0)
        cols = lax.broadcasted_iota(jnp.int32, (CHUNK, CHUNK), 1)
        lane = lax.broadcasted_iota(jnp.int32, (CHUNK, 128), 1)
        row = lax.broadcasted_iota(jnp.int32, (CHUNK, 1), 0)
        as_row = lambda col: jnp.sum(jnp.where(rows == cols, col, 0.0), axis=0, keepdims=True)
        mm = lambda a, b: jnp.dot(a, b, preferred_element_type=F32)
        bf = lambda a: a.astype(BF16)
        ksl = [slice(ML_DK * h, ML_DK * (h + 1)) for h in HEADS]
        vsl = [slice(ML_DV * h, ML_DV * (h + 1)) for h in HEADS]
        c_in = [cst_ref[0, s, :] for s in ksl]
        n_in = [nst_ref[0, h:h + 1, :] for h in HEADS]
        m_in = [mst_ref[0, h:h + 1, 0:1] for h in HEADS]
        dcn, dcn_t, dnn = carried
        f = _ml_chunk_fwd(q, k, v, k.T, x, xt, c_in, n_in, m_in)
        qh, kh, vh, p, winter, decay = f["qh"], f["kh"], f["vh"], f["p"], f["winter"], f["decay"]
        qth = [qt[s, :] for s in ksl]
        vth = [vt[s, :] for s in vsl]
        c_t = [bf(c.T) for c in c_in]
        dmat_t = _each(lambda br, bc, lc: jnp.where(rows <= cols, br - bc + lc, NEG_BIG),
                       f["brow"], f["bcol"], f["licol"])
        inter_row = _each(lambda br, m: br + m, f["brow"], m_in)
        mt_row = _each(lambda d, i: jnp.maximum(jnp.max(d, axis=0, keepdims=True), i), dmat_t, inter_row)
        wt_t = _each(lambda d, m: jnp.exp(d - m), dmat_t, mt_row)
        p_t = _each(lambda w, a, b: w * (mm(a, b) * QK_SCALE), wt_t, kh, qth)
        winter_row = _each(lambda i, m: jnp.exp(i - m), inter_row, mt_row)
        hh = [hs_ref[:, s] for s in vsl]
        hn_h = [hn_ref[:, s] for s in vsl]
        sg = [_sigmoid(o_ref[:, s]) for s in vsl]
        dyh = [dy_ref[:, s] for s in vsl]
        norm = [_rms(a) for a in hh]
        hhat, r = [a for a, _ in norm], [b for _, b in norm]
        dyn = _each(lambda a, b: a * b, dyh, sg)
        do = _each(lambda d, hx, g, s: d * hx * g * s * (1.0 - s), dyh, hhat, hn_h, sg)
        dhn = _each(lambda a, b: jnp.sum(a * b, axis=0, keepdims=True), dyn, hhat)
        dh = _each(_rms_bwd, hhat, r, hn_h, dyn)
        inv = [1.0 / a for a in f["nrm"]]
        dnum = _each(lambda a, b: a * b, dh, inv)
        dnrm = _each(lambda a, b, c: -jnp.sum(a * b, axis=1, keepdims=True) * c, dh, hh, inv)
        dden = _each(lambda d, e, g: jnp.where(jnp.abs(d) > e, g * jnp.sign(d), 0.0), f["den"], f["emt"], dnrm)
        dnb = [bf(a) for a in dnum]
        dnt = [bf(a.T) for a in dnum]
        rmat = _each(lambda a, b, d: mm(a, b) + d, dnb, vth, dden)
        rmat_t = _each(lambda a, b, d: mm(a, b) + as_row(d), vh, dnt, dden)
        ds = _each(lambda w, a: bf(w * a), f["wt"], rmat)
        ds_t = _each(lambda w, a: bf(w * a), wt_t, rmat_t)
        dv = _each(lambda a, b: mm(bf(a), b), p_t, dnb)
        dqs = _each(lambda s, kk, w, d, ct, dd, n: mm(s, kk) + w * (mm(d, ct) + dd * n),
                    ds, kh, winter, dnb, c_t, dden, n_in)
        dk = _each(lambda s, a: mm(s, a) * QK_SCALE, ds_t, qh)
        dinter = _each(lambda qc, dn_, qn, dd, w: (jnp.sum(qc * dn_, axis=1, keepdims=True) + qn * dd) * w,
                       f["qc"], dnum, f["qn"], dden, winter)
        wq = _each(lambda w, a: w * a * QK_SCALE, winter, f["qf"])
        wq_t = _each(lambda w, a: bf(w * a.astype(F32) * QK_SCALE), winter_row, qth)
        dc_loc = _each(mm, wq_t, dnb)
        dct_loc = _each(lambda a, b: mm(a, bf(b)), dnt, wq)
        dn_loc = _each(lambda a, d: jnp.sum(a * d, axis=0, keepdims=True), wq, dden)
        cs_q = _each(lambda a, b: jnp.sum(a * b, axis=1, keepdims=True), p_t, rmat_t)
        db = _each(lambda a, b, di, cs: jnp.sum(a * b, axis=1, keepdims=True) + di - cs, p, rmat, dinter, cs_q)
        ddecay = _each(lambda dc_, c, dn_, n: jnp.sum(jnp.sum(dc_ * c, axis=1, keepdims=True), axis=0, keepdims=True)
                       + jnp.sum(dn_ * n, axis=1, keepdims=True), dcn, c_in, dnn, n_in)
        dkw = _each(lambda a, b, n: mm(a, bf(b)) + n, vh, dcn_t, dnn)
        dk = _each(lambda a, w, b: a + w * b, dk, f["wkf"], dkw)
        da = _each(lambda a, kk, w: jnp.sum(a * kk.astype(F32), axis=1, keepdims=True) * w, dkw, kh, f["wkf"])
        dv = _each(lambda a, kw, dc_: a + mm(bf(kw), bf(dc_)), dv, f["kw"], dcn)
        dgtot = _each(lambda a, dd, d: jnp.sum(a, axis=0, keepdims=True) + dd * d, da, ddecay, decay)
        db = _each(lambda a, b, g: a - b + jnp.where(row == CHUNK - 1, g, 0.0), db, da, dgtot)
        dli = _each(lambda a, b: a + b, cs_q, da)
        dc_new = _each(lambda d, a, b: d * a + b, decay, dcn, dc_loc)
        dct_new = _each(lambda d, a, b: d * a + b, decay, dcn_t, dct_loc)
        dn_new = _each(lambda d, a, b: d * a + b, decay, dnn, dn_loc)
        dx = jnp.zeros((CHUNK, 128), F32)
        for h in HEADS:
            dhn_ref[:, vsl[h]] += dhn[h]
            dp_ref[:, ksl[h]] = bf(dqs[h] * QK_SCALE)
            dp_ref[:, ML_QK + ML_DK * h:ML_QK + ML_DK * (h + 1)] = bf(dk[h])
            dp_ref[:, 2 * ML_QK + ML_DV * h:2 * ML_QK + ML_DV * (h + 1)] = bf(dv[h])
            dp_ref[:, 2 * ML_QK + ML_V + ML_DV * h:2 * ML_QK + ML_V + ML_DV * (h + 1)] = bf(do[h])
            dx = jnp.where(lane == h, dli[h], dx)
            dx = jnp.where(lane == ML_HEADS + h, db[h], dx)
        dlf = _cumsum_rows(dx, reverse=True)
        dact = jnp.where(lane < ML_HEADS, dx, dlf * _sigmoid(-act))
        dz = dact * (1.0 - th * th)
        dp_ref[:, 2 * ML_QK + 2 * ML_V:] = dz.astype(BF16)
        dbif_ref[...] += jnp.sum(dz, axis=0, keepdims=True)
        return dc_new, dct_new, dn_new

    def body(q_ref, k_ref, v_ref, o_ref, gt_ref, bif_ref, hn_ref, hs_ref, cst_ref, nst_ref, mst_ref, dy_ref,
             dp_ref, dhn_ref, dbif_ref, dc_sc, dct_sc, dn_sc):
        @pl.when(pl.program_id(0) == 0)
        def _():
            dc_sc[...] = jnp.zeros_like(dc_sc)
            dct_sc[...] = jnp.zeros_like(dct_sc)
            dn_sc[...] = jnp.zeros_like(dn_sc)
            dhn_ref[...] = jnp.zeros_like(dhn_ref)
            dbif_ref[...] = jnp.zeros_like(dbif_ref)

        dc_all, dct_all, dn_all = dc_sc[...], dct_sc[...], dn_sc[...]
        carried = ([dc_all[ML_DK * h:ML_DK * (h + 1), :] for h in HEADS],
                   [dct_all[ML_DV * h:ML_DV * (h + 1), :] for h in HEADS],
                   [dn_all[h:h + 1, :] for h in HEADS])
        for sub in reversed(range(ML_STEP_CHUNKS)):
            rs = pl.ds(CHUNK * sub, CHUNK)
            one = pl.ds(sub, 1)
            carried = chunk(q_ref.at[rs], k_ref.at[rs], v_ref.at[rs], o_ref.at[rs], gt_ref.at[rs], bif_ref, hn_ref,
                            hs_ref.at[rs], cst_ref.at[one], nst_ref.at[one], mst_ref.at[one], dy_ref.at[rs],
                            dp_ref.at[rs], dhn_ref, dbif_ref, carried)
        for h in HEADS:
            dc_sc[ML_DK * h:ML_DK * (h + 1), :] = carried[0][h]
            dct_sc[ML_DV * h:ML_DV * (h + 1), :] = carried[1][h]
            dn_sc[h:h + 1, :] = carried[2][h]

    rev = lambda i: (steps - 1 - i, 0)
    rev3 = lambda i: (steps - 1 - i, 0, 0)
    return _side_call(
        body, side, name="mlstm_bwd", steps=steps,
        in_specs=_ml_specs(steps, True) + [
            _full((1, 128)), _full((1, ML_V)), pl.BlockSpec((ML_STEP, ML_V), rev),
            pl.BlockSpec((ML_STEP_CHUNKS, ML_HEADS * ML_DK, ML_DV), rev3),
            pl.BlockSpec((ML_STEP_CHUNKS, ML_HEADS, ML_DK), rev3),
            pl.BlockSpec((ML_STEP_CHUNKS, ML_HEADS, 128), rev3), pl.BlockSpec((ML_STEP, ML_V), rev)],
        out_specs=[pl.BlockSpec((ML_STEP, ML_IN_PAD), rev), _full((1, ML_V)), _full((1, 128))],
        out_shape=[_sds((t, ML_IN_PAD), BF16), _sds((1, ML_V), F32), _sds((1, 128), F32)],
        scratch_shapes=[pltpu.VMEM((ML_HEADS * ML_DK, ML_DV), F32), pltpu.VMEM((ML_HEADS * ML_DV, ML_DK), F32),
                        pltpu.VMEM((ML_HEADS, ML_DK), F32)],
        args=(qkv, qkv, qkv, og, og, bif, hn, hs, cst, nst, mst, dy))


LRU_TM = 256
GELU_K = math.sqrt(2.0 / math.pi)
GELU_C = 0.044715


def _gelu(x):
    th = jnp.tanh(GELU_K * (x + GELU_C * x * x * x))
    return 0.5 * x * (1.0 + th), th


def _neg_expm1(x):
    series = -x * (1.0 + x * (0.5 + x * (1.0 / 6.0 + x * (1.0 / 24.0))))
    return jnp.where(x > -0.05, series, 1.0 - jnp.exp(x))


def _block_diag_dot(a, w_ref, dims):
    parts = [lax.dot_general(a[:, LRU_BLOCK * n:LRU_BLOCK * (n + 1)], w_ref[n], dims, preferred_element_type=F32)
             for n in range(LRU_BLOCKS)]
    return jnp.concatenate(parts, axis=1)


def _lru_gates(r, lam):
    ls = _log_sigmoid(lam)
    la = LRU_C * r * ls
    a = jnp.exp(la)
    em = _neg_expm1(2.0 * la)
    mult = jnp.sqrt(em)
    return ls, a, em, mult


def _lru_fwd(proj, cw, cb, wa, ba, wx, bx, lam, side):
    t = proj.shape[0]
    w = D_MODEL
    tm = min(LRU_TM, t)

    def body(gb_ref, up_ref, cw_ref, cb_ref, wa_ref, ba_ref, wx_ref, bx_ref, lam_ref,
             y_ref, u_ref, r_ref, i_ref, h_ref, tail_sc, hprev_sc):
        @pl.when(pl.program_id(0) == 0)
        def _():
            tail_sc[...] = jnp.zeros_like(tail_sc)
            hprev_sc[...] = jnp.zeros_like(hprev_sc)

        up = up_ref[...]
        ext = jnp.concatenate([tail_sc[...], up], axis=0)
        u = cb_ref[...] + cw_ref[CONV_WIDTH - 1:CONV_WIDTH, :] * up
        for s in range(1, CONV_WIDTH):
            u = u + cw_ref[CONV_WIDTH - 1 - s:CONV_WIDTH - s, :] * pltpu.roll(ext, s, 0)[8:8 + tm]
        tail_sc[...] = up[tm - 8:tm]
        ub = u.astype(BF16)
        r = _sigmoid(_block_diag_dot(ub, wa_ref, (((1,), (0,)), ((), ()))) + ba_ref[...])
        ig = _sigmoid(_block_diag_dot(ub, wx_ref, (((1,), (0,)), ((), ()))) + bx_ref[...])
        _, a, _, mult = _lru_gates(r, lam_ref[...])
        h = _scan_rows(a, mult * ig * u, hprev_sc[0:1, :])
        hprev_sc[0:1, :] = h[tm - 1:tm]
        u_ref[...] = u
        r_ref[...] = r
        i_ref[...] = ig
        h_ref[...] = h
        gel, _ = _gelu(gb_ref[...])
        y_ref[...] = (h * gel).astype(BF16)

    vec = _full((1, w))
    wspec = _full((LRU_BLOCKS, LRU_BLOCK, LRU_BLOCK))
    return _side_call(
        body, side, name="lru_fwd", steps=t // tm,
        in_specs=[pl.BlockSpec((tm, w), lambda i: (i, 0)), pl.BlockSpec((tm, w), lambda i: (i, 1)),
                  _full((CONV_WIDTH, w)), vec, wspec, vec, wspec, vec, vec],
        out_specs=[_rows(tm, w)] * 5,
        out_shape=[_sds((t, w), BF16)] + [_sds((t, w), F32)] * 4,
        scratch_shapes=[pltpu.VMEM((8, w), F32), pltpu.VMEM((8, w), F32)],
        args=(proj, proj, cw, cb, wa, ba, wx, bx, lam))


def _lru_bwd(proj, cw, wa, wx, lam, u, r, ig, h, dy, side):
    t = proj.shape[0]
    w = D_MODEL
    tm = min(LRU_TM, t)
    nt = t // tm

    def body(gb_ref, up_ref, cw_ref, wa_ref, wx_ref, lam_ref, u_ref, r_ref, i_ref, h_ref, hp_ref, dy_ref,
             dp_ref, dcw_ref, dcb_ref, dwa_ref, dba_ref, dwx_ref, dbx_ref, dlam_ref, carry_sc, dutail_sc, dls_sc):
        step = pl.program_id(0)

        @pl.when(step == 0)
        def _():
            carry_sc[...] = jnp.zeros_like(carry_sc)
            dutail_sc[...] = jnp.zeros_like(dutail_sc)
            dls_sc[...] = jnp.zeros_like(dls_sc)
            for ref in (dcw_ref, dcb_ref, dwa_ref, dba_ref, dwx_ref, dbx_ref):
                ref[...] = jnp.zeros_like(ref)

        row = lax.broadcasted_iota(jnp.int32, (tm, w), 0)
        u_t, r_t, i_t, h_t = u_ref[...], r_ref[...], i_ref[...], h_ref[...]
        ls, a, em, mult = _lru_gates(r_t, lam_ref[...])
        gb = gb_ref[...]
        gel, th = _gelu(gb)
        dyv = dy_ref[...]
        dgb = dyv * h_t * (0.5 * (1.0 + th) + 0.5 * gb * (1.0 - th * th) * GELU_K * (1.0 + 3.0 * GELU_C * gb * gb))
        a_next = jnp.where(row < tm - 1, pltpu.roll(a, tm - 1, 0), 1.0)
        g = _scan_rows(a_next, dyv * gel, carry_sc[0:1, :], reverse=True)
        carry_sc[0:1, :] = a[0:1] * g[0:1]
        has_prev = jnp.where(step == nt - 1, 0.0, 1.0)
        h_prev = jnp.where(row >= 1, pltpu.roll(h_t, 1, 0), hp_ref[7:8, :] * has_prev)
        dmult = g * i_t * u_t
        dig = g * mult * u_t
        du = g * mult * i_t
        dla = g * h_prev * a - dmult * (1.0 - em) / mult
        dls_sc[0:1, :] += jnp.sum(dla * r_t, axis=0, keepdims=True) * LRU_C
        dpa = dla * (LRU_C * ls) * r_t * (1.0 - r_t)
        dpx = dig * i_t * (1.0 - i_t)
        dba_ref[...] += jnp.sum(dpa, axis=0, keepdims=True)
        dbx_ref[...] += jnp.sum(dpx, axis=0, keepdims=True)
        ub = u_t.astype(BF16)
        dpab = dpa.astype(BF16)
        dpxb = dpx.astype(BF16)
        for n in range(LRU_BLOCKS):
            cs = slice(LRU_BLOCK * n, LRU_BLOCK * (n + 1))
            dwa_ref[n] += lax.dot_general(ub[:, cs], dpab[:, cs], TN_DIMS, preferred_element_type=F32)
            dwx_ref[n] += lax.dot_general(ub[:, cs], dpxb[:, cs], TN_DIMS, preferred_element_type=F32)
        du = du + _block_diag_dot(dpab, wa_ref, NT_DIMS) + _block_diag_dot(dpxb, wx_ref, NT_DIMS)
        dcb_ref[...] += jnp.sum(du, axis=0, keepdims=True)
        ext = jnp.concatenate([du, dutail_sc[...]], axis=0)
        up = up_ref[...]
        dup = cw_ref[CONV_WIDTH - 1:CONV_WIDTH, :] * du
        dcw_ref[CONV_WIDTH - 1:CONV_WIDTH, :] += jnp.sum(up * du, axis=0, keepdims=True)
        for s in range(1, CONV_WIDTH):
            du_s = pltpu.roll(ext, tm + 8 - s, 0)[0:tm]
            dup = dup + cw_ref[CONV_WIDTH - 1 - s:CONV_WIDTH - s, :] * du_s
            dcw_ref[CONV_WIDTH - 1 - s:CONV_WIDTH - s, :] += jnp.sum(up * du_s, axis=0, keepdims=True)
        dutail_sc[...] = du[0:8]
        dp_ref[:, :w] = dgb.astype(BF16)
        dp_ref[:, w:] = dup.astype(BF16)

        @pl.when(step == nt - 1)
        def _():
            dlam_ref[...] = dls_sc[0:1, :] * _sigmoid(-lam_ref[...])

    rev = lambda col: (lambda i: (nt - 1 - i, col))
    vec = _full((1, w))
    wspec = _full((LRU_BLOCKS, LRU_BLOCK, LRU_BLOCK))
    tile = pl.BlockSpec((tm, w), rev(0))
    prev8 = pl.BlockSpec((8, w), lambda i: (jnp.maximum((nt - 1 - i) * (tm // 8) - 1, 0), 0))
    return _side_call(
        body, side, name="lru_bwd", steps=nt,
        in_specs=[tile, pl.BlockSpec((tm, w), rev(1)), _full((CONV_WIDTH, w)), wspec, wspec, vec,
                  tile, tile, tile, tile, prev8, tile],
        out_specs=[pl.BlockSpec((tm, 2 * w), rev(0)), _full((CONV_WIDTH, w)), vec, wspec, vec, wspec, vec, vec],
        out_shape=[_sds((t, 2 * w), BF16), _sds((CONV_WIDTH, w), F32), _sds((1, w), F32),
                   _sds((LRU_BLOCKS, LRU_BLOCK, LRU_BLOCK), F32), _sds((1, w), F32),
                   _sds((LRU_BLOCKS, LRU_BLOCK, LRU_BLOCK), F32), _sds((1, w), F32), _sds((1, w), F32)],
        scratch_shapes=[pltpu.VMEM((8, w), F32), pltpu.VMEM((8, w), F32), pltpu.VMEM((8, w), F32)],
        args=(proj, proj, cw, wa, wx, lam, u, r, ig, h, h, dy))


LANES = 1024
HALF_FFN = D_FF // N_CHIPS
GROUPS = {
    "ml": (("ml_w_in", 1024, ML_IN // N_CHIPS), ("ml_w_out", 256, 1024)),
    "lru": (("lru_w_in", 1024, 512),
            ("lru_gates", 2 * LRU_BLOCKS * 64, LRU_BLOCK),
            ("lru_w_out", 256, 1024)),
    "ffn0": (("ffn_g0", HALF_FFN, 1024), ("ffn_u0", HALF_FFN, 1024), ("ffn_down0", HALF_FFN, 1024)),
    "ffn1": (("ffn_g1", HALF_FFN, 1024), ("ffn_u1", HALF_FFN, 1024), ("ffn_down1", HALF_FFN, 1024)),
}

SMALL_ROWS = 24
ROW_LOSS, ROW_HEAD_NORM, ROW_B_IF, ROW_LRU = 8, 9, 10, 11


def _row_tile(rows, cols, itemsize, budget=3 << 19):
    best = 16
    for t in range(16, rows + 1, 16):
        if rows % t == 0 and t * cols * itemsize <= budget:
            best = t
    return best


def _chip_peers():
    x, y, c = lax.axis_index("x"), lax.axis_index("y"), lax.axis_index("c")
    return x, y, c, [(1 - x, y), (x, 1 - y), (1 - x, 1 - y)]


HBM_SPEC = pl.BlockSpec(memory_space=pltpu.HBM)


def _remote(src, dst, send_sems, recv_sems, k, to):
    return pltpu.make_async_remote_copy(src_ref=src, dst_ref=dst, send_sem=send_sems.at[k], recv_sem=recv_sems.at[k],
                                        device_id=to, device_id_type=MESH)


GATHER_SEMS = 7


def _gather_copies(kind, specs, part_refs, out_refs, send_sems, recv_sems):
    x, y, c, chips = _chip_peers()
    me = 2 * x + y
    sib = (x, y, 1 - c)
    copy = functools.partial(_remote, send_sems=send_sems, recv_sems=recv_sems)
    out = []
    for p, (_, rows, _) in enumerate(specs):
        mine = pl.ds(c * (rows // 2), rows // 2)
        theirs = pl.ds((1 - c) * (rows // 2), rows // 2)
        base = GATHER_SEMS * p
        for j, (cx, cy) in enumerate(chips):
            land = out_refs[p].at[2 * cx + cy, mine]
            other = out_refs[p].at[2 * cx + cy, theirs]
            if kind == "first":
                out.append(copy(part_refs[p].at[mine], out_refs[p].at[me, mine], k=base + j, to=(cx, cy, c)))
            elif kind == "landed":
                out.append(copy(land, land, k=base + j, to=sib))
            elif kind == "forward":
                out.append(copy(land, land, k=base + 3 + j, to=sib))
            else:
                out.append(copy(other, other, k=base + 3 + j, to=sib))
        if kind in ("first", "arriving"):
            out.append(copy(part_refs[p], out_refs[p].at[me], k=base + 6, to=sib))
    return out


def _gather_start(specs, *refs):
    for cp in _gather_copies("first", specs, *refs):
        cp.start()


def _gather_finish(specs, *refs):
    forwards = _gather_copies("forward", specs, *refs)
    for land, fwd in zip(_gather_copies("landed", specs, *refs), forwards):
        land.wait_recv()
        fwd.start()
    for cp in _gather_copies("arriving", specs, *refs):
        cp.wait_recv()
    for cp in _gather_copies("first", specs, *refs) + forwards:
        cp.wait_send()


def _gather_shapes(specs):
    return [_sds((N_CHIPS, rows, cols), BF16) for _, rows, cols in specs]


def _gather_side(specs, parts):
    return _Side(inputs=list(parts), out_shape=_gather_shapes(specs), n_sems=GATHER_SEMS * len(specs),
                 start=functools.partial(_gather_start, specs), finish=functools.partial(_gather_finish, specs))


def _gather_weights(specs, parts, small):
    n = len(specs)

    def body(*refs):
        part_refs, small_ref = refs[:n], refs[n]
        out_refs, outs_ref = refs[n + 1:2 * n + 1], refs[2 * n + 1]
        send_sems, recv_sems, small_send, small_recv, loc_sem = refs[2 * n + 2:]
        x, y, c, chips = _chip_peers()
        me = 2 * x + y
        local = pltpu.make_async_copy(small_ref, outs_ref.at[me], loc_sem.at[0])
        local.start()
        _gather_start(specs, part_refs, out_refs, send_sems, recv_sems)
        sent = [_remote(small_ref, outs_ref.at[me], small_send, small_recv, j, (cx, cy, c))
                for j, (cx, cy) in enumerate(chips)]
        for cp in sent:
            cp.start()
        _gather_finish(specs, part_refs, out_refs, send_sems, recv_sems)
        for j, (cx, cy) in enumerate(chips):
            _remote(small_ref, outs_ref.at[2 * cx + cy], small_send, small_recv, j, (cx, cy, c)).wait_recv()
        for cp in sent:
            cp.wait_send()
        local.wait()

    dma = pltpu.SemaphoreType.DMA
    return pl.pallas_call(
        body, name="gather_weights",
        in_specs=[HBM_SPEC] * (n + 1), out_specs=[HBM_SPEC] * (n + 1),
        out_shape=_gather_shapes(specs) + [_sds((N_CHIPS,) + small.shape, small.dtype)],
        scratch_shapes=[dma((GATHER_SEMS * n,)), dma((GATHER_SEMS * n,)), dma((3,)), dma((3,)), dma((1,))],
    )(*parts, small)


def _exchange_copies(specs, g_refs, out_refs, send_sems, recv_sems):
    x, y, c, _ = _chip_peers()
    return [_remote(g_refs[p].at[:, pl.ds((1 - c) * (rows // 2), rows // 2)], out_refs[p], send_sems, recv_sems, p,
                    (x, y, 1 - c)) for p, (_, rows, _) in enumerate(specs)]


def _exchange_start(specs, *refs):
    for cp in _exchange_copies(specs, *refs):
        cp.start()


def _exchange_finish(specs, *refs):
    for cp in _exchange_copies(specs, *refs):
        cp.wait()


def _exchange_shapes(specs):
    return [_sds((N_CHIPS, rows // 2, cols), BF16) for _, rows, cols in specs]


def _exchange_side(specs, gparts):
    return _Side(inputs=list(gparts), out_shape=_exchange_shapes(specs), n_sems=len(specs),
                 start=functools.partial(_exchange_start, specs), finish=functools.partial(_exchange_finish, specs))


def _exchange_halves(specs, gparts, name):
    n = len(specs)

    def body(*refs):
        _exchange_start(specs, refs[:n], refs[n:2 * n], *refs[2 * n:])
        _exchange_finish(specs, refs[:n], refs[n:2 * n], *refs[2 * n:])

    return pl.pallas_call(
        body, name=name, in_specs=[HBM_SPEC] * n, out_specs=[HBM_SPEC] * n, out_shape=_exchange_shapes(specs),
        scratch_shapes=[pltpu.SemaphoreType.DMA((n,)), pltpu.SemaphoreType.DMA((n,))],
    )(*gparts)


def _add_halves(g, recv, pos, name):
    _, half, cols = recv.shape
    tr = _row_tile(half, cols, 2)
    tiles = half // tr

    def body(pos_ref, a_ref, b_ref, o_ref):
        o_ref[...] = (a_ref[...].astype(F32) + b_ref[...].astype(F32)).astype(BF16)

    spec = pl.BlockSpec((1, tr, cols), lambda k, i, pos_ref: (k, i, 0))
    return pl.pallas_call(
        body, name=name,
        grid_spec=pltpu.PrefetchScalarGridSpec(
            num_scalar_prefetch=1, grid=(N_CHIPS, tiles),
            in_specs=[pl.BlockSpec((1, tr, cols), lambda k, i, pos_ref: (k, pos_ref[0] * tiles + i, 0)), spec],
            out_specs=spec),
        out_shape=_sds((N_CHIPS, half, cols), BF16),
        compiler_params=_params("parallel", "parallel"),
    )(pos, g, recv)


def _scatter_copies(n, s1_refs, recv_refs, send_sems, recv_sems):
    x, y, c, chips = _chip_peers()
    return [_remote(s1_refs[p].at[2 * cx + cy], recv_refs[p].at[j], send_sems, recv_sems, 3 * p + j, (cx, cy, c))
            for p in range(n) for j, (cx, cy) in enumerate(chips)]


def _scatter_start(n, s1_refs, recv_refs, send_sems, recv_sems):
    for cp in _scatter_copies(n, s1_refs, recv_refs, send_sems, recv_sems):
        cp.start()


def _scatter_finish(n, s1_refs, recv_refs, send_sems, recv_sems):
    for cp in _scatter_copies(n, s1_refs, recv_refs, send_sems, recv_sems):
        cp.wait()


def _scatter_shapes(s1):
    return [_sds((3,) + a.shape[1:], a.dtype) for a in s1]


def _scatter_side(s1):
    n = len(s1)
    return _Side(inputs=list(s1), out_shape=_scatter_shapes(s1), n_sems=3 * n,
                 start=functools.partial(_scatter_start, n), finish=functools.partial(_scatter_finish, n))


def _sum_chips(s1, recv, pos, name):
    _, half, cols = recv.shape
    tr = _row_tile(half, cols, 4)
    tiles = half // tr

    def body(pos_ref, a_ref, b_ref, o_ref):
        acc = a_ref[0].astype(F32)
        for j in range(3):
            acc = acc + b_ref[j].astype(F32)
        o_ref[...] = acc

    return pl.pallas_call(
        body, name=name,
        grid_spec=pltpu.PrefetchScalarGridSpec(
            num_scalar_prefetch=1, grid=(tiles,),
            in_specs=[pl.BlockSpec((1, tr, cols), lambda i, pos_ref: (pos_ref[1], i, 0)),
                      pl.BlockSpec((3, tr, cols), lambda i, pos_ref: (0, i, 0))],
            out_specs=pl.BlockSpec((tr, cols), lambda i, pos_ref: (pos_ref[0] * tiles + i, 0))),
        out_shape=_sds((2 * half, cols), F32),
        compiler_params=_params("parallel"),
    )(pos, s1, recv)


def _sum_small(small_all):
    def body(a_ref, o_ref):
        acc = a_ref[0]
        for d in range(1, 8):
            acc = acc + a_ref[d]
        o_ref[...] = acc

    return pl.pallas_call(body, name="sum_small", out_shape=_sds(small_all.shape[1:], F32))(small_all)


def _join_halves(specs, s2, small):
    n = len(specs)
    flips = [(fx, fy, fc) for fx in (0, 1) for fy in (0, 1) for fc in (0, 1)][1:]

    def body(*refs):
        small_ref, all_ref = refs[n], refs[2 * n + 1]
        buf_refs = refs[n + 1:2 * n + 1]
        send_sems, recv_sems, small_send, small_recv, loc_sem = refs[2 * n + 2:]
        x, y, c, _ = _chip_peers()
        my_slot = all_ref.at[4 * x + 2 * y + c]
        local = pltpu.make_async_copy(small_ref, my_slot, loc_sem.at[0])
        local.start()
        sent = []
        for p, (_, rows, _) in enumerate(specs):
            mine = buf_refs[p].at[pl.ds(c * (rows // 2), rows // 2)]
            sent.append(_remote(mine, mine, send_sems, recv_sems, p, (x, y, 1 - c)))
        peers = [(1 - x if fx else x, 1 - y if fy else y, 1 - c if fc else c) for fx, fy, fc in flips]
        sent += [_remote(small_ref, my_slot, small_send, small_recv, i, p) for i, p in enumerate(peers)]
        for cp in sent:
            cp.start()
        for p, (_, rows, _) in enumerate(specs):
            theirs = buf_refs[p].at[pl.ds((1 - c) * (rows // 2), rows // 2)]
            _remote(theirs, theirs, send_sems, recv_sems, p, (x, y, 1 - c)).wait_recv()
        for i, (px, py, pc) in enumerate(peers):
            _remote(small_ref, all_ref.at[4 * px + 2 * py + pc], small_send, small_recv, i, peers[i]).wait_recv()
        for cp in sent:
            cp.wait_send()
        local.wait()

    dma = pltpu.SemaphoreType.DMA
    *joined, small_all = pl.pallas_call(
        body, name="join_halves", in_specs=[HBM_SPEC] * (n + 1), out_specs=[HBM_SPEC] * (n + 1),
        out_shape=[_sds(a.shape, a.dtype) for a in s2] + [_sds((8,) + small.shape, small.dtype)],
        input_output_aliases={p: p for p in range(n)},
        scratch_shapes=[dma((n,)), dma((n,)), dma((7,)), dma((7,)), dma((1,))],
    )(*s2, small)
    return joined, small_all


def _adamw(w, g, m, v, name):
    rows, cols = w.shape[0], w.shape[-1]
    tm = rows
    for cand in (512, 256, 128, 64, 32, 16, 8) if w.ndim == 2 else (rows // 4,):
        if rows % cand == 0 and rows > cand:
            tm = cand
            break

    def body(w_ref, g_ref, m_ref, v_ref, d_ref, nm_ref, nv_ref):
        gv = g_ref[...]
        nm = ADAM_B1 * m_ref[...] + (1.0 - ADAM_B1) * gv
        nv = ADAM_B2 * v_ref[...] + (1.0 - ADAM_B2) * (gv * gv)
        m_hat = nm / (1.0 - ADAM_B1 ** ADAM_STEP)
        v_hat = nv / (1.0 - ADAM_B2 ** ADAM_STEP)
        d_ref[...] = -ADAM_LR * (m_hat / (jnp.sqrt(v_hat) + ADAM_EPS) + ADAM_WD * w_ref[...])
        nm_ref[...] = nm
        nv_ref[...] = nv

    spec = _rows(tm, cols) if w.ndim == 2 else pl.BlockSpec((tm, 1, cols), lambda i: (i, 0, 0))
    return pl.pallas_call(
        body, name=name, grid=(rows // tm,), in_specs=[spec] * 4, out_specs=[spec] * 3,
        out_shape=[_sds(w.shape, F32)] * 3, compiler_params=_params("parallel"),
    )(w, g, m, v)


WEIGHTS = ("ml_w_in", "ml_b_if", "ml_head_norm", "ml_w_out", "lru_w_in", "lru_conv_w", "lru_conv_b", "lru_w_gate_a",
           "lru_b_gate_a", "lru_w_gate_x", "lru_b_gate_x", "lru_lambda", "lru_w_out", "norm_pre_mix", "norm_post_mix",
           "norm_pre_ffn", "norm_post_ffn", "ffn_w_gate", "ffn_w_up", "ffn_w_down")
LRU_VECTORS = ("lru_conv_b", "lru_b_gate_a", "lru_b_gate_x", "lru_lambda")
NORMS = ("norm_pre_mix", "norm_post_mix", "norm_pre_ffn", "norm_post_ffn")


def _by_cols(a):
    return jnp.concatenate([a[k] for k in range(N_CHIPS)], axis=-1)


def _by_chip(a, width):
    return jnp.stack([a[..., k * width:(k + 1) * width] for k in range(N_CHIPS)])


def _weight_shards(w):
    bf = lambda a: a.astype(BF16)
    shards = dict(
        ml_w_in=bf(w["ml_w_in"][0]), ml_w_out=bf(w["ml_w_out"][0]), lru_w_in=bf(w["lru_w_in"][0]),
        lru_gates=bf(jnp.concatenate([w["lru_w_gate_a"][0], w["lru_w_gate_x"][0]], axis=0)).reshape(-1, LRU_BLOCK),
        lru_w_out=bf(w["lru_w_out"][0]))
    gate_t, up_t = bf(jnp.swapaxes(w["ffn_w_gate"], 1, 2)), bf(jnp.swapaxes(w["ffn_w_up"], 1, 2))
    for layer in range(2):
        shards[f"ffn_g{layer}"] = gate_t[layer]
        shards[f"ffn_u{layer}"] = up_t[layer]
        shards[f"ffn_down{layer}"] = bf(w["ffn_w_down"][layer])
    return shards


def _ffn_weights(parts):
    return [a.reshape(D_FF, 1024) for a in parts]


def _ffn_grads(dgu, z, act, dy, layer):
    dws = [_matmul_tn(dgu, z, True, 256, f"ffn{layer}_dw_gate", window=(0, D_FF)),
           _matmul_tn(dgu, z, True, 256, f"ffn{layer}_dw_up", window=(D_FF, D_FF)),
           _matmul_tn(act, dy, True, 256, f"ffn{layer}_dw_down")]
    return [a.reshape(N_CHIPS, HALF_FFN, 1024) for a in dws]


def _sum_group(group, s1, recv, pos):
    return [_sum_chips(a, r, pos, "sum_chips_" + n) for (n, _, _), a, r in zip(GROUPS[group], s1, recv)]


def _update(w, grads, m, v):
    delta, new_m, new_v = {}, {}, {}
    for n in WEIGHTS:
        if n == "ml_w_in":
            view = lambda a: jnp.transpose(a, (2, 0, 1))
            back = lambda a: jnp.transpose(a, (1, 2, 0))
        else:
            flip = (lambda a: jnp.swapaxes(a, 1, 2)) if n in ("ffn_w_gate", "ffn_w_up") else (lambda a: a)
            shape = flip(w[n]).shape
            view = lambda a: flip(a).reshape(-1, shape[-1])
            back = lambda a: flip(a.reshape(shape))
        d, nm, nv = _adamw(view(w[n]), view(grads[n]), view(m[n]), view(v[n]), "adamw_" + n)
        delta[n], new_m[n], new_v[n] = back(d), back(nm), back(nv)
    return delta, new_m, new_v


def kernel(x, ml_w_in, ml_b_if, ml_head_norm, ml_w_out, lru_w_in, lru_conv_w, lru_conv_b, lru_w_gate_a, lru_b_gate_a, lru_w_gate_x, lru_b_gate_x, lru_lambda, lru_w_out, norm_pre_mix, norm_post_mix, norm_pre_ffn, norm_post_ffn, ffn_w_gate, ffn_w_up, ffn_w_down, loss_target, m_ml_w_in, m_ml_b_if, m_ml_head_norm, m_ml_w_out, m_lru_w_in, m_lru_conv_w, m_lru_conv_b, m_lru_w_gate_a, m_lru_b_gate_a, m_lru_w_gate_x, m_lru_b_gate_x, m_lru_lambda, m_lru_w_out, m_norm_pre_mix, m_norm_post_mix, m_norm_pre_ffn, m_norm_post_ffn, m_ffn_w_gate, m_ffn_w_up, m_ffn_w_down, v_ml_w_in, v_ml_b_if, v_ml_head_norm, v_ml_w_out, v_lru_w_in, v_lru_conv_w, v_lru_conv_b, v_lru_w_gate_a, v_lru_b_gate_a, v_lru_w_gate_x, v_lru_b_gate_x, v_lru_lambda, v_lru_w_out, v_norm_pre_mix, v_norm_post_mix, v_norm_pre_ffn, v_norm_post_ffn, v_ffn_w_gate, v_ffn_w_up, v_ffn_w_down):
    args = locals()
    w = {n: args[n] for n in WEIGHTS}
    m = {n: args["m_" + n] for n in WEIGHTS}
    v = {n: args["v_" + n] for n in WEIGHTS}
    xs, target = x[0], loss_target[0]
    mx, my, mc, _ = _chip_peers()
    chip = 2 * mx + my
    pos = jnp.stack([mc, chip])
    row = lambda a, i: a[i:i + 1]
    npm, nqm, npf, nqf = (w[n] for n in NORMS)
    shards = _weight_shards(w)
    of = lambda group: [shards[n] for n, _, _ in GROUPS[group]]
    bif = jnp.pad(w["ml_b_if"], ((0, 0), (0, 128 - 2 * ML_HEADS)))
    hn = w["ml_head_norm"]

    small = jnp.concatenate([w["lru_conv_w"][0]] + [w[n] for n in LRU_VECTORS], axis=0)
    ml_in_spec, ml_out_spec = GROUPS["ml"][:1], GROUPS["ml"][1:]
    ml_in_parts, smalls = _gather_weights(ml_in_spec, of("ml")[:1], small)
    ml_w_in = jnp.concatenate([ml_in_parts[k] for k in range(N_CHIPS)]
                              + [jnp.zeros((D_MODEL, ML_IN_PAD - ML_IN), BF16)], axis=1)
    vec = _by_cols(smalls)
    lru_spec, ffn1_spec = GROUPS["lru"], GROUPS["ffn1"]
    ffn0_spec = GROUPS["ffn0"]
    (z0, qkv, og), got = _norm_matmul(xs, row(npm, 0), ml_w_in, 2 * ML_QK + ML_V, "ml_in",
                                      _gather_side(ml_out_spec + ffn0_spec[2:], of("ml")[1:] + of("ffn0")[2:]))
    ml_w_out = got[0].reshape(1024, 1024)
    ffn0_down = got[1:]
    (y0, hs, cst, nst, mst), got = _mlstm_fwd(qkv, og, bif, hn, _gather_side(ffn0_spec[:2], of("ffn0")[:2]))
    wg0, wu0, w_down0 = _ffn_weights(got + ffn0_down)
    (ymix0, h1), got = _matmul_postnorm(y0, ml_w_out, row(nqm, 0), xs, "ml_out",
                                        _gather_side(lru_spec[:1], of("lru")[:1]))
    lru_w_in = _by_cols(got[0])
    (zf0, gu0, yf0, h2), got = _ffn_forward(
        h1, row(npf, 0), wg0, wu0, w_down0, row(nqf, 0), None, "ffn0_fwd",
        _gather_side(lru_spec[1:] + ffn1_spec[:2], of("lru")[1:] + of("ffn1")[:2]))
    gates = got[0].reshape(N_CHIPS, 2, LRU_BLOCKS, 64, LRU_BLOCK).transpose(1, 2, 0, 3, 4)
    gates = gates.reshape(2, LRU_BLOCKS, LRU_BLOCK, LRU_BLOCK)
    lru_w_out = got[1].reshape(1024, 1024)
    ffn1_gu = got[2:4]
    (z1, proj1), _ = _norm_matmul(h2, row(npm, 1), lru_w_in, 0, "lru_in", None)
    (y1, u, r, ig, hl), got = _lru_fwd(proj1, vec[0:4], vec[4:5], gates[0], vec[5:6], gates[1], vec[6:7], vec[7:8],
                                       _gather_side(ffn1_spec[2:], of("ffn1")[2:]))
    wg1, wu1, w_down1 = _ffn_weights(ffn1_gu + got)
    (ymix1, h3), _ = _matmul_postnorm(y1, lru_w_out, row(nqm, 1), h2, "lru_out", None)
    (zf1, gu1, yf1, dh4, loss_part), _ = _ffn_forward(
        h3, row(npf, 1), wg1, wu1, w_down1, row(nqf, 1), target, "ffn1_fwd", None)

    add = lambda group, gparts, recv: [_add_halves(g, r, pos, "add_halves_" + n)
                                       for (n, _, _), g, r in zip(GROUPS[group], gparts, recv)]
    (dyf1, dgu1, act1, dh3, dqf1, dpf1), _ = _ffn_backward(
        dh4, yf1, row(nqf, 1), w_down1, gu1, wg1, wu1, h3, row(npf, 1), "ffn1_bwd", None)
    g_ffn1 = _ffn_grads(dgu1, zf1, act1, dyf1, 1)

    (dymix1, dy1, dqm1), recv = _bwd_out(dh3, ymix1, row(nqm, 1), lru_w_out, "lru_bwd_out",
                                         _exchange_side(GROUPS["ffn1"], g_ffn1))
    s1_ffn1 = add("ffn1", g_ffn1, recv)
    dw_lru_out = _matmul_tn(y1, dymix1, False, 256, "lru_dw_out")
    (dproj1, dcw, dcb, dwa, dba, dwx, dbx, dlam), recv_ffn1 = _lru_bwd(
        proj1, vec[0:4], gates[0], gates[1], vec[7:8], u, r, ig, hl, dy1, _scatter_side(s1_ffn1))
    dw_lru_in = _matmul_tn(z1, dproj1, False, 512, "lru_dw_in")
    dgates = jnp.stack([dwa, dwx]).astype(BF16).reshape(2, LRU_BLOCKS, N_CHIPS, 64, LRU_BLOCK)
    dgates = dgates.transpose(2, 0, 1, 3, 4).reshape(N_CHIPS, -1, LRU_BLOCK)
    g_lru = [_by_chip(dw_lru_in, 512), dgates, dw_lru_out.reshape(N_CHIPS, 256, 1024)]
    (dh2, dpm1), recv = _bwd_in(dproj1, [lru_w_in], False, h2, row(npm, 1), dh3, "lru_bwd_in",
                                _exchange_side(GROUPS["lru"], g_lru))
    s1_lru = add("lru", g_lru, recv)

    (dyf0, dgu0, act0, dh1, dqf0, dpf0), recv_lru = _ffn_backward(
        dh2, yf0, row(nqf, 0), w_down0, gu0, wg0, wu0, h1, row(npf, 0), "ffn0_bwd", _scatter_side(s1_lru))
    g_ffn0 = _ffn_grads(dgu0, zf0, act0, dyf0, 0)

    (dymix0, dy0, dqm0), recv = _bwd_out(dh1, ymix0, row(nqm, 0), ml_w_out, "ml_bwd_out",
                                         _exchange_side(GROUPS["ffn0"], g_ffn0))
    s1_ffn0 = add("ffn0", g_ffn0, recv)
    dw_ml_out = _matmul_tn(y0, dymix0, False, 256, "ml_dw_out")
    (dproj0, dhn, dbif), recv_ffn0 = _mlstm_bwd(qkv, og, bif, hn, hs, cst, nst, mst, dy0, _scatter_side(s1_ffn0))
    dw_ml_in = _matmul_tn(z0, dproj0, False, 640, "ml_dw_in")
    g_ml = [_by_chip(dw_ml_in[:, :ML_IN], ML_IN // N_CHIPS), dw_ml_out.reshape(N_CHIPS, 256, 1024)]
    s1_ml = add("ml", g_ml, _exchange_halves(GROUPS["ml"], g_ml, "exchange_halves_ml"))
    (dx, dpm0), recv_ml = _bwd_in(dproj0, [ml_w_in], False, xs, row(npm, 0), dh1, "ml_bwd_in", _scatter_side(s1_ml))

    pad_lanes = lambda a: jnp.pad(a, ((0, 0), (0, LANES - a.shape[1])))
    small = jnp.concatenate(
        [jnp.concatenate([dpm0, dpm1]), jnp.concatenate([dqm0, dqm1]), jnp.concatenate([dpf0, dpf1]),
         jnp.concatenate([dqf0, dqf1]), pad_lanes(loss_part), dhn, pad_lanes(dbif), dcw, dcb, dba, dbx, dlam,
         jnp.zeros((SMALL_ROWS - 19, LANES), F32)], axis=0)
    order = ("ml", "lru", "ffn0", "ffn1")
    s2 = (_sum_group("ml", s1_ml, recv_ml, pos) + _sum_group("lru", s1_lru, recv_lru, pos)
          + _sum_group("ffn0", s1_ffn0, recv_ffn0, pos) + _sum_group("ffn1", s1_ffn1, recv_ffn1, pos))
    specs = sum((GROUPS[g] for g in order), ())
    joined, small_all = _join_halves(specs, s2, small)
    red = dict(zip([n for n, _, _ in specs], joined))
    vsum = _sum_small(small_all)

    dgates = red["lru_gates"].reshape(2, LRU_BLOCKS, 64, LRU_BLOCK)
    cols = lambda a: lax.dynamic_slice_in_dim(a, chip * 256, 256, axis=1)
    grads = dict(
        ml_w_in=red["ml_w_in"], ml_w_out=red["ml_w_out"], lru_w_in=red["lru_w_in"], lru_w_gate_a=dgates[0],
        lru_w_gate_x=dgates[1], lru_w_out=red["lru_w_out"],
        ffn_w_gate=jnp.swapaxes(jnp.stack([red["ffn_g0"], red["ffn_g1"]]), 1, 2),
        ffn_w_up=jnp.swapaxes(jnp.stack([red["ffn_u0"], red["ffn_u1"]]), 1, 2),
        ffn_w_down=jnp.stack([red["ffn_down0"], red["ffn_down1"]]),
        ml_head_norm=vsum[ROW_HEAD_NORM:ROW_HEAD_NORM + 1], ml_b_if=vsum[ROW_B_IF:ROW_B_IF + 1, :2 * ML_HEADS],
        lru_conv_w=cols(vsum[ROW_LRU:ROW_LRU + 4]))
    for i, n in enumerate(NORMS):
        grads[n] = vsum[2 * i:2 * i + 2]
    for i, n in enumerate(LRU_VECTORS):
        grads[n] = cols(vsum[ROW_LRU + 4 + i:ROW_LRU + 5 + i])
    loss = vsum[ROW_LOSS, 0]
    grads = {n: grads[n].reshape(w[n].shape) for n in WEIGHTS}
    delta, new_m, new_v = _update(w, grads, m, v)
    return (loss, dx[None], *[grads[n] for n in WEIGHTS], *[delta[n] for n in WEIGHTS],
            *[new_m[n] for n in WEIGHTS], *[new_v[n] for n in WEIGHTS])
```

```python
import functools
import math
from typing import Callable, NamedTuple

import jax
import jax.numpy as jnp
from jax import lax
from jax.experimental import pallas as pl
from jax.experimental.pallas import tpu as pltpu

F32 = jnp.float32
BF16 = jnp.bfloat16
MESH = pl.DeviceIdType.MESH

D_MODEL = 1024
D_FF = 2816
ML_HEADS = 8
ML_DK = 64
ML_DV = 128
ML_QK = ML_HEADS * ML_DK
ML_V = ML_HEADS * ML_DV
ML_IN = 2 * ML_QK + 2 * ML_V + 2 * ML_HEADS
ML_IN_PAD = 3200
CHUNK = 64
GATE_CAP = 15.0
ML_M_INIT = -1e30
NEG_BIG = -1e30
LRU_BLOCKS = 4
LRU_BLOCK = 256
CONV_WIDTH = 4
LRU_C = 8.0
EPS = 1e-6
QK_SCALE = ML_DK ** -0.5

ADAM_LR = 0.001
ADAM_B1 = 0.9
ADAM_B2 = 0.999
ADAM_EPS = 1e-08
ADAM_WD = 0.01
ADAM_STEP = 10

N_CHIPS = 4
V7X_VMEM_LIMIT = 56 * 1024 * 1024

NT_DIMS = (((1,), (1,)), ((), ()))
TN_DIMS = (((0,), (0,)), ((), ()))


def _params(*semantics):
    return pltpu.CompilerParams(dimension_semantics=semantics, vmem_limit_bytes=V7X_VMEM_LIMIT)


def _sds(shape, dtype):
    return jax.ShapeDtypeStruct(shape, dtype)


def _full(shape):
    return pl.BlockSpec(shape, lambda *_: (0,) * len(shape))


def _rows(tm, n):
    return pl.BlockSpec((tm, n), lambda i: (i, 0))


def _sigmoid(x):
    return 1.0 / (1.0 + jnp.exp(-x))


def _log_sigmoid(x):
    return jnp.minimum(x, 0.0) - jnp.log1p(jnp.exp(-jnp.abs(x)))


def _rms(x):
    r = lax.rsqrt(jnp.mean(x * x, axis=-1, keepdims=True) + EPS)
    return x * r, r


def _rms_bwd(xhat, r, g, dy):
    dxh = dy * g
    return r * (dxh - xhat * jnp.mean(dxh * xhat, axis=-1, keepdims=True))


SUBLANES = 8


def _scan_rows(a, b, carry, reverse=False):
    n, w = a.shape
    groups = n // SUBLANES
    a3, b3 = a.reshape(groups, SUBLANES, w), b.reshape(groups, SUBLANES, w)
    sub = lax.broadcasted_iota(jnp.int32, a3.shape, 1)
    s = 1
    while s < SUBLANES:
        keep = sub < SUBLANES - s if reverse else sub >= s
        shift = SUBLANES - s if reverse else s
        b3 = b3 + a3 * jnp.where(keep, pltpu.roll(b3, shift, 1), 0.0)
        a3 = a3 * jnp.where(keep, pltpu.roll(a3, shift, 1), 1.0)
        s *= 2
    out = [None] * groups
    for g in (reversed(range(groups)) if reverse else range(groups)):
        out[g] = b3[g] + a3[g] * carry
        carry = out[g][0:1] if reverse else out[g][SUBLANES - 1:SUBLANES]
    return jnp.concatenate(out, axis=0)


def _cumsum_rows(x, reverse=False):
    n = x.shape[0]
    row = lax.broadcasted_iota(jnp.int32, x.shape, 0)
    s = 1
    while s < n:
        if reverse:
            x = x + jnp.where(row < n - s, pltpu.roll(x, n - s, 0), 0.0)
        else:
            x = x + jnp.where(row >= s, pltpu.roll(x, s, 0), 0.0)
        s *= 2
    return x


def _norm_matmul(h, g, w, n_bf16, name, side):
    t, d = h.shape
    n = w.shape[1]
    tm = min(512, t)

    def body(h_ref, g_ref, w_ref, z_ref, *o_refs):
        xhat, _ = _rms(h_ref[...])
        z = (xhat * g_ref[...]).astype(BF16)
        z_ref[...] = z
        out = jnp.dot(z, w_ref[...], preferred_element_type=F32)
        if n_bf16:
            o_refs[0][...] = out[:, :n_bf16].astype(BF16)
            o_refs[1][...] = out[:, n_bf16:]
        else:
            o_refs[0][...] = out

    if n_bf16:
        out_specs = [_rows(tm, d), _rows(tm, n_bf16), _rows(tm, n - n_bf16)]
        out_shape = [_sds((t, d), BF16), _sds((t, n_bf16), BF16), _sds((t, n - n_bf16), F32)]
    else:
        out_specs = [_rows(tm, d), _rows(tm, n)]
        out_shape = [_sds((t, d), BF16), _sds((t, n), F32)]
    return _side_call(body, side, name=name, steps=t // tm, in_specs=[_rows(tm, d), _full((1, d)), _full((d, n))],
                      out_specs=out_specs, out_shape=out_shape, scratch_shapes=[], args=(h, g, w))


def _matmul_postnorm(a, w, g, res, name, side):
    t = res.shape[0]
    k, d = w.shape
    tm = min(512, t)

    def body(a_ref, w_ref, g_ref, res_ref, y_ref, o_ref):
        y = jnp.dot(a_ref[...], w_ref[...], preferred_element_type=F32)
        y_ref[...] = y
        yhat, _ = _rms(y)
        o_ref[...] = res_ref[...] + yhat * g_ref[...]

    return _side_call(
        body, side, name=name, steps=t // tm,
        in_specs=[_rows(tm, k), _full((k, d)), _full((1, d)), _rows(tm, d)],
        out_specs=[_rows(tm, d), _rows(tm, d)],
        out_shape=[_sds((t, d), F32), _sds((t, d), F32)],
        scratch_shapes=[], args=(a, w, g, res))


class _Side(NamedTuple):
    inputs: list
    out_shape: list
    n_sems: int
    start: Callable
    finish: Callable


def _side_call(body, side, *, name, steps, in_specs, out_specs, out_shape, scratch_shapes, args):
    n_in, n_out, n_scr = len(in_specs), len(out_specs), len(scratch_shapes)
    if side is None:
        outs = pl.pallas_call(
            body, name=name, grid=(steps,), in_specs=in_specs, out_specs=out_specs, out_shape=out_shape,
            scratch_shapes=scratch_shapes, compiler_params=_params("arbitrary"))(*args)
        return list(outs), []
    s_in, s_out = len(side.inputs), len(side.out_shape)

    def carrying(*refs):
        ins, side_ins = refs[:n_in], refs[n_in:n_in + s_in]
        outs = refs[n_in + s_in:n_in + s_in + n_out]
        side_outs = refs[n_in + s_in + n_out:n_in + s_in + n_out + s_out]
        scratch = refs[n_in + s_in + n_out + s_out:]
        own, sems = scratch[:n_scr], scratch[n_scr:]

        @pl.when(pl.program_id(0) == 0)
        def _():
            side.start(side_ins, side_outs, *sems)

        body(*ins, *outs, *own)

        @pl.when(pl.program_id(0) == steps - 1)
        def _():
            side.finish(side_ins, side_outs, *sems)

    outs = pl.pallas_call(
        carrying, name=name, grid=(steps,), in_specs=list(in_specs) + [HBM_SPEC] * s_in,
        out_specs=list(out_specs) + [HBM_SPEC] * s_out, out_shape=list(out_shape) + list(side.out_shape),
        scratch_shapes=list(scratch_shapes) + [pltpu.SemaphoreType.DMA((side.n_sems,))] * 2,
        compiler_params=_params("arbitrary"))(*args, *side.inputs)
    return list(outs[:n_out]), list(outs[n_out:])


def _ffn_forward(h, g_pre, wg_t, wu_t, w_down, g_post, target, name, side):
    t, d = h.shape
    k = w_down.shape[0]
    tm = 256

    def body(*refs):
        if target is None:
            h_ref, gpre_ref, wg_ref, wu_ref, wd_ref, gpost_ref, z_ref, gu_ref, y_ref, o_ref = refs
        else:
            h_ref, gpre_ref, wg_ref, wu_ref, wd_ref, gpost_ref, t_ref, z_ref, gu_ref, y_ref, o_ref, l_ref = refs

            @pl.when(pl.program_id(0) == 0)
            def _():
                l_ref[...] = jnp.zeros_like(l_ref)

        hv = h_ref[...]
        xhat, _ = _rms(hv)
        z = (xhat * gpre_ref[...]).astype(BF16)
        z_ref[...] = z
        gate = lax.dot_general(z, wg_ref[...], NT_DIMS, preferred_element_type=F32).astype(BF16)
        up = lax.dot_general(z, wu_ref[...], NT_DIMS, preferred_element_type=F32).astype(BF16)
        gu_ref[:, :k] = gate
        gu_ref[:, k:] = up
        gate = gate.astype(F32)
        act = (gate * _sigmoid(gate) * up.astype(F32)).astype(BF16)
        y = jnp.dot(act, wd_ref[...], preferred_element_type=F32)
        y_ref[...] = y
        yhat, _ = _rms(y)
        out = hv + yhat * gpost_ref[...]
        if target is None:
            o_ref[...] = out
        else:
            err = out - t_ref[...]
            o_ref[...] = err * (1.0 / d)
            part = jnp.sum(jnp.sum(err * err, axis=1, keepdims=True), axis=0, keepdims=True) * (0.5 / d)
            l_ref[...] += jnp.broadcast_to(part, l_ref.shape)

    in_specs = [_rows(tm, d), _full((1, d)), _full((k, d)), _full((k, d)), _full((k, d)), _full((1, d))]
    out_specs = [_rows(tm, d), _rows(tm, 2 * k), _rows(tm, d), _rows(tm, d)]
    out_shape = [_sds((t, d), BF16), _sds((t, 2 * k), BF16), _sds((t, d), F32), _sds((t, d), F32)]
    args = (h, g_pre, wg_t, wu_t, w_down, g_post)
    if target is not None:
        in_specs, args = in_specs + [_rows(tm, d)], args + (target,)
        out_specs, out_shape = out_specs + [_full((1, 128))], out_shape + [_sds((1, 128), F32)]
    return _side_call(body, side, name=name, steps=t // tm, in_specs=in_specs, out_specs=out_specs,
                      out_shape=out_shape, scratch_shapes=[], args=args)


def _bwd_out(dout, y, g, w, name, side):
    t, d = dout.shape
    k = w.shape[0]
    tm = min(512, t)

    def body(dout_ref, y_ref, g_ref, w_ref, dy_ref, da_ref, dg_ref):
        @pl.when(pl.program_id(0) == 0)
        def _():
            dg_ref[...] = jnp.zeros_like(dg_ref)

        do = dout_ref[...]
        yhat, r = _rms(y_ref[...])
        dg_ref[...] += jnp.sum(do * yhat, axis=0, keepdims=True)
        dy = _rms_bwd(yhat, r, g_ref[...], do).astype(BF16)
        dy_ref[...] = dy
        da_ref[...] = lax.dot_general(dy, w_ref[...], NT_DIMS, preferred_element_type=F32)

    return _side_call(
        body, side, name=name, steps=t // tm,
        in_specs=[_rows(tm, d), _rows(tm, d), _full((1, d)), _full((k, d))],
        out_specs=[_rows(tm, d), _rows(tm, k), _full((1, d))],
        out_shape=[_sds((t, d), BF16), _sds((t, k), F32), _sds((1, d), F32)],
        scratch_shapes=[], args=(dout, y, g, w))


def _ffn_backward(dout, y, g_post, w_down, gu, wg_t, wu_t, h, g_pre, name, side):
    t, d = dout.shape
    k = w_down.shape[0]
    tm = 256

    def body(dout_ref, y_ref, gpost_ref, wd_ref, gu_ref, wg_ref, wu_ref, h_ref, gpre_ref,
             dy_ref, dgu_ref, act_ref, dh_ref, dgpost_ref, dgpre_ref):
        @pl.when(pl.program_id(0) == 0)
        def _():
            dgpost_ref[...] = jnp.zeros_like(dgpost_ref)
            dgpre_ref[...] = jnp.zeros_like(dgpre_ref)

        do = dout_ref[...]
        yhat, r = _rms(y_ref[...])
        dgpost_ref[...] += jnp.sum(do * yhat, axis=0, keepdims=True)
        dy = _rms_bwd(yhat, r, gpost_ref[...], do).astype(BF16)
        dy_ref[...] = dy
        da = lax.dot_general(dy, wd_ref[...], NT_DIMS, preferred_element_type=F32)
        gate = gu_ref[:, :k].astype(F32)
        up = gu_ref[:, k:].astype(F32)
        sg = _sigmoid(gate)
        silu = gate * sg
        act_ref[...] = (silu * up).astype(BF16)
        dgate = (da * up * (sg * (1.0 + gate * (1.0 - sg)))).astype(BF16)
        dup = (da * silu).astype(BF16)
        dgu_ref[:, :k] = dgate
        dgu_ref[:, k:] = dup
        dz = (jnp.dot(dgate, wg_ref[...], preferred_element_type=F32)
              + jnp.dot(dup, wu_ref[...], preferred_element_type=F32))
        hhat, r2 = _rms(h_ref[...])
        dgpre_ref[...] += jnp.sum(dz * hhat, axis=0, keepdims=True)
        dh_ref[...] = do + _rms_bwd(hhat, r2, gpre_ref[...], dz)

    vec = _full((1, d))
    wspec = _full((k, d))
    return _side_call(
        body, side, name=name, steps=t // tm,
        in_specs=[_rows(tm, d), _rows(tm, d), vec, wspec, _rows(tm, 2 * k), wspec, wspec, _rows(tm, d), vec],
        out_specs=[_rows(tm, d), _rows(tm, 2 * k), _rows(tm, k), _rows(tm, d), vec, vec],
        out_shape=[_sds((t, d), BF16), _sds((t, 2 * k), BF16), _sds((t, k), BF16), _sds((t, d), F32),
                   _sds((1, d), F32), _sds((1, d), F32)],
        scratch_shapes=[], args=(dout, y, g_post, w_down, gu, wg_t, wu_t, h, g_pre))


def _bwd_in(dp, ws, transposed, h, g, dout, name, side):
    t, d = h.shape
    n = dp.shape[1]
    tm = min(512, t)
    widths = [w.shape[0] if transposed else w.shape[1] for w in ws]

    def body(dp_ref, *refs):
        w_refs = refs[:len(ws)]
        h_ref, g_ref, dout_ref, dh_ref, dg_ref = refs[len(ws):]

        @pl.when(pl.program_id(0) == 0)
        def _():
            dg_ref[...] = jnp.zeros_like(dg_ref)

        dz, at = None, 0
        for w_ref, width in zip(w_refs, widths):
            block = dp_ref[:, at:at + width]
            if transposed:
                part = jnp.dot(block, w_ref[...], preferred_element_type=F32)
            else:
                part = lax.dot_general(block, w_ref[...], NT_DIMS, preferred_element_type=F32)
            dz = part if dz is None else dz + part
            at += width
        hhat, r = _rms(h_ref[...])
        dg_ref[...] += jnp.sum(dz * hhat, axis=0, keepdims=True)
        dh_ref[...] = dout_ref[...] + _rms_bwd(hhat, r, g_ref[...], dz)

    return _side_call(
        body, side, name=name, steps=t // tm,
        in_specs=[_rows(tm, n)] + [_full(w.shape) for w in ws] + [_rows(tm, d), _full((1, d)), _rows(tm, d)],
        out_specs=[_rows(tm, d), _full((1, d))],
        out_shape=[_sds((t, d), F32), _sds((1, d), F32)],
        scratch_shapes=[], args=(dp, *ws, h, g, dout))


def _matmul_tn(a, b, tile_a, tile, name, window=None):
    t, ka = a.shape
    nb = b.shape[1]
    col0 = 0
    if window is not None:
        col0, ka = window

    def body(a_ref, b_ref, o_ref):
        o_ref[...] = lax.dot_general(a_ref[...], b_ref[...], TN_DIMS, preferred_element_type=F32).astype(BF16)

    if tile_a:
        grid = (ka // tile,)
        in_specs = [pl.BlockSpec((t, tile), lambda i: (0, col0 // tile + i)), _full((t, nb))]
        out_specs = pl.BlockSpec((tile, nb), lambda i: (i, 0))
    else:
        grid = (nb // tile,)
        in_specs = [_full((t, ka)), pl.BlockSpec((t, tile), lambda i: (0, i))]
        out_specs = pl.BlockSpec((ka, tile), lambda i: (0, i))
    return pl.pallas_call(
        body, name=name, grid=grid, in_specs=in_specs, out_specs=out_specs,
        out_shape=_sds((ka, nb), BF16), compiler_params=_params("parallel"),
    )(a, b)


def _ml_gate_prep(gt, bif):
    th = jnp.tanh((gt + bif) / GATE_CAP)
    act = GATE_CAP * th
    cum = _cumsum_rows(_log_sigmoid(act))
    lane = lax.broadcasted_iota(jnp.int32, gt.shape, 1)
    x = jnp.where(lane < ML_HEADS, act, cum)
    return x, x.T, th, act


HEADS = range(ML_HEADS)


def _each(fn, *per_head):
    return [fn(*a) for a in zip(*per_head)]


def _ml_chunk_fwd(q, k, v, kt, x, xt, c_in, n_in, m_in):
    causal = (lax.broadcasted_iota(jnp.int32, (CHUNK, CHUNK), 0)
              >= lax.broadcasted_iota(jnp.int32, (CHUNK, CHUNK), 1))
    f = {}
    qh = f["qh"] = [q[:, ML_DK * h:ML_DK * (h + 1)] for h in HEADS]
    kh = f["kh"] = [k[:, ML_DK * h:ML_DK * (h + 1)] for h in HEADS]
    f["vh"] = [v[:, ML_DV * h:ML_DV * (h + 1)] for h in HEADS]
    kth = [kt[ML_DK * h:ML_DK * (h + 1), :] for h in HEADS]
    s = _each(lambda a, b: jnp.dot(a, b, preferred_element_type=F32) * QK_SCALE, qh, kth)
    f["qc"] = _each(lambda a, c: jnp.dot(a, c.astype(BF16), preferred_element_type=F32) * QK_SCALE, qh, c_in)
    bcol = f["bcol"] = [x[:, ML_HEADS + h:ML_HEADS + h + 1] for h in HEADS]
    licol = f["licol"] = [x[:, h:h + 1] for h in HEADS]
    brow = f["brow"] = [xt[ML_HEADS + h:ML_HEADS + h + 1, :] for h in HEADS]
    lirow = [xt[h:h + 1, :] for h in HEADS]
    dmat = _each(lambda bc, br, lr: jnp.where(causal, bc - br + lr, NEG_BIG), bcol, brow, lirow)
    inter = _each(lambda bc, m: bc + m, bcol, m_in)
    mt = _each(lambda d, i: jnp.maximum(jnp.max(d, axis=1, keepdims=True), i), dmat, inter)
    wt = f["wt"] = _each(lambda d, m: jnp.exp(d - m), dmat, mt)
    p = f["p"] = _each(lambda a, b: a * b, wt, s)
    winter = f["winter"] = _each(lambda i, m: jnp.exp(i - m), inter, mt)
    qf = f["qf"] = [a.astype(F32) for a in qh]
    qn = f["qn"] = _each(lambda a, n: jnp.sum(a * n, axis=1, keepdims=True) * QK_SCALE, qf, n_in)
    den = f["den"] = _each(lambda a, w, b: jnp.sum(a, axis=1, keepdims=True) + w * b, p, winter, qn)
    emt = f["emt"] = [jnp.exp(-m) for m in mt]
    f["nrm"] = _each(lambda d, e: jnp.maximum(jnp.abs(d), e), den, emt)
    gtot = [bc[CHUNK - 1:CHUNK, :] for bc in bcol]
    a_col = _each(lambda g, bc, lc: g - bc + lc, gtot, bcol, licol)
    a_row = _each(lambda g, br, lr: g - br + lr, gtot, brow, lirow)
    m_new = f["m_new"] = _each(lambda g, m, a: jnp.maximum(g + m, jnp.max(a, axis=1, keepdims=True)),
                               gtot, m_in, a_row)
    f["decay"] = _each(lambda g, m, mn: jnp.exp(g + m - mn), gtot, m_in, m_new)
    wkf = f["wkf"] = _each(lambda a, mn: jnp.exp(a - mn), a_col, m_new)
    f["kw"] = _each(lambda a, w: a.astype(F32) * w, kh, wkf)
    f["ktw"] = _each(lambda a, ar, mn: (a.astype(F32) * jnp.exp(ar - mn)).astype(BF16), kth, a_row, m_new)
    return f


ML_STEP_CHUNKS = 4
ML_STEP = CHUNK * ML_STEP_CHUNKS


def _ml_specs(steps, rev):
    def at(col):
        if rev:
            return lambda i: (steps - 1 - i, col)
        return lambda i: (i, col)

    return [pl.BlockSpec((ML_STEP, ML_QK), at(0)), pl.BlockSpec((ML_STEP, ML_QK), at(1)),
            pl.BlockSpec((ML_STEP, ML_V), at(1)), pl.BlockSpec((ML_STEP, ML_V), at(0)),
            pl.BlockSpec((ML_STEP, 128), at(ML_V // 128))]


def _mlstm_fwd(qkv, og, bif, hn, side):
    t = qkv.shape[0]
    nc = t // CHUNK
    steps = t // ML_STEP

    def body(q_ref, k_ref, v_ref, o_ref, gt_ref, bif_ref, hn_ref, y_ref, hs_ref, cst_ref, nst_ref, mst_ref,
             c_sc, n_sc, m_sc):
        @pl.when(pl.program_id(0) == 0)
        def _():
            c_sc[...] = jnp.zeros_like(c_sc)
            n_sc[...] = jnp.zeros_like(n_sc)
            m_sc[...] = jnp.full_like(m_sc, ML_M_INIT)

        c_all, n_all, m_all = c_sc[...], n_sc[...], m_sc[...]
        c_in = [c_all[ML_DK * h:ML_DK * (h + 1), :] for h in HEADS]
        n_in = [n_all[h:h + 1, :] for h in HEADS]
        m_in = [m_all[h:h + 1, 0:1] for h in HEADS]
        for sub in range(ML_STEP_CHUNKS):
            rs = slice(CHUNK * sub, CHUNK * (sub + 1))
            for h in HEADS:
                cst_ref[sub, ML_DK * h:ML_DK * (h + 1), :] = c_in[h]
                nst_ref[sub, h:h + 1, :] = n_in[h]
                mst_ref[sub, h:h + 1, :] = jnp.broadcast_to(m_in[h], (1, 128))
            x, xt, _, _ = _ml_gate_prep(gt_ref[rs, :], bif_ref[...])
            q, k, v = q_ref[rs, :], k_ref[rs, :], v_ref[rs, :]
            f = _ml_chunk_fwd(q, k, v, k.T, x, xt, c_in, n_in, m_in)
            num = _each(lambda p, vh, w, qc: jnp.dot(p.astype(BF16), vh, preferred_element_type=F32) + w * qc,
                        f["p"], f["vh"], f["winter"], f["qc"])
            hh = _each(lambda a, b: a / b, num, f["nrm"])
            hhat = [_rms(a)[0] for a in hh]
            c_in = _each(lambda d, c, kw, vh: d * c + jnp.dot(kw, vh, preferred_element_type=F32),
                         f["decay"], c_in, f["ktw"], f["vh"])
            n_in = _each(lambda d, n, kw: d * n + jnp.sum(kw, axis=0, keepdims=True), f["decay"], n_in, f["kw"])
            m_in = f["m_new"]
            for h in HEADS:
                vs = slice(ML_DV * h, ML_DV * (h + 1))
                hs_ref[rs, vs] = hh[h]
                y_ref[rs, vs] = (hhat[h] * hn_ref[:, vs] * _sigmoid(o_ref[rs, vs])).astype(BF16)
        for h in HEADS:
            c_sc[ML_DK * h:ML_DK * (h + 1), :] = c_in[h]
            n_sc[h:h + 1, :] = n_in[h]
            m_sc[h:h + 1, :] = jnp.broadcast_to(m_in[h], (1, 128))

    return _side_call(
        body, side, name="mlstm_fwd", steps=steps,
        in_specs=_ml_specs(steps, False) + [_full((1, 128)), _full((1, ML_V))],
        out_specs=[_rows(ML_STEP, ML_V), _rows(ML_STEP, ML_V),
                   pl.BlockSpec((ML_STEP_CHUNKS, ML_HEADS * ML_DK, ML_DV), lambda i: (i, 0, 0)),
                   pl.BlockSpec((ML_STEP_CHUNKS, ML_HEADS, ML_DK), lambda i: (i, 0, 0)),
                   pl.BlockSpec((ML_STEP_CHUNKS, ML_HEADS, 128), lambda i: (i, 0, 0))],
        out_shape=[_sds((t, ML_V), BF16), _sds((t, ML_V), F32),
                   _sds((nc, ML_HEADS * ML_DK, ML_DV), F32), _sds((nc, ML_HEADS, ML_DK), F32),
                   _sds((nc, ML_HEADS, 128), F32)],
        scratch_shapes=[pltpu.VMEM((ML_HEADS * ML_DK, ML_DV), F32), pltpu.VMEM((ML_HEADS, ML_DK), F32),
                        pltpu.VMEM((ML_HEADS, 128), F32)],
        args=(qkv, qkv, qkv, og, og, bif, hn))


def _mlstm_bwd(qkv, og, bif, hn, hs, cst, nst, mst, dy, side):
    t = qkv.shape[0]
    steps = t // ML_STEP

    def chunk(q_ref, k_ref, v_ref, o_ref, gt_ref, bif_ref, hn_ref, hs_ref, cst_ref, nst_ref, mst_ref, dy_ref,
              dp_ref, dhn_ref, dbif_ref, carried):
        x, xt, th, act = _ml_gate_prep(gt_ref[...], bif_ref[...])
        q, k, v = q_ref[...], k_ref[...], v_ref[...]
        qt, vt = q.T, v.T
        rows = lax.broadcasted_iota(jnp.int32, (CHUNK, CHUNK), 0)
        cols = lax.broadcasted_iota(jnp.int32, (CHUNK, CHUNK), 1)
        lane = lax.broadcasted_iota(jnp.int32, (CHUNK, 128), 1)
        row = lax.broadcasted_iota(jnp.int32, (CHUNK, 1), 0)
        as_row = lambda col: jnp.sum(jnp.where(rows == cols, col, 0.0), axis=0, keepdims=True)
        mm = lambda a, b: jnp.dot(a, b, preferred_element_type=F32)
        bf = lambda a: a.astype(BF16)
        ksl = [slice(ML_DK * h, ML_DK * (h + 1)) for h in HEADS]
        vsl = [slice(ML_DV * h, ML_DV * (h + 1)) for h in HEADS]
        c_in = [cst_ref[0, s, :] for s in ksl]
        n_in = [nst_ref[0, h:h + 1, :] for h in HEADS]
        m_in = [mst_ref[0, h:h + 1, 0:1] for h in HEADS]
        dcn, dcn_t, dnn = carried
        f = _ml_chunk_fwd(q, k, v, k.T, x, xt, c_in, n_in, m_in)
        qh, kh, vh, p, winter, decay = f["qh"], f["kh"], f["vh"], f["p"], f["winter"], f["decay"]
        qth = [qt[s, :] for s in ksl]
        vth = [vt[s, :] for s in vsl]
        c_t = [bf(c.T) for c in c_in]
        dmat_t = _each(lambda br, bc, lc: jnp.where(rows <= cols, br - bc + lc, NEG_BIG),
                       f["brow"], f["bcol"], f["licol"])
        inter_row = _each(lambda br, m: br + m, f["brow"], m_in)
        mt_row = _each(lambda d, i: jnp.maximum(jnp.max(d, axis=0, keepdims=True), i), dmat_t, inter_row)
        wt_t = _each(lambda d, m: jnp.exp(d - m), dmat_t, mt_row)
        p_t = _each(lambda w, a, b: w * (mm(a, b) * QK_SCALE), wt_t, kh, qth)
        winter_row = _each(lambda i, m: jnp.exp(i - m), inter_row, mt_row)
        hh = [hs_ref[:, s] for s in vsl]
        hn_h = [hn_ref[:, s] for s in vsl]
        sg = [_sigmoid(o_ref[:, s]) for s in vsl]
        dyh = [dy_ref[:, s] for s in vsl]
        norm = [_rms(a) for a in hh]
        hhat, r = [a for a, _ in norm], [b for _, b in norm]
        dyn = _each(lambda a, b: a * b, dyh, sg)
        do = _each(lambda d, hx, g, s: d * hx * g * s * (1.0 - s), dyh, hhat, hn_h, sg)
        dhn = _each(lambda a, b: jnp.sum(a * b, axis=0, keepdims=True), dyn, hhat)
        dh = _each(_rms_bwd, hhat, r, hn_h, dyn)
        inv = [1.0 / a for a in f["nrm"]]
        dnum = _each(lambda a, b: a * b, dh, inv)
        dnrm = _each(lambda a, b, c: -jnp.sum(a * b, axis=1, keepdims=True) * c, dh, hh, inv)
        dden = _each(lambda d, e, g: jnp.where(jnp.abs(d) > e, g * jnp.sign(d), 0.0), f["den"], f["emt"], dnrm)
        dnb = [bf(a) for a in dnum]
        dnt = [bf(a.T) for a in dnum]
        rmat = _each(lambda a, b, d: mm(a, b) + d, dnb, vth, dden)
        rmat_t = _each(lambda a, b, d: mm(a, b) + as_row(d), vh, dnt, dden)
        ds = _each(lambda w, a: bf(w * a), f["wt"], rmat)
        ds_t = _each(lambda w, a: bf(w * a), wt_t, rmat_t)
        dv = _each(lambda a, b: mm(bf(a), b), p_t, dnb)
        dqs = _each(lambda s, kk, w, d, ct, dd, n: mm(s, kk) + w * (mm(d, ct) + dd * n),
                    ds, kh, winter, dnb, c_t, dden, n_in)
        dk = _each(lambda s, a: mm(s, a) * QK_SCALE, ds_t, qh)
        dinter = _each(lambda qc, dn_, qn, dd, w: (jnp.sum(qc * dn_, axis=1, keepdims=True) + qn * dd) * w,
                       f["qc"], dnum, f["qn"], dden, winter)
        wq = _each(lambda w, a: w * a * QK_SCALE, winter, f["qf"])
        wq_t = _each(lambda w, a: bf(w * a.astype(F32) * QK_SCALE), winter_row, qth)
        dc_loc = _each(mm, wq_t, dnb)
        dct_loc = _each(lambda a, b: mm(a, bf(b)), dnt, wq)
        dn_loc = _each(lambda a, d: jnp.sum(a * d, axis=0, keepdims=True), wq, dden)
        cs_q = _each(lambda a, b: jnp.sum(a * b, axis=1, keepdims=True), p_t, rmat_t)
        db = _each(lambda a, b, di, cs: jnp.sum(a * b, axis=1, keepdims=True) + di - cs, p, rmat, dinter, cs_q)
        ddecay = _each(lambda dc_, c, dn_, n: jnp.sum(jnp.sum(dc_ * c, axis=1, keepdims=True), axis=0, keepdims=True)
                       + jnp.sum(dn_ * n, axis=1, keepdims=True), dcn, c_in, dnn, n_in)
        dkw = _each(lambda a, b, n: mm(a, bf(b)) + n, vh, dcn_t, dnn)
        dk = _each(lambda a, w, b: a + w * b, dk, f["wkf"], dkw)
        da = _each(lambda a, kk, w: jnp.sum(a * kk.astype(F32), axis=1, keepdims=True) * w, dkw, kh, f["wkf"])
        dv = _each(lambda a, kw, dc_: a + mm(bf(kw), bf(dc_)), dv, f["kw"], dcn)
        dgtot = _each(lambda a, dd, d: jnp.sum(a, axis=0, keepdims=True) + dd * d, da, ddecay, decay)
        db = _each(lambda a, b, g: a - b + jnp.where(row == CHUNK - 1, g, 0.0), db, da, dgtot)
        dli = _each(lambda a, b: a + b, cs_q, da)
        dc_new = _each(lambda d, a, b: d * a + b, decay, dcn, dc_loc)
        dct_new = _each(lambda d, a, b: d * a + b, decay, dcn_t, dct_loc)
        dn_new = _each(lambda d, a, b: d * a + b, decay, dnn, dn_loc)
        dx = jnp.zeros((CHUNK, 128), F32)
        for h in HEADS:
            dhn_ref[:, vsl[h]] += dhn[h]
            dp_ref[:, ksl[h]] = bf(dqs[h] * QK_SCALE)
            dp_ref[:, ML_QK + ML_DK * h:ML_QK + ML_DK * (h + 1)] = bf(dk[h])
            dp_ref[:, 2 * ML_QK + ML_DV * h:2 * ML_QK + ML_DV * (h + 1)] = bf(dv[h])
            dp_ref[:, 2 * ML_QK + ML_V + ML_DV * h:2 * ML_QK + ML_V + ML_DV * (h + 1)] = bf(do[h])
            dx = jnp.where(lane == h, dli[h], dx)
            dx = jnp.where(lane == ML_HEADS + h, db[h], dx)
        dlf = _cumsum_rows(dx, reverse=True)
        dact = jnp.where(lane < ML_HEADS, dx, dlf * _sigmoid(-act))
        dz = dact * (1.0 - th * th)
        dp_ref[:, 2 * ML_QK + 2 * ML_V:] = dz.astype(BF16)
        dbif_ref[...] += jnp.sum(dz, axis=0, keepdims=True)
        return dc_new, dct_new, dn_new

    def body(q_ref, k_ref, v_ref, o_ref, gt_ref, bif_ref, hn_ref, hs_ref, cst_ref, nst_ref, mst_ref, dy_ref,
             dp_ref, dhn_ref, dbif_ref, dc_sc, dct_sc, dn_sc):
        @pl.when(pl.program_id(0) == 0)
        def _():
            dc_sc[...] = jnp.zeros_like(dc_sc)
            dct_sc[...] = jnp.zeros_like(dct_sc)
            dn_sc[...] = jnp.zeros_like(dn_sc)
            dhn_ref[...] = jnp.zeros_like(dhn_ref)
            dbif_ref[...] = jnp.zeros_like(dbif_ref)

        dc_all, dct_all, dn_all = dc_sc[...], dct_sc[...], dn_sc[...]
        carried = ([dc_all[ML_DK * h:ML_DK * (h + 1), :] for h in HEADS],
                   [dct_all[ML_DV * h:ML_DV * (h + 1), :] for h in HEADS],
                   [dn_all[h:h + 1, :] for h in HEADS])
        for sub in reversed(range(ML_STEP_CHUNKS)):
            rs = pl.ds(CHUNK * sub, CHUNK)
            one = pl.ds(sub, 1)
            carried = chunk(q_ref.at[rs], k_ref.at[rs], v_ref.at[rs], o_ref.at[rs], gt_ref.at[rs], bif_ref, hn_ref,
                            hs_ref.at[rs], cst_ref.at[one], nst_ref.at[one], mst_ref.at[one], dy_ref.at[rs],
                            dp_ref.at[rs], dhn_ref, dbif_ref, carried)
        for h in HEADS:
            dc_sc[ML_DK * h:ML_DK * (h + 1), :] = carried[0][h]
            dct_sc[ML_DV * h:ML_DV * (h + 1), :] = carried[1][h]
            dn_sc[h:h + 1, :] = carried[2][h]

    rev = lambda i: (steps - 1 - i, 0)
    rev3 = lambda i: (steps - 1 - i, 0, 0)
    return _side_call(
        body, side, name="mlstm_bwd", steps=steps,
        in_specs=_ml_specs(steps, True) + [
            _full((1, 128)), _full((1, ML_V)), pl.BlockSpec((ML_STEP, ML_V), rev),
            pl.BlockSpec((ML_STEP_CHUNKS, ML_HEADS * ML_DK, ML_DV), rev3),
            pl.BlockSpec((ML_STEP_CHUNKS, ML_HEADS, ML_DK), rev3),
            pl.BlockSpec((ML_STEP_CHUNKS, ML_HEADS, 128), rev3), pl.BlockSpec((ML_STEP, ML_V), rev)],
        out_specs=[pl.BlockSpec((ML_STEP, ML_IN_PAD), rev), _full((1, ML_V)), _full((1, 128))],
        out_shape=[_sds((t, ML_IN_PAD), BF16), _sds((1, ML_V), F32), _sds((1, 128), F32)],
        scratch_shapes=[pltpu.VMEM((ML_HEADS * ML_DK, ML_DV), F32), pltpu.VMEM((ML_HEADS * ML_DV, ML_DK), F32),
                        pltpu.VMEM((ML_HEADS, ML_DK), F32)],
        args=(qkv, qkv, qkv, og, og, bif, hn, hs, cst, nst, mst, dy))


LRU_TM = 256
GELU_K = math.sqrt(2.0 / math.pi)
GELU_C = 0.044715


def _gelu(x):
    th = jnp.tanh(GELU_K * (x + GELU_C * x * x * x))
    return 0.5 * x * (1.0 + th), th


def _neg_expm1(x):
    series = -x * (1.0 + x * (0.5 + x * (1.0 / 6.0 + x * (1.0 / 24.0))))
    return jnp.where(x > -0.05, series, 1.0 - jnp.exp(x))


def _block_diag_dot(a, w_ref, dims):
    parts = [lax.dot_general(a[:, LRU_BLOCK * n:LRU_BLOCK * (n + 1)], w_ref[n], dims, preferred_element_type=F32)
             for n in range(LRU_BLOCKS)]
    return jnp.concatenate(parts, axis=1)


def _lru_gates(r, lam):
    ls = _log_sigmoid(lam)
    la = LRU_C * r * ls
    a = jnp.exp(la)
    em = _neg_expm1(2.0 * la)
    mult = jnp.sqrt(em)
    return ls, a, em, mult


def _lru_fwd(proj, cw, cb, wa, ba, wx, bx, lam, side):
    t = proj.shape[0]
    w = D_MODEL
    tm = min(LRU_TM, t)

    def body(gb_ref, up_ref, cw_ref, cb_ref, wa_ref, ba_ref, wx_ref, bx_ref, lam_ref,
             y_ref, u_ref, r_ref, i_ref, h_ref, tail_sc, hprev_sc):
        @pl.when(pl.program_id(0) == 0)
        def _():
            tail_sc[...] = jnp.zeros_like(tail_sc)
            hprev_sc[...] = jnp.zeros_like(hprev_sc)

        up = up_ref[...]
        ext = jnp.concatenate([tail_sc[...], up], axis=0)
        u = cb_ref[...] + cw_ref[CONV_WIDTH - 1:CONV_WIDTH, :] * up
        for s in range(1, CONV_WIDTH):
            u = u + cw_ref[CONV_WIDTH - 1 - s:CONV_WIDTH - s, :] * pltpu.roll(ext, s, 0)[8:8 + tm]
        tail_sc[...] = up[tm - 8:tm]
        ub = u.astype(BF16)
        r = _sigmoid(_block_diag_dot(ub, wa_ref, (((1,), (0,)), ((), ()))) + ba_ref[...])
        ig = _sigmoid(_block_diag_dot(ub, wx_ref, (((1,), (0,)), ((), ()))) + bx_ref[...])
        _, a, _, mult = _lru_gates(r, lam_ref[...])
        h = _scan_rows(a, mult * ig * u, hprev_sc[0:1, :])
        hprev_sc[0:1, :] = h[tm - 1:tm]
        u_ref[...] = u
        r_ref[...] = r
        i_ref[...] = ig
        h_ref[...] = h
        gel, _ = _gelu(gb_ref[...])
        y_ref[...] = (h * gel).astype(BF16)

    vec = _full((1, w))
    wspec = _full((LRU_BLOCKS, LRU_BLOCK, LRU_BLOCK))
    return _side_call(
        body, side, name="lru_fwd", steps=t // tm,
        in_specs=[pl.BlockSpec((tm, w), lambda i: (i, 0)), pl.BlockSpec((tm, w), lambda i: (i, 1)),
                  _full((CONV_WIDTH, w)), vec, wspec, vec, wspec, vec, vec],
        out_specs=[_rows(tm, w)] * 5,
        out_shape=[_sds((t, w), BF16)] + [_sds((t, w), F32)] * 4,
        scratch_shapes=[pltpu.VMEM((8, w), F32), pltpu.VMEM((8, w), F32)],
        args=(proj, proj, cw, cb, wa, ba, wx, bx, lam))


def _lru_bwd(proj, cw, wa, wx, lam, u, r, ig, h, dy, side):
    t = proj.shape[0]
    w = D_MODEL
    tm = min(LRU_TM, t)
    nt = t // tm

    def body(gb_ref, up_ref, cw_ref, wa_ref, wx_ref, lam_ref, u_ref, r_ref, i_ref, h_ref, hp_ref, dy_ref,
             dp_ref, dcw_ref, dcb_ref, dwa_ref, dba_ref, dwx_ref, dbx_ref, dlam_ref, carry_sc, dutail_sc, dls_sc):
        step = pl.program_id(0)

        @pl.when(step == 0)
        def _():
            carry_sc[...] = jnp.zeros_like(carry_sc)
            dutail_sc[...] = jnp.zeros_like(dutail_sc)
            dls_sc[...] = jnp.zeros_like(dls_sc)
            for ref in (dcw_ref, dcb_ref, dwa_ref, dba_ref, dwx_ref, dbx_ref):
                ref[...] = jnp.zeros_like(ref)

        row = lax.broadcasted_iota(jnp.int32, (tm, w), 0)
        u_t, r_t, i_t, h_t = u_ref[...], r_ref[...], i_ref[...], h_ref[...]
        ls, a, em, mult = _lru_gates(r_t, lam_ref[...])
        gb = gb_ref[...]
        gel, th = _gelu(gb)
        dyv = dy_ref[...]
        dgb = dyv * h_t * (0.5 * (1.0 + th) + 0.5 * gb * (1.0 - th * th) * GELU_K * (1.0 + 3.0 * GELU_C * gb * gb))
        a_next = jnp.where(row < tm - 1, pltpu.roll(a, tm - 1, 0), 1.0)
        g = _scan_rows(a_next, dyv * gel, carry_sc[0:1, :], reverse=True)
        carry_sc[0:1, :] = a[0:1] * g[0:1]
        has_prev = jnp.where(step == nt - 1, 0.0, 1.0)
        h_prev = jnp.where(row >= 1, pltpu.roll(h_t, 1, 0), hp_ref[7:8, :] * has_prev)
        dmult = g * i_t * u_t
        dig = g * mult * u_t
        du = g * mult * i_t
        dla = g * h_prev * a - dmult * (1.0 - em) / mult
        dls_sc[0:1, :] += jnp.sum(dla * r_t, axis=0, keepdims=True) * LRU_C
        dpa = dla * (LRU_C * ls) * r_t * (1.0 - r_t)
        dpx = dig * i_t * (1.0 - i_t)
        dba_ref[...] += jnp.sum(dpa, axis=0, keepdims=True)
        dbx_ref[...] += jnp.sum(dpx, axis=0, keepdims=True)
        ub = u_t.astype(BF16)
        dpab = dpa.astype(BF16)
        dpxb = dpx.astype(BF16)
        for n in range(LRU_BLOCKS):
            cs = slice(LRU_BLOCK * n, LRU_BLOCK * (n + 1))
            dwa_ref[n] += lax.dot_general(ub[:, cs], dpab[:, cs], TN_DIMS, preferred_element_type=F32)
            dwx_ref[n] += lax.dot_general(ub[:, cs], dpxb[:, cs], TN_DIMS, preferred_element_type=F32)
        du = du + _block_diag_dot(dpab, wa_ref, NT_DIMS) + _block_diag_dot(dpxb, wx_ref, NT_DIMS)
        dcb_ref[...] += jnp.sum(du, axis=0, keepdims=True)
        ext = jnp.concatenate([du, dutail_sc[...]], axis=0)
        up = up_ref[...]
        dup = cw_ref[CONV_WIDTH - 1:CONV_WIDTH, :] * du
        dcw_ref[CONV_WIDTH - 1:CONV_WIDTH, :] += jnp.sum(up * du, axis=0, keepdims=True)
        for s in range(1, CONV_WIDTH):
            du_s = pltpu.roll(ext, tm + 8 - s, 0)[0:tm]
            dup = dup + cw_ref[CONV_WIDTH - 1 - s:CONV_WIDTH - s, :] * du_s
            dcw_ref[CONV_WIDTH - 1 - s:CONV_WIDTH - s, :] += jnp.sum(up * du_s, axis=0, keepdims=True)
        dutail_sc[...] = du[0:8]
        dp_ref[:, :w] = dgb.astype(BF16)
        dp_ref[:, w:] = dup.astype(BF16)

        @pl.when(step == nt - 1)
        def _():
            dlam_ref[...] = dls_sc[0:1, :] * _sigmoid(-lam_ref[...])

    rev = lambda col: (lambda i: (nt - 1 - i, col))
    vec = _full((1, w))
    wspec = _full((LRU_BLOCKS, LRU_BLOCK, LRU_BLOCK))
    tile = pl.BlockSpec((tm, w), rev(0))
    prev8 = pl.BlockSpec((8, w), lambda i: (jnp.maximum((nt - 1 - i) * (tm // 8) - 1, 0), 0))
    return _side_call(
        body, side, name="lru_bwd", steps=nt,
        in_specs=[tile, pl.BlockSpec((tm, w), rev(1)), _full((CONV_WIDTH, w)), wspec, wspec, vec,
                  tile, tile, tile, tile, prev8, tile],
        out_specs=[pl.BlockSpec((tm, 2 * w), rev(0)), _full((CONV_WIDTH, w)), vec, wspec, vec, wspec, vec, vec],
        out_shape=[_sds((t, 2 * w), BF16), _sds((CONV_WIDTH, w), F32), _sds((1, w), F32),
                   _sds((LRU_BLOCKS, LRU_BLOCK, LRU_BLOCK), F32), _sds((1, w), F32),
                   _sds((LRU_BLOCKS, LRU_BLOCK, LRU_BLOCK), F32), _sds((1, w), F32), _sds((1, w), F32)],
        scratch_shapes=[pltpu.VMEM((8, w), F32), pltpu.VMEM((8, w), F32), pltpu.VMEM((8, w), F32)],
        args=(proj, proj, cw, wa, wx, lam, u, r, ig, h, h, dy))


LANES = 1024
HALF_FFN = D_FF // N_CHIPS
GROUPS = {
    "ml": (("ml_w_in", 1024, ML_IN // N_CHIPS), ("ml_w_out", 256, 1024)),
    "lru": (("lru_w_in", 1024, 512),
            ("lru_gates", 2 * LRU_BLOCKS * 64, LRU_BLOCK),
            ("lru_w_out", 256, 1024)),
    "ffn0": (("ffn_g0", HALF_FFN, 1024), ("ffn_u0", HALF_FFN, 1024), ("ffn_down0", HALF_FFN, 1024)),
    "ffn1": (("ffn_g1", HALF_FFN, 1024), ("ffn_u1", HALF_FFN, 1024), ("ffn_down1", HALF_FFN, 1024)),
}

SMALL_ROWS = 24
ROW_LOSS, ROW_HEAD_NORM, ROW_B_IF, ROW_LRU = 8, 9, 10, 11


def _row_tile(rows, cols, itemsize, budget=3 << 19):
    best = 16
    for t in range(16, rows + 1, 16):
        if rows % t == 0 and t * cols * itemsize <= budget:
            best = t
    return best


def _chip_peers():
    x, y, c = lax.axis_index("x"), lax.axis_index("y"), lax.axis_index("c")
    return x, y, c, [(1 - x, y), (x, 1 - y), (1 - x, 1 - y)]


HBM_SPEC = pl.BlockSpec(memory_space=pltpu.HBM)


def _remote(src, dst, send_sems, recv_sems, k, to):
    return pltpu.make_async_remote_copy(src_ref=src, dst_ref=dst, send_sem=send_sems.at[k], recv_sem=recv_sems.at[k],
                                        device_id=to, device_id_type=MESH)


GATHER_SEMS = 7


def _gather_copies(kind, specs, part_refs, out_refs, send_sems, recv_sems):
    x, y, c, chips = _chip_peers()
    me = 2 * x + y
    sib = (x, y, 1 - c)
    copy = functools.partial(_remote, send_sems=send_sems, recv_sems=recv_sems)
    out = []
    for p, (_, rows, _) in enumerate(specs):
        mine = pl.ds(c * (rows // 2), rows // 2)
        theirs = pl.ds((1 - c) * (rows // 2), rows // 2)
        base = GATHER_SEMS * p
        for j, (cx, cy) in enumerate(chips):
            land = out_refs[p].at[2 * cx + cy, mine]
            other = out_refs[p].at[2 * cx + cy, theirs]
            if kind == "first":
                out.append(copy(part_refs[p].at[mine], out_refs[p].at[me, mine], k=base + j, to=(cx, cy, c)))
            elif kind == "landed":
                out.append(copy(land, land, k=base + j, to=sib))
            elif kind == "forward":
                out.append(copy(land, land, k=base + 3 + j, to=sib))
            else:
                out.append(copy(other, other, k=base + 3 + j, to=sib))
        if kind in ("first", "arriving"):
            out.append(copy(part_refs[p], out_refs[p].at[me], k=base + 6, to=sib))
    return out


def _gather_start(specs, *refs):
    for cp in _gather_copies("first", specs, *refs):
        cp.start()


def _gather_finish(specs, *refs):
    forwards = _gather_copies("forward", specs, *refs)
    for land, fwd in zip(_gather_copies("landed", specs, *refs), forwards):
        land.wait_recv()
        fwd.start()
    for cp in _gather_copies("arriving", specs, *refs):
        cp.wait_recv()
    for cp in _gather_copies("first", specs, *refs) + forwards:
        cp.wait_send()


def _gather_shapes(specs):
    return [_sds((N_CHIPS, rows, cols), BF16) for _, rows, cols in specs]


def _gather_side(specs, parts):
    return _Side(inputs=list(parts), out_shape=_gather_shapes(specs), n_sems=GATHER_SEMS * len(specs),
                 start=functools.partial(_gather_start, specs), finish=functools.partial(_gather_finish, specs))


def _gather_weights(specs, parts, small):
    n = len(specs)

    def body(*refs):
        part_refs, small_ref = refs[:n], refs[n]
        out_refs, outs_ref = refs[n + 1:2 * n + 1], refs[2 * n + 1]
        send_sems, recv_sems, small_send, small_recv, loc_sem = refs[2 * n + 2:]
        x, y, c, chips = _chip_peers()
        me = 2 * x + y
        local = pltpu.make_async_copy(small_ref, outs_ref.at[me], loc_sem.at[0])
        local.start()
        _gather_start(specs, part_refs, out_refs, send_sems, recv_sems)
        sent = [_remote(small_ref, outs_ref.at[me], small_send, small_recv, j, (cx, cy, c))
                for j, (cx, cy) in enumerate(chips)]
        for cp in sent:
            cp.start()
        _gather_finish(specs, part_refs, out_refs, send_sems, recv_sems)
        for j, (cx, cy) in enumerate(chips):
            _remote(small_ref, outs_ref.at[2 * cx + cy], small_send, small_recv, j, (cx, cy, c)).wait_recv()
        for cp in sent:
            cp.wait_send()
        local.wait()

    dma = pltpu.SemaphoreType.DMA
    return pl.pallas_call(
        body, name="gather_weights",
        in_specs=[HBM_SPEC] * (n + 1), out_specs=[HBM_SPEC] * (n + 1),
        out_shape=_gather_shapes(specs) + [_sds((N_CHIPS,) + small.shape, small.dtype)],
        scratch_shapes=[dma((GATHER_SEMS * n,)), dma((GATHER_SEMS * n,)), dma((3,)), dma((3,)), dma((1,))],
    )(*parts, small)


def _exchange_copies(specs, g_refs, out_refs, send_sems, recv_sems):
    x, y, c, _ = _chip_peers()
    return [_remote(g_refs[p].at[:, pl.ds((1 - c) * (rows // 2), rows // 2)], out_refs[p], send_sems, recv_sems, p,
                    (x, y, 1 - c)) for p, (_, rows, _) in enumerate(specs)]


def _exchange_start(specs, *refs):
    for cp in _exchange_copies(specs, *refs):
        cp.start()


def _exchange_finish(specs, *refs):
    for cp in _exchange_copies(specs, *refs):
        cp.wait()


def _exchange_shapes(specs):
    return [_sds((N_CHIPS, rows // 2, cols), BF16) for _, rows, cols in specs]


def _exchange_side(specs, gparts):
    return _Side(inputs=list(gparts), out_shape=_exchange_shapes(specs), n_sems=len(specs),
                 start=functools.partial(_exchange_start, specs), finish=functools.partial(_exchange_finish, specs))


def _exchange_halves(specs, gparts, name):
    n = len(specs)

    def body(*refs):
        _exchange_start(specs, refs[:n], refs[n:2 * n], *refs[2 * n:])
        _exchange_finish(specs, refs[:n], refs[n:2 * n], *refs[2 * n:])

    return pl.pallas_call(
        body, name=name, in_specs=[HBM_SPEC] * n, out_specs=[HBM_SPEC] * n, out_shape=_exchange_shapes(specs),
        scratch_shapes=[pltpu.SemaphoreType.DMA((n,)), pltpu.SemaphoreType.DMA((n,))],
    )(*gparts)


def _add_halves(gs, recvs, pos, name):
    n = len(gs)
    _, half, cols = recvs[0].shape
    tr = _row_tile(half, cols, 2)
    tiles = half // tr

    def body(pos_ref, *refs):
        for a_ref, b_ref, o_ref in zip(refs[:n], refs[n:2 * n], refs[2 * n:]):
            o_ref[...] = (a_ref[...].astype(F32) + b_ref[...].astype(F32)).astype(BF16)

    spec = pl.BlockSpec((1, tr, cols), lambda k, i, pos_ref: (k, i, 0))
    mine = pl.BlockSpec((1, tr, cols), lambda k, i, pos_ref: (k, pos_ref[0] * tiles + i, 0))
    return pl.pallas_call(
        body, name=name,
        grid_spec=pltpu.PrefetchScalarGridSpec(
            num_scalar_prefetch=1, grid=(N_CHIPS, tiles), in_specs=[mine] * n + [spec] * n, out_specs=[spec] * n),
        out_shape=[_sds((N_CHIPS, half, cols), BF16)] * n,
        compiler_params=_params("parallel", "parallel"),
    )(pos, *gs, *recvs)


def _scatter_copies(n, s1_refs, recv_refs, send_sems, recv_sems):
    x, y, c, chips = _chip_peers()
    return [_remote(s1_refs[p].at[2 * cx + cy], recv_refs[p].at[j], send_sems, recv_sems, 3 * p + j, (cx, cy, c))
            for p in range(n) for j, (cx, cy) in enumerate(chips)]


def _scatter_start(n, s1_refs, recv_refs, send_sems, recv_sems):
    for cp in _scatter_copies(n, s1_refs, recv_refs, send_sems, recv_sems):
        cp.start()


def _scatter_finish(n, s1_refs, recv_refs, send_sems, recv_sems):
    for cp in _scatter_copies(n, s1_refs, recv_refs, send_sems, recv_sems):
        cp.wait()


def _scatter_shapes(s1):
    return [_sds((3,) + a.shape[1:], a.dtype) for a in s1]


def _scatter_side(s1):
    n = len(s1)
    return _Side(inputs=list(s1), out_shape=_scatter_shapes(s1), n_sems=3 * n,
                 start=functools.partial(_scatter_start, n), finish=functools.partial(_scatter_finish, n))


def _sum_chips(s1s, recvs, pos, name):
    n = len(s1s)
    _, half, cols = recvs[0].shape
    tr = _row_tile(half, cols, 4)
    tiles = half // tr

    def body(pos_ref, *refs):
        for a_ref, b_ref, o_ref in zip(refs[:n], refs[n:2 * n], refs[2 * n:]):
            acc = a_ref[0].astype(F32)
            for j in range(3):
                acc = acc + b_ref[j].astype(F32)
            o_ref[...] = acc

    return pl.pallas_call(
        body, name=name,
        grid_spec=pltpu.PrefetchScalarGridSpec(
            num_scalar_prefetch=1, grid=(tiles,),
            in_specs=[pl.BlockSpec((1, tr, cols), lambda i, pos_ref: (pos_ref[1], i, 0))] * n
            + [pl.BlockSpec((3, tr, cols), lambda i, pos_ref: (0, i, 0))] * n,
            out_specs=[pl.BlockSpec((tr, cols), lambda i, pos_ref: (pos_ref[0] * tiles + i, 0))] * n),
        out_shape=[_sds((2 * half, cols), F32)] * n,
        compiler_params=_params("parallel"),
    )(pos, *s1s, *recvs)


def _sum_small(small_all):
    def body(a_ref, o_ref):
        acc = a_ref[0]
        for d in range(1, 8):
            acc = acc + a_ref[d]
        o_ref[...] = acc

    return pl.pallas_call(body, name="sum_small", out_shape=_sds(small_all.shape[1:], F32))(small_all)


def _join_halves(specs, s2, small):
    n = len(specs)
    flips = [(fx, fy, fc) for fx in (0, 1) for fy in (0, 1) for fc in (0, 1)][1:]

    def body(*refs):
        small_ref, all_ref = refs[n], refs[2 * n + 1]
        buf_refs = refs[n + 1:2 * n + 1]
        send_sems, recv_sems, small_send, small_recv, loc_sem = refs[2 * n + 2:]
        x, y, c, _ = _chip_peers()
        my_slot = all_ref.at[4 * x + 2 * y + c]
        local = pltpu.make_async_copy(small_ref, my_slot, loc_sem.at[0])
        local.start()
        sent = []
        for p, (_, rows, _) in enumerate(specs):
            mine = buf_refs[p].at[pl.ds(c * (rows // 2), rows // 2)]
            sent.append(_remote(mine, mine, send_sems, recv_sems, p, (x, y, 1 - c)))
        peers = [(1 - x if fx else x, 1 - y if fy else y, 1 - c if fc else c) for fx, fy, fc in flips]
        sent += [_remote(small_ref, my_slot, small_send, small_recv, i, p) for i, p in enumerate(peers)]
        for cp in sent:
            cp.start()
        for p, (_, rows, _) in enumerate(specs):
            theirs = buf_refs[p].at[pl.ds((1 - c) * (rows // 2), rows // 2)]
            _remote(theirs, theirs, send_sems, recv_sems, p, (x, y, 1 - c)).wait_recv()
        for i, (px, py, pc) in enumerate(peers):
            _remote(small_ref, all_ref.at[4 * px + 2 * py + pc], small_send, small_recv, i, peers[i]).wait_recv()
        for cp in sent:
            cp.wait_send()
        local.wait()

    dma = pltpu.SemaphoreType.DMA
    *joined, small_all = pl.pallas_call(
        body, name="join_halves", in_specs=[HBM_SPEC] * (n + 1), out_specs=[HBM_SPEC] * (n + 1),
        out_shape=[_sds(a.shape, a.dtype) for a in s2] + [_sds((8,) + small.shape, small.dtype)],
        input_output_aliases={p: p for p in range(n)},
        scratch_shapes=[dma((n,)), dma((n,)), dma((7,)), dma((7,)), dma((1,))],
    )(*s2, small)
    return joined, small_all


def _adamw(w, g, m, v, name):
    rows, cols = w.shape[0], w.shape[-1]
    tm = rows
    for cand in (512, 256, 128, 64, 32, 16, 8) if w.ndim == 2 else (rows // 4,):
        if rows % cand == 0 and rows > cand:
            tm = cand
            break

    def body(w_ref, g_ref, m_ref, v_ref, d_ref, nm_ref, nv_ref):
        gv = g_ref[...]
        nm = ADAM_B1 * m_ref[...] + (1.0 - ADAM_B1) * gv
        nv = ADAM_B2 * v_ref[...] + (1.0 - ADAM_B2) * (gv * gv)
        m_hat = nm / (1.0 - ADAM_B1 ** ADAM_STEP)
        v_hat = nv / (1.0 - ADAM_B2 ** ADAM_STEP)
        d_ref[...] = -ADAM_LR * (m_hat / (jnp.sqrt(v_hat) + ADAM_EPS) + ADAM_WD * w_ref[...])
        nm_ref[...] = nm
        nv_ref[...] = nv

    spec = _rows(tm, cols) if w.ndim == 2 else pl.BlockSpec((tm, 1, cols), lambda i: (i, 0, 0))
    return pl.pallas_call(
        body, name=name, grid=(rows // tm,), in_specs=[spec] * 4, out_specs=[spec] * 3,
        out_shape=[_sds(w.shape, F32)] * 3, compiler_params=_params("parallel"),
    )(w, g, m, v)


WEIGHTS = ("ml_w_in", "ml_b_if", "ml_head_norm", "ml_w_out", "lru_w_in", "lru_conv_w", "lru_conv_b", "lru_w_gate_a",
           "lru_b_gate_a", "lru_w_gate_x", "lru_b_gate_x", "lru_lambda", "lru_w_out", "norm_pre_mix", "norm_post_mix",
           "norm_pre_ffn", "norm_post_ffn", "ffn_w_gate", "ffn_w_up", "ffn_w_down")
LRU_VECTORS = ("lru_conv_b", "lru_b_gate_a", "lru_b_gate_x", "lru_lambda")
NORMS = ("norm_pre_mix", "norm_post_mix", "norm_pre_ffn", "norm_post_ffn")


def _by_cols(a):
    return jnp.concatenate([a[k] for k in range(N_CHIPS)], axis=-1)


def _by_chip(a, width):
    return jnp.stack([a[..., k * width:(k + 1) * width] for k in range(N_CHIPS)])


def _weight_shards(w):
    bf = lambda a: a.astype(BF16)
    shards = dict(
        ml_w_in=bf(w["ml_w_in"][0]), ml_w_out=bf(w["ml_w_out"][0]), lru_w_in=bf(w["lru_w_in"][0]),
        lru_gates=bf(jnp.concatenate([w["lru_w_gate_a"][0], w["lru_w_gate_x"][0]], axis=0)).reshape(-1, LRU_BLOCK),
        lru_w_out=bf(w["lru_w_out"][0]))
    gate_t, up_t = bf(jnp.swapaxes(w["ffn_w_gate"], 1, 2)), bf(jnp.swapaxes(w["ffn_w_up"], 1, 2))
    for layer in range(2):
        shards[f"ffn_g{layer}"] = gate_t[layer]
        shards[f"ffn_u{layer}"] = up_t[layer]
        shards[f"ffn_down{layer}"] = bf(w["ffn_w_down"][layer])
    return shards


def _ffn_weights(parts):
    return [a.reshape(D_FF, 1024) for a in parts]


def _ffn_grads(dgu, z, act, dy, layer):
    dws = [_matmul_tn(dgu, z, True, 256, f"ffn{layer}_dw_gate", window=(0, D_FF)),
           _matmul_tn(dgu, z, True, 256, f"ffn{layer}_dw_up", window=(D_FF, D_FF)),
           _matmul_tn(act, dy, True, 256, f"ffn{layer}_dw_down")]
    return [a.reshape(N_CHIPS, HALF_FFN, 1024) for a in dws]


def _by_shape(fn, group, xs, ys, pos, name):
    specs = GROUPS[group]
    out = [None] * len(specs)
    for shape in sorted({s[1:] for s in specs}):
        idx = [i for i, s in enumerate(specs) if s[1:] == shape]
        res = fn([xs[i] for i in idx], [ys[i] for i in idx], pos, f"{name}_{group}_{shape[0]}x{shape[1]}")
        for i, r in zip(idx, res):
            out[i] = r
    return out


def _sum_group(group, s1, recv, pos):
    return _by_shape(_sum_chips, group, s1, recv, pos, "sum_chips")


def _update(w, grads, m, v):
    delta, new_m, new_v = {}, {}, {}
    for n in WEIGHTS:
        if n == "ml_w_in":
            view = lambda a: jnp.transpose(a, (2, 0, 1))
            back = lambda a: jnp.transpose(a, (1, 2, 0))
        else:
            flip = (lambda a: jnp.swapaxes(a, 1, 2)) if n in ("ffn_w_gate", "ffn_w_up") else (lambda a: a)
            shape = flip(w[n]).shape
            view = lambda a: flip(a).reshape(-1, shape[-1])
            back = lambda a: flip(a.reshape(shape))
        d, nm, nv = _adamw(view(w[n]), view(grads[n]), view(m[n]), view(v[n]), "adamw_" + n)
        delta[n], new_m[n], new_v[n] = back(d), back(nm), back(nv)
    return delta, new_m, new_v


def kernel(x, ml_w_in, ml_b_if, ml_head_norm, ml_w_out, lru_w_in, lru_conv_w, lru_conv_b, lru_w_gate_a, lru_b_gate_a, lru_w_gate_x, lru_b_gate_x, lru_lambda, lru_w_out, norm_pre_mix, norm_post_mix, norm_pre_ffn, norm_post_ffn, ffn_w_gate, ffn_w_up, ffn_w_down, loss_target, m_ml_w_in, m_ml_b_if, m_ml_head_norm, m_ml_w_out, m_lru_w_in, m_lru_conv_w, m_lru_conv_b, m_lru_w_gate_a, m_lru_b_gate_a, m_lru_w_gate_x, m_lru_b_gate_x, m_lru_lambda, m_lru_w_out, m_norm_pre_mix, m_norm_post_mix, m_norm_pre_ffn, m_norm_post_ffn, m_ffn_w_gate, m_ffn_w_up, m_ffn_w_down, v_ml_w_in, v_ml_b_if, v_ml_head_norm, v_ml_w_out, v_lru_w_in, v_lru_conv_w, v_lru_conv_b, v_lru_w_gate_a, v_lru_b_gate_a, v_lru_w_gate_x, v_lru_b_gate_x, v_lru_lambda, v_lru_w_out, v_norm_pre_mix, v_norm_post_mix, v_norm_pre_ffn, v_norm_post_ffn, v_ffn_w_gate, v_ffn_w_up, v_ffn_w_down):
    args = locals()
    w = {n: args[n] for n in WEIGHTS}
    m = {n: args["m_" + n] for n in WEIGHTS}
    v = {n: args["v_" + n] for n in WEIGHTS}
    xs, target = x[0], loss_target[0]
    mx, my, mc, _ = _chip_peers()
    chip = 2 * mx + my
    pos = jnp.stack([mc, chip])
    row = lambda a, i: a[i:i + 1]
    npm, nqm, npf, nqf = (w[n] for n in NORMS)
    shards = _weight_shards(w)
    of = lambda group: [shards[n] for n, _, _ in GROUPS[group]]
    bif = jnp.pad(w["ml_b_if"], ((0, 0), (0, 128 - 2 * ML_HEADS)))
    hn = w["ml_head_norm"]

    small = jnp.concatenate([w["lru_conv_w"][0]] + [w[n] for n in LRU_VECTORS], axis=0)
    ml_in_spec, ml_out_spec = GROUPS["ml"][:1], GROUPS["ml"][1:]
    ml_in_parts, smalls = _gather_weights(ml_in_spec, of("ml")[:1], small)
    ml_w_in = jnp.concatenate([ml_in_parts[k] for k in range(N_CHIPS)]
                              + [jnp.zeros((D_MODEL, ML_IN_PAD - ML_IN), BF16)], axis=1)
    vec = _by_cols(smalls)
    lru_spec, ffn1_spec = GROUPS["lru"], GROUPS["ffn1"]
    (z0, qkv, og), got = _norm_matmul(xs, row(npm, 0), ml_w_in, 2 * ML_QK + ML_V, "ml_in",
                                      _gather_side(ml_out_spec, of("ml")[1:]))
    ml_w_out = got[0].reshape(1024, 1024)
    (y0, hs, cst, nst, mst), got = _mlstm_fwd(qkv, og, bif, hn, _gather_side(GROUPS["ffn0"], of("ffn0")))
    wg0, wu0, w_down0 = _ffn_weights(got)
    (ymix0, h1), got = _matmul_postnorm(y0, ml_w_out, row(nqm, 0), xs, "ml_out",
                                        _gather_side(lru_spec[:1], of("lru")[:1]))
    lru_w_in = _by_cols(got[0])
    (zf0, gu0, yf0, h2), got = _ffn_forward(
        h1, row(npf, 0), wg0, wu0, w_down0, row(nqf, 0), None, "ffn0_fwd",
        _gather_side(lru_spec[1:] + ffn1_spec[:2], of("lru")[1:] + of("ffn1")[:2]))
    gates = got[0].reshape(N_CHIPS, 2, LRU_BLOCKS, 64, LRU_BLOCK).transpose(1, 2, 0, 3, 4)
    gates = gates.reshape(2, LRU_BLOCKS, LRU_BLOCK, LRU_BLOCK)
    lru_w_out = got[1].reshape(1024, 1024)
    ffn1_gu = got[2:4]
    (z1, proj1), _ = _norm_matmul(h2, row(npm, 1), lru_w_in, 0, "lru_in", None)
    (y1, u, r, ig, hl), got = _lru_fwd(proj1, vec[0:4], vec[4:5], gates[0], vec[5:6], gates[1], vec[6:7], vec[7:8],
                                       _gather_side(ffn1_spec[2:], of("ffn1")[2:]))
    wg1, wu1, w_down1 = _ffn_weights(ffn1_gu + got)
    (ymix1, h3), _ = _matmul_postnorm(y1, lru_w_out, row(nqm, 1), h2, "lru_out", None)
    (zf1, gu1, yf1, dh4, loss_part), _ = _ffn_forward(
        h3, row(npf, 1), wg1, wu1, w_down1, row(nqf, 1), target, "ffn1_fwd", None)

    add = lambda group, gparts, recv: _by_shape(_add_halves, group, gparts, recv, pos, "add_halves")
    (dyf1, dgu1, act1, dh3, dqf1, dpf1), _ = _ffn_backward(
        dh4, yf1, row(nqf, 1), w_down1, gu1, wg1, wu1, h3, row(npf, 1), "ffn1_bwd", None)
    g_ffn1 = _ffn_grads(dgu1, zf1, act1, dyf1, 1)

    (dymix1, dy1, dqm1), recv = _bwd_out(dh3, ymix1, row(nqm, 1), lru_w_out, "lru_bwd_out",
                                         _exchange_side(GROUPS["ffn1"], g_ffn1))
    s1_ffn1 = add("ffn1", g_ffn1, recv)
    dw_lru_out = _matmul_tn(y1, dymix1, False, 256, "lru_dw_out")
    (dproj1, dcw, dcb, dwa, dba, dwx, dbx, dlam), recv_ffn1 = _lru_bwd(
        proj1, vec[0:4], gates[0], gates[1], vec[7:8], u, r, ig, hl, dy1, _scatter_side(s1_ffn1))
    dw_lru_in = _matmul_tn(z1, dproj1, False, 512, "lru_dw_in")
    dgates = jnp.stack([dwa, dwx]).astype(BF16).reshape(2, LRU_BLOCKS, N_CHIPS, 64, LRU_BLOCK)
    dgates = dgates.transpose(2, 0, 1, 3, 4).reshape(N_CHIPS, -1, LRU_BLOCK)
    g_lru = [_by_chip(dw_lru_in, 512), dgates, dw_lru_out.reshape(N_CHIPS, 256, 1024)]
    (dh2, dpm1), recv = _bwd_in(dproj1, [lru_w_in], False, h2, row(npm, 1), dh3, "lru_bwd_in",
                                _exchange_side(GROUPS["lru"], g_lru))
    s1_lru = add("lru", g_lru, recv)

    (dyf0, dgu0, act0, dh1, dqf0, dpf0), recv_lru = _ffn_backward(
        dh2, yf0, row(nqf, 0), w_down0, gu0, wg0, wu0, h1, row(npf, 0), "ffn0_bwd", _scatter_side(s1_lru))
    g_ffn0 = _ffn_grads(dgu0, zf0, act0, dyf0, 0)

    (dymix0, dy0, dqm0), recv = _bwd_out(dh1, ymix0, row(nqm, 0), ml_w_out, "ml_bwd_out",
                                         _exchange_side(GROUPS["ffn0"], g_ffn0))
    s1_ffn0 = add("ffn0", g_ffn0, recv)
    dw_ml_out = _matmul_tn(y0, dymix0, False, 256, "ml_dw_out")
    (dproj0, dhn, dbif), recv_ffn0 = _mlstm_bwd(qkv, og, bif, hn, hs, cst, nst, mst, dy0, _scatter_side(s1_ffn0))
    dw_ml_in = _matmul_tn(z0, dproj0, False, 640, "ml_dw_in")
    g_ml = [_by_chip(dw_ml_in[:, :ML_IN], ML_IN // N_CHIPS), dw_ml_out.reshape(N_CHIPS, 256, 1024)]
    s1_ml = add("ml", g_ml, _exchange_halves(GROUPS["ml"], g_ml, "exchange_halves_ml"))
    (dx, dpm0), recv_ml = _bwd_in(dproj0, [ml_w_in], False, xs, row(npm, 0), dh1, "ml_bwd_in", _scatter_side(s1_ml))

    pad_lanes = lambda a: jnp.pad(a, ((0, 0), (0, LANES - a.shape[1])))
    small = jnp.concatenate(
        [jnp.concatenate([dpm0, dpm1]), jnp.concatenate([dqm0, dqm1]), jnp.concatenate([dpf0, dpf1]),
         jnp.concatenate([dqf0, dqf1]), pad_lanes(loss_part), dhn, pad_lanes(dbif), dcw, dcb, dba, dbx, dlam,
         jnp.zeros((SMALL_ROWS - 19, LANES), F32)], axis=0)
    order = ("ml", "lru", "ffn0", "ffn1")
    s2 = (_sum_group("ml", s1_ml, recv_ml, pos) + _sum_group("lru", s1_lru, recv_lru, pos)
          + _sum_group("ffn0", s1_ffn0, recv_ffn0, pos) + _sum_group("ffn1", s1_ffn1, recv_ffn1, pos))
    specs = sum((GROUPS[g] for g in order), ())
    joined, small_all = _join_halves(specs, s2, small)
    red = dict(zip([n for n, _, _ in specs], joined))
    vsum = _sum_small(small_all)

    dgates = red["lru_gates"].reshape(2, LRU_BLOCKS, 64, LRU_BLOCK)
    cols = lambda a: lax.dynamic_slice_in_dim(a, chip * 256, 256, axis=1)
    grads = dict(
        ml_w_in=red["ml_w_in"], ml_w_out=red["ml_w_out"], lru_w_in=red["lru_w_in"], lru_w_gate_a=dgates[0],
        lru_w_gate_x=dgates[1], lru_w_out=red["lru_w_out"],
        ffn_w_gate=jnp.swapaxes(jnp.stack([red["ffn_g0"], red["ffn_g1"]]), 1, 2),
        ffn_w_up=jnp.swapaxes(jnp.stack([red["ffn_u0"], red["ffn_u1"]]), 1, 2),
        ffn_w_down=jnp.stack([red["ffn_down0"], red["ffn_down1"]]),
        ml_head_norm=vsum[ROW_HEAD_NORM:ROW_HEAD_NORM + 1], ml_b_if=vsum[ROW_B_IF:ROW_B_IF + 1, :2 * ML_HEADS],
        lru_conv_w=cols(vsum[ROW_LRU:ROW_LRU + 4]))
    for i, n in enumerate(NORMS):
        grads[n] = vsum[2 * i:2 * i + 2]
    for i, n in enumerate(LRU_VECTORS):
        grads[n] = cols(vsum[ROW_LRU + 4 + i:ROW_LRU + 5 + i])
    loss = vsum[ROW_LOSS, 0]
    grads = {n: grads[n].reshape(w[n].shape) for n in WEIGHTS}
    delta, new_m, new_v = _update(w, grads, m, v)
    return (loss, dx[None], *[grads[n] for n in WEIGHTS], *[delta[n] for n in WEIGHTS],
            *[new_m[n] for n in WEIGHTS], *[new_v[n] for n in WEIGHTS])
```

```python
import functools
import math
from typing import Callable, NamedTuple

import jax
import jax.numpy as jnp
from jax import lax
from jax.experimental import pallas as pl
from jax.experimental.pallas import tpu as pltpu

F32 = jnp.float32
BF16 = jnp.bfloat16
MESH = pl.DeviceIdType.MESH

D_MODEL = 1024
D_FF = 2816
ML_HEADS = 8
ML_DK = 64
ML_DV = 128
ML_QK = ML_HEADS * ML_DK
ML_V = ML_HEADS * ML_DV
ML_IN = 2 * ML_QK + 2 * ML_V + 2 * ML_HEADS
ML_IN_PAD = 3200
CHUNK = 64
GATE_CAP = 15.0
ML_M_INIT = -1e30
NEG_BIG = -1e30
LRU_BLOCKS = 4
LRU_BLOCK = 256
CONV_WIDTH = 4
LRU_C = 8.0
EPS = 1e-6
QK_SCALE = ML_DK ** -0.5

ADAM_LR = 0.001
ADAM_B1 = 0.9
ADAM_B2 = 0.999
ADAM_EPS = 1e-08
ADAM_WD = 0.01
ADAM_STEP = 10

N_CHIPS = 4
V7X_VMEM_LIMIT = 56 * 1024 * 1024

NT_DIMS = (((1,), (1,)), ((), ()))
TN_DIMS = (((0,), (0,)), ((), ()))


def _params(*semantics):
    return pltpu.CompilerParams(dimension_semantics=semantics, vmem_limit_bytes=V7X_VMEM_LIMIT)


def _sds(shape, dtype):
    return jax.ShapeDtypeStruct(shape, dtype)


def _full(shape):
    return pl.BlockSpec(shape, lambda *_: (0,) * len(shape))


def _rows(tm, n):
    return pl.BlockSpec((tm, n), lambda i: (i, 0))


def _sigmoid(x):
    return 1.0 / (1.0 + jnp.exp(-x))


def _log_sigmoid(x):
    return jnp.minimum(x, 0.0) - jnp.log1p(jnp.exp(-jnp.abs(x)))


def _rms(x):
    r = lax.rsqrt(jnp.mean(x * x, axis=-1, keepdims=True) + EPS)
    return x * r, r


def _rms_bwd(xhat, r, g, dy):
    dxh = dy * g
    return r * (dxh - xhat * jnp.mean(dxh * xhat, axis=-1, keepdims=True))


SUBLANES = 8


def _scan_rows(a, b, carry, reverse=False):
    n, w = a.shape
    groups = n // SUBLANES
    a3, b3 = a.reshape(groups, SUBLANES, w), b.reshape(groups, SUBLANES, w)
    sub = lax.broadcasted_iota(jnp.int32, a3.shape, 1)
    s = 1
    while s < SUBLANES:
        keep = sub < SUBLANES - s if reverse else sub >= s
        shift = SUBLANES - s if reverse else s
        b3 = b3 + a3 * jnp.where(keep, pltpu.roll(b3, shift, 1), 0.0)
        a3 = a3 * jnp.where(keep, pltpu.roll(a3, shift, 1), 1.0)
        s *= 2
    out = [None] * groups
    for g in (reversed(range(groups)) if reverse else range(groups)):
        out[g] = b3[g] + a3[g] * carry
        carry = out[g][0:1] if reverse else out[g][SUBLANES - 1:SUBLANES]
    return jnp.concatenate(out, axis=0)


def _cumsum_rows(x, reverse=False):
    n = x.shape[0]
    row = lax.broadcasted_iota(jnp.int32, x.shape, 0)
    s = 1
    while s < n:
        if reverse:
            x = x + jnp.where(row < n - s, pltpu.roll(x, n - s, 0), 0.0)
        else:
            x = x + jnp.where(row >= s, pltpu.roll(x, s, 0), 0.0)
        s *= 2
    return x


def _norm_matmul(h, g, w, n_bf16, name, side):
    t, d = h.shape
    n = w.shape[1]
    tm = min(512, t)

    def body(h_ref, g_ref, w_ref, z_ref, *o_refs):
        xhat, _ = _rms(h_ref[...])
        z = (xhat * g_ref[...]).astype(BF16)
        z_ref[...] = z
        out = jnp.dot(z, w_ref[...], preferred_element_type=F32)
        if n_bf16:
            o_refs[0][...] = out[:, :n_bf16].astype(BF16)
            o_refs[1][...] = out[:, n_bf16:]
        else:
            o_refs[0][...] = out

    if n_bf16:
        out_specs = [_rows(tm, d), _rows(tm, n_bf16), _rows(tm, n - n_bf16)]
        out_shape = [_sds((t, d), BF16), _sds((t, n_bf16), BF16), _sds((t, n - n_bf16), F32)]
    else:
        out_specs = [_rows(tm, d), _rows(tm, n)]
        out_shape = [_sds((t, d), BF16), _sds((t, n), F32)]
    return _side_call(body, side, name=name, steps=t // tm, in_specs=[_rows(tm, d), _full((1, d)), _full((d, n))],
                      out_specs=out_specs, out_shape=out_shape, scratch_shapes=[], args=(h, g, w))


def _matmul_postnorm(a, w, g, res, name, side):
    t = res.shape[0]
    k, d = w.shape
    tm = min(512, t)

    def body(a_ref, w_ref, g_ref, res_ref, y_ref, o_ref):
        y = jnp.dot(a_ref[...], w_ref[...], preferred_element_type=F32)
        y_ref[...] = y
        yhat, _ = _rms(y)
        o_ref[...] = res_ref[...] + yhat * g_ref[...]

    return _side_call(
        body, side, name=name, steps=t // tm,
        in_specs=[_rows(tm, k), _full((k, d)), _full((1, d)), _rows(tm, d)],
        out_specs=[_rows(tm, d), _rows(tm, d)],
        out_shape=[_sds((t, d), F32), _sds((t, d), F32)],
        scratch_shapes=[], args=(a, w, g, res))


class _Side(NamedTuple):
    inputs: list
    out_shape: list
    n_sems: int
    start: Callable
    finish: Callable


def _side_call(body, side, *, name, steps, in_specs, out_specs, out_shape, scratch_shapes, args):
    n_in, n_out, n_scr = len(in_specs), len(out_specs), len(scratch_shapes)
    if side is None:
        outs = pl.pallas_call(
            body, name=name, grid=(steps,), in_specs=in_specs, out_specs=out_specs, out_shape=out_shape,
            scratch_shapes=scratch_shapes, compiler_params=_params("arbitrary"))(*args)
        return list(outs), []
    s_in, s_out = len(side.inputs), len(side.out_shape)

    def carrying(*refs):
        ins, side_ins = refs[:n_in], refs[n_in:n_in + s_in]
        outs = refs[n_in + s_in:n_in + s_in + n_out]
        side_outs = refs[n_in + s_in + n_out:n_in + s_in + n_out + s_out]
        scratch = refs[n_in + s_in + n_out + s_out:]
        own, sems = scratch[:n_scr], scratch[n_scr:]

        @pl.when(pl.program_id(0) == 0)
        def _():
            side.start(side_ins, side_outs, *sems)

        body(*ins, *outs, *own)

        @pl.when(pl.program_id(0) == steps - 1)
        def _():
            side.finish(side_ins, side_outs, *sems)

    outs = pl.pallas_call(
        carrying, name=name, grid=(steps,), in_specs=list(in_specs) + [HBM_SPEC] * s_in,
        out_specs=list(out_specs) + [HBM_SPEC] * s_out, out_shape=list(out_shape) + list(side.out_shape),
        scratch_shapes=list(scratch_shapes) + [pltpu.SemaphoreType.DMA((side.n_sems,))] * 2,
        compiler_params=_params("arbitrary"))(*args, *side.inputs)
    return list(outs[:n_out]), list(outs[n_out:])


def _ffn_forward(h, g_pre, wg_t, wu_t, w_down, g_post, target, name, side):
    t, d = h.shape
    k = w_down.shape[0]
    tm = 256

    def body(*refs):
        if target is None:
            h_ref, gpre_ref, wg_ref, wu_ref, wd_ref, gpost_ref, z_ref, gu_ref, y_ref, o_ref = refs
        else:
            h_ref, gpre_ref, wg_ref, wu_ref, wd_ref, gpost_ref, t_ref, z_ref, gu_ref, y_ref, o_ref, l_ref = refs

            @pl.when(pl.program_id(0) == 0)
            def _():
                l_ref[...] = jnp.zeros_like(l_ref)

        hv = h_ref[...]
        xhat, _ = _rms(hv)
        z = (xhat * gpre_ref[...]).astype(BF16)
        z_ref[...] = z
        gate = lax.dot_general(z, wg_ref[...], NT_DIMS, preferred_element_type=F32).astype(BF16)
        up = lax.dot_general(z, wu_ref[...], NT_DIMS, preferred_element_type=F32).astype(BF16)
        gu_ref[:, :k] = gate
        gu_ref[:, k:] = up
        gate = gate.astype(F32)
        act = (gate * _sigmoid(gate) * up.astype(F32)).astype(BF16)
        y = jnp.dot(act, wd_ref[...], preferred_element_type=F32)
        y_ref[...] = y
        yhat, _ = _rms(y)
        out = hv + yhat * gpost_ref[...]
        if target is None:
            o_ref[...] = out
        else:
            err = out - t_ref[...]
            o_ref[...] = err * (1.0 / d)
            part = jnp.sum(jnp.sum(err * err, axis=1, keepdims=True), axis=0, keepdims=True) * (0.5 / d)
            l_ref[...] += jnp.broadcast_to(part, l_ref.shape)

    in_specs = [_rows(tm, d), _full((1, d)), _full((k, d)), _full((k, d)), _full((k, d)), _full((1, d))]
    out_specs = [_rows(tm, d), _rows(tm, 2 * k), _rows(tm, d), _rows(tm, d)]
    out_shape = [_sds((t, d), BF16), _sds((t, 2 * k), BF16), _sds((t, d), F32), _sds((t, d), F32)]
    args = (h, g_pre, wg_t, wu_t, w_down, g_post)
    if target is not None:
        in_specs, args = in_specs + [_rows(tm, d)], args + (target,)
        out_specs, out_shape = out_specs + [_full((1, 128))], out_shape + [_sds((1, 128), F32)]
    return _side_call(body, side, name=name, steps=t // tm, in_specs=in_specs, out_specs=out_specs,
                      out_shape=out_shape, scratch_shapes=[], args=args)


def _bwd_out(dout, y, g, w, name, side):
    t, d = dout.shape
    k = w.shape[0]
    tm = min(512, t)

    def body(dout_ref, y_ref, g_ref, w_ref, dy_ref, da_ref, dg_ref):
        @pl.when(pl.program_id(0) == 0)
        def _():
            dg_ref[...] = jnp.zeros_like(dg_ref)

        do = dout_ref[...]
        yhat, r = _rms(y_ref[...])
        dg_ref[...] += jnp.sum(do * yhat, axis=0, keepdims=True)
        dy = _rms_bwd(yhat, r, g_ref[...], do).astype(BF16)
        dy_ref[...] = dy
        da_ref[...] = lax.dot_general(dy, w_ref[...], NT_DIMS, preferred_element_type=F32)

    return _side_call(
        body, side, name=name, steps=t // tm,
        in_specs=[_rows(tm, d), _rows(tm, d), _full((1, d)), _full((k, d))],
        out_specs=[_rows(tm, d), _rows(tm, k), _full((1, d))],
        out_shape=[_sds((t, d), BF16), _sds((t, k), F32), _sds((1, d), F32)],
        scratch_shapes=[], args=(dout, y, g, w))


def _ffn_backward(dout, y, g_post, w_down, gu, wg_t, wu_t, h, g_pre, name, side):
    t, d = dout.shape
    k = w_down.shape[0]
    tm = 256

    def body(dout_ref, y_ref, gpost_ref, wd_ref, gu_ref, wg_ref, wu_ref, h_ref, gpre_ref,
             dy_ref, dgu_ref, act_ref, dh_ref, dgpost_ref, dgpre_ref):
        @pl.when(pl.program_id(0) == 0)
        def _():
            dgpost_ref[...] = jnp.zeros_like(dgpost_ref)
            dgpre_ref[...] = jnp.zeros_like(dgpre_ref)

        do = dout_ref[...]
        yhat, r = _rms(y_ref[...])
        dgpost_ref[...] += jnp.sum(do * yhat, axis=0, keepdims=True)
        dy = _rms_bwd(yhat, r, gpost_ref[...], do).astype(BF16)
        dy_ref[...] = dy
        da = lax.dot_general(dy, wd_ref[...], NT_DIMS, preferred_element_type=F32)
        gate = gu_ref[:, :k].astype(F32)
        up = gu_ref[:, k:].astype(F32)
        sg = _sigmoid(gate)
        silu = gate * sg
        act_ref[...] = (silu * up).astype(BF16)
        dgate = (da * up * (sg * (1.0 + gate * (1.0 - sg)))).astype(BF16)
        dup = (da * silu).astype(BF16)
        dgu_ref[:, :k] = dgate
        dgu_ref[:, k:] = dup
        dz = (jnp.dot(dgate, wg_ref[...], preferred_element_type=F32)
              + jnp.dot(dup, wu_ref[...], preferred_element_type=F32))
        hhat, r2 = _rms(h_ref[...])
        dgpre_ref[...] += jnp.sum(dz * hhat, axis=0, keepdims=True)
        dh_ref[...] = do + _rms_bwd(hhat, r2, gpre_ref[...], dz)

    vec = _full((1, d))
    wspec = _full((k, d))
    return _side_call(
        body, side, name=name, steps=t // tm,
        in_specs=[_rows(tm, d), _rows(tm, d), vec, wspec, _rows(tm, 2 * k), wspec, wspec, _rows(tm, d), vec],
        out_specs=[_rows(tm, d), _rows(tm, 2 * k), _rows(tm, k), _rows(tm, d), vec, vec],
        out_shape=[_sds((t, d), BF16), _sds((t, 2 * k), BF16), _sds((t, k), BF16), _sds((t, d), F32),
                   _sds((1, d), F32), _sds((1, d), F32)],
        scratch_shapes=[], args=(dout, y, g_post, w_down, gu, wg_t, wu_t, h, g_pre))


def _bwd_in(dp, ws, transposed, h, g, dout, name, side):
    t, d = h.shape
    n = dp.shape[1]
    tm = min(512, t)
    widths = [w.shape[0] if transposed else w.shape[1] for w in ws]

    def body(dp_ref, *refs):
        w_refs = refs[:len(ws)]
        h_ref, g_ref, dout_ref, dh_ref, dg_ref = refs[len(ws):]

        @pl.when(pl.program_id(0) == 0)
        def _():
            dg_ref[...] = jnp.zeros_like(dg_ref)

        dz, at = None, 0
        for w_ref, width in zip(w_refs, widths):
            block = dp_ref[:, at:at + width]
            if transposed:
                part = jnp.dot(block, w_ref[...], preferred_element_type=F32)
            else:
                part = lax.dot_general(block, w_ref[...], NT_DIMS, preferred_element_type=F32)
            dz = part if dz is None else dz + part
            at += width
        hhat, r = _rms(h_ref[...])
        dg_ref[...] += jnp.sum(dz * hhat, axis=0, keepdims=True)
        dh_ref[...] = dout_ref[...] + _rms_bwd(hhat, r, g_ref[...], dz)

    return _side_call(
        body, side, name=name, steps=t // tm,
        in_specs=[_rows(tm, n)] + [_full(w.shape) for w in ws] + [_rows(tm, d), _full((1, d)), _rows(tm, d)],
        out_specs=[_rows(tm, d), _full((1, d))],
        out_shape=[_sds((t, d), F32), _sds((1, d), F32)],
        scratch_shapes=[], args=(dp, *ws, h, g, dout))


def _matmul_tn(a, b, tile_a, tile, name, window=None):
    t, ka = a.shape
    nb = b.shape[1]
    col0 = 0
    if window is not None:
        col0, ka = window

    def body(a_ref, b_ref, o_ref):
        o_ref[...] = lax.dot_general(a_ref[...], b_ref[...], TN_DIMS, preferred_element_type=F32).astype(BF16)

    if tile_a:
        grid = (ka // tile,)
        in_specs = [pl.BlockSpec((t, tile), lambda i: (0, col0 // tile + i)), _full((t, nb))]
        out_specs = pl.BlockSpec((tile, nb), lambda i: (i, 0))
    else:
        grid = (nb // tile,)
        in_specs = [_full((t, ka)), pl.BlockSpec((t, tile), lambda i: (0, i))]
        out_specs = pl.BlockSpec((ka, tile), lambda i: (0, i))
    return pl.pallas_call(
        body, name=name, grid=grid, in_specs=in_specs, out_specs=out_specs,
        out_shape=_sds((ka, nb), BF16), compiler_params=_params("parallel"),
    )(a, b)


def _ml_gate_prep(gt, bif):
    th = jnp.tanh((gt + bif) / GATE_CAP)
    act = GATE_CAP * th
    cum = _cumsum_rows(_log_sigmoid(act))
    lane = lax.broadcasted_iota(jnp.int32, gt.shape, 1)
    x = jnp.where(lane < ML_HEADS, act, cum)
    return x, x.T, th, act


HEADS = range(ML_HEADS)


def _each(fn, *per_head):
    return [fn(*a) for a in zip(*per_head)]


def _ml_chunk_fwd(q, k, v, kt, x, xt, c_in, n_in, m_in):
    causal = (lax.broadcasted_iota(jnp.int32, (CHUNK, CHUNK), 0)
              >= lax.broadcasted_iota(jnp.int32, (CHUNK, CHUNK), 1))
    f = {}
    qh = f["qh"] = [q[:, ML_DK * h:ML_DK * (h + 1)] for h in HEADS]
    kh = f["kh"] = [k[:, ML_DK * h:ML_DK * (h + 1)] for h in HEADS]
    f["vh"] = [v[:, ML_DV * h:ML_DV * (h + 1)] for h in HEADS]
    kth = [kt[ML_DK * h:ML_DK * (h + 1), :] for h in HEADS]
    s = _each(lambda a, b: jnp.dot(a, b, preferred_element_type=F32) * QK_SCALE, qh, kth)
    f["qc"] = _each(lambda a, c: jnp.dot(a, c.astype(BF16), preferred_element_type=F32) * QK_SCALE, qh, c_in)
    bcol = f["bcol"] = [x[:, ML_HEADS + h:ML_HEADS + h + 1] for h in HEADS]
    licol = f["licol"] = [x[:, h:h + 1] for h in HEADS]
    brow = f["brow"] = [xt[ML_HEADS + h:ML_HEADS + h + 1, :] for h in HEADS]
    lirow = [xt[h:h + 1, :] for h in HEADS]
    dmat = _each(lambda bc, br, lr: jnp.where(causal, bc - br + lr, NEG_BIG), bcol, brow, lirow)
    inter = _each(lambda bc, m: bc + m, bcol, m_in)
    mt = _each(lambda d, i: jnp.maximum(jnp.max(d, axis=1, keepdims=True), i), dmat, inter)
    wt = f["wt"] = _each(lambda d, m: jnp.exp(d - m), dmat, mt)
    p = f["p"] = _each(lambda a, b: a * b, wt, s)
    winter = f["winter"] = _each(lambda i, m: jnp.exp(i - m), inter, mt)
    qf = f["qf"] = [a.astype(F32) for a in qh]
    qn = f["qn"] = _each(lambda a, n: jnp.sum(a * n, axis=1, keepdims=True) * QK_SCALE, qf, n_in)
    den = f["den"] = _each(lambda a, w, b: jnp.sum(a, axis=1, keepdims=True) + w * b, p, winter, qn)
    emt = f["emt"] = [jnp.exp(-m) for m in mt]
    f["nrm"] = _each(lambda d, e: jnp.maximum(jnp.abs(d), e), den, emt)
    gtot = [bc[CHUNK - 1:CHUNK, :] for bc in bcol]
    a_col = _each(lambda g, bc, lc: g - bc + lc, gtot, bcol, licol)
    a_row = _each(lambda g, br, lr: g - br + lr, gtot, brow, lirow)
    m_new = f["m_new"] = _each(lambda g, m, a: jnp.maximum(g + m, jnp.max(a, axis=1, keepdims=True)),
                               gtot, m_in, a_row)
    f["decay"] = _each(lambda g, m, mn: jnp.exp(g + m - mn), gtot, m_in, m_new)
    wkf = f["wkf"] = _each(lambda a, mn: jnp.exp(a - mn), a_col, m_new)
    f["kw"] = _each(lambda a, w: a.astype(F32) * w, kh, wkf)
    f["ktw"] = _each(lambda a, ar, mn: (a.astype(F32) * jnp.exp(ar - mn)).astype(BF16), kth, a_row, m_new)
    return f


ML_STEP_CHUNKS = 4
ML_STEP = CHUNK * ML_STEP_CHUNKS


def _ml_specs(steps, rev):
    def at(col):
        if rev:
            return lambda i: (steps - 1 - i, col)
        return lambda i: (i, col)

    return [pl.BlockSpec((ML_STEP, ML_QK), at(0)), pl.BlockSpec((ML_STEP, ML_QK), at(1)),
            pl.BlockSpec((ML_STEP, ML_V), at(1)), pl.BlockSpec((ML_STEP, ML_V), at(0)),
            pl.BlockSpec((ML_STEP, 128), at(ML_V // 128))]


def _mlstm_fwd(qkv, og, bif, hn, side):
    t = qkv.shape[0]
    nc = t // CHUNK
    steps = t // ML_STEP

    def body(q_ref, k_ref, v_ref, o_ref, gt_ref, bif_ref, hn_ref, y_ref, hs_ref, cst_ref, nst_ref, mst_ref,
             c_sc, n_sc, m_sc):
        @pl.when(pl.program_id(0) == 0)
        def _():
            c_sc[...] = jnp.zeros_like(c_sc)
            n_sc[...] = jnp.zeros_like(n_sc)
            m_sc[...] = jnp.full_like(m_sc, ML_M_INIT)

        c_all, n_all, m_all = c_sc[...], n_sc[...], m_sc[...]
        c_in = [c_all[ML_DK * h:ML_DK * (h + 1), :] for h in HEADS]
        n_in = [n_all[h:h + 1, :] for h in HEADS]
        m_in = [m_all[h:h + 1, 0:1] for h in HEADS]
        for sub in range(ML_STEP_CHUNKS):
            rs = slice(CHUNK * sub, CHUNK * (sub + 1))
            for h in HEADS:
                cst_ref[sub, ML_DK * h:ML_DK * (h + 1), :] = c_in[h]
                nst_ref[sub, h:h + 1, :] = n_in[h]
                mst_ref[sub, h:h + 1, :] = jnp.broadcast_to(m_in[h], (1, 128))
            x, xt, _, _ = _ml_gate_prep(gt_ref[rs, :], bif_ref[...])
            q, k, v = q_ref[rs, :], k_ref[rs, :], v_ref[rs, :]
            f = _ml_chunk_fwd(q, k, v, k.T, x, xt, c_in, n_in, m_in)
            num = _each(lambda p, vh, w, qc: jnp.dot(p.astype(BF16), vh, preferred_element_type=F32) + w * qc,
                        f["p"], f["vh"], f["winter"], f["qc"])
            hh = _each(lambda a, b: a / b, num, f["nrm"])
            hhat = [_rms(a)[0] for a in hh]
            c_in = _each(lambda d, c, kw, vh: d * c + jnp.dot(kw, vh, preferred_element_type=F32),
                         f["decay"], c_in, f["ktw"], f["vh"])
            n_in = _each(lambda d, n, kw: d * n + jnp.sum(kw, axis=0, keepdims=True), f["decay"], n_in, f["kw"])
            m_in = f["m_new"]
            for h in HEADS:
                vs = slice(ML_DV * h, ML_DV * (h + 1))
                hs_ref[rs, vs] = hh[h]
                y_ref[rs, vs] = (hhat[h] * hn_ref[:, vs] * _sigmoid(o_ref[rs, vs])).astype(BF16)
        for h in HEADS:
            c_sc[ML_DK * h:ML_DK * (h + 1), :] = c_in[h]
            n_sc[h:h + 1, :] = n_in[h]
            m_sc[h:h + 1, :] = jnp.broadcast_to(m_in[h], (1, 128))

    return _side_call(
        body, side, name="mlstm_fwd", steps=steps,
        in_specs=_ml_specs(steps, False) + [_full((1, 128)), _full((1, ML_V))],
        out_specs=[_rows(ML_STEP, ML_V), _rows(ML_STEP, ML_V),
                   pl.BlockSpec((ML_STEP_CHUNKS, ML_HEADS * ML_DK, ML_DV), lambda i: (i, 0, 0)),
                   pl.BlockSpec((ML_STEP_CHUNKS, ML_HEADS, ML_DK), lambda i: (i, 0, 0)),
                   pl.BlockSpec((ML_STEP_CHUNKS, ML_HEADS, 128), lambda i: (i, 0, 0))],
        out_shape=[_sds((t, ML_V), BF16), _sds((t, ML_V), F32),
                   _sds((nc, ML_HEADS * ML_DK, ML_DV), F32), _sds((nc, ML_HEADS, ML_DK), F32),
                   _sds((nc, ML_HEADS, 128), F32)],
        scratch_shapes=[pltpu.VMEM((ML_HEADS * ML_DK, ML_DV), F32), pltpu.VMEM((ML_HEADS, ML_DK), F32),
                        pltpu.VMEM((ML_HEADS, 128), F32)],
        args=(qkv, qkv, qkv, og, og, bif, hn))


def _mlstm_bwd(qkv, og, bif, hn, hs, cst, nst, mst, dy, side):
    t = qkv.shape[0]
    steps = t // ML_STEP

    def chunk(q_ref, k_ref, v_ref, o_ref, gt_ref, bif_ref, hn_ref, hs_ref, cst_ref, nst_ref, mst_ref, dy_ref,
              dp_ref, dhn_ref, dbif_ref, carried):
        x, xt, th, act = _ml_gate_prep(gt_ref[...], bif_ref[...])
        q, k, v = q_ref[...], k_ref[...], v_ref[...]
        qt, vt = q.T, v.T
        rows = lax.broadcasted_iota(jnp.int32, (CHUNK, CHUNK), 0)
        cols = lax.broadcasted_iota(jnp.int32, (CHUNK, CHUNK), 1)
        lane = lax.broadcasted_iota(jnp.int32, (CHUNK, 128), 1)
        row = lax.broadcasted_iota(jnp.int32, (CHUNK, 1), 0)
        as_row = lambda col: jnp.sum(jnp.where(rows == cols, col, 0.0), axis=0, keepdims=True)
        mm = lambda a, b: jnp.dot(a, b, preferred_element_type=F32)
        bf = lambda a: a.astype(BF16)
        ksl = [slice(ML_DK * h, ML_DK * (h + 1)) for h in HEADS]
        vsl = [slice(ML_DV * h, ML_DV * (h + 1)) for h in HEADS]
        c_in = [cst_ref[0, s, :] for s in ksl]
        n_in = [nst_ref[0, h:h + 1, :] for h in HEADS]
        m_in = [mst_ref[0, h:h + 1, 0:1] for h in HEADS]
        dcn, dcn_t, dnn = carried
        f = _ml_chunk_fwd(q, k, v, k.T, x, xt, c_in, n_in, m_in)
        qh, kh, vh, p, winter, decay = f["qh"], f["kh"], f["vh"], f["p"], f["winter"], f["decay"]
        qth = [qt[s, :] for s in ksl]
        vth = [vt[s, :] for s in vsl]
        c_t = [bf(c.T) for c in c_in]
        dmat_t = _each(lambda br, bc, lc: jnp.where(rows <= cols, br - bc + lc, NEG_BIG),
                       f["brow"], f["bcol"], f["licol"])
        inter_row = _each(lambda br, m: br + m, f["brow"], m_in)
        mt_row = _each(lambda d, i: jnp.maximum(jnp.max(d, axis=0, keepdims=True), i), dmat_t, inter_row)
        wt_t = _each(lambda d, m: jnp.exp(d - m), dmat_t, mt_row)
        p_t = _each(lambda w, a, b: w * (mm(a, b) * QK_SCALE), wt_t, kh, qth)
        winter_row = _each(lambda i, m: jnp.exp(i - m), inter_row, mt_row)
        hh = [hs_ref[:, s] for s in vsl]
        hn_h = [hn_ref[:, s] for s in vsl]
        sg = [_sigmoid(o_ref[:, s]) for s in vsl]
        dyh = [dy_ref[:, s] for s in vsl]
        norm = [_rms(a) for a in hh]
        hhat, r = [a for a, _ in norm], [b for _, b in norm]
        dyn = _each(lambda a, b: a * b, dyh, sg)
        do = _each(lambda d, hx, g, s: d * hx * g * s * (1.0 - s), dyh, hhat, hn_h, sg)
        dhn = _each(lambda a, b: jnp.sum(a * b, axis=0, keepdims=True), dyn, hhat)
        dh = _each(_rms_bwd, hhat, r, hn_h, dyn)
        inv = [1.0 / a for a in f["nrm"]]
        dnum = _each(lambda a, b: a * b, dh, inv)
        dnrm = _each(lambda a, b, c: -jnp.sum(a * b, axis=1, keepdims=True) * c, dh, hh, inv)
        dden = _each(lambda d, e, g: jnp.where(jnp.abs(d) > e, g * jnp.sign(d), 0.0), f["den"], f["emt"], dnrm)
        dnb = [bf(a) for a in dnum]
        dnt = [bf(a.T) for a in dnum]
        rmat = _each(lambda a, b, d: mm(a, b) + d, dnb, vth, dden)
        rmat_t = _each(lambda a, b, d: mm(a, b) + as_row(d), vh, dnt, dden)
        ds = _each(lambda w, a: bf(w * a), f["wt"], rmat)
        ds_t = _each(lambda w, a: bf(w * a), wt_t, rmat_t)
        dv = _each(lambda a, b: mm(bf(a), b), p_t, dnb)
        dqs = _each(lambda s, kk, w, d, ct, dd, n: mm(s, kk) + w * (mm(d, ct) + dd * n),
                    ds, kh, winter, dnb, c_t, dden, n_in)
        dk = _each(lambda s, a: mm(s, a) * QK_SCALE, ds_t, qh)
        dinter = _each(lambda qc, dn_, qn, dd, w: (jnp.sum(qc * dn_, axis=1, keepdims=True) + qn * dd) * w,
                       f["qc"], dnum, f["qn"], dden, winter)
        wq = _each(lambda w, a: w * a * QK_SCALE, winter, f["qf"])
        wq_t = _each(lambda w, a: bf(w * a.astype(F32) * QK_SCALE), winter_row, qth)
        dc_loc = _each(mm, wq_t, dnb)
        dct_loc = _each(lambda a, b: mm(a, bf(b)), dnt, wq)
        dn_loc = _each(lambda a, d: jnp.sum(a * d, axis=0, keepdims=True), wq, dden)
        cs_q = _each(lambda a, b: jnp.sum(a * b, axis=1, keepdims=True), p_t, rmat_t)
        db = _each(lambda a, b, di, cs: jnp.sum(a * b, axis=1, keepdims=True) + di - cs, p, rmat, dinter, cs_q)
        ddecay = _each(lambda dc_, c, dn_, n: jnp.sum(jnp.sum(dc_ * c, axis=1, keepdims=True), axis=0, keepdims=True)
                       + jnp.sum(dn_ * n, axis=1, keepdims=True), dcn, c_in, dnn, n_in)
        dkw = _each(lambda a, b, n: mm(a, bf(b)) + n, vh, dcn_t, dnn)
        dk = _each(lambda a, w, b: a + w * b, dk, f["wkf"], dkw)
        da = _each(lambda a, kk, w: jnp.sum(a * kk.astype(F32), axis=1, keepdims=True) * w, dkw, kh, f["wkf"])
        dv = _each(lambda a, kw, dc_: a + mm(bf(kw), bf(dc_)), dv, f["kw"], dcn)
        dgtot = _each(lambda a, dd, d: jnp.sum(a, axis=0, keepdims=True) + dd * d, da, ddecay, decay)
        db = _each(lambda a, b, g: a - b + jnp.where(row == CHUNK - 1, g, 0.0), db, da, dgtot)
        dli = _each(lambda a, b: a + b, cs_q, da)
        dc_new = _each(lambda d, a, b: d * a + b, decay, dcn, dc_loc)
        dct_new = _each(lambda d, a, b: d * a + b, decay, dcn_t, dct_loc)
        dn_new = _each(lambda d, a, b: d * a + b, decay, dnn, dn_loc)
        dx = jnp.zeros((CHUNK, 128), F32)
        for h in HEADS:
            dhn_ref[:, vsl[h]] += dhn[h]
            dp_ref[:, ksl[h]] = bf(dqs[h] * QK_SCALE)
            dp_ref[:, ML_QK + ML_DK * h:ML_QK + ML_DK * (h + 1)] = bf(dk[h])
            dp_ref[:, 2 * ML_QK + ML_DV * h:2 * ML_QK + ML_DV * (h + 1)] = bf(dv[h])
            dp_ref[:, 2 * ML_QK + ML_V + ML_DV * h:2 * ML_QK + ML_V + ML_DV * (h + 1)] = bf(do[h])
            dx = jnp.where(lane == h, dli[h], dx)
            dx = jnp.where(lane == ML_HEADS + h, db[h], dx)
        dlf = _cumsum_rows(dx, reverse=True)
        dact = jnp.where(lane < ML_HEADS, dx, dlf * _sigmoid(-act))
        dz = dact * (1.0 - th * th)
        dp_ref[:, 2 * ML_QK + 2 * ML_V:] = dz.astype(BF16)
        dbif_ref[...] += jnp.sum(dz, axis=0, keepdims=True)
        return dc_new, dct_new, dn_new

    def body(q_ref, k_ref, v_ref, o_ref, gt_ref, bif_ref, hn_ref, hs_ref, cst_ref, nst_ref, mst_ref, dy_ref,
             dp_ref, dhn_ref, dbif_ref, dc_sc, dct_sc, dn_sc):
        @pl.when(pl.program_id(0) == 0)
        def _():
            dc_sc[...] = jnp.zeros_like(dc_sc)
            dct_sc[...] = jnp.zeros_like(dct_sc)
            dn_sc[...] = jnp.zeros_like(dn_sc)
            dhn_ref[...] = jnp.zeros_like(dhn_ref)
            dbif_ref[...] = jnp.zeros_like(dbif_ref)

        dc_all, dct_all, dn_all = dc_sc[...], dct_sc[...], dn_sc[...]
        carried = ([dc_all[ML_DK * h:ML_DK * (h + 1), :] for h in HEADS],
                   [dct_all[ML_DV * h:ML_DV * (h + 1), :] for h in HEADS],
                   [dn_all[h:h + 1, :] for h in HEADS])
        for sub in reversed(range(ML_STEP_CHUNKS)):
            rs = pl.ds(CHUNK * sub, CHUNK)
            one = pl.ds(sub, 1)
            carried = chunk(q_ref.at[rs], k_ref.at[rs], v_ref.at[rs], o_ref.at[rs], gt_ref.at[rs], bif_ref, hn_ref,
                            hs_ref.at[rs], cst_ref.at[one], nst_ref.at[one], mst_ref.at[one], dy_ref.at[rs],
                            dp_ref.at[rs], dhn_ref, dbif_ref, carried)
        for h in HEADS:
            dc_sc[ML_DK * h:ML_DK * (h + 1), :] = carried[0][h]
            dct_sc[ML_DV * h:ML_DV * (h + 1), :] = carried[1][h]
            dn_sc[h:h + 1, :] = carried[2][h]

    rev = lambda i: (steps - 1 - i, 0)
    rev3 = lambda i: (steps - 1 - i, 0, 0)
    return _side_call(
        body, side, name="mlstm_bwd", steps=steps,
        in_specs=_ml_specs(steps, True) + [
            _full((1, 128)), _full((1, ML_V)), pl.BlockSpec((ML_STEP, ML_V), rev),
            pl.BlockSpec((ML_STEP_CHUNKS, ML_HEADS * ML_DK, ML_DV), rev3),
            pl.BlockSpec((ML_STEP_CHUNKS, ML_HEADS, ML_DK), rev3),
            pl.BlockSpec((ML_STEP_CHUNKS, ML_HEADS, 128), rev3), pl.BlockSpec((ML_STEP, ML_V), rev)],
        out_specs=[pl.BlockSpec((ML_STEP, ML_IN_PAD), rev), _full((1, ML_V)), _full((1, 128))],
        out_shape=[_sds((t, ML_IN_PAD), BF16), _sds((1, ML_V), F32), _sds((1, 128), F32)],
        scratch_shapes=[pltpu.VMEM((ML_HEADS * ML_DK, ML_DV), F32), pltpu.VMEM((ML_HEADS * ML_DV, ML_DK), F32),
                        pltpu.VMEM((ML_HEADS, ML_DK), F32)],
        args=(qkv, qkv, qkv, og, og, bif, hn, hs, cst, nst, mst, dy))


LRU_TM = 256
GELU_K = math.sqrt(2.0 / math.pi)
GELU_C = 0.044715


def _gelu(x):
    th = jnp.tanh(GELU_K * (x + GELU_C * x * x * x))
    return 0.5 * x * (1.0 + th), th


def _neg_expm1(x):
    series = -x * (1.0 + x * (0.5 + x * (1.0 / 6.0 + x * (1.0 / 24.0))))
    return jnp.where(x > -0.05, series, 1.0 - jnp.exp(x))


def _block_diag_dot(a, w_ref, dims):
    parts = [lax.dot_general(a[:, LRU_BLOCK * n:LRU_BLOCK * (n + 1)], w_ref[n], dims, preferred_element_type=F32)
             for n in range(LRU_BLOCKS)]
    return jnp.concatenate(parts, axis=1)


def _lru_gates(r, lam):
    ls = _log_sigmoid(lam)
    la = LRU_C * r * ls
    a = jnp.exp(la)
    em = _neg_expm1(2.0 * la)
    mult = jnp.sqrt(em)
    return ls, a, em, mult


def _lru_fwd(proj, cw, cb, wa, ba, wx, bx, lam, side):
    t = proj.shape[0]
    w = D_MODEL
    tm = min(LRU_TM, t)

    def body(gb_ref, up_ref, cw_ref, cb_ref, wa_ref, ba_ref, wx_ref, bx_ref, lam_ref,
             y_ref, u_ref, r_ref, i_ref, h_ref, tail_sc, hprev_sc):
        @pl.when(pl.program_id(0) == 0)
        def _():
            tail_sc[...] = jnp.zeros_like(tail_sc)
            hprev_sc[...] = jnp.zeros_like(hprev_sc)

        up = up_ref[...]
        ext = jnp.concatenate([tail_sc[...], up], axis=0)
        u = cb_ref[...] + cw_ref[CONV_WIDTH - 1:CONV_WIDTH, :] * up
        for s in range(1, CONV_WIDTH):
            u = u + cw_ref[CONV_WIDTH - 1 - s:CONV_WIDTH - s, :] * pltpu.roll(ext, s, 0)[8:8 + tm]
        tail_sc[...] = up[tm - 8:tm]
        ub = u.astype(BF16)
        r = _sigmoid(_block_diag_dot(ub, wa_ref, (((1,), (0,)), ((), ()))) + ba_ref[...])
        ig = _sigmoid(_block_diag_dot(ub, wx_ref, (((1,), (0,)), ((), ()))) + bx_ref[...])
        _, a, _, mult = _lru_gates(r, lam_ref[...])
        h = _scan_rows(a, mult * ig * u, hprev_sc[0:1, :])
        hprev_sc[0:1, :] = h[tm - 1:tm]
        u_ref[...] = u
        r_ref[...] = r
        i_ref[...] = ig
        h_ref[...] = h
        gel, _ = _gelu(gb_ref[...])
        y_ref[...] = (h * gel).astype(BF16)

    vec = _full((1, w))
    wspec = _full((LRU_BLOCKS, LRU_BLOCK, LRU_BLOCK))
    return _side_call(
        body, side, name="lru_fwd", steps=t // tm,
        in_specs=[pl.BlockSpec((tm, w), lambda i: (i, 0)), pl.BlockSpec((tm, w), lambda i: (i, 1)),
                  _full((CONV_WIDTH, w)), vec, wspec, vec, wspec, vec, vec],
        out_specs=[_rows(tm, w)] * 5,
        out_shape=[_sds((t, w), BF16)] + [_sds((t, w), F32)] * 4,
        scratch_shapes=[pltpu.VMEM((8, w), F32), pltpu.VMEM((8, w), F32)],
        args=(proj, proj, cw, cb, wa, ba, wx, bx, lam))


def _lru_bwd(proj, cw, wa, wx, lam, u, r, ig, h, dy, side):
    t = proj.shape[0]
    w = D_MODEL
    tm = min(LRU_TM, t)
    nt = t // tm

    def body(gb_ref, up_ref, cw_ref, wa_ref, wx_ref, lam_ref, u_ref, r_ref, i_ref, h_ref, hp_ref, dy_ref,
             dp_ref, dcw_ref, dcb_ref, dwa_ref, dba_ref, dwx_ref, dbx_ref, dlam_ref, carry_sc, dutail_sc, dls_sc):
        step = pl.program_id(0)

        @pl.when(step == 0)
        def _():
            carry_sc[...] = jnp.zeros_like(carry_sc)
            dutail_sc[...] = jnp.zeros_like(dutail_sc)
            dls_sc[...] = jnp.zeros_like(dls_sc)
            for ref in (dcw_ref, dcb_ref, dwa_ref, dba_ref, dwx_ref, dbx_ref):
                ref[...] = jnp.zeros_like(ref)

        row = lax.broadcasted_iota(jnp.int32, (tm, w), 0)
        u_t, r_t, i_t, h_t = u_ref[...], r_ref[...], i_ref[...], h_ref[...]
        ls, a, em, mult = _lru_gates(r_t, lam_ref[...])
        gb = gb_ref[...]
        gel, th = _gelu(gb)
        dyv = dy_ref[...]
        dgb = dyv * h_t * (0.5 * (1.0 + th) + 0.5 * gb * (1.0 - th * th) * GELU_K * (1.0 + 3.0 * GELU_C * gb * gb))
        a_next = jnp.where(row < tm - 1, pltpu.roll(a, tm - 1, 0), 1.0)
        g = _scan_rows(a_next, dyv * gel, carry_sc[0:1, :], reverse=True)
        carry_sc[0:1, :] = a[0:1] * g[0:1]
        has_prev = jnp.where(step == nt - 1, 0.0, 1.0)
        h_prev = jnp.where(row >= 1, pltpu.roll(h_t, 1, 0), hp_ref[7:8, :] * has_prev)
        dmult = g * i_t * u_t
        dig = g * mult * u_t
        du = g * mult * i_t
        dla = g * h_prev * a - dmult * (1.0 - em) / mult
        dls_sc[0:1, :] += jnp.sum(dla * r_t, axis=0, keepdims=True) * LRU_C
        dpa = dla * (LRU_C * ls) * r_t * (1.0 - r_t)
        dpx = dig * i_t * (1.0 - i_t)
        dba_ref[...] += jnp.sum(dpa, axis=0, keepdims=True)
        dbx_ref[...] += jnp.sum(dpx, axis=0, keepdims=True)
        ub = u_t.astype(BF16)
        dpab = dpa.astype(BF16)
        dpxb = dpx.astype(BF16)
        for n in range(LRU_BLOCKS):
            cs = slice(LRU_BLOCK * n, LRU_BLOCK * (n + 1))
            dwa_ref[n] += lax.dot_general(ub[:, cs], dpab[:, cs], TN_DIMS, preferred_element_type=F32)
            dwx_ref[n] += lax.dot_general(ub[:, cs], dpxb[:, cs], TN_DIMS, preferred_element_type=F32)
        du = du + _block_diag_dot(dpab, wa_ref, NT_DIMS) + _block_diag_dot(dpxb, wx_ref, NT_DIMS)
        dcb_ref[...] += jnp.sum(du, axis=0, keepdims=True)
        ext = jnp.concatenate([du, dutail_sc[...]], axis=0)
        up = up_ref[...]
        dup = cw_ref[CONV_WIDTH - 1:CONV_WIDTH, :] * du
        dcw_ref[CONV_WIDTH - 1:CONV_WIDTH, :] += jnp.sum(up * du, axis=0, keepdims=True)
        for s in range(1, CONV_WIDTH):
            du_s = pltpu.roll(ext, tm + 8 - s, 0)[0:tm]
            dup = dup + cw_ref[CONV_WIDTH - 1 - s:CONV_WIDTH - s, :] * du_s
            dcw_ref[CONV_WIDTH - 1 - s:CONV_WIDTH - s, :] += jnp.sum(up * du_s, axis=0, keepdims=True)
        dutail_sc[...] = du[0:8]
        dp_ref[:, :w] = dgb.astype(BF16)
        dp_ref[:, w:] = dup.astype(BF16)

        @pl.when(step == nt - 1)
        def _():
            dlam_ref[...] = dls_sc[0:1, :] * _sigmoid(-lam_ref[...])

    rev = lambda col: (lambda i: (nt - 1 - i, col))
    vec = _full((1, w))
    wspec = _full((LRU_BLOCKS, LRU_BLOCK, LRU_BLOCK))
    tile = pl.BlockSpec((tm, w), rev(0))
    prev8 = pl.BlockSpec((8, w), lambda i: (jnp.maximum((nt - 1 - i) * (tm // 8) - 1, 0), 0))
    return _side_call(
        body, side, name="lru_bwd", steps=nt,
        in_specs=[tile, pl.BlockSpec((tm, w), rev(1)), _full((CONV_WIDTH, w)), wspec, wspec, vec,
                  tile, tile, tile, tile, prev8, tile],
        out_specs=[pl.BlockSpec((tm, 2 * w), rev(0)), _full((CONV_WIDTH, w)), vec, wspec, vec, wspec, vec, vec],
        out_shape=[_sds((t, 2 * w), BF16), _sds((CONV_WIDTH, w), F32), _sds((1, w), F32),
                   _sds((LRU_BLOCKS, LRU_BLOCK, LRU_BLOCK), F32), _sds((1, w), F32),
                   _sds((LRU_BLOCKS, LRU_BLOCK, LRU_BLOCK), F32), _sds((1, w), F32), _sds((1, w), F32)],
        scratch_shapes=[pltpu.VMEM((8, w), F32), pltpu.VMEM((8, w), F32), pltpu.VMEM((8, w), F32)],
        args=(proj, proj, cw, wa, wx, lam, u, r, ig, h, h, dy))


LANES = 1024
HALF_FFN = D_FF // N_CHIPS
GROUPS = {
    "ml": (("ml_w_in", 1024, ML_IN // N_CHIPS), ("ml_w_out", 256, 1024)),
    "lru": (("lru_w_in", 1024, 512),
            ("lru_gates", 2 * LRU_BLOCKS * 64, LRU_BLOCK),
            ("lru_w_out", 256, 1024)),
    "ffn0": (("ffn_g0", HALF_FFN, 1024), ("ffn_u0", HALF_FFN, 1024), ("ffn_down0", HALF_FFN, 1024)),
    "ffn1": (("ffn_g1", HALF_FFN, 1024), ("ffn_u1", HALF_FFN, 1024), ("ffn_down1", HALF_FFN, 1024)),
}

SMALL_ROWS = 24
ROW_LOSS, ROW_HEAD_NORM, ROW_B_IF, ROW_LRU = 8, 9, 10, 11


def _row_tile(rows, cols, itemsize, budget=3 << 19):
    best = 16
    for t in range(16, rows + 1, 16):
        if rows % t == 0 and t * cols * itemsize <= budget:
            best = t
    return best


def _chip_peers():
    x, y, c = lax.axis_index("x"), lax.axis_index("y"), lax.axis_index("c")
    return x, y, c, [(1 - x, y), (x, 1 - y), (1 - x, 1 - y)]


HBM_SPEC = pl.BlockSpec(memory_space=pltpu.HBM)


def _remote(src, dst, send_sems, recv_sems, k, to):
    return pltpu.make_async_remote_copy(src_ref=src, dst_ref=dst, send_sem=send_sems.at[k], recv_sem=recv_sems.at[k],
                                        device_id=to, device_id_type=MESH)


GATHER_SEMS = 7


def _gather_copies(kind, specs, part_refs, out_refs, send_sems, recv_sems):
    x, y, c, chips = _chip_peers()
    me = 2 * x + y
    sib = (x, y, 1 - c)
    copy = functools.partial(_remote, send_sems=send_sems, recv_sems=recv_sems)
    out = []
    for p, (_, rows, _) in enumerate(specs):
        mine = pl.ds(c * (rows // 2), rows // 2)
        theirs = pl.ds((1 - c) * (rows // 2), rows // 2)
        base = GATHER_SEMS * p
        for j, (cx, cy) in enumerate(chips):
            land = out_refs[p].at[2 * cx + cy, mine]
            other = out_refs[p].at[2 * cx + cy, theirs]
            if kind == "first":
                out.append(copy(part_refs[p].at[mine], out_refs[p].at[me, mine], k=base + j, to=(cx, cy, c)))
            elif kind == "landed":
                out.append(copy(land, land, k=base + j, to=sib))
            elif kind == "forward":
                out.append(copy(land, land, k=base + 3 + j, to=sib))
            else:
                out.append(copy(other, other, k=base + 3 + j, to=sib))
        if kind in ("first", "arriving"):
            out.append(copy(part_refs[p], out_refs[p].at[me], k=base + 6, to=sib))
    return out


def _gather_start(specs, *refs):
    for cp in _gather_copies("first", specs, *refs):
        cp.start()


def _gather_finish(specs, *refs):
    forwards = _gather_copies("forward", specs, *refs)
    for land, fwd in zip(_gather_copies("landed", specs, *refs), forwards):
        land.wait_recv()
        fwd.start()
    for cp in _gather_copies("arriving", specs, *refs):
        cp.wait_recv()
    for cp in _gather_copies("first", specs, *refs) + forwards:
        cp.wait_send()


def _gather_shapes(specs):
    return [_sds((N_CHIPS, rows, cols), BF16) for _, rows, cols in specs]


def _gather_side(specs, parts):
    return _Side(inputs=list(parts), out_shape=_gather_shapes(specs), n_sems=GATHER_SEMS * len(specs),
                 start=functools.partial(_gather_start, specs), finish=functools.partial(_gather_finish, specs))


def _gather_weights(specs, parts, small):
    n = len(specs)

    def body(*refs):
        part_refs, small_ref = refs[:n], refs[n]
        out_refs, outs_ref = refs[n + 1:2 * n + 1], refs[2 * n + 1]
        send_sems, recv_sems, small_send, small_recv, loc_sem = refs[2 * n + 2:]
        x, y, c, chips = _chip_peers()
        me = 2 * x + y
        local = pltpu.make_async_copy(small_ref, outs_ref.at[me], loc_sem.at[0])
        local.start()
        _gather_start(specs, part_refs, out_refs, send_sems, recv_sems)
        sent = [_remote(small_ref, outs_ref.at[me], small_send, small_recv, j, (cx, cy, c))
                for j, (cx, cy) in enumerate(chips)]
        for cp in sent:
            cp.start()
        _gather_finish(specs, part_refs, out_refs, send_sems, recv_sems)
        for j, (cx, cy) in enumerate(chips):
            _remote(small_ref, outs_ref.at[2 * cx + cy], small_send, small_recv, j, (cx, cy, c)).wait_recv()
        for cp in sent:
            cp.wait_send()
        local.wait()

    dma = pltpu.SemaphoreType.DMA
    return pl.pallas_call(
        body, name="gather_weights",
        in_specs=[HBM_SPEC] * (n + 1), out_specs=[HBM_SPEC] * (n + 1),
        out_shape=_gather_shapes(specs) + [_sds((N_CHIPS,) + small.shape, small.dtype)],
        scratch_shapes=[dma((GATHER_SEMS * n,)), dma((GATHER_SEMS * n,)), dma((3,)), dma((3,)), dma((1,))],
    )(*parts, small)


def _exchange_copies(specs, g_refs, out_refs, send_sems, recv_sems):
    x, y, c, _ = _chip_peers()
    return [_remote(g_refs[p].at[:, pl.ds((1 - c) * (rows // 2), rows // 2)], out_refs[p], send_sems, recv_sems, p,
                    (x, y, 1 - c)) for p, (_, rows, _) in enumerate(specs)]


def _exchange_start(specs, *refs):
    for cp in _exchange_copies(specs, *refs):
        cp.start()


def _exchange_finish(specs, *refs):
    for cp in _exchange_copies(specs, *refs):
        cp.wait()


def _exchange_shapes(specs):
    return [_sds((N_CHIPS, rows // 2, cols), BF16) for _, rows, cols in specs]


def _exchange_side(specs, gparts):
    return _Side(inputs=list(gparts), out_shape=_exchange_shapes(specs), n_sems=len(specs),
                 start=functools.partial(_exchange_start, specs), finish=functools.partial(_exchange_finish, specs))


def _exchange_halves(specs, gparts, name):
    n = len(specs)

    def body(*refs):
        _exchange_start(specs, refs[:n], refs[n:2 * n], *refs[2 * n:])
        _exchange_finish(specs, refs[:n], refs[n:2 * n], *refs[2 * n:])

    return pl.pallas_call(
        body, name=name, in_specs=[HBM_SPEC] * n, out_specs=[HBM_SPEC] * n, out_shape=_exchange_shapes(specs),
        scratch_shapes=[pltpu.SemaphoreType.DMA((n,)), pltpu.SemaphoreType.DMA((n,))],
    )(*gparts)


def _add_halves(gs, recvs, pos, name):
    n = len(gs)
    _, half, cols = recvs[0].shape
    tr = _row_tile(half, cols, 2)
    tiles = half // tr

    def body(pos_ref, *refs):
        for a_ref, b_ref, o_ref in zip(refs[:n], refs[n:2 * n], refs[2 * n:]):
            o_ref[...] = (a_ref[...].astype(F32) + b_ref[...].astype(F32)).astype(BF16)

    spec = pl.BlockSpec((1, tr, cols), lambda k, i, pos_ref: (k, i, 0))
    mine = pl.BlockSpec((1, tr, cols), lambda k, i, pos_ref: (k, pos_ref[0] * tiles + i, 0))
    return pl.pallas_call(
        body, name=name,
        grid_spec=pltpu.PrefetchScalarGridSpec(
            num_scalar_prefetch=1, grid=(N_CHIPS, tiles), in_specs=[mine] * n + [spec] * n, out_specs=[spec] * n),
        out_shape=[_sds((N_CHIPS, half, cols), BF16)] * n,
        compiler_params=_params("parallel", "parallel"),
    )(pos, *gs, *recvs)


def _scatter_copies(n, s1_refs, recv_refs, send_sems, recv_sems):
    x, y, c, chips = _chip_peers()
    return [_remote(s1_refs[p].at[2 * cx + cy], recv_refs[p].at[j], send_sems, recv_sems, 3 * p + j, (cx, cy, c))
            for p in range(n) for j, (cx, cy) in enumerate(chips)]


def _scatter_start(n, s1_refs, recv_refs, send_sems, recv_sems):
    for cp in _scatter_copies(n, s1_refs, recv_refs, send_sems, recv_sems):
        cp.start()


def _scatter_finish(n, s1_refs, recv_refs, send_sems, recv_sems):
    for cp in _scatter_copies(n, s1_refs, recv_refs, send_sems, recv_sems):
        cp.wait()


def _scatter_shapes(s1):
    return [_sds((3,) + a.shape[1:], a.dtype) for a in s1]


def _scatter_side(s1):
    n = len(s1)
    return _Side(inputs=list(s1), out_shape=_scatter_shapes(s1), n_sems=3 * n,
                 start=functools.partial(_scatter_start, n), finish=functools.partial(_scatter_finish, n))


def _sum_chips(s1s, recvs, pos, name):
    n = len(s1s)
    _, half, cols = recvs[0].shape
    tr = _row_tile(half, cols, 4)
    tiles = half // tr

    def body(pos_ref, *refs):
        for a_ref, b_ref, o_ref in zip(refs[:n], refs[n:2 * n], refs[2 * n:]):
            acc = a_ref[0].astype(F32)
            for j in range(3):
                acc = acc + b_ref[j].astype(F32)
            o_ref[...] = acc

    return pl.pallas_call(
        body, name=name,
        grid_spec=pltpu.PrefetchScalarGridSpec(
            num_scalar_prefetch=1, grid=(tiles,),
            in_specs=[pl.BlockSpec((1, tr, cols), lambda i, pos_ref: (pos_ref[1], i, 0))] * n
            + [pl.BlockSpec((3, tr, cols), lambda i, pos_ref: (0, i, 0))] * n,
            out_specs=[pl.BlockSpec((tr, cols), lambda i, pos_ref: (pos_ref[0] * tiles + i, 0))] * n),
        out_shape=[_sds((2 * half, cols), F32)] * n,
        compiler_params=_params("parallel"),
    )(pos, *s1s, *recvs)


def _sum_small(small_all):
    def body(a_ref, o_ref):
        acc = a_ref[0]
        for d in range(1, 8):
            acc = acc + a_ref[d]
        o_ref[...] = acc

    return pl.pallas_call(body, name="sum_small", out_shape=_sds(small_all.shape[1:], F32))(small_all)


def _join_halves(specs, s2, small):
    n = len(specs)
    flips = [(fx, fy, fc) for fx in (0, 1) for fy in (0, 1) for fc in (0, 1)][1:]

    def body(*refs):
        small_ref, all_ref = refs[n], refs[2 * n + 1]
        buf_refs = refs[n + 1:2 * n + 1]
        send_sems, recv_sems, small_send, small_recv, loc_sem = refs[2 * n + 2:]
        x, y, c, _ = _chip_peers()
        my_slot = all_ref.at[4 * x + 2 * y + c]
        local = pltpu.make_async_copy(small_ref, my_slot, loc_sem.at[0])
        local.start()
        sent = []
        for p, (_, rows, _) in enumerate(specs):
            mine = buf_refs[p].at[pl.ds(c * (rows // 2), rows // 2)]
            sent.append(_remote(mine, mine, send_sems, recv_sems, p, (x, y, 1 - c)))
        peers = [(1 - x if fx else x, 1 - y if fy else y, 1 - c if fc else c) for fx, fy, fc in flips]
        sent += [_remote(small_ref, my_slot, small_send, small_recv, i, p) for i, p in enumerate(peers)]
        for cp in sent:
            cp.start()
        for p, (_, rows, _) in enumerate(specs):
            theirs = buf_refs[p].at[pl.ds((1 - c) * (rows // 2), rows // 2)]
            _remote(theirs, theirs, send_sems, recv_sems, p, (x, y, 1 - c)).wait_recv()
        for i, (px, py, pc) in enumerate(peers):
            _remote(small_ref, all_ref.at[4 * px + 2 * py + pc], small_send, small_recv, i, peers[i]).wait_recv()
        for cp in sent:
            cp.wait_send()
        local.wait()

    dma = pltpu.SemaphoreType.DMA
    *joined, small_all = pl.pallas_call(
        body, name="join_halves", in_specs=[HBM_SPEC] * (n + 1), out_specs=[HBM_SPEC] * (n + 1),
        out_shape=[_sds(a.shape, a.dtype) for a in s2] + [_sds((8,) + small.shape, small.dtype)],
        input_output_aliases={p: p for p in range(n)},
        scratch_shapes=[dma((n,)), dma((n,)), dma((7,)), dma((7,)), dma((1,))],
    )(*s2, small)
    return joined, small_all


def _adamw(ws, gs, ms, vs, name):
    n = len(ws)
    w = ws[0]
    rows, cols = w.shape[0], w.shape[-1]
    tm = rows
    for cand in (512 // n, 256 // n, 128, 64, 32, 16, 8) if w.ndim == 2 else (rows // 4,):
        if rows % cand == 0 and rows > cand and (cand % 8 == 0 or w.ndim == 3):
            tm = cand
            break

    def body(*refs):
        ins, outs = refs[:4 * n], refs[4 * n:]
        for i in range(n):
            w_ref, g_ref, m_ref, v_ref = ins[i], ins[n + i], ins[2 * n + i], ins[3 * n + i]
            gv = g_ref[...]
            nm = ADAM_B1 * m_ref[...] + (1.0 - ADAM_B1) * gv
            nv = ADAM_B2 * v_ref[...] + (1.0 - ADAM_B2) * (gv * gv)
            m_hat = nm / (1.0 - ADAM_B1 ** ADAM_STEP)
            v_hat = nv / (1.0 - ADAM_B2 ** ADAM_STEP)
            outs[i][...] = -ADAM_LR * (m_hat / (jnp.sqrt(v_hat) + ADAM_EPS) + ADAM_WD * w_ref[...])
            outs[n + i][...] = nm
            outs[2 * n + i][...] = nv

    spec = _rows(tm, cols) if w.ndim == 2 else pl.BlockSpec((tm, 1, cols), lambda i: (i, 0, 0))
    out = pl.pallas_call(
        body, name=name, grid=(rows // tm,), in_specs=[spec] * (4 * n), out_specs=[spec] * (3 * n),
        out_shape=[_sds(w.shape, F32)] * (3 * n), compiler_params=_params("parallel"),
    )(*ws, *gs, *ms, *vs)
    return out[:n], out[n:2 * n], out[2 * n:]


WEIGHTS = ("ml_w_in", "ml_b_if", "ml_head_norm", "ml_w_out", "lru_w_in", "lru_conv_w", "lru_conv_b", "lru_w_gate_a",
           "lru_b_gate_a", "lru_w_gate_x", "lru_b_gate_x", "lru_lambda", "lru_w_out", "norm_pre_mix", "norm_post_mix",
           "norm_pre_ffn", "norm_post_ffn", "ffn_w_gate", "ffn_w_up", "ffn_w_down")
LRU_VECTORS = ("lru_conv_b", "lru_b_gate_a", "lru_b_gate_x", "lru_lambda")
NORMS = ("norm_pre_mix", "norm_post_mix", "norm_pre_ffn", "norm_post_ffn")


def _by_cols(a):
    return jnp.concatenate([a[k] for k in range(N_CHIPS)], axis=-1)


def _by_chip(a, width):
    return jnp.stack([a[..., k * width:(k + 1) * width] for k in range(N_CHIPS)])


def _weight_shards(w):
    bf = lambda a: a.astype(BF16)
    shards = dict(
        ml_w_in=bf(w["ml_w_in"][0]), ml_w_out=bf(w["ml_w_out"][0]), lru_w_in=bf(w["lru_w_in"][0]),
        lru_gates=bf(jnp.concatenate([w["lru_w_gate_a"][0], w["lru_w_gate_x"][0]], axis=0)).reshape(-1, LRU_BLOCK),
        lru_w_out=bf(w["lru_w_out"][0]))
    gate_t, up_t = bf(jnp.swapaxes(w["ffn_w_gate"], 1, 2)), bf(jnp.swapaxes(w["ffn_w_up"], 1, 2))
    for layer in range(2):
        shards[f"ffn_g{layer}"] = gate_t[layer]
        shards[f"ffn_u{layer}"] = up_t[layer]
        shards[f"ffn_down{layer}"] = bf(w["ffn_w_down"][layer])
    return shards


def _ffn_weights(parts):
    return [a.reshape(D_FF, 1024) for a in parts]


def _ffn_grads(dgu, z, act, dy, layer):
    dws = [_matmul_tn(dgu, z, True, 256, f"ffn{layer}_dw_gate", window=(0, D_FF)),
           _matmul_tn(dgu, z, True, 256, f"ffn{layer}_dw_up", window=(D_FF, D_FF)),
           _matmul_tn(act, dy, True, 256, f"ffn{layer}_dw_down")]
    return [a.reshape(N_CHIPS, HALF_FFN, 1024) for a in dws]


def _by_shape(fn, group, xs, ys, pos, name):
    specs = GROUPS[group]
    out = [None] * len(specs)
    for shape in sorted({s[1:] for s in specs}):
        idx = [i for i, s in enumerate(specs) if s[1:] == shape]
        res = fn([xs[i] for i in idx], [ys[i] for i in idx], pos, f"{name}_{group}_{shape[0]}x{shape[1]}")
        for i, r in zip(idx, res):
            out[i] = r
    return out


def _sum_group(group, s1, recv, pos):
    return _by_shape(_sum_chips, group, s1, recv, pos, "sum_chips")


def _update(w, grads, m, v):
    views, backs, alike = {}, {}, {}
    for n in WEIGHTS:
        if n == "ml_w_in":
            views[n] = lambda a: jnp.transpose(a, (2, 0, 1))
            backs[n] = lambda a: jnp.transpose(a, (1, 2, 0))
        else:
            flip = (lambda a: jnp.swapaxes(a, 1, 2)) if n in ("ffn_w_gate", "ffn_w_up") else (lambda a: a)
            shape = flip(w[n]).shape
            views[n] = lambda a, flip=flip, shape=shape: flip(a).reshape(-1, shape[-1])
            backs[n] = lambda a, flip=flip, shape=shape: flip(a.reshape(shape))
        alike.setdefault(views[n](w[n]).shape, []).append(n)
    delta, new_m, new_v = {}, {}, {}
    for names in alike.values():
        of = lambda d: [views[n](d[n]) for n in names]
        ds, nms, nvs = _adamw(of(w), of(grads), of(m), of(v), "adamw_" + names[0])
        for n, d, nm, nv in zip(names, ds, nms, nvs):
            delta[n], new_m[n], new_v[n] = backs[n](d), backs[n](nm), backs[n](nv)
    return delta, new_m, new_v


def kernel(x, ml_w_in, ml_b_if, ml_head_norm, ml_w_out, lru_w_in, lru_conv_w, lru_conv_b, lru_w_gate_a, lru_b_gate_a, lru_w_gate_x, lru_b_gate_x, lru_lambda, lru_w_out, norm_pre_mix, norm_post_mix, norm_pre_ffn, norm_post_ffn, ffn_w_gate, ffn_w_up, ffn_w_down, loss_target, m_ml_w_in, m_ml_b_if, m_ml_head_norm, m_ml_w_out, m_lru_w_in, m_lru_conv_w, m_lru_conv_b, m_lru_w_gate_a, m_lru_b_gate_a, m_lru_w_gate_x, m_lru_b_gate_x, m_lru_lambda, m_lru_w_out, m_norm_pre_mix, m_norm_post_mix, m_norm_pre_ffn, m_norm_post_ffn, m_ffn_w_gate, m_ffn_w_up, m_ffn_w_down, v_ml_w_in, v_ml_b_if, v_ml_head_norm, v_ml_w_out, v_lru_w_in, v_lru_conv_w, v_lru_conv_b, v_lru_w_gate_a, v_lru_b_gate_a, v_lru_w_gate_x, v_lru_b_gate_x, v_lru_lambda, v_lru_w_out, v_norm_pre_mix, v_norm_post_mix, v_norm_pre_ffn, v_norm_post_ffn, v_ffn_w_gate, v_ffn_w_up, v_ffn_w_down):
    args = locals()
    w = {n: args[n] for n in WEIGHTS}
    m = {n: args["m_" + n] for n in WEIGHTS}
    v = {n: args["v_" + n] for n in WEIGHTS}
    xs, target = x[0], loss_target[0]
    mx, my, mc, _ = _chip_peers()
    chip = 2 * mx + my
    pos = jnp.stack([mc, chip])
    row = lambda a, i: a[i:i + 1]
    npm, nqm, npf, nqf = (w[n] for n in NORMS)
    shards = _weight_shards(w)
    of = lambda group: [shards[n] for n, _, _ in GROUPS[group]]
    bif = jnp.pad(w["ml_b_if"], ((0, 0), (0, 128 - 2 * ML_HEADS)))
    hn = w["ml_head_norm"]

    small = jnp.concatenate([w["lru_conv_w"][0]] + [w[n] for n in LRU_VECTORS], axis=0)
    ml_in_spec, ml_out_spec = GROUPS["ml"][:1], GROUPS["ml"][1:]
    ml_in_parts, smalls = _gather_weights(ml_in_spec, of("ml")[:1], small)
    ml_w_in = jnp.concatenate([ml_in_parts[k] for k in range(N_CHIPS)]
                              + [jnp.zeros((D_MODEL, ML_IN_PAD - ML_IN), BF16)], axis=1)
    vec = _by_cols(smalls)
    lru_spec, ffn1_spec = GROUPS["lru"], GROUPS["ffn1"]
    (z0, qkv, og), got = _norm_matmul(xs, row(npm, 0), ml_w_in, 2 * ML_QK + ML_V, "ml_in",
                                      _gather_side(ml_out_spec, of("ml")[1:]))
    ml_w_out = got[0].reshape(1024, 1024)
    (y0, hs, cst, nst, mst), got = _mlstm_fwd(qkv, og, bif, hn, _gather_side(GROUPS["ffn0"], of("ffn0")))
    wg0, wu0, w_down0 = _ffn_weights(got)
    (ymix0, h1), got = _matmul_postnorm(y0, ml_w_out, row(nqm, 0), xs, "ml_out",
                                        _gather_side(lru_spec[:1], of("lru")[:1]))
    lru_w_in = _by_cols(got[0])
    (zf0, gu0, yf0, h2), got = _ffn_forward(
        h1, row(npf, 0), wg0, wu0, w_down0, row(nqf, 0), None, "ffn0_fwd",
        _gather_side(lru_spec[1:] + ffn1_spec[:2], of("lru")[1:] + of("ffn1")[:2]))
    gates = got[0].reshape(N_CHIPS, 2, LRU_BLOCKS, 64, LRU_BLOCK).transpose(1, 2, 0, 3, 4)
    gates = gates.reshape(2, LRU_BLOCKS, LRU_BLOCK, LRU_BLOCK)
    lru_w_out = got[1].reshape(1024, 1024)
    ffn1_gu = got[2:4]
    (z1, proj1), _ = _norm_matmul(h2, row(npm, 1), lru_w_in, 0, "lru_in", None)
    (y1, u, r, ig, hl), got = _lru_fwd(proj1, vec[0:4], vec[4:5], gates[0], vec[5:6], gates[1], vec[6:7], vec[7:8],
                                       _gather_side(ffn1_spec[2:], of("ffn1")[2:]))
    wg1, wu1, w_down1 = _ffn_weights(ffn1_gu + got)
    (ymix1, h3), _ = _matmul_postnorm(y1, lru_w_out, row(nqm, 1), h2, "lru_out", None)
    (zf1, gu1, yf1, dh4, loss_part), _ = _ffn_forward(
        h3, row(npf, 1), wg1, wu1, w_down1, row(nqf, 1), target, "ffn1_fwd", None)

    add = lambda group, gparts, recv: _by_shape(_add_halves, group, gparts, recv, pos, "add_halves")
    (dyf1, dgu1, act1, dh3, dqf1, dpf1), _ = _ffn_backward(
        dh4, yf1, row(nqf, 1), w_down1, gu1, wg1, wu1, h3, row(npf, 1), "ffn1_bwd", None)
    g_ffn1 = _ffn_grads(dgu1, zf1, act1, dyf1, 1)

    (dymix1, dy1, dqm1), recv = _bwd_out(dh3, ymix1, row(nqm, 1), lru_w_out, "lru_bwd_out",
                                         _exchange_side(GROUPS["ffn1"], g_ffn1))
    s1_ffn1 = add("ffn1", g_ffn1, recv)
    dw_lru_out = _matmul_tn(y1, dymix1, False, 256, "lru_dw_out")
    (dproj1, dcw, dcb, dwa, dba, dwx, dbx, dlam), recv_ffn1 = _lru_bwd(
        proj1, vec[0:4], gates[0], gates[1], vec[7:8], u, r, ig, hl, dy1, _scatter_side(s1_ffn1))
    dw_lru_in = _matmul_tn(z1, dproj1, False, 512, "lru_dw_in")
    dgates = jnp.stack([dwa, dwx]).astype(BF16).reshape(2, LRU_BLOCKS, N_CHIPS, 64, LRU_BLOCK)
    dgates = dgates.transpose(2, 0, 1, 3, 4).reshape(N_CHIPS, -1, LRU_BLOCK)
    g_lru = [_by_chip(dw_lru_in, 512), dgates, dw_lru_out.reshape(N_CHIPS, 256, 1024)]
    (dh2, dpm1), recv = _bwd_in(dproj1, [lru_w_in], False, h2, row(npm, 1), dh3, "lru_bwd_in",
                                _exchange_side(GROUPS["lru"], g_lru))
    s1_lru = add("lru", g_lru, recv)

    (dyf0, dgu0, act0, dh1, dqf0, dpf0), recv_lru = _ffn_backward(
        dh2, yf0, row(nqf, 0), w_down0, gu0, wg0, wu0, h1, row(npf, 0), "ffn0_bwd", _scatter_side(s1_lru))
    g_ffn0 = _ffn_grads(dgu0, zf0, act0, dyf0, 0)

    (dymix0, dy0, dqm0), recv = _bwd_out(dh1, ymix0, row(nqm, 0), ml_w_out, "ml_bwd_out",
                                         _exchange_side(GROUPS["ffn0"], g_ffn0))
    s1_ffn0 = add("ffn0", g_ffn0, recv)
    dw_ml_out = _matmul_tn(y0, dymix0, False, 256, "ml_dw_out")
    (dproj0, dhn, dbif), recv_ffn0 = _mlstm_bwd(qkv, og, bif, hn, hs, cst, nst, mst, dy0, _scatter_side(s1_ffn0))
    dw_ml_in = _matmul_tn(z0, dproj0, False, 640, "ml_dw_in")
    g_ml = [_by_chip(dw_ml_in[:, :ML_IN], ML_IN // N_CHIPS), dw_ml_out.reshape(N_CHIPS, 256, 1024)]
    s1_ml = add("ml", g_ml, _exchange_halves(GROUPS["ml"], g_ml, "exchange_halves_ml"))
    (dx, dpm0), recv_ml = _bwd_in(dproj0, [ml_w_in], False, xs, row(npm, 0), dh1, "ml_bwd_in", _scatter_side(s1_ml))

    pad_lanes = lambda a: jnp.pad(a, ((0, 0), (0, LANES - a.shape[1])))
    small = jnp.concatenate(
        [jnp.concatenate([dpm0, dpm1]), jnp.concatenate([dqm0, dqm1]), jnp.concatenate([dpf0, dpf1]),
         jnp.concatenate([dqf0, dqf1]), pad_lanes(loss_part), dhn, pad_lanes(dbif), dcw, dcb, dba, dbx, dlam,
         jnp.zeros((SMALL_ROWS - 19, LANES), F32)], axis=0)
    order = ("ml", "lru", "ffn0", "ffn1")
    s2 = (_sum_group("ml", s1_ml, recv_ml, pos) + _sum_group("lru", s1_lru, recv_lru, pos)
          + _sum_group("ffn0", s1_ffn0, recv_ffn0, pos) + _sum_group("ffn1", s1_ffn1, recv_ffn1, pos))
    specs = sum((GROUPS[g] for g in order), ())
    joined, small_all = _join_halves(specs, s2, small)
    red = dict(zip([n for n, _, _ in specs], joined))
    vsum = _sum_small(small_all)

    dgates = red["lru_gates"].reshape(2, LRU_BLOCKS, 64, LRU_BLOCK)
    cols = lambda a: lax.dynamic_slice_in_dim(a, chip * 256, 256, axis=1)
    grads = dict(
        ml_w_in=red["ml_w_in"], ml_w_out=red["ml_w_out"], lru_w_in=red["lru_w_in"], lru_w_gate_a=dgates[0],
        lru_w_gate_x=dgates[1], lru_w_out=red["lru_w_out"],
        ffn_w_gate=jnp.swapaxes(jnp.stack([red["ffn_g0"], red["ffn_g1"]]), 1, 2),
        ffn_w_up=jnp.swapaxes(jnp.stack([red["ffn_u0"], red["ffn_u1"]]), 1, 2),
        ffn_w_down=jnp.stack([red["ffn_down0"], red["ffn_down1"]]),
        ml_head_norm=vsum[ROW_HEAD_NORM:ROW_HEAD_NORM + 1], ml_b_if=vsum[ROW_B_IF:ROW_B_IF + 1, :2 * ML_HEADS],
        lru_conv_w=cols(vsum[ROW_LRU:ROW_LRU + 4]))
    for i, n in enumerate(NORMS):
        grads[n] = vsum[2 * i:2 * i + 2]
    for i, n in enumerate(LRU_VECTORS):
        grads[n] = cols(vsum[ROW_LRU + 4 + i:ROW_LRU + 5 + i])
    loss = vsum[ROW_LOSS, 0]
    grads = {n: grads[n].reshape(w[n].shape) for n in WEIGHTS}
    delta, new_m, new_v = _update(w, grads, m, v)
    return (loss, dx[None], *[grads[n] for n in WEIGHTS], *[delta[n] for n in WEIGHTS],
            *[new_m[n] for n in WEIGHTS], *[new_v[n] for n in WEIGHTS])
```

```python
import functools
import math
from typing import Callable, NamedTuple

import jax
import jax.numpy as jnp
from jax import lax
from jax.experimental import pallas as pl
from jax.experimental.pallas import tpu as pltpu

F32 = jnp.float32
BF16 = jnp.bfloat16
MESH = pl.DeviceIdType.MESH

D_MODEL = 1024
D_FF = 2816
ML_HEADS = 8
ML_DK = 64
ML_DV = 128
ML_QK = ML_HEADS * ML_DK
ML_V = ML_HEADS * ML_DV
ML_IN = 2 * ML_QK + 2 * ML_V + 2 * ML_HEADS
ML_IN_PAD = 3200
CHUNK = 64
GATE_CAP = 15.0
ML_M_INIT = -1e30
NEG_BIG = -1e30
LRU_BLOCKS = 4
LRU_BLOCK = 256
CONV_WIDTH = 4
LRU_C = 8.0
EPS = 1e-6
QK_SCALE = ML_DK ** -0.5

ADAM_LR = 0.001
ADAM_B1 = 0.9
ADAM_B2 = 0.999
ADAM_EPS = 1e-08
ADAM_WD = 0.01
ADAM_STEP = 10

N_CHIPS = 4
V7X_VMEM_LIMIT = 56 * 1024 * 1024

NT_DIMS = (((1,), (1,)), ((), ()))
TN_DIMS = (((0,), (0,)), ((), ()))


def _params(*semantics):
    return pltpu.CompilerParams(dimension_semantics=semantics, vmem_limit_bytes=V7X_VMEM_LIMIT)


def _sds(shape, dtype):
    return jax.ShapeDtypeStruct(shape, dtype)


def _full(shape):
    return pl.BlockSpec(shape, lambda *_: (0,) * len(shape))


def _rows(tm, n):
    return pl.BlockSpec((tm, n), lambda i: (i, 0))


def _sigmoid(x):
    return 1.0 / (1.0 + jnp.exp(-x))


def _log_sigmoid(x):
    return jnp.minimum(x, 0.0) - jnp.log1p(jnp.exp(-jnp.abs(x)))


def _rms(x):
    r = lax.rsqrt(jnp.mean(x * x, axis=-1, keepdims=True) + EPS)
    return x * r, r


def _rms_bwd(xhat, r, g, dy):
    dxh = dy * g
    return r * (dxh - xhat * jnp.mean(dxh * xhat, axis=-1, keepdims=True))


SUBLANES = 8


def _scan_rows(a, b, carry, reverse=False):
    n, w = a.shape
    groups = n // SUBLANES
    a3, b3 = a.reshape(groups, SUBLANES, w), b.reshape(groups, SUBLANES, w)
    sub = lax.broadcasted_iota(jnp.int32, a3.shape, 1)
    s = 1
    while s < SUBLANES:
        keep = sub < SUBLANES - s if reverse else sub >= s
        shift = SUBLANES - s if reverse else s
        b3 = b3 + a3 * jnp.where(keep, pltpu.roll(b3, shift, 1), 0.0)
        a3 = a3 * jnp.where(keep, pltpu.roll(a3, shift, 1), 1.0)
        s *= 2
    out = [None] * groups
    for g in (reversed(range(groups)) if reverse else range(groups)):
        out[g] = b3[g] + a3[g] * carry
        carry = out[g][0:1] if reverse else out[g][SUBLANES - 1:SUBLANES]
    return jnp.concatenate(out, axis=0)


def _cumsum_rows(x, reverse=False):
    n = x.shape[0]
    row = lax.broadcasted_iota(jnp.int32, x.shape, 0)
    s = 1
    while s < n:
        if reverse:
            x = x + jnp.where(row < n - s, pltpu.roll(x, n - s, 0), 0.0)
        else:
            x = x + jnp.where(row >= s, pltpu.roll(x, s, 0), 0.0)
        s *= 2
    return x


def _norm_matmul(h, g, w, n_bf16, name, side):
    t, d = h.shape
    n = w.shape[1]
    tm = min(512, t)

    def body(h_ref, g_ref, w_ref, z_ref, *o_refs):
        xhat, _ = _rms(h_ref[...])
        z = (xhat * g_ref[...]).astype(BF16)
        z_ref[...] = z
        out = jnp.dot(z, w_ref[...], preferred_element_type=F32)
        if n_bf16:
            o_refs[0][...] = out[:, :n_bf16].astype(BF16)
            o_refs[1][...] = out[:, n_bf16:]
        else:
            o_refs[0][...] = out

    if n_bf16:
        out_specs = [_rows(tm, d), _rows(tm, n_bf16), _rows(tm, n - n_bf16)]
        out_shape = [_sds((t, d), BF16), _sds((t, n_bf16), BF16), _sds((t, n - n_bf16), F32)]
    else:
        out_specs = [_rows(tm, d), _rows(tm, n)]
        out_shape = [_sds((t, d), BF16), _sds((t, n), F32)]
    return _side_call(body, side, name=name, steps=t // tm, in_specs=[_rows(tm, d), _full((1, d)), _full((d, n))],
                      out_specs=out_specs, out_shape=out_shape, scratch_shapes=[], args=(h, g, w))


def _matmul_postnorm(a, w, g, res, name, side):
    t = res.shape[0]
    k, d = w.shape
    tm = min(512, t)

    def body(a_ref, w_ref, g_ref, res_ref, y_ref, o_ref):
        y = jnp.dot(a_ref[...], w_ref[...], preferred_element_type=F32)
        y_ref[...] = y
        yhat, _ = _rms(y)
        o_ref[...] = res_ref[...] + yhat * g_ref[...]

    return _side_call(
        body, side, name=name, steps=t // tm,
        in_specs=[_rows(tm, k), _full((k, d)), _full((1, d)), _rows(tm, d)],
        out_specs=[_rows(tm, d), _rows(tm, d)],
        out_shape=[_sds((t, d), F32), _sds((t, d), F32)],
        scratch_shapes=[], args=(a, w, g, res))


class _Side(NamedTuple):
    inputs: list
    out_shape: list
    n_sems: int
    start: Callable
    finish: Callable


def _side_call(body, side, *, name, steps, in_specs, out_specs, out_shape, scratch_shapes, args):
    n_in, n_out, n_scr = len(in_specs), len(out_specs), len(scratch_shapes)
    if side is None:
        outs = pl.pallas_call(
            body, name=name, grid=(steps,), in_specs=in_specs, out_specs=out_specs, out_shape=out_shape,
            scratch_shapes=scratch_shapes, compiler_params=_params("arbitrary"))(*args)
        return list(outs), []
    s_in, s_out = len(side.inputs), len(side.out_shape)

    def carrying(*refs):
        ins, side_ins = refs[:n_in], refs[n_in:n_in + s_in]
        outs = refs[n_in + s_in:n_in + s_in + n_out]
        side_outs = refs[n_in + s_in + n_out:n_in + s_in + n_out + s_out]
        scratch = refs[n_in + s_in + n_out + s_out:]
        own, sems = scratch[:n_scr], scratch[n_scr:]

        @pl.when(pl.program_id(0) == 0)
        def _():
            side.start(side_ins, side_outs, *sems)

        body(*ins, *outs, *own)

        @pl.when(pl.program_id(0) == steps - 1)
        def _():
            side.finish(side_ins, side_outs, *sems)

    outs = pl.pallas_call(
        carrying, name=name, grid=(steps,), in_specs=list(in_specs) + [HBM_SPEC] * s_in,
        out_specs=list(out_specs) + [HBM_SPEC] * s_out, out_shape=list(out_shape) + list(side.out_shape),
        scratch_shapes=list(scratch_shapes) + [pltpu.SemaphoreType.DMA((side.n_sems,))] * 2,
        compiler_params=_params("arbitrary"))(*args, *side.inputs)
    return list(outs[:n_out]), list(outs[n_out:])


def _ffn_forward(h, g_pre, wg_t, wu_t, w_down, g_post, target, name, side):
    t, d = h.shape
    k = w_down.shape[0]
    tm = 256

    def body(*refs):
        if target is None:
            h_ref, gpre_ref, wg_ref, wu_ref, wd_ref, gpost_ref, z_ref, gu_ref, y_ref, o_ref = refs
        else:
            h_ref, gpre_ref, wg_ref, wu_ref, wd_ref, gpost_ref, t_ref, z_ref, gu_ref, y_ref, o_ref, l_ref = refs

            @pl.when(pl.program_id(0) == 0)
            def _():
                l_ref[...] = jnp.zeros_like(l_ref)

        hv = h_ref[...]
        xhat, _ = _rms(hv)
        z = (xhat * gpre_ref[...]).astype(BF16)
        z_ref[...] = z
        gate = lax.dot_general(z, wg_ref[...], NT_DIMS, preferred_element_type=F32).astype(BF16)
        up = lax.dot_general(z, wu_ref[...], NT_DIMS, preferred_element_type=F32).astype(BF16)
        gu_ref[:, :k] = gate
        gu_ref[:, k:] = up
        gate = gate.astype(F32)
        act = (gate * _sigmoid(gate) * up.astype(F32)).astype(BF16)
        y = jnp.dot(act, wd_ref[...], preferred_element_type=F32)
        y_ref[...] = y
        yhat, _ = _rms(y)
        out = hv + yhat * gpost_ref[...]
        if target is None:
            o_ref[...] = out
        else:
            err = out - t_ref[...]
            o_ref[...] = err * (1.0 / d)
            part = jnp.sum(jnp.sum(err * err, axis=1, keepdims=True), axis=0, keepdims=True) * (0.5 / d)
            l_ref[...] += jnp.broadcast_to(part, l_ref.shape)

    in_specs = [_rows(tm, d), _full((1, d)), _full((k, d)), _full((k, d)), _full((k, d)), _full((1, d))]
    out_specs = [_rows(tm, d), _rows(tm, 2 * k), _rows(tm, d), _rows(tm, d)]
    out_shape = [_sds((t, d), BF16), _sds((t, 2 * k), BF16), _sds((t, d), F32), _sds((t, d), F32)]
    args = (h, g_pre, wg_t, wu_t, w_down, g_post)
    if target is not None:
        in_specs, args = in_specs + [_rows(tm, d)], args + (target,)
        out_specs, out_shape = out_specs + [_full((1, 128))], out_shape + [_sds((1, 128), F32)]
    return _side_call(body, side, name=name, steps=t // tm, in_specs=in_specs, out_specs=out_specs,
                      out_shape=out_shape, scratch_shapes=[], args=args)


def _bwd_out(dout, y, g, w, name, side):
    t, d = dout.shape
    k = w.shape[0]
    tm = min(512, t)

    def body(dout_ref, y_ref, g_ref, w_ref, dy_ref, da_ref, dg_ref):
        @pl.when(pl.program_id(0) == 0)
        def _():
            dg_ref[...] = jnp.zeros_like(dg_ref)

        do = dout_ref[...]
        yhat, r = _rms(y_ref[...])
        dg_ref[...] += jnp.sum(do * yhat, axis=0, keepdims=True)
        dy = _rms_bwd(yhat, r, g_ref[...], do).astype(BF16)
        dy_ref[...] = dy
        da_ref[...] = lax.dot_general(dy, w_ref[...], NT_DIMS, preferred_element_type=F32)

    return _side_call(
        body, side, name=name, steps=t // tm,
        in_specs=[_rows(tm, d), _rows(tm, d), _full((1, d)), _full((k, d))],
        out_specs=[_rows(tm, d), _rows(tm, k), _full((1, d))],
        out_shape=[_sds((t, d), BF16), _sds((t, k), F32), _sds((1, d), F32)],
        scratch_shapes=[], args=(dout, y, g, w))


def _ffn_backward(dout, y, g_post, w_down, gu, wg_t, wu_t, h, g_pre, name, side):
    t, d = dout.shape
    k = w_down.shape[0]
    tm = 256

    def body(dout_ref, y_ref, gpost_ref, wd_ref, gu_ref, wg_ref, wu_ref, h_ref, gpre_ref,
             dy_ref, dgu_ref, act_ref, dh_ref, dgpost_ref, dgpre_ref):
        @pl.when(pl.program_id(0) == 0)
        def _():
            dgpost_ref[...] = jnp.zeros_like(dgpost_ref)
            dgpre_ref[...] = jnp.zeros_like(dgpre_ref)

        do = dout_ref[...]
        yhat, r = _rms(y_ref[...])
        dgpost_ref[...] += jnp.sum(do * yhat, axis=0, keepdims=True)
        dy = _rms_bwd(yhat, r, gpost_ref[...], do).astype(BF16)
        dy_ref[...] = dy
        da = lax.dot_general(dy, wd_ref[...], NT_DIMS, preferred_element_type=F32)
        gate = gu_ref[:, :k].astype(F32)
        up = gu_ref[:, k:].astype(F32)
        sg = _sigmoid(gate)
        silu = gate * sg
        act_ref[...] = (silu * up).astype(BF16)
        dgate = (da * up * (sg * (1.0 + gate * (1.0 - sg)))).astype(BF16)
        dup = (da * silu).astype(BF16)
        dgu_ref[:, :k] = dgate
        dgu_ref[:, k:] = dup
        dz = (jnp.dot(dgate, wg_ref[...], preferred_element_type=F32)
              + jnp.dot(dup, wu_ref[...], preferred_element_type=F32))
        hhat, r2 = _rms(h_ref[...])
        dgpre_ref[...] += jnp.sum(dz * hhat, axis=0, keepdims=True)
        dh_ref[...] = do + _rms_bwd(hhat, r2, gpre_ref[...], dz)

    vec = _full((1, d))
    wspec = _full((k, d))
    return _side_call(
        body, side, name=name, steps=t // tm,
        in_specs=[_rows(tm, d), _rows(tm, d), vec, wspec, _rows(tm, 2 * k), wspec, wspec, _rows(tm, d), vec],
        out_specs=[_rows(tm, d), _rows(tm, 2 * k), _rows(tm, k), _rows(tm, d), vec, vec],
        out_shape=[_sds((t, d), BF16), _sds((t, 2 * k), BF16), _sds((t, k), BF16), _sds((t, d), F32),
                   _sds((1, d), F32), _sds((1, d), F32)],
        scratch_shapes=[], args=(dout, y, g_post, w_down, gu, wg_t, wu_t, h, g_pre))


def _bwd_in(dp, ws, transposed, h, g, dout, name, side):
    t, d = h.shape
    n = dp.shape[1]
    tm = min(512, t)
    widths = [w.shape[0] if transposed else w.shape[1] for w in ws]

    def body(dp_ref, *refs):
        w_refs = refs[:len(ws)]
        h_ref, g_ref, dout_ref, dh_ref, dg_ref = refs[len(ws):]

        @pl.when(pl.program_id(0) == 0)
        def _():
            dg_ref[...] = jnp.zeros_like(dg_ref)

        dz, at = None, 0
        for w_ref, width in zip(w_refs, widths):
            block = dp_ref[:, at:at + width]
            if transposed:
                part = jnp.dot(block, w_ref[...], preferred_element_type=F32)
            else:
                part = lax.dot_general(block, w_ref[...], NT_DIMS, preferred_element_type=F32)
            dz = part if dz is None else dz + part
            at += width
        hhat, r = _rms(h_ref[...])
        dg_ref[...] += jnp.sum(dz * hhat, axis=0, keepdims=True)
        dh_ref[...] = dout_ref[...] + _rms_bwd(hhat, r, g_ref[...], dz)

    return _side_call(
        body, side, name=name, steps=t // tm,
        in_specs=[_rows(tm, n)] + [_full(w.shape) for w in ws] + [_rows(tm, d), _full((1, d)), _rows(tm, d)],
        out_specs=[_rows(tm, d), _full((1, d))],
        out_shape=[_sds((t, d), F32), _sds((1, d), F32)],
        scratch_shapes=[], args=(dp, *ws, h, g, dout))


def _matmul_tn(a, b, tile_a, tile, name, window=None):
    t, ka = a.shape
    nb = b.shape[1]
    col0 = 0
    if window is not None:
        col0, ka = window

    def body(a_ref, b_ref, o_ref):
        o_ref[...] = lax.dot_general(a_ref[...], b_ref[...], TN_DIMS, preferred_element_type=F32).astype(BF16)

    if tile_a:
        grid = (ka // tile,)
        in_specs = [pl.BlockSpec((t, tile), lambda i: (0, col0 // tile + i)), _full((t, nb))]
        out_specs = pl.BlockSpec((tile, nb), lambda i: (i, 0))
    else:
        grid = (nb // tile,)
        in_specs = [_full((t, ka)), pl.BlockSpec((t, tile), lambda i: (0, i))]
        out_specs = pl.BlockSpec((ka, tile), lambda i: (0, i))
    return pl.pallas_call(
        body, name=name, grid=grid, in_specs=in_specs, out_specs=out_specs,
        out_shape=_sds((ka, nb), BF16), compiler_params=_params("parallel"),
    )(a, b)


def _ml_gate_prep(gt, bif):
    th = jnp.tanh((gt + bif) / GATE_CAP)
    act = GATE_CAP * th
    cum = _cumsum_rows(_log_sigmoid(act))
    lane = lax.broadcasted_iota(jnp.int32, gt.shape, 1)
    x = jnp.where(lane < ML_HEADS, act, cum)
    return x, x.T, th, act


HEADS = range(ML_HEADS)


def _each(fn, *per_head):
    return [fn(*a) for a in zip(*per_head)]


def _ml_chunk_fwd(q, k, v, kt, x, xt, c_in, n_in, m_in):
    causal = (lax.broadcasted_iota(jnp.int32, (CHUNK, CHUNK), 0)
              >= lax.broadcasted_iota(jnp.int32, (CHUNK, CHUNK), 1))
    f = {}
    qh = f["qh"] = [q[:, ML_DK * h:ML_DK * (h + 1)] for h in HEADS]
    kh = f["kh"] = [k[:, ML_DK * h:ML_DK * (h + 1)] for h in HEADS]
    f["vh"] = [v[:, ML_DV * h:ML_DV * (h + 1)] for h in HEADS]
    kth = [kt[ML_DK * h:ML_DK * (h + 1), :] for h in HEADS]
    s = _each(lambda a, b: jnp.dot(a, b, preferred_element_type=F32) * QK_SCALE, qh, kth)
    f["qc"] = _each(lambda a, c: jnp.dot(a, c.astype(BF16), preferred_element_type=F32) * QK_SCALE, qh, c_in)
    bcol = f["bcol"] = [x[:, ML_HEADS + h:ML_HEADS + h + 1] for h in HEADS]
    licol = f["licol"] = [x[:, h:h + 1] for h in HEADS]
    brow = f["brow"] = [xt[ML_HEADS + h:ML_HEADS + h + 1, :] for h in HEADS]
    lirow = [xt[h:h + 1, :] for h in HEADS]
    dmat = _each(lambda bc, br, lr: jnp.where(causal, bc - br + lr, NEG_BIG), bcol, brow, lirow)
    inter = _each(lambda bc, m: bc + m, bcol, m_in)
    mt = _each(lambda d, i: jnp.maximum(jnp.max(d, axis=1, keepdims=True), i), dmat, inter)
    wt = f["wt"] = _each(lambda d, m: jnp.exp(d - m), dmat, mt)
    p = f["p"] = _each(lambda a, b: a * b, wt, s)
    winter = f["winter"] = _each(lambda i, m: jnp.exp(i - m), inter, mt)
    qf = f["qf"] = [a.astype(F32) for a in qh]
    qn = f["qn"] = _each(lambda a, n: jnp.sum(a * n, axis=1, keepdims=True) * QK_SCALE, qf, n_in)
    den = f["den"] = _each(lambda a, w, b: jnp.sum(a, axis=1, keepdims=True) + w * b, p, winter, qn)
    emt = f["emt"] = [jnp.exp(-m) for m in mt]
    f["nrm"] = _each(lambda d, e: jnp.maximum(jnp.abs(d), e), den, emt)
    gtot = [bc[CHUNK - 1:CHUNK, :] for bc in bcol]
    a_col = _each(lambda g, bc, lc: g - bc + lc, gtot, bcol, licol)
    a_row = _each(lambda g, br, lr: g - br + lr, gtot, brow, lirow)
    m_new = f["m_new"] = _each(lambda g, m, a: jnp.maximum(g + m, jnp.max(a, axis=1, keepdims=True)),
                               gtot, m_in, a_row)
    f["decay"] = _each(lambda g, m, mn: jnp.exp(g + m - mn), gtot, m_in, m_new)
    wkf = f["wkf"] = _each(lambda a, mn: jnp.exp(a - mn), a_col, m_new)
    f["kw"] = _each(lambda a, w: a.astype(F32) * w, kh, wkf)
    f["ktw"] = _each(lambda a, ar, mn: (a.astype(F32) * jnp.exp(ar - mn)).astype(BF16), kth, a_row, m_new)
    return f


ML_STEP_CHUNKS = 4
ML_STEP = CHUNK * ML_STEP_CHUNKS


def _ml_specs(steps, rev):
    def at(col):
        if rev:
            return lambda i: (steps - 1 - i, col)
        return lambda i: (i, col)

    return [pl.BlockSpec((ML_STEP, ML_QK), at(0)), pl.BlockSpec((ML_STEP, ML_QK), at(1)),
            pl.BlockSpec((ML_STEP, ML_V), at(1)), pl.BlockSpec((ML_STEP, ML_V), at(0)),
            pl.BlockSpec((ML_STEP, 128), at(ML_V // 128))]


def _mlstm_fwd(qkv, og, bif, hn, side):
    t = qkv.shape[0]
    nc = t // CHUNK
    steps = t // ML_STEP

    def body(q_ref, k_ref, v_ref, o_ref, gt_ref, bif_ref, hn_ref, y_ref, hs_ref, cst_ref, nst_ref, mst_ref,
             c_sc, n_sc, m_sc):
        @pl.when(pl.program_id(0) == 0)
        def _():
            c_sc[...] = jnp.zeros_like(c_sc)
            n_sc[...] = jnp.zeros_like(n_sc)
            m_sc[...] = jnp.full_like(m_sc, ML_M_INIT)

        c_all, n_all, m_all = c_sc[...], n_sc[...], m_sc[...]
        c_in = [c_all[ML_DK * h:ML_DK * (h + 1), :] for h in HEADS]
        n_in = [n_all[h:h + 1, :] for h in HEADS]
        m_in = [m_all[h:h + 1, 0:1] for h in HEADS]
        for sub in range(ML_STEP_CHUNKS):
            rs = slice(CHUNK * sub, CHUNK * (sub + 1))
            for h in HEADS:
                cst_ref[sub, ML_DK * h:ML_DK * (h + 1), :] = c_in[h]
                nst_ref[sub, h:h + 1, :] = n_in[h]
                mst_ref[sub, h:h + 1, :] = jnp.broadcast_to(m_in[h], (1, 128))
            x, xt, _, _ = _ml_gate_prep(gt_ref[rs, :], bif_ref[...])
            q, k, v = q_ref[rs, :], k_ref[rs, :], v_ref[rs, :]
            f = _ml_chunk_fwd(q, k, v, k.T, x, xt, c_in, n_in, m_in)
            num = _each(lambda p, vh, w, qc: jnp.dot(p.astype(BF16), vh, preferred_element_type=F32) + w * qc,
                        f["p"], f["vh"], f["winter"], f["qc"])
            hh = _each(lambda a, b: a / b, num, f["nrm"])
            hhat = [_rms(a)[0] for a in hh]
            c_in = _each(lambda d, c, kw, vh: d * c + jnp.dot(kw, vh, preferred_element_type=F32),
                         f["decay"], c_in, f["ktw"], f["vh"])
            n_in = _each(lambda d, n, kw: d * n + jnp.sum(kw, axis=0, keepdims=True), f["decay"], n_in, f["kw"])
            m_in = f["m_new"]
            for h in HEADS:
                vs = slice(ML_DV * h, ML_DV * (h + 1))
                hs_ref[rs, vs] = hh[h]
                y_ref[rs, vs] = (hhat[h] * hn_ref[:, vs] * _sigmoid(o_ref[rs, vs])).astype(BF16)
        for h in HEADS:
            c_sc[ML_DK * h:ML_DK * (h + 1), :] = c_in[h]
            n_sc[h:h + 1, :] = n_in[h]
            m_sc[h:h + 1, :] = jnp.broadcast_to(m_in[h], (1, 128))

    return _side_call(
        body, side, name="mlstm_fwd", steps=steps,
        in_specs=_ml_specs(steps, False) + [_full((1, 128)), _full((1, ML_V))],
        out_specs=[_rows(ML_STEP, ML_V), _rows(ML_STEP, ML_V),
                   pl.BlockSpec((ML_STEP_CHUNKS, ML_HEADS * ML_DK, ML_DV), lambda i: (i, 0, 0)),
                   pl.BlockSpec((ML_STEP_CHUNKS, ML_HEADS, ML_DK), lambda i: (i, 0, 0)),
                   pl.BlockSpec((ML_STEP_CHUNKS, ML_HEADS, 128), lambda i: (i, 0, 0))],
        out_shape=[_sds((t, ML_V), BF16), _sds((t, ML_V), F32),
                   _sds((nc, ML_HEADS * ML_DK, ML_DV), F32), _sds((nc, ML_HEADS, ML_DK), F32),
                   _sds((nc, ML_HEADS, 128), F32)],
        scratch_shapes=[pltpu.VMEM((ML_HEADS * ML_DK, ML_DV), F32), pltpu.VMEM((ML_HEADS, ML_DK), F32),
                        pltpu.VMEM((ML_HEADS, 128), F32)],
        args=(qkv, qkv, qkv, og, og, bif, hn))


def _mlstm_bwd(qkv, og, bif, hn, hs, cst, nst, mst, dy, side):
    t = qkv.shape[0]
    steps = t // ML_STEP

    def chunk(q_ref, k_ref, v_ref, o_ref, gt_ref, bif_ref, hn_ref, hs_ref, cst_ref, nst_ref, mst_ref, dy_ref,
              dp_ref, dhn_ref, dbif_ref, carried):
        x, xt, th, act = _ml_gate_prep(gt_ref[...], bif_ref[...])
        q, k, v = q_ref[...], k_ref[...], v_ref[...]
        qt, vt = q.T, v.T
        rows = lax.broadcasted_iota(jnp.int32, (CHUNK, CHUNK), 0)
        cols = lax.broadcasted_iota(jnp.int32, (CHUNK, CHUNK), 1)
        lane = lax.broadcasted_iota(jnp.int32, (CHUNK, 128), 1)
        row = lax.broadcasted_iota(jnp.int32, (CHUNK, 1), 0)
        as_row = lambda col: jnp.sum(jnp.where(rows == cols, col, 0.0), axis=0, keepdims=True)
        mm = lambda a, b: jnp.dot(a, b, preferred_element_type=F32)
        bf = lambda a: a.astype(BF16)
        ksl = [slice(ML_DK * h, ML_DK * (h + 1)) for h in HEADS]
        vsl = [slice(ML_DV * h, ML_DV * (h + 1)) for h in HEADS]
        c_in = [cst_ref[0, s, :] for s in ksl]
        n_in = [nst_ref[0, h:h + 1, :] for h in HEADS]
        m_in = [mst_ref[0, h:h + 1, 0:1] for h in HEADS]
        dcn, dcn_t, dnn = carried
        f = _ml_chunk_fwd(q, k, v, k.T, x, xt, c_in, n_in, m_in)
        qh, kh, vh, p, winter, decay = f["qh"], f["kh"], f["vh"], f["p"], f["winter"], f["decay"]
        qth = [qt[s, :] for s in ksl]
        vth = [vt[s, :] for s in vsl]
        c_t = [bf(c.T) for c in c_in]
        dmat_t = _each(lambda br, bc, lc: jnp.where(rows <= cols, br - bc + lc, NEG_BIG),
                       f["brow"], f["bcol"], f["licol"])
        inter_row = _each(lambda br, m: br + m, f["brow"], m_in)
        mt_row = _each(lambda d, i: jnp.maximum(jnp.max(d, axis=0, keepdims=True), i), dmat_t, inter_row)
        wt_t = _each(lambda d, m: jnp.exp(d - m), dmat_t, mt_row)
        p_t = _each(lambda w, a, b: w * (mm(a, b) * QK_SCALE), wt_t, kh, qth)
        winter_row = _each(lambda i, m: jnp.exp(i - m), inter_row, mt_row)
        hh = [hs_ref[:, s] for s in vsl]
        hn_h = [hn_ref[:, s] for s in vsl]
        sg = [_sigmoid(o_ref[:, s]) for s in vsl]
        dyh = [dy_ref[:, s] for s in vsl]
        norm = [_rms(a) for a in hh]
        hhat, r = [a for a, _ in norm], [b for _, b in norm]
        dyn = _each(lambda a, b: a * b, dyh, sg)
        do = _each(lambda d, hx, g, s: d * hx * g * s * (1.0 - s), dyh, hhat, hn_h, sg)
        dhn = _each(lambda a, b: jnp.sum(a * b, axis=0, keepdims=True), dyn, hhat)
        dh = _each(_rms_bwd, hhat, r, hn_h, dyn)
        inv = [1.0 / a for a in f["nrm"]]
        dnum = _each(lambda a, b: a * b, dh, inv)
        dnrm = _each(lambda a, b, c: -jnp.sum(a * b, axis=1, keepdims=True) * c, dh, hh, inv)
        dden = _each(lambda d, e, g: jnp.where(jnp.abs(d) > e, g * jnp.sign(d), 0.0), f["den"], f["emt"], dnrm)
        dnb = [bf(a) for a in dnum]
        dnt = [bf(a.T) for a in dnum]
        rmat = _each(lambda a, b, d: mm(a, b) + d, dnb, vth, dden)
        rmat_t = _each(lambda a, b, d: mm(a, b) + as_row(d), vh, dnt, dden)
        ds = _each(lambda w, a: bf(w * a), f["wt"], rmat)
        ds_t = _each(lambda w, a: bf(w * a), wt_t, rmat_t)
        dv = _each(lambda a, b: mm(bf(a), b), p_t, dnb)
        dqs = _each(lambda s, kk, w, d, ct, dd, n: mm(s, kk) + w * (mm(d, ct) + dd * n),
                    ds, kh, winter, dnb, c_t, dden, n_in)
        dk = _each(lambda s, a: mm(s, a) * QK_SCALE, ds_t, qh)
        dinter = _each(lambda qc, dn_, qn, dd, w: (jnp.sum(qc * dn_, axis=1, keepdims=True) + qn * dd) * w,
                       f["qc"], dnum, f["qn"], dden, winter)
        wq = _each(lambda w, a: w * a * QK_SCALE, winter, f["qf"])
        wq_t = _each(lambda w, a: bf(w * a.astype(F32) * QK_SCALE), winter_row, qth)
        dc_loc = _each(mm, wq_t, dnb)
        dct_loc = _each(lambda a, b: mm(a, bf(b)), dnt, wq)
        dn_loc = _each(lambda a, d: jnp.sum(a * d, axis=0, keepdims=True), wq, dden)
        cs_q = _each(lambda a, b: jnp.sum(a * b, axis=1, keepdims=True), p_t, rmat_t)
        db = _each(lambda a, b, di, cs: jnp.sum(a * b, axis=1, keepdims=True) + di - cs, p, rmat, dinter, cs_q)
        ddecay = _each(lambda dc_, c, dn_, n: jnp.sum(jnp.sum(dc_ * c, axis=1, keepdims=True), axis=0, keepdims=True)
                       + jnp.sum(dn_ * n, axis=1, keepdims=True), dcn, c_in, dnn, n_in)
        dkw = _each(lambda a, b, n: mm(a, bf(b)) + n, vh, dcn_t, dnn)
        dk = _each(lambda a, w, b: a + w * b, dk, f["wkf"], dkw)
        da = _each(lambda a, kk, w: jnp.sum(a * kk.astype(F32), axis=1, keepdims=True) * w, dkw, kh, f["wkf"])
        dv = _each(lambda a, kw, dc_: a + mm(bf(kw), bf(dc_)), dv, f["kw"], dcn)
        dgtot = _each(lambda a, dd, d: jnp.sum(a, axis=0, keepdims=True) + dd * d, da, ddecay, decay)
        db = _each(lambda a, b, g: a - b + jnp.where(row == CHUNK - 1, g, 0.0), db, da, dgtot)
        dli = _each(lambda a, b: a + b, cs_q, da)
        dc_new = _each(lambda d, a, b: d * a + b, decay, dcn, dc_loc)
        dct_new = _each(lambda d, a, b: d * a + b, decay, dcn_t, dct_loc)
        dn_new = _each(lambda d, a, b: d * a + b, decay, dnn, dn_loc)
        dx = jnp.zeros((CHUNK, 128), F32)
        for h in HEADS:
            dhn_ref[:, vsl[h]] += dhn[h]
            dp_ref[:, ksl[h]] = bf(dqs[h] * QK_SCALE)
            dp_ref[:, ML_QK + ML_DK * h:ML_QK + ML_DK * (h + 1)] = bf(dk[h])
            dp_ref[:, 2 * ML_QK + ML_DV * h:2 * ML_QK + ML_DV * (h + 1)] = bf(dv[h])
            dp_ref[:, 2 * ML_QK + ML_V + ML_DV * h:2 * ML_QK + ML_V + ML_DV * (h + 1)] = bf(do[h])
            dx = jnp.where(lane == h, dli[h], dx)
            dx = jnp.where(lane == ML_HEADS + h, db[h], dx)
        dlf = _cumsum_rows(dx, reverse=True)
        dact = jnp.where(lane < ML_HEADS, dx, dlf * _sigmoid(-act))
        dz = dact * (1.0 - th * th)
        dp_ref[:, 2 * ML_QK + 2 * ML_V:] = dz.astype(BF16)
        dbif_ref[...] += jnp.sum(dz, axis=0, keepdims=True)
        return dc_new, dct_new, dn_new

    def body(q_ref, k_ref, v_ref, o_ref, gt_ref, bif_ref, hn_ref, hs_ref, cst_ref, nst_ref, mst_ref, dy_ref,
             dp_ref, dhn_ref, dbif_ref, dc_sc, dct_sc, dn_sc):
        @pl.when(pl.program_id(0) == 0)
        def _():
            dc_sc[...] = jnp.zeros_like(dc_sc)
            dct_sc[...] = jnp.zeros_like(dct_sc)
            dn_sc[...] = jnp.zeros_like(dn_sc)
            dhn_ref[...] = jnp.zeros_like(dhn_ref)
            dbif_ref[...] = jnp.zeros_like(dbif_ref)

        dc_all, dct_all, dn_all = dc_sc[...], dct_sc[...], dn_sc[...]
        carried = ([dc_all[ML_DK * h:ML_DK * (h + 1), :] for h in HEADS],
                   [dct_all[ML_DV * h:ML_DV * (h + 1), :] for h in HEADS],
                   [dn_all[h:h + 1, :] for h in HEADS])
        for sub in reversed(range(ML_STEP_CHUNKS)):
            rs = pl.ds(CHUNK * sub, CHUNK)
            one = pl.ds(sub, 1)
            carried = chunk(q_ref.at[rs], k_ref.at[rs], v_ref.at[rs], o_ref.at[rs], gt_ref.at[rs], bif_ref, hn_ref,
                            hs_ref.at[rs], cst_ref.at[one], nst_ref.at[one], mst_ref.at[one], dy_ref.at[rs],
                            dp_ref.at[rs], dhn_ref, dbif_ref, carried)
        for h in HEADS:
            dc_sc[ML_DK * h:ML_DK * (h + 1), :] = carried[0][h]
            dct_sc[ML_DV * h:ML_DV * (h + 1), :] = carried[1][h]
            dn_sc[h:h + 1, :] = carried[2][h]

    rev = lambda i: (steps - 1 - i, 0)
    rev3 = lambda i: (steps - 1 - i, 0, 0)
    return _side_call(
        body, side, name="mlstm_bwd", steps=steps,
        in_specs=_ml_specs(steps, True) + [
            _full((1, 128)), _full((1, ML_V)), pl.BlockSpec((ML_STEP, ML_V), rev),
            pl.BlockSpec((ML_STEP_CHUNKS, ML_HEADS * ML_DK, ML_DV), rev3),
            pl.BlockSpec((ML_STEP_CHUNKS, ML_HEADS, ML_DK), rev3),
            pl.BlockSpec((ML_STEP_CHUNKS, ML_HEADS, 128), rev3), pl.BlockSpec((ML_STEP, ML_V), rev)],
        out_specs=[pl.BlockSpec((ML_STEP, ML_IN_PAD), rev), _full((1, ML_V)), _full((1, 128))],
        out_shape=[_sds((t, ML_IN_PAD), BF16), _sds((1, ML_V), F32), _sds((1, 128), F32)],
        scratch_shapes=[pltpu.VMEM((ML_HEADS * ML_DK, ML_DV), F32), pltpu.VMEM((ML_HEADS * ML_DV, ML_DK), F32),
                        pltpu.VMEM((ML_HEADS, ML_DK), F32)],
        args=(qkv, qkv, qkv, og, og, bif, hn, hs, cst, nst, mst, dy))


LRU_TM = 256
GELU_K = math.sqrt(2.0 / math.pi)
GELU_C = 0.044715


def _gelu(x):
    th = jnp.tanh(GELU_K * (x + GELU_C * x * x * x))
    return 0.5 * x * (1.0 + th), th


def _neg_expm1(x):
    series = -x * (1.0 + x * (0.5 + x * (1.0 / 6.0 + x * (1.0 / 24.0))))
    return jnp.where(x > -0.05, series, 1.0 - jnp.exp(x))


def _block_diag_dot(a, w_ref, dims):
    parts = [lax.dot_general(a[:, LRU_BLOCK * n:LRU_BLOCK * (n + 1)], w_ref[n], dims, preferred_element_type=F32)
             for n in range(LRU_BLOCKS)]
    return jnp.concatenate(parts, axis=1)


def _lru_gates(r, lam):
    ls = _log_sigmoid(lam)
    la = LRU_C * r * ls
    a = jnp.exp(la)
    em = _neg_expm1(2.0 * la)
    mult = jnp.sqrt(em)
    return ls, a, em, mult


def _lru_fwd(proj, cw, cb, wa, ba, wx, bx, lam, w_out, g_post, res, side):
    t = proj.shape[0]
    w = D_MODEL
    tm = min(LRU_TM, t)

    def body(gb_ref, up_ref, cw_ref, cb_ref, wa_ref, ba_ref, wx_ref, bx_ref, lam_ref, wo_ref, gpost_ref, res_ref,
             y_ref, u_ref, r_ref, i_ref, h_ref, ymix_ref, o_ref, tail_sc, hprev_sc):
        @pl.when(pl.program_id(0) == 0)
        def _():
            tail_sc[...] = jnp.zeros_like(tail_sc)
            hprev_sc[...] = jnp.zeros_like(hprev_sc)

        up = up_ref[...]
        ext = jnp.concatenate([tail_sc[...], up], axis=0)
        u = cb_ref[...] + cw_ref[CONV_WIDTH - 1:CONV_WIDTH, :] * up
        for s in range(1, CONV_WIDTH):
            u = u + cw_ref[CONV_WIDTH - 1 - s:CONV_WIDTH - s, :] * pltpu.roll(ext, s, 0)[8:8 + tm]
        tail_sc[...] = up[tm - 8:tm]
        ub = u.astype(BF16)
        r = _sigmoid(_block_diag_dot(ub, wa_ref, (((1,), (0,)), ((), ()))) + ba_ref[...])
        ig = _sigmoid(_block_diag_dot(ub, wx_ref, (((1,), (0,)), ((), ()))) + bx_ref[...])
        _, a, _, mult = _lru_gates(r, lam_ref[...])
        h = _scan_rows(a, mult * ig * u, hprev_sc[0:1, :])
        hprev_sc[0:1, :] = h[tm - 1:tm]
        u_ref[...] = u
        r_ref[...] = r
        i_ref[...] = ig
        h_ref[...] = h
        gel, _ = _gelu(gb_ref[...])
        y = (h * gel).astype(BF16)
        y_ref[...] = y
        ymix = jnp.dot(y, wo_ref[...], preferred_element_type=F32)
        ymix_ref[...] = ymix
        yhat, _ = _rms(ymix)
        o_ref[...] = res_ref[...] + yhat * gpost_ref[...]

    vec = _full((1, w))
    wspec = _full((LRU_BLOCKS, LRU_BLOCK, LRU_BLOCK))
    return _side_call(
        body, side, name="lru_fwd", steps=t // tm,
        in_specs=[pl.BlockSpec((tm, w), lambda i: (i, 0)), pl.BlockSpec((tm, w), lambda i: (i, 1)),
                  _full((CONV_WIDTH, w)), vec, wspec, vec, wspec, vec, vec, _full((w, w)), vec, _rows(tm, w)],
        out_specs=[_rows(tm, w)] * 7,
        out_shape=[_sds((t, w), BF16)] + [_sds((t, w), F32)] * 6,
        scratch_shapes=[pltpu.VMEM((8, w), F32), pltpu.VMEM((8, w), F32)],
        args=(proj, proj, cw, cb, wa, ba, wx, bx, lam, w_out, g_post, res))


def _lru_bwd(proj, cw, wa, wx, lam, u, r, ig, h, dy, side):
    t = proj.shape[0]
    w = D_MODEL
    tm = min(LRU_TM, t)
    nt = t // tm

    def body(gb_ref, up_ref, cw_ref, wa_ref, wx_ref, lam_ref, u_ref, r_ref, i_ref, h_ref, hp_ref, dy_ref,
             dp_ref, dcw_ref, dcb_ref, dwa_ref, dba_ref, dwx_ref, dbx_ref, dlam_ref, carry_sc, dutail_sc, dls_sc):
        step = pl.program_id(0)

        @pl.when(step == 0)
        def _():
            carry_sc[...] = jnp.zeros_like(carry_sc)
            dutail_sc[...] = jnp.zeros_like(dutail_sc)
            dls_sc[...] = jnp.zeros_like(dls_sc)
            for ref in (dcw_ref, dcb_ref, dwa_ref, dba_ref, dwx_ref, dbx_ref):
                ref[...] = jnp.zeros_like(ref)

        row = lax.broadcasted_iota(jnp.int32, (tm, w), 0)
        u_t, r_t, i_t, h_t = u_ref[...], r_ref[...], i_ref[...], h_ref[...]
        ls, a, em, mult = _lru_gates(r_t, lam_ref[...])
        gb = gb_ref[...]
        gel, th = _gelu(gb)
        dyv = dy_ref[...]
        dgb = dyv * h_t * (0.5 * (1.0 + th) + 0.5 * gb * (1.0 - th * th) * GELU_K * (1.0 + 3.0 * GELU_C * gb * gb))
        a_next = jnp.where(row < tm - 1, pltpu.roll(a, tm - 1, 0), 1.0)
        g = _scan_rows(a_next, dyv * gel, carry_sc[0:1, :], reverse=True)
        carry_sc[0:1, :] = a[0:1] * g[0:1]
        has_prev = jnp.where(step == nt - 1, 0.0, 1.0)
        h_prev = jnp.where(row >= 1, pltpu.roll(h_t, 1, 0), hp_ref[7:8, :] * has_prev)
        dmult = g * i_t * u_t
        dig = g * mult * u_t
        du = g * mult * i_t
        dla = g * h_prev * a - dmult * (1.0 - em) / mult
        dls_sc[0:1, :] += jnp.sum(dla * r_t, axis=0, keepdims=True) * LRU_C
        dpa = dla * (LRU_C * ls) * r_t * (1.0 - r_t)
        dpx = dig * i_t * (1.0 - i_t)
        dba_ref[...] += jnp.sum(dpa, axis=0, keepdims=True)
        dbx_ref[...] += jnp.sum(dpx, axis=0, keepdims=True)
        ub = u_t.astype(BF16)
        dpab = dpa.astype(BF16)
        dpxb = dpx.astype(BF16)
        for n in range(LRU_BLOCKS):
            cs = slice(LRU_BLOCK * n, LRU_BLOCK * (n + 1))
            dwa_ref[n] += lax.dot_general(ub[:, cs], dpab[:, cs], TN_DIMS, preferred_element_type=F32)
            dwx_ref[n] += lax.dot_general(ub[:, cs], dpxb[:, cs], TN_DIMS, preferred_element_type=F32)
        du = du + _block_diag_dot(dpab, wa_ref, NT_DIMS) + _block_diag_dot(dpxb, wx_ref, NT_DIMS)
        dcb_ref[...] += jnp.sum(du, axis=0, keepdims=True)
        ext = jnp.concatenate([du, dutail_sc[...]], axis=0)
        up = up_ref[...]
        dup = cw_ref[CONV_WIDTH - 1:CONV_WIDTH, :] * du
        dcw_ref[CONV_WIDTH - 1:CONV_WIDTH, :] += jnp.sum(up * du, axis=0, keepdims=True)
        for s in range(1, CONV_WIDTH):
            du_s = pltpu.roll(ext, tm + 8 - s, 0)[0:tm]
            dup = dup + cw_ref[CONV_WIDTH - 1 - s:CONV_WIDTH - s, :] * du_s
            dcw_ref[CONV_WIDTH - 1 - s:CONV_WIDTH - s, :] += jnp.sum(up * du_s, axis=0, keepdims=True)
        dutail_sc[...] = du[0:8]
        dp_ref[:, :w] = dgb.astype(BF16)
        dp_ref[:, w:] = dup.astype(BF16)

        @pl.when(step == nt - 1)
        def _():
            dlam_ref[...] = dls_sc[0:1, :] * _sigmoid(-lam_ref[...])

    rev = lambda col: (lambda i: (nt - 1 - i, col))
    vec = _full((1, w))
    wspec = _full((LRU_BLOCKS, LRU_BLOCK, LRU_BLOCK))
    tile = pl.BlockSpec((tm, w), rev(0))
    prev8 = pl.BlockSpec((8, w), lambda i: (jnp.maximum((nt - 1 - i) * (tm // 8) - 1, 0), 0))
    return _side_call(
        body, side, name="lru_bwd", steps=nt,
        in_specs=[tile, pl.BlockSpec((tm, w), rev(1)), _full((CONV_WIDTH, w)), wspec, wspec, vec,
                  tile, tile, tile, tile, prev8, tile],
        out_specs=[pl.BlockSpec((tm, 2 * w), rev(0)), _full((CONV_WIDTH, w)), vec, wspec, vec, wspec, vec, vec],
        out_shape=[_sds((t, 2 * w), BF16), _sds((CONV_WIDTH, w), F32), _sds((1, w), F32),
                   _sds((LRU_BLOCKS, LRU_BLOCK, LRU_BLOCK), F32), _sds((1, w), F32),
                   _sds((LRU_BLOCKS, LRU_BLOCK, LRU_BLOCK), F32), _sds((1, w), F32), _sds((1, w), F32)],
        scratch_shapes=[pltpu.VMEM((8, w), F32), pltpu.VMEM((8, w), F32), pltpu.VMEM((8, w), F32)],
        args=(proj, proj, cw, wa, wx, lam, u, r, ig, h, h, dy))


LANES = 1024
HALF_FFN = D_FF // N_CHIPS
GROUPS = {
    "ml": (("ml_w_in", 1024, ML_IN // N_CHIPS), ("ml_w_out", 256, 1024)),
    "lru": (("lru_w_in", 1024, 512),
            ("lru_gates", 2 * LRU_BLOCKS * 64, LRU_BLOCK),
            ("lru_w_out", 256, 1024)),
    "ffn0": (("ffn_g0", HALF_FFN, 1024), ("ffn_u0", HALF_FFN, 1024), ("ffn_down0", HALF_FFN, 1024)),
    "ffn1": (("ffn_g1", HALF_FFN, 1024), ("ffn_u1", HALF_FFN, 1024), ("ffn_down1", HALF_FFN, 1024)),
}

SMALL_ROWS = 24
ROW_LOSS, ROW_HEAD_NORM, ROW_B_IF, ROW_LRU = 8, 9, 10, 11


def _row_tile(rows, cols, itemsize, budget=3 << 19):
    best = 16
    for t in range(16, rows + 1, 16):
        if rows % t == 0 and t * cols * itemsize <= budget:
            best = t
    return best


def _chip_peers():
    x, y, c = lax.axis_index("x"), lax.axis_index("y"), lax.axis_index("c")
    return x, y, c, [(1 - x, y), (x, 1 - y), (1 - x, 1 - y)]


HBM_SPEC = pl.BlockSpec(memory_space=pltpu.HBM)


def _remote(src, dst, send_sems, recv_sems, k, to):
    return pltpu.make_async_remote_copy(src_ref=src, dst_ref=dst, send_sem=send_sems.at[k], recv_sem=recv_sems.at[k],
                                        device_id=to, device_id_type=MESH)


GATHER_SEMS = 7


def _gather_copies(kind, specs, part_refs, out_refs, send_sems, recv_sems):
    x, y, c, chips = _chip_peers()
    me = 2 * x + y
    sib = (x, y, 1 - c)
    copy = functools.partial(_remote, send_sems=send_sems, recv_sems=recv_sems)
    out = []
    for p, (_, rows, _) in enumerate(specs):
        mine = pl.ds(c * (rows // 2), rows // 2)
        theirs = pl.ds((1 - c) * (rows // 2), rows // 2)
        base = GATHER_SEMS * p
        for j, (cx, cy) in enumerate(chips):
            land = out_refs[p].at[2 * cx + cy, mine]
            other = out_refs[p].at[2 * cx + cy, theirs]
            if kind == "first":
                out.append(copy(part_refs[p].at[mine], out_refs[p].at[me, mine], k=base + j, to=(cx, cy, c)))
            elif kind == "landed":
                out.append(copy(land, land, k=base + j, to=sib))
            elif kind == "forward":
                out.append(copy(land, land, k=base + 3 + j, to=sib))
            else:
                out.append(copy(other, other, k=base + 3 + j, to=sib))
        if kind in ("first", "arriving"):
            out.append(copy(part_refs[p], out_refs[p].at[me], k=base + 6, to=sib))
    return out


def _gather_start(specs, *refs):
    for cp in _gather_copies("first", specs, *refs):
        cp.start()


def _gather_finish(specs, *refs):
    forwards = _gather_copies("forward", specs, *refs)
    for land, fwd in zip(_gather_copies("landed", specs, *refs), forwards):
        land.wait_recv()
        fwd.start()
    for cp in _gather_copies("arriving", specs, *refs):
        cp.wait_recv()
    for cp in _gather_copies("first", specs, *refs) + forwards:
        cp.wait_send()


def _gather_shapes(specs):
    return [_sds((N_CHIPS, rows, cols), BF16) for _, rows, cols in specs]


def _gather_side(specs, parts):
    return _Side(inputs=list(parts), out_shape=_gather_shapes(specs), n_sems=GATHER_SEMS * len(specs),
                 start=functools.partial(_gather_start, specs), finish=functools.partial(_gather_finish, specs))


def _gather_weights(specs, parts, small):
    n = len(specs)

    def body(*refs):
        part_refs, small_ref = refs[:n], refs[n]
        out_refs, outs_ref = refs[n + 1:2 * n + 1], refs[2 * n + 1]
        send_sems, recv_sems, small_send, small_recv, loc_sem = refs[2 * n + 2:]
        x, y, c, chips = _chip_peers()
        me = 2 * x + y
        local = pltpu.make_async_copy(small_ref, outs_ref.at[me], loc_sem.at[0])
        local.start()
        _gather_start(specs, part_refs, out_refs, send_sems, recv_sems)
        sent = [_remote(small_ref, outs_ref.at[me], small_send, small_recv, j, (cx, cy, c))
                for j, (cx, cy) in enumerate(chips)]
        for cp in sent:
            cp.start()
        _gather_finish(specs, part_refs, out_refs, send_sems, recv_sems)
        for j, (cx, cy) in enumerate(chips):
            _remote(small_ref, outs_ref.at[2 * cx + cy], small_send, small_recv, j, (cx, cy, c)).wait_recv()
        for cp in sent:
            cp.wait_send()
        local.wait()

    dma = pltpu.SemaphoreType.DMA
    return pl.pallas_call(
        body, name="gather_weights",
        in_specs=[HBM_SPEC] * (n + 1), out_specs=[HBM_SPEC] * (n + 1),
        out_shape=_gather_shapes(specs) + [_sds((N_CHIPS,) + small.shape, small.dtype)],
        scratch_shapes=[dma((GATHER_SEMS * n,)), dma((GATHER_SEMS * n,)), dma((3,)), dma((3,)), dma((1,))],
    )(*parts, small)


def _exchange_copies(specs, g_refs, out_refs, send_sems, recv_sems):
    x, y, c, _ = _chip_peers()
    return [_remote(g_refs[p].at[:, pl.ds((1 - c) * (rows // 2), rows // 2)], out_refs[p], send_sems, recv_sems, p,
                    (x, y, 1 - c)) for p, (_, rows, _) in enumerate(specs)]


def _exchange_start(specs, *refs):
    for cp in _exchange_copies(specs, *refs):
        cp.start()


def _exchange_finish(specs, *refs):
    for cp in _exchange_copies(specs, *refs):
        cp.wait()


def _exchange_shapes(specs):
    return [_sds((N_CHIPS, rows // 2, cols), BF16) for _, rows, cols in specs]


def _exchange_side(specs, gparts):
    return _Side(inputs=list(gparts), out_shape=_exchange_shapes(specs), n_sems=len(specs),
                 start=functools.partial(_exchange_start, specs), finish=functools.partial(_exchange_finish, specs))


def _exchange_halves(specs, gparts, name):
    n = len(specs)

    def body(*refs):
        _exchange_start(specs, refs[:n], refs[n:2 * n], *refs[2 * n:])
        _exchange_finish(specs, refs[:n], refs[n:2 * n], *refs[2 * n:])

    return pl.pallas_call(
        body, name=name, in_specs=[HBM_SPEC] * n, out_specs=[HBM_SPEC] * n, out_shape=_exchange_shapes(specs),
        scratch_shapes=[pltpu.SemaphoreType.DMA((n,)), pltpu.SemaphoreType.DMA((n,))],
    )(*gparts)


def _add_halves(gs, recvs, pos, name):
    n = len(gs)
    _, half, cols = recvs[0].shape
    tr = _row_tile(half, cols, 2)
    tiles = half // tr

    def body(pos_ref, *refs):
        for a_ref, b_ref, o_ref in zip(refs[:n], refs[n:2 * n], refs[2 * n:]):
            o_ref[...] = (a_ref[...].astype(F32) + b_ref[...].astype(F32)).astype(BF16)

    spec = pl.BlockSpec((1, tr, cols), lambda k, i, pos_ref: (k, i, 0))
    mine = pl.BlockSpec((1, tr, cols), lambda k, i, pos_ref: (k, pos_ref[0] * tiles + i, 0))
    return pl.pallas_call(
        body, name=name,
        grid_spec=pltpu.PrefetchScalarGridSpec(
            num_scalar_prefetch=1, grid=(N_CHIPS, tiles), in_specs=[mine] * n + [spec] * n, out_specs=[spec] * n),
        out_shape=[_sds((N_CHIPS, half, cols), BF16)] * n,
        compiler_params=_params("parallel", "parallel"),
    )(pos, *gs, *recvs)


def _scatter_copies(n, s1_refs, recv_refs, send_sems, recv_sems):
    x, y, c, chips = _chip_peers()
    return [_remote(s1_refs[p].at[2 * cx + cy], recv_refs[p].at[j], send_sems, recv_sems, 3 * p + j, (cx, cy, c))
            for p in range(n) for j, (cx, cy) in enumerate(chips)]


def _scatter_start(n, s1_refs, recv_refs, send_sems, recv_sems):
    for cp in _scatter_copies(n, s1_refs, recv_refs, send_sems, recv_sems):
        cp.start()


def _scatter_finish(n, s1_refs, recv_refs, send_sems, recv_sems):
    for cp in _scatter_copies(n, s1_refs, recv_refs, send_sems, recv_sems):
        cp.wait()


def _scatter_shapes(s1):
    return [_sds((3,) + a.shape[1:], a.dtype) for a in s1]


def _scatter_side(s1):
    n = len(s1)
    return _Side(inputs=list(s1), out_shape=_scatter_shapes(s1), n_sems=3 * n,
                 start=functools.partial(_scatter_start, n), finish=functools.partial(_scatter_finish, n))


def _sum_chips(s1s, recvs, pos, name):
    n = len(s1s)
    _, half, cols = recvs[0].shape
    tr = _row_tile(half, cols, 4)
    tiles = half // tr

    def body(pos_ref, *refs):
        for a_ref, b_ref, o_ref in zip(refs[:n], refs[n:2 * n], refs[2 * n:]):
            acc = a_ref[0].astype(F32)
            for j in range(3):
                acc = acc + b_ref[j].astype(F32)
            o_ref[...] = acc

    return pl.pallas_call(
        body, name=name,
        grid_spec=pltpu.PrefetchScalarGridSpec(
            num_scalar_prefetch=1, grid=(tiles,),
            in_specs=[pl.BlockSpec((1, tr, cols), lambda i, pos_ref: (pos_ref[1], i, 0))] * n
            + [pl.BlockSpec((3, tr, cols), lambda i, pos_ref: (0, i, 0))] * n,
            out_specs=[pl.BlockSpec((tr, cols), lambda i, pos_ref: (pos_ref[0] * tiles + i, 0))] * n),
        out_shape=[_sds((2 * half, cols), F32)] * n,
        compiler_params=_params("parallel"),
    )(pos, *s1s, *recvs)


def _sum_small(small_all):
    def body(a_ref, o_ref):
        acc = a_ref[0]
        for d in range(1, 8):
            acc = acc + a_ref[d]
        o_ref[...] = acc

    return pl.pallas_call(body, name="sum_small", out_shape=_sds(small_all.shape[1:], F32))(small_all)


def _join_halves(specs, s2, small):
    n = len(specs)
    flips = [(fx, fy, fc) for fx in (0, 1) for fy in (0, 1) for fc in (0, 1)][1:]

    def body(*refs):
        small_ref, all_ref = refs[n], refs[2 * n + 1]
        buf_refs = refs[n + 1:2 * n + 1]
        send_sems, recv_sems, small_send, small_recv, loc_sem = refs[2 * n + 2:]
        x, y, c, _ = _chip_peers()
        my_slot = all_ref.at[4 * x + 2 * y + c]
        local = pltpu.make_async_copy(small_ref, my_slot, loc_sem.at[0])
        local.start()
        sent = []
        for p, (_, rows, _) in enumerate(specs):
            mine = buf_refs[p].at[pl.ds(c * (rows // 2), rows // 2)]
            sent.append(_remote(mine, mine, send_sems, recv_sems, p, (x, y, 1 - c)))
        peers = [(1 - x if fx else x, 1 - y if fy else y, 1 - c if fc else c) for fx, fy, fc in flips]
        sent += [_remote(small_ref, my_slot, small_send, small_recv, i, p) for i, p in enumerate(peers)]
        for cp in sent:
            cp.start()
        for p, (_, rows, _) in enumerate(specs):
            theirs = buf_refs[p].at[pl.ds((1 - c) * (rows // 2), rows // 2)]
            _remote(theirs, theirs, send_sems, recv_sems, p, (x, y, 1 - c)).wait_recv()
        for i, (px, py, pc) in enumerate(peers):
            _remote(small_ref, all_ref.at[4 * px + 2 * py + pc], small_send, small_recv, i, peers[i]).wait_recv()
        for cp in sent:
            cp.wait_send()
        local.wait()

    dma = pltpu.SemaphoreType.DMA
    *joined, small_all = pl.pallas_call(
        body, name="join_halves", in_specs=[HBM_SPEC] * (n + 1), out_specs=[HBM_SPEC] * (n + 1),
        out_shape=[_sds(a.shape, a.dtype) for a in s2] + [_sds((8,) + small.shape, small.dtype)],
        input_output_aliases={p: p for p in range(n)},
        scratch_shapes=[dma((n,)), dma((n,)), dma((7,)), dma((7,)), dma((1,))],
    )(*s2, small)
    return joined, small_all


def _adamw(ws, gs, ms, vs, name):
    n = len(ws)
    w = ws[0]
    rows, cols = w.shape[0], w.shape[-1]
    tm = rows
    for cand in (512 // n, 256 // n, 128, 64, 32, 16, 8) if w.ndim == 2 else (rows // 4,):
        if rows % cand == 0 and rows > cand and (cand % 8 == 0 or w.ndim == 3):
            tm = cand
            break

    def body(*refs):
        ins, outs = refs[:4 * n], refs[4 * n:]
        for i in range(n):
            w_ref, g_ref, m_ref, v_ref = ins[i], ins[n + i], ins[2 * n + i], ins[3 * n + i]
            gv = g_ref[...]
            nm = ADAM_B1 * m_ref[...] + (1.0 - ADAM_B1) * gv
            nv = ADAM_B2 * v_ref[...] + (1.0 - ADAM_B2) * (gv * gv)
            m_hat = nm / (1.0 - ADAM_B1 ** ADAM_STEP)
            v_hat = nv / (1.0 - ADAM_B2 ** ADAM_STEP)
            outs[i][...] = -ADAM_LR * (m_hat / (jnp.sqrt(v_hat) + ADAM_EPS) + ADAM_WD * w_ref[...])
            outs[n + i][...] = nm
            outs[2 * n + i][...] = nv

    spec = _rows(tm, cols) if w.ndim == 2 else pl.BlockSpec((tm, 1, cols), lambda i: (i, 0, 0))
    out = pl.pallas_call(
        body, name=name, grid=(rows // tm,), in_specs=[spec] * (4 * n), out_specs=[spec] * (3 * n),
        out_shape=[_sds(w.shape, F32)] * (3 * n), compiler_params=_params("parallel"),
    )(*ws, *gs, *ms, *vs)
    return out[:n], out[n:2 * n], out[2 * n:]


WEIGHTS = ("ml_w_in", "ml_b_if", "ml_head_norm", "ml_w_out", "lru_w_in", "lru_conv_w", "lru_conv_b", "lru_w_gate_a",
           "lru_b_gate_a", "lru_w_gate_x", "lru_b_gate_x", "lru_lambda", "lru_w_out", "norm_pre_mix", "norm_post_mix",
           "norm_pre_ffn", "norm_post_ffn", "ffn_w_gate", "ffn_w_up", "ffn_w_down")
LRU_VECTORS = ("lru_conv_b", "lru_b_gate_a", "lru_b_gate_x", "lru_lambda")
NORMS = ("norm_pre_mix", "norm_post_mix", "norm_pre_ffn", "norm_post_ffn")


def _by_cols(a):
    return jnp.concatenate([a[k] for k in range(N_CHIPS)], axis=-1)


def _by_chip(a, width):
    return jnp.stack([a[..., k * width:(k + 1) * width] for k in range(N_CHIPS)])


def _weight_shards(w):
    bf = lambda a: a.astype(BF16)
    shards = dict(
        ml_w_in=bf(w["ml_w_in"][0]), ml_w_out=bf(w["ml_w_out"][0]), lru_w_in=bf(w["lru_w_in"][0]),
        lru_gates=bf(jnp.concatenate([w["lru_w_gate_a"][0], w["lru_w_gate_x"][0]], axis=0)).reshape(-1, LRU_BLOCK),
        lru_w_out=bf(w["lru_w_out"][0]))
    gate_t, up_t = bf(jnp.swapaxes(w["ffn_w_gate"], 1, 2)), bf(jnp.swapaxes(w["ffn_w_up"], 1, 2))
    for layer in range(2):
        shards[f"ffn_g{layer}"] = gate_t[layer]
        shards[f"ffn_u{layer}"] = up_t[layer]
        shards[f"ffn_down{layer}"] = bf(w["ffn_w_down"][layer])
    return shards


def _ffn_weights(parts):
    return [a.reshape(D_FF, 1024) for a in parts]


def _ffn_grads(dgu, z, act, dy, layer):
    dws = [_matmul_tn(dgu, z, True, 256, f"ffn{layer}_dw_gate", window=(0, D_FF)),
           _matmul_tn(dgu, z, True, 256, f"ffn{layer}_dw_up", window=(D_FF, D_FF)),
           _matmul_tn(act, dy, True, 256, f"ffn{layer}_dw_down")]
    return [a.reshape(N_CHIPS, HALF_FFN, 1024) for a in dws]


def _by_shape(fn, group, xs, ys, pos, name):
    specs = GROUPS[group]
    out = [None] * len(specs)
    for shape in sorted({s[1:] for s in specs}):
        idx = [i for i, s in enumerate(specs) if s[1:] == shape]
        res = fn([xs[i] for i in idx], [ys[i] for i in idx], pos, f"{name}_{group}_{shape[0]}x{shape[1]}")
        for i, r in zip(idx, res):
            out[i] = r
    return out


def _sum_group(group, s1, recv, pos):
    return _by_shape(_sum_chips, group, s1, recv, pos, "sum_chips")


def _update(w, grads, m, v):
    views, backs, alike = {}, {}, {}
    for n in WEIGHTS:
        if n == "ml_w_in":
            views[n] = lambda a: jnp.transpose(a, (2, 0, 1))
            backs[n] = lambda a: jnp.transpose(a, (1, 2, 0))
        else:
            flip = (lambda a: jnp.swapaxes(a, 1, 2)) if n in ("ffn_w_gate", "ffn_w_up") else (lambda a: a)
            shape = flip(w[n]).shape
            views[n] = lambda a, flip=flip, shape=shape: flip(a).reshape(-1, shape[-1])
            backs[n] = lambda a, flip=flip, shape=shape: flip(a.reshape(shape))
        alike.setdefault(views[n](w[n]).shape, []).append(n)
    delta, new_m, new_v = {}, {}, {}
    for names in alike.values():
        of = lambda d: [views[n](d[n]) for n in names]
        ds, nms, nvs = _adamw(of(w), of(grads), of(m), of(v), "adamw_" + names[0])
        for n, d, nm, nv in zip(names, ds, nms, nvs):
            delta[n], new_m[n], new_v[n] = backs[n](d), backs[n](nm), backs[n](nv)
    return delta, new_m, new_v


def kernel(x, ml_w_in, ml_b_if, ml_head_norm, ml_w_out, lru_w_in, lru_conv_w, lru_conv_b, lru_w_gate_a, lru_b_gate_a, lru_w_gate_x, lru_b_gate_x, lru_lambda, lru_w_out, norm_pre_mix, norm_post_mix, norm_pre_ffn, norm_post_ffn, ffn_w_gate, ffn_w_up, ffn_w_down, loss_target, m_ml_w_in, m_ml_b_if, m_ml_head_norm, m_ml_w_out, m_lru_w_in, m_lru_conv_w, m_lru_conv_b, m_lru_w_gate_a, m_lru_b_gate_a, m_lru_w_gate_x, m_lru_b_gate_x, m_lru_lambda, m_lru_w_out, m_norm_pre_mix, m_norm_post_mix, m_norm_pre_ffn, m_norm_post_ffn, m_ffn_w_gate, m_ffn_w_up, m_ffn_w_down, v_ml_w_in, v_ml_b_if, v_ml_head_norm, v_ml_w_out, v_lru_w_in, v_lru_conv_w, v_lru_conv_b, v_lru_w_gate_a, v_lru_b_gate_a, v_lru_w_gate_x, v_lru_b_gate_x, v_lru_lambda, v_lru_w_out, v_norm_pre_mix, v_norm_post_mix, v_norm_pre_ffn, v_norm_post_ffn, v_ffn_w_gate, v_ffn_w_up, v_ffn_w_down):
    args = locals()
    w = {n: args[n] for n in WEIGHTS}
    m = {n: args["m_" + n] for n in WEIGHTS}
    v = {n: args["v_" + n] for n in WEIGHTS}
    xs, target = x[0], loss_target[0]
    mx, my, mc, _ = _chip_peers()
    chip = 2 * mx + my
    pos = jnp.stack([mc, chip])
    row = lambda a, i: a[i:i + 1]
    npm, nqm, npf, nqf = (w[n] for n in NORMS)
    shards = _weight_shards(w)
    of = lambda group: [shards[n] for n, _, _ in GROUPS[group]]
    bif = jnp.pad(w["ml_b_if"], ((0, 0), (0, 128 - 2 * ML_HEADS)))
    hn = w["ml_head_norm"]

    small = jnp.concatenate([w["lru_conv_w"][0]] + [w[n] for n in LRU_VECTORS], axis=0)
    ml_in_spec, ml_out_spec = GROUPS["ml"][:1], GROUPS["ml"][1:]
    ml_in_parts, smalls = _gather_weights(ml_in_spec, of("ml")[:1], small)
    ml_w_in = jnp.concatenate([ml_in_parts[k] for k in range(N_CHIPS)]
                              + [jnp.zeros((D_MODEL, ML_IN_PAD - ML_IN), BF16)], axis=1)
    vec = _by_cols(smalls)
    lru_spec, ffn1_spec = GROUPS["lru"], GROUPS["ffn1"]
    (z0, qkv, og), got = _norm_matmul(xs, row(npm, 0), ml_w_in, 2 * ML_QK + ML_V, "ml_in",
                                      _gather_side(ml_out_spec, of("ml")[1:]))
    ml_w_out = got[0].reshape(1024, 1024)
    (y0, hs, cst, nst, mst), got = _mlstm_fwd(qkv, og, bif, hn, _gather_side(GROUPS["ffn0"], of("ffn0")))
    wg0, wu0, w_down0 = _ffn_weights(got)
    (ymix0, h1), got = _matmul_postnorm(y0, ml_w_out, row(nqm, 0), xs, "ml_out",
                                        _gather_side(lru_spec[:1], of("lru")[:1]))
    lru_w_in = _by_cols(got[0])
    (zf0, gu0, yf0, h2), got = _ffn_forward(
        h1, row(npf, 0), wg0, wu0, w_down0, row(nqf, 0), None, "ffn0_fwd",
        _gather_side(lru_spec[1:] + ffn1_spec[:2], of("lru")[1:] + of("ffn1")[:2]))
    gates = got[0].reshape(N_CHIPS, 2, LRU_BLOCKS, 64, LRU_BLOCK).transpose(1, 2, 0, 3, 4)
    gates = gates.reshape(2, LRU_BLOCKS, LRU_BLOCK, LRU_BLOCK)
    lru_w_out = got[1].reshape(1024, 1024)
    ffn1_gu = got[2:4]
    (z1, proj1), _ = _norm_matmul(h2, row(npm, 1), lru_w_in, 0, "lru_in", None)
    (y1, u, r, ig, hl, ymix1, h3), got = _lru_fwd(
        proj1, vec[0:4], vec[4:5], gates[0], vec[5:6], gates[1], vec[6:7], vec[7:8], lru_w_out, row(nqm, 1), h2,
        _gather_side(ffn1_spec[2:], of("ffn1")[2:]))
    wg1, wu1, w_down1 = _ffn_weights(ffn1_gu + got)
    (zf1, gu1, yf1, dh4, loss_part), _ = _ffn_forward(
        h3, row(npf, 1), wg1, wu1, w_down1, row(nqf, 1), target, "ffn1_fwd", None)

    add = lambda group, gparts, recv: _by_shape(_add_halves, group, gparts, recv, pos, "add_halves")
    (dyf1, dgu1, act1, dh3, dqf1, dpf1), _ = _ffn_backward(
        dh4, yf1, row(nqf, 1), w_down1, gu1, wg1, wu1, h3, row(npf, 1), "ffn1_bwd", None)
    g_ffn1 = _ffn_grads(dgu1, zf1, act1, dyf1, 1)

    (dymix1, dy1, dqm1), recv = _bwd_out(dh3, ymix1, row(nqm, 1), lru_w_out, "lru_bwd_out",
                                         _exchange_side(GROUPS["ffn1"], g_ffn1))
    s1_ffn1 = add("ffn1", g_ffn1, recv)
    dw_lru_out = _matmul_tn(y1, dymix1, False, 256, "lru_dw_out")
    (dproj1, dcw, dcb, dwa, dba, dwx, dbx, dlam), recv_ffn1 = _lru_bwd(
        proj1, vec[0:4], gates[0], gates[1], vec[7:8], u, r, ig, hl, dy1, _scatter_side(s1_ffn1))
    dw_lru_in = _matmul_tn(z1, dproj1, False, 512, "lru_dw_in")
    dgates = jnp.stack([dwa, dwx]).astype(BF16).reshape(2, LRU_BLOCKS, N_CHIPS, 64, LRU_BLOCK)
    dgates = dgates.transpose(2, 0, 1, 3, 4).reshape(N_CHIPS, -1, LRU_BLOCK)
    g_lru = [_by_chip(dw_lru_in, 512), dgates, dw_lru_out.reshape(N_CHIPS, 256, 1024)]
    (dh2, dpm1), recv = _bwd_in(dproj1, [lru_w_in], False, h2, row(npm, 1), dh3, "lru_bwd_in",
                                _exchange_side(GROUPS["lru"], g_lru))
    s1_lru = add("lru", g_lru, recv)

    (dyf0, dgu0, act0, dh1, dqf0, dpf0), recv_lru = _ffn_backward(
        dh2, yf0, row(nqf, 0), w_down0, gu0, wg0, wu0, h1, row(npf, 0), "ffn0_bwd", _scatter_side(s1_lru))
    g_ffn0 = _ffn_grads(dgu0, zf0, act0, dyf0, 0)

    (dymix0, dy0, dqm0), recv = _bwd_out(dh1, ymix0, row(nqm, 0), ml_w_out, "ml_bwd_out",
                                         _exchange_side(GROUPS["ffn0"], g_ffn0))
    s1_ffn0 = add("ffn0", g_ffn0, recv)
    dw_ml_out = _matmul_tn(y0, dymix0, False, 256, "ml_dw_out")
    (dproj0, dhn, dbif), recv_ffn0 = _mlstm_bwd(qkv, og, bif, hn, hs, cst, nst, mst, dy0, _scatter_side(s1_ffn0))
    dw_ml_in = _matmul_tn(z0, dproj0, False, 640, "ml_dw_in")
    g_ml = [_by_chip(dw_ml_in[:, :ML_IN], ML_IN // N_CHIPS), dw_ml_out.reshape(N_CHIPS, 256, 1024)]
    s1_ml = add("ml", g_ml, _exchange_halves(GROUPS["ml"], g_ml, "exchange_halves_ml"))
    (dx, dpm0), recv_ml = _bwd_in(dproj0, [ml_w_in], False, xs, row(npm, 0), dh1, "ml_bwd_in", _scatter_side(s1_ml))

    pad_lanes = lambda a: jnp.pad(a, ((0, 0), (0, LANES - a.shape[1])))
    small = jnp.concatenate(
        [jnp.concatenate([dpm0, dpm1]), jnp.concatenate([dqm0, dqm1]), jnp.concatenate([dpf0, dpf1]),
         jnp.concatenate([dqf0, dqf1]), pad_lanes(loss_part), dhn, pad_lanes(dbif), dcw, dcb, dba, dbx, dlam,
         jnp.zeros((SMALL_ROWS - 19, LANES), F32)], axis=0)
    order = ("ml", "lru", "ffn0", "ffn1")
    s2 = (_sum_group("ml", s1_ml, recv_ml, pos) + _sum_group("lru", s1_lru, recv_lru, pos)
          + _sum_group("ffn0", s1_ffn0, recv_ffn0, pos) + _sum_group("ffn1", s1_ffn1, recv_ffn1, pos))
    specs = sum((GROUPS[g] for g in order), ())
    joined, small_all = _join_halves(specs, s2, small)
    red = dict(zip([n for n, _, _ in specs], joined))
    vsum = _sum_small(small_all)

    dgates = red["lru_gates"].reshape(2, LRU_BLOCKS, 64, LRU_BLOCK)
    cols = lambda a: lax.dynamic_slice_in_dim(a, chip * 256, 256, axis=1)
    grads = dict(
        ml_w_in=red["ml_w_in"], ml_w_out=red["ml_w_out"], lru_w_in=red["lru_w_in"], lru_w_gate_a=dgates[0],
        lru_w_gate_x=dgates[1], lru_w_out=red["lru_w_out"],
        ffn_w_gate=jnp.swapaxes(jnp.stack([red["ffn_g0"], red["ffn_g1"]]), 1, 2),
        ffn_w_up=jnp.swapaxes(jnp.stack([red["ffn_u0"], red["ffn_u1"]]), 1, 2),
        ffn_w_down=jnp.stack([red["ffn_down0"], red["ffn_down1"]]),
        ml_head_norm=vsum[ROW_HEAD_NORM:ROW_HEAD_NORM + 1], ml_b_if=vsum[ROW_B_IF:ROW_B_IF + 1, :2 * ML_HEADS],
        lru_conv_w=cols(vsum[ROW_LRU:ROW_LRU + 4]))
    for i, n in enumerate(NORMS):
        grads[n] = vsum[2 * i:2 * i + 2]
    for i, n in enumerate(LRU_VECTORS):
        grads[n] = cols(vsum[ROW_LRU + 4 + i:ROW_LRU + 5 + i])
    loss = vsum[ROW_LOSS, 0]
    grads = {n: grads[n].reshape(w[n].shape) for n in WEIGHTS}
    delta, new_m, new_v = _update(w, grads, m, v)
    return (loss, dx[None], *[grads[n] for n in WEIGHTS], *[delta[n] for n in WEIGHTS],
            *[new_m[n] for n in WEIGHTS], *[new_v[n] for n in WEIGHTS])
```
